```python
import math
import jax, jax.numpy as jnp
from jax import lax
import numpy as np

D_MODEL = 1024
BATCH = 8
SEQ = 4096
DEPTH = 1

D_CONV = D_MODEL
CONV_WIDTH = 3
N_HEADS = 16
HEAD_DIM = 64
D_ATTN = N_HEADS * HEAD_DIM
ATTN_PATTERNS = ((128, 1), (512, 4), (2048, 16))
QB = 128
IN_COLS = 4 * D_CONV + 4 * D_ATTN + 2 * D_MODEL
EPS = 1e-6

kernel_name = "hybrid_gated_shortconv_dilated_alibi_attn"


def rms_norm(x, g):
    xf = x.astype(jnp.float32)
    y = xf * lax.rsqrt(jnp.mean(xf * xf, axis=-1, keepdims=True) + EPS)
    return (y * g.astype(jnp.float32)).astype(x.dtype)


def alibi_slopes(n_heads):
    return jnp.exp2(-8.0 * jnp.arange(1, n_heads + 1, dtype=jnp.float32) / n_heads)


def causal_depthwise_conv(u, w):
    return lax.conv_general_dilated(
        u, w.astype(u.dtype)[:, None, :], window_strides=(1,),
        padding=[(CONV_WIDTH - 1, 0)],
        dimension_numbers=("NWC", "WIO", "NWC"),
        feature_group_count=u.shape[-1])


def dilated_window_attention(q, k, v, window, dilation, slopes):
    B, S, H, Dh = q.shape
    d = dilation
    win = window // dilation
    assert win == QB
    L = S // d
    nb = -(-L // QB)
    Lp = nb * QB

    def to_sub(t):
        t = t.reshape(B, L, d, H, Dh)
        return jnp.pad(t, ((0, 0), (0, Lp - L), (0, 0), (0, 0), (0, 0)))

    qs = (to_sub(q) * (Dh ** -0.5)).reshape(B, nb, QB, d, H, Dh)

    def key_blocks(t):
        tp = jnp.pad(to_sub(t), ((0, 0), (QB, 0), (0, 0), (0, 0), (0, 0)))
        prev = tp[:, :Lp].reshape(B, nb, QB, d, H, Dh)
        cur = tp[:, QB:QB + Lp].reshape(B, nb, QB, d, H, Dh)
        return jnp.concatenate([prev, cur], axis=2)

    kb = key_blocks(k)
    vb = key_blocks(v)

    s = jnp.einsum("bnqrhe,bnkrhe->bnrhqk", qs, kb,
                   preferred_element_type=jnp.float32)
    q_loc = jnp.arange(QB)[:, None] + QB
    k_loc = jnp.arange(2 * QB)[None, :]
    delta = q_loc - k_loc
    key_sub = jnp.arange(nb)[:, None] * QB - QB + jnp.arange(2 * QB)[None, :]
    valid = ((delta >= 0) & (delta <= win))[None] & (key_sub >= 0)[:, None, :]
    bias = -slopes[:, None, None] * (d * delta).astype(jnp.float32)[None]
    s = jnp.where(valid[None, :, None, None], s + bias, -jnp.inf)

    m = jnp.max(s, axis=-1, keepdims=True)
    p = jnp.exp(s - m)
    den = jnp.sum(p, axis=-1)
    lse = m[..., 0] + jnp.log(den)
    o = jnp.einsum("bnrhqk,bnkrhe->bnqrhe", p, vb.astype(jnp.float32))
    o = o / jnp.transpose(den, (0, 1, 4, 2, 3))[..., None]
    o = o.reshape(B, Lp, d, H, Dh)[:, :L].reshape(B, S, H, Dh)
    lse = jnp.transpose(lse, (0, 1, 4, 2, 3)).reshape(B, Lp, d, H)[:, :L].reshape(B, S, H)
    return o, lse


def mixture_of_dilations(q, k, v, slopes):
    outs, lses = [], []
    for window, dilation in ATTN_PATTERNS:
        o, lse = dilated_window_attention(q, k, v, window, dilation, slopes)
        outs.append(o)
        lses.append(lse)
    alpha = jax.nn.softmax(jnp.stack(lses, axis=0), axis=0)
    o = jnp.sum(alpha[..., None] * jnp.stack(outs, axis=0), axis=0)
    return o.astype(q.dtype)


def hybrid_mixer(h, w_in, b_merge, conv_w, w_out_conv, w_out_attn, w_o):
    B, S, _ = h.shape
    proj = jnp.einsum("bsd,dc->bsc", h, w_in)
    splits = np.cumsum([D_CONV] * 4 + [D_ATTN] * 4 + [D_MODEL])
    xc, bg, cg, zc, q, k, v, za, g_conv, g_attn = jnp.split(proj, splits, axis=-1)

    c = causal_depthwise_conv(cg * xc, conv_w)
    y_conv = jnp.einsum("bsc,cd->bsd", jax.nn.silu(zc) * bg * c, w_out_conv)

    shp = (B, S, N_HEADS, HEAD_DIM)
    o = mixture_of_dilations(q.reshape(shp), k.reshape(shp), v.reshape(shp),
                             alibi_slopes(N_HEADS)).reshape(B, S, D_ATTN)
    y_attn = jnp.einsum("bsc,cd->bsd", jax.nn.silu(za) * o, w_out_attn)

    g_conv = jax.nn.sigmoid(g_conv + b_merge[:D_MODEL])
    g_attn = jax.nn.sigmoid(g_attn + b_merge[D_MODEL:])
    merged = g_conv * y_conv + g_attn * y_attn
    return jnp.einsum("bsd,de->bse", merged, w_o)


def _fwd_setup_inputs(seed: int = 0) -> dict:
    key = jax.random.key(seed)
    ks = jax.random.split(key, 10)
    f32 = jnp.float32
    x = jax.random.normal(ks[0], (BATCH, SEQ, D_MODEL), f32)
    norm_g = 1.0 + 0.02 * jax.random.normal(ks[1], (DEPTH, D_MODEL), f32)
    w_in = jax.random.normal(ks[2], (DEPTH, D_MODEL, IN_COLS), f32) * D_MODEL ** -0.5
    b_merge = 0.01 * jax.random.normal(ks[3], (DEPTH, 2 * D_MODEL), f32)
    conv_w = jax.random.normal(ks[4], (DEPTH, CONV_WIDTH, D_CONV), f32) * CONV_WIDTH ** -0.5
    w_out_conv = jax.random.normal(ks[5], (DEPTH, D_CONV, D_MODEL), f32) * D_CONV ** -0.5
    w_out_attn = jax.random.normal(ks[6], (DEPTH, D_ATTN, D_MODEL), f32) * D_ATTN ** -0.5
    w_o = jax.random.normal(ks[7], (DEPTH, D_MODEL, D_MODEL), f32) * D_MODEL ** -0.5
    final_g = 1.0 + 0.02 * jax.random.normal(ks[8], (D_MODEL,), f32)
    return {"x": x, "norm_g": norm_g, "w_in": w_in, "b_merge": b_merge,
            "conv_w": conv_w, "w_out_conv": w_out_conv, "w_out_attn": w_out_attn,
            "w_o": w_o, "final_g": final_g}


def _fwd_reference(x, norm_g, w_in, b_merge, conv_w, w_out_conv, w_out_attn, w_o, final_g):
    h = x
    for layer in range(DEPTH):
        u = rms_norm(h, norm_g[layer])
        h = h + hybrid_mixer(u, w_in[layer], b_merge[layer], conv_w[layer],
                             w_out_conv[layer], w_out_attn[layer], w_o[layer])
    return rms_norm(h, final_g)


import jax as _jax
import jax.numpy as _jnp

TWIN_FORMAT = 'train_step'
FWD_PARAMS = ['x', 'norm_g', 'w_in', 'b_merge', 'conv_w', 'w_out_conv', 'w_out_attn', 'w_o', 'final_g']
TWIN_WEIGHTS = ['norm_g', 'w_in', 'b_merge', 'conv_w', 'w_out_conv', 'w_out_attn', 'w_o', 'final_g']
TWIN_DIFF_INPUT = 'x'
TWIN_INPUTS = ['x', 'norm_g', 'w_in', 'b_merge', 'conv_w', 'w_out_conv', 'w_out_attn', 'w_o', 'final_g', 'loss_target', 'm_norm_g', 'm_w_in', 'm_b_merge', 'm_conv_w', 'm_w_out_conv', 'm_w_out_attn', 'm_w_o', 'm_final_g', 'v_norm_g', 'v_w_in', 'v_b_merge', 'v_conv_w', 'v_w_out_conv', 'v_w_out_attn', 'v_w_o', 'v_final_g']
TWIN_OUTPUTS = ['loss', 'grad_x', 'grad_norm_g', 'grad_w_in', 'grad_b_merge', 'grad_conv_w', 'grad_w_out_conv', 'grad_w_out_attn', 'grad_w_o', 'grad_final_g', 'delta_norm_g', 'delta_w_in', 'delta_b_merge', 'delta_conv_w', 'delta_w_out_conv', 'delta_w_out_attn', 'delta_w_o', 'delta_final_g', 'new_m_norm_g', 'new_m_w_in', 'new_m_b_merge', 'new_m_conv_w', 'new_m_w_out_conv', 'new_m_w_out_attn', 'new_m_w_o', 'new_m_final_g', 'new_v_norm_g', 'new_v_w_in', 'new_v_b_merge', 'new_v_conv_w', 'new_v_w_out_conv', 'new_v_w_out_attn', 'new_v_w_o', 'new_v_final_g']
TWIN_LEAF_KINDS = {'loss': 'loss', 'grad_x': 'grad_x', 'grad_norm_g': 'grad_w', 'grad_w_in': 'grad_w', 'grad_b_merge': 'grad_w', 'grad_conv_w': 'grad_w', 'grad_w_out_conv': 'grad_w', 'grad_w_out_attn': 'grad_w', 'grad_w_o': 'grad_w', 'grad_final_g': 'grad_w', 'delta_norm_g': 'delta_w', 'delta_w_in': 'delta_w', 'delta_b_merge': 'delta_w', 'delta_conv_w': 'delta_w', 'delta_w_out_conv': 'delta_w', 'delta_w_out_attn': 'delta_w', 'delta_w_o': 'delta_w', 'delta_final_g': 'delta_w', 'new_m_norm_g': 'new_m', 'new_m_w_in': 'new_m', 'new_m_b_merge': 'new_m', 'new_m_conv_w': 'new_m', 'new_m_w_out_conv': 'new_m', 'new_m_w_out_attn': 'new_m', 'new_m_w_o': 'new_m', 'new_m_final_g': 'new_m', 'new_v_norm_g': 'new_v', 'new_v_w_in': 'new_v', 'new_v_b_merge': 'new_v', 'new_v_conv_w': 'new_v', 'new_v_w_out_conv': 'new_v', 'new_v_w_out_attn': 'new_v', 'new_v_w_o': 'new_v', 'new_v_final_g': 'new_v'}


def _forward(args):
    return _fwd_reference(*[args[k] for k in FWD_PARAMS])


def _output_shape():
    out = _jax.eval_shape(lambda: _forward(_fwd_setup_inputs(0)))
    return out.shape, out.dtype

N_MICROBATCH = 1
ADAM_LR = 0.001
ADAM_B1 = 0.9
ADAM_B2 = 0.999
ADAM_EPS = 1e-08
ADAM_WD = 0.01
ADAM_STEP = 10
PER_EXAMPLE_BATCH_AXIS = {'x': 0, 'loss_target': 0}
SHARED_INPUTS = []
_WEIGHT_DTYPES = {'norm_g': _jnp.float32, 'w_in': _jnp.float32, 'b_merge': _jnp.float32, 'conv_w': _jnp.float32, 'w_out_conv': _jnp.float32, 'w_out_attn': _jnp.float32, 'w_o': _jnp.float32, 'final_g': _jnp.float32}
MOMENT_SCALE = {'norm_g': 1.175128e-01, 'w_in': 3.695595e-02, 'b_merge': 1.681737e-02, 'conv_w': 5.640728e-02, 'w_out_conv': 5.443986e-02, 'w_out_attn': 2.224242e-02, 'w_o': 5.774092e-02, 'final_g': 3.200220e+01}


def _to_microbatches(a, axis):
    t = _jnp.moveaxis(a, axis, 0)
    t = t.reshape((N_MICROBATCH, t.shape[0] // N_MICROBATCH) + t.shape[1:])
    return _jnp.moveaxis(t, 1, axis + 1)


def setup_inputs(seed: int = 0) -> dict:
    inp = _fwd_setup_inputs(seed)
    key = _jax.random.fold_in(_jax.random.key(seed), 7919)
    shape, _ = _output_shape()
    out = dict(inp)
    out["loss_target"] = _jax.random.normal(_jax.random.fold_in(key, 0), shape, _jnp.float32)
    for i, name in enumerate(TWIN_WEIGHTS):
        w = inp[name].astype(_jnp.float32)
        if MOMENT_SCALE is None:
            s = _jnp.sqrt(_jnp.mean(_jnp.square(w)) + 1e-30)
        else:
            s = MOMENT_SCALE[name]
        km, kv = _jax.random.split(_jax.random.fold_in(key, i + 1))
        out[name] = w
        out["m_" + name] = s * _jax.random.normal(km, w.shape, _jnp.float32)
        out["v_" + name] = (s * s) * _jax.random.uniform(kv, w.shape, _jnp.float32, 0.5, 1.5)
    if N_MICROBATCH > 1:
        for name, axis in PER_EXAMPLE_BATCH_AXIS.items():
            out[name] = _to_microbatches(out[name], axis)
    return {'x': out['x'], 'norm_g': out['norm_g'], 'w_in': out['w_in'], 'b_merge': out['b_merge'], 'conv_w': out['conv_w'], 'w_out_conv': out['w_out_conv'], 'w_out_attn': out['w_out_attn'], 'w_o': out['w_o'], 'final_g': out['final_g'], 'loss_target': out['loss_target'], 'm_norm_g': out['m_norm_g'], 'm_w_in': out['m_w_in'], 'm_b_merge': out['m_b_merge'], 'm_conv_w': out['m_conv_w'], 'm_w_out_conv': out['m_w_out_conv'], 'm_w_out_attn': out['m_w_out_attn'], 'm_w_o': out['m_w_o'], 'm_final_g': out['m_final_g'], 'v_norm_g': out['v_norm_g'], 'v_w_in': out['v_w_in'], 'v_b_merge': out['v_b_merge'], 'v_conv_w': out['v_conv_w'], 'v_w_out_conv': out['v_w_out_conv'], 'v_w_out_attn': out['v_w_out_attn'], 'v_w_o': out['v_w_o'], 'v_final_g': out['v_final_g']}


def _loss(weights, diff, rest, loss_target):
    with _jax.named_scope("forward"):
        args = {**rest, TWIN_DIFF_INPUT: diff, **{k: w.astype(_WEIGHT_DTYPES[k]) for k, w in weights.items()}}
        y = _forward(args)
    with _jax.named_scope("loss_head"):
        err = _jnp.square(y.astype(_jnp.float32) - loss_target)
        return 0.5 * _jnp.sum(_jnp.mean(err, axis=-1)) if err.ndim else 0.5 * err


def _adamw(w, g, m, v):
    m = ADAM_B1 * m + (1.0 - ADAM_B1) * g
    v = ADAM_B2 * v + (1.0 - ADAM_B2) * _jnp.square(g)
    m_hat = m / (1.0 - ADAM_B1 ** ADAM_STEP)
    v_hat = v / (1.0 - ADAM_B2 ** ADAM_STEP)
    delta = -ADAM_LR * (m_hat / (_jnp.sqrt(v_hat) + ADAM_EPS) + ADAM_WD * w)
    return delta, m, v


def reference(x, norm_g, w_in, b_merge, conv_w, w_out_conv, w_out_attn, w_o, final_g, loss_target, m_norm_g, m_w_in, m_b_merge, m_conv_w, m_w_out_conv, m_w_out_attn, m_w_o, m_final_g, v_norm_g, v_w_in, v_b_merge, v_conv_w, v_w_out_conv, v_w_out_attn, v_w_o, v_final_g):
    given = dict(x=x, norm_g=norm_g, w_in=w_in, b_merge=b_merge, conv_w=conv_w, w_out_conv=w_out_conv, w_out_attn=w_out_attn, w_o=w_o, final_g=final_g, loss_target=loss_target, m_norm_g=m_norm_g, m_w_in=m_w_in, m_b_merge=m_b_merge, m_conv_w=m_conv_w, m_w_out_conv=m_w_out_conv, m_w_out_attn=m_w_out_attn, m_w_o=m_w_o, m_final_g=m_final_g, v_norm_g=v_norm_g, v_w_in=v_w_in, v_b_merge=v_b_merge, v_conv_w=v_conv_w, v_w_out_conv=v_w_out_conv, v_w_out_attn=v_w_out_attn, v_w_o=v_w_o, v_final_g=v_final_g)
    weights = {n: given[n] for n in TWIN_WEIGHTS}
    shared = {n: given[n] for n in SHARED_INPUTS}
    per_example = {n: given[n] for n in ['x']}
    grad_fn = _jax.value_and_grad(_loss, argnums=(0, 1))

    def one_microbatch(ex, loss_target):
        ex = dict(ex)
        diff = ex.pop(TWIN_DIFF_INPUT)
        return grad_fn(weights, diff, {**shared, **ex}, loss_target)

    if N_MICROBATCH == 1:
        loss, (grad_w, grad_x) = one_microbatch(per_example, given["loss_target"])
    else:
        def body(carry, xs):
            loss_sum, grad_sum = carry
            l_k, (gw_k, gx_k) = one_microbatch(xs[0], xs[1])
            with _jax.named_scope("update"):
                return (loss_sum + l_k, _jax.tree.map(_jnp.add, grad_sum, gw_k)), gx_k

        init = (_jnp.zeros((), _jnp.float32), _jax.tree.map(_jnp.zeros_like, weights))
        (loss, grad_w), grad_x = _jax.lax.scan(body, init, (per_example, given["loss_target"]))
    with _jax.named_scope("update"):
        delta_w, new_m, new_v = {}, {}, {}
        for n in TWIN_WEIGHTS:
            delta_w[n], new_m[n], new_v[n] = _adamw(weights[n], grad_w[n], given["m_" + n], given["v_" + n])
    return (loss, grad_x, *[grad_w[n] for n in TWIN_WEIGHTS], *[delta_w[n] for n in TWIN_WEIGHTS],
            *[new_m[n] for n in TWIN_WEIGHTS], *[new_v[n] for n in TWIN_WEIGHTS])
```

```python
import functools

import jax
import jax.numpy as jnp
from jax import lax
from jax.experimental import pallas as pl
from jax.experimental.pallas import tpu as pltpu

D = 1024
N_HEADS = 16
HEAD_DIM = 64
N_SEG = 10
IN_COLS = N_SEG * D
N_DEV = 8
W_IN_SHARD = IN_COLS // N_DEV
ROW_SHARD = D // N_DEV
QB = 128
DILATIONS = (1, 4, 16)
EPS = 1e-6
NEG = -1e30
BF = jnp.bfloat16
F32 = jnp.float32
MESH = pl.DeviceIdType.MESH

ADAM_LR = 0.001
ADAM_B1 = 0.9
ADAM_B2 = 0.999
ADAM_EPS = 1e-08
ADAM_WD = 0.01
ADAM_STEP = 10

V7X_VMEM_BYTES = 64 * 1024 * 1024
VMEM_LIMIT = V7X_VMEM_BYTES - 8 * 1024 * 1024
ROW_TILE = 256

VMEM_SPEC = pl.BlockSpec(memory_space=pltpu.VMEM)
ANY_SPEC = pl.BlockSpec(memory_space=pl.ANY)
SMEM_SPEC = pl.BlockSpec(memory_space=pltpu.SMEM)


def _params(n_grid_axes, vmem=VMEM_LIMIT):
    return pltpu.CompilerParams(dimension_semantics=("arbitrary",) * n_grid_axes, vmem_limit_bytes=vmem)


def _dot(a, b):
    return jnp.dot(a, b, preferred_element_type=F32)


def _dot_nt(a, b):
    return lax.dot_general(a, b, (((1,), (1,)), ((), ())), preferred_element_type=F32)


def _sigmoid(z):
    return 1.0 / (1.0 + jnp.exp(-z))


def _my_place():
    x, y, c = lax.axis_index("x"), lax.axis_index("y"), lax.axis_index("c")
    return x, y, c, 4 * x + 2 * y + c


def _peers(x, y, c):
    out = []
    for k in range(1, N_DEV):
        px = 1 - x if k & 4 else x
        py = 1 - y if k & 2 else y
        pc = 1 - c if k & 1 else c
        out.append(((px, py, pc), 4 * px + 2 * py + pc))
    return out


def _allgather_weights(w_in, w3, cw):
    def body(w_in_ref, w3_ref, cw_ref, o_in, o_3, o_cw, in_bf, w3_bf, send_sems, recv_sems, local_sems):
        x, y, c, me = _my_place()

        def cast_rows(i, carry):
            r = pl.multiple_of(i * 128, 128)
            in_bf[pl.ds(r, 128), :] = w_in_ref[pl.ds(r, 128), :].astype(BF)
            return carry

        lax.fori_loop(0, D // 128, cast_rows, 0)
        for a in range(3):
            w3_bf[a] = w3_ref[a].astype(BF)

        srcs = (in_bf, w3_bf, cw_ref)
        outs = (o_in, o_3, o_cw)
        local = [pltpu.make_async_copy(srcs[a], outs[a].at[me], local_sems.at[a]) for a in range(3)]
        for cp in local:
            cp.start()
        remote = []
        for k, (peer, _) in enumerate(_peers(x, y, c)):
            for a in range(3):
                cp = pltpu.make_async_remote_copy(
                    src_ref=srcs[a], dst_ref=outs[a].at[me], send_sem=send_sems.at[k, a],
                    recv_sem=recv_sems.at[k, a], device_id=peer, device_id_type=MESH)
                cp.start()
                remote.append(cp)
        for cp in remote:
            cp.wait()
        for cp in local:
            cp.wait()

    return pl.pallas_call(
        body, name="allgather_weights",
        out_shape=(jax.ShapeDtypeStruct((N_DEV, D, W_IN_SHARD), BF),
                   jax.ShapeDtypeStruct((N_DEV, 3, ROW_SHARD, D), BF),
                   jax.ShapeDtypeStruct((N_DEV, 8, 128), F32)),
        in_specs=[VMEM_SPEC, VMEM_SPEC, VMEM_SPEC],
        out_specs=(ANY_SPEC, ANY_SPEC, ANY_SPEC),
        scratch_shapes=[pltpu.VMEM((D, W_IN_SHARD), BF), pltpu.VMEM((3, ROW_SHARD, D), BF),
                        pltpu.SemaphoreType.DMA((N_DEV - 1, 3)), pltpu.SemaphoreType.DMA((N_DEV - 1, 3)),
                        pltpu.SemaphoreType.DMA((3,))],
        compiler_params=pltpu.CompilerParams(vmem_limit_bytes=VMEM_LIMIT),
    )(w_in, w3, cw)


def _exchange_grads(g_in, g_3):
    def body(g_in_ref, g_3_ref, r_in, r_3, send_sems, recv_sems, local_sems):
        x, y, c, me = _my_place()
        srcs = (g_in_ref, g_3_ref)
        outs = (r_in, r_3)
        local = [pltpu.make_async_copy(srcs[a].at[me], outs[a].at[me], local_sems.at[a]) for a in range(2)]
        for cp in local:
            cp.start()
        remote = []
        for k, (peer, peer_idx) in enumerate(_peers(x, y, c)):
            for a in range(2):
                cp = pltpu.make_async_remote_copy(
                    src_ref=srcs[a].at[peer_idx], dst_ref=outs[a].at[me], send_sem=send_sems.at[k, a],
                    recv_sem=recv_sems.at[k, a], device_id=peer, device_id_type=MESH)
                cp.start()
                remote.append(cp)
        for cp in remote:
            cp.wait()
        for cp in local:
            cp.wait()

    return pl.pallas_call(
        body, name="exchange_grads",
        out_shape=(jax.ShapeDtypeStruct(g_in.shape, BF), jax.ShapeDtypeStruct(g_3.shape, BF)),
        in_specs=[ANY_SPEC, ANY_SPEC], out_specs=(ANY_SPEC, ANY_SPEC),
        scratch_shapes=[pltpu.SemaphoreType.DMA((N_DEV - 1, 2)), pltpu.SemaphoreType.DMA((N_DEV - 1, 2)),
                        pltpu.SemaphoreType.DMA((2,))],
    )(g_in, g_3)


def _allreduce_small(p_mid, p_conv, p_norm):
    def body(a_ref, b_ref, c_ref, out_ref, mine, gathered, send_sems, recv_sems):
        x, y, c, me = _my_place()
        mine[...] = a_ref[...] + b_ref[...] + c_ref[...]
        gathered[me] = mine[...]
        remote = []
        for k, (peer, _) in enumerate(_peers(x, y, c)):
            cp = pltpu.make_async_remote_copy(
                src_ref=mine, dst_ref=gathered.at[me], send_sem=send_sems.at[k], recv_sem=recv_sems.at[k],
                device_id=peer, device_id_type=MESH)
            cp.start()
            remote.append(cp)
        for cp in remote:
            cp.wait()
        total = gathered[0]
        for s in range(1, N_DEV):
            total = total + gathered[s]
        out_ref[...] = total

    return pl.pallas_call(
        body, name="allreduce_small",
        out_shape=jax.ShapeDtypeStruct((8, D), F32),
        in_specs=[VMEM_SPEC, VMEM_SPEC, VMEM_SPEC], out_specs=VMEM_SPEC,
        scratch_shapes=[pltpu.VMEM((8, D), F32), pltpu.VMEM((N_DEV, 8, D), F32),
                        pltpu.SemaphoreType.DMA((N_DEV - 1,)), pltpu.SemaphoreType.DMA((N_DEV - 1,))],
    )(p_mid, p_conv, p_norm)


def _norm_proj(x2, norm_g, w_all):
    S = x2.shape[0]
    tm = ROW_TILE

    def body(x_ref, g_ref, w_ref, proj_ref, ut_ref):
        xv = x_ref[...]
        r = lax.rsqrt(jnp.mean(xv * xv, axis=-1, keepdims=True) + EPS)
        u = xv * r * g_ref[...]
        ut_ref[...] = u.T.astype(BF)
        ub = u.astype(BF)
        for j in range(N_DEV):
            proj_ref[:, j * W_IN_SHARD:(j + 1) * W_IN_SHARD] = _dot(ub, w_ref[j]).astype(BF)

    return pl.pallas_call(
        body, name="norm_proj", grid=(S // tm,),
        in_specs=[pl.BlockSpec((tm, D), lambda i: (i, 0)), pl.BlockSpec((1, D), lambda i: (0, 0)), VMEM_SPEC],
        out_specs=(pl.BlockSpec((tm, IN_COLS), lambda i: (i, 0)), pl.BlockSpec((D, tm), lambda i: (0, i))),
        out_shape=(jax.ShapeDtypeStruct((S, IN_COLS), BF), jax.ShapeDtypeStruct((D, S), BF)),
        compiler_params=_params(1),
    )(x2, norm_g, w_all)


CONV_TM, CONV_TC = 256, 512
HALO = 16


def _conv_fwd(proj, cw8):
    S = proj.shape[0]
    tm, tc = CONV_TM, CONV_TC
    nct = D // tc

    def seg(s):
        return pl.BlockSpec((tm, tc), lambda i, j, s=s: (i, s * nct + j))

    def halo_before(s):
        return pl.BlockSpec((HALO, tc), lambda i, j, s=s: (jnp.maximum(i * (tm // HALO) - 1, 0), s * nct + j))

    def body(xc, bg, cg, zc, xch, cgh, cw, out):
        i = pl.program_id(0)
        a = cg[...].astype(F32) * xc[...].astype(F32)
        ah = cgh[...].astype(F32) * xch[...].astype(F32)
        ah = jnp.where(i > 0, ah, 0.0)
        row = lax.broadcasted_iota(jnp.int32, (tm, tc), 0)
        a1 = jnp.where(row == 0, ah[HALO - 1:HALO, :], pltpu.roll(a, 1, 0))
        a2 = jnp.where(row == 0, ah[HALO - 2:HALO - 1, :],
                       jnp.where(row == 1, ah[HALO - 1:HALO, :], pltpu.roll(a, 2, 0)))
        w = cw[...]
        conv = w[0:1, :] * a2 + w[1:2, :] * a1 + w[2:3, :] * a
        z = zc[...].astype(F32)
        out[...] = (z * _sigmoid(z) * bg[...].astype(F32) * conv).astype(BF)

    return pl.pallas_call(
        body, name="conv_fwd", grid=(S // tm, nct),
        in_specs=[seg(0), seg(1), seg(2), seg(3), halo_before(0), halo_before(2),
                  pl.BlockSpec((8, tc), lambda i, j: (0, j))],
        out_specs=pl.BlockSpec((tm, tc), lambda i, j: (i, j)),
        out_shape=jax.ShapeDtypeStruct((S, D), BF),
        compiler_params=_params(2),
    )(proj, proj, proj, proj, proj, proj, cw8)


def _attn_masks():
    row = lax.broadcasted_iota(jnp.int32, (QB, QB), 0)
    lane = lax.broadcasted_iota(jnp.int32, (QB, QB), 1)
    dist = (row - lane).astype(F32)
    return row, lane, dist


def _attn_fwd(proj, slopes, d):
    S = proj.shape[0]
    L = S // d
    nb = L // QB
    pv = proj.reshape(L, d * IN_COLS)
    cpr = IN_COLS // 128
    hpr = D // 128

    def in_spec(s):
        return pl.BlockSpec((L, 128), lambda r, hp, s=s: (0, r * cpr + s * hpr + hp))

    out_spec = pl.BlockSpec((L, 128), lambda r, hp: (0, r * hpr + hp))

    def body(sl_ref, q_ref, k_ref, v_ref, o_ref, lse_ref):
        hp = pl.program_id(1)
        row, lane, dist = _attn_masks()
        low = lane < HEAD_DIM
        head_mask = (low.astype(F32).astype(BF), (~low).astype(F32).astype(BF))
        mask_cur = row >= lane
        mask_prev = lane >= row
        slope = (sl_ref[2 * hp] * float(d), sl_ref[2 * hp + 1] * float(d))

        def block(n, carry):
            r0 = pl.multiple_of(n * QB, QB)
            p0 = pl.multiple_of(jnp.maximum(n - 1, 0) * QB, QB)
            q2 = q_ref[pl.ds(r0, QB), :] * 0.125
            kc, kp = k_ref[pl.ds(r0, QB), :], k_ref[pl.ds(p0, QB), :]
            vc, vp = v_ref[pl.ds(r0, QB), :], v_ref[pl.ds(p0, QB), :]
            prev_ok = jnp.logical_and(mask_prev, n > 0)
            o_h, lse_h = [], []
            for a in range(2):
                sc = _dot_nt(q2, kc * head_mask[a]) - slope[a] * dist
                sp = _dot_nt(q2, kp * head_mask[a]) - slope[a] * (dist + float(QB))
                sc = jnp.where(mask_cur, sc, NEG)
                sp = jnp.where(prev_ok, sp, NEG)
                m = jnp.maximum(jnp.max(sc, axis=-1, keepdims=True), jnp.max(sp, axis=-1, keepdims=True))
                pc = jnp.exp(sc - m)
                pp = jnp.exp(sp - m)
                den = jnp.sum(pc, axis=-1, keepdims=True) + jnp.sum(pp, axis=-1, keepdims=True)
                o_h.append((_dot(pc.astype(BF), vc) + _dot(pp.astype(BF), vp)) / den)
                lse_h.append(m + jnp.log(den))
            o_ref[pl.ds(r0, QB), :] = jnp.where(low, o_h[0], o_h[1])
            lse_ref[pl.ds(r0, QB), :] = jnp.where(low, lse_h[0], lse_h[1])
            return carry

        lax.fori_loop(0, nb, block, 0)

    o, lse = pl.pallas_call(
        body, name=f"attn_fwd_d{d}", grid=(d, hpr),
        in_specs=[SMEM_SPEC, in_spec(4), in_spec(5), in_spec(6)],
        out_specs=(out_spec, out_spec),
        out_shape=(jax.ShapeDtypeStruct((L, d * D), F32), jax.ShapeDtypeStruct((L, d * D), F32)),
        compiler_params=_params(2),
    )(slopes, pv, pv, pv)
    return o.reshape(S, D), lse.reshape(S, D)


def _attn_combine(outs, lses, proj):
    S = proj.shape[0]
    tm = ROW_TILE
    tile = pl.BlockSpec((tm, D), lambda i: (i, 0))

    def body(o1, o2, o3, l1, l2, l3, za, o_ref, lse_ref, ya_ref):
        a, b, c = l1[...], l2[...], l3[...]
        m = jnp.maximum(jnp.maximum(a, b), c)
        ea, eb, ec = jnp.exp(a - m), jnp.exp(b - m), jnp.exp(c - m)
        den = ea + eb + ec
        o = (ea / den) * o1[...] + (eb / den) * o2[...] + (ec / den) * o3[...]
        o_ref[...] = o
        lse_ref[...] = m + jnp.log(den)
        z = za[...].astype(F32)
        ya_ref[...] = (z * _sigmoid(z) * o).astype(BF)

    return pl.pallas_call(
        body, name="attn_combine", grid=(S // tm,),
        in_specs=[tile] * 6 + [pl.BlockSpec((tm, D), lambda i: (i, 7))],
        out_specs=(tile, tile, tile),
        out_shape=(jax.ShapeDtypeStruct((S, D), F32), jax.ShapeDtypeStruct((S, D), F32),
                   jax.ShapeDtypeStruct((S, D), BF)),
        compiler_params=_params(1),
    )(*outs, *lses, proj)


def _set_rows(shape, rows):
    idx = lax.broadcasted_iota(jnp.int32, shape, 0)
    out = jnp.zeros(shape, F32)
    for r, val in rows.items():
        out = out + jnp.where(idx == r, val, 0.0)
    return out


def _mid(yc_in, ya_in, proj, o, x2, target, b_merge, final_g, w3):
    S = x2.shape[0]
    tm = ROW_TILE
    nsteps = S // tm
    tile = pl.BlockSpec((tm, D), lambda i: (i, 0))

    def body(yc_ref, ya_ref, za_ref, gcp_ref, gap_ref, o_ref, x_ref, t_ref, b_ref, fg_ref, w_ref,
             dh_ref, dmid_ref, do_ref, dyc_ref, gw_ref, small_ref, acc, stage):
        i = pl.program_id(0)

        @pl.when(i == 0)
        def _():
            acc[...] = jnp.zeros_like(acc)
            small_ref[...] = jnp.zeros_like(small_ref)

        wc, wa, wo = w_ref[0], w_ref[1], w_ref[2]
        yc_in_b, ya_in_b = yc_ref[...], ya_ref[...]
        yc = _dot(yc_in_b, wc)
        ya = _dot(ya_in_b, wa)
        b = b_ref[...]
        gc = _sigmoid(gcp_ref[...].astype(F32) + b[:, :D])
        ga = _sigmoid(gap_ref[...].astype(F32) + b[:, D:])
        merged = gc * yc + ga * ya
        merged_b = merged.astype(BF)
        h = x_ref[...] + _dot(merged_b, wo)
        r2 = lax.rsqrt(jnp.mean(h * h, axis=-1, keepdims=True) + EPS)
        n = h * r2
        fg = fg_ref[...]
        err = n * fg - t_ref[...]
        loss = 0.5 * jnp.sum(jnp.sum(err * err, axis=-1, keepdims=True) / D, axis=0, keepdims=True)
        dy = err / D
        g_fg = jnp.sum(dy * n, axis=0, keepdims=True)
        dn = dy * fg
        dh = r2 * (dn - n * jnp.mean(dn * n, axis=-1, keepdims=True))
        dh_ref[...] = dh
        dh_b = dh.astype(BF)
        dmerged = _dot_nt(dh_b, wo)
        acc[2] += _dot(merged.T.astype(BF), dh_b)
        dyc = (dmerged * gc).astype(BF)
        dya = (dmerged * ga).astype(BF)
        dgcp = dmerged * yc * gc * (1.0 - gc)
        dgap = dmerged * ya * ga * (1.0 - ga)
        dmid_ref[1] = dgcp.astype(BF)
        dmid_ref[2] = dgap.astype(BF)
        acc[0] += _dot(yc_in_b.astype(F32).T.astype(BF), dyc)
        acc[1] += _dot(ya_in_b.astype(F32).T.astype(BF), dya)
        dyc_ref[...] = _dot_nt(dyc, wc).astype(BF)
        dya_in = _dot_nt(dya, wa)
        z = za_ref[...].astype(F32)
        sg = _sigmoid(z)
        do_ref[...] = (dya_in * (z * sg)).astype(BF)
        dmid_ref[0] = (dya_in * o_ref[...] * (sg * (1.0 + z * (1.0 - sg)))).astype(BF)
        small_ref[...] += _set_rows((8, D), {
            1: jnp.sum(dgcp, axis=0, keepdims=True), 2: jnp.sum(dgap, axis=0, keepdims=True),
            3: g_fg, 7: jnp.broadcast_to(loss, (1, D))})

        @pl.when(i == nsteps - 1)
        def _():
            for p in range(N_DEV):
                for a in range(3):
                    stage[...] = acc[a, p * ROW_SHARD:(p + 1) * ROW_SHARD, :].astype(BF)
                    pltpu.sync_copy(stage, gw_ref.at[p, a])

    return pl.pallas_call(
        body, name="mid", grid=(nsteps,),
        in_specs=[tile, tile, pl.BlockSpec((tm, D), lambda i: (i, 7)), pl.BlockSpec((tm, D), lambda i: (i, 8)),
                  pl.BlockSpec((tm, D), lambda i: (i, 9)), tile, tile, tile,
                  pl.BlockSpec((1, 2 * D), lambda i: (0, 0)), pl.BlockSpec((1, D), lambda i: (0, 0)), VMEM_SPEC],
        out_specs=(tile, pl.BlockSpec((3, tm, D), lambda i: (0, i, 0)), tile, tile,
                   ANY_SPEC, pl.BlockSpec((8, D), lambda i: (0, 0))),
        out_shape=(jax.ShapeDtypeStruct((S, D), F32), jax.ShapeDtypeStruct((3, S, D), BF),
                   jax.ShapeDtypeStruct((S, D), BF), jax.ShapeDtypeStruct((S, D), BF),
                   jax.ShapeDtypeStruct((N_DEV, 3, ROW_SHARD, D), BF), jax.ShapeDtypeStruct((8, D), F32)),
        scratch_shapes=[pltpu.VMEM((3, D, D), F32), pltpu.VMEM((ROW_SHARD, D), BF)],
        compiler_params=_params(1),
    )(yc_in, ya_in, proj, proj, proj, o, x2, target, b_merge, final_g, w3)


def _conv_bwd(dyc_in, proj, cw8):
    S = proj.shape[0]
    tm, tc = CONV_TM, CONV_TC
    nct = D // tc
    nrt = S // tm
    last_halo = S // HALO - 1

    def seg(s):
        return pl.BlockSpec((tm, tc), lambda j, i, s=s: (i, s * nct + j))

    def halo_before(s):
        return pl.BlockSpec((HALO, tc), lambda j, i, s=s: (jnp.maximum(i * (tm // HALO) - 1, 0), s * nct + j))

    def halo_after(s):
        return pl.BlockSpec((HALO, tc), lambda j, i, s=s: (jnp.minimum((i + 1) * (tm // HALO), last_halo), s * nct + j))

    def body(dy, xc, bg, cg, zc, xch, cgh, dyn, bgn, zcn, cw, dout, gcw):
        i = pl.program_id(1)

        @pl.when(i == 0)
        def _():
            gcw[...] = jnp.zeros_like(gcw)

        xcv, cgv = xc[...].astype(F32), cg[...].astype(F32)
        a = cgv * xcv
        ah = jnp.where(i > 0, cgh[...].astype(F32) * xch[...].astype(F32), 0.0)
        row = lax.broadcasted_iota(jnp.int32, (tm, tc), 0)
        a1 = jnp.where(row == 0, ah[HALO - 1:HALO, :], pltpu.roll(a, 1, 0))
        a2 = jnp.where(row == 0, ah[HALO - 2:HALO - 1, :],
                       jnp.where(row == 1, ah[HALO - 1:HALO, :], pltpu.roll(a, 2, 0)))
        w = cw[...]
        conv = w[0:1, :] * a2 + w[1:2, :] * a1 + w[2:3, :] * a
        z = zc[...].astype(F32)
        sg = _sigmoid(z)
        silu = z * sg
        bgv = bg[...].astype(F32)
        dyv = dy[...].astype(F32)
        dout[3] = (dyv * bgv * conv * (sg * (1.0 + z * (1.0 - sg)))).astype(BF)
        dout[1] = (dyv * silu * conv).astype(BF)
        dc = dyv * silu * bgv
        zn = zcn[...].astype(F32)
        dcn = dyn[...].astype(F32) * (zn * _sigmoid(zn)) * bgn[...].astype(F32)
        dcn = jnp.where(i < nrt - 1, dcn, 0.0)
        dc1 = jnp.where(row == tm - 1, dcn[0:1, :], pltpu.roll(dc, tm - 1, 0))
        dc2 = jnp.where(row == tm - 1, dcn[1:2, :],
                        jnp.where(row == tm - 2, dcn[0:1, :], pltpu.roll(dc, tm - 2, 0)))
        da = w[2:3, :] * dc + w[1:2, :] * dc1 + w[0:1, :] * dc2
        dout[2] = (da * xcv).astype(BF)
        dout[0] = (da * cgv).astype(BF)
        gcw[...] += _set_rows((8, tc), {
            4: jnp.sum(dc * a2, axis=0, keepdims=True), 5: jnp.sum(dc * a1, axis=0, keepdims=True),
            6: jnp.sum(dc * a, axis=0, keepdims=True)})

    return pl.pallas_call(
        body, name="conv_bwd", grid=(nct, nrt),
        in_specs=[pl.BlockSpec((tm, tc), lambda j, i: (i, j)), seg(0), seg(1), seg(2), seg(3),
                  halo_before(0), halo_before(2),
                  pl.BlockSpec((HALO, tc), lambda j, i: (jnp.minimum((i + 1) * (tm // HALO), last_halo), j)),
                  halo_after(1), halo_after(3), pl.BlockSpec((8, tc), lambda j, i: (0, j))],
        out_specs=(pl.BlockSpec((4, tm, tc), lambda j, i: (0, i, j)), pl.BlockSpec((8, tc), lambda j, i: (0, j))),
        out_shape=(jax.ShapeDtypeStruct((4, S, D), BF), jax.ShapeDtypeStruct((8, D), F32)),
        compiler_params=_params(2),
    )(dyc_in, proj, proj, proj, proj, proj, proj, dyc_in, proj, proj, cw8)


def _attn_bwd(proj, slopes, do, o, lse, d):
    S = proj.shape[0]
    L = S // d
    nb = L // QB
    pv = proj.reshape(L, d * IN_COLS)
    cpr = IN_COLS // 128
    hpr = D // 128

    def in_spec(s):
        return pl.BlockSpec((L, 128), lambda r, hp, s=s: (0, r * cpr + s * hpr + hp))

    act_spec = pl.BlockSpec((L, 128), lambda r, hp: (0, r * hpr + hp))

    def body(sl_ref, q_ref, k_ref, v_ref, do_ref, o_ref, lse_ref, dq_ref, dk_ref, dv_ref):
        hp = pl.program_id(1)
        row, lane, dist = _attn_masks()
        low = lane < HEAD_DIM
        head_mask = (low.astype(F32).astype(BF), (~low).astype(F32).astype(BF))
        head_mask_f = (low.astype(F32), (~low).astype(F32))
        mask_cur = row >= lane
        mask_prev = lane >= row
        slope = (sl_ref[2 * hp] * float(d), sl_ref[2 * hp + 1] * float(d))
        dk_ref[...] = jnp.zeros_like(dk_ref)
        dv_ref[...] = jnp.zeros_like(dv_ref)

        def block(n, carry):
            r0 = pl.multiple_of(n * QB, QB)
            p0 = pl.multiple_of(jnp.maximum(n - 1, 0) * QB, QB)
            cur, prev = pl.ds(r0, QB), pl.ds(p0, QB)
            q2 = q_ref[cur, :] * 0.125
            kc, kp = k_ref[cur, :], k_ref[prev, :]
            vc, vp = v_ref[cur, :], v_ref[prev, :]
            do2 = do_ref[cur, :]
            do_o = do2.astype(F32) * o_ref[cur, :]
            lse2 = lse_ref[cur, :]
            prev_ok = jnp.logical_and(mask_prev, n > 0)
            dq = jnp.zeros((QB, 128), F32)
            dk_c = jnp.zeros((QB, 128), F32)
            dk_p = jnp.zeros((QB, 128), F32)
            dv_c = jnp.zeros((QB, 128), F32)
            dv_p = jnp.zeros((QB, 128), F32)
            for a in range(2):
                kcm, kpm = kc * head_mask[a], kp * head_mask[a]
                qm, dom = q2 * head_mask[a], do2 * head_mask[a]
                lse_col = lse2[:, a * HEAD_DIM:a * HEAD_DIM + 1]
                drow = jnp.sum(do_o * head_mask_f[a], axis=-1, keepdims=True)
                sc = _dot_nt(q2, kcm) - slope[a] * dist
                sp = _dot_nt(q2, kpm) - slope[a] * (dist + float(QB))
                pc = jnp.where(mask_cur, jnp.exp(sc - lse_col), 0.0)
                pp = jnp.where(prev_ok, jnp.exp(sp - lse_col), 0.0)
                dsc = (pc * (_dot_nt(dom, vc) - drow)).astype(BF)
                dsp = (pp * (_dot_nt(dom, vp) - drow)).astype(BF)
                dq = dq + _dot(dsc, kcm) + _dot(dsp, kpm)
                dk_c = dk_c + _dot(dsc.astype(F32).T.astype(BF), qm)
                dk_p = dk_p + _dot(dsp.astype(F32).T.astype(BF), qm)
                dv_c = dv_c + _dot(pc.T.astype(BF), dom)
                dv_p = dv_p + _dot(pp.T.astype(BF), dom)
            dq_ref[cur, :] = dq * 0.125
            dk_ref[cur, :] += dk_c
            dk_ref[prev, :] += dk_p
            dv_ref[cur, :] += dv_c
            dv_ref[prev, :] += dv_p
            return carry

        lax.fori_loop(0, nb, block, 0)

    view = lambda t: t.reshape(L, d * D)
    shape = jax.ShapeDtypeStruct((L, d * D), F32)
    dq, dk, dv = pl.pallas_call(
        body, name=f"attn_bwd_d{d}", grid=(d, hpr),
        in_specs=[SMEM_SPEC, in_spec(4), in_spec(5), in_spec(6), act_spec, act_spec, act_spec],
        out_specs=(act_spec, act_spec, act_spec),
        out_shape=(shape, shape, shape),
        compiler_params=_params(2),
    )(slopes, pv, pv, pv, view(do), view(o), view(lse))
    return dq.reshape(S, D), dk.reshape(S, D), dv.reshape(S, D)


def _attn_sum(parts):
    S = parts[0][0].shape[0]
    tm = ROW_TILE
    tile = pl.BlockSpec((tm, D), lambda i: (i, 0))

    def body(*refs):
        ins, out = refs[:9], refs[9]
        for t in range(3):
            out[t] = (ins[t][...] + ins[3 + t][...] + ins[6 + t][...]).astype(BF)

    return pl.pallas_call(
        body, name="attn_sum", grid=(S // tm,),
        in_specs=[tile] * 9, out_specs=pl.BlockSpec((3, tm, D), lambda i: (0, i, 0)),
        out_shape=jax.ShapeDtypeStruct((3, S, D), BF),
        compiler_params=_params(1),
    )(*[t for p in parts for t in p])


GROUP_SEGS = (4, 3, 3)
WG_TN = 256


def _wgrad_in(ut, d_conv, d_attn, d_mid):
    S = ut.shape[1]
    tn = WG_TN
    per_seg = D // tn
    n_tiles = IN_COLS // tn
    first = (0, GROUP_SEGS[0] * per_seg, (GROUP_SEGS[0] + GROUP_SEGS[1]) * per_seg, n_tiles)

    def group_spec(g):
        lo, hi = first[g], first[g + 1]

        def index(j):
            jj = jnp.clip(j, lo, hi - 1) - lo
            return (jj // per_seg, 0, jj % per_seg)

        return pl.BlockSpec((1, S, tn), index)

    per_shard = W_IN_SHARD // tn

    def body(ut_ref, a_ref, b_ref, c_ref, out_ref):
        j = pl.program_id(0)
        for g, ref in enumerate((a_ref, b_ref, c_ref)):
            @pl.when(jnp.logical_and(j >= first[g], j < first[g + 1]))
            def _(ref=ref):
                out_ref[0] = _dot(ut_ref[...], ref[0]).astype(BF)

    return pl.pallas_call(
        body, name="wgrad_in", grid=(n_tiles,),
        in_specs=[VMEM_SPEC, group_spec(0), group_spec(1), group_spec(2)],
        out_specs=pl.BlockSpec((1, D, tn), lambda j: (j // per_shard, 0, j % per_shard)),
        out_shape=jax.ShapeDtypeStruct((N_DEV, D, W_IN_SHARD), BF),
        compiler_params=_params(1),
    )(ut, d_conv, d_attn, d_mid)


def _proj_pieces():
    cuts = sorted(set(range(0, IN_COLS + 1, D)) | set(range(0, IN_COLS + 1, W_IN_SHARD)))
    return [(lo // D, lo % D, lo // W_IN_SHARD, lo % W_IN_SHARD, hi - lo) for lo, hi in zip(cuts[:-1], cuts[1:])]


def _dgrad_norm_bwd(d_conv, d_attn, d_mid, w_all, x2, dh, norm_g):
    S = x2.shape[0]
    tm = ROW_TILE
    tile = pl.BlockSpec((tm, D), lambda i: (i, 0))
    pieces = _proj_pieces()

    def body(a_ref, b_ref, c_ref, w_ref, x_ref, dh_ref, g_ref, gx_ref, small_ref):
        i = pl.program_id(0)

        @pl.when(i == 0)
        def _():
            small_ref[...] = jnp.zeros_like(small_ref)

        groups = (a_ref, b_ref, c_ref)
        du = jnp.zeros((tm, D), F32)
        for s, sc, p, pc, width in pieces:
            g = 0 if s < 4 else (1 if s < 7 else 2)
            local = s - (0, 4, 7)[g]
            du = du + _dot_nt(groups[g][local, :, sc:sc + width], w_ref[p, :, pc:pc + width])
        xv = x_ref[...]
        r = lax.rsqrt(jnp.mean(xv * xv, axis=-1, keepdims=True) + EPS)
        n = xv * r
        dn = du * g_ref[...]
        gx_ref[...] = dh_ref[...] + r * (dn - n * jnp.mean(dn * n, axis=-1, keepdims=True))
        small_ref[...] += _set_rows((8, D), {0: jnp.sum(du * n, axis=0, keepdims=True)})

    return pl.pallas_call(
        body, name="dgrad_norm_bwd", grid=(S // tm,),
        in_specs=[pl.BlockSpec((4, tm, D), lambda i: (0, i, 0)), pl.BlockSpec((3, tm, D), lambda i: (0, i, 0)),
                  pl.BlockSpec((3, tm, D), lambda i: (0, i, 0)), VMEM_SPEC, tile, tile,
                  pl.BlockSpec((1, D), lambda i: (0, 0))],
        out_specs=(tile, pl.BlockSpec((8, D), lambda i: (0, 0))),
        out_shape=(jax.ShapeDtypeStruct((S, D), F32), jax.ShapeDtypeStruct((8, D), F32)),
        compiler_params=_params(1),
    )(d_conv, d_attn, d_mid, w_all, x2, dh, norm_g)


def _adamw_math(w, g, m, v):
    m = ADAM_B1 * m + (1.0 - ADAM_B1) * g
    v = ADAM_B2 * v + (1.0 - ADAM_B2) * (g * g)
    m_hat = m / (1.0 - ADAM_B1 ** ADAM_STEP)
    v_hat = v / (1.0 - ADAM_B2 ** ADAM_STEP)
    delta = -ADAM_LR * (m_hat / (jnp.sqrt(v_hat) + ADAM_EPS) + ADAM_WD * w)
    return delta, m, v


def _sum_adamw(parts, w, m, v, tm, name):
    R, C = w.shape
    tile = pl.BlockSpec((tm, C), lambda i: (i, 0))

    def body(p_ref, w_ref, m_ref, v_ref, g_out, d_out, m_out, v_out):
        g = p_ref[0].astype(F32)
        for s in range(1, N_DEV):
            g = g + p_ref[s].astype(F32)
        g_out[...] = g
        d_out[...], m_out[...], v_out[...] = _adamw_math(w_ref[...], g, m_ref[...], v_ref[...])

    shape = jax.ShapeDtypeStruct((R, C), F32)
    return pl.pallas_call(
        body, name=name, grid=(R // tm,),
        in_specs=[pl.BlockSpec((N_DEV, tm, C), lambda i: (0, i, 0)), tile, tile, tile],
        out_specs=(tile, tile, tile, tile), out_shape=(shape, shape, shape, shape),
        compiler_params=_params(1),
    )(parts, w, m, v)


def _adamw(g, w, m, v, name):
    def body(g_ref, w_ref, m_ref, v_ref, d_out, m_out, v_out):
        d_out[...], m_out[...], v_out[...] = _adamw_math(w_ref[...], g_ref[...], m_ref[...], v_ref[...])

    shape = jax.ShapeDtypeStruct(w.shape, F32)
    return pl.pallas_call(
        body, name=name, in_specs=[VMEM_SPEC] * 4, out_specs=(VMEM_SPEC,) * 3, out_shape=(shape, shape, shape),
    )(g, w, m, v)


def _alibi_slopes():
    return jnp.exp2(-8.0 * jnp.arange(1, N_HEADS + 1, dtype=F32) / N_HEADS)


def _local_step(x2, target, norm_g, b_merge, final_g, w_all, w3, cw8):
    slopes = _alibi_slopes()
    proj, ut = _norm_proj(x2, norm_g, w_all)
    yc_in = _conv_fwd(proj, cw8)
    fwd = [_attn_fwd(proj, slopes, d) for d in DILATIONS]
    o, lse, ya_in = _attn_combine([f[0] for f in fwd], [f[1] for f in fwd], proj)
    dh, d_mid, do, dyc_in, g_3, small_mid = _mid(yc_in, ya_in, proj, o, x2, target, b_merge, final_g, w3)
    d_conv, small_conv = _conv_bwd(dyc_in, proj, cw8)
    d_attn = _attn_sum([_attn_bwd(proj, slopes, do, o, lse, d) for d in DILATIONS])
    g_in = _wgrad_in(ut, d_conv, d_attn, d_mid)
    grad_x, small_norm = _dgrad_norm_bwd(d_conv, d_attn, d_mid, w_all, x2, dh, norm_g)
    return grad_x, g_in, g_3, small_mid, small_conv, small_norm


def kernel(x, norm_g, w_in, b_merge, conv_w, w_out_conv, w_out_attn, w_o, final_g, loss_target, m_norm_g, m_w_in, m_b_merge, m_conv_w, m_w_out_conv, m_w_out_attn, m_w_o, m_final_g, v_norm_g, v_w_in, v_b_merge, v_conv_w, v_w_out_conv, v_w_out_attn, v_w_o, v_final_g):
    me = 4 * lax.axis_index("x") + 2 * lax.axis_index("y") + lax.axis_index("c")
    stack3 = lambda a, b, c: jnp.concatenate([a, b, c], axis=0)
    pad8 = lambda a: jnp.pad(a, ((0, 8 - a.shape[0]), (0, 0)))

    w3_shard = stack3(w_out_conv, w_out_attn, w_o)
    w_all, w3_all, cw_all = _allgather_weights(w_in[0], w3_shard, pad8(conv_w[0]))
    w3 = jnp.transpose(w3_all, (1, 0, 2, 3)).reshape(3, D, D)
    cw8 = jnp.transpose(cw_all, (1, 0, 2)).reshape(8, D)

    final_g2 = final_g.reshape(1, D)
    grad_x, g_in, g_3, small_mid, small_conv, small_norm = _local_step(
        x[0], loss_target[0], norm_g, b_merge, final_g2, w_all, w3, cw8)

    r_in, r_3 = _exchange_grads(g_in, g_3)
    small = _allreduce_small(small_mid, small_conv, small_norm)

    g_w_in, d_w_in, nm_w_in, nv_w_in = _sum_adamw(r_in, w_in[0], m_w_in[0], v_w_in[0], 128, "adamw_w_in")
    g_w3, d_w3, nm_w3, nv_w3 = _sum_adamw(
        r_3.reshape(N_DEV, 3 * ROW_SHARD, D), w3_shard.reshape(3 * ROW_SHARD, D),
        stack3(m_w_out_conv, m_w_out_attn, m_w_o).reshape(3 * ROW_SHARD, D),
        stack3(v_w_out_conv, v_w_out_attn, v_w_o).reshape(3 * ROW_SHARD, D), ROW_SHARD, "adamw_w3")

    def pack(ng, bm, fg):
        return pad8(jnp.concatenate([ng, bm.reshape(2, D), fg.reshape(1, D)], axis=0))

    d_s, nm_s, nv_s = _adamw(small, pack(norm_g, b_merge, final_g), pack(m_norm_g, m_b_merge, m_final_g),
                             pack(v_norm_g, v_b_merge, v_final_g), "adamw_small")
    g_cw = lax.dynamic_slice(small, (4, me * ROW_SHARD), (3, ROW_SHARD))
    d_cw, nm_cw, nv_cw = _adamw(g_cw, conv_w[0], m_conv_w[0], v_conv_w[0], "adamw_conv_w")

    loss = small[7, 0]
    split3 = lambda t: tuple(t[a * ROW_SHARD:(a + 1) * ROW_SHARD][None] for a in range(3))
    unpack = lambda t: (t[0:1], t[1:3].reshape(1, 2 * D), t[3])

    def leaves(in_, small_, cw_, w3_):
        ng, bm, fg = unpack(small_)
        wc, wa, wo = split3(w3_)
        return (ng, in_[None], bm, cw_[None], wc, wa, wo, fg)

    return (loss, grad_x[None],
            *leaves(g_w_in, small, g_cw, g_w3),
            *leaves(d_w_in, d_s, d_cw, d_w3),
            *leaves(nm_w_in, nm_s, nm_cw, nm_w3),
            *leaves(nv_w_in, nv_s, nv_cw, nv_w3))
```

```python
import functools

import jax
import jax.numpy as jnp
from jax import lax
from jax.experimental import pallas as pl
from jax.experimental.pallas import tpu as pltpu

D = 1024
N_HEADS = 16
HEAD_DIM = 64
N_SEG = 10
IN_COLS = N_SEG * D
N_DEV = 8
W_IN_SHARD = IN_COLS // N_DEV
ROW_SHARD = D // N_DEV
QB = 128
DILATIONS = (1, 4, 16)
EPS = 1e-6
NEG = -1e30
BF = jnp.bfloat16
F32 = jnp.float32
MESH = pl.DeviceIdType.MESH

ADAM_LR = 0.001
ADAM_B1 = 0.9
ADAM_B2 = 0.999
ADAM_EPS = 1e-08
ADAM_WD = 0.01
ADAM_STEP = 10

V7X_VMEM_BYTES = 64 * 1024 * 1024
VMEM_LIMIT = V7X_VMEM_BYTES - 8 * 1024 * 1024
ROW_TILE = 256

VMEM_SPEC = pl.BlockSpec(memory_space=pltpu.VMEM)
ANY_SPEC = pl.BlockSpec(memory_space=pl.ANY)
SMEM_SPEC = pl.BlockSpec(memory_space=pltpu.SMEM)


def _params(n_grid_axes, vmem=VMEM_LIMIT):
    return pltpu.CompilerParams(dimension_semantics=("arbitrary",) * n_grid_axes, vmem_limit_bytes=vmem)


def _dot(a, b):
    return jnp.dot(a, b, preferred_element_type=F32)


def _dot_nt(a, b):
    return lax.dot_general(a, b, (((1,), (1,)), ((), ())), preferred_element_type=F32)


def _dot_tn(a, b):
    return lax.dot_general(a, b, (((0,), (0,)), ((), ())), preferred_element_type=F32)


def _sigmoid(z):
    return 1.0 / (1.0 + jnp.exp(-z))


def _my_place():
    x, y, c = lax.axis_index("x"), lax.axis_index("y"), lax.axis_index("c")
    return x, y, c, 4 * x + 2 * y + c


def _peers(x, y, c):
    out = []
    for k in range(1, N_DEV):
        px = 1 - x if k & 4 else x
        py = 1 - y if k & 2 else y
        pc = 1 - c if k & 1 else c
        out.append(((px, py, pc), 4 * px + 2 * py + pc))
    return out


def _allgather_weights(w_in, w3, cw):
    def body(w_in_ref, w3_ref, cw_ref, o_in, o_3, o_cw, in_bf, w3_bf, send_sems, recv_sems, local_sems):
        x, y, c, me = _my_place()

        def cast_rows(i, carry):
            r = pl.multiple_of(i * 128, 128)
            in_bf[pl.ds(r, 128), :] = w_in_ref[pl.ds(r, 128), :].astype(BF)
            return carry

        lax.fori_loop(0, D // 128, cast_rows, 0)
        for a in range(3):
            w3_bf[a] = w3_ref[a].astype(BF)

        srcs = (in_bf, w3_bf, cw_ref)
        outs = (o_in, o_3, o_cw)
        local = [pltpu.make_async_copy(srcs[a], outs[a].at[me], local_sems.at[a]) for a in range(3)]
        for cp in local:
            cp.start()
        remote = []
        for k, (peer, _) in enumerate(_peers(x, y, c)):
            for a in range(3):
                cp = pltpu.make_async_remote_copy(
                    src_ref=srcs[a], dst_ref=outs[a].at[me], send_sem=send_sems.at[k, a],
                    recv_sem=recv_sems.at[k, a], device_id=peer, device_id_type=MESH)
                cp.start()
                remote.append(cp)
        for cp in remote:
            cp.wait()
        for cp in local:
            cp.wait()

    return pl.pallas_call(
        body, name="allgather_weights",
        out_shape=(jax.ShapeDtypeStruct((N_DEV, D, W_IN_SHARD), BF),
                   jax.ShapeDtypeStruct((N_DEV, 3, ROW_SHARD, D), BF),
                   jax.ShapeDtypeStruct((N_DEV, 8, 128), F32)),
        in_specs=[VMEM_SPEC, VMEM_SPEC, VMEM_SPEC],
        out_specs=(ANY_SPEC, ANY_SPEC, ANY_SPEC),
        scratch_shapes=[pltpu.VMEM((D, W_IN_SHARD), BF), pltpu.VMEM((3, ROW_SHARD, D), BF),
                        pltpu.SemaphoreType.DMA((N_DEV - 1, 3)), pltpu.SemaphoreType.DMA((N_DEV - 1, 3)),
                        pltpu.SemaphoreType.DMA((3,))],
        compiler_params=pltpu.CompilerParams(vmem_limit_bytes=VMEM_LIMIT),
    )(w_in, w3, cw)


def _exchange_grads(g_in, g_3):
    def body(g_in_ref, g_3_ref, r_in, r_3, send_sems, recv_sems, local_sems):
        x, y, c, me = _my_place()
        srcs = (g_in_ref, g_3_ref)
        outs = (r_in, r_3)
        local = [pltpu.make_async_copy(srcs[a].at[me], outs[a].at[me], local_sems.at[a]) for a in range(2)]
        for cp in local:
            cp.start()
        remote = []
        for k, (peer, peer_idx) in enumerate(_peers(x, y, c)):
            for a in range(2):
                cp = pltpu.make_async_remote_copy(
                    src_ref=srcs[a].at[peer_idx], dst_ref=outs[a].at[me], send_sem=send_sems.at[k, a],
                    recv_sem=recv_sems.at[k, a], device_id=peer, device_id_type=MESH)
                cp.start()
                remote.append(cp)
        for cp in remote:
            cp.wait()
        for cp in local:
            cp.wait()

    return pl.pallas_call(
        body, name="exchange_grads",
        out_shape=(jax.ShapeDtypeStruct(g_in.shape, BF), jax.ShapeDtypeStruct(g_3.shape, BF)),
        in_specs=[ANY_SPEC, ANY_SPEC], out_specs=(ANY_SPEC, ANY_SPEC),
        scratch_shapes=[pltpu.SemaphoreType.DMA((N_DEV - 1, 2)), pltpu.SemaphoreType.DMA((N_DEV - 1, 2)),
                        pltpu.SemaphoreType.DMA((2,))],
    )(g_in, g_3)


def _allreduce_small(p_mid, p_conv, p_norm):
    def body(a_ref, b_ref, c_ref, out_ref, mine, gathered, send_sems, recv_sems):
        x, y, c, me = _my_place()
        mine[...] = a_ref[...] + b_ref[...] + c_ref[...]
        gathered[me] = mine[...]
        remote = []
        for k, (peer, _) in enumerate(_peers(x, y, c)):
            cp = pltpu.make_async_remote_copy(
                src_ref=mine, dst_ref=gathered.at[me], send_sem=send_sems.at[k], recv_sem=recv_sems.at[k],
                device_id=peer, device_id_type=MESH)
            cp.start()
            remote.append(cp)
        for cp in remote:
            cp.wait()
        total = gathered[0]
        for s in range(1, N_DEV):
            total = total + gathered[s]
        out_ref[...] = total

    return pl.pallas_call(
        body, name="allreduce_small",
        out_shape=jax.ShapeDtypeStruct((8, D), F32),
        in_specs=[VMEM_SPEC, VMEM_SPEC, VMEM_SPEC], out_specs=VMEM_SPEC,
        scratch_shapes=[pltpu.VMEM((8, D), F32), pltpu.VMEM((N_DEV, 8, D), F32),
                        pltpu.SemaphoreType.DMA((N_DEV - 1,)), pltpu.SemaphoreType.DMA((N_DEV - 1,))],
    )(p_mid, p_conv, p_norm)


def _proj_pieces():
    cuts = sorted(set(range(0, IN_COLS + 1, D)) | set(range(0, IN_COLS + 1, W_IN_SHARD)))
    return [(lo // D, lo % D, lo // W_IN_SHARD, lo % W_IN_SHARD, hi - lo) for lo, hi in zip(cuts[:-1], cuts[1:])]


def _seg_dest(seg):
    if seg < 4:
        return 0, seg * D
    if seg < 7:
        return 1, (seg - 4) * D
    return 0, (seg - 3) * D


def _norm_proj(x2, norm_g, w_all):
    S = x2.shape[0]
    tm = ROW_TILE
    pieces = _proj_pieces()

    def body(x_ref, g_ref, w_ref, pa_ref, qkv_ref, ut_ref):
        xv = x_ref[...]
        r = lax.rsqrt(jnp.mean(xv * xv, axis=-1, keepdims=True) + EPS)
        u = xv * r * g_ref[...]
        ut_ref[...] = u.T.astype(BF)
        ub = u.astype(BF)
        for j in range(N_DEV):
            res = _dot(ub, w_ref[j])
            for seg, sc, p, pc, width in pieces:
                if p != j:
                    continue
                which, col = _seg_dest(seg)
                if which == 0:
                    pa_ref[:, col + sc:col + sc + width] = res[:, pc:pc + width].astype(BF)
                else:
                    qkv_ref[:, col + sc:col + sc + width] = res[:, pc:pc + width]

    return pl.pallas_call(
        body, name="norm_proj", grid=(S // tm,),
        in_specs=[pl.BlockSpec((tm, D), lambda i: (i, 0)), pl.BlockSpec((1, D), lambda i: (0, 0)), VMEM_SPEC],
        out_specs=(pl.BlockSpec((tm, 7 * D), lambda i: (i, 0)), pl.BlockSpec((tm, 3 * D), lambda i: (i, 0)),
                   pl.BlockSpec((D, tm), lambda i: (0, i))),
        out_shape=(jax.ShapeDtypeStruct((S, 7 * D), BF), jax.ShapeDtypeStruct((S, 3 * D), F32),
                   jax.ShapeDtypeStruct((D, S), BF)),
        compiler_params=_params(1),
    )(x2, norm_g, w_all)


CONV_TM, CONV_TC = 256, 512
HALO = 16


def _conv_fwd(pa, cw8):
    S = pa.shape[0]
    tm, tc = CONV_TM, CONV_TC
    nct = D // tc

    def seg(s):
        return pl.BlockSpec((tm, tc), lambda i, j, s=s: (i, s * nct + j))

    def halo_before(s):
        return pl.BlockSpec((HALO, tc), lambda i, j, s=s: (jnp.maximum(i * (tm // HALO) - 1, 0), s * nct + j))

    def body(xc, bg, cg, zc, xch, cgh, cw, out):
        i = pl.program_id(0)
        a = cg[...].astype(F32) * xc[...].astype(F32)
        ah = cgh[...].astype(F32) * xch[...].astype(F32)
        ah = jnp.where(i > 0, ah, 0.0)
        row = lax.broadcasted_iota(jnp.int32, (tm, tc), 0)
        a1 = jnp.where(row == 0, ah[HALO - 1:HALO, :], pltpu.roll(a, 1, 0))
        a2 = jnp.where(row == 0, ah[HALO - 2:HALO - 1, :],
                       jnp.where(row == 1, ah[HALO - 1:HALO, :], pltpu.roll(a, 2, 0)))
        w = cw[...]
        conv = w[0:1, :] * a2 + w[1:2, :] * a1 + w[2:3, :] * a
        z = zc[...].astype(F32)
        out[...] = (z * _sigmoid(z) * bg[...].astype(F32) * conv).astype(BF)

    return pl.pallas_call(
        body, name="conv_fwd", grid=(S // tm, nct),
        in_specs=[seg(0), seg(1), seg(2), seg(3), halo_before(0), halo_before(2),
                  pl.BlockSpec((8, tc), lambda i, j: (0, j))],
        out_specs=pl.BlockSpec((tm, tc), lambda i, j: (i, j)),
        out_shape=jax.ShapeDtypeStruct((S, D), BF),
        compiler_params=_params(2),
    )(pa, pa, pa, pa, pa, pa, cw8)


ATT_UNROLL = 2


def _fold_masks():
    row = lax.broadcasted_iota(jnp.int32, (QB, QB), 0)
    lane = lax.broadcasted_iota(jnp.int32, (QB, QB), 1)
    tri_le = lane <= row
    dist = jnp.where(tri_le, row - lane, row - lane + QB).astype(F32)
    return tri_le, lane == row, dist, lane < HEAD_DIM


def _block_rows(b, d, S):
    nb = S // (QB * d)
    r, n = b // nb, b % nb
    cur0 = r + n * (QB * d)
    prev0 = r + jnp.maximum(n - 1, 0) * (QB * d)
    if d == 1:
        return n, pl.ds(pl.multiple_of(cur0, QB), QB), pl.ds(pl.multiple_of(prev0, QB), QB)
    return n, pl.ds(cur0, QB, stride=d), pl.ds(prev0, QB, stride=d)


def _head_col(t, a):
    return t[:, a * HEAD_DIM:a * HEAD_DIM + 1]


def _attn_fwd(qkv, pa, slopes):
    S = qkv.shape[0]
    hpr = D // 128
    n_blocks = S // QB

    def body(sl_ref, q_ref, k_ref, v_ref, za_ref, o_ref, lse_ref, ya_ref, acc, m_s, l_s):
        hp = pl.program_id(0)
        tri_le, diag, dist, low = _fold_masks()
        low_b = low.astype(F32).astype(BF)
        high_b = 1.0 - low_b
        m_s[...] = jnp.full(m_s.shape, NEG, F32)
        l_s[...] = jnp.zeros(l_s.shape, F32)
        acc[...] = jnp.zeros(acc.shape, F32)

        for d in DILATIONS:
            slope = [sl_ref[2 * hp + a] * float(d) for a in range(2)]
            bias = [slope[a] * dist for a in range(2)]

            def block(b, d=d, slope=slope, bias=bias):
                n, cur, prev = _block_rows(b, d, S)
                has_prev = n > 0
                valid = jnp.logical_or(tri_le, has_prev)
                q2 = (q_ref[cur, :] * 0.125).astype(BF)
                qs = jnp.concatenate([q2 * low_b, q2 * high_b], axis=0)
                vp = v_ref[prev, :]
                kcat = jnp.concatenate([k_ref[prev, :], k_ref[cur, :]], axis=0).astype(BF)
                vcat = jnp.concatenate([vp, v_ref[cur, :]], axis=0).astype(BF)
                s2 = _dot_nt(qs, kcat)
                m_old2, l_old2 = m_s[cur, :], l_s[cur, :]
                p_rows, m_h, l_h, alpha_h, pe_h = [], [], [], [], []
                for a in range(2):
                    sp, sc = s2[a * QB:(a + 1) * QB, :QB], s2[a * QB:(a + 1) * QB, QB:]
                    comb = jnp.where(valid, jnp.where(tri_le, sc, sp) - bias[a], NEG)
                    e = jnp.sum(jnp.where(diag, sp, 0.0), axis=-1, keepdims=True) - slope[a] * float(QB)
                    e = jnp.where(has_prev, e, NEG)
                    m_old = _head_col(m_old2, a)
                    m_new = jnp.maximum(jnp.maximum(m_old, jnp.max(comb, axis=-1, keepdims=True)), e)
                    p = jnp.exp(comb - m_new)
                    pe = jnp.exp(e - m_new)
                    alpha = jnp.exp(m_old - m_new)
                    l_h.append(alpha * _head_col(l_old2, a) + jnp.sum(p, axis=-1, keepdims=True) + pe)
                    p_rows.append(jnp.concatenate([jnp.where(tri_le, 0.0, p).astype(BF),
                                                   jnp.where(tri_le, p, 0.0).astype(BF)], axis=1))
                    m_h.append(m_new)
                    alpha_h.append(alpha)
                    pe_h.append(pe)
                pv = _dot(jnp.concatenate(p_rows, axis=0), vcat)
                acc[cur, :] = (jnp.where(low, alpha_h[0], alpha_h[1]) * acc[cur, :]
                               + jnp.where(low, pv[:QB], pv[QB:]) + jnp.where(low, pe_h[0], pe_h[1]) * vp)
                m_s[cur, :] = jnp.where(low, m_h[0], m_h[1])
                l_s[cur, :] = jnp.where(low, l_h[0], l_h[1])

            def several(it, carry, block=block):
                for u in range(ATT_UNROLL):
                    block(it * ATT_UNROLL + u)
                return carry

            lax.fori_loop(0, n_blocks // ATT_UNROLL, several, 0)

        def finish(i, carry):
            rows = pl.ds(pl.multiple_of(i * QB, QB), QB)
            l = l_s[rows, :]
            o = acc[rows, :] / l
            o_ref[rows, :] = o
            lse_ref[rows, :] = m_s[rows, :] + jnp.log(l)
            z = za_ref[rows, :].astype(F32)
            ya_ref[rows, :] = (z * _sigmoid(z) * o).astype(BF)
            return carry

        lax.fori_loop(0, n_blocks, finish, 0)

    col = lambda s: pl.BlockSpec((S, 128), lambda h, s=s: (0, s * hpr + h))
    return pl.pallas_call(
        body, name="attn_fwd", grid=(hpr,),
        in_specs=[SMEM_SPEC, col(0), col(1), col(2), col(4)],
        out_specs=(col(0), col(0), col(0)),
        out_shape=(jax.ShapeDtypeStruct((S, D), F32), jax.ShapeDtypeStruct((S, D), F32),
                   jax.ShapeDtypeStruct((S, D), BF)),
        scratch_shapes=[pltpu.VMEM((S, 128), F32)] * 3,
        compiler_params=_params(1),
    )(slopes, qkv, qkv, qkv, pa)


def _set_rows(shape, rows):
    idx = lax.broadcasted_iota(jnp.int32, shape, 0)
    out = jnp.zeros(shape, F32)
    for r, val in rows.items():
        out = out + jnp.where(idx == r, val, 0.0)
    return out


def _mid(yc_in, ya_in, pa, o, x2, target, b_merge, final_g, w3):
    S = x2.shape[0]
    tm = ROW_TILE
    nsteps = S // tm
    tile = pl.BlockSpec((tm, D), lambda i: (i, 0))

    def body(yc_ref, ya_ref, za_ref, gcp_ref, gap_ref, o_ref, x_ref, t_ref, b_ref, fg_ref, w_ref,
             dh_ref, dmid_ref, do_ref, dyc_ref, gw_ref, small_ref, acc, stage):
        i = pl.program_id(0)

        @pl.when(i == 0)
        def _():
            acc[...] = jnp.zeros_like(acc)
            small_ref[...] = jnp.zeros_like(small_ref)

        wc, wa, wo = w_ref[0], w_ref[1], w_ref[2]
        yc_in_b, ya_in_b = yc_ref[...], ya_ref[...]
        yc = _dot(yc_in_b, wc)
        ya = _dot(ya_in_b, wa)
        b = b_ref[...]
        gc = _sigmoid(gcp_ref[...].astype(F32) + b[:, :D])
        ga = _sigmoid(gap_ref[...].astype(F32) + b[:, D:])
        merged = gc * yc + ga * ya
        merged_b = merged.astype(BF)
        h = x_ref[...] + _dot(merged_b, wo)
        r2 = lax.rsqrt(jnp.mean(h * h, axis=-1, keepdims=True) + EPS)
        n = h * r2
        fg = fg_ref[...]
        err = n * fg - t_ref[...]
        loss = 0.5 * jnp.sum(jnp.sum(err * err, axis=-1, keepdims=True) / D, axis=0, keepdims=True)
        dy = err / D
        g_fg = jnp.sum(dy * n, axis=0, keepdims=True)
        dn = dy * fg
        dh = r2 * (dn - n * jnp.mean(dn * n, axis=-1, keepdims=True))
        dh_ref[...] = dh
        dh_b = dh.astype(BF)
        dmerged = _dot_nt(dh_b, wo)
        acc[2] += _dot(merged.T.astype(BF), dh_b)
        dyc = (dmerged * gc).astype(BF)
        dya = (dmerged * ga).astype(BF)
        dgcp = dmerged * yc * gc * (1.0 - gc)
        dgap = dmerged * ya * ga * (1.0 - ga)
        dmid_ref[1] = dgcp.astype(BF)
        dmid_ref[2] = dgap.astype(BF)
        acc[0] += _dot(yc_in_b.astype(F32).T.astype(BF), dyc)
        acc[1] += _dot(ya_in_b.astype(F32).T.astype(BF), dya)
        dyc_ref[...] = _dot_nt(dyc, wc).astype(BF)
        dya_in = _dot_nt(dya, wa)
        z = za_ref[...].astype(F32)
        sg = _sigmoid(z)
        do_ref[...] = dya_in * (z * sg)
        dmid_ref[0] = (dya_in * o_ref[...] * (sg * (1.0 + z * (1.0 - sg)))).astype(BF)
        small_ref[...] += _set_rows((8, D), {
            1: jnp.sum(dgcp, axis=0, keepdims=True), 2: jnp.sum(dgap, axis=0, keepdims=True),
            3: g_fg, 7: jnp.broadcast_to(loss, (1, D))})

        @pl.when(i == nsteps - 1)
        def _():
            for p in range(N_DEV):
                for a in range(3):
                    stage[...] = acc[a, p * ROW_SHARD:(p + 1) * ROW_SHARD, :].astype(BF)
                    pltpu.sync_copy(stage, gw_ref.at[p, a])

    return pl.pallas_call(
        body, name="mid", grid=(nsteps,),
        in_specs=[tile, tile, pl.BlockSpec((tm, D), lambda i: (i, 4)), pl.BlockSpec((tm, D), lambda i: (i, 5)),
                  pl.BlockSpec((tm, D), lambda i: (i, 6)), tile, tile, tile,
                  pl.BlockSpec((1, 2 * D), lambda i: (0, 0)), pl.BlockSpec((1, D), lambda i: (0, 0)), VMEM_SPEC],
        out_specs=(tile, pl.BlockSpec((3, tm, D), lambda i: (0, i, 0)), tile, tile,
                   ANY_SPEC, pl.BlockSpec((8, D), lambda i: (0, 0))),
        out_shape=(jax.ShapeDtypeStruct((S, D), F32), jax.ShapeDtypeStruct((3, S, D), BF),
                   jax.ShapeDtypeStruct((S, D), F32), jax.ShapeDtypeStruct((S, D), BF),
                   jax.ShapeDtypeStruct((N_DEV, 3, ROW_SHARD, D), BF), jax.ShapeDtypeStruct((8, D), F32)),
        scratch_shapes=[pltpu.VMEM((3, D, D), F32), pltpu.VMEM((ROW_SHARD, D), BF)],
        compiler_params=_params(1),
    )(yc_in, ya_in, pa, pa, pa, o, x2, target, b_merge, final_g, w3)


def _conv_bwd(dyc_in, pa, cw8):
    S = pa.shape[0]
    tm, tc = CONV_TM, CONV_TC
    nct = D // tc
    nrt = S // tm
    last_halo = S // HALO - 1

    def seg(s):
        return pl.BlockSpec((tm, tc), lambda j, i, s=s: (i, s * nct + j))

    def halo_before(s):
        return pl.BlockSpec((HALO, tc), lambda j, i, s=s: (jnp.maximum(i * (tm // HALO) - 1, 0), s * nct + j))

    def halo_after(s):
        return pl.BlockSpec((HALO, tc), lambda j, i, s=s: (jnp.minimum((i + 1) * (tm // HALO), last_halo), s * nct + j))

    def body(dy, xc, bg, cg, zc, xch, cgh, dyn, bgn, zcn, cw, dout, gcw):
        i = pl.program_id(1)

        @pl.when(i == 0)
        def _():
            gcw[...] = jnp.zeros_like(gcw)

        xcv, cgv = xc[...].astype(F32), cg[...].astype(F32)
        a = cgv * xcv
        ah = jnp.where(i > 0, cgh[...].astype(F32) * xch[...].astype(F32), 0.0)
        row = lax.broadcasted_iota(jnp.int32, (tm, tc), 0)
        a1 = jnp.where(row == 0, ah[HALO - 1:HALO, :], pltpu.roll(a, 1, 0))
        a2 = jnp.where(row == 0, ah[HALO - 2:HALO - 1, :],
                       jnp.where(row == 1, ah[HALO - 1:HALO, :], pltpu.roll(a, 2, 0)))
        w = cw[...]
        conv = w[0:1, :] * a2 + w[1:2, :] * a1 + w[2:3, :] * a
        z = zc[...].astype(F32)
        sg = _sigmoid(z)
        silu = z * sg
        bgv = bg[...].astype(F32)
        dyv = dy[...].astype(F32)
        dout[3] = (dyv * bgv * conv * (sg * (1.0 + z * (1.0 - sg)))).astype(BF)
        dout[1] = (dyv * silu * conv).astype(BF)
        dc = dyv * silu * bgv
        zn = zcn[...].astype(F32)
        dcn = dyn[...].astype(F32) * (zn * _sigmoid(zn)) * bgn[...].astype(F32)
        dcn = jnp.where(i < nrt - 1, dcn, 0.0)
        dc1 = jnp.where(row == tm - 1, dcn[0:1, :], pltpu.roll(dc, tm - 1, 0))
        dc2 = jnp.where(row == tm - 1, dcn[1:2, :],
                        jnp.where(row == tm - 2, dcn[0:1, :], pltpu.roll(dc, tm - 2, 0)))
        da = w[2:3, :] * dc + w[1:2, :] * dc1 + w[0:1, :] * dc2
        dout[2] = (da * xcv).astype(BF)
        dout[0] = (da * cgv).astype(BF)
        gcw[...] += _set_rows((8, tc), {
            4: jnp.sum(dc * a2, axis=0, keepdims=True), 5: jnp.sum(dc * a1, axis=0, keepdims=True),
            6: jnp.sum(dc * a, axis=0, keepdims=True)})

    return pl.pallas_call(
        body, name="conv_bwd", grid=(nct, nrt),
        in_specs=[pl.BlockSpec((tm, tc), lambda j, i: (i, j)), seg(0), seg(1), seg(2), seg(3),
                  halo_before(0), halo_before(2),
                  pl.BlockSpec((HALO, tc), lambda j, i: (jnp.minimum((i + 1) * (tm // HALO), last_halo), j)),
                  halo_after(1), halo_after(3), pl.BlockSpec((8, tc), lambda j, i: (0, j))],
        out_specs=(pl.BlockSpec((4, tm, tc), lambda j, i: (0, i, j)), pl.BlockSpec((8, tc), lambda j, i: (0, j))),
        out_shape=(jax.ShapeDtypeStruct((4, S, D), BF), jax.ShapeDtypeStruct((8, D), F32)),
        compiler_params=_params(2),
    )(dyc_in, pa, pa, pa, pa, pa, pa, dyc_in, pa, pa, cw8)


def _attn_bwd(qkv, slopes, do, o, lse):
    S = qkv.shape[0]
    hpr = D // 128
    n_blocks = S // QB

    def body(sl_ref, q_ref, k_ref, v_ref, do_ref, o_ref, lse_ref, out_ref, dq_s, dk_s, dv_s, dd_s):
        hp = pl.program_id(0)
        tri_le, diag, dist, low = _fold_masks()
        low_b = low.astype(F32).astype(BF)
        high_b = 1.0 - low_b
        dq_s[...] = jnp.zeros(dq_s.shape, F32)
        dk_s[...] = jnp.zeros(dk_s.shape, F32)
        dv_s[...] = jnp.zeros(dv_s.shape, F32)

        def row_dots(i, carry):
            rows = pl.ds(pl.multiple_of(i * QB, QB), QB)
            prod = do_ref[rows, :] * o_ref[rows, :]
            dd_s[rows, :] = jnp.where(low, jnp.sum(jnp.where(low, prod, 0.0), axis=-1, keepdims=True),
                                      jnp.sum(jnp.where(low, 0.0, prod), axis=-1, keepdims=True))
            return carry

        lax.fori_loop(0, n_blocks, row_dots, 0)

        for d in DILATIONS:
            slope = [sl_ref[2 * hp + a] * float(d) for a in range(2)]
            bias = [slope[a] * dist for a in range(2)]

            def block(b, d=d, slope=slope, bias=bias):
                n, cur, prev = _block_rows(b, d, S)
                has_prev = n > 0
                valid = jnp.logical_or(tri_le, has_prev)
                q2f = q_ref[cur, :] * 0.125
                q2 = q2f.astype(BF)
                qs = jnp.concatenate([q2 * low_b, q2 * high_b], axis=0)
                kp, vp = k_ref[prev, :], v_ref[prev, :]
                kcat = jnp.concatenate([kp, k_ref[cur, :]], axis=0).astype(BF)
                vcat = jnp.concatenate([vp, v_ref[cur, :]], axis=0).astype(BF)
                do2f = do_ref[cur, :]
                do2 = do2f.astype(BF)
                dos = jnp.concatenate([do2 * low_b, do2 * high_b], axis=0)
                s2 = _dot_nt(qs, kcat)
                dp2 = _dot_nt(dos, vcat)
                lse2, dd2 = lse_ref[cur, :], dd_s[cur, :]
                p_rows, ds_rows, pe_h, dse_h = [], [], [], []
                for a in range(2):
                    hs = slice(a * QB, (a + 1) * QB)
                    sp, sc = s2[hs, :QB], s2[hs, QB:]
                    dpp, dpc = dp2[hs, :QB], dp2[hs, QB:]
                    lse_col, dd_col = _head_col(lse2, a), _head_col(dd2, a)
                    comb = jnp.where(tri_le, sc, sp) - bias[a]
                    e = jnp.sum(jnp.where(diag, sp, 0.0), axis=-1, keepdims=True) - slope[a] * float(QB)
                    p = jnp.where(valid, jnp.exp(comb - lse_col), 0.0)
                    pe = jnp.where(has_prev, jnp.exp(e - lse_col), 0.0)
                    dpe = jnp.sum(jnp.where(diag, dpp, 0.0), axis=-1, keepdims=True)
                    ds = p * (jnp.where(tri_le, dpc, dpp) - dd_col)
                    dse_h.append(pe * (dpe - dd_col))
                    pe_h.append(pe)
                    p_rows.append(jnp.concatenate([jnp.where(tri_le, 0.0, p).astype(BF),
                                                   jnp.where(tri_le, p, 0.0).astype(BF)], axis=1))
                    ds_rows.append(jnp.concatenate([jnp.where(tri_le, 0.0, ds).astype(BF),
                                                    jnp.where(tri_le, ds, 0.0).astype(BF)], axis=1))
                pst = jnp.concatenate(p_rows, axis=0)
                dst = jnp.concatenate(ds_rows, axis=0)
                pe2 = jnp.where(low, pe_h[0], pe_h[1])
                dse2 = jnp.where(low, dse_h[0], dse_h[1])
                dq = _dot(dst, kcat)
                dq_s[cur, :] += (jnp.where(low, dq[:QB], dq[QB:]) + dse2 * kp) * 0.125
                dk = _dot_tn(dst, qs)
                dv = _dot_tn(pst, dos)
                dk_s[prev, :] += dk[:QB] + dse2 * q2f
                dk_s[cur, :] += dk[QB:]
                dv_s[prev, :] += dv[:QB] + pe2 * do2f
                dv_s[cur, :] += dv[QB:]

            def several(it, carry, block=block):
                for u in range(ATT_UNROLL):
                    block(it * ATT_UNROLL + u)
                return carry

            lax.fori_loop(0, n_blocks // ATT_UNROLL, several, 0)

        def finish(i, carry):
            rows = pl.ds(pl.multiple_of(i * QB, QB), QB)
            out_ref[0, rows, :] = dq_s[rows, :].astype(BF)
            out_ref[1, rows, :] = dk_s[rows, :].astype(BF)
            out_ref[2, rows, :] = dv_s[rows, :].astype(BF)
            return carry

        lax.fori_loop(0, n_blocks, finish, 0)

    col = lambda s: pl.BlockSpec((S, 128), lambda h, s=s: (0, s * hpr + h))
    return pl.pallas_call(
        body, name="attn_bwd", grid=(hpr,),
        in_specs=[SMEM_SPEC, col(0), col(1), col(2), col(0), col(0), col(0)],
        out_specs=pl.BlockSpec((3, S, 128), lambda h: (0, 0, h)),
        out_shape=jax.ShapeDtypeStruct((3, S, D), BF),
        scratch_shapes=[pltpu.VMEM((S, 128), F32)] * 4,
        compiler_params=_params(1),
    )(slopes, qkv, qkv, qkv, do, o, lse)


GROUP_SEGS = (4, 3, 3)
WG_TN = 256


def _wgrad_in(ut, d_conv, d_attn, d_mid):
    S = ut.shape[1]
    tn = WG_TN
    per_seg = D // tn
    n_tiles = IN_COLS // tn
    first = (0, GROUP_SEGS[0] * per_seg, (GROUP_SEGS[0] + GROUP_SEGS[1]) * per_seg, n_tiles)

    def group_spec(g):
        lo, hi = first[g], first[g + 1]

        def index(j):
            jj = jnp.clip(j, lo, hi - 1) - lo
            return (jj // per_seg, 0, jj % per_seg)

        return pl.BlockSpec((1, S, tn), index)

    per_shard = W_IN_SHARD // tn

    def body(ut_ref, a_ref, b_ref, c_ref, out_ref):
        j = pl.program_id(0)
        for g, ref in enumerate((a_ref, b_ref, c_ref)):
            @pl.when(jnp.logical_and(j >= first[g], j < first[g + 1]))
            def _(ref=ref):
                out_ref[0] = _dot(ut_ref[...], ref[0]).astype(BF)

    return pl.pallas_call(
        body, name="wgrad_in", grid=(n_tiles,),
        in_specs=[VMEM_SPEC, group_spec(0), group_spec(1), group_spec(2)],
        out_specs=pl.BlockSpec((1, D, tn), lambda j: (j // per_shard, 0, j % per_shard)),
        out_shape=jax.ShapeDtypeStruct((N_DEV, D, W_IN_SHARD), BF),
        compiler_params=_params(1),
    )(ut, d_conv, d_attn, d_mid)


def _dgrad_norm_bwd(d_conv, d_attn, d_mid, w_all, x2, dh, norm_g):
    S = x2.shape[0]
    tm = ROW_TILE
    tile = pl.BlockSpec((tm, D), lambda i: (i, 0))
    pieces = _proj_pieces()

    def body(a_ref, b_ref, c_ref, w_ref, x_ref, dh_ref, g_ref, gx_ref, small_ref):
        i = pl.program_id(0)

        @pl.when(i == 0)
        def _():
            small_ref[...] = jnp.zeros_like(small_ref)

        groups = (a_ref, b_ref, c_ref)
        du = jnp.zeros((tm, D), F32)
        for s, sc, p, pc, width in pieces:
            g = 0 if s < 4 else (1 if s < 7 else 2)
            local = s - (0, 4, 7)[g]
            du = du + _dot_nt(groups[g][local, :, sc:sc + width], w_ref[p, :, pc:pc + width])
        xv = x_ref[...]
        r = lax.rsqrt(jnp.mean(xv * xv, axis=-1, keepdims=True) + EPS)
        n = xv * r
        dn = du * g_ref[...]
        gx_ref[...] = dh_ref[...] + r * (dn - n * jnp.mean(dn * n, axis=-1, keepdims=True))
        small_ref[...] += _set_rows((8, D), {0: jnp.sum(du * n, axis=0, keepdims=True)})

    return pl.pallas_call(
        body, name="dgrad_norm_bwd", grid=(S // tm,),
        in_specs=[pl.BlockSpec((4, tm, D), lambda i: (0, i, 0)), pl.BlockSpec((3, tm, D), lambda i: (0, i, 0)),
                  pl.BlockSpec((3, tm, D), lambda i: (0, i, 0)), VMEM_SPEC, tile, tile,
                  pl.BlockSpec((1, D), lambda i: (0, 0))],
        out_specs=(tile, pl.BlockSpec((8, D), lambda i: (0, 0))),
        out_shape=(jax.ShapeDtypeStruct((S, D), F32), jax.ShapeDtypeStruct((8, D), F32)),
        compiler_params=_params(1),
    )(d_conv, d_attn, d_mid, w_all, x2, dh, norm_g)


def _adamw_math(w, g, m, v):
    m = ADAM_B1 * m + (1.0 - ADAM_B1) * g
    v = ADAM_B2 * v + (1.0 - ADAM_B2) * (g * g)
    m_hat = m / (1.0 - ADAM_B1 ** ADAM_STEP)
    v_hat = v / (1.0 - ADAM_B2 ** ADAM_STEP)
    delta = -ADAM_LR * (m_hat / (jnp.sqrt(v_hat) + ADAM_EPS) + ADAM_WD * w)
    return delta, m, v


def _sum_adamw(parts, w, m, v, tm, name):
    R, C = w.shape
    tile = pl.BlockSpec((tm, C), lambda i: (i, 0))

    def body(p_ref, w_ref, m_ref, v_ref, g_out, d_out, m_out, v_out):
        g = p_ref[0].astype(F32)
        for s in range(1, N_DEV):
            g = g + p_ref[s].astype(F32)
        g_out[...] = g
        d_out[...], m_out[...], v_out[...] = _adamw_math(w_ref[...], g, m_ref[...], v_ref[...])

    shape = jax.ShapeDtypeStruct((R, C), F32)
    return pl.pallas_call(
        body, name=name, grid=(R // tm,),
        in_specs=[pl.BlockSpec((N_DEV, tm, C), lambda i: (0, i, 0)), tile, tile, tile],
        out_specs=(tile, tile, tile, tile), out_shape=(shape, shape, shape, shape),
        compiler_params=_params(1),
    )(parts, w, m, v)


def _adamw(g, w, m, v, name):
    def body(g_ref, w_ref, m_ref, v_ref, d_out, m_out, v_out):
        d_out[...], m_out[...], v_out[...] = _adamw_math(w_ref[...], g_ref[...], m_ref[...], v_ref[...])

    shape = jax.ShapeDtypeStruct(w.shape, F32)
    return pl.pallas_call(
        body, name=name, in_specs=[VMEM_SPEC] * 4, out_specs=(VMEM_SPEC,) * 3, out_shape=(shape, shape, shape),
    )(g, w, m, v)


def _alibi_slopes():
    return jnp.exp2(-8.0 * jnp.arange(1, N_HEADS + 1, dtype=F32) / N_HEADS)


def _local_step(x2, target, norm_g, b_merge, final_g, w_all, w3, cw8):
    slopes = _alibi_slopes()
    pa, qkv, ut = _norm_proj(x2, norm_g, w_all)
    yc_in = _conv_fwd(pa, cw8)
    o, lse, ya_in = _attn_fwd(qkv, pa, slopes)
    dh, d_mid, do, dyc_in, g_3, small_mid = _mid(yc_in, ya_in, pa, o, x2, target, b_merge, final_g, w3)
    d_conv, small_conv = _conv_bwd(dyc_in, pa, cw8)
    d_attn = _attn_bwd(qkv, slopes, do, o, lse)
    g_in = _wgrad_in(ut, d_conv, d_attn, d_mid)
    grad_x, small_norm = _dgrad_norm_bwd(d_conv, d_attn, d_mid, w_all, x2, dh, norm_g)
    return grad_x, g_in, g_3, small_mid, small_conv, small_norm


def kernel(x, norm_g, w_in, b_merge, conv_w, w_out_conv, w_out_attn, w_o, final_g, loss_target, m_norm_g, m_w_in, m_b_merge, m_conv_w, m_w_out_conv, m_w_out_attn, m_w_o, m_final_g, v_norm_g, v_w_in, v_b_merge, v_conv_w, v_w_out_conv, v_w_out_attn, v_w_o, v_final_g):
    me = 4 * lax.axis_index("x") + 2 * lax.axis_index("y") + lax.axis_index("c")
    stack3 = lambda a, b, c: jnp.concatenate([a, b, c], axis=0)
    pad8 = lambda a: jnp.pad(a, ((0, 8 - a.shape[0]), (0, 0)))

    w3_shard = stack3(w_out_conv, w_out_attn, w_o)
    w_all, w3_all, cw_all = _allgather_weights(w_in[0], w3_shard, pad8(conv_w[0]))
    w3 = jnp.transpose(w3_all, (1, 0, 2, 3)).reshape(3, D, D)
    cw8 = jnp.transpose(cw_all, (1, 0, 2)).reshape(8, D)

    final_g2 = final_g.reshape(1, D)
    grad_x, g_in, g_3, small_mid, small_conv, small_norm = _local_step(
        x[0], loss_target[0], norm_g, b_merge, final_g2, w_all, w3, cw8)

    r_in, r_3 = _exchange_grads(g_in, g_3)
    small = _allreduce_small(small_mid, small_conv, small_norm)

    g_w_in, d_w_in, nm_w_in, nv_w_in = _sum_adamw(r_in, w_in[0], m_w_in[0], v_w_in[0], 128, "adamw_w_in")
    g_w3, d_w3, nm_w3, nv_w3 = _sum_adamw(
        r_3.reshape(N_DEV, 3 * ROW_SHARD, D), w3_shard.reshape(3 * ROW_SHARD, D),
        stack3(m_w_out_conv, m_w_out_attn, m_w_o).reshape(3 * ROW_SHARD, D),
        stack3(v_w_out_conv, v_w_out_attn, v_w_o).reshape(3 * ROW_SHARD, D), ROW_SHARD, "adamw_w3")

    def pack(ng, bm, fg):
        return pad8(jnp.concatenate([ng, bm.reshape(2, D), fg.reshape(1, D)], axis=0))

    d_s, nm_s, nv_s = _adamw(small, pack(norm_g, b_merge, final_g), pack(m_norm_g, m_b_merge, m_final_g),
                             pack(v_norm_g, v_b_merge, v_final_g), "adamw_small")
    g_cw = lax.dynamic_slice(small, (4, me * ROW_SHARD), (3, ROW_SHARD))
    d_cw, nm_cw, nv_cw = _adamw(g_cw, conv_w[0], m_conv_w[0], v_conv_w[0], "adamw_conv_w")

    loss = small[7, 0]
    split3 = lambda t: tuple(t[a * ROW_SHARD:(a + 1) * ROW_SHARD][None] for a in range(3))
    unpack = lambda t: (t[0:1], t[1:3].reshape(1, 2 * D), t[3])

    def leaves(in_, small_, cw_, w3_):
        ng, bm, fg = unpack(small_)
        wc, wa, wo = split3(w3_)
        return (ng, in_[None], bm, cw_[None], wc, wa, wo, fg)

    return (loss, grad_x[None],
            *leaves(g_w_in, small, g_cw, g_w3),
            *leaves(d_w_in, d_s, d_cw, d_w3),
            *leaves(nm_w_in, nm_s, nm_cw, nm_w3),
            *leaves(nv_w_in, nv_s, nv_cw, nv_w3))
```

```python
import functools

import jax
import jax.numpy as jnp
from jax import lax
from jax.experimental import pallas as pl
from jax.experimental.pallas import tpu as pltpu

D = 1024
N_HEADS = 16
HEAD_DIM = 64
N_SEG = 10
IN_COLS = N_SEG * D
N_DEV = 8
W_IN_SHARD = IN_COLS // N_DEV
ROW_SHARD = D // N_DEV
QB = 128
DILATIONS = (1, 4, 16)
EPS = 1e-6
NEG = -1e30
BF = jnp.bfloat16
F32 = jnp.float32
MESH = pl.DeviceIdType.MESH

ADAM_LR = 0.001
ADAM_B1 = 0.9
ADAM_B2 = 0.999
ADAM_EPS = 1e-08
ADAM_WD = 0.01
ADAM_STEP = 10

V7X_VMEM_BYTES = 64 * 1024 * 1024
VMEM_LIMIT = V7X_VMEM_BYTES - 8 * 1024 * 1024
ROW_TILE = 256

VMEM_SPEC = pl.BlockSpec(memory_space=pltpu.VMEM)
ANY_SPEC = pl.BlockSpec(memory_space=pl.ANY)
SMEM_SPEC = pl.BlockSpec(memory_space=pltpu.SMEM)


def _params(n_grid_axes, vmem=VMEM_LIMIT):
    return pltpu.CompilerParams(dimension_semantics=("arbitrary",) * n_grid_axes, vmem_limit_bytes=vmem)


def _dot(a, b):
    return jnp.dot(a, b, preferred_element_type=F32)


def _dot_nt(a, b):
    return lax.dot_general(a, b, (((1,), (1,)), ((), ())), preferred_element_type=F32)


def _dot_tn(a, b):
    return lax.dot_general(a, b, (((0,), (0,)), ((), ())), preferred_element_type=F32)


def _sigmoid(z):
    return 1.0 / (1.0 + jnp.exp(-z))


def _my_place():
    x, y, c = lax.axis_index("x"), lax.axis_index("y"), lax.axis_index("c")
    return x, y, c, 4 * x + 2 * y + c


def _peers(x, y, c):
    out = []
    for k in range(1, N_DEV):
        px = 1 - x if k & 4 else x
        py = 1 - y if k & 2 else y
        pc = 1 - c if k & 1 else c
        out.append(((px, py, pc), 4 * px + 2 * py + pc))
    return out


def _allgather_weights(w_in, w3, cw):
    def body(w_in_ref, w3_ref, cw_ref, o_in, o_3, o_cw, in_bf, w3_bf, send_sems, recv_sems, local_sems):
        x, y, c, me = _my_place()

        def cast_rows(i, carry):
            r = pl.multiple_of(i * 128, 128)
            in_bf[pl.ds(r, 128), :] = w_in_ref[pl.ds(r, 128), :].astype(BF)
            return carry

        lax.fori_loop(0, D // 128, cast_rows, 0)
        for a in range(3):
            w3_bf[a] = w3_ref[a].astype(BF)

        srcs = (in_bf, w3_bf, cw_ref)
        outs = (o_in, o_3, o_cw)
        local = [pltpu.make_async_copy(srcs[a], outs[a].at[me], local_sems.at[a]) for a in range(3)]
        for cp in local:
            cp.start()
        remote = []
        for k, (peer, _) in enumerate(_peers(x, y, c)):
            for a in range(3):
                cp = pltpu.make_async_remote_copy(
                    src_ref=srcs[a], dst_ref=outs[a].at[me], send_sem=send_sems.at[k, a],
                    recv_sem=recv_sems.at[k, a], device_id=peer, device_id_type=MESH)
                cp.start()
                remote.append(cp)
        for cp in remote:
            cp.wait()
        for cp in local:
            cp.wait()

    return pl.pallas_call(
        body, name="allgather_weights",
        out_shape=(jax.ShapeDtypeStruct((N_DEV, D, W_IN_SHARD), BF),
                   jax.ShapeDtypeStruct((N_DEV, 3, ROW_SHARD, D), BF),
                   jax.ShapeDtypeStruct((N_DEV, 8, 128), F32)),
        in_specs=[VMEM_SPEC, VMEM_SPEC, VMEM_SPEC],
        out_specs=(ANY_SPEC, ANY_SPEC, ANY_SPEC),
        scratch_shapes=[pltpu.VMEM((D, W_IN_SHARD), BF), pltpu.VMEM((3, ROW_SHARD, D), BF),
                        pltpu.SemaphoreType.DMA((N_DEV - 1, 3)), pltpu.SemaphoreType.DMA((N_DEV - 1, 3)),
                        pltpu.SemaphoreType.DMA((3,))],
        compiler_params=pltpu.CompilerParams(vmem_limit_bytes=VMEM_LIMIT),
    )(w_in, w3, cw)


def _exchange_grads(g_in, g_3):
    def body(g_in_ref, g_3_ref, r_in, r_3, send_sems, recv_sems, local_sems):
        x, y, c, me = _my_place()
        srcs = (g_in_ref, g_3_ref)
        outs = (r_in, r_3)
        local = [pltpu.make_async_copy(srcs[a].at[me], outs[a].at[me], local_sems.at[a]) for a in range(2)]
        for cp in local:
            cp.start()
        remote = []
        for k, (peer, peer_idx) in enumerate(_peers(x, y, c)):
            for a in range(2):
                cp = pltpu.make_async_remote_copy(
                    src_ref=srcs[a].at[peer_idx], dst_ref=outs[a].at[me], send_sem=send_sems.at[k, a],
                    recv_sem=recv_sems.at[k, a], device_id=peer, device_id_type=MESH)
                cp.start()
                remote.append(cp)
        for cp in remote:
            cp.wait()
        for cp in local:
            cp.wait()

    return pl.pallas_call(
        body, name="exchange_grads",
        out_shape=(jax.ShapeDtypeStruct(g_in.shape, BF), jax.ShapeDtypeStruct(g_3.shape, BF)),
        in_specs=[ANY_SPEC, ANY_SPEC], out_specs=(ANY_SPEC, ANY_SPEC),
        scratch_shapes=[pltpu.SemaphoreType.DMA((N_DEV - 1, 2)), pltpu.SemaphoreType.DMA((N_DEV - 1, 2)),
                        pltpu.SemaphoreType.DMA((2,))],
    )(g_in, g_3)


def _allreduce_small(p_mid, p_conv, p_norm):
    def body(a_ref, b_ref, c_ref, out_ref, mine, gathered, send_sems, recv_sems):
        x, y, c, me = _my_place()
        mine[...] = a_ref[...] + b_ref[...] + c_ref[...]
        gathered[me] = mine[...]
        remote = []
        for k, (peer, _) in enumerate(_peers(x, y, c)):
            cp = pltpu.make_async_remote_copy(
                src_ref=mine, dst_ref=gathered.at[me], send_sem=send_sems.at[k], recv_sem=recv_sems.at[k],
                device_id=peer, device_id_type=MESH)
            cp.start()
            remote.append(cp)
        for cp in remote:
            cp.wait()
        total = gathered[0]
        for s in range(1, N_DEV):
            total = total + gathered[s]
        out_ref[...] = total

    return pl.pallas_call(
        body, name="allreduce_small",
        out_shape=jax.ShapeDtypeStruct((8, D), F32),
        in_specs=[VMEM_SPEC, VMEM_SPEC, VMEM_SPEC], out_specs=VMEM_SPEC,
        scratch_shapes=[pltpu.VMEM((8, D), F32), pltpu.VMEM((N_DEV, 8, D), F32),
                        pltpu.SemaphoreType.DMA((N_DEV - 1,)), pltpu.SemaphoreType.DMA((N_DEV - 1,))],
    )(p_mid, p_conv, p_norm)


def _proj_pieces():
    cuts = sorted(set(range(0, IN_COLS + 1, D)) | set(range(0, IN_COLS + 1, W_IN_SHARD)))
    return [(lo // D, lo % D, lo // W_IN_SHARD, lo % W_IN_SHARD, hi - lo) for lo, hi in zip(cuts[:-1], cuts[1:])]


def _seg_dest(seg):
    if seg < 4:
        return 0, seg * D
    if seg < 7:
        return 1, (seg - 4) * D
    return 0, (seg - 3) * D


def _norm_proj(x2, norm_g, w_all):
    S = x2.shape[0]
    tm = ROW_TILE
    pieces = _proj_pieces()

    def body(x_ref, g_ref, w_ref, pa_ref, qkv_ref, ut_ref):
        xv = x_ref[...]
        r = lax.rsqrt(jnp.mean(xv * xv, axis=-1, keepdims=True) + EPS)
        u = xv * r * g_ref[...]
        ut_ref[...] = u.T.astype(BF)
        ub = u.astype(BF)
        for j in range(N_DEV):
            res = _dot(ub, w_ref[j])
            for seg, sc, p, pc, width in pieces:
                if p != j:
                    continue
                which, col = _seg_dest(seg)
                if which == 0:
                    pa_ref[:, col + sc:col + sc + width] = res[:, pc:pc + width].astype(BF)
                else:
                    qkv_ref[:, col + sc:col + sc + width] = res[:, pc:pc + width]

    return pl.pallas_call(
        body, name="norm_proj", grid=(S // tm,),
        in_specs=[pl.BlockSpec((tm, D), lambda i: (i, 0)), pl.BlockSpec((1, D), lambda i: (0, 0)), VMEM_SPEC],
        out_specs=(pl.BlockSpec((tm, 7 * D), lambda i: (i, 0)), pl.BlockSpec((tm, 3 * D), lambda i: (i, 0)),
                   pl.BlockSpec((D, tm), lambda i: (0, i))),
        out_shape=(jax.ShapeDtypeStruct((S, 7 * D), BF), jax.ShapeDtypeStruct((S, 3 * D), F32),
                   jax.ShapeDtypeStruct((D, S), BF)),
        compiler_params=_params(1),
    )(x2, norm_g, w_all)


CONV_TM, CONV_TC = 256, 512
HALO = 16


def _conv_fwd(pa, cw8):
    S = pa.shape[0]
    tm, tc = CONV_TM, CONV_TC
    nct = D // tc

    def seg(s):
        return pl.BlockSpec((tm, tc), lambda i, j, s=s: (i, s * nct + j))

    def halo_before(s):
        return pl.BlockSpec((HALO, tc), lambda i, j, s=s: (jnp.maximum(i * (tm // HALO) - 1, 0), s * nct + j))

    def body(xc, bg, cg, zc, xch, cgh, cw, out):
        i = pl.program_id(0)
        a = cg[...].astype(F32) * xc[...].astype(F32)
        ah = cgh[...].astype(F32) * xch[...].astype(F32)
        ah = jnp.where(i > 0, ah, 0.0)
        row = lax.broadcasted_iota(jnp.int32, (tm, tc), 0)
        a1 = jnp.where(row == 0, ah[HALO - 1:HALO, :], pltpu.roll(a, 1, 0))
        a2 = jnp.where(row == 0, ah[HALO - 2:HALO - 1, :],
                       jnp.where(row == 1, ah[HALO - 1:HALO, :], pltpu.roll(a, 2, 0)))
        w = cw[...]
        conv = w[0:1, :] * a2 + w[1:2, :] * a1 + w[2:3, :] * a
        z = zc[...].astype(F32)
        out[...] = (z * _sigmoid(z) * bg[...].astype(F32) * conv).astype(BF)

    return pl.pallas_call(
        body, name="conv_fwd", grid=(S // tm, nct),
        in_specs=[seg(0), seg(1), seg(2), seg(3), halo_before(0), halo_before(2),
                  pl.BlockSpec((8, tc), lambda i, j: (0, j))],
        out_specs=pl.BlockSpec((tm, tc), lambda i, j: (i, j)),
        out_shape=jax.ShapeDtypeStruct((S, D), BF),
        compiler_params=_params(2),
    )(pa, pa, pa, pa, pa, pa, cw8)


ATT_UNROLL = 2


def _fold_masks():
    row = lax.broadcasted_iota(jnp.int32, (QB, QB), 0)
    lane = lax.broadcasted_iota(jnp.int32, (QB, QB), 1)
    tri_le = lane <= row
    dist = jnp.where(tri_le, row - lane, row - lane + QB).astype(F32)
    return tri_le, dist, lane < HEAD_DIM


def _block_rows(b, d, S):
    nb = S // (QB * d)
    r, n = b // nb, b % nb
    cur0 = r + n * (QB * d)
    prev0 = r + jnp.maximum(n - 1, 0) * (QB * d)
    if d == 1:
        return n, pl.ds(pl.multiple_of(cur0, QB), QB), pl.ds(pl.multiple_of(prev0, QB), QB)
    return n, pl.ds(cur0, QB, stride=d), pl.ds(prev0, QB, stride=d)


def _head_sum_matrix():
    r = lax.broadcasted_iota(jnp.int32, (2 * QB, 2 * QB), 0)
    c = lax.broadcasted_iota(jnp.int32, (2 * QB, 2 * QB), 1)
    return (((r % QB) // HEAD_DIM) == (c // QB)).astype(F32).astype(BF)


def _hi_lo(t):
    hi = t.astype(BF)
    return jnp.concatenate([hi, (t - hi.astype(F32)).astype(BF)], axis=1)


def _attn_fwd(qkv, pa, slopes):
    S = qkv.shape[0]
    hpr = D // 128
    n_blocks = S // QB

    def body(sl_ref, q_ref, k_ref, v_ref, za_ref, o_ref, lse_ref, ya_ref, acc, m_s, l_s):
        hp = pl.program_id(0)
        tri_le, dist, low = _fold_masks()
        low_b = low.astype(F32).astype(BF)
        high_b = 1.0 - low_b
        head_sum = _head_sum_matrix()
        ones_b = jnp.ones((2 * QB, QB), BF)
        m_s[...] = jnp.full(m_s.shape, NEG, F32)
        l_s[...] = jnp.zeros(l_s.shape, F32)
        acc[...] = jnp.zeros(acc.shape, F32)

        for d in DILATIONS:
            slope = [sl_ref[2 * hp + a] * float(d) for a in range(2)]
            bias = [slope[a] * dist for a in range(2)]

            def block(b, d=d, slope=slope, bias=bias):
                n, cur, prev = _block_rows(b, d, S)
                has_prev = n > 0
                valid = jnp.logical_or(tri_le, has_prev)
                q2 = (q_ref[cur, :] * 0.125).astype(BF)
                qs = jnp.concatenate([q2 * low_b, q2 * high_b], axis=0)
                vp = v_ref[prev, :]
                kp_b = k_ref[prev, :].astype(BF)
                kcat = jnp.concatenate([kp_b, k_ref[cur, :].astype(BF)], axis=0)
                vcat = jnp.concatenate([vp, v_ref[cur, :]], axis=0).astype(BF)
                s2 = _dot_nt(qs, kcat)
                e2 = _dot(_hi_lo(q2.astype(F32) * kp_b.astype(F32)), head_sum)
                p_rows, alpha_h, pe_h = [], [], []
                for a in range(2):
                    sp, sc = s2[a * QB:(a + 1) * QB, :QB], s2[a * QB:(a + 1) * QB, QB:]
                    comb = jnp.where(valid, jnp.where(tri_le, sc, sp) - bias[a], NEG)
                    e = jnp.where(has_prev, e2[:, a * QB:(a + 1) * QB] - slope[a] * float(QB), NEG)
                    m_old = m_s.at[a][cur, :]
                    m_new = jnp.maximum(jnp.maximum(m_old, jnp.max(comb, axis=-1, keepdims=True)), e)
                    m_s.at[a][cur, :] = m_new
                    p = jnp.exp(comb - m_new)
                    pe_h.append(jnp.exp(e - m_new))
                    alpha_h.append(jnp.exp(m_old - m_new))
                    p_rows.append(jnp.concatenate([jnp.where(tri_le, 0.0, p).astype(BF),
                                                   jnp.where(tri_le, p, 0.0).astype(BF)], axis=1))
                pv = _dot(jnp.concatenate(p_rows, axis=0), jnp.concatenate([vcat, ones_b], axis=1))
                for a in range(2):
                    l_a = l_s.at[a]
                    l_a[cur, :] = alpha_h[a] * l_a[cur, :] + pv[a * QB:(a + 1) * QB, QB:] + pe_h[a]
                acc[cur, :] = (jnp.where(low, alpha_h[0], alpha_h[1]) * acc[cur, :]
                               + jnp.where(low, pv[:QB, :QB], pv[QB:, :QB]) + jnp.where(low, pe_h[0], pe_h[1]) * vp)

            def several(it, carry, block=block):
                for u in range(ATT_UNROLL):
                    block(it * ATT_UNROLL + u)
                return carry

            lax.fori_loop(0, n_blocks // ATT_UNROLL, several, 0)

        def finish(i, carry):
            rows = pl.ds(pl.multiple_of(i * QB, QB), QB)
            l0, l1 = l_s[0, rows, :], l_s[1, rows, :]
            o = acc[rows, :] / jnp.where(low, l0, l1)
            o_ref[rows, :] = o
            lse_ref[0, rows, :] = m_s[0, rows, :] + jnp.log(l0)
            lse_ref[1, rows, :] = m_s[1, rows, :] + jnp.log(l1)
            z = za_ref[rows, :].astype(F32)
            ya_ref[rows, :] = (z * _sigmoid(z) * o).astype(BF)
            return carry

        lax.fori_loop(0, n_blocks, finish, 0)

    col = lambda s: pl.BlockSpec((S, 128), lambda h, s=s: (0, s * hpr + h))
    return pl.pallas_call(
        body, name="attn_fwd", grid=(hpr,),
        in_specs=[SMEM_SPEC, col(0), col(1), col(2), col(4)],
        out_specs=(col(0), pl.BlockSpec((2, S, 128), lambda h: (0, 0, h)), col(0)),
        out_shape=(jax.ShapeDtypeStruct((S, D), F32), jax.ShapeDtypeStruct((2, S, D), F32),
                   jax.ShapeDtypeStruct((S, D), BF)),
        scratch_shapes=[pltpu.VMEM((S, 128), F32), pltpu.VMEM((2, S, 128), F32), pltpu.VMEM((2, S, 128), F32)],
        compiler_params=_params(1),
    )(slopes, qkv, qkv, qkv, pa)


def _set_rows(shape, rows):
    idx = lax.broadcasted_iota(jnp.int32, shape, 0)
    out = jnp.zeros(shape, F32)
    for r, val in rows.items():
        out = out + jnp.where(idx == r, val, 0.0)
    return out


def _mid(yc_in, ya_in, pa, o, x2, target, b_merge, final_g, w3):
    S = x2.shape[0]
    tm = ROW_TILE
    nsteps = S // tm
    tile = pl.BlockSpec((tm, D), lambda i: (i, 0))

    def body(yc_ref, ya_ref, za_ref, gcp_ref, gap_ref, o_ref, x_ref, t_ref, b_ref, fg_ref, w_ref,
             dh_ref, dmid_ref, do_ref, dyc_ref, gw_ref, small_ref, acc, stage):
        i = pl.program_id(0)

        @pl.when(i == 0)
        def _():
            acc[...] = jnp.zeros_like(acc)
            small_ref[...] = jnp.zeros_like(small_ref)

        wc, wa, wo = w_ref[0], w_ref[1], w_ref[2]
        yc_in_b, ya_in_b = yc_ref[...], ya_ref[...]
        yc = _dot(yc_in_b, wc)
        ya = _dot(ya_in_b, wa)
        b = b_ref[...]
        gc = _sigmoid(gcp_ref[...].astype(F32) + b[:, :D])
        ga = _sigmoid(gap_ref[...].astype(F32) + b[:, D:])
        merged = gc * yc + ga * ya
        merged_b = merged.astype(BF)
        h = x_ref[...] + _dot(merged_b, wo)
        r2 = lax.rsqrt(jnp.mean(h * h, axis=-1, keepdims=True) + EPS)
        n = h * r2
        fg = fg_ref[...]
        err = n * fg - t_ref[...]
        loss = 0.5 * jnp.sum(jnp.sum(err * err, axis=-1, keepdims=True) / D, axis=0, keepdims=True)
        dy = err / D
        g_fg = jnp.sum(dy * n, axis=0, keepdims=True)
        dn = dy * fg
        dh = r2 * (dn - n * jnp.mean(dn * n, axis=-1, keepdims=True))
        dh_ref[...] = dh
        dh_b = dh.astype(BF)
        dmerged = _dot_nt(dh_b, wo)
        acc[2] += _dot(merged.T.astype(BF), dh_b)
        dyc = (dmerged * gc).astype(BF)
        dya = (dmerged * ga).astype(BF)
        dgcp = dmerged * yc * gc * (1.0 - gc)
        dgap = dmerged * ya * ga * (1.0 - ga)
        dmid_ref[1] = dgcp.astype(BF)
        dmid_ref[2] = dgap.astype(BF)
        acc[0] += _dot(yc_in_b.astype(F32).T.astype(BF), dyc)
        acc[1] += _dot(ya_in_b.astype(F32).T.astype(BF), dya)
        dyc_ref[...] = _dot_nt(dyc, wc).astype(BF)
        dya_in = _dot_nt(dya, wa)
        z = za_ref[...].astype(F32)
        sg = _sigmoid(z)
        do_ref[...] = dya_in * (z * sg)
        dmid_ref[0] = (dya_in * o_ref[...] * (sg * (1.0 + z * (1.0 - sg)))).astype(BF)
        small_ref[...] += _set_rows((8, D), {
            1: jnp.sum(dgcp, axis=0, keepdims=True), 2: jnp.sum(dgap, axis=0, keepdims=True),
            3: g_fg, 7: jnp.broadcast_to(loss, (1, D))})

        @pl.when(i == nsteps - 1)
        def _():
            for p in range(N_DEV):
                for a in range(3):
                    stage[...] = acc[a, p * ROW_SHARD:(p + 1) * ROW_SHARD, :].astype(BF)
                    pltpu.sync_copy(stage, gw_ref.at[p, a])

    return pl.pallas_call(
        body, name="mid", grid=(nsteps,),
        in_specs=[tile, tile, pl.BlockSpec((tm, D), lambda i: (i, 4)), pl.BlockSpec((tm, D), lambda i: (i, 5)),
                  pl.BlockSpec((tm, D), lambda i: (i, 6)), tile, tile, tile,
                  pl.BlockSpec((1, 2 * D), lambda i: (0, 0)), pl.BlockSpec((1, D), lambda i: (0, 0)), VMEM_SPEC],
        out_specs=(tile, pl.BlockSpec((3, tm, D), lambda i: (0, i, 0)), tile, tile,
                   ANY_SPEC, pl.BlockSpec((8, D), lambda i: (0, 0))),
        out_shape=(jax.ShapeDtypeStruct((S, D), F32), jax.ShapeDtypeStruct((3, S, D), BF),
                   jax.ShapeDtypeStruct((S, D), F32), jax.ShapeDtypeStruct((S, D), BF),
                   jax.ShapeDtypeStruct((N_DEV, 3, ROW_SHARD, D), BF), jax.ShapeDtypeStruct((8, D), F32)),
        scratch_shapes=[pltpu.VMEM((3, D, D), F32), pltpu.VMEM((ROW_SHARD, D), BF)],
        compiler_params=_params(1),
    )(yc_in, ya_in, pa, pa, pa, o, x2, target, b_merge, final_g, w3)


def _conv_bwd(dyc_in, pa, cw8):
    S = pa.shape[0]
    tm, tc = CONV_TM, CONV_TC
    nct = D // tc
    nrt = S // tm
    last_halo = S // HALO - 1

    def seg(s):
        return pl.BlockSpec((tm, tc), lambda j, i, s=s: (i, s * nct + j))

    def halo_before(s):
        return pl.BlockSpec((HALO, tc), lambda j, i, s=s: (jnp.maximum(i * (tm // HALO) - 1, 0), s * nct + j))

    def halo_after(s):
        return pl.BlockSpec((HALO, tc), lambda j, i, s=s: (jnp.minimum((i + 1) * (tm // HALO), last_halo), s * nct + j))

    def body(dy, xc, bg, cg, zc, xch, cgh, dyn, bgn, zcn, cw, dout, gcw):
        i = pl.program_id(1)

        @pl.when(i == 0)
        def _():
            gcw[...] = jnp.zeros_like(gcw)

        xcv, cgv = xc[...].astype(F32), cg[...].astype(F32)
        a = cgv * xcv
        ah = jnp.where(i > 0, cgh[...].astype(F32) * xch[...].astype(F32), 0.0)
        row = lax.broadcasted_iota(jnp.int32, (tm, tc), 0)
        a1 = jnp.where(row == 0, ah[HALO - 1:HALO, :], pltpu.roll(a, 1, 0))
        a2 = jnp.where(row == 0, ah[HALO - 2:HALO - 1, :],
                       jnp.where(row == 1, ah[HALO - 1:HALO, :], pltpu.roll(a, 2, 0)))
        w = cw[...]
        conv = w[0:1, :] * a2 + w[1:2, :] * a1 + w[2:3, :] * a
        z = zc[...].astype(F32)
        sg = _sigmoid(z)
        silu = z * sg
        bgv = bg[...].astype(F32)
        dyv = dy[...].astype(F32)
        dout[3] = (dyv * bgv * conv * (sg * (1.0 + z * (1.0 - sg)))).astype(BF)
        dout[1] = (dyv * silu * conv).astype(BF)
        dc = dyv * silu * bgv
        zn = zcn[...].astype(F32)
        dcn = dyn[...].astype(F32) * (zn * _sigmoid(zn)) * bgn[...].astype(F32)
        dcn = jnp.where(i < nrt - 1, dcn, 0.0)
        dc1 = jnp.where(row == tm - 1, dcn[0:1, :], pltpu.roll(dc, tm - 1, 0))
        dc2 = jnp.where(row == tm - 1, dcn[1:2, :],
                        jnp.where(row == tm - 2, dcn[0:1, :], pltpu.roll(dc, tm - 2, 0)))
        da = w[2:3, :] * dc + w[1:2, :] * dc1 + w[0:1, :] * dc2
        dout[2] = (da * xcv).astype(BF)
        dout[0] = (da * cgv).astype(BF)
        gcw[...] += _set_rows((8, tc), {
            4: jnp.sum(dc * a2, axis=0, keepdims=True), 5: jnp.sum(dc * a1, axis=0, keepdims=True),
            6: jnp.sum(dc * a, axis=0, keepdims=True)})

    return pl.pallas_call(
        body, name="conv_bwd", grid=(nct, nrt),
        in_specs=[pl.BlockSpec((tm, tc), lambda j, i: (i, j)), seg(0), seg(1), seg(2), seg(3),
                  halo_before(0), halo_before(2),
                  pl.BlockSpec((HALO, tc), lambda j, i: (jnp.minimum((i + 1) * (tm // HALO), last_halo), j)),
                  halo_after(1), halo_after(3), pl.BlockSpec((8, tc), lambda j, i: (0, j))],
        out_specs=(pl.BlockSpec((4, tm, tc), lambda j, i: (0, i, j)), pl.BlockSpec((8, tc), lambda j, i: (0, j))),
        out_shape=(jax.ShapeDtypeStruct((4, S, D), BF), jax.ShapeDtypeStruct((8, D), F32)),
        compiler_params=_params(2),
    )(dyc_in, pa, pa, pa, pa, pa, pa, dyc_in, pa, pa, cw8)


def _attn_bwd(qkv, slopes, do, o, lse):
    S = qkv.shape[0]
    hpr = D // 128
    n_blocks = S // QB

    def body(sl_ref, q_ref, k_ref, v_ref, do_ref, o_ref, lse_ref, out_ref, dq_s, dk_s, dv_s, dd_s):
        hp = pl.program_id(0)
        tri_le, dist, low = _fold_masks()
        low_b = low.astype(F32).astype(BF)
        high_b = 1.0 - low_b
        head_sum = _head_sum_matrix()
        dq_s[...] = jnp.zeros(dq_s.shape, F32)
        dk_s[...] = jnp.zeros(dk_s.shape, F32)
        dv_s[...] = jnp.zeros(dv_s.shape, F32)

        def row_dots(i, carry):
            rows = pl.ds(pl.multiple_of(i * QB, QB), QB)
            dd = _dot(_hi_lo(do_ref[rows, :] * o_ref[rows, :]), head_sum)
            dd_s[0, rows, :] = dd[:, :QB]
            dd_s[1, rows, :] = dd[:, QB:]
            return carry

        lax.fori_loop(0, n_blocks, row_dots, 0)

        for d in DILATIONS:
            slope = [sl_ref[2 * hp + a] * float(d) for a in range(2)]
            bias = [slope[a] * dist for a in range(2)]

            def block(b, d=d, slope=slope, bias=bias):
                n, cur, prev = _block_rows(b, d, S)
                has_prev = n > 0
                valid = jnp.logical_or(tri_le, has_prev)
                q2f = q_ref[cur, :] * 0.125
                q2 = q2f.astype(BF)
                qs = jnp.concatenate([q2 * low_b, q2 * high_b], axis=0)
                kp, vp = k_ref[prev, :], v_ref[prev, :]
                kp_b, vp_b = kp.astype(BF), vp.astype(BF)
                kcat = jnp.concatenate([kp_b, k_ref[cur, :].astype(BF)], axis=0)
                vcat = jnp.concatenate([vp_b, v_ref[cur, :].astype(BF)], axis=0)
                do2f = do_ref[cur, :]
                do2 = do2f.astype(BF)
                dos = jnp.concatenate([do2 * low_b, do2 * high_b], axis=0)
                s2 = _dot_nt(qs, kcat)
                dp2 = _dot_nt(dos, vcat)
                diag2 = _dot(jnp.concatenate([_hi_lo(q2.astype(F32) * kp_b.astype(F32)),
                                              _hi_lo(do2.astype(F32) * vp_b.astype(F32))], axis=0), head_sum)
                p_rows, ds_rows, pe_h, dse_h = [], [], [], []
                for a in range(2):
                    hs = slice(a * QB, (a + 1) * QB)
                    sp, sc = s2[hs, :QB], s2[hs, QB:]
                    dpp, dpc = dp2[hs, :QB], dp2[hs, QB:]
                    lse_a, dd_a = lse_ref.at[a][cur, :], dd_s.at[a][cur, :]
                    comb = jnp.where(tri_le, sc, sp) - bias[a]
                    e = diag2[:QB, hs] - slope[a] * float(QB)
                    p = jnp.where(valid, jnp.exp(comb - lse_a), 0.0)
                    pe = jnp.where(has_prev, jnp.exp(e - lse_a), 0.0)
                    ds = p * (jnp.where(tri_le, dpc, dpp) - dd_a)
                    dse_h.append(pe * (diag2[QB:, hs] - dd_a))
                    pe_h.append(pe)
                    p_rows.append(jnp.concatenate([jnp.where(tri_le, 0.0, p).astype(BF),
                                                   jnp.where(tri_le, p, 0.0).astype(BF)], axis=1))
                    ds_rows.append(jnp.concatenate([jnp.where(tri_le, 0.0, ds).astype(BF),
                                                    jnp.where(tri_le, ds, 0.0).astype(BF)], axis=1))
                pst = jnp.concatenate(p_rows, axis=0)
                dst = jnp.concatenate(ds_rows, axis=0)
                pe2 = jnp.where(low, pe_h[0], pe_h[1])
                dse2 = jnp.where(low, dse_h[0], dse_h[1])
                dq = _dot(dst, kcat)
                dq_s[cur, :] += (jnp.where(low, dq[:QB], dq[QB:]) + dse2 * kp) * 0.125
                dk = _dot_tn(dst, qs)
                dv = _dot_tn(pst, dos)
                dk_s[prev, :] += dk[:QB] + dse2 * q2f
                dk_s[cur, :] += dk[QB:]
                dv_s[prev, :] += dv[:QB] + pe2 * do2f
                dv_s[cur, :] += dv[QB:]

            def several(it, carry, block=block):
                for u in range(ATT_UNROLL):
                    block(it * ATT_UNROLL + u)
                return carry

            lax.fori_loop(0, n_blocks // ATT_UNROLL, several, 0)

        def finish(i, carry):
            rows = pl.ds(pl.multiple_of(i * QB, QB), QB)
            out_ref[0, rows, :] = dq_s[rows, :].astype(BF)
            out_ref[1, rows, :] = dk_s[rows, :].astype(BF)
            out_ref[2, rows, :] = dv_s[rows, :].astype(BF)
            return carry

        lax.fori_loop(0, n_blocks, finish, 0)

    col = lambda s: pl.BlockSpec((S, 128), lambda h, s=s: (0, s * hpr + h))
    return pl.pallas_call(
        body, name="attn_bwd", grid=(hpr,),
        in_specs=[SMEM_SPEC, col(0), col(1), col(2), col(0), col(0), pl.BlockSpec((2, S, 128), lambda h: (0, 0, h))],
        out_specs=pl.BlockSpec((3, S, 128), lambda h: (0, 0, h)),
        out_shape=jax.ShapeDtypeStruct((3, S, D), BF),
        scratch_shapes=[pltpu.VMEM((S, 128), F32)] * 3 + [pltpu.VMEM((2, S, 128), F32)],
        compiler_params=_params(1),
    )(slopes, qkv, qkv, qkv, do, o, lse)


GROUP_SEGS = (4, 3, 3)
WG_TN = 256


def _wgrad_in(ut, d_conv, d_attn, d_mid):
    S = ut.shape[1]
    tn = WG_TN
    per_seg = D // tn
    n_tiles = IN_COLS // tn
    first = (0, GROUP_SEGS[0] * per_seg, (GROUP_SEGS[0] + GROUP_SEGS[1]) * per_seg, n_tiles)

    def group_spec(g):
        lo, hi = first[g], first[g + 1]

        def index(j):
            jj = jnp.clip(j, lo, hi - 1) - lo
            return (jj // per_seg, 0, jj % per_seg)

        return pl.BlockSpec((1, S, tn), index)

    per_shard = W_IN_SHARD // tn

    def body(ut_ref, a_ref, b_ref, c_ref, out_ref):
        j = pl.program_id(0)
        for g, ref in enumerate((a_ref, b_ref, c_ref)):
            @pl.when(jnp.logical_and(j >= first[g], j < first[g + 1]))
            def _(ref=ref):
                out_ref[0] = _dot(ut_ref[...], ref[0]).astype(BF)

    return pl.pallas_call(
        body, name="wgrad_in", grid=(n_tiles,),
        in_specs=[VMEM_SPEC, group_spec(0), group_spec(1), group_spec(2)],
        out_specs=pl.BlockSpec((1, D, tn), lambda j: (j // per_shard, 0, j % per_shard)),
        out_shape=jax.ShapeDtypeStruct((N_DEV, D, W_IN_SHARD), BF),
        compiler_params=_params(1),
    )(ut, d_conv, d_attn, d_mid)


def _dgrad_norm_bwd(d_conv, d_attn, d_mid, w_all, x2, dh, norm_g):
    S = x2.shape[0]
    tm = ROW_TILE
    tile = pl.BlockSpec((tm, D), lambda i: (i, 0))
    pieces = _proj_pieces()

    def body(a_ref, b_ref, c_ref, w_ref, x_ref, dh_ref, g_ref, gx_ref, small_ref):
        i = pl.program_id(0)

        @pl.when(i == 0)
        def _():
            small_ref[...] = jnp.zeros_like(small_ref)

        groups = (a_ref, b_ref, c_ref)
        du = jnp.zeros((tm, D), F32)
        for s, sc, p, pc, width in pieces:
            g = 0 if s < 4 else (1 if s < 7 else 2)
            local = s - (0, 4, 7)[g]
            du = du + _dot_nt(groups[g][local, :, sc:sc + width], w_ref[p, :, pc:pc + width])
        xv = x_ref[...]
        r = lax.rsqrt(jnp.mean(xv * xv, axis=-1, keepdims=True) + EPS)
        n = xv * r
        dn = du * g_ref[...]
        gx_ref[...] = dh_ref[...] + r * (dn - n * jnp.mean(dn * n, axis=-1, keepdims=True))
        small_ref[...] += _set_rows((8, D), {0: jnp.sum(du * n, axis=0, keepdims=True)})

    return pl.pallas_call(
        body, name="dgrad_norm_bwd", grid=(S // tm,),
        in_specs=[pl.BlockSpec((4, tm, D), lambda i: (0, i, 0)), pl.BlockSpec((3, tm, D), lambda i: (0, i, 0)),
                  pl.BlockSpec((3, tm, D), lambda i: (0, i, 0)), VMEM_SPEC, tile, tile,
                  pl.BlockSpec((1, D), lambda i: (0, 0))],
        out_specs=(tile, pl.BlockSpec((8, D), lambda i: (0, 0))),
        out_shape=(jax.ShapeDtypeStruct((S, D), F32), jax.ShapeDtypeStruct((8, D), F32)),
        compiler_params=_params(1),
    )(d_conv, d_attn, d_mid, w_all, x2, dh, norm_g)


def _adamw_math(w, g, m, v):
    m = ADAM_B1 * m + (1.0 - ADAM_B1) * g
    v = ADAM_B2 * v + (1.0 - ADAM_B2) * (g * g)
    m_hat = m / (1.0 - ADAM_B1 ** ADAM_STEP)
    v_hat = v / (1.0 - ADAM_B2 ** ADAM_STEP)
    delta = -ADAM_LR * (m_hat / (jnp.sqrt(v_hat) + ADAM_EPS) + ADAM_WD * w)
    return delta, m, v


def _sum_adamw(parts, w, m, v, tm, name):
    R, C = w.shape
    tile = pl.BlockSpec((tm, C), lambda i: (i, 0))

    def body(p_ref, w_ref, m_ref, v_ref, g_out, d_out, m_out, v_out):
        g = p_ref[0].astype(F32)
        for s in range(1, N_DEV):
            g = g + p_ref[s].astype(F32)
        g_out[...] = g
        d_out[...], m_out[...], v_out[...] = _adamw_math(w_ref[...], g, m_ref[...], v_ref[...])

    shape = jax.ShapeDtypeStruct((R, C), F32)
    return pl.pallas_call(
        body, name=name, grid=(R // tm,),
        in_specs=[pl.BlockSpec((N_DEV, tm, C), lambda i: (0, i, 0)), tile, tile, tile],
        out_specs=(tile, tile, tile, tile), out_shape=(shape, shape, shape, shape),
        compiler_params=_params(1),
    )(parts, w, m, v)


def _adamw(g, w, m, v, name):
    def body(g_ref, w_ref, m_ref, v_ref, d_out, m_out, v_out):
        d_out[...], m_out[...], v_out[...] = _adamw_math(w_ref[...], g_ref[...], m_ref[...], v_ref[...])

    shape = jax.ShapeDtypeStruct(w.shape, F32)
    return pl.pallas_call(
        body, name=name, in_specs=[VMEM_SPEC] * 4, out_specs=(VMEM_SPEC,) * 3, out_shape=(shape, shape, shape),
    )(g, w, m, v)


def _alibi_slopes():
    return jnp.exp2(-8.0 * jnp.arange(1, N_HEADS + 1, dtype=F32) / N_HEADS)


def _local_step(x2, target, norm_g, b_merge, final_g, w_all, w3, cw8):
    slopes = _alibi_slopes()
    pa, qkv, ut = _norm_proj(x2, norm_g, w_all)
    yc_in = _conv_fwd(pa, cw8)
    o, lse, ya_in = _attn_fwd(qkv, pa, slopes)
    dh, d_mid, do, dyc_in, g_3, small_mid = _mid(yc_in, ya_in, pa, o, x2, target, b_merge, final_g, w3)
    d_conv, small_conv = _conv_bwd(dyc_in, pa, cw8)
    d_attn = _attn_bwd(qkv, slopes, do, o, lse)
    g_in = _wgrad_in(ut, d_conv, d_attn, d_mid)
    grad_x, small_norm = _dgrad_norm_bwd(d_conv, d_attn, d_mid, w_all, x2, dh, norm_g)
    return grad_x, g_in, g_3, small_mid, small_conv, small_norm


def kernel(x, norm_g, w_in, b_merge, conv_w, w_out_conv, w_out_attn, w_o, final_g, loss_target, m_norm_g, m_w_in, m_b_merge, m_conv_w, m_w_out_conv, m_w_out_attn, m_w_o, m_final_g, v_norm_g, v_w_in, v_b_merge, v_conv_w, v_w_out_conv, v_w_out_attn, v_w_o, v_final_g):
    me = 4 * lax.axis_index("x") + 2 * lax.axis_index("y") + lax.axis_index("c")
    stack3 = lambda a, b, c: jnp.concatenate([a, b, c], axis=0)
    pad8 = lambda a: jnp.pad(a, ((0, 8 - a.shape[0]), (0, 0)))

    w3_shard = stack3(w_out_conv, w_out_attn, w_o)
    w_all, w3_all, cw_all = _allgather_weights(w_in[0], w3_shard, pad8(conv_w[0]))
    w3 = jnp.transpose(w3_all, (1, 0, 2, 3)).reshape(3, D, D)
    cw8 = jnp.transpose(cw_all, (1, 0, 2)).reshape(8, D)

    final_g2 = final_g.reshape(1, D)
    grad_x, g_in, g_3, small_mid, small_conv, small_norm = _local_step(
        x[0], loss_target[0], norm_g, b_merge, final_g2, w_all, w3, cw8)

    r_in, r_3 = _exchange_grads(g_in, g_3)
    small = _allreduce_small(small_mid, small_conv, small_norm)

    g_w_in, d_w_in, nm_w_in, nv_w_in = _sum_adamw(r_in, w_in[0], m_w_in[0], v_w_in[0], 128, "adamw_w_in")
    g_w3, d_w3, nm_w3, nv_w3 = _sum_adamw(
        r_3.reshape(N_DEV, 3 * ROW_SHARD, D), w3_shard.reshape(3 * ROW_SHARD, D),
        stack3(m_w_out_conv, m_w_out_attn, m_w_o).reshape(3 * ROW_SHARD, D),
        stack3(v_w_out_conv, v_w_out_attn, v_w_o).reshape(3 * ROW_SHARD, D), ROW_SHARD, "adamw_w3")

    def pack(ng, bm, fg):
        return pad8(jnp.concatenate([ng, bm.reshape(2, D), fg.reshape(1, D)], axis=0))

    d_s, nm_s, nv_s = _adamw(small, pack(norm_g, b_merge, final_g), pack(m_norm_g, m_b_merge, m_final_g),
                             pack(v_norm_g, v_b_merge, v_final_g), "adamw_small")
    g_cw = lax.dynamic_slice(small, (4, me * ROW_SHARD), (3, ROW_SHARD))
    d_cw, nm_cw, nv_cw = _adamw(g_cw, conv_w[0], m_conv_w[0], v_conv_w[0], "adamw_conv_w")

    loss = small[7, 0]
    split3 = lambda t: tuple(t[a * ROW_SHARD:(a + 1) * ROW_SHARD][None] for a in range(3))
    unpack = lambda t: (t[0:1], t[1:3].reshape(1, 2 * D), t[3])

    def leaves(in_, small_, cw_, w3_):
        ng, bm, fg = unpack(small_)
        wc, wa, wo = split3(w3_)
        return (ng, in_[None], bm, cw_[None], wc, wa, wo, fg)

    return (loss, grad_x[None],
            *leaves(g_w_in, small, g_cw, g_w3),
            *leaves(d_w_in, d_s, d_cw, d_w3),
            *leaves(nm_w_in, nm_s, nm_cw, nm_w3),
            *leaves(nv_w_in, nv_s, nv_cw, nv_w3))
```

```python
import functools

import jax
import jax.numpy as jnp
from jax import lax
from jax.experimental import pallas as pl
from jax.experimental.pallas import tpu as pltpu

D = 1024
N_HEADS = 16
HEAD_DIM = 64
N_SEG = 10
IN_COLS = N_SEG * D
N_DEV = 8
W_IN_SHARD = IN_COLS // N_DEV
ROW_SHARD = D // N_DEV
QB = 128
DILATIONS = (1, 4, 16)
EPS = 1e-6
NEG = -1e30
BF = jnp.bfloat16
F32 = jnp.float32
MESH = pl.DeviceIdType.MESH

ADAM_LR = 0.001
ADAM_B1 = 0.9
ADAM_B2 = 0.999
ADAM_EPS = 1e-08
ADAM_WD = 0.01
ADAM_STEP = 10

V7X_VMEM_BYTES = 64 * 1024 * 1024
VMEM_LIMIT = V7X_VMEM_BYTES - 8 * 1024 * 1024
ROW_TILE = 256

VMEM_SPEC = pl.BlockSpec(memory_space=pltpu.VMEM)
ANY_SPEC = pl.BlockSpec(memory_space=pl.ANY)
SMEM_SPEC = pl.BlockSpec(memory_space=pltpu.SMEM)


def _params(n_grid_axes, vmem=VMEM_LIMIT):
    return pltpu.CompilerParams(dimension_semantics=("arbitrary",) * n_grid_axes, vmem_limit_bytes=vmem)


def _dot(a, b):
    return jnp.dot(a, b, preferred_element_type=F32)


def _dot_nt(a, b):
    return lax.dot_general(a, b, (((1,), (1,)), ((), ())), preferred_element_type=F32)


def _dot_tn(a, b):
    return lax.dot_general(a, b, (((0,), (0,)), ((), ())), preferred_element_type=F32)


def _sigmoid(z):
    return 1.0 / (1.0 + jnp.exp(-z))


def _my_place():
    x, y, c = lax.axis_index("x"), lax.axis_index("y"), lax.axis_index("c")
    return x, y, c, 4 * x + 2 * y + c


def _peers(x, y, c):
    out = []
    for k in range(1, N_DEV):
        px = 1 - x if k & 4 else x
        py = 1 - y if k & 2 else y
        pc = 1 - c if k & 1 else c
        out.append(((px, py, pc), 4 * px + 2 * py + pc))
    return out


def _allgather_weights(w_in, w3, cw):
    def body(w_in_ref, w3_ref, cw_ref, o_in, o_3, o_cw, in_bf, w3_bf, send_sems, recv_sems, local_sems):
        x, y, c, me = _my_place()

        def cast_rows(i, carry):
            r = pl.multiple_of(i * 128, 128)
            in_bf[pl.ds(r, 128), :] = w_in_ref[pl.ds(r, 128), :].astype(BF)
            return carry

        lax.fori_loop(0, D // 128, cast_rows, 0)
        for a in range(3):
            w3_bf[a] = w3_ref[a].astype(BF)

        srcs = (in_bf, w3_bf, cw_ref)
        outs = (o_in, o_3, o_cw)
        local = [pltpu.make_async_copy(srcs[a], outs[a].at[me], local_sems.at[a]) for a in range(3)]
        for cp in local:
            cp.start()
        remote = []
        for k, (peer, _) in enumerate(_peers(x, y, c)):
            for a in range(3):
                cp = pltpu.make_async_remote_copy(
                    src_ref=srcs[a], dst_ref=outs[a].at[me], send_sem=send_sems.at[k, a],
                    recv_sem=recv_sems.at[k, a], device_id=peer, device_id_type=MESH)
                cp.start()
                remote.append(cp)
        for cp in remote:
            cp.wait()
        for cp in local:
            cp.wait()

    return pl.pallas_call(
        body, name="allgather_weights",
        out_shape=(jax.ShapeDtypeStruct((N_DEV, D, W_IN_SHARD), BF),
                   jax.ShapeDtypeStruct((N_DEV, 3, ROW_SHARD, D), BF),
                   jax.ShapeDtypeStruct((N_DEV, 8, 128), F32)),
        in_specs=[VMEM_SPEC, VMEM_SPEC, VMEM_SPEC],
        out_specs=(ANY_SPEC, ANY_SPEC, ANY_SPEC),
        scratch_shapes=[pltpu.VMEM((D, W_IN_SHARD), BF), pltpu.VMEM((3, ROW_SHARD, D), BF),
                        pltpu.SemaphoreType.DMA((N_DEV - 1, 3)), pltpu.SemaphoreType.DMA((N_DEV - 1, 3)),
                        pltpu.SemaphoreType.DMA((3,))],
        compiler_params=pltpu.CompilerParams(vmem_limit_bytes=VMEM_LIMIT),
    )(w_in, w3, cw)


class _GradExchange:
    def __init__(self, src, dst, send_sems, recv_sems, local_sem, cols):
        self.src, self.dst, self.cols = src, dst, cols
        self.send_sems, self.recv_sems, self.local_sem = send_sems, recv_sems, local_sem
        self.me = _my_place()[3]

    def _block(self, ref, idx, p):
        cols = self.cols(p)
        return ref.at[idx] if cols == () else ref.at[idx, :, cols[0]:cols[1]]

    def _remote(self, p, source):
        return pltpu.make_async_remote_copy(
            src_ref=self._block(self.src, p, p), dst_ref=self._block(self.dst, source, p),
            send_sem=self.send_sems.at[p], recv_sem=self.recv_sems.at[source],
            device_id=(p >> 2, (p >> 1) & 1, p & 1), device_id_type=MESH)

    def _local(self, p):
        return pltpu.make_async_copy(self._block(self.src, p, p), self._block(self.dst, p, p), self.local_sem)

    def start(self):
        for p in range(N_DEV):
            if self.cols(p) is None:
                continue
            pl.when(self.me != p)(lambda p=p: self._remote(p, self.me).start())
            pl.when(self.me == p)(lambda p=p: self._local(p).start())

    def finish(self):
        for p in range(N_DEV):
            if self.cols(p) is None:
                continue
            pl.when(self.me != p)(lambda p=p: self._remote(p, self.me).wait_send())

            def receive(p=p):
                self._local(p).wait()
                for s in range(N_DEV):
                    if s != p:
                        self._remote(p, s).wait_recv()

            pl.when(self.me == p)(receive)


def _shard_cols(*ranges):
    def cols(p):
        found = None
        for lo, hi in ranges:
            a, b = max(lo, p * W_IN_SHARD), min(hi, (p + 1) * W_IN_SHARD)
            if a < b:
                assert found is None
                found = (a - p * W_IN_SHARD, b - p * W_IN_SHARD)
        return found

    return cols


GRAD_EXCHANGE_SEMS = [pltpu.SemaphoreType.DMA((N_DEV,)), pltpu.SemaphoreType.DMA((N_DEV,)), pltpu.SemaphoreType.DMA]


def _allreduce_small(p_mid, p_conv, p_norm):
    def body(a_ref, b_ref, c_ref, out_ref, mine, gathered, send_sems, recv_sems):
        x, y, c, me = _my_place()
        mine[...] = a_ref[...] + b_ref[...] + c_ref[...]
        gathered[me] = mine[...]
        remote = []
        for k, (peer, _) in enumerate(_peers(x, y, c)):
            cp = pltpu.make_async_remote_copy(
                src_ref=mine, dst_ref=gathered.at[me], send_sem=send_sems.at[k], recv_sem=recv_sems.at[k],
                device_id=peer, device_id_type=MESH)
            cp.start()
            remote.append(cp)
        for cp in remote:
            cp.wait()
        total = gathered[0]
        for s in range(1, N_DEV):
            total = total + gathered[s]
        out_ref[...] = total

    return pl.pallas_call(
        body, name="allreduce_small",
        out_shape=jax.ShapeDtypeStruct((8, D), F32),
        in_specs=[VMEM_SPEC, VMEM_SPEC, VMEM_SPEC], out_specs=VMEM_SPEC,
        scratch_shapes=[pltpu.VMEM((8, D), F32), pltpu.VMEM((N_DEV, 8, D), F32),
                        pltpu.SemaphoreType.DMA((N_DEV - 1,)), pltpu.SemaphoreType.DMA((N_DEV - 1,))],
    )(p_mid, p_conv, p_norm)


def _proj_pieces():
    cuts = sorted(set(range(0, IN_COLS + 1, D)) | set(range(0, IN_COLS + 1, W_IN_SHARD)))
    return [(lo // D, lo % D, lo // W_IN_SHARD, lo % W_IN_SHARD, hi - lo) for lo, hi in zip(cuts[:-1], cuts[1:])]


def _seg_dest(seg):
    if seg < 4:
        return 0, seg * D
    if seg < 7:
        return 1, (seg - 4) * D
    return 0, (seg - 3) * D


def _norm_proj(x2, norm_g, w_all):
    S = x2.shape[0]
    tm = ROW_TILE
    pieces = _proj_pieces()

    def body(x_ref, g_ref, w_ref, pa_ref, qkv_ref, ut_ref):
        xv = x_ref[...]
        r = lax.rsqrt(jnp.mean(xv * xv, axis=-1, keepdims=True) + EPS)
        u = xv * r * g_ref[...]
        ut_ref[...] = u.T.astype(BF)
        ub = u.astype(BF)
        for j in range(N_DEV):
            res = _dot(ub, w_ref[j])
            for seg, sc, p, pc, width in pieces:
                if p != j:
                    continue
                which, col = _seg_dest(seg)
                if which == 0:
                    pa_ref[:, col + sc:col + sc + width] = res[:, pc:pc + width].astype(BF)
                else:
                    qkv_ref[:, col + sc:col + sc + width] = res[:, pc:pc + width]

    return pl.pallas_call(
        body, name="norm_proj", grid=(S // tm,),
        in_specs=[pl.BlockSpec((tm, D), lambda i: (i, 0)), pl.BlockSpec((1, D), lambda i: (0, 0)), VMEM_SPEC],
        out_specs=(pl.BlockSpec((tm, 7 * D), lambda i: (i, 0)), pl.BlockSpec((tm, 3 * D), lambda i: (i, 0)),
                   pl.BlockSpec((D, tm), lambda i: (0, i))),
        out_shape=(jax.ShapeDtypeStruct((S, 7 * D), BF), jax.ShapeDtypeStruct((S, 3 * D), F32),
                   jax.ShapeDtypeStruct((D, S), BF)),
        compiler_params=_params(1),
    )(x2, norm_g, w_all)


CONV_TM, CONV_TC = 256, 512
HALO = 16


def _conv_fwd(pa, cw8):
    S = pa.shape[0]
    tm, tc = CONV_TM, CONV_TC
    nct = D // tc

    def seg(s):
        return pl.BlockSpec((tm, tc), lambda i, j, s=s: (i, s * nct + j))

    def halo_before(s):
        return pl.BlockSpec((HALO, tc), lambda i, j, s=s: (jnp.maximum(i * (tm // HALO) - 1, 0), s * nct + j))

    def body(xc, bg, cg, zc, xch, cgh, cw, out):
        i = pl.program_id(0)
        a = cg[...].astype(F32) * xc[...].astype(F32)
        ah = cgh[...].astype(F32) * xch[...].astype(F32)
        ah = jnp.where(i > 0, ah, 0.0)
        row = lax.broadcasted_iota(jnp.int32, (tm, tc), 0)
        a1 = jnp.where(row == 0, ah[HALO - 1:HALO, :], pltpu.roll(a, 1, 0))
        a2 = jnp.where(row == 0, ah[HALO - 2:HALO - 1, :],
                       jnp.where(row == 1, ah[HALO - 1:HALO, :], pltpu.roll(a, 2, 0)))
        w = cw[...]
        conv = w[0:1, :] * a2 + w[1:2, :] * a1 + w[2:3, :] * a
        z = zc[...].astype(F32)
        out[...] = (z * _sigmoid(z) * bg[...].astype(F32) * conv).astype(BF)

    return pl.pallas_call(
        body, name="conv_fwd", grid=(S // tm, nct),
        in_specs=[seg(0), seg(1), seg(2), seg(3), halo_before(0), halo_before(2),
                  pl.BlockSpec((8, tc), lambda i, j: (0, j))],
        out_specs=pl.BlockSpec((tm, tc), lambda i, j: (i, j)),
        out_shape=jax.ShapeDtypeStruct((S, D), BF),
        compiler_params=_params(2),
    )(pa, pa, pa, pa, pa, pa, cw8)


ATT_UNROLL = 2


def _fold_masks():
    row = lax.broadcasted_iota(jnp.int32, (QB, QB), 0)
    lane = lax.broadcasted_iota(jnp.int32, (QB, QB), 1)
    tri_le = lane <= row
    dist = jnp.where(tri_le, row - lane, row - lane + QB).astype(F32)
    return tri_le, dist, lane < HEAD_DIM


def _block_rows(b, d, S):
    nb = S // (QB * d)
    r, n = b // nb, b % nb
    cur0 = r + n * (QB * d)
    prev0 = r + jnp.maximum(n - 1, 0) * (QB * d)
    if d == 1:
        return n, pl.ds(pl.multiple_of(cur0, QB), QB), pl.ds(pl.multiple_of(prev0, QB), QB)
    return n, pl.ds(cur0, QB, stride=d), pl.ds(prev0, QB, stride=d)


def _head_sum_matrix():
    r = lax.broadcasted_iota(jnp.int32, (2 * QB, 2 * QB), 0)
    c = lax.broadcasted_iota(jnp.int32, (2 * QB, 2 * QB), 1)
    return (((r % QB) // HEAD_DIM) == (c // QB)).astype(F32).astype(BF)


def _hi_lo(t):
    hi = t.astype(BF)
    return jnp.concatenate([hi, (t - hi.astype(F32)).astype(BF)], axis=1)


def _attn_fwd(qkv, pa, slopes):
    S = qkv.shape[0]
    hpr = D // 128
    n_blocks = S // QB

    def body(sl_ref, q_ref, k_ref, v_ref, za_ref, o_ref, lse_ref, ya_ref, acc, m_s, l_s):
        hp = pl.program_id(0)
        tri_le, dist, low = _fold_masks()
        low_b = low.astype(F32).astype(BF)
        high_b = 1.0 - low_b
        head_sum = _head_sum_matrix()
        ones_b = jnp.ones((2 * QB, QB), BF)
        m_s[...] = jnp.full(m_s.shape, NEG, F32)
        l_s[...] = jnp.zeros(l_s.shape, F32)
        acc[...] = jnp.zeros(acc.shape, F32)

        for d in DILATIONS:
            slope = [sl_ref[2 * hp + a] * float(d) for a in range(2)]
            bias = [slope[a] * dist for a in range(2)]

            def block(b, d=d, slope=slope, bias=bias):
                n, cur, prev = _block_rows(b, d, S)
                has_prev = n > 0
                valid = jnp.logical_or(tri_le, has_prev)
                q2 = (q_ref[cur, :] * 0.125).astype(BF)
                qs = jnp.concatenate([q2 * low_b, q2 * high_b], axis=0)
                vp = v_ref[prev, :]
                kp_b = k_ref[prev, :].astype(BF)
                kcat = jnp.concatenate([kp_b, k_ref[cur, :].astype(BF)], axis=0)
                vcat = jnp.concatenate([vp, v_ref[cur, :]], axis=0).astype(BF)
                s2 = _dot_nt(qs, kcat)
                e2 = _dot(_hi_lo(q2.astype(F32) * kp_b.astype(F32)), head_sum)
                p_rows, alpha_h, pe_h = [], [], []
                for a in range(2):
                    sp, sc = s2[a * QB:(a + 1) * QB, :QB], s2[a * QB:(a + 1) * QB, QB:]
                    comb = jnp.where(valid, jnp.where(tri_le, sc, sp) - bias[a], NEG)
                    e = jnp.where(has_prev, e2[:, a * QB:(a + 1) * QB] - slope[a] * float(QB), NEG)
                    m_old = m_s.at[a][cur, :]
                    m_new = jnp.maximum(jnp.maximum(m_old, jnp.max(comb, axis=-1, keepdims=True)), e)
                    m_s.at[a][cur, :] = m_new
                    p = jnp.exp(comb - m_new)
                    pe_h.append(jnp.exp(e - m_new))
                    alpha_h.append(jnp.exp(m_old - m_new))
                    p_rows.append(jnp.concatenate([jnp.where(tri_le, 0.0, p).astype(BF),
                                                   jnp.where(tri_le, p, 0.0).astype(BF)], axis=1))
                pv = _dot(jnp.concatenate(p_rows, axis=0), jnp.concatenate([vcat, ones_b], axis=1))
                for a in range(2):
                    l_a = l_s.at[a]
                    l_a[cur, :] = alpha_h[a] * l_a[cur, :] + pv[a * QB:(a + 1) * QB, QB:] + pe_h[a]
                acc[cur, :] = (jnp.where(low, alpha_h[0], alpha_h[1]) * acc[cur, :]
                               + jnp.where(low, pv[:QB, :QB], pv[QB:, :QB]) + jnp.where(low, pe_h[0], pe_h[1]) * vp)

            def several(it, carry, block=block):
                for u in range(ATT_UNROLL):
                    block(it * ATT_UNROLL + u)
                return carry

            lax.fori_loop(0, n_blocks // ATT_UNROLL, several, 0)

        def finish(i, carry):
            rows = pl.ds(pl.multiple_of(i * QB, QB), QB)
            l0, l1 = l_s[0, rows, :], l_s[1, rows, :]
            o = acc[rows, :] / jnp.where(low, l0, l1)
            o_ref[rows, :] = o
            lse_ref[0, rows, :] = m_s[0, rows, :] + jnp.log(l0)
            lse_ref[1, rows, :] = m_s[1, rows, :] + jnp.log(l1)
            z = za_ref[rows, :].astype(F32)
            ya_ref[rows, :] = (z * _sigmoid(z) * o).astype(BF)
            return carry

        lax.fori_loop(0, n_blocks, finish, 0)

    col = lambda s: pl.BlockSpec((S, 128), lambda h, s=s: (0, s * hpr + h))
    return pl.pallas_call(
        body, name="attn_fwd", grid=(hpr,),
        in_specs=[SMEM_SPEC, col(0), col(1), col(2), col(4)],
        out_specs=(col(0), pl.BlockSpec((2, S, 128), lambda h: (0, 0, h)), col(0)),
        out_shape=(jax.ShapeDtypeStruct((S, D), F32), jax.ShapeDtypeStruct((2, S, D), F32),
                   jax.ShapeDtypeStruct((S, D), BF)),
        scratch_shapes=[pltpu.VMEM((S, 128), F32), pltpu.VMEM((2, S, 128), F32), pltpu.VMEM((2, S, 128), F32)],
        compiler_params=_params(1),
    )(slopes, qkv, qkv, qkv, pa)


def _set_rows(shape, rows):
    idx = lax.broadcasted_iota(jnp.int32, shape, 0)
    out = jnp.zeros(shape, F32)
    for r, val in rows.items():
        out = out + jnp.where(idx == r, val, 0.0)
    return out


def _mid(yc_in, ya_in, pa, o, x2, target, b_merge, final_g, w3):
    S = x2.shape[0]
    tm = ROW_TILE
    nsteps = S // tm
    tile = pl.BlockSpec((tm, D), lambda i: (i, 0))

    def body(yc_ref, ya_ref, za_ref, gcp_ref, gap_ref, o_ref, x_ref, t_ref, b_ref, fg_ref, w_ref,
             dh_ref, dmid_ref, do_ref, dyc_ref, gw_ref, small_ref, acc, stage):
        i = pl.program_id(0)

        @pl.when(i == 0)
        def _():
            acc[...] = jnp.zeros_like(acc)
            small_ref[...] = jnp.zeros_like(small_ref)

        wc, wa, wo = w_ref[0], w_ref[1], w_ref[2]
        yc_in_b, ya_in_b = yc_ref[...], ya_ref[...]
        yc = _dot(yc_in_b, wc)
        ya = _dot(ya_in_b, wa)
        b = b_ref[...]
        gc = _sigmoid(gcp_ref[...].astype(F32) + b[:, :D])
        ga = _sigmoid(gap_ref[...].astype(F32) + b[:, D:])
        merged = gc * yc + ga * ya
        merged_b = merged.astype(BF)
        h = x_ref[...] + _dot(merged_b, wo)
        r2 = lax.rsqrt(jnp.mean(h * h, axis=-1, keepdims=True) + EPS)
        n = h * r2
        fg = fg_ref[...]
        err = n * fg - t_ref[...]
        loss = 0.5 * jnp.sum(jnp.sum(err * err, axis=-1, keepdims=True) / D, axis=0, keepdims=True)
        dy = err / D
        g_fg = jnp.sum(dy * n, axis=0, keepdims=True)
        dn = dy * fg
        dh = r2 * (dn - n * jnp.mean(dn * n, axis=-1, keepdims=True))
        dh_ref[...] = dh
        dh_b = dh.astype(BF)
        dmerged = _dot_nt(dh_b, wo)
        acc[2] += _dot(merged.T.astype(BF), dh_b)
        dyc = (dmerged * gc).astype(BF)
        dya = (dmerged * ga).astype(BF)
        dgcp = dmerged * yc * gc * (1.0 - gc)
        dgap = dmerged * ya * ga * (1.0 - ga)
        dmid_ref[1] = dgcp.astype(BF)
        dmid_ref[2] = dgap.astype(BF)
        acc[0] += _dot(yc_in_b.astype(F32).T.astype(BF), dyc)
        acc[1] += _dot(ya_in_b.astype(F32).T.astype(BF), dya)
        dyc_ref[...] = _dot_nt(dyc, wc).astype(BF)
        dya_in = _dot_nt(dya, wa)
        z = za_ref[...].astype(F32)
        sg = _sigmoid(z)
        do_ref[...] = dya_in * (z * sg)
        dmid_ref[0] = (dya_in * o_ref[...] * (sg * (1.0 + z * (1.0 - sg)))).astype(BF)
        small_ref[...] += _set_rows((8, D), {
            1: jnp.sum(dgcp, axis=0, keepdims=True), 2: jnp.sum(dgap, axis=0, keepdims=True),
            3: g_fg, 7: jnp.broadcast_to(loss, (1, D))})

        @pl.when(i == nsteps - 1)
        def _():
            for p in range(N_DEV):
                for a in range(3):
                    stage[...] = acc[a, p * ROW_SHARD:(p + 1) * ROW_SHARD, :].astype(BF)
                    pltpu.sync_copy(stage, gw_ref.at[p, a])

    return pl.pallas_call(
        body, name="mid", grid=(nsteps,),
        in_specs=[tile, tile, pl.BlockSpec((tm, D), lambda i: (i, 4)), pl.BlockSpec((tm, D), lambda i: (i, 5)),
                  pl.BlockSpec((tm, D), lambda i: (i, 6)), tile, tile, tile,
                  pl.BlockSpec((1, 2 * D), lambda i: (0, 0)), pl.BlockSpec((1, D), lambda i: (0, 0)), VMEM_SPEC],
        out_specs=(tile, pl.BlockSpec((3, tm, D), lambda i: (0, i, 0)), tile, tile,
                   ANY_SPEC, pl.BlockSpec((8, D), lambda i: (0, 0))),
        out_shape=(jax.ShapeDtypeStruct((S, D), F32), jax.ShapeDtypeStruct((3, S, D), BF),
                   jax.ShapeDtypeStruct((S, D), F32), jax.ShapeDtypeStruct((S, D), BF),
                   jax.ShapeDtypeStruct((N_DEV, 3, ROW_SHARD, D), BF), jax.ShapeDtypeStruct((8, D), F32)),
        scratch_shapes=[pltpu.VMEM((3, D, D), F32), pltpu.VMEM((ROW_SHARD, D), BF)],
        compiler_params=_params(1),
    )(yc_in, ya_in, pa, pa, pa, o, x2, target, b_merge, final_g, w3)


def _conv_bwd(dyc_in, pa, cw8):
    S = pa.shape[0]
    tm, tc = CONV_TM, CONV_TC
    nct = D // tc
    nrt = S // tm
    last_halo = S // HALO - 1

    def seg(s):
        return pl.BlockSpec((tm, tc), lambda j, i, s=s: (i, s * nct + j))

    def halo_before(s):
        return pl.BlockSpec((HALO, tc), lambda j, i, s=s: (jnp.maximum(i * (tm // HALO) - 1, 0), s * nct + j))

    def halo_after(s):
        return pl.BlockSpec((HALO, tc), lambda j, i, s=s: (jnp.minimum((i + 1) * (tm // HALO), last_halo), s * nct + j))

    def body(dy, xc, bg, cg, zc, xch, cgh, dyn, bgn, zcn, cw, dout, gcw):
        i = pl.program_id(1)

        @pl.when(i == 0)
        def _():
            gcw[...] = jnp.zeros_like(gcw)

        xcv, cgv = xc[...].astype(F32), cg[...].astype(F32)
        a = cgv * xcv
        ah = jnp.where(i > 0, cgh[...].astype(F32) * xch[...].astype(F32), 0.0)
        row = lax.broadcasted_iota(jnp.int32, (tm, tc), 0)
        a1 = jnp.where(row == 0, ah[HALO - 1:HALO, :], pltpu.roll(a, 1, 0))
        a2 = jnp.where(row == 0, ah[HALO - 2:HALO - 1, :],
                       jnp.where(row == 1, ah[HALO - 1:HALO, :], pltpu.roll(a, 2, 0)))
        w = cw[...]
        conv = w[0:1, :] * a2 + w[1:2, :] * a1 + w[2:3, :] * a
        z = zc[...].astype(F32)
        sg = _sigmoid(z)
        silu = z * sg
        bgv = bg[...].astype(F32)
        dyv = dy[...].astype(F32)
        dout[3] = (dyv * bgv * conv * (sg * (1.0 + z * (1.0 - sg)))).astype(BF)
        dout[1] = (dyv * silu * conv).astype(BF)
        dc = dyv * silu * bgv
        zn = zcn[...].astype(F32)
        dcn = dyn[...].astype(F32) * (zn * _sigmoid(zn)) * bgn[...].astype(F32)
        dcn = jnp.where(i < nrt - 1, dcn, 0.0)
        dc1 = jnp.where(row == tm - 1, dcn[0:1, :], pltpu.roll(dc, tm - 1, 0))
        dc2 = jnp.where(row == tm - 1, dcn[1:2, :],
                        jnp.where(row == tm - 2, dcn[0:1, :], pltpu.roll(dc, tm - 2, 0)))
        da = w[2:3, :] * dc + w[1:2, :] * dc1 + w[0:1, :] * dc2
        dout[2] = (da * xcv).astype(BF)
        dout[0] = (da * cgv).astype(BF)
        gcw[...] += _set_rows((8, tc), {
            4: jnp.sum(dc * a2, axis=0, keepdims=True), 5: jnp.sum(dc * a1, axis=0, keepdims=True),
            6: jnp.sum(dc * a, axis=0, keepdims=True)})

    return pl.pallas_call(
        body, name="conv_bwd", grid=(nct, nrt),
        in_specs=[pl.BlockSpec((tm, tc), lambda j, i: (i, j)), seg(0), seg(1), seg(2), seg(3),
                  halo_before(0), halo_before(2),
                  pl.BlockSpec((HALO, tc), lambda j, i: (jnp.minimum((i + 1) * (tm // HALO), last_halo), j)),
                  halo_after(1), halo_after(3), pl.BlockSpec((8, tc), lambda j, i: (0, j))],
        out_specs=(pl.BlockSpec((4, tm, tc), lambda j, i: (0, i, j)), pl.BlockSpec((8, tc), lambda j, i: (0, j))),
        out_shape=(jax.ShapeDtypeStruct((4, S, D), BF), jax.ShapeDtypeStruct((8, D), F32)),
        compiler_params=_params(2),
    )(dyc_in, pa, pa, pa, pa, pa, pa, dyc_in, pa, pa, cw8)


def _attn_bwd(qkv, slopes, do, o, lse, g_in, g_3):
    S = qkv.shape[0]
    hpr = D // 128
    n_blocks = S // QB

    def body(sl_ref, q_ref, k_ref, v_ref, do_ref, o_ref, lse_ref, gin_ref, g3_ref, out_ref, rin_ref, r3_ref,
             dq_s, dk_s, dv_s, dd_s, *sems):
        hp = pl.program_id(0)
        exchanges = (_GradExchange(gin_ref, rin_ref, *sems[:3], _shard_cols((0, SEG0_ATTN * D), (SEG0_MID * D, IN_COLS))),
                     _GradExchange(g3_ref, r3_ref, *sems[3:], lambda p: ()))

        @pl.when(hp == 0)
        def _():
            for ex in exchanges:
                ex.start()

        tri_le, dist, low = _fold_masks()
        low_b = low.astype(F32).astype(BF)
        high_b = 1.0 - low_b
        head_sum = _head_sum_matrix()
        dq_s[...] = jnp.zeros(dq_s.shape, F32)
        dk_s[...] = jnp.zeros(dk_s.shape, F32)
        dv_s[...] = jnp.zeros(dv_s.shape, F32)

        def row_dots(i, carry):
            rows = pl.ds(pl.multiple_of(i * QB, QB), QB)
            dd = _dot(_hi_lo(do_ref[rows, :] * o_ref[rows, :]), head_sum)
            dd_s[0, rows, :] = dd[:, :QB]
            dd_s[1, rows, :] = dd[:, QB:]
            return carry

        lax.fori_loop(0, n_blocks, row_dots, 0)

        for d in DILATIONS:
            slope = [sl_ref[2 * hp + a] * float(d) for a in range(2)]
            bias = [slope[a] * dist for a in range(2)]

            def block(b, d=d, slope=slope, bias=bias):
                n, cur, prev = _block_rows(b, d, S)
                has_prev = n > 0
                valid = jnp.logical_or(tri_le, has_prev)
                q2f = q_ref[cur, :] * 0.125
                q2 = q2f.astype(BF)
                qs = jnp.concatenate([q2 * low_b, q2 * high_b], axis=0)
                kp, vp = k_ref[prev, :], v_ref[prev, :]
                kp_b, vp_b = kp.astype(BF), vp.astype(BF)
                kcat = jnp.concatenate([kp_b, k_ref[cur, :].astype(BF)], axis=0)
                vcat = jnp.concatenate([vp_b, v_ref[cur, :].astype(BF)], axis=0)
                do2f = do_ref[cur, :]
                do2 = do2f.astype(BF)
                dos = jnp.concatenate([do2 * low_b, do2 * high_b], axis=0)
                s2 = _dot_nt(qs, kcat)
                dp2 = _dot_nt(dos, vcat)
                diag2 = _dot(jnp.concatenate([_hi_lo(q2.astype(F32) * kp_b.astype(F32)),
                                              _hi_lo(do2.astype(F32) * vp_b.astype(F32))], axis=0), head_sum)
                p_rows, ds_rows, pe_h, dse_h = [], [], [], []
                for a in range(2):
                    hs = slice(a * QB, (a + 1) * QB)
                    sp, sc = s2[hs, :QB], s2[hs, QB:]
                    dpp, dpc = dp2[hs, :QB], dp2[hs, QB:]
                    lse_a, dd_a = lse_ref.at[a][cur, :], dd_s.at[a][cur, :]
                    comb = jnp.where(tri_le, sc, sp) - bias[a]
                    e = diag2[:QB, hs] - slope[a] * float(QB)
                    p = jnp.where(valid, jnp.exp(comb - lse_a), 0.0)
                    pe = jnp.where(has_prev, jnp.exp(e - lse_a), 0.0)
                    ds = p * (jnp.where(tri_le, dpc, dpp) - dd_a)
                    dse_h.append(pe * (diag2[QB:, hs] - dd_a))
                    pe_h.append(pe)
                    p_rows.append(jnp.concatenate([jnp.where(tri_le, 0.0, p).astype(BF),
                                                   jnp.where(tri_le, p, 0.0).astype(BF)], axis=1))
                    ds_rows.append(jnp.concatenate([jnp.where(tri_le, 0.0, ds).astype(BF),
                                                    jnp.where(tri_le, ds, 0.0).astype(BF)], axis=1))
                pst = jnp.concatenate(p_rows, axis=0)
                dst = jnp.concatenate(ds_rows, axis=0)
                pe2 = jnp.where(low, pe_h[0], pe_h[1])
                dse2 = jnp.where(low, dse_h[0], dse_h[1])
                dq = _dot(dst, kcat)
                dq_s[cur, :] += (jnp.where(low, dq[:QB], dq[QB:]) + dse2 * kp) * 0.125
                dk = _dot_tn(dst, qs)
                dv = _dot_tn(pst, dos)
                dk_s[prev, :] += dk[:QB] + dse2 * q2f
                dk_s[cur, :] += dk[QB:]
                dv_s[prev, :] += dv[:QB] + pe2 * do2f
                dv_s[cur, :] += dv[QB:]

            def several(it, carry, block=block):
                for u in range(ATT_UNROLL):
                    block(it * ATT_UNROLL + u)
                return carry

            lax.fori_loop(0, n_blocks // ATT_UNROLL, several, 0)

        def finish(i, carry):
            rows = pl.ds(pl.multiple_of(i * QB, QB), QB)
            out_ref[0, rows, :] = dq_s[rows, :].astype(BF)
            out_ref[1, rows, :] = dk_s[rows, :].astype(BF)
            out_ref[2, rows, :] = dv_s[rows, :].astype(BF)
            return carry

        lax.fori_loop(0, n_blocks, finish, 0)

        @pl.when(hp == hpr - 1)
        def _():
            for ex in exchanges:
                ex.finish()

    col = lambda s: pl.BlockSpec((S, 128), lambda h, s=s: (0, s * hpr + h))
    return pl.pallas_call(
        body, name="attn_bwd", grid=(hpr,),
        in_specs=[SMEM_SPEC, col(0), col(1), col(2), col(0), col(0), pl.BlockSpec((2, S, 128), lambda h: (0, 0, h)),
                  ANY_SPEC, ANY_SPEC],
        out_specs=(pl.BlockSpec((3, S, 128), lambda h: (0, 0, h)), ANY_SPEC, ANY_SPEC),
        out_shape=(jax.ShapeDtypeStruct((3, S, D), BF), jax.ShapeDtypeStruct(g_in.shape, BF),
                   jax.ShapeDtypeStruct(g_3.shape, BF)),
        scratch_shapes=([pltpu.VMEM((S, 128), F32)] * 3 + [pltpu.VMEM((2, S, 128), F32)]
                        + GRAD_EXCHANGE_SEMS + GRAD_EXCHANGE_SEMS),
        compiler_params=_params(1),
    )(slopes, qkv, qkv, qkv, do, o, lse, g_in, g_3)


WG_TN = 256
SEG0_CONV, SEG0_ATTN, SEG0_MID = 0, 4, 7


def _wgrad_in(ut, d_group, seg0, g_in, name):
    S = ut.shape[1]
    tn = WG_TN
    per_seg = D // tn
    per_shard = W_IN_SHARD // tn
    n_tiles = d_group.shape[0] * per_seg
    tile0 = seg0 * per_seg

    def body(ut_ref, d_ref, *rest):
        rest[-1][0] = _dot(ut_ref[...], d_ref[0]).astype(BF)

    operands, in_specs, aliases = [ut, d_group], [VMEM_SPEC, pl.BlockSpec((1, S, tn), lambda t: (t // per_seg, 0, t % per_seg))], {}
    if g_in is not None:
        operands.append(g_in)
        in_specs.append(ANY_SPEC)
        aliases = {2: 0}
    return pl.pallas_call(
        body, name=name, grid=(n_tiles,), in_specs=in_specs,
        out_specs=pl.BlockSpec((1, D, tn), lambda t: ((tile0 + t) // per_shard, 0, (tile0 + t) % per_shard)),
        out_shape=jax.ShapeDtypeStruct((N_DEV, D, W_IN_SHARD), BF),
        input_output_aliases=aliases,
        compiler_params=_params(1),
    )(*operands)


def _dgrad_norm_bwd(d_conv, d_attn, d_mid, w_all, x2, dh, norm_g, g_in, r_in):
    S = x2.shape[0]
    tm = ROW_TILE
    nsteps = S // tm
    tile = pl.BlockSpec((tm, D), lambda i: (i, 0))
    pieces = _proj_pieces()

    def body(a_ref, b_ref, c_ref, w_ref, x_ref, dh_ref, g_ref, gin_ref, rin_in_ref, gx_ref, small_ref, rin_ref, *sems):
        i = pl.program_id(0)
        exchange = _GradExchange(gin_ref, rin_ref, *sems, _shard_cols((SEG0_ATTN * D, SEG0_MID * D)))

        @pl.when(i == 0)
        def _():
            small_ref[...] = jnp.zeros_like(small_ref)
            exchange.start()

        groups = (a_ref, b_ref, c_ref)
        du = jnp.zeros((tm, D), F32)
        for s, sc, p, pc, width in pieces:
            g = 0 if s < 4 else (1 if s < 7 else 2)
            local = s - (0, 4, 7)[g]
            du = du + _dot_nt(groups[g][local, :, sc:sc + width], w_ref[p, :, pc:pc + width])
        xv = x_ref[...]
        r = lax.rsqrt(jnp.mean(xv * xv, axis=-1, keepdims=True) + EPS)
        n = xv * r
        dn = du * g_ref[...]
        gx_ref[...] = dh_ref[...] + r * (dn - n * jnp.mean(dn * n, axis=-1, keepdims=True))
        small_ref[...] += _set_rows((8, D), {0: jnp.sum(du * n, axis=0, keepdims=True)})

        @pl.when(i == nsteps - 1)
        def _():
            exchange.finish()

    return pl.pallas_call(
        body, name="dgrad_norm_bwd", grid=(nsteps,),
        in_specs=[pl.BlockSpec((4, tm, D), lambda i: (0, i, 0)), pl.BlockSpec((3, tm, D), lambda i: (0, i, 0)),
                  pl.BlockSpec((3, tm, D), lambda i: (0, i, 0)), VMEM_SPEC, tile, tile,
                  pl.BlockSpec((1, D), lambda i: (0, 0)), ANY_SPEC, ANY_SPEC],
        out_specs=(tile, pl.BlockSpec((8, D), lambda i: (0, 0)), ANY_SPEC),
        out_shape=(jax.ShapeDtypeStruct((S, D), F32), jax.ShapeDtypeStruct((8, D), F32),
                   jax.ShapeDtypeStruct(r_in.shape, BF)),
        scratch_shapes=GRAD_EXCHANGE_SEMS,
        input_output_aliases={8: 2},
        compiler_params=_params(1),
    )(d_conv, d_attn, d_mid, w_all, x2, dh, norm_g, g_in, r_in)


def _adamw_math(w, g, m, v):
    m = ADAM_B1 * m + (1.0 - ADAM_B1) * g
    v = ADAM_B2 * v + (1.0 - ADAM_B2) * (g * g)
    m_hat = m / (1.0 - ADAM_B1 ** ADAM_STEP)
    v_hat = v / (1.0 - ADAM_B2 ** ADAM_STEP)
    delta = -ADAM_LR * (m_hat / (jnp.sqrt(v_hat) + ADAM_EPS) + ADAM_WD * w)
    return delta, m, v


def _sum_adamw(parts, w, m, v, tm, name):
    R, C = w.shape
    tile = pl.BlockSpec((tm, C), lambda i: (i, 0))

    def body(p_ref, w_ref, m_ref, v_ref, g_out, d_out, m_out, v_out):
        g = p_ref[0].astype(F32)
        for s in range(1, N_DEV):
            g = g + p_ref[s].astype(F32)
        g_out[...] = g
        d_out[...], m_out[...], v_out[...] = _adamw_math(w_ref[...], g, m_ref[...], v_ref[...])

    shape = jax.ShapeDtypeStruct((R, C), F32)
    return pl.pallas_call(
        body, name=name, grid=(R // tm,),
        in_specs=[pl.BlockSpec((N_DEV, tm, C), lambda i: (0, i, 0)), tile, tile, tile],
        out_specs=(tile, tile, tile, tile), out_shape=(shape, shape, shape, shape),
        compiler_params=_params(1),
    )(parts, w, m, v)


def _adamw(g, w, m, v, name):
    def body(g_ref, w_ref, m_ref, v_ref, d_out, m_out, v_out):
        d_out[...], m_out[...], v_out[...] = _adamw_math(w_ref[...], g_ref[...], m_ref[...], v_ref[...])

    shape = jax.ShapeDtypeStruct(w.shape, F32)
    return pl.pallas_call(
        body, name=name, in_specs=[VMEM_SPEC] * 4, out_specs=(VMEM_SPEC,) * 3, out_shape=(shape, shape, shape),
    )(g, w, m, v)


def _alibi_slopes():
    return jnp.exp2(-8.0 * jnp.arange(1, N_HEADS + 1, dtype=F32) / N_HEADS)


def _local_step(x2, target, norm_g, b_merge, final_g, w_all, w3, cw8):
    slopes = _alibi_slopes()
    pa, qkv, ut = _norm_proj(x2, norm_g, w_all)
    yc_in = _conv_fwd(pa, cw8)
    o, lse, ya_in = _attn_fwd(qkv, pa, slopes)
    dh, d_mid, do, dyc_in, g_3, small_mid = _mid(yc_in, ya_in, pa, o, x2, target, b_merge, final_g, w3)
    g_in = _wgrad_in(ut, d_mid, SEG0_MID, None, "wgrad_in_mid")
    d_conv, small_conv = _conv_bwd(dyc_in, pa, cw8)
    g_in = _wgrad_in(ut, d_conv, SEG0_CONV, g_in, "wgrad_in_conv")
    d_attn, r_in, r_3 = _attn_bwd(qkv, slopes, do, o, lse, g_in, g_3)
    g_in = _wgrad_in(ut, d_attn, SEG0_ATTN, g_in, "wgrad_in_attn")
    grad_x, small_norm, r_in = _dgrad_norm_bwd(d_conv, d_attn, d_mid, w_all, x2, dh, norm_g, g_in, r_in)
    return grad_x, r_in, r_3, small_mid, small_conv, small_norm


def kernel(x, norm_g, w_in, b_merge, conv_w, w_out_conv, w_out_attn, w_o, final_g, loss_target, m_norm_g, m_w_in, m_b_merge, m_conv_w, m_w_out_conv, m_w_out_attn, m_w_o, m_final_g, v_norm_g, v_w_in, v_b_merge, v_conv_w, v_w_out_conv, v_w_out_attn, v_w_o, v_final_g):
    me = 4 * lax.axis_index("x") + 2 * lax.axis_index("y") + lax.axis_index("c")
    stack3 = lambda a, b, c: jnp.concatenate([a, b, c], axis=0)
    pad8 = lambda a: jnp.pad(a, ((0, 8 - a.shape[0]), (0, 0)))

    w3_shard = stack3(w_out_conv, w_out_attn, w_o)
    w_all, w3_all, cw_all = _allgather_weights(w_in[0], w3_shard, pad8(conv_w[0]))
    w3 = jnp.transpose(w3_all, (1, 0, 2, 3)).reshape(3, D, D)
    cw8 = jnp.transpose(cw_all, (1, 0, 2)).reshape(8, D)

    final_g2 = final_g.reshape(1, D)
    grad_x, r_in, r_3, small_mid, small_conv, small_norm = _local_step(
        x[0], loss_target[0], norm_g, b_merge, final_g2, w_all, w3, cw8)

    small = _allreduce_small(small_mid, small_conv, small_norm)

    g_w_in, d_w_in, nm_w_in, nv_w_in = _sum_adamw(r_in, w_in[0], m_w_in[0], v_w_in[0], 128, "adamw_w_in")
    g_w3, d_w3, nm_w3, nv_w3 = _sum_adamw(
        r_3.reshape(N_DEV, 3 * ROW_SHARD, D), w3_shard.reshape(3 * ROW_SHARD, D),
        stack3(m_w_out_conv, m_w_out_attn, m_w_o).reshape(3 * ROW_SHARD, D),
        stack3(v_w_out_conv, v_w_out_attn, v_w_o).reshape(3 * ROW_SHARD, D), ROW_SHARD, "adamw_w3")

    def pack(ng, bm, fg):
        return pad8(jnp.concatenate([ng, bm.reshape(2, D), fg.reshape(1, D)], axis=0))

    d_s, nm_s, nv_s = _adamw(small, pack(norm_g, b_merge, final_g), pack(m_norm_g, m_b_merge, m_final_g),
                             pack(v_norm_g, v_b_merge, v_final_g), "adamw_small")
    g_cw = lax.dynamic_slice(small, (4, me * ROW_SHARD), (3, ROW_SHARD))
    d_cw, nm_cw, nv_cw = _adamw(g_cw, conv_w[0], m_conv_w[0], v_conv_w[0], "adamw_conv_w")

    loss = small[7, 0]
    split3 = lambda t: tuple(t[a * ROW_SHARD:(a + 1) * ROW_SHARD][None] for a in range(3))
    unpack = lambda t: (t[0:1], t[1:3].reshape(1, 2 * D), t[3])

    def leaves(in_, small_, cw_, w3_):
        ng, bm, fg = unpack(small_)
        wc, wa, wo = split3(w3_)
        return (ng, in_[None], bm, cw_[None], wc, wa, wo, fg)

    return (loss, grad_x[None],
            *leaves(g_w_in, small, g_cw, g_w3),
            *leaves(d_w_in, d_s, d_cw, d_w3),
            *leaves(nm_w_in, nm_s, nm_cw, nm_w3),
            *leaves(nv_w_in, nv_s, nv_cw, nv_w3))
```

```python
import functools

import jax
import jax.numpy as jnp
from jax import lax
from jax.experimental import pallas as pl
from jax.experimental.pallas import tpu as pltpu

D = 1024
N_HEADS = 16
HEAD_DIM = 64
N_SEG = 10
IN_COLS = N_SEG * D
N_DEV = 8
W_IN_SHARD = IN_COLS // N_DEV
ROW_SHARD = D // N_DEV
QB = 128
DILATIONS = (1, 4, 16)
EPS = 1e-6
NEG = -1e30
BF = jnp.bfloat16
F32 = jnp.float32
MESH = pl.DeviceIdType.MESH

ADAM_LR = 0.001
ADAM_B1 = 0.9
ADAM_B2 = 0.999
ADAM_EPS = 1e-08
ADAM_WD = 0.01
ADAM_STEP = 10

V7X_VMEM_BYTES = 64 * 1024 * 1024
VMEM_LIMIT = V7X_VMEM_BYTES - 8 * 1024 * 1024
ROW_TILE = 256

VMEM_SPEC = pl.BlockSpec(memory_space=pltpu.VMEM)
ANY_SPEC = pl.BlockSpec(memory_space=pl.ANY)
SMEM_SPEC = pl.BlockSpec(memory_space=pltpu.SMEM)


def _params(n_grid_axes, vmem=VMEM_LIMIT):
    return pltpu.CompilerParams(dimension_semantics=("arbitrary",) * n_grid_axes, vmem_limit_bytes=vmem)


def _dot(a, b):
    return jnp.dot(a, b, preferred_element_type=F32)


def _dot_nt(a, b):
    return lax.dot_general(a, b, (((1,), (1,)), ((), ())), preferred_element_type=F32)


def _dot_tn(a, b):
    return lax.dot_general(a, b, (((0,), (0,)), ((), ())), preferred_element_type=F32)


def _sigmoid(z):
    return 1.0 / (1.0 + jnp.exp(-z))


def _my_place():
    x, y, c = lax.axis_index("x"), lax.axis_index("y"), lax.axis_index("c")
    return x, y, c, 4 * x + 2 * y + c


def _peers(x, y, c):
    out = []
    for k in range(1, N_DEV):
        px = 1 - x if k & 4 else x
        py = 1 - y if k & 2 else y
        pc = 1 - c if k & 1 else c
        out.append(((px, py, pc), 4 * px + 2 * py + pc))
    return out


def _device(p):
    return (p >> 2, (p >> 1) & 1, p & 1)


def _shard_cols(*ranges):
    def cols(p):
        found = None
        for lo, hi in ranges:
            a, b = max(lo, p * W_IN_SHARD), min(hi, (p + 1) * W_IN_SHARD)
            if a < b:
                assert found is None
                found = (a - p * W_IN_SHARD, b - p * W_IN_SHARD)
        return found

    return cols


def _whole(p):
    return ()


def _block(ref, idx, cols):
    return ref.at[idx] if cols == () else ref.at[idx, :, cols[0]:cols[1]]


class _WeightGather:
    def __init__(self, src, dst, send_sems, recv_sems, cols):
        self.src, self.dst, self.cols = src, dst, cols
        self.send_sems, self.recv_sems = send_sems, recv_sems
        self.me = _my_place()[3]

    def _copy(self, p, target):
        cols = self.cols(p)
        return pltpu.make_async_remote_copy(
            src_ref=self.src(p, cols), dst_ref=_block(self.dst, p, cols), send_sem=self.send_sems.at[target],
            recv_sem=self.recv_sems.at[p], device_id=_device(target), device_id_type=MESH)

    def _each(self, send, receive):
        for p in range(N_DEV):
            if self.cols(p) is None:
                continue

            def sender(p=p):
                for t in range(N_DEV):
                    if t != p:
                        send(self._copy(p, t))

            pl.when(self.me == p)(sender)
            pl.when(self.me != p)(lambda p=p: receive(self._copy(p, p)))

    def start(self):
        self._each(lambda cp: cp.start(), lambda cp: None)

    def finish(self):
        self._each(lambda cp: cp.wait_send(), lambda cp: cp.wait_recv())


WEIGHT_GATHER_SEMS = [pltpu.SemaphoreType.DMA((N_DEV,)), pltpu.SemaphoreType.DMA((N_DEV,))]
QKV_COLS = _shard_cols((4 * D, 7 * D))
REST_COLS = _shard_cols((0, 4 * D), (7 * D, IN_COLS))


def _gather_qkv_weights(w_in, w3, cw):
    def body(w_in_ref, w3_ref, cw_ref, o_in, o_3, o_cw, in_bf, w3_bf, local_sems, *sems):
        me = _my_place()[3]

        def cast_rows(i, carry):
            r = pl.multiple_of(i * 128, 128)
            in_bf[pl.ds(r, 128), :] = w_in_ref[pl.ds(r, 128), :].astype(BF)
            return carry

        lax.fori_loop(0, D // 128, cast_rows, 0)
        for a in range(3):
            w3_bf[a] = w3_ref[a].astype(BF)
        gather = _WeightGather(lambda p, cols: in_bf.at[:, cols[0]:cols[1]], o_in, *sems, QKV_COLS)
        gather.start()
        local = [pltpu.make_async_copy(src, dst.at[me], local_sems.at[a])
                 for a, (src, dst) in enumerate(((in_bf, o_in), (w3_bf, o_3), (cw_ref, o_cw)))]
        for cp in local:
            cp.start()
        gather.finish()
        for cp in local:
            cp.wait()

    return pl.pallas_call(
        body, name="gather_qkv_weights",
        out_shape=(jax.ShapeDtypeStruct((N_DEV, D, W_IN_SHARD), BF),
                   jax.ShapeDtypeStruct((N_DEV, 3, ROW_SHARD, D), BF),
                   jax.ShapeDtypeStruct((N_DEV, 8, 128), F32)),
        in_specs=[VMEM_SPEC, VMEM_SPEC, VMEM_SPEC],
        out_specs=(ANY_SPEC, ANY_SPEC, ANY_SPEC),
        scratch_shapes=[pltpu.VMEM((D, W_IN_SHARD), BF), pltpu.VMEM((3, ROW_SHARD, D), BF),
                        pltpu.SemaphoreType.DMA((3,))] + WEIGHT_GATHER_SEMS,
        compiler_params=pltpu.CompilerParams(vmem_limit_bytes=VMEM_LIMIT),
    )(w_in, w3, cw)


class _GradExchange:
    def __init__(self, src, dst, send_sems, recv_sems, local_sem, cols):
        self.src, self.dst, self.cols = src, dst, cols
        self.send_sems, self.recv_sems, self.local_sem = send_sems, recv_sems, local_sem
        self.me = _my_place()[3]

    def _remote(self, p, source):
        return pltpu.make_async_remote_copy(
            src_ref=_block(self.src, p, self.cols(p)), dst_ref=_block(self.dst, source, self.cols(p)),
            send_sem=self.send_sems.at[p], recv_sem=self.recv_sems.at[source],
            device_id=_device(p), device_id_type=MESH)

    def _local(self, p):
        return pltpu.make_async_copy(_block(self.src, p, self.cols(p)), _block(self.dst, p, self.cols(p)),
                                     self.local_sem)

    def start(self):
        for p in range(N_DEV):
            if self.cols(p) is None:
                continue
            pl.when(self.me != p)(lambda p=p: self._remote(p, self.me).start())
            pl.when(self.me == p)(lambda p=p: self._local(p).start())

    def finish(self):
        for p in range(N_DEV):
            if self.cols(p) is None:
                continue
            pl.when(self.me != p)(lambda p=p: self._remote(p, self.me).wait_send())

            def receive(p=p):
                self._local(p).wait()
                for s in range(N_DEV):
                    if s != p:
                        self._remote(p, s).wait_recv()

            pl.when(self.me == p)(receive)


GRAD_EXCHANGE_SEMS = [pltpu.SemaphoreType.DMA((N_DEV,)), pltpu.SemaphoreType.DMA((N_DEV,)), pltpu.SemaphoreType.DMA]


def _allreduce_small(p_mid, p_conv, p_norm):
    def body(a_ref, b_ref, c_ref, out_ref, mine, gathered, send_sems, recv_sems):
        x, y, c, me = _my_place()
        mine[...] = a_ref[...] + b_ref[...] + c_ref[...]
        gathered[me] = mine[...]
        remote = []
        for k, (peer, _) in enumerate(_peers(x, y, c)):
            cp = pltpu.make_async_remote_copy(
                src_ref=mine, dst_ref=gathered.at[me], send_sem=send_sems.at[k], recv_sem=recv_sems.at[k],
                device_id=peer, device_id_type=MESH)
            cp.start()
            remote.append(cp)
        for cp in remote:
            cp.wait()
        total = gathered[0]
        for s in range(1, N_DEV):
            total = total + gathered[s]
        out_ref[...] = total

    return pl.pallas_call(
        body, name="allreduce_small",
        out_shape=jax.ShapeDtypeStruct((8, D), F32),
        in_specs=[VMEM_SPEC, VMEM_SPEC, VMEM_SPEC], out_specs=VMEM_SPEC,
        scratch_shapes=[pltpu.VMEM((8, D), F32), pltpu.VMEM((N_DEV, 8, D), F32),
                        pltpu.SemaphoreType.DMA((N_DEV - 1,)), pltpu.SemaphoreType.DMA((N_DEV - 1,))],
    )(p_mid, p_conv, p_norm)


def _proj_pieces():
    cuts = sorted(set(range(0, IN_COLS + 1, D)) | set(range(0, IN_COLS + 1, W_IN_SHARD)))
    return [(lo // D, lo % D, lo // W_IN_SHARD, lo % W_IN_SHARD, hi - lo) for lo, hi in zip(cuts[:-1], cuts[1:])]


def _norm(x2, norm_g):
    S = x2.shape[0]
    tm = ROW_TILE

    def body(x_ref, g_ref, u_ref, ut_ref):
        xv = x_ref[...]
        r = lax.rsqrt(jnp.mean(xv * xv, axis=-1, keepdims=True) + EPS)
        u = xv * r * g_ref[...]
        u_ref[...] = u.astype(BF)
        ut_ref[...] = u.T.astype(BF)

    return pl.pallas_call(
        body, name="norm", grid=(S // tm,),
        in_specs=[pl.BlockSpec((tm, D), lambda i: (i, 0)), pl.BlockSpec((1, D), lambda i: (0, 0))],
        out_specs=(pl.BlockSpec((tm, D), lambda i: (i, 0)), pl.BlockSpec((D, tm), lambda i: (0, i))),
        out_shape=(jax.ShapeDtypeStruct((S, D), BF), jax.ShapeDtypeStruct((D, S), BF)),
        compiler_params=_params(1),
    )(x2, norm_g)


PROJ_TN = 256


def _proj_cols(u, w_all, seg0, n_seg, dtype, name):
    S = u.shape[0]
    tn = PROJ_TN
    per_shard = W_IN_SHARD // tn
    tile0 = seg0 * D // tn

    def body(u_ref, w_ref, out_ref):
        out_ref[...] = _dot(u_ref[...], w_ref[0]).astype(dtype)

    return pl.pallas_call(
        body, name=name, grid=(n_seg * D // tn,),
        in_specs=[VMEM_SPEC, pl.BlockSpec((1, D, tn), lambda t: ((tile0 + t) // per_shard, 0, (tile0 + t) % per_shard))],
        out_specs=pl.BlockSpec((S, tn), lambda t: (0, t)),
        out_shape=jax.ShapeDtypeStruct((S, n_seg * D), dtype),
        compiler_params=_params(1),
    )(u, w_all)


CONV_TM, CONV_TC = 256, 512
HALO = 16


def _conv_fwd(pa, cw8):
    S = pa.shape[0]
    tm, tc = CONV_TM, CONV_TC
    nct = D // tc

    def seg(s):
        return pl.BlockSpec((tm, tc), lambda i, j, s=s: (i, s * nct + j))

    def halo_before(s):
        return pl.BlockSpec((HALO, tc), lambda i, j, s=s: (jnp.maximum(i * (tm // HALO) - 1, 0), s * nct + j))

    def body(xc, bg, cg, zc, xch, cgh, cw, out):
        i = pl.program_id(0)
        a = cg[...].astype(F32) * xc[...].astype(F32)
        ah = cgh[...].astype(F32) * xch[...].astype(F32)
        ah = jnp.where(i > 0, ah, 0.0)
        row = lax.broadcasted_iota(jnp.int32, (tm, tc), 0)
        a1 = jnp.where(row == 0, ah[HALO - 1:HALO, :], pltpu.roll(a, 1, 0))
        a2 = jnp.where(row == 0, ah[HALO - 2:HALO - 1, :],
                       jnp.where(row == 1, ah[HALO - 1:HALO, :], pltpu.roll(a, 2, 0)))
        w = cw[...]
        conv = w[0:1, :] * a2 + w[1:2, :] * a1 + w[2:3, :] * a
        z = zc[...].astype(F32)
        out[...] = (z * _sigmoid(z) * bg[...].astype(F32) * conv).astype(BF)

    return pl.pallas_call(
        body, name="conv_fwd", grid=(S // tm, nct),
        in_specs=[seg(0), seg(1), seg(2), seg(3), halo_before(0), halo_before(2),
                  pl.BlockSpec((8, tc), lambda i, j: (0, j))],
        out_specs=pl.BlockSpec((tm, tc), lambda i, j: (i, j)),
        out_shape=jax.ShapeDtypeStruct((S, D), BF),
        compiler_params=_params(2),
    )(pa, pa, pa, pa, pa, pa, cw8)


ATT_UNROLL = 2


def _fold_masks():
    row = lax.broadcasted_iota(jnp.int32, (QB, QB), 0)
    lane = lax.broadcasted_iota(jnp.int32, (QB, QB), 1)
    tri_le = lane <= row
    dist = jnp.where(tri_le, row - lane, row - lane + QB).astype(F32)
    return tri_le, dist, lane < HEAD_DIM


def _block_rows(b, d, S):
    nb = S // (QB * d)
    r, n = b // nb, b % nb
    cur0 = r + n * (QB * d)
    prev0 = r + jnp.maximum(n - 1, 0) * (QB * d)
    if d == 1:
        return n, pl.ds(pl.multiple_of(cur0, QB), QB), pl.ds(pl.multiple_of(prev0, QB), QB)
    return n, pl.ds(cur0, QB, stride=d), pl.ds(prev0, QB, stride=d)


def _head_sum_matrix():
    r = lax.broadcasted_iota(jnp.int32, (2 * QB, 2 * QB), 0)
    c = lax.broadcasted_iota(jnp.int32, (2 * QB, 2 * QB), 1)
    return (((r % QB) // HEAD_DIM) == (c // QB)).astype(F32).astype(BF)


def _hi_lo(t):
    hi = t.astype(BF)
    return jnp.concatenate([hi, (t - hi.astype(F32)).astype(BF)], axis=1)


def _attn_fwd(qkv, slopes, w_all, w3_all, cw_all):
    S = qkv.shape[0]
    hpr = D // 128
    n_blocks = S // QB

    def body(sl_ref, q_ref, k_ref, v_ref, w_in_ref, w3_in_ref, cw_in_ref, o_ref, lse_ref, w_ref, w3_ref, cw_ref,
             acc, m_s, l_s, *sems):
        hp = pl.program_id(0)
        me = _my_place()[3]
        gathers = (_WeightGather(lambda p, cols: _block(w_ref, me, cols), w_ref, *sems[0:2], REST_COLS),
                   _WeightGather(lambda p, cols: w3_ref.at[me], w3_ref, *sems[2:4], _whole),
                   _WeightGather(lambda p, cols: cw_ref.at[me], cw_ref, *sems[4:6], _whole))

        @pl.when(hp == 0)
        def _():
            for g in gathers:
                g.start()

        tri_le, dist, low = _fold_masks()
        low_b = low.astype(F32).astype(BF)
        high_b = 1.0 - low_b
        head_sum = _head_sum_matrix()
        ones_b = jnp.ones((2 * QB, QB), BF)
        m_s[...] = jnp.full(m_s.shape, NEG, F32)
        l_s[...] = jnp.zeros(l_s.shape, F32)
        acc[...] = jnp.zeros(acc.shape, F32)

        for d in DILATIONS:
            slope = [sl_ref[2 * hp + a] * float(d) for a in range(2)]
            bias = [slope[a] * dist for a in range(2)]

            def block(b, d=d, slope=slope, bias=bias):
                n, cur, prev = _block_rows(b, d, S)
                has_prev = n > 0
                valid = jnp.logical_or(tri_le, has_prev)
                q2 = (q_ref[cur, :] * 0.125).astype(BF)
                qs = jnp.concatenate([q2 * low_b, q2 * high_b], axis=0)
                vp = v_ref[prev, :]
                kp_b = k_ref[prev, :].astype(BF)
                kcat = jnp.concatenate([kp_b, k_ref[cur, :].astype(BF)], axis=0)
                vcat = jnp.concatenate([vp, v_ref[cur, :]], axis=0).astype(BF)
                s2 = _dot_nt(qs, kcat)
                e2 = _dot(_hi_lo(q2.astype(F32) * kp_b.astype(F32)), head_sum)
                p_rows, alpha_h, pe_h = [], [], []
                for a in range(2):
                    sp, sc = s2[a * QB:(a + 1) * QB, :QB], s2[a * QB:(a + 1) * QB, QB:]
                    comb = jnp.where(valid, jnp.where(tri_le, sc, sp) - bias[a], NEG)
                    e = jnp.where(has_prev, e2[:, a * QB:(a + 1) * QB] - slope[a] * float(QB), NEG)
                    m_old = m_s.at[a][cur, :]
                    m_new = jnp.maximum(jnp.maximum(m_old, jnp.max(comb, axis=-1, keepdims=True)), e)
                    m_s.at[a][cur, :] = m_new
                    p = jnp.exp(comb - m_new)
                    pe_h.append(jnp.exp(e - m_new))
                    alpha_h.append(jnp.exp(m_old - m_new))
                    p_rows.append(jnp.concatenate([jnp.where(tri_le, 0.0, p).astype(BF),
                                                   jnp.where(tri_le, p, 0.0).astype(BF)], axis=1))
                pv = _dot(jnp.concatenate(p_rows, axis=0), jnp.concatenate([vcat, ones_b], axis=1))
                for a in range(2):
                    l_a = l_s.at[a]
                    l_a[cur, :] = alpha_h[a] * l_a[cur, :] + pv[a * QB:(a + 1) * QB, QB:] + pe_h[a]
                acc[cur, :] = (jnp.where(low, alpha_h[0], alpha_h[1]) * acc[cur, :]
                               + jnp.where(low, pv[:QB, :QB], pv[QB:, :QB]) + jnp.where(low, pe_h[0], pe_h[1]) * vp)

            def several(it, carry, block=block):
                for u in range(ATT_UNROLL):
                    block(it * ATT_UNROLL + u)
                return carry

            lax.fori_loop(0, n_blocks // ATT_UNROLL, several, 0)

        def finish(i, carry):
            rows = pl.ds(pl.multiple_of(i * QB, QB), QB)
            l0, l1 = l_s[0, rows, :], l_s[1, rows, :]
            o = acc[rows, :] / jnp.where(low, l0, l1)
            o_ref[rows, :] = o
            lse_ref[0, rows, :] = m_s[0, rows, :] + jnp.log(l0)
            lse_ref[1, rows, :] = m_s[1, rows, :] + jnp.log(l1)
            return carry

        lax.fori_loop(0, n_blocks, finish, 0)

        @pl.when(hp == hpr - 1)
        def _():
            for g in gathers:
                g.finish()

    col = lambda s: pl.BlockSpec((S, 128), lambda h, s=s: (0, s * hpr + h))
    gathered = (w_all, w3_all, cw_all)
    return pl.pallas_call(
        body, name="attn_fwd", grid=(hpr,),
        in_specs=[SMEM_SPEC, col(0), col(1), col(2), ANY_SPEC, ANY_SPEC, ANY_SPEC],
        out_specs=(col(0), pl.BlockSpec((2, S, 128), lambda h: (0, 0, h)), ANY_SPEC, ANY_SPEC, ANY_SPEC),
        out_shape=(jax.ShapeDtypeStruct((S, D), F32), jax.ShapeDtypeStruct((2, S, D), F32),
                   *[jax.ShapeDtypeStruct(t.shape, t.dtype) for t in gathered]),
        scratch_shapes=([pltpu.VMEM((S, 128), F32), pltpu.VMEM((2, S, 128), F32), pltpu.VMEM((2, S, 128), F32)]
                        + WEIGHT_GATHER_SEMS * 3),
        input_output_aliases={4: 2, 5: 3, 6: 4},
        compiler_params=_params(1),
    )(slopes, qkv, qkv, qkv, *gathered)


def _set_rows(shape, rows):
    idx = lax.broadcasted_iota(jnp.int32, shape, 0)
    out = jnp.zeros(shape, F32)
    for r, val in rows.items():
        out = out + jnp.where(idx == r, val, 0.0)
    return out


def _mid(yc_in, pa_mid, o, x2, target, b_merge, final_g, w3):
    S = x2.shape[0]
    tm = ROW_TILE
    nsteps = S // tm
    tile = pl.BlockSpec((tm, D), lambda i: (i, 0))

    def body(yc_ref, za_ref, gcp_ref, gap_ref, o_ref, x_ref, t_ref, b_ref, fg_ref, w_ref,
             dh_ref, dmid_ref, do_ref, dyc_ref, gw_ref, small_ref, acc, stage):
        i = pl.program_id(0)

        @pl.when(i == 0)
        def _():
            acc[...] = jnp.zeros_like(acc)
            small_ref[...] = jnp.zeros_like(small_ref)

        wc, wa, wo = w_ref[0], w_ref[1], w_ref[2]
        z = za_ref[...].astype(F32)
        sg = _sigmoid(z)
        ov = o_ref[...]
        yc_in_b, ya_in_b = yc_ref[...], (z * sg * ov).astype(BF)
        yc = _dot(yc_in_b, wc)
        ya = _dot(ya_in_b, wa)
        b = b_ref[...]
        gc = _sigmoid(gcp_ref[...].astype(F32) + b[:, :D])
        ga = _sigmoid(gap_ref[...].astype(F32) + b[:, D:])
        merged = gc * yc + ga * ya
        merged_b = merged.astype(BF)
        h = x_ref[...] + _dot(merged_b, wo)
        r2 = lax.rsqrt(jnp.mean(h * h, axis=-1, keepdims=True) + EPS)
        n = h * r2
        fg = fg_ref[...]
        err = n * fg - t_ref[...]
        loss = 0.5 * jnp.sum(jnp.sum(err * err, axis=-1, keepdims=True) / D, axis=0, keepdims=True)
        dy = err / D
        g_fg = jnp.sum(dy * n, axis=0, keepdims=True)
        dn = dy * fg
        dh = r2 * (dn - n * jnp.mean(dn * n, axis=-1, keepdims=True))
        dh_ref[...] = dh
        dh_b = dh.astype(BF)
        dmerged = _dot_nt(dh_b, wo)
        acc[2] += _dot(merged.T.astype(BF), dh_b)
        dyc = (dmerged * gc).astype(BF)
        dya = (dmerged * ga).astype(BF)
        dgcp = dmerged * yc * gc * (1.0 - gc)
        dgap = dmerged * ya * ga * (1.0 - ga)
        dmid_ref[1] = dgcp.astype(BF)
        dmid_ref[2] = dgap.astype(BF)
        acc[0] += _dot(yc_in_b.astype(F32).T.astype(BF), dyc)
        acc[1] += _dot(ya_in_b.astype(F32).T.astype(BF), dya)
        dyc_ref[...] = _dot_nt(dyc, wc).astype(BF)
        dya_in = _dot_nt(dya, wa)
        do_ref[...] = dya_in * (z * sg)
        dmid_ref[0] = (dya_in * ov * (sg * (1.0 + z * (1.0 - sg)))).astype(BF)
        small_ref[...] += _set_rows((8, D), {
            1: jnp.sum(dgcp, axis=0, keepdims=True), 2: jnp.sum(dgap, axis=0, keepdims=True),
            3: g_fg, 7: jnp.broadcast_to(loss, (1, D))})

        @pl.when(i == nsteps - 1)
        def _():
            for p in range(N_DEV):
                for a in range(3):
                    stage[...] = acc[a, p * ROW_SHARD:(p + 1) * ROW_SHARD, :].astype(BF)
                    pltpu.sync_copy(stage, gw_ref.at[p, a])

    return pl.pallas_call(
        body, name="mid", grid=(nsteps,),
        in_specs=[tile, pl.BlockSpec((tm, D), lambda i: (i, 0)), pl.BlockSpec((tm, D), lambda i: (i, 1)),
                  pl.BlockSpec((tm, D), lambda i: (i, 2)), tile, tile, tile,
                  pl.BlockSpec((1, 2 * D), lambda i: (0, 0)), pl.BlockSpec((1, D), lambda i: (0, 0)), VMEM_SPEC],
        out_specs=(tile, pl.BlockSpec((3, tm, D), lambda i: (0, i, 0)), tile, tile,
                   ANY_SPEC, pl.BlockSpec((8, D), lambda i: (0, 0))),
        out_shape=(jax.ShapeDtypeStruct((S, D), F32), jax.ShapeDtypeStruct((3, S, D), BF),
                   jax.ShapeDtypeStruct((S, D), F32), jax.ShapeDtypeStruct((S, D), BF),
                   jax.ShapeDtypeStruct((N_DEV, 3, ROW_SHARD, D), BF), jax.ShapeDtypeStruct((8, D), F32)),
        scratch_shapes=[pltpu.VMEM((3, D, D), F32), pltpu.VMEM((ROW_SHARD, D), BF)],
        compiler_params=_params(1),
    )(yc_in, pa_mid, pa_mid, pa_mid, o, x2, target, b_merge, final_g, w3)


def _conv_bwd(dyc_in, pa, cw8):
    S = pa.shape[0]
    tm, tc = CONV_TM, CONV_TC
    nct = D // tc
    nrt = S // tm
    last_halo = S // HALO - 1

    def seg(s):
        return pl.BlockSpec((tm, tc), lambda j, i, s=s: (i, s * nct + j))

    def halo_before(s):
        return pl.BlockSpec((HALO, tc), lambda j, i, s=s: (jnp.maximum(i * (tm // HALO) - 1, 0), s * nct + j))

    def halo_after(s):
        return pl.BlockSpec((HALO, tc), lambda j, i, s=s: (jnp.minimum((i + 1) * (tm // HALO), last_halo), s * nct + j))

    def body(dy, xc, bg, cg, zc, xch, cgh, dyn, bgn, zcn, cw, dout, gcw):
        i = pl.program_id(1)

        @pl.when(i == 0)
        def _():
            gcw[...] = jnp.zeros_like(gcw)

        xcv, cgv = xc[...].astype(F32), cg[...].astype(F32)
        a = cgv * xcv
        ah = jnp.where(i > 0, cgh[...].astype(F32) * xch[...].astype(F32), 0.0)
        row = lax.broadcasted_iota(jnp.int32, (tm, tc), 0)
        a1 = jnp.where(row == 0, ah[HALO - 1:HALO, :], pltpu.roll(a, 1, 0))
        a2 = jnp.where(row == 0, ah[HALO - 2:HALO - 1, :],
                       jnp.where(row == 1, ah[HALO - 1:HALO, :], pltpu.roll(a, 2, 0)))
        w = cw[...]
        conv = w[0:1, :] * a2 + w[1:2, :] * a1 + w[2:3, :] * a
        z = zc[...].astype(F32)
        sg = _sigmoid(z)
        silu = z * sg
        bgv = bg[...].astype(F32)
        dyv = dy[...].astype(F32)
        dout[3] = (dyv * bgv * conv * (sg * (1.0 + z * (1.0 - sg)))).astype(BF)
        dout[1] = (dyv * silu * conv).astype(BF)
        dc = dyv * silu * bgv
        zn = zcn[...].astype(F32)
        dcn = dyn[...].astype(F32) * (zn * _sigmoid(zn)) * bgn[...].astype(F32)
        dcn = jnp.where(i < nrt - 1, dcn, 0.0)
        dc1 = jnp.where(row == tm - 1, dcn[0:1, :], pltpu.roll(dc, tm - 1, 0))
        dc2 = jnp.where(row == tm - 1, dcn[1:2, :],
                        jnp.where(row == tm - 2, dcn[0:1, :], pltpu.roll(dc, tm - 2, 0)))
        da = w[2:3, :] * dc + w[1:2, :] * dc1 + w[0:1, :] * dc2
        dout[2] = (da * xcv).astype(BF)
        dout[0] = (da * cgv).astype(BF)
        gcw[...] += _set_rows((8, tc), {
            4: jnp.sum(dc * a2, axis=0, keepdims=True), 5: jnp.sum(dc * a1, axis=0, keepdims=True),
            6: jnp.sum(dc * a, axis=0, keepdims=True)})

    return pl.pallas_call(
        body, name="conv_bwd", grid=(nct, nrt),
        in_specs=[pl.BlockSpec((tm, tc), lambda j, i: (i, j)), seg(0), seg(1), seg(2), seg(3),
                  halo_before(0), halo_before(2),
                  pl.BlockSpec((HALO, tc), lambda j, i: (jnp.minimum((i + 1) * (tm // HALO), last_halo), j)),
                  halo_after(1), halo_after(3), pl.BlockSpec((8, tc), lambda j, i: (0, j))],
        out_specs=(pl.BlockSpec((4, tm, tc), lambda j, i: (0, i, j)), pl.BlockSpec((8, tc), lambda j, i: (0, j))),
        out_shape=(jax.ShapeDtypeStruct((4, S, D), BF), jax.ShapeDtypeStruct((8, D), F32)),
        compiler_params=_params(2),
    )(dyc_in, pa, pa, pa, pa, pa, pa, dyc_in, pa, pa, cw8)


def _attn_bwd(qkv, slopes, do, o, lse, g_in, g_3):
    S = qkv.shape[0]
    hpr = D // 128
    n_blocks = S // QB

    def body(sl_ref, q_ref, k_ref, v_ref, do_ref, o_ref, lse_ref, gin_ref, g3_ref, out_ref, rin_ref, r3_ref,
             dq_s, dk_s, dv_s, dd_s, *sems):
        hp = pl.program_id(0)
        exchanges = (_GradExchange(gin_ref, rin_ref, *sems[:3], _shard_cols((0, SEG0_ATTN * D), (SEG0_MID * D, IN_COLS))),
                     _GradExchange(g3_ref, r3_ref, *sems[3:], lambda p: ()))

        @pl.when(hp == 0)
        def _():
            for ex in exchanges:
                ex.start()

        tri_le, dist, low = _fold_masks()
        low_b = low.astype(F32).astype(BF)
        high_b = 1.0 - low_b
        head_sum = _head_sum_matrix()
        dq_s[...] = jnp.zeros(dq_s.shape, F32)
        dk_s[...] = jnp.zeros(dk_s.shape, F32)
        dv_s[...] = jnp.zeros(dv_s.shape, F32)

        def row_dots(i, carry):
            rows = pl.ds(pl.multiple_of(i * QB, QB), QB)
            dd = _dot(_hi_lo(do_ref[rows, :] * o_ref[rows, :]), head_sum)
            dd_s[0, rows, :] = dd[:, :QB]
            dd_s[1, rows, :] = dd[:, QB:]
            return carry

        lax.fori_loop(0, n_blocks, row_dots, 0)

        for d in DILATIONS:
            slope = [sl_ref[2 * hp + a] * float(d) for a in range(2)]
            bias = [slope[a] * dist for a in range(2)]

            def block(b, d=d, slope=slope, bias=bias):
                n, cur, prev = _block_rows(b, d, S)
                has_prev = n > 0
                valid = jnp.logical_or(tri_le, has_prev)
                q2f = q_ref[cur, :] * 0.125
                q2 = q2f.astype(BF)
                qs = jnp.concatenate([q2 * low_b, q2 * high_b], axis=0)
                kp, vp = k_ref[prev, :], v_ref[prev, :]
                kp_b, vp_b = kp.astype(BF), vp.astype(BF)
                kcat = jnp.concatenate([kp_b, k_ref[cur, :].astype(BF)], axis=0)
                vcat = jnp.concatenate([vp_b, v_ref[cur, :].astype(BF)], axis=0)
                do2f = do_ref[cur, :]
                do2 = do2f.astype(BF)
                dos = jnp.concatenate([do2 * low_b, do2 * high_b], axis=0)
                s2 = _dot_nt(qs, kcat)
                dp2 = _dot_nt(dos, vcat)
                diag2 = _dot(jnp.concatenate([_hi_lo(q2.astype(F32) * kp_b.astype(F32)),
                                              _hi_lo(do2.astype(F32) * vp_b.astype(F32))], axis=0), head_sum)
                p_rows, ds_rows, pe_h, dse_h = [], [], [], []
                for a in range(2):
                    hs = slice(a * QB, (a + 1) * QB)
                    sp, sc = s2[hs, :QB], s2[hs, QB:]
                    dpp, dpc = dp2[hs, :QB], dp2[hs, QB:]
                    lse_a, dd_a = lse_ref.at[a][cur, :], dd_s.at[a][cur, :]
                    comb = jnp.where(tri_le, sc, sp) - bias[a]
                    e = diag2[:QB, hs] - slope[a] * float(QB)
                    p = jnp.where(valid, jnp.exp(comb - lse_a), 0.0)
                    pe = jnp.where(has_prev, jnp.exp(e - lse_a), 0.0)
                    ds = p * (jnp.where(tri_le, dpc, dpp) - dd_a)
                    dse_h.append(pe * (diag2[QB:, hs] - dd_a))
                    pe_h.append(pe)
                    p_rows.append(jnp.concatenate([jnp.where(tri_le, 0.0, p).astype(BF),
                                                   jnp.where(tri_le, p, 0.0).astype(BF)], axis=1))
                    ds_rows.append(jnp.concatenate([jnp.where(tri_le, 0.0, ds).astype(BF),
                                                    jnp.where(tri_le, ds, 0.0).astype(BF)], axis=1))
                pst = jnp.concatenate(p_rows, axis=0)
                dst = jnp.concatenate(ds_rows, axis=0)
                pe2 = jnp.where(low, pe_h[0], pe_h[1])
                dse2 = jnp.where(low, dse_h[0], dse_h[1])
                dq = _dot(dst, kcat)
                dq_s[cur, :] += (jnp.where(low, dq[:QB], dq[QB:]) + dse2 * kp) * 0.125
                dk = _dot_tn(dst, qs)
                dv = _dot_tn(pst, dos)
                dk_s[prev, :] += dk[:QB] + dse2 * q2f
                dk_s[cur, :] += dk[QB:]
                dv_s[prev, :] += dv[:QB] + pe2 * do2f
                dv_s[cur, :] += dv[QB:]

            def several(it, carry, block=block):
                for u in range(ATT_UNROLL):
                    block(it * ATT_UNROLL + u)
                return carry

            lax.fori_loop(0, n_blocks // ATT_UNROLL, several, 0)

        def finish(i, carry):
            rows = pl.ds(pl.multiple_of(i * QB, QB), QB)
            out_ref[0, rows, :] = dq_s[rows, :].astype(BF)
            out_ref[1, rows, :] = dk_s[rows, :].astype(BF)
            out_ref[2, rows, :] = dv_s[rows, :].astype(BF)
            return carry

        lax.fori_loop(0, n_blocks, finish, 0)

        @pl.when(hp == hpr - 1)
        def _():
            for ex in exchanges:
                ex.finish()

    col = lambda s: pl.BlockSpec((S, 128), lambda h, s=s: (0, s * hpr + h))
    return pl.pallas_call(
        body, name="attn_bwd", grid=(hpr,),
        in_specs=[SMEM_SPEC, col(0), col(1), col(2), col(0), col(0), pl.BlockSpec((2, S, 128), lambda h: (0, 0, h)),
                  ANY_SPEC, ANY_SPEC],
        out_specs=(pl.BlockSpec((3, S, 128), lambda h: (0, 0, h)), ANY_SPEC, ANY_SPEC),
        out_shape=(jax.ShapeDtypeStruct((3, S, D), BF), jax.ShapeDtypeStruct(g_in.shape, BF),
                   jax.ShapeDtypeStruct(g_3.shape, BF)),
        scratch_shapes=([pltpu.VMEM((S, 128), F32)] * 3 + [pltpu.VMEM((2, S, 128), F32)]
                        + GRAD_EXCHANGE_SEMS + GRAD_EXCHANGE_SEMS),
        compiler_params=_params(1),
    )(slopes, qkv, qkv, qkv, do, o, lse, g_in, g_3)


WG_TN = 256
SEG0_CONV, SEG0_ATTN, SEG0_MID = 0, 4, 7


def _wgrad_in(ut, d_group, seg0, g_in, name):
    S = ut.shape[1]
    tn = WG_TN
    per_seg = D // tn
    per_shard = W_IN_SHARD // tn
    n_tiles = d_group.shape[0] * per_seg
    tile0 = seg0 * per_seg

    def body(ut_ref, d_ref, *rest):
        rest[-1][0] = _dot(ut_ref[...], d_ref[0]).astype(BF)

    operands, in_specs, aliases = [ut, d_group], [VMEM_SPEC, pl.BlockSpec((1, S, tn), lambda t: (t // per_seg, 0, t % per_seg))], {}
    if g_in is not None:
        operands.append(g_in)
        in_specs.append(ANY_SPEC)
        aliases = {2: 0}
    return pl.pallas_call(
        body, name=name, grid=(n_tiles,), in_specs=in_specs,
        out_specs=pl.BlockSpec((1, D, tn), lambda t: ((tile0 + t) // per_shard, 0, (tile0 + t) % per_shard)),
        out_shape=jax.ShapeDtypeStruct((N_DEV, D, W_IN_SHARD), BF),
        input_output_aliases=aliases,
        compiler_params=_params(1),
    )(*operands)


def _dgrad_norm_bwd(d_conv, d_attn, d_mid, w_all, x2, dh, norm_g, g_in, r_in):
    S = x2.shape[0]
    tm = ROW_TILE
    nsteps = S // tm
    tile = pl.BlockSpec((tm, D), lambda i: (i, 0))
    pieces = _proj_pieces()

    def body(a_ref, b_ref, c_ref, w_ref, x_ref, dh_ref, g_ref, gin_ref, rin_in_ref, gx_ref, small_ref, rin_ref, *sems):
        i = pl.program_id(0)
        exchange = _GradExchange(gin_ref, rin_ref, *sems, _shard_cols((SEG0_ATTN * D, SEG0_MID * D)))

        @pl.when(i == 0)
        def _():
            small_ref[...] = jnp.zeros_like(small_ref)
            exchange.start()

        groups = (a_ref, b_ref, c_ref)
        du = jnp.zeros((tm, D), F32)
        for s, sc, p, pc, width in pieces:
            g = 0 if s < 4 else (1 if s < 7 else 2)
            local = s - (0, 4, 7)[g]
            du = du + _dot_nt(groups[g][local, :, sc:sc + width], w_ref[p, :, pc:pc + width])
        xv = x_ref[...]
        r = lax.rsqrt(jnp.mean(xv * xv, axis=-1, keepdims=True) + EPS)
        n = xv * r
        dn = du * g_ref[...]
        gx_ref[...] = dh_ref[...] + r * (dn - n * jnp.mean(dn * n, axis=-1, keepdims=True))
        small_ref[...] += _set_rows((8, D), {0: jnp.sum(du * n, axis=0, keepdims=True)})

        @pl.when(i == nsteps - 1)
        def _():
            exchange.finish()

    return pl.pallas_call(
        body, name="dgrad_norm_bwd", grid=(nsteps,),
        in_specs=[pl.BlockSpec((4, tm, D), lambda i: (0, i, 0)), pl.BlockSpec((3, tm, D), lambda i: (0, i, 0)),
                  pl.BlockSpec((3, tm, D), lambda i: (0, i, 0)), VMEM_SPEC, tile, tile,
                  pl.BlockSpec((1, D), lambda i: (0, 0)), ANY_SPEC, ANY_SPEC],
        out_specs=(tile, pl.BlockSpec((8, D), lambda i: (0, 0)), ANY_SPEC),
        out_shape=(jax.ShapeDtypeStruct((S, D), F32), jax.ShapeDtypeStruct((8, D), F32),
                   jax.ShapeDtypeStruct(r_in.shape, BF)),
        scratch_shapes=GRAD_EXCHANGE_SEMS,
        input_output_aliases={8: 2},
        compiler_params=_params(1),
    )(d_conv, d_attn, d_mid, w_all, x2, dh, norm_g, g_in, r_in)


def _adamw_math(w, g, m, v):
    m = ADAM_B1 * m + (1.0 - ADAM_B1) * g
    v = ADAM_B2 * v + (1.0 - ADAM_B2) * (g * g)
    m_hat = m / (1.0 - ADAM_B1 ** ADAM_STEP)
    v_hat = v / (1.0 - ADAM_B2 ** ADAM_STEP)
    delta = -ADAM_LR * (m_hat / (jnp.sqrt(v_hat) + ADAM_EPS) + ADAM_WD * w)
    return delta, m, v


def _sum_adamw(parts, w, m, v, tm, name):
    R, C = w.shape
    tile = pl.BlockSpec((tm, C), lambda i: (i, 0))

    def body(p_ref, w_ref, m_ref, v_ref, g_out, d_out, m_out, v_out):
        g = p_ref[0].astype(F32)
        for s in range(1, N_DEV):
            g = g + p_ref[s].astype(F32)
        g_out[...] = g
        d_out[...], m_out[...], v_out[...] = _adamw_math(w_ref[...], g, m_ref[...], v_ref[...])

    shape = jax.ShapeDtypeStruct((R, C), F32)
    return pl.pallas_call(
        body, name=name, grid=(R // tm,),
        in_specs=[pl.BlockSpec((N_DEV, tm, C), lambda i: (0, i, 0)), tile, tile, tile],
        out_specs=(tile, tile, tile, tile), out_shape=(shape, shape, shape, shape),
        compiler_params=_params(1),
    )(parts, w, m, v)


def _adamw(g, w, m, v, name):
    def body(g_ref, w_ref, m_ref, v_ref, d_out, m_out, v_out):
        d_out[...], m_out[...], v_out[...] = _adamw_math(w_ref[...], g_ref[...], m_ref[...], v_ref[...])

    shape = jax.ShapeDtypeStruct(w.shape, F32)
    return pl.pallas_call(
        body, name=name, in_specs=[VMEM_SPEC] * 4, out_specs=(VMEM_SPEC,) * 3, out_shape=(shape, shape, shape),
    )(g, w, m, v)


def _alibi_slopes():
    return jnp.exp2(-8.0 * jnp.arange(1, N_HEADS + 1, dtype=F32) / N_HEADS)


def _local_step(x2, target, norm_g, b_merge, final_g, w_all, w3_all, cw_all):
    slopes = _alibi_slopes()
    u, ut = _norm(x2, norm_g)
    qkv = _proj_cols(u, w_all, SEG0_ATTN, 3, F32, "proj_qkv")
    o, lse, w_all, w3_all, cw_all = _attn_fwd(qkv, slopes, w_all, w3_all, cw_all)
    w3 = jnp.transpose(w3_all, (1, 0, 2, 3)).reshape(3, D, D)
    cw8 = jnp.transpose(cw_all, (1, 0, 2)).reshape(8, D)
    pa = _proj_cols(u, w_all, SEG0_CONV, 4, BF, "proj_conv")
    yc_in = _conv_fwd(pa, cw8)
    pa_mid = _proj_cols(u, w_all, SEG0_MID, 3, BF, "proj_mid")
    dh, d_mid, do, dyc_in, g_3, small_mid = _mid(yc_in, pa_mid, o, x2, target, b_merge, final_g, w3)
    g_in = _wgrad_in(ut, d_mid, SEG0_MID, None, "wgrad_in_mid")
    d_conv, small_conv = _conv_bwd(dyc_in, pa, cw8)
    g_in = _wgrad_in(ut, d_conv, SEG0_CONV, g_in, "wgrad_in_conv")
    d_attn, r_in, r_3 = _attn_bwd(qkv, slopes, do, o, lse, g_in, g_3)
    g_in = _wgrad_in(ut, d_attn, SEG0_ATTN, g_in, "wgrad_in_attn")
    grad_x, small_norm, r_in = _dgrad_norm_bwd(d_conv, d_attn, d_mid, w_all, x2, dh, norm_g, g_in, r_in)
    return grad_x, r_in, r_3, small_mid, small_conv, small_norm


def kernel(x, norm_g, w_in, b_merge, conv_w, w_out_conv, w_out_attn, w_o, final_g, loss_target, m_norm_g, m_w_in, m_b_merge, m_conv_w, m_w_out_conv, m_w_out_attn, m_w_o, m_final_g, v_norm_g, v_w_in, v_b_merge, v_conv_w, v_w_out_conv, v_w_out_attn, v_w_o, v_final_g):
    me = 4 * lax.axis_index("x") + 2 * lax.axis_index("y") + lax.axis_index("c")
    stack3 = lambda a, b, c: jnp.concatenate([a, b, c], axis=0)
    pad8 = lambda a: jnp.pad(a, ((0, 8 - a.shape[0]), (0, 0)))

    w3_shard = stack3(w_out_conv, w_out_attn, w_o)
    w_all, w3_all, cw_all = _gather_qkv_weights(w_in[0], w3_shard, pad8(conv_w[0]))

    final_g2 = final_g.reshape(1, D)
    grad_x, r_in, r_3, small_mid, small_conv, small_norm = _local_step(
        x[0], loss_target[0], norm_g, b_merge, final_g2, w_all, w3_all, cw_all)

    small = _allreduce_small(small_mid, small_conv, small_norm)

    g_w_in, d_w_in, nm_w_in, nv_w_in = _sum_adamw(r_in, w_in[0], m_w_in[0], v_w_in[0], 128, "adamw_w_in")
    g_w3, d_w3, nm_w3, nv_w3 = _sum_adamw(
        r_3.reshape(N_DEV, 3 * ROW_SHARD, D), w3_shard.reshape(3 * ROW_SHARD, D),
        stack3(m_w_out_conv, m_w_out_attn, m_w_o).reshape(3 * ROW_SHARD, D),
        stack3(v_w_out_conv, v_w_out_attn, v_w_o).reshape(3 * ROW_SHARD, D), ROW_SHARD, "adamw_w3")

    def pack(ng, bm, fg):
        return pad8(jnp.concatenate([ng, bm.reshape(2, D), fg.reshape(1, D)], axis=0))

    d_s, nm_s, nv_s = _adamw(small, pack(norm_g, b_merge, final_g), pack(m_norm_g, m_b_merge, m_final_g),
                             pack(v_norm_g, v_b_merge, v_final_g), "adamw_small")
    g_cw = lax.dynamic_slice(small, (4, me * ROW_SHARD), (3, ROW_SHARD))
    d_cw, nm_cw, nv_cw = _adamw(g_cw, conv_w[0], m_conv_w[0], v_conv_w[0], "adamw_conv_w")

    loss = small[7, 0]
    split3 = lambda t: tuple(t[a * ROW_SHARD:(a + 1) * ROW_SHARD][None] for a in range(3))
    unpack = lambda t: (t[0:1], t[1:3].reshape(1, 2 * D), t[3])

    def leaves(in_, small_, cw_, w3_):
        ng, bm, fg = unpack(small_)
        wc, wa, wo = split3(w3_)
        return (ng, in_[None], bm, cw_[None], wc, wa, wo, fg)

    return (loss, grad_x[None],
            *leaves(g_w_in, small, g_cw, g_w3),
            *leaves(d_w_in, d_s, d_cw, d_w3),
            *leaves(nm_w_in, nm_s, nm_cw, nm_w3),
            *leaves(nv_w_in, nv_s, nv_cw, nv_w3))
```

```python
import functools

import jax
import jax.numpy as jnp
from jax import lax
from jax.experimental import pallas as pl
from jax.experimental.pallas import tpu as pltpu

D = 1024
N_HEADS = 16
HEAD_DIM = 64
N_SEG = 10
IN_COLS = N_SEG * D
N_DEV = 8
W_IN_SHARD = IN_COLS // N_DEV
ROW_SHARD = D // N_DEV
QB = 128
DILATIONS = (1, 4, 16)
EPS = 1e-6
NEG = -1e30
BF = jnp.bfloat16
F32 = jnp.float32
MESH = pl.DeviceIdType.MESH

ADAM_LR = 0.001
ADAM_B1 = 0.9
ADAM_B2 = 0.999
ADAM_EPS = 1e-08
ADAM_WD = 0.01
ADAM_STEP = 10

V7X_VMEM_BYTES = 64 * 1024 * 1024
VMEM_LIMIT = V7X_VMEM_BYTES - 8 * 1024 * 1024
ROW_TILE = 256

VMEM_SPEC = pl.BlockSpec(memory_space=pltpu.VMEM)
ANY_SPEC = pl.BlockSpec(memory_space=pl.ANY)
SMEM_SPEC = pl.BlockSpec(memory_space=pltpu.SMEM)


def _params(n_grid_axes, vmem=VMEM_LIMIT):
    return pltpu.CompilerParams(dimension_semantics=("arbitrary",) * n_grid_axes, vmem_limit_bytes=vmem)


def _dot(a, b):
    return jnp.dot(a, b, preferred_element_type=F32)


def _dot_nt(a, b):
    return lax.dot_general(a, b, (((1,), (1,)), ((), ())), preferred_element_type=F32)


def _dot_tn(a, b):
    return lax.dot_general(a, b, (((0,), (0,)), ((), ())), preferred_element_type=F32)


def _sigmoid(z):
    return 1.0 / (1.0 + jnp.exp(-z))


def _my_place():
    x, y, c = lax.axis_index("x"), lax.axis_index("y"), lax.axis_index("c")
    return x, y, c, 4 * x + 2 * y + c


def _peers(x, y, c):
    out = []
    for k in range(1, N_DEV):
        px = 1 - x if k & 4 else x
        py = 1 - y if k & 2 else y
        pc = 1 - c if k & 1 else c
        out.append(((px, py, pc), 4 * px + 2 * py + pc))
    return out


def _device(p):
    return (p >> 2, (p >> 1) & 1, p & 1)


def _shard_cols(*ranges):
    def cols(p):
        found = None
        for lo, hi in ranges:
            a, b = max(lo, p * W_IN_SHARD), min(hi, (p + 1) * W_IN_SHARD)
            if a < b:
                assert found is None
                found = (a - p * W_IN_SHARD, b - p * W_IN_SHARD)
        return found

    return cols


def _whole(p):
    return ()


def _block(ref, idx, cols):
    return ref.at[idx] if cols == () else ref.at[idx, :, cols[0]:cols[1]]


class _WeightGather:
    def __init__(self, src, dst, send_sems, recv_sems, cols):
        self.src, self.dst, self.cols = src, dst, cols
        self.send_sems, self.recv_sems = send_sems, recv_sems
        self.me = _my_place()[3]

    def _copy(self, p, target):
        cols = self.cols(p)
        return pltpu.make_async_remote_copy(
            src_ref=self.src(p, cols), dst_ref=_block(self.dst, p, cols), send_sem=self.send_sems.at[target],
            recv_sem=self.recv_sems.at[p], device_id=_device(target), device_id_type=MESH)

    def _each(self, send, receive):
        for p in range(N_DEV):
            if self.cols(p) is None:
                continue

            def sender(p=p):
                for t in range(N_DEV):
                    if t != p:
                        send(self._copy(p, t), (t - p) % N_DEV - 1)

            pl.when(self.me == p)(sender)
            pl.when(self.me != p)(lambda p=p: receive(self._copy(p, p)))

    def start(self, step=None):
        if step is None:
            self._each(lambda cp, slot: cp.start(), lambda cp: None)
        else:
            self._each(lambda cp, slot: pl.when(step == slot)(cp.start), lambda cp: None)

    def finish(self):
        self._each(lambda cp, slot: cp.wait_send(), lambda cp: cp.wait_recv())


WEIGHT_GATHER_SEMS = [pltpu.SemaphoreType.DMA((N_DEV,)), pltpu.SemaphoreType.DMA((N_DEV,))]
QKV_COLS = _shard_cols((4 * D, 7 * D))
REST_COLS = _shard_cols((0, 4 * D), (7 * D, IN_COLS))


def _gather_qkv_weights(w_in, w3, cw):
    def body(w_in_ref, w3_ref, cw_ref, o_in, o_3, o_cw, in_bf, w3_bf, local_sems, *sems):
        me = _my_place()[3]

        def cast_rows(i, carry):
            r = pl.multiple_of(i * 128, 128)
            in_bf[pl.ds(r, 128), :] = w_in_ref[pl.ds(r, 128), :].astype(BF)
            return carry

        lax.fori_loop(0, D // 128, cast_rows, 0)
        for a in range(3):
            w3_bf[a] = w3_ref[a].astype(BF)
        gather = _WeightGather(lambda p, cols: in_bf.at[:, cols[0]:cols[1]], o_in, *sems, QKV_COLS)
        gather.start()
        local = [pltpu.make_async_copy(src, dst.at[me], local_sems.at[a])
                 for a, (src, dst) in enumerate(((in_bf, o_in), (w3_bf, o_3), (cw_ref, o_cw)))]
        for cp in local:
            cp.start()
        gather.finish()
        for cp in local:
            cp.wait()

    return pl.pallas_call(
        body, name="gather_qkv_weights",
        out_shape=(jax.ShapeDtypeStruct((N_DEV, D, W_IN_SHARD), BF),
                   jax.ShapeDtypeStruct((N_DEV, 3, ROW_SHARD, D), BF),
                   jax.ShapeDtypeStruct((N_DEV, 8, 128), F32)),
        in_specs=[VMEM_SPEC, VMEM_SPEC, VMEM_SPEC],
        out_specs=(ANY_SPEC, ANY_SPEC, ANY_SPEC),
        scratch_shapes=[pltpu.VMEM((D, W_IN_SHARD), BF), pltpu.VMEM((3, ROW_SHARD, D), BF),
                        pltpu.SemaphoreType.DMA((3,))] + WEIGHT_GATHER_SEMS,
        compiler_params=pltpu.CompilerParams(vmem_limit_bytes=VMEM_LIMIT),
    )(w_in, w3, cw)


class _GradExchange:
    def __init__(self, src, dst, send_sems, recv_sems, local_sem, cols):
        self.src, self.dst, self.cols = src, dst, cols
        self.send_sems, self.recv_sems, self.local_sem = send_sems, recv_sems, local_sem
        self.me = _my_place()[3]

    def _remote(self, p, source):
        return pltpu.make_async_remote_copy(
            src_ref=_block(self.src, p, self.cols(p)), dst_ref=_block(self.dst, source, self.cols(p)),
            send_sem=self.send_sems.at[p], recv_sem=self.recv_sems.at[source],
            device_id=_device(p), device_id_type=MESH)

    def _local(self, p):
        return pltpu.make_async_copy(_block(self.src, p, self.cols(p)), _block(self.dst, p, self.cols(p)),
                                     self.local_sem)

    def start(self, step):
        for p in range(N_DEV):
            if self.cols(p) is None:
                continue
            pl.when(step == (p - self.me) % N_DEV - 1)(lambda p=p: self._remote(p, self.me).start())
            pl.when(jnp.logical_and(step == 0, self.me == p))(lambda p=p: self._local(p).start())

    def finish(self):
        for p in range(N_DEV):
            if self.cols(p) is None:
                continue
            pl.when(self.me != p)(lambda p=p: self._remote(p, self.me).wait_send())

            def receive(p=p):
                self._local(p).wait()
                for s in range(N_DEV):
                    if s != p:
                        self._remote(p, s).wait_recv()

            pl.when(self.me == p)(receive)


GRAD_EXCHANGE_SEMS = [pltpu.SemaphoreType.DMA((N_DEV,)), pltpu.SemaphoreType.DMA((N_DEV,)), pltpu.SemaphoreType.DMA]


def _allreduce_small(p_mid, p_conv, p_norm):
    def body(a_ref, b_ref, c_ref, out_ref, mine, gathered, send_sems, recv_sems):
        x, y, c, me = _my_place()
        mine[...] = a_ref[...] + b_ref[...] + c_ref[...]
        gathered[me] = mine[...]
        remote = []
        for k, (peer, _) in enumerate(_peers(x, y, c)):
            cp = pltpu.make_async_remote_copy(
                src_ref=mine, dst_ref=gathered.at[me], send_sem=send_sems.at[k], recv_sem=recv_sems.at[k],
                device_id=peer, device_id_type=MESH)
            cp.start()
            remote.append(cp)
        for cp in remote:
            cp.wait()
        total = gathered[0]
        for s in range(1, N_DEV):
            total = total + gathered[s]
        out_ref[...] = total

    return pl.pallas_call(
        body, name="allreduce_small",
        out_shape=jax.ShapeDtypeStruct((8, D), F32),
        in_specs=[VMEM_SPEC, VMEM_SPEC, VMEM_SPEC], out_specs=VMEM_SPEC,
        scratch_shapes=[pltpu.VMEM((8, D), F32), pltpu.VMEM((N_DEV, 8, D), F32),
                        pltpu.SemaphoreType.DMA((N_DEV - 1,)), pltpu.SemaphoreType.DMA((N_DEV - 1,))],
    )(p_mid, p_conv, p_norm)


def _proj_pieces():
    cuts = sorted(set(range(0, IN_COLS + 1, D)) | set(range(0, IN_COLS + 1, W_IN_SHARD)))
    return [(lo // D, lo % D, lo // W_IN_SHARD, lo % W_IN_SHARD, hi - lo) for lo, hi in zip(cuts[:-1], cuts[1:])]


def _norm(x2, norm_g):
    S = x2.shape[0]
    tm = ROW_TILE

    def body(x_ref, g_ref, u_ref, ut_ref):
        xv = x_ref[...]
        r = lax.rsqrt(jnp.mean(xv * xv, axis=-1, keepdims=True) + EPS)
        u = xv * r * g_ref[...]
        u_ref[...] = u.astype(BF)
        ut_ref[...] = u.T.astype(BF)

    return pl.pallas_call(
        body, name="norm", grid=(S // tm,),
        in_specs=[pl.BlockSpec((tm, D), lambda i: (i, 0)), pl.BlockSpec((1, D), lambda i: (0, 0))],
        out_specs=(pl.BlockSpec((tm, D), lambda i: (i, 0)), pl.BlockSpec((D, tm), lambda i: (0, i))),
        out_shape=(jax.ShapeDtypeStruct((S, D), BF), jax.ShapeDtypeStruct((D, S), BF)),
        compiler_params=_params(1),
    )(x2, norm_g)


PROJ_TN = 256


def _proj_cols(u, w_all, seg0, n_seg, dtype, name):
    S = u.shape[0]
    tn = PROJ_TN
    per_shard = W_IN_SHARD // tn
    tile0 = seg0 * D // tn

    def body(u_ref, w_ref, out_ref):
        out_ref[...] = _dot(u_ref[...], w_ref[0]).astype(dtype)

    return pl.pallas_call(
        body, name=name, grid=(n_seg * D // tn,),
        in_specs=[VMEM_SPEC, pl.BlockSpec((1, D, tn), lambda t: ((tile0 + t) // per_shard, 0, (tile0 + t) % per_shard))],
        out_specs=pl.BlockSpec((S, tn), lambda t: (0, t)),
        out_shape=jax.ShapeDtypeStruct((S, n_seg * D), dtype),
        compiler_params=_params(1),
    )(u, w_all)


CONV_TM, CONV_TC = 256, 512
HALO = 16


def _conv_fwd(pa, cw8):
    S = pa.shape[0]
    tm, tc = CONV_TM, CONV_TC
    nct = D // tc

    def seg(s):
        return pl.BlockSpec((tm, tc), lambda i, j, s=s: (i, s * nct + j))

    def halo_before(s):
        return pl.BlockSpec((HALO, tc), lambda i, j, s=s: (jnp.maximum(i * (tm // HALO) - 1, 0), s * nct + j))

    def body(xc, bg, cg, zc, xch, cgh, cw, out):
        i = pl.program_id(0)
        a = cg[...].astype(F32) * xc[...].astype(F32)
        ah = cgh[...].astype(F32) * xch[...].astype(F32)
        ah = jnp.where(i > 0, ah, 0.0)
        row = lax.broadcasted_iota(jnp.int32, (tm, tc), 0)
        a1 = jnp.where(row == 0, ah[HALO - 1:HALO, :], pltpu.roll(a, 1, 0))
        a2 = jnp.where(row == 0, ah[HALO - 2:HALO - 1, :],
                       jnp.where(row == 1, ah[HALO - 1:HALO, :], pltpu.roll(a, 2, 0)))
        w = cw[...]
        conv = w[0:1, :] * a2 + w[1:2, :] * a1 + w[2:3, :] * a
        z = zc[...].astype(F32)
        out[...] = (z * _sigmoid(z) * bg[...].astype(F32) * conv).astype(BF)

    return pl.pallas_call(
        body, name="conv_fwd", grid=(S // tm, nct),
        in_specs=[seg(0), seg(1), seg(2), seg(3), halo_before(0), halo_before(2),
                  pl.BlockSpec((8, tc), lambda i, j: (0, j))],
        out_specs=pl.BlockSpec((tm, tc), lambda i, j: (i, j)),
        out_shape=jax.ShapeDtypeStruct((S, D), BF),
        compiler_params=_params(2),
    )(pa, pa, pa, pa, pa, pa, cw8)


ATT_UNROLL = 2


def _fold_masks():
    row = lax.broadcasted_iota(jnp.int32, (QB, QB), 0)
    lane = lax.broadcasted_iota(jnp.int32, (QB, QB), 1)
    tri_le = lane <= row
    dist = jnp.where(tri_le, row - lane, row - lane + QB).astype(F32)
    return tri_le, dist, lane < HEAD_DIM


def _block_rows(b, d, S):
    nb = S // (QB * d)
    r, n = b // nb, b % nb
    cur0 = r + n * (QB * d)
    prev0 = r + jnp.maximum(n - 1, 0) * (QB * d)
    if d == 1:
        return n, pl.ds(pl.multiple_of(cur0, QB), QB), pl.ds(pl.multiple_of(prev0, QB), QB)
    return n, pl.ds(cur0, QB, stride=d), pl.ds(prev0, QB, stride=d)


def _head_sum_matrix():
    r = lax.broadcasted_iota(jnp.int32, (2 * QB, 2 * QB), 0)
    c = lax.broadcasted_iota(jnp.int32, (2 * QB, 2 * QB), 1)
    return (((r % QB) // HEAD_DIM) == (c // QB)).astype(F32).astype(BF)


def _hi_lo(t):
    hi = t.astype(BF)
    return jnp.concatenate([hi, (t - hi.astype(F32)).astype(BF)], axis=1)


def _attn_fwd(qkv, slopes, w_all, w3_all, cw_all):
    S = qkv.shape[0]
    hpr = D // 128
    n_blocks = S // QB

    def body(sl_ref, q_ref, k_ref, v_ref, w_in_ref, w3_in_ref, cw_in_ref, o_ref, lse_ref, w_ref, w3_ref, cw_ref,
             acc, m_s, l_s, *sems):
        hp = pl.program_id(0)
        me = _my_place()[3]
        gathers = (_WeightGather(lambda p, cols: _block(w_ref, me, cols), w_ref, *sems[0:2], REST_COLS),
                   _WeightGather(lambda p, cols: w3_ref.at[me], w3_ref, *sems[2:4], _whole),
                   _WeightGather(lambda p, cols: cw_ref.at[me], cw_ref, *sems[4:6], _whole))

        for g in gathers:
            g.start(hp)

        tri_le, dist, low = _fold_masks()
        low_b = low.astype(F32).astype(BF)
        high_b = 1.0 - low_b
        head_sum = _head_sum_matrix()
        ones_b = jnp.ones((2 * QB, QB), BF)
        m_s[...] = jnp.full(m_s.shape, NEG, F32)
        l_s[...] = jnp.zeros(l_s.shape, F32)
        acc[...] = jnp.zeros(acc.shape, F32)

        for d in DILATIONS:
            slope = [sl_ref[2 * hp + a] * float(d) for a in range(2)]
            bias = [slope[a] * dist for a in range(2)]

            def block(b, d=d, slope=slope, bias=bias):
                n, cur, prev = _block_rows(b, d, S)
                has_prev = n > 0
                valid = jnp.logical_or(tri_le, has_prev)
                q2 = (q_ref[cur, :] * 0.125).astype(BF)
                qs = jnp.concatenate([q2 * low_b, q2 * high_b], axis=0)
                vp = v_ref[prev, :]
                kp_b = k_ref[prev, :].astype(BF)
                kcat = jnp.concatenate([kp_b, k_ref[cur, :].astype(BF)], axis=0)
                vcat = jnp.concatenate([vp, v_ref[cur, :]], axis=0).astype(BF)
                s2 = _dot_nt(qs, kcat)
                e2 = _dot(_hi_lo(q2.astype(F32) * kp_b.astype(F32)), head_sum)
                p_rows, alpha_h, pe_h = [], [], []
                for a in range(2):
                    sp, sc = s2[a * QB:(a + 1) * QB, :QB], s2[a * QB:(a + 1) * QB, QB:]
                    comb = jnp.where(valid, jnp.where(tri_le, sc, sp) - bias[a], NEG)
                    e = jnp.where(has_prev, e2[:, a * QB:(a + 1) * QB] - slope[a] * float(QB), NEG)
                    m_old = m_s.at[a][cur, :]
                    m_new = jnp.maximum(jnp.maximum(m_old, jnp.max(comb, axis=-1, keepdims=True)), e)
                    m_s.at[a][cur, :] = m_new
                    p = jnp.exp(comb - m_new)
                    pe_h.append(jnp.exp(e - m_new))
                    alpha_h.append(jnp.exp(m_old - m_new))
                    p_rows.append(jnp.concatenate([jnp.where(tri_le, 0.0, p).astype(BF),
                                                   jnp.where(tri_le, p, 0.0).astype(BF)], axis=1))
                pv = _dot(jnp.concatenate(p_rows, axis=0), jnp.concatenate([vcat, ones_b], axis=1))
                for a in range(2):
                    l_a = l_s.at[a]
                    l_a[cur, :] = alpha_h[a] * l_a[cur, :] + pv[a * QB:(a + 1) * QB, QB:] + pe_h[a]
                acc[cur, :] = (jnp.where(low, alpha_h[0], alpha_h[1]) * acc[cur, :]
                               + jnp.where(low, pv[:QB, :QB], pv[QB:, :QB]) + jnp.where(low, pe_h[0], pe_h[1]) * vp)

            def several(it, carry, block=block):
                for u in range(ATT_UNROLL):
                    block(it * ATT_UNROLL + u)
                return carry

            lax.fori_loop(0, n_blocks // ATT_UNROLL, several, 0)

        def finish(i, carry):
            rows = pl.ds(pl.multiple_of(i * QB, QB), QB)
            l0, l1 = l_s[0, rows, :], l_s[1, rows, :]
            o = acc[rows, :] / jnp.where(low, l0, l1)
            o_ref[rows, :] = o
            lse_ref[0, rows, :] = m_s[0, rows, :] + jnp.log(l0)
            lse_ref[1, rows, :] = m_s[1, rows, :] + jnp.log(l1)
            return carry

        lax.fori_loop(0, n_blocks, finish, 0)

        @pl.when(hp == hpr - 1)
        def _():
            for g in gathers:
                g.finish()

    col = lambda s: pl.BlockSpec((S, 128), lambda h, s=s: (0, s * hpr + h))
    gathered = (w_all, w3_all, cw_all)
    return pl.pallas_call(
        body, name="attn_fwd", grid=(hpr,),
        in_specs=[SMEM_SPEC, col(0), col(1), col(2), ANY_SPEC, ANY_SPEC, ANY_SPEC],
        out_specs=(col(0), pl.BlockSpec((2, S, 128), lambda h: (0, 0, h)), ANY_SPEC, ANY_SPEC, ANY_SPEC),
        out_shape=(jax.ShapeDtypeStruct((S, D), F32), jax.ShapeDtypeStruct((2, S, D), F32),
                   *[jax.ShapeDtypeStruct(t.shape, t.dtype) for t in gathered]),
        scratch_shapes=([pltpu.VMEM((S, 128), F32), pltpu.VMEM((2, S, 128), F32), pltpu.VMEM((2, S, 128), F32)]
                        + WEIGHT_GATHER_SEMS * 3),
        input_output_aliases={4: 2, 5: 3, 6: 4},
        compiler_params=_params(1),
    )(slopes, qkv, qkv, qkv, *gathered)


def _set_rows(shape, rows):
    idx = lax.broadcasted_iota(jnp.int32, shape, 0)
    out = jnp.zeros(shape, F32)
    for r, val in rows.items():
        out = out + jnp.where(idx == r, val, 0.0)
    return out


def _mid(yc_in, pa_mid, o, x2, target, b_merge, final_g, w3):
    S = x2.shape[0]
    tm = ROW_TILE
    nsteps = S // tm
    tile = pl.BlockSpec((tm, D), lambda i: (i, 0))

    def body(yc_ref, za_ref, gcp_ref, gap_ref, o_ref, x_ref, t_ref, b_ref, fg_ref, w_ref,
             dh_ref, dmid_ref, do_ref, dyc_ref, gw_ref, small_ref, acc, stage):
        i = pl.program_id(0)

        @pl.when(i == 0)
        def _():
            acc[...] = jnp.zeros_like(acc)
            small_ref[...] = jnp.zeros_like(small_ref)

        wc, wa, wo = w_ref[0], w_ref[1], w_ref[2]
        z = za_ref[...].astype(F32)
        sg = _sigmoid(z)
        ov = o_ref[...]
        yc_in_b, ya_in_b = yc_ref[...], (z * sg * ov).astype(BF)
        yc = _dot(yc_in_b, wc)
        ya = _dot(ya_in_b, wa)
        b = b_ref[...]
        gc = _sigmoid(gcp_ref[...].astype(F32) + b[:, :D])
        ga = _sigmoid(gap_ref[...].astype(F32) + b[:, D:])
        merged = gc * yc + ga * ya
        merged_b = merged.astype(BF)
        h = x_ref[...] + _dot(merged_b, wo)
        r2 = lax.rsqrt(jnp.mean(h * h, axis=-1, keepdims=True) + EPS)
        n = h * r2
        fg = fg_ref[...]
        err = n * fg - t_ref[...]
        loss = 0.5 * jnp.sum(jnp.sum(err * err, axis=-1, keepdims=True) / D, axis=0, keepdims=True)
        dy = err / D
        g_fg = jnp.sum(dy * n, axis=0, keepdims=True)
        dn = dy * fg
        dh = r2 * (dn - n * jnp.mean(dn * n, axis=-1, keepdims=True))
        dh_ref[...] = dh
        dh_b = dh.astype(BF)
        dmerged = _dot_nt(dh_b, wo)
        acc[2] += _dot(merged.T.astype(BF), dh_b)
        dyc = (dmerged * gc).astype(BF)
        dya = (dmerged * ga).astype(BF)
        dgcp = dmerged * yc * gc * (1.0 - gc)
        dgap = dmerged * ya * ga * (1.0 - ga)
        dmid_ref[1] = dgcp.astype(BF)
        dmid_ref[2] = dgap.astype(BF)
        acc[0] += _dot(yc_in_b.astype(F32).T.astype(BF), dyc)
        acc[1] += _dot(ya_in_b.astype(F32).T.astype(BF), dya)
        dyc_ref[...] = _dot_nt(dyc, wc).astype(BF)
        dya_in = _dot_nt(dya, wa)
        do_ref[...] = dya_in * (z * sg)
        dmid_ref[0] = (dya_in * ov * (sg * (1.0 + z * (1.0 - sg)))).astype(BF)
        small_ref[...] += _set_rows((8, D), {
            1: jnp.sum(dgcp, axis=0, keepdims=True), 2: jnp.sum(dgap, axis=0, keepdims=True),
            3: g_fg, 7: jnp.broadcast_to(loss, (1, D))})

        @pl.when(i == nsteps - 1)
        def _():
            for p in range(N_DEV):
                for a in range(3):
                    stage[...] = acc[a, p * ROW_SHARD:(p + 1) * ROW_SHARD, :].astype(BF)
                    pltpu.sync_copy(stage, gw_ref.at[p, a])

    return pl.pallas_call(
        body, name="mid", grid=(nsteps,),
        in_specs=[tile, pl.BlockSpec((tm, D), lambda i: (i, 0)), pl.BlockSpec((tm, D), lambda i: (i, 1)),
                  pl.BlockSpec((tm, D), lambda i: (i, 2)), tile, tile, tile,
                  pl.BlockSpec((1, 2 * D), lambda i: (0, 0)), pl.BlockSpec((1, D), lambda i: (0, 0)), VMEM_SPEC],
        out_specs=(tile, pl.BlockSpec((3, tm, D), lambda i: (0, i, 0)), tile, tile,
                   ANY_SPEC, pl.BlockSpec((8, D), lambda i: (0, 0))),
        out_shape=(jax.ShapeDtypeStruct((S, D), F32), jax.ShapeDtypeStruct((3, S, D), BF),
                   jax.ShapeDtypeStruct((S, D), F32), jax.ShapeDtypeStruct((S, D), BF),
                   jax.ShapeDtypeStruct((N_DEV, 3, ROW_SHARD, D), BF), jax.ShapeDtypeStruct((8, D), F32)),
        scratch_shapes=[pltpu.VMEM((3, D, D), F32), pltpu.VMEM((ROW_SHARD, D), BF)],
        compiler_params=_params(1),
    )(yc_in, pa_mid, pa_mid, pa_mid, o, x2, target, b_merge, final_g, w3)


def _conv_bwd(dyc_in, pa, cw8):
    S = pa.shape[0]
    tm, tc = CONV_TM, CONV_TC
    nct = D // tc
    nrt = S // tm
    last_halo = S // HALO - 1

    def seg(s):
        return pl.BlockSpec((tm, tc), lambda j, i, s=s: (i, s * nct + j))

    def halo_before(s):
        return pl.BlockSpec((HALO, tc), lambda j, i, s=s: (jnp.maximum(i * (tm // HALO) - 1, 0), s * nct + j))

    def halo_after(s):
        return pl.BlockSpec((HALO, tc), lambda j, i, s=s: (jnp.minimum((i + 1) * (tm // HALO), last_halo), s * nct + j))

    def body(dy, xc, bg, cg, zc, xch, cgh, dyn, bgn, zcn, cw, dout, gcw):
        i = pl.program_id(1)

        @pl.when(i == 0)
        def _():
            gcw[...] = jnp.zeros_like(gcw)

        xcv, cgv = xc[...].astype(F32), cg[...].astype(F32)
        a = cgv * xcv
        ah = jnp.where(i > 0, cgh[...].astype(F32) * xch[...].astype(F32), 0.0)
        row = lax.broadcasted_iota(jnp.int32, (tm, tc), 0)
        a1 = jnp.where(row == 0, ah[HALO - 1:HALO, :], pltpu.roll(a, 1, 0))
        a2 = jnp.where(row == 0, ah[HALO - 2:HALO - 1, :],
                       jnp.where(row == 1, ah[HALO - 1:HALO, :], pltpu.roll(a, 2, 0)))
        w = cw[...]
        conv = w[0:1, :] * a2 + w[1:2, :] * a1 + w[2:3, :] * a
        z = zc[...].astype(F32)
        sg = _sigmoid(z)
        silu = z * sg
        bgv = bg[...].astype(F32)
        dyv = dy[...].astype(F32)
        dout[3] = (dyv * bgv * conv * (sg * (1.0 + z * (1.0 - sg)))).astype(BF)
        dout[1] = (dyv * silu * conv).astype(BF)
        dc = dyv * silu * bgv
        zn = zcn[...].astype(F32)
        dcn = dyn[...].astype(F32) * (zn * _sigmoid(zn)) * bgn[...].astype(F32)
        dcn = jnp.where(i < nrt - 1, dcn, 0.0)
        dc1 = jnp.where(row == tm - 1, dcn[0:1, :], pltpu.roll(dc, tm - 1, 0))
        dc2 = jnp.where(row == tm - 1, dcn[1:2, :],
                        jnp.where(row == tm - 2, dcn[0:1, :], pltpu.roll(dc, tm - 2, 0)))
        da = w[2:3, :] * dc + w[1:2, :] * dc1 + w[0:1, :] * dc2
        dout[2] = (da * xcv).astype(BF)
        dout[0] = (da * cgv).astype(BF)
        gcw[...] += _set_rows((8, tc), {
            4: jnp.sum(dc * a2, axis=0, keepdims=True), 5: jnp.sum(dc * a1, axis=0, keepdims=True),
            6: jnp.sum(dc * a, axis=0, keepdims=True)})

    return pl.pallas_call(
        body, name="conv_bwd", grid=(nct, nrt),
        in_specs=[pl.BlockSpec((tm, tc), lambda j, i: (i, j)), seg(0), seg(1), seg(2), seg(3),
                  halo_before(0), halo_before(2),
                  pl.BlockSpec((HALO, tc), lambda j, i: (jnp.minimum((i + 1) * (tm // HALO), last_halo), j)),
                  halo_after(1), halo_after(3), pl.BlockSpec((8, tc), lambda j, i: (0, j))],
        out_specs=(pl.BlockSpec((4, tm, tc), lambda j, i: (0, i, j)), pl.BlockSpec((8, tc), lambda j, i: (0, j))),
        out_shape=(jax.ShapeDtypeStruct((4, S, D), BF), jax.ShapeDtypeStruct((8, D), F32)),
        compiler_params=_params(2),
    )(dyc_in, pa, pa, pa, pa, pa, pa, dyc_in, pa, pa, cw8)


def _attn_bwd(qkv, slopes, do, o, lse, g_in, g_3):
    S = qkv.shape[0]
    hpr = D // 128
    n_blocks = S // QB

    def body(sl_ref, q_ref, k_ref, v_ref, do_ref, o_ref, lse_ref, gin_ref, g3_ref, out_ref, rin_ref, r3_ref,
             dq_s, dk_s, dv_s, dd_s, *sems):
        hp = pl.program_id(0)
        exchanges = (_GradExchange(gin_ref, rin_ref, *sems[:3], _shard_cols((0, SEG0_ATTN * D), (SEG0_MID * D, IN_COLS))),
                     _GradExchange(g3_ref, r3_ref, *sems[3:], lambda p: ()))

        for ex in exchanges:
            ex.start(hp)

        tri_le, dist, low = _fold_masks()
        low_b = low.astype(F32).astype(BF)
        high_b = 1.0 - low_b
        head_sum = _head_sum_matrix()
        dq_s[...] = jnp.zeros(dq_s.shape, F32)
        dk_s[...] = jnp.zeros(dk_s.shape, F32)
        dv_s[...] = jnp.zeros(dv_s.shape, F32)

        def row_dots(i, carry):
            rows = pl.ds(pl.multiple_of(i * QB, QB), QB)
            dd = _dot(_hi_lo(do_ref[rows, :] * o_ref[rows, :]), head_sum)
            dd_s[0, rows, :] = dd[:, :QB]
            dd_s[1, rows, :] = dd[:, QB:]
            return carry

        lax.fori_loop(0, n_blocks, row_dots, 0)

        for d in DILATIONS:
            slope = [sl_ref[2 * hp + a] * float(d) for a in range(2)]
            bias = [slope[a] * dist for a in range(2)]

            def block(b, d=d, slope=slope, bias=bias):
                n, cur, prev = _block_rows(b, d, S)
                has_prev = n > 0
                valid = jnp.logical_or(tri_le, has_prev)
                q2f = q_ref[cur, :] * 0.125
                q2 = q2f.astype(BF)
                qs = jnp.concatenate([q2 * low_b, q2 * high_b], axis=0)
                kp, vp = k_ref[prev, :], v_ref[prev, :]
                kp_b, vp_b = kp.astype(BF), vp.astype(BF)
                kcat = jnp.concatenate([kp_b, k_ref[cur, :].astype(BF)], axis=0)
                vcat = jnp.concatenate([vp_b, v_ref[cur, :].astype(BF)], axis=0)
                do2f = do_ref[cur, :]
                do2 = do2f.astype(BF)
                dos = jnp.concatenate([do2 * low_b, do2 * high_b], axis=0)
                s2 = _dot_nt(qs, kcat)
                dp2 = _dot_nt(dos, vcat)
                diag2 = _dot(jnp.concatenate([_hi_lo(q2.astype(F32) * kp_b.astype(F32)),
                                              _hi_lo(do2.astype(F32) * vp_b.astype(F32))], axis=0), head_sum)
                p_rows, ds_rows, pe_h, dse_h = [], [], [], []
                for a in range(2):
                    hs = slice(a * QB, (a + 1) * QB)
                    sp, sc = s2[hs, :QB], s2[hs, QB:]
                    dpp, dpc = dp2[hs, :QB], dp2[hs, QB:]
                    lse_a, dd_a = lse_ref.at[a][cur, :], dd_s.at[a][cur, :]
                    comb = jnp.where(tri_le, sc, sp) - bias[a]
                    e = diag2[:QB, hs] - slope[a] * float(QB)
                    p = jnp.where(valid, jnp.exp(comb - lse_a), 0.0)
                    pe = jnp.where(has_prev, jnp.exp(e - lse_a), 0.0)
                    ds = p * (jnp.where(tri_le, dpc, dpp) - dd_a)
                    dse_h.append(pe * (diag2[QB:, hs] - dd_a))
                    pe_h.append(pe)
                    p_rows.append(jnp.concatenate([jnp.where(tri_le, 0.0, p).astype(BF),
                                                   jnp.where(tri_le, p, 0.0).astype(BF)], axis=1))
                    ds_rows.append(jnp.concatenate([jnp.where(tri_le, 0.0, ds).astype(BF),
                                                    jnp.where(tri_le, ds, 0.0).astype(BF)], axis=1))
                pst = jnp.concatenate(p_rows, axis=0)
                dst = jnp.concatenate(ds_rows, axis=0)
                pe2 = jnp.where(low, pe_h[0], pe_h[1])
                dse2 = jnp.where(low, dse_h[0], dse_h[1])
                dq = _dot(dst, kcat)
                dq_s[cur, :] += (jnp.where(low, dq[:QB], dq[QB:]) + dse2 * kp) * 0.125
                dk = _dot_tn(dst, qs)
                dv = _dot_tn(pst, dos)
                dk_s[prev, :] += dk[:QB] + dse2 * q2f
                dk_s[cur, :] += dk[QB:]
                dv_s[prev, :] += dv[:QB] + pe2 * do2f
                dv_s[cur, :] += dv[QB:]

            def several(it, carry, block=block):
                for u in range(ATT_UNROLL):
                    block(it * ATT_UNROLL + u)
                return carry

            lax.fori_loop(0, n_blocks // ATT_UNROLL, several, 0)

        def finish(i, carry):
            rows = pl.ds(pl.multiple_of(i * QB, QB), QB)
            out_ref[0, rows, :] = dq_s[rows, :].astype(BF)
            out_ref[1, rows, :] = dk_s[rows, :].astype(BF)
            out_ref[2, rows, :] = dv_s[rows, :].astype(BF)
            return carry

        lax.fori_loop(0, n_blocks, finish, 0)

        @pl.when(hp == hpr - 1)
        def _():
            for ex in exchanges:
                ex.finish()

    col = lambda s: pl.BlockSpec((S, 128), lambda h, s=s: (0, s * hpr + h))
    return pl.pallas_call(
        body, name="attn_bwd", grid=(hpr,),
        in_specs=[SMEM_SPEC, col(0), col(1), col(2), col(0), col(0), pl.BlockSpec((2, S, 128), lambda h: (0, 0, h)),
                  ANY_SPEC, ANY_SPEC],
        out_specs=(pl.BlockSpec((3, S, 128), lambda h: (0, 0, h)), ANY_SPEC, ANY_SPEC),
        out_shape=(jax.ShapeDtypeStruct((3, S, D), BF), jax.ShapeDtypeStruct(g_in.shape, BF),
                   jax.ShapeDtypeStruct(g_3.shape, BF)),
        scratch_shapes=([pltpu.VMEM((S, 128), F32)] * 3 + [pltpu.VMEM((2, S, 128), F32)]
                        + GRAD_EXCHANGE_SEMS + GRAD_EXCHANGE_SEMS),
        compiler_params=_params(1),
    )(slopes, qkv, qkv, qkv, do, o, lse, g_in, g_3)


WG_TN = 256
SEG0_CONV, SEG0_ATTN, SEG0_MID = 0, 4, 7


def _wgrad_in(ut, d_group, seg0, g_in, name):
    S = ut.shape[1]
    tn = WG_TN
    per_seg = D // tn
    per_shard = W_IN_SHARD // tn
    n_tiles = d_group.shape[0] * per_seg
    tile0 = seg0 * per_seg

    def body(ut_ref, d_ref, *rest):
        rest[-1][0] = _dot(ut_ref[...], d_ref[0]).astype(BF)

    operands, in_specs, aliases = [ut, d_group], [VMEM_SPEC, pl.BlockSpec((1, S, tn), lambda t: (t // per_seg, 0, t % per_seg))], {}
    if g_in is not None:
        operands.append(g_in)
        in_specs.append(ANY_SPEC)
        aliases = {2: 0}
    return pl.pallas_call(
        body, name=name, grid=(n_tiles,), in_specs=in_specs,
        out_specs=pl.BlockSpec((1, D, tn), lambda t: ((tile0 + t) // per_shard, 0, (tile0 + t) % per_shard)),
        out_shape=jax.ShapeDtypeStruct((N_DEV, D, W_IN_SHARD), BF),
        input_output_aliases=aliases,
        compiler_params=_params(1),
    )(*operands)


def _dgrad_norm_bwd(d_conv, d_attn, d_mid, w_all, x2, dh, norm_g, g_in, r_in):
    S = x2.shape[0]
    tm = ROW_TILE
    nsteps = S // tm
    tile = pl.BlockSpec((tm, D), lambda i: (i, 0))
    pieces = _proj_pieces()

    def body(a_ref, b_ref, c_ref, w_ref, x_ref, dh_ref, g_ref, gin_ref, rin_in_ref, gx_ref, small_ref, rin_ref, *sems):
        i = pl.program_id(0)
        exchange = _GradExchange(gin_ref, rin_ref, *sems, _shard_cols((SEG0_ATTN * D, SEG0_MID * D)))

        @pl.when(i == 0)
        def _():
            small_ref[...] = jnp.zeros_like(small_ref)

        exchange.start(i)

        groups = (a_ref, b_ref, c_ref)
        du = jnp.zeros((tm, D), F32)
        for s, sc, p, pc, width in pieces:
            g = 0 if s < 4 else (1 if s < 7 else 2)
            local = s - (0, 4, 7)[g]
            du = du + _dot_nt(groups[g][local, :, sc:sc + width], w_ref[p, :, pc:pc + width])
        xv = x_ref[...]
        r = lax.rsqrt(jnp.mean(xv * xv, axis=-1, keepdims=True) + EPS)
        n = xv * r
        dn = du * g_ref[...]
        gx_ref[...] = dh_ref[...] + r * (dn - n * jnp.mean(dn * n, axis=-1, keepdims=True))
        small_ref[...] += _set_rows((8, D), {0: jnp.sum(du * n, axis=0, keepdims=True)})

        @pl.when(i == nsteps - 1)
        def _():
            exchange.finish()

    return pl.pallas_call(
        body, name="dgrad_norm_bwd", grid=(nsteps,),
        in_specs=[pl.BlockSpec((4, tm, D), lambda i: (0, i, 0)), pl.BlockSpec((3, tm, D), lambda i: (0, i, 0)),
                  pl.BlockSpec((3, tm, D), lambda i: (0, i, 0)), VMEM_SPEC, tile, tile,
                  pl.BlockSpec((1, D), lambda i: (0, 0)), ANY_SPEC, ANY_SPEC],
        out_specs=(tile, pl.BlockSpec((8, D), lambda i: (0, 0)), ANY_SPEC),
        out_shape=(jax.ShapeDtypeStruct((S, D), F32), jax.ShapeDtypeStruct((8, D), F32),
                   jax.ShapeDtypeStruct(r_in.shape, BF)),
        scratch_shapes=GRAD_EXCHANGE_SEMS,
        input_output_aliases={8: 2},
        compiler_params=_params(1),
    )(d_conv, d_attn, d_mid, w_all, x2, dh, norm_g, g_in, r_in)


def _adamw_math(w, g, m, v):
    m = ADAM_B1 * m + (1.0 - ADAM_B1) * g
    v = ADAM_B2 * v + (1.0 - ADAM_B2) * (g * g)
    m_hat = m / (1.0 - ADAM_B1 ** ADAM_STEP)
    v_hat = v / (1.0 - ADAM_B2 ** ADAM_STEP)
    delta = -ADAM_LR * (m_hat / (jnp.sqrt(v_hat) + ADAM_EPS) + ADAM_WD * w)
    return delta, m, v


def _sum_adamw(parts, w, m, v, tm, name):
    R, C = w.shape
    tile = pl.BlockSpec((tm, C), lambda i: (i, 0))

    def body(p_ref, w_ref, m_ref, v_ref, g_out, d_out, m_out, v_out):
        g = p_ref[0].astype(F32)
        for s in range(1, N_DEV):
            g = g + p_ref[s].astype(F32)
        g_out[...] = g
        d_out[...], m_out[...], v_out[...] = _adamw_math(w_ref[...], g, m_ref[...], v_ref[...])

    shape = jax.ShapeDtypeStruct((R, C), F32)
    return pl.pallas_call(
        body, name=name, grid=(R // tm,),
        in_specs=[pl.BlockSpec((N_DEV, tm, C), lambda i: (0, i, 0)), tile, tile, tile],
        out_specs=(tile, tile, tile, tile), out_shape=(shape, shape, shape, shape),
        compiler_params=_params(1),
    )(parts, w, m, v)


def _adamw(g, w, m, v, name):
    def body(g_ref, w_ref, m_ref, v_ref, d_out, m_out, v_out):
        d_out[...], m_out[...], v_out[...] = _adamw_math(w_ref[...], g_ref[...], m_ref[...], v_ref[...])

    shape = jax.ShapeDtypeStruct(w.shape, F32)
    return pl.pallas_call(
        body, name=name, in_specs=[VMEM_SPEC] * 4, out_specs=(VMEM_SPEC,) * 3, out_shape=(shape, shape, shape),
    )(g, w, m, v)


def _alibi_slopes():
    return jnp.exp2(-8.0 * jnp.arange(1, N_HEADS + 1, dtype=F32) / N_HEADS)


def _local_step(x2, target, norm_g, b_merge, final_g, w_all, w3_all, cw_all):
    slopes = _alibi_slopes()
    u, ut = _norm(x2, norm_g)
    qkv = _proj_cols(u, w_all, SEG0_ATTN, 3, F32, "proj_qkv")
    o, lse, w_all, w3_all, cw_all = _attn_fwd(qkv, slopes, w_all, w3_all, cw_all)
    w3 = jnp.transpose(w3_all, (1, 0, 2, 3)).reshape(3, D, D)
    cw8 = jnp.transpose(cw_all, (1, 0, 2)).reshape(8, D)
    pa = _proj_cols(u, w_all, SEG0_CONV, 4, BF, "proj_conv")
    yc_in = _conv_fwd(pa, cw8)
    pa_mid = _proj_cols(u, w_all, SEG0_MID, 3, BF, "proj_mid")
    dh, d_mid, do, dyc_in, g_3, small_mid = _mid(yc_in, pa_mid, o, x2, target, b_merge, final_g, w3)
    g_in = _wgrad_in(ut, d_mid, SEG0_MID, None, "wgrad_in_mid")
    d_conv, small_conv = _conv_bwd(dyc_in, pa, cw8)
    g_in = _wgrad_in(ut, d_conv, SEG0_CONV, g_in, "wgrad_in_conv")
    d_attn, r_in, r_3 = _attn_bwd(qkv, slopes, do, o, lse, g_in, g_3)
    g_in = _wgrad_in(ut, d_attn, SEG0_ATTN, g_in, "wgrad_in_attn")
    grad_x, small_norm, r_in = _dgrad_norm_bwd(d_conv, d_attn, d_mid, w_all, x2, dh, norm_g, g_in, r_in)
    return grad_x, r_in, r_3, small_mid, small_conv, small_norm


def kernel(x, norm_g, w_in, b_merge, conv_w, w_out_conv, w_out_attn, w_o, final_g, loss_target, m_norm_g, m_w_in, m_b_merge, m_conv_w, m_w_out_conv, m_w_out_attn, m_w_o, m_final_g, v_norm_g, v_w_in, v_b_merge, v_conv_w, v_w_out_conv, v_w_out_attn, v_w_o, v_final_g):
    me = 4 * lax.axis_index("x") + 2 * lax.axis_index("y") + lax.axis_index("c")
    stack3 = lambda a, b, c: jnp.concatenate([a, b, c], axis=0)
    pad8 = lambda a: jnp.pad(a, ((0, 8 - a.shape[0]), (0, 0)))

    w3_shard = stack3(w_out_conv, w_out_attn, w_o)
    w_all, w3_all, cw_all = _gather_qkv_weights(w_in[0], w3_shard, pad8(conv_w[0]))

    final_g2 = final_g.reshape(1, D)
    grad_x, r_in, r_3, small_mid, small_conv, small_norm = _local_step(
        x[0], loss_target[0], norm_g, b_merge, final_g2, w_all, w3_all, cw_all)

    small = _allreduce_small(small_mid, small_conv, small_norm)

    g_w_in, d_w_in, nm_w_in, nv_w_in = _sum_adamw(r_in, w_in[0], m_w_in[0], v_w_in[0], 128, "adamw_w_in")
    g_w3, d_w3, nm_w3, nv_w3 = _sum_adamw(
        r_3.reshape(N_DEV, 3 * ROW_SHARD, D), w3_shard.reshape(3 * ROW_SHARD, D),
        stack3(m_w_out_conv, m_w_out_attn, m_w_o).reshape(3 * ROW_SHARD, D),
        stack3(v_w_out_conv, v_w_out_attn, v_w_o).reshape(3 * ROW_SHARD, D), ROW_SHARD, "adamw_w3")

    def pack(ng, bm, fg):
        return pad8(jnp.concatenate([ng, bm.reshape(2, D), fg.reshape(1, D)], axis=0))

    d_s, nm_s, nv_s = _adamw(small, pack(norm_g, b_merge, final_g), pack(m_norm_g, m_b_merge, m_final_g),
                             pack(v_norm_g, v_b_merge, v_final_g), "adamw_small")
    g_cw = lax.dynamic_slice(small, (4, me * ROW_SHARD), (3, ROW_SHARD))
    d_cw, nm_cw, nv_cw = _adamw(g_cw, conv_w[0], m_conv_w[0], v_conv_w[0], "adamw_conv_w")

    loss = small[7, 0]
    split3 = lambda t: tuple(t[a * ROW_SHARD:(a + 1) * ROW_SHARD][None] for a in range(3))
    unpack = lambda t: (t[0:1], t[1:3].reshape(1, 2 * D), t[3])

    def leaves(in_, small_, cw_, w3_):
        ng, bm, fg = unpack(small_)
        wc, wa, wo = split3(w3_)
        return (ng, in_[None], bm, cw_[None], wc, wa, wo, fg)

    return (loss, grad_x[None],
            *leaves(g_w_in, small, g_cw, g_w3),
            *leaves(d_w_in, d_s, d_cw, d_w3),
            *leaves(nm_w_in, nm_s, nm_cw, nm_w3),
            *leaves(nv_w_in, nv_s, nv_cw, nv_w3))
```

```python
import functools

import jax
import jax.numpy as jnp
from jax import lax
from jax.experimental import pallas as pl
from jax.experimental.pallas import tpu as pltpu

D = 1024
N_HEADS = 16
HEAD_DIM = 64
N_SEG = 10
IN_COLS = N_SEG * D
N_DEV = 8
W_IN_SHARD = IN_COLS // N_DEV
ROW_SHARD = D // N_DEV
QB = 128
DILATIONS = (1, 4, 16)
EPS = 1e-6
NEG = -1e30
BF = jnp.bfloat16
F32 = jnp.float32
MESH = pl.DeviceIdType.MESH

ADAM_LR = 0.001
ADAM_B1 = 0.9
ADAM_B2 = 0.999
ADAM_EPS = 1e-08
ADAM_WD = 0.01
ADAM_STEP = 10

V7X_VMEM_BYTES = 64 * 1024 * 1024
VMEM_LIMIT = V7X_VMEM_BYTES - 8 * 1024 * 1024
ROW_TILE = 256

VMEM_SPEC = pl.BlockSpec(memory_space=pltpu.VMEM)
ANY_SPEC = pl.BlockSpec(memory_space=pl.ANY)
SMEM_SPEC = pl.BlockSpec(memory_space=pltpu.SMEM)


def _params(n_grid_axes, vmem=VMEM_LIMIT):
    return pltpu.CompilerParams(dimension_semantics=("arbitrary",) * n_grid_axes, vmem_limit_bytes=vmem)


def _dot(a, b):
    return jnp.dot(a, b, preferred_element_type=F32)


def _dot_nt(a, b):
    return lax.dot_general(a, b, (((1,), (1,)), ((), ())), preferred_element_type=F32)


def _dot_tn(a, b):
    return lax.dot_general(a, b, (((0,), (0,)), ((), ())), preferred_element_type=F32)


def _sigmoid(z):
    return 1.0 / (1.0 + jnp.exp(-z))


def _my_place():
    x, y, c = lax.axis_index("x"), lax.axis_index("y"), lax.axis_index("c")
    return x, y, c, 4 * x + 2 * y + c


def _peers(x, y, c):
    out = []
    for k in range(1, N_DEV):
        px = 1 - x if k & 4 else x
        py = 1 - y if k & 2 else y
        pc = 1 - c if k & 1 else c
        out.append(((px, py, pc), 4 * px + 2 * py + pc))
    return out


def _device(p):
    return (p >> 2, (p >> 1) & 1, p & 1)


def _shard_cols(*ranges):
    def cols(p):
        found = None
        for lo, hi in ranges:
            a, b = max(lo, p * W_IN_SHARD), min(hi, (p + 1) * W_IN_SHARD)
            if a < b:
                assert found is None
                found = (a - p * W_IN_SHARD, b - p * W_IN_SHARD)
        return found

    return cols


def _whole(p):
    return ()


def _block(ref, idx, cols):
    return ref.at[idx] if cols == () else ref.at[idx, :, cols[0]:cols[1]]


class _WeightGather:
    def __init__(self, src, dst, send_sems, recv_sems, cols):
        self.src, self.dst, self.cols = src, dst, cols
        self.send_sems, self.recv_sems = send_sems, recv_sems
        self.me = _my_place()[3]

    def _copy(self, p, target):
        cols = self.cols(p)
        return pltpu.make_async_remote_copy(
            src_ref=self.src(p, cols), dst_ref=_block(self.dst, p, cols), send_sem=self.send_sems.at[target],
            recv_sem=self.recv_sems.at[p], device_id=_device(target), device_id_type=MESH)

    def _each(self, send, receive):
        for p in range(N_DEV):
            if self.cols(p) is None:
                continue

            def sender(p=p):
                for t in range(N_DEV):
                    if t != p:
                        send(self._copy(p, t))

            pl.when(self.me == p)(sender)
            pl.when(self.me != p)(lambda p=p: receive(self._copy(p, p)))

    def start(self):
        self._each(lambda cp: cp.start(), lambda cp: None)

    def finish(self):
        self._each(lambda cp: cp.wait_send(), lambda cp: cp.wait_recv())


WEIGHT_GATHER_SEMS = [pltpu.SemaphoreType.DMA((N_DEV,)), pltpu.SemaphoreType.DMA((N_DEV,))]
REST_COLS = _shard_cols((0, 4 * D), (7 * D, IN_COLS))
HEAD_PAIRS = D // 128


def _qkv_piece(h, seg):
    col = (4 + seg) * D + 128 * h
    return col // W_IN_SHARD, col % W_IN_SHARD


class _PieceGather:
    def __init__(self, src, dst, send_sems, recv_sems):
        self.src, self.dst, self.send_sems, self.recv_sems = src, dst, send_sems, recv_sems
        self.me = _my_place()[3]

    def _copy(self, i, target):
        p, lo = _qkv_piece(i // 3, i % 3)
        return pltpu.make_async_remote_copy(
            src_ref=self.src(p, lo, lo + 128), dst_ref=self.dst.at[p, :, lo:lo + 128], send_sem=self.send_sems.at[i, target],
            recv_sem=self.recv_sems.at[i], device_id=_device(target), device_id_type=MESH)

    def _owner(self, i, act):
        p = _qkv_piece(i // 3, i % 3)[0]

        def sender():
            for t in range(N_DEV):
                if t != p:
                    act(self._copy(i, t))

        pl.when(self.me == p)(sender)

    def start(self, pieces):
        for i in pieces:
            self._owner(i, lambda cp: cp.start())

    def wait_send(self, pieces):
        for i in pieces:
            self._owner(i, lambda cp: cp.wait_send())

    def wait_recv(self, pieces):
        for i in pieces:
            p = _qkv_piece(i // 3, i % 3)[0]
            pl.when(self.me != p)(lambda i=i, p=p: self._copy(i, p).wait_recv())


def _piece_sems(n):
    return [pltpu.SemaphoreType.DMA((n, N_DEV)), pltpu.SemaphoreType.DMA((n,))]


def _gather_first_weights(w_in, w3, cw):
    def body(w_in_ref, w3_ref, cw_ref, o_in, o_3, o_cw, in_bf, w3_bf, local_sems, *sems):
        me = _my_place()[3]

        def cast_rows(i, carry):
            r = pl.multiple_of(i * 128, 128)
            in_bf[pl.ds(r, 128), :] = w_in_ref[pl.ds(r, 128), :].astype(BF)
            return carry

        lax.fori_loop(0, D // 128, cast_rows, 0)
        for a in range(3):
            w3_bf[a] = w3_ref[a].astype(BF)
        gather = _PieceGather(lambda p, lo, hi: in_bf.at[:, lo:hi], o_in, *sems)
        gather.start(range(3))
        local = [pltpu.make_async_copy(src, dst.at[me], local_sems.at[a])
                 for a, (src, dst) in enumerate(((in_bf, o_in), (w3_bf, o_3), (cw_ref, o_cw)))]
        for cp in local:
            cp.start()
        gather.wait_recv(range(3))
        gather.wait_send(range(3))
        for cp in local:
            cp.wait()

    return pl.pallas_call(
        body, name="gather_first_weights",
        out_shape=(jax.ShapeDtypeStruct((N_DEV, D, W_IN_SHARD), BF),
                   jax.ShapeDtypeStruct((N_DEV, 3, ROW_SHARD, D), BF),
                   jax.ShapeDtypeStruct((N_DEV, 8, 128), F32)),
        in_specs=[VMEM_SPEC, VMEM_SPEC, VMEM_SPEC],
        out_specs=(ANY_SPEC, ANY_SPEC, ANY_SPEC),
        scratch_shapes=[pltpu.VMEM((D, W_IN_SHARD), BF), pltpu.VMEM((3, ROW_SHARD, D), BF),
                        pltpu.SemaphoreType.DMA((3,))] + _piece_sems(3),
        compiler_params=pltpu.CompilerParams(vmem_limit_bytes=VMEM_LIMIT),
    )(w_in, w3, cw)


class _GradExchange:
    def __init__(self, src, dst, send_sems, recv_sems, local_sem, cols):
        self.src, self.dst, self.cols = src, dst, cols
        self.send_sems, self.recv_sems, self.local_sem = send_sems, recv_sems, local_sem
        self.me = _my_place()[3]

    def _remote(self, p, source):
        return pltpu.make_async_remote_copy(
            src_ref=_block(self.src, p, self.cols(p)), dst_ref=_block(self.dst, source, self.cols(p)),
            send_sem=self.send_sems.at[p], recv_sem=self.recv_sems.at[source],
            device_id=_device(p), device_id_type=MESH)

    def _local(self, p):
        return pltpu.make_async_copy(_block(self.src, p, self.cols(p)), _block(self.dst, p, self.cols(p)),
                                     self.local_sem)

    def start(self):
        for p in range(N_DEV):
            if self.cols(p) is None:
                continue
            pl.when(self.me != p)(lambda p=p: self._remote(p, self.me).start())
            pl.when(self.me == p)(lambda p=p: self._local(p).start())

    def finish(self):
        for p in range(N_DEV):
            if self.cols(p) is None:
                continue
            pl.when(self.me != p)(lambda p=p: self._remote(p, self.me).wait_send())

            def receive(p=p):
                self._local(p).wait()
                for s in range(N_DEV):
                    if s != p:
                        self._remote(p, s).wait_recv()

            pl.when(self.me == p)(receive)


GRAD_EXCHANGE_SEMS = [pltpu.SemaphoreType.DMA((N_DEV,)), pltpu.SemaphoreType.DMA((N_DEV,)), pltpu.SemaphoreType.DMA]


def _allreduce_small(p_mid, p_conv, p_norm):
    def body(a_ref, b_ref, c_ref, out_ref, mine, gathered, send_sems, recv_sems):
        x, y, c, me = _my_place()
        mine[...] = a_ref[...] + b_ref[...] + c_ref[...]
        gathered[me] = mine[...]
        remote = []
        for k, (peer, _) in enumerate(_peers(x, y, c)):
            cp = pltpu.make_async_remote_copy(
                src_ref=mine, dst_ref=gathered.at[me], send_sem=send_sems.at[k], recv_sem=recv_sems.at[k],
                device_id=peer, device_id_type=MESH)
            cp.start()
            remote.append(cp)
        for cp in remote:
            cp.wait()
        total = gathered[0]
        for s in range(1, N_DEV):
            total = total + gathered[s]
        out_ref[...] = total

    return pl.pallas_call(
        body, name="allreduce_small",
        out_shape=jax.ShapeDtypeStruct((8, D), F32),
        in_specs=[VMEM_SPEC, VMEM_SPEC, VMEM_SPEC], out_specs=VMEM_SPEC,
        scratch_shapes=[pltpu.VMEM((8, D), F32), pltpu.VMEM((N_DEV, 8, D), F32),
                        pltpu.SemaphoreType.DMA((N_DEV - 1,)), pltpu.SemaphoreType.DMA((N_DEV - 1,))],
    )(p_mid, p_conv, p_norm)


def _proj_pieces():
    cuts = sorted(set(range(0, IN_COLS + 1, D)) | set(range(0, IN_COLS + 1, W_IN_SHARD)))
    return [(lo // D, lo % D, lo // W_IN_SHARD, lo % W_IN_SHARD, hi - lo) for lo, hi in zip(cuts[:-1], cuts[1:])]


def _norm(x2, norm_g):
    S = x2.shape[0]
    tm = ROW_TILE

    def body(x_ref, g_ref, u_ref, ut_ref):
        xv = x_ref[...]
        r = lax.rsqrt(jnp.mean(xv * xv, axis=-1, keepdims=True) + EPS)
        u = xv * r * g_ref[...]
        u_ref[...] = u.astype(BF)
        ut_ref[...] = u.T.astype(BF)

    return pl.pallas_call(
        body, name="norm", grid=(S // tm,),
        in_specs=[pl.BlockSpec((tm, D), lambda i: (i, 0)), pl.BlockSpec((1, D), lambda i: (0, 0))],
        out_specs=(pl.BlockSpec((tm, D), lambda i: (i, 0)), pl.BlockSpec((D, tm), lambda i: (0, i))),
        out_shape=(jax.ShapeDtypeStruct((S, D), BF), jax.ShapeDtypeStruct((D, S), BF)),
        compiler_params=_params(1),
    )(x2, norm_g)


PROJ_TN = 256


def _proj_cols(u, w_all, seg0, n_seg, dtype, name):
    S = u.shape[0]
    tn = PROJ_TN
    per_shard = W_IN_SHARD // tn
    tile0 = seg0 * D // tn

    def body(u_ref, w_ref, out_ref):
        out_ref[...] = _dot(u_ref[...], w_ref[0]).astype(dtype)

    return pl.pallas_call(
        body, name=name, grid=(n_seg * D // tn,),
        in_specs=[VMEM_SPEC, pl.BlockSpec((1, D, tn), lambda t: ((tile0 + t) // per_shard, 0, (tile0 + t) % per_shard))],
        out_specs=pl.BlockSpec((S, tn), lambda t: (0, t)),
        out_shape=jax.ShapeDtypeStruct((S, n_seg * D), dtype),
        compiler_params=_params(1),
    )(u, w_all)


CONV_TM, CONV_TC = 256, 512
HALO = 16


def _conv_fwd(pa, cw8):
    S = pa.shape[0]
    tm, tc = CONV_TM, CONV_TC
    nct = D // tc

    def seg(s):
        return pl.BlockSpec((tm, tc), lambda i, j, s=s: (i, s * nct + j))

    def halo_before(s):
        return pl.BlockSpec((HALO, tc), lambda i, j, s=s: (jnp.maximum(i * (tm // HALO) - 1, 0), s * nct + j))

    def body(xc, bg, cg, zc, xch, cgh, cw, out):
        i = pl.program_id(0)
        a = cg[...].astype(F32) * xc[...].astype(F32)
        ah = cgh[...].astype(F32) * xch[...].astype(F32)
        ah = jnp.where(i > 0, ah, 0.0)
        row = lax.broadcasted_iota(jnp.int32, (tm, tc), 0)
        a1 = jnp.where(row == 0, ah[HALO - 1:HALO, :], pltpu.roll(a, 1, 0))
        a2 = jnp.where(row == 0, ah[HALO - 2:HALO - 1, :],
                       jnp.where(row == 1, ah[HALO - 1:HALO, :], pltpu.roll(a, 2, 0)))
        w = cw[...]
        conv = w[0:1, :] * a2 + w[1:2, :] * a1 + w[2:3, :] * a
        z = zc[...].astype(F32)
        out[...] = (z * _sigmoid(z) * bg[...].astype(F32) * conv).astype(BF)

    return pl.pallas_call(
        body, name="conv_fwd", grid=(S // tm, nct),
        in_specs=[seg(0), seg(1), seg(2), seg(3), halo_before(0), halo_before(2),
                  pl.BlockSpec((8, tc), lambda i, j: (0, j))],
        out_specs=pl.BlockSpec((tm, tc), lambda i, j: (i, j)),
        out_shape=jax.ShapeDtypeStruct((S, D), BF),
        compiler_params=_params(2),
    )(pa, pa, pa, pa, pa, pa, cw8)


ATT_UNROLL = 2
CH = 32


def _fold_masks():
    row = lax.broadcasted_iota(jnp.int32, (QB, QB), 0)
    lane = lax.broadcasted_iota(jnp.int32, (QB, QB), 1)
    tri_le = lane <= row
    dist = jnp.where(tri_le, row - lane, row - lane + QB).astype(F32)
    return tri_le, dist, lane < HEAD_DIM


def _rows(start, size, d):
    return pl.ds(pl.multiple_of(start, size), size) if d == 1 else pl.ds(start, size, stride=d)


def _block_starts(b, d, S):
    nb = S // (QB * d)
    r, n = b // nb, b % nb
    return n, r + n * (QB * d), r + jnp.maximum(n - 1, 0) * (QB * d)


def _block_rows(b, d, S):
    n, cur0, prev0 = _block_starts(b, d, S)
    return n, _rows(cur0, QB, d), _rows(prev0, QB, d)


def _head_sum_matrix():
    r = lax.broadcasted_iota(jnp.int32, (2 * QB, 2 * QB), 0)
    c = lax.broadcasted_iota(jnp.int32, (2 * QB, 2 * QB), 1)
    return (((r % QB) // HEAD_DIM) == (c // QB)).astype(F32).astype(BF)


def _hi_lo(t):
    hi = t.astype(BF)
    return jnp.concatenate([hi, (t - hi.astype(F32)).astype(BF)], axis=1)


PROJ_ROWS = 512


def _attn_fwd(u, slopes, w_all, w3_all, cw_all):
    S = u.shape[0]
    hpr = HEAD_PAIRS
    n_blocks = S // QB
    later = range(3, 3 * hpr)

    def body(sl_ref, u_ref, w_in_ref, w3_in_ref, cw_in_ref, o_ref, lse_ref, q_ref, k_ref, v_ref, w_ref, w3_ref,
             cw_ref, acc, m_s, l_s, w_tile, tile_sems, *sems):
        hp = pl.program_id(0)
        me = _my_place()[3]
        pieces = _PieceGather(lambda p, lo, hi: w_ref.at[p, :, lo:hi], w_ref, *sems[0:2])
        gathers = (_WeightGather(lambda p, cols: _block(w_ref, me, cols), w_ref, *sems[2:4], REST_COLS),
                   _WeightGather(lambda p, cols: w3_ref.at[me], w3_ref, *sems[4:6], _whole),
                   _WeightGather(lambda p, cols: cw_ref.at[me], cw_ref, *sems[6:8], _whole))

        @pl.when(hp == 0)
        def _():
            pieces.start(later)
            for g in gathers:
                g.start()

        for h in range(hpr):
            @pl.when(hp == h)
            def _(h=h):
                if h > 0:
                    pieces.wait_recv(range(3 * h, 3 * h + 3))
                fetch = []
                for seg in range(3):
                    p, lo = _qkv_piece(h, seg)
                    fetch.append(pltpu.make_async_copy(w_ref.at[p, :, lo:lo + 128], w_tile.at[seg], tile_sems.at[seg]))
                    fetch[-1].start()
                for cp in fetch:
                    cp.wait()

        def project(i, carry):
            rows = pl.ds(pl.multiple_of(i * PROJ_ROWS, PROJ_ROWS), PROJ_ROWS)
            ub = u_ref[rows, :]
            for seg, ref in enumerate((q_ref, k_ref, v_ref)):
                ref[rows, :] = _dot(ub, w_tile[seg])
            return carry

        lax.fori_loop(0, S // PROJ_ROWS, project, 0)

        tri_le, dist, low = _fold_masks()
        low_b = low.astype(F32).astype(BF)
        high_b = 1.0 - low_b
        head_sum = _head_sum_matrix()
        ones_b = jnp.ones((2 * QB, QB), BF)
        m_s[...] = jnp.full(m_s.shape, NEG, F32)
        l_s[...] = jnp.zeros(l_s.shape, F32)
        acc[...] = jnp.zeros(acc.shape, F32)

        for d in DILATIONS:
            slope = [sl_ref[2 * hp + a] * float(d) for a in range(2)]
            bias = [slope[a] * dist for a in range(2)]

            def block(b, d=d, slope=slope, bias=bias):
                n, cur0, prev0 = _block_starts(b, d, S)
                cur, prev = _rows(cur0, QB, d), _rows(prev0, QB, d)
                has_prev = n > 0
                q2 = (q_ref[cur, :] * 0.125).astype(BF)
                qs = jnp.concatenate([q2 * low_b, q2 * high_b], axis=0)
                vp = v_ref[prev, :]
                kp_b = k_ref[prev, :].astype(BF)
                kcat = jnp.concatenate([kp_b, k_ref[cur, :].astype(BF)], axis=0)
                vcat = jnp.concatenate([vp, v_ref[cur, :]], axis=0).astype(BF)
                s2 = _dot_nt(qs, kcat)
                e2 = _dot(_hi_lo(q2.astype(F32) * kp_b.astype(F32)), head_sum)
                p_rows, alpha_c, pe_c = [[], []], [[], []], [[], []]
                for c in range(QB // CH):
                    rows_c = _rows(cur0 + c * CH * d, CH, d)
                    cs = slice(c * CH, (c + 1) * CH)
                    tri = tri_le[cs]
                    valid = jnp.logical_or(tri, has_prev)
                    for a in range(2):
                        hs = slice(a * QB + c * CH, a * QB + (c + 1) * CH)
                        comb = jnp.where(valid, jnp.where(tri, s2[hs, QB:], s2[hs, :QB]) - bias[a][cs], NEG)
                        e = jnp.where(has_prev, e2[cs, a * QB:(a + 1) * QB] - slope[a] * float(QB), NEG)
                        m_old = m_s.at[a][rows_c, :]
                        m_new = jnp.maximum(jnp.maximum(m_old, jnp.max(comb, axis=-1, keepdims=True)), e)
                        m_s.at[a][rows_c, :] = m_new
                        p = jnp.exp(comb - m_new)
                        pe_c[a].append(jnp.exp(e - m_new))
                        alpha_c[a].append(jnp.exp(m_old - m_new))
                        p_rows[a].append(jnp.concatenate([jnp.where(tri, 0.0, p).astype(BF),
                                                          jnp.where(tri, p, 0.0).astype(BF)], axis=1))
                pv = _dot(jnp.concatenate(p_rows[0] + p_rows[1], axis=0), jnp.concatenate([vcat, ones_b], axis=1))
                for c in range(QB // CH):
                    rows_c = _rows(cur0 + c * CH * d, CH, d)
                    cs = slice(c * CH, (c + 1) * CH)
                    lowc = low[cs]
                    for a in range(2):
                        l_a = l_s.at[a]
                        l_a[rows_c, :] = (alpha_c[a][c] * l_a[rows_c, :] + pv[a * QB + c * CH:a * QB + (c + 1) * CH, QB:]
                                          + pe_c[a][c])
                    acc[rows_c, :] = (jnp.where(lowc, alpha_c[0][c], alpha_c[1][c]) * acc[rows_c, :]
                                      + jnp.where(lowc, pv[cs, :QB], pv[QB + c * CH:QB + (c + 1) * CH, :QB])
                                      + jnp.where(lowc, pe_c[0][c], pe_c[1][c]) * vp[cs])

            def several(it, carry, block=block):
                for u in range(ATT_UNROLL):
                    block(it * ATT_UNROLL + u)
                return carry

            lax.fori_loop(0, n_blocks // ATT_UNROLL, several, 0)


        def finish(i, carry):
            rows = pl.ds(pl.multiple_of(i * QB, QB), QB)
            l0, l1 = l_s[0, rows, :], l_s[1, rows, :]
            o = acc[rows, :] / jnp.where(low, l0, l1)
            o_ref[rows, :] = o
            lse_ref[0, rows, :] = m_s[0, rows, :] + jnp.log(l0)
            lse_ref[1, rows, :] = m_s[1, rows, :] + jnp.log(l1)
            return carry

        lax.fori_loop(0, n_blocks, finish, 0)

        @pl.when(hp == hpr - 1)
        def _():
            pieces.wait_send(later)
            for g in gathers:
                g.finish()

    col = pl.BlockSpec((S, 128), lambda h: (0, h))
    act = jax.ShapeDtypeStruct((S, D), F32)
    gathered = (w_all, w3_all, cw_all)
    return pl.pallas_call(
        body, name="attn_fwd", grid=(hpr,),
        in_specs=[SMEM_SPEC, VMEM_SPEC, ANY_SPEC, ANY_SPEC, ANY_SPEC],
        out_specs=(col, pl.BlockSpec((2, S, 128), lambda h: (0, 0, h)), col, col, col, ANY_SPEC, ANY_SPEC, ANY_SPEC),
        out_shape=(act, jax.ShapeDtypeStruct((2, S, D), F32), act, act, act,
                   *[jax.ShapeDtypeStruct(t.shape, t.dtype) for t in gathered]),
        scratch_shapes=([pltpu.VMEM((S, 128), F32), pltpu.VMEM((2, S, 128), F32), pltpu.VMEM((2, S, 128), F32),
                         pltpu.VMEM((3, D, 128), BF), pltpu.SemaphoreType.DMA((3,))]
                        + _piece_sems(3 * hpr) + WEIGHT_GATHER_SEMS * 3),
        input_output_aliases={2: 5, 3: 6, 4: 7},
        compiler_params=_params(1),
    )(slopes, u, *gathered)


def _set_rows(shape, rows):
    idx = lax.broadcasted_iota(jnp.int32, shape, 0)
    out = jnp.zeros(shape, F32)
    for r, val in rows.items():
        out = out + jnp.where(idx == r, val, 0.0)
    return out


def _mid(yc_in, pa_mid, o, x2, target, b_merge, final_g, w3):
    S = x2.shape[0]
    tm = ROW_TILE
    nsteps = S // tm
    tile = pl.BlockSpec((tm, D), lambda i: (i, 0))

    def body(yc_ref, za_ref, gcp_ref, gap_ref, o_ref, x_ref, t_ref, b_ref, fg_ref, w_ref,
             dh_ref, dmid_ref, do_ref, dyc_ref, gw_ref, small_ref, acc, stage):
        i = pl.program_id(0)

        @pl.when(i == 0)
        def _():
            acc[...] = jnp.zeros_like(acc)
            small_ref[...] = jnp.zeros_like(small_ref)

        wc, wa, wo = w_ref[0], w_ref[1], w_ref[2]
        z = za_ref[...].astype(F32)
        sg = _sigmoid(z)
        ov = o_ref[...]
        yc_in_b, ya_in_b = yc_ref[...], (z * sg * ov).astype(BF)
        yc = _dot(yc_in_b, wc)
        ya = _dot(ya_in_b, wa)
        b = b_ref[...]
        gc = _sigmoid(gcp_ref[...].astype(F32) + b[:, :D])
        ga = _sigmoid(gap_ref[...].astype(F32) + b[:, D:])
        merged = gc * yc + ga * ya
        merged_b = merged.astype(BF)
        h = x_ref[...] + _dot(merged_b, wo)
        r2 = lax.rsqrt(jnp.mean(h * h, axis=-1, keepdims=True) + EPS)
        n = h * r2
        fg = fg_ref[...]
        err = n * fg - t_ref[...]
        loss = 0.5 * jnp.sum(jnp.sum(err * err, axis=-1, keepdims=True) / D, axis=0, keepdims=True)
        dy = err / D
        g_fg = jnp.sum(dy * n, axis=0, keepdims=True)
        dn = dy * fg
        dh = r2 * (dn - n * jnp.mean(dn * n, axis=-1, keepdims=True))
        dh_ref[...] = dh
        dh_b = dh.astype(BF)
        dmerged = _dot_nt(dh_b, wo)
        acc[2] += _dot(merged.T.astype(BF), dh_b)
        dyc = (dmerged * gc).astype(BF)
        dya = (dmerged * ga).astype(BF)
        dgcp = dmerged * yc * gc * (1.0 - gc)
        dgap = dmerged * ya * ga * (1.0 - ga)
        dmid_ref[1] = dgcp.astype(BF)
        dmid_ref[2] = dgap.astype(BF)
        acc[0] += _dot(yc_in_b.astype(F32).T.astype(BF), dyc)
        acc[1] += _dot(ya_in_b.astype(F32).T.astype(BF), dya)
        dyc_ref[...] = _dot_nt(dyc, wc).astype(BF)
        dya_in = _dot_nt(dya, wa)
        do_ref[...] = dya_in * (z * sg)
        dmid_ref[0] = (dya_in * ov * (sg * (1.0 + z * (1.0 - sg)))).astype(BF)
        small_ref[...] += _set_rows((8, D), {
            1: jnp.sum(dgcp, axis=0, keepdims=True), 2: jnp.sum(dgap, axis=0, keepdims=True),
            3: g_fg, 7: jnp.broadcast_to(loss, (1, D))})

        @pl.when(i == nsteps - 1)
        def _():
            for p in range(N_DEV):
                for a in range(3):
                    stage[...] = acc[a, p * ROW_SHARD:(p + 1) * ROW_SHARD, :].astype(BF)
                    pltpu.sync_copy(stage, gw_ref.at[p, a])

    return pl.pallas_call(
        body, name="mid", grid=(nsteps,),
        in_specs=[tile, pl.BlockSpec((tm, D), lambda i: (i, 0)), pl.BlockSpec((tm, D), lambda i: (i, 1)),
                  pl.BlockSpec((tm, D), lambda i: (i, 2)), tile, tile, tile,
                  pl.BlockSpec((1, 2 * D), lambda i: (0, 0)), pl.BlockSpec((1, D), lambda i: (0, 0)), VMEM_SPEC],
        out_specs=(tile, pl.BlockSpec((3, tm, D), lambda i: (0, i, 0)), tile, tile,
                   ANY_SPEC, pl.BlockSpec((8, D), lambda i: (0, 0))),
        out_shape=(jax.ShapeDtypeStruct((S, D), F32), jax.ShapeDtypeStruct((3, S, D), BF),
                   jax.ShapeDtypeStruct((S, D), F32), jax.ShapeDtypeStruct((S, D), BF),
                   jax.ShapeDtypeStruct((N_DEV, 3, ROW_SHARD, D), BF), jax.ShapeDtypeStruct((8, D), F32)),
        scratch_shapes=[pltpu.VMEM((3, D, D), F32), pltpu.VMEM((ROW_SHARD, D), BF)],
        compiler_params=_params(1),
    )(yc_in, pa_mid, pa_mid, pa_mid, o, x2, target, b_merge, final_g, w3)


def _conv_bwd(dyc_in, pa, cw8):
    S = pa.shape[0]
    tm, tc = CONV_TM, CONV_TC
    nct = D // tc
    nrt = S // tm
    last_halo = S // HALO - 1

    def seg(s):
        return pl.BlockSpec((tm, tc), lambda j, i, s=s: (i, s * nct + j))

    def halo_before(s):
        return pl.BlockSpec((HALO, tc), lambda j, i, s=s: (jnp.maximum(i * (tm // HALO) - 1, 0), s * nct + j))

    def halo_after(s):
        return pl.BlockSpec((HALO, tc), lambda j, i, s=s: (jnp.minimum((i + 1) * (tm // HALO), last_halo), s * nct + j))

    def body(dy, xc, bg, cg, zc, xch, cgh, dyn, bgn, zcn, cw, dout, gcw):
        i = pl.program_id(1)

        @pl.when(i == 0)
        def _():
            gcw[...] = jnp.zeros_like(gcw)

        xcv, cgv = xc[...].astype(F32), cg[...].astype(F32)
        a = cgv * xcv
        ah = jnp.where(i > 0, cgh[...].astype(F32) * xch[...].astype(F32), 0.0)
        row = lax.broadcasted_iota(jnp.int32, (tm, tc), 0)
        a1 = jnp.where(row == 0, ah[HALO - 1:HALO, :], pltpu.roll(a, 1, 0))
        a2 = jnp.where(row == 0, ah[HALO - 2:HALO - 1, :],
                       jnp.where(row == 1, ah[HALO - 1:HALO, :], pltpu.roll(a, 2, 0)))
        w = cw[...]
        conv = w[0:1, :] * a2 + w[1:2, :] * a1 + w[2:3, :] * a
        z = zc[...].astype(F32)
        sg = _sigmoid(z)
        silu = z * sg
        bgv = bg[...].astype(F32)
        dyv = dy[...].astype(F32)
        dout[3] = (dyv * bgv * conv * (sg * (1.0 + z * (1.0 - sg)))).astype(BF)
        dout[1] = (dyv * silu * conv).astype(BF)
        dc = dyv * silu * bgv
        zn = zcn[...].astype(F32)
        dcn = dyn[...].astype(F32) * (zn * _sigmoid(zn)) * bgn[...].astype(F32)
        dcn = jnp.where(i < nrt - 1, dcn, 0.0)
        dc1 = jnp.where(row == tm - 1, dcn[0:1, :], pltpu.roll(dc, tm - 1, 0))
        dc2 = jnp.where(row == tm - 1, dcn[1:2, :],
                        jnp.where(row == tm - 2, dcn[0:1, :], pltpu.roll(dc, tm - 2, 0)))
        da = w[2:3, :] * dc + w[1:2, :] * dc1 + w[0:1, :] * dc2
        dout[2] = (da * xcv).astype(BF)
        dout[0] = (da * cgv).astype(BF)
        gcw[...] += _set_rows((8, tc), {
            4: jnp.sum(dc * a2, axis=0, keepdims=True), 5: jnp.sum(dc * a1, axis=0, keepdims=True),
            6: jnp.sum(dc * a, axis=0, keepdims=True)})

    return pl.pallas_call(
        body, name="conv_bwd", grid=(nct, nrt),
        in_specs=[pl.BlockSpec((tm, tc), lambda j, i: (i, j)), seg(0), seg(1), seg(2), seg(3),
                  halo_before(0), halo_before(2),
                  pl.BlockSpec((HALO, tc), lambda j, i: (jnp.minimum((i + 1) * (tm // HALO), last_halo), j)),
                  halo_after(1), halo_after(3), pl.BlockSpec((8, tc), lambda j, i: (0, j))],
        out_specs=(pl.BlockSpec((4, tm, tc), lambda j, i: (0, i, j)), pl.BlockSpec((8, tc), lambda j, i: (0, j))),
        out_shape=(jax.ShapeDtypeStruct((4, S, D), BF), jax.ShapeDtypeStruct((8, D), F32)),
        compiler_params=_params(2),
    )(dyc_in, pa, pa, pa, pa, pa, pa, dyc_in, pa, pa, cw8)


def _attn_bwd(q, k, v, slopes, do, o, lse, g_in, g_3):
    S = q.shape[0]
    hpr = HEAD_PAIRS
    n_blocks = S // QB

    def body(sl_ref, q_ref, k_ref, v_ref, do_ref, o_ref, lse_ref, gin_ref, g3_ref, out_ref, rin_ref, r3_ref,
             dq_s, dk_s, dv_s, dd_s, *sems):
        hp = pl.program_id(0)
        exchanges = (_GradExchange(gin_ref, rin_ref, *sems[:3], _shard_cols((0, SEG0_ATTN * D), (SEG0_MID * D, IN_COLS))),
                     _GradExchange(g3_ref, r3_ref, *sems[3:], lambda p: ()))

        @pl.when(hp == 0)
        def _():
            for ex in exchanges:
                ex.start()

        tri_le, dist, low = _fold_masks()
        low_b = low.astype(F32).astype(BF)
        high_b = 1.0 - low_b
        head_sum = _head_sum_matrix()
        dq_s[...] = jnp.zeros(dq_s.shape, F32)
        dk_s[...] = jnp.zeros(dk_s.shape, F32)
        dv_s[...] = jnp.zeros(dv_s.shape, F32)

        def row_dots(i, carry):
            rows = pl.ds(pl.multiple_of(i * QB, QB), QB)
            dd = _dot(_hi_lo(do_ref[rows, :] * o_ref[rows, :]), head_sum)
            dd_s[0, rows, :] = dd[:, :QB]
            dd_s[1, rows, :] = dd[:, QB:]
            return carry

        lax.fori_loop(0, n_blocks, row_dots, 0)

        for d in DILATIONS:
            slope = [sl_ref[2 * hp + a] * float(d) for a in range(2)]
            bias = [slope[a] * dist for a in range(2)]

            def block(b, d=d, slope=slope, bias=bias):
                n, cur, prev = _block_rows(b, d, S)
                has_prev = n > 0
                valid = jnp.logical_or(tri_le, has_prev)
                q2f = q_ref[cur, :] * 0.125
                q2 = q2f.astype(BF)
                qs = jnp.concatenate([q2 * low_b, q2 * high_b], axis=0)
                kp, vp = k_ref[prev, :], v_ref[prev, :]
                kp_b, vp_b = kp.astype(BF), vp.astype(BF)
                kcat = jnp.concatenate([kp_b, k_ref[cur, :].astype(BF)], axis=0)
                vcat = jnp.concatenate([vp_b, v_ref[cur, :].astype(BF)], axis=0)
                do2f = do_ref[cur, :]
                do2 = do2f.astype(BF)
                dos = jnp.concatenate([do2 * low_b, do2 * high_b], axis=0)
                s2 = _dot_nt(qs, kcat)
                dp2 = _dot_nt(dos, vcat)
                diag2 = _dot(jnp.concatenate([_hi_lo(q2.astype(F32) * kp_b.astype(F32)),
                                              _hi_lo(do2.astype(F32) * vp_b.astype(F32))], axis=0), head_sum)
                p_rows, ds_rows, pe_h, dse_h = [], [], [], []
                for a in range(2):
                    hs = slice(a * QB, (a + 1) * QB)
                    sp, sc = s2[hs, :QB], s2[hs, QB:]
                    dpp, dpc = dp2[hs, :QB], dp2[hs, QB:]
                    lse_a, dd_a = lse_ref.at[a][cur, :], dd_s.at[a][cur, :]
                    comb = jnp.where(tri_le, sc, sp) - bias[a]
                    e = diag2[:QB, hs] - slope[a] * float(QB)
                    p = jnp.where(valid, jnp.exp(comb - lse_a), 0.0)
                    pe = jnp.where(has_prev, jnp.exp(e - lse_a), 0.0)
                    ds = p * (jnp.where(tri_le, dpc, dpp) - dd_a)
                    dse_h.append(pe * (diag2[QB:, hs] - dd_a))
                    pe_h.append(pe)
                    p_rows.append(jnp.concatenate([jnp.where(tri_le, 0.0, p).astype(BF),
                                                   jnp.where(tri_le, p, 0.0).astype(BF)], axis=1))
                    ds_rows.append(jnp.concatenate([jnp.where(tri_le, 0.0, ds).astype(BF),
                                                    jnp.where(tri_le, ds, 0.0).astype(BF)], axis=1))
                pst = jnp.concatenate(p_rows, axis=0)
                dst = jnp.concatenate(ds_rows, axis=0)
                pe2 = jnp.where(low, pe_h[0], pe_h[1])
                dse2 = jnp.where(low, dse_h[0], dse_h[1])
                dq = _dot(dst, kcat)
                dq_s[cur, :] += (jnp.where(low, dq[:QB], dq[QB:]) + dse2 * kp) * 0.125
                dk = _dot_tn(dst, qs)
                dv = _dot_tn(pst, dos)
                dk_s[prev, :] += dk[:QB] + dse2 * q2f
                dk_s[cur, :] += dk[QB:]
                dv_s[prev, :] += dv[:QB] + pe2 * do2f
                dv_s[cur, :] += dv[QB:]

            def several(it, carry, block=block):
                for u in range(ATT_UNROLL):
                    block(it * ATT_UNROLL + u)
                return carry

            lax.fori_loop(0, n_blocks // ATT_UNROLL, several, 0)

        def finish(i, carry):
            rows = pl.ds(pl.multiple_of(i * QB, QB), QB)
            out_ref[0, rows, :] = dq_s[rows, :].astype(BF)
            out_ref[1, rows, :] = dk_s[rows, :].astype(BF)
            out_ref[2, rows, :] = dv_s[rows, :].astype(BF)
            return carry

        lax.fori_loop(0, n_blocks, finish, 0)

        @pl.when(hp == hpr - 1)
        def _():
            for ex in exchanges:
                ex.finish()

    col = pl.BlockSpec((S, 128), lambda h: (0, h))
    return pl.pallas_call(
        body, name="attn_bwd", grid=(hpr,),
        in_specs=[SMEM_SPEC, col, col, col, col, col, pl.BlockSpec((2, S, 128), lambda h: (0, 0, h)),
                  ANY_SPEC, ANY_SPEC],
        out_specs=(pl.BlockSpec((3, S, 128), lambda h: (0, 0, h)), ANY_SPEC, ANY_SPEC),
        out_shape=(jax.ShapeDtypeStruct((3, S, D), BF), jax.ShapeDtypeStruct(g_in.shape, BF),
                   jax.ShapeDtypeStruct(g_3.shape, BF)),
        scratch_shapes=([pltpu.VMEM((S, 128), F32)] * 3 + [pltpu.VMEM((2, S, 128), F32)]
                        + GRAD_EXCHANGE_SEMS + GRAD_EXCHANGE_SEMS),
        compiler_params=_params(1),
    )(slopes, q, k, v, do, o, lse, g_in, g_3)


WG_TN = 256
SEG0_CONV, SEG0_ATTN, SEG0_MID = 0, 4, 7


def _wgrad_in(ut, d_group, seg0, g_in, name):
    S = ut.shape[1]
    tn = WG_TN
    per_seg = D // tn
    per_shard = W_IN_SHARD // tn
    n_tiles = d_group.shape[0] * per_seg
    tile0 = seg0 * per_seg

    def body(ut_ref, d_ref, *rest):
        rest[-1][0] = _dot(ut_ref[...], d_ref[0]).astype(BF)

    operands, in_specs, aliases = [ut, d_group], [VMEM_SPEC, pl.BlockSpec((1, S, tn), lambda t: (t // per_seg, 0, t % per_seg))], {}
    if g_in is not None:
        operands.append(g_in)
        in_specs.append(ANY_SPEC)
        aliases = {2: 0}
    return pl.pallas_call(
        body, name=name, grid=(n_tiles,), in_specs=in_specs,
        out_specs=pl.BlockSpec((1, D, tn), lambda t: ((tile0 + t) // per_shard, 0, (tile0 + t) % per_shard)),
        out_shape=jax.ShapeDtypeStruct((N_DEV, D, W_IN_SHARD), BF),
        input_output_aliases=aliases,
        compiler_params=_params(1),
    )(*operands)


def _dgrad_norm_bwd(d_conv, d_attn, d_mid, w_all, x2, dh, norm_g, g_in, r_in):
    S = x2.shape[0]
    tm = ROW_TILE
    nsteps = S // tm
    tile = pl.BlockSpec((tm, D), lambda i: (i, 0))
    pieces = _proj_pieces()

    def body(a_ref, b_ref, c_ref, w_ref, x_ref, dh_ref, g_ref, gin_ref, rin_in_ref, gx_ref, small_ref, rin_ref, *sems):
        i = pl.program_id(0)
        exchange = _GradExchange(gin_ref, rin_ref, *sems, _shard_cols((SEG0_ATTN * D, SEG0_MID * D)))

        @pl.when(i == 0)
        def _():
            small_ref[...] = jnp.zeros_like(small_ref)
            exchange.start()

        groups = (a_ref, b_ref, c_ref)
        du = jnp.zeros((tm, D), F32)
        for s, sc, p, pc, width in pieces:
            g = 0 if s < 4 else (1 if s < 7 else 2)
            local = s - (0, 4, 7)[g]
            du = du + _dot_nt(groups[g][local, :, sc:sc + width], w_ref[p, :, pc:pc + width])
        xv = x_ref[...]
        r = lax.rsqrt(jnp.mean(xv * xv, axis=-1, keepdims=True) + EPS)
        n = xv * r
        dn = du * g_ref[...]
        gx_ref[...] = dh_ref[...] + r * (dn - n * jnp.mean(dn * n, axis=-1, keepdims=True))
        small_ref[...] += _set_rows((8, D), {0: jnp.sum(du * n, axis=0, keepdims=True)})

        @pl.when(i == nsteps - 1)
        def _():
            exchange.finish()

    return pl.pallas_call(
        body, name="dgrad_norm_bwd", grid=(nsteps,),
        in_specs=[pl.BlockSpec((4, tm, D), lambda i: (0, i, 0)), pl.BlockSpec((3, tm, D), lambda i: (0, i, 0)),
                  pl.BlockSpec((3, tm, D), lambda i: (0, i, 0)), VMEM_SPEC, tile, tile,
                  pl.BlockSpec((1, D), lambda i: (0, 0)), ANY_SPEC, ANY_SPEC],
        out_specs=(tile, pl.BlockSpec((8, D), lambda i: (0, 0)), ANY_SPEC),
        out_shape=(jax.ShapeDtypeStruct((S, D), F32), jax.ShapeDtypeStruct((8, D), F32),
                   jax.ShapeDtypeStruct(r_in.shape, BF)),
        scratch_shapes=GRAD_EXCHANGE_SEMS,
        input_output_aliases={8: 2},
        compiler_params=_params(1),
    )(d_conv, d_attn, d_mid, w_all, x2, dh, norm_g, g_in, r_in)


def _adamw_math(w, g, m, v):
    m = ADAM_B1 * m + (1.0 - ADAM_B1) * g
    v = ADAM_B2 * v + (1.0 - ADAM_B2) * (g * g)
    m_hat = m / (1.0 - ADAM_B1 ** ADAM_STEP)
    v_hat = v / (1.0 - ADAM_B2 ** ADAM_STEP)
    delta = -ADAM_LR * (m_hat / (jnp.sqrt(v_hat) + ADAM_EPS) + ADAM_WD * w)
    return delta, m, v


def _sum_adamw(parts, w, m, v, tm, name):
    R, C = w.shape
    tile = pl.BlockSpec((tm, C), lambda i: (i, 0))

    def body(p_ref, w_ref, m_ref, v_ref, g_out, d_out, m_out, v_out):
        g = p_ref[0].astype(F32)
        for s in range(1, N_DEV):
            g = g + p_ref[s].astype(F32)
        g_out[...] = g
        d_out[...], m_out[...], v_out[...] = _adamw_math(w_ref[...], g, m_ref[...], v_ref[...])

    shape = jax.ShapeDtypeStruct((R, C), F32)
    return pl.pallas_call(
        body, name=name, grid=(R // tm,),
        in_specs=[pl.BlockSpec((N_DEV, tm, C), lambda i: (0, i, 0)), tile, tile, tile],
        out_specs=(tile, tile, tile, tile), out_shape=(shape, shape, shape, shape),
        compiler_params=_params(1),
    )(parts, w, m, v)


def _adamw(g, w, m, v, name):
    def body(g_ref, w_ref, m_ref, v_ref, d_out, m_out, v_out):
        d_out[...], m_out[...], v_out[...] = _adamw_math(w_ref[...], g_ref[...], m_ref[...], v_ref[...])

    shape = jax.ShapeDtypeStruct(w.shape, F32)
    return pl.pallas_call(
        body, name=name, in_specs=[VMEM_SPEC] * 4, out_specs=(VMEM_SPEC,) * 3, out_shape=(shape, shape, shape),
    )(g, w, m, v)


def _alibi_slopes():
    return jnp.exp2(-8.0 * jnp.arange(1, N_HEADS + 1, dtype=F32) / N_HEADS)


def _local_step(x2, target, norm_g, b_merge, final_g, w_all, w3_all, cw_all):
    slopes = _alibi_slopes()
    u, ut = _norm(x2, norm_g)
    o, lse, q, k, v, w_all, w3_all, cw_all = _attn_fwd(u, slopes, w_all, w3_all, cw_all)
    w3 = jnp.transpose(w3_all, (1, 0, 2, 3)).reshape(3, D, D)
    cw8 = jnp.transpose(cw_all, (1, 0, 2)).reshape(8, D)
    pa = _proj_cols(u, w_all, SEG0_CONV, 4, BF, "proj_conv")
    yc_in = _conv_fwd(pa, cw8)
    pa_mid = _proj_cols(u, w_all, SEG0_MID, 3, BF, "proj_mid")
    dh, d_mid, do, dyc_in, g_3, small_mid = _mid(yc_in, pa_mid, o, x2, target, b_merge, final_g, w3)
    g_in = _wgrad_in(ut, d_mid, SEG0_MID, None, "wgrad_in_mid")
    d_conv, small_conv = _conv_bwd(dyc_in, pa, cw8)
    g_in = _wgrad_in(ut, d_conv, SEG0_CONV, g_in, "wgrad_in_conv")
    d_attn, r_in, r_3 = _attn_bwd(q, k, v, slopes, do, o, lse, g_in, g_3)
    g_in = _wgrad_in(ut, d_attn, SEG0_ATTN, g_in, "wgrad_in_attn")
    grad_x, small_norm, r_in = _dgrad_norm_bwd(d_conv, d_attn, d_mid, w_all, x2, dh, norm_g, g_in, r_in)
    return grad_x, r_in, r_3, small_mid, small_conv, small_norm


def kernel(x, norm_g, w_in, b_merge, conv_w, w_out_conv, w_out_attn, w_o, final_g, loss_target, m_norm_g, m_w_in, m_b_merge, m_conv_w, m_w_out_conv, m_w_out_attn, m_w_o, m_final_g, v_norm_g, v_w_in, v_b_merge, v_conv_w, v_w_out_conv, v_w_out_attn, v_w_o, v_final_g):
    me = 4 * lax.axis_index("x") + 2 * lax.axis_index("y") + lax.axis_index("c")
    stack3 = lambda a, b, c: jnp.concatenate([a, b, c], axis=0)
    pad8 = lambda a: jnp.pad(a, ((0, 8 - a.shape[0]), (0, 0)))

    w3_shard = stack3(w_out_conv, w_out_attn, w_o)
    w_all, w3_all, cw_all = _gather_first_weights(w_in[0], w3_shard, pad8(conv_w[0]))

    final_g2 = final_g.reshape(1, D)
    grad_x, r_in, r_3, small_mid, small_conv, small_norm = _local_step(
        x[0], loss_target[0], norm_g, b_merge, final_g2, w_all, w3_all, cw_all)

    small = _allreduce_small(small_mid, small_conv, small_norm)

    g_w_in, d_w_in, nm_w_in, nv_w_in = _sum_adamw(r_in, w_in[0], m_w_in[0], v_w_in[0], 128, "adamw_w_in")
    g_w3, d_w3, nm_w3, nv_w3 = _sum_adamw(
        r_3.reshape(N_DEV, 3 * ROW_SHARD, D), w3_shard.reshape(3 * ROW_SHARD, D),
        stack3(m_w_out_conv, m_w_out_attn, m_w_o).reshape(3 * ROW_SHARD, D),
        stack3(v_w_out_conv, v_w_out_attn, v_w_o).reshape(3 * ROW_SHARD, D), ROW_SHARD, "adamw_w3")

    def pack(ng, bm, fg):
        return pad8(jnp.concatenate([ng, bm.reshape(2, D), fg.reshape(1, D)], axis=0))

    d_s, nm_s, nv_s = _adamw(small, pack(norm_g, b_merge, final_g), pack(m_norm_g, m_b_merge, m_final_g),
                             pack(v_norm_g, v_b_merge, v_final_g), "adamw_small")
    g_cw = lax.dynamic_slice(small, (4, me * ROW_SHARD), (3, ROW_SHARD))
    d_cw, nm_cw, nv_cw = _adamw(g_cw, conv_w[0], m_conv_w[0], v_conv_w[0], "adamw_conv_w")

    loss = small[7, 0]
    split3 = lambda t: tuple(t[a * ROW_SHARD:(a + 1) * ROW_SHARD][None] for a in range(3))
    unpack = lambda t: (t[0:1], t[1:3].reshape(1, 2 * D), t[3])

    def leaves(in_, small_, cw_, w3_):
        ng, bm, fg = unpack(small_)
        wc, wa, wo = split3(w3_)
        return (ng, in_[None], bm, cw_[None], wc, wa, wo, fg)

    return (loss, grad_x[None],
            *leaves(g_w_in, small, g_cw, g_w3),
            *leaves(d_w_in, d_s, d_cw, d_w3),
            *leaves(nm_w_in, nm_s, nm_cw, nm_w3),
            *leaves(nv_w_in, nv_s, nv_cw, nv_w3))
```

```python
import functools

import jax
import jax.numpy as jnp
from jax import lax
from jax.experimental import pallas as pl
from jax.experimental.pallas import tpu as pltpu

D = 1024
N_HEADS = 16
HEAD_DIM = 64
N_SEG = 10
IN_COLS = N_SEG * D
N_DEV = 8
W_IN_SHARD = IN_COLS // N_DEV
ROW_SHARD = D // N_DEV
QB = 128
DILATIONS = (1, 4, 16)
EPS = 1e-6
NEG = -1e30
BF = jnp.bfloat16
F32 = jnp.float32
MESH = pl.DeviceIdType.MESH

ADAM_LR = 0.001
ADAM_B1 = 0.9
ADAM_B2 = 0.999
ADAM_EPS = 1e-08
ADAM_WD = 0.01
ADAM_STEP = 10

V7X_VMEM_BYTES = 64 * 1024 * 1024
VMEM_LIMIT = V7X_VMEM_BYTES - 8 * 1024 * 1024
ROW_TILE = 256

VMEM_SPEC = pl.BlockSpec(memory_space=pltpu.VMEM)
ANY_SPEC = pl.BlockSpec(memory_space=pl.ANY)
SMEM_SPEC = pl.BlockSpec(memory_space=pltpu.SMEM)


def _params(n_grid_axes, vmem=VMEM_LIMIT):
    return pltpu.CompilerParams(dimension_semantics=("arbitrary",) * n_grid_axes, vmem_limit_bytes=vmem)


def _dot(a, b):
    return jnp.dot(a, b, preferred_element_type=F32)


def _dot_nt(a, b):
    return lax.dot_general(a, b, (((1,), (1,)), ((), ())), preferred_element_type=F32)


def _dot_tn(a, b):
    return lax.dot_general(a, b, (((0,), (0,)), ((), ())), preferred_element_type=F32)


def _sigmoid(z):
    return 1.0 / (1.0 + jnp.exp(-z))


def _my_place():
    x, y, c = lax.axis_index("x"), lax.axis_index("y"), lax.axis_index("c")
    return x, y, c, 4 * x + 2 * y + c


def _peers(x, y, c):
    out = []
    for k in range(1, N_DEV):
        px = 1 - x if k & 4 else x
        py = 1 - y if k & 2 else y
        pc = 1 - c if k & 1 else c
        out.append(((px, py, pc), 4 * px + 2 * py + pc))
    return out


def _device(p):
    return (p >> 2, (p >> 1) & 1, p & 1)


def _shard_cols(*ranges):
    def cols(p):
        found = None
        for lo, hi in ranges:
            a, b = max(lo, p * W_IN_SHARD), min(hi, (p + 1) * W_IN_SHARD)
            if a < b:
                assert found is None
                found = (a - p * W_IN_SHARD, b - p * W_IN_SHARD)
        return found

    return cols


def _whole(p):
    return ()


def _block(ref, idx, cols):
    return ref.at[idx] if cols == () else ref.at[idx, :, cols[0]:cols[1]]


class _WeightGather:
    def __init__(self, src, dst, send_sems, recv_sems, cols):
        self.src, self.dst, self.cols = src, dst, cols
        self.send_sems, self.recv_sems = send_sems, recv_sems
        self.me = _my_place()[3]

    def _copy(self, p, target):
        cols = self.cols(p)
        return pltpu.make_async_remote_copy(
            src_ref=self.src(p, cols), dst_ref=_block(self.dst, p, cols), send_sem=self.send_sems.at[target],
            recv_sem=self.recv_sems.at[p], device_id=_device(target), device_id_type=MESH)

    def _each(self, send, receive):
        for p in range(N_DEV):
            if self.cols(p) is None:
                continue

            def sender(p=p):
                for t in range(N_DEV):
                    if t != p:
                        send(self._copy(p, t))

            pl.when(self.me == p)(sender)
            pl.when(self.me != p)(lambda p=p: receive(self._copy(p, p)))

    def start(self):
        self._each(lambda cp: cp.start(), lambda cp: None)

    def finish(self):
        self._each(lambda cp: cp.wait_send(), lambda cp: cp.wait_recv())


WEIGHT_GATHER_SEMS = [pltpu.SemaphoreType.DMA((N_DEV,)), pltpu.SemaphoreType.DMA((N_DEV,))]
REST_COLS = _shard_cols((0, 4 * D), (7 * D, IN_COLS))
HEAD_PAIRS = D // 128


def _qkv_piece(h, seg):
    col = (4 + seg) * D + 128 * h
    return col // W_IN_SHARD, col % W_IN_SHARD


class _PieceGather:
    def __init__(self, src, dst, send_sems, recv_sems):
        self.src, self.dst, self.send_sems, self.recv_sems = src, dst, send_sems, recv_sems
        self.me = _my_place()[3]

    def _copy(self, i, target):
        p, lo = _qkv_piece(i // 3, i % 3)
        return pltpu.make_async_remote_copy(
            src_ref=self.src(p, lo, lo + 128), dst_ref=self.dst.at[p, :, lo:lo + 128], send_sem=self.send_sems.at[i, target],
            recv_sem=self.recv_sems.at[i], device_id=_device(target), device_id_type=MESH)

    def _owner(self, i, act):
        p = _qkv_piece(i // 3, i % 3)[0]

        def sender():
            for t in range(N_DEV):
                if t != p:
                    act(self._copy(i, t))

        pl.when(self.me == p)(sender)

    def start(self, pieces):
        for i in pieces:
            self._owner(i, lambda cp: cp.start())

    def wait_send(self, pieces):
        for i in pieces:
            self._owner(i, lambda cp: cp.wait_send())

    def wait_recv(self, pieces):
        for i in pieces:
            p = _qkv_piece(i // 3, i % 3)[0]
            pl.when(self.me != p)(lambda i=i, p=p: self._copy(i, p).wait_recv())


def _piece_sems(n):
    return [pltpu.SemaphoreType.DMA((n, N_DEV)), pltpu.SemaphoreType.DMA((n,))]


def _gather_first_weights(w_in, w3, cw):
    def body(w_in_ref, w3_ref, cw_ref, o_in, o_3, o_cw, in_bf, w3_bf, local_sems, *sems):
        me = _my_place()[3]

        def cast_rows(i, carry):
            r = pl.multiple_of(i * 128, 128)
            in_bf[pl.ds(r, 128), :] = w_in_ref[pl.ds(r, 128), :].astype(BF)
            return carry

        lax.fori_loop(0, D // 128, cast_rows, 0)
        for a in range(3):
            w3_bf[a] = w3_ref[a].astype(BF)
        gather = _PieceGather(lambda p, lo, hi: in_bf.at[:, lo:hi], o_in, *sems)
        gather.start(range(3))
        local = [pltpu.make_async_copy(src, dst.at[me], local_sems.at[a])
                 for a, (src, dst) in enumerate(((in_bf, o_in), (w3_bf, o_3), (cw_ref, o_cw)))]
        for cp in local:
            cp.start()
        gather.wait_recv(range(3))
        gather.wait_send(range(3))
        for cp in local:
            cp.wait()

    return pl.pallas_call(
        body, name="gather_first_weights",
        out_shape=(jax.ShapeDtypeStruct((N_DEV, D, W_IN_SHARD), BF),
                   jax.ShapeDtypeStruct((N_DEV, 3, ROW_SHARD, D), BF),
                   jax.ShapeDtypeStruct((N_DEV, 8, 128), F32)),
        in_specs=[VMEM_SPEC, VMEM_SPEC, VMEM_SPEC],
        out_specs=(ANY_SPEC, ANY_SPEC, ANY_SPEC),
        scratch_shapes=[pltpu.VMEM((D, W_IN_SHARD), BF), pltpu.VMEM((3, ROW_SHARD, D), BF),
                        pltpu.SemaphoreType.DMA((3,))] + _piece_sems(3),
        compiler_params=pltpu.CompilerParams(vmem_limit_bytes=VMEM_LIMIT),
    )(w_in, w3, cw)


class _GradExchange:
    def __init__(self, src, dst, send_sems, recv_sems, local_sem, cols):
        self.src, self.dst, self.cols = src, dst, cols
        self.send_sems, self.recv_sems, self.local_sem = send_sems, recv_sems, local_sem
        self.me = _my_place()[3]

    def _remote(self, p, source):
        return pltpu.make_async_remote_copy(
            src_ref=_block(self.src, p, self.cols(p)), dst_ref=_block(self.dst, source, self.cols(p)),
            send_sem=self.send_sems.at[p], recv_sem=self.recv_sems.at[source],
            device_id=_device(p), device_id_type=MESH)

    def _local(self, p):
        return pltpu.make_async_copy(_block(self.src, p, self.cols(p)), _block(self.dst, p, self.cols(p)),
                                     self.local_sem)

    def start(self):
        for p in range(N_DEV):
            if self.cols(p) is None:
                continue
            pl.when(self.me != p)(lambda p=p: self._remote(p, self.me).start())
            pl.when(self.me == p)(lambda p=p: self._local(p).start())

    def finish(self):
        for p in range(N_DEV):
            if self.cols(p) is None:
                continue
            pl.when(self.me != p)(lambda p=p: self._remote(p, self.me).wait_send())

            def receive(p=p):
                self._local(p).wait()
                for s in range(N_DEV):
                    if s != p:
                        self._remote(p, s).wait_recv()

            pl.when(self.me == p)(receive)


GRAD_EXCHANGE_SEMS = [pltpu.SemaphoreType.DMA((N_DEV,)), pltpu.SemaphoreType.DMA((N_DEV,)), pltpu.SemaphoreType.DMA]


def _allreduce_small(p_mid, p_conv, p_norm):
    def body(a_ref, b_ref, c_ref, out_ref, mine, gathered, send_sems, recv_sems):
        x, y, c, me = _my_place()
        mine[...] = a_ref[...] + b_ref[...] + c_ref[...]
        gathered[me] = mine[...]
        remote = []
        for k, (peer, _) in enumerate(_peers(x, y, c)):
            cp = pltpu.make_async_remote_copy(
                src_ref=mine, dst_ref=gathered.at[me], send_sem=send_sems.at[k], recv_sem=recv_sems.at[k],
                device_id=peer, device_id_type=MESH)
            cp.start()
            remote.append(cp)
        for cp in remote:
            cp.wait()
        total = gathered[0]
        for s in range(1, N_DEV):
            total = total + gathered[s]
        out_ref[...] = total

    return pl.pallas_call(
        body, name="allreduce_small",
        out_shape=jax.ShapeDtypeStruct((8, D), F32),
        in_specs=[VMEM_SPEC, VMEM_SPEC, VMEM_SPEC], out_specs=VMEM_SPEC,
        scratch_shapes=[pltpu.VMEM((8, D), F32), pltpu.VMEM((N_DEV, 8, D), F32),
                        pltpu.SemaphoreType.DMA((N_DEV - 1,)), pltpu.SemaphoreType.DMA((N_DEV - 1,))],
    )(p_mid, p_conv, p_norm)


def _proj_pieces():
    cuts = sorted(set(range(0, IN_COLS + 1, D)) | set(range(0, IN_COLS + 1, W_IN_SHARD)))
    return [(lo // D, lo % D, lo // W_IN_SHARD, lo % W_IN_SHARD, hi - lo) for lo, hi in zip(cuts[:-1], cuts[1:])]


def _norm(x2, norm_g):
    S = x2.shape[0]
    tm = ROW_TILE

    def body(x_ref, g_ref, u_ref, ut_ref):
        xv = x_ref[...]
        r = lax.rsqrt(jnp.mean(xv * xv, axis=-1, keepdims=True) + EPS)
        u = xv * r * g_ref[...]
        u_ref[...] = u.astype(BF)
        ut_ref[...] = u.T.astype(BF)

    return pl.pallas_call(
        body, name="norm", grid=(S // tm,),
        in_specs=[pl.BlockSpec((tm, D), lambda i: (i, 0)), pl.BlockSpec((1, D), lambda i: (0, 0))],
        out_specs=(pl.BlockSpec((tm, D), lambda i: (i, 0)), pl.BlockSpec((D, tm), lambda i: (0, i))),
        out_shape=(jax.ShapeDtypeStruct((S, D), BF), jax.ShapeDtypeStruct((D, S), BF)),
        compiler_params=_params(1),
    )(x2, norm_g)


PROJ_TN = 256


def _proj_cols(u, w_all, seg0, n_seg, dtype, name):
    S = u.shape[0]
    tn = PROJ_TN
    per_shard = W_IN_SHARD // tn
    tile0 = seg0 * D // tn

    def body(u_ref, w_ref, out_ref):
        out_ref[...] = _dot(u_ref[...], w_ref[0]).astype(dtype)

    return pl.pallas_call(
        body, name=name, grid=(n_seg * D // tn,),
        in_specs=[VMEM_SPEC, pl.BlockSpec((1, D, tn), lambda t: ((tile0 + t) // per_shard, 0, (tile0 + t) % per_shard))],
        out_specs=pl.BlockSpec((S, tn), lambda t: (0, t)),
        out_shape=jax.ShapeDtypeStruct((S, n_seg * D), dtype),
        compiler_params=_params(1),
    )(u, w_all)


CONV_TM, CONV_TC = 256, 512
HALO = 16


def _conv_fwd(pa, cw8):
    S = pa.shape[0]
    tm, tc = CONV_TM, CONV_TC
    nct = D // tc

    def seg(s):
        return pl.BlockSpec((tm, tc), lambda i, j, s=s: (i, s * nct + j))

    def halo_before(s):
        return pl.BlockSpec((HALO, tc), lambda i, j, s=s: (jnp.maximum(i * (tm // HALO) - 1, 0), s * nct + j))

    def body(xc, bg, cg, zc, xch, cgh, cw, out):
        i = pl.program_id(0)
        a = cg[...].astype(F32) * xc[...].astype(F32)
        ah = cgh[...].astype(F32) * xch[...].astype(F32)
        ah = jnp.where(i > 0, ah, 0.0)
        row = lax.broadcasted_iota(jnp.int32, (tm, tc), 0)
        a1 = jnp.where(row == 0, ah[HALO - 1:HALO, :], pltpu.roll(a, 1, 0))
        a2 = jnp.where(row == 0, ah[HALO - 2:HALO - 1, :],
                       jnp.where(row == 1, ah[HALO - 1:HALO, :], pltpu.roll(a, 2, 0)))
        w = cw[...]
        conv = w[0:1, :] * a2 + w[1:2, :] * a1 + w[2:3, :] * a
        z = zc[...].astype(F32)
        out[...] = (z * _sigmoid(z) * bg[...].astype(F32) * conv).astype(BF)

    return pl.pallas_call(
        body, name="conv_fwd", grid=(S // tm, nct),
        in_specs=[seg(0), seg(1), seg(2), seg(3), halo_before(0), halo_before(2),
                  pl.BlockSpec((8, tc), lambda i, j: (0, j))],
        out_specs=pl.BlockSpec((tm, tc), lambda i, j: (i, j)),
        out_shape=jax.ShapeDtypeStruct((S, D), BF),
        compiler_params=_params(2),
    )(pa, pa, pa, pa, pa, pa, cw8)


ATT_UNROLL = 2


def _fold_masks():
    row = lax.broadcasted_iota(jnp.int32, (QB, QB), 0)
    lane = lax.broadcasted_iota(jnp.int32, (QB, QB), 1)
    tri_le = lane <= row
    dist = jnp.where(tri_le, row - lane, row - lane + QB).astype(F32)
    return tri_le, dist, lane < HEAD_DIM


def _rows(start, size, d):
    return pl.ds(pl.multiple_of(start, size), size) if d == 1 else pl.ds(start, size, stride=d)


def _block_starts(b, d, S):
    nb = S // (QB * d)
    r, n = b // nb, b % nb
    return n, r + n * (QB * d), r + jnp.maximum(n - 1, 0) * (QB * d)


def _block_rows(b, d, S):
    n, cur0, prev0 = _block_starts(b, d, S)
    return n, _rows(cur0, QB, d), _rows(prev0, QB, d)


def _head_sum_matrix():
    r = lax.broadcasted_iota(jnp.int32, (2 * QB, 2 * QB), 0)
    c = lax.broadcasted_iota(jnp.int32, (2 * QB, 2 * QB), 1)
    return (((r % QB) // HEAD_DIM) == (c // QB)).astype(F32).astype(BF)


def _hi_lo(t):
    hi = t.astype(BF)
    return jnp.concatenate([hi, (t - hi.astype(F32)).astype(BF)], axis=1)


PROJ_ROWS = 512


def _attn_fwd(u, slopes, w_all, w3_all, cw_all):
    S = u.shape[0]
    hpr = HEAD_PAIRS
    n_blocks = S // QB
    later = range(3, 3 * hpr)

    def body(sl_ref, u_ref, w_in_ref, w3_in_ref, cw_in_ref, o_ref, lse_ref, q_ref, k_ref, v_ref, w_ref, w3_ref,
             cw_ref, acc, m_s, l_s, w_tile, tile_sems, *sems):
        hp = pl.program_id(0)
        me = _my_place()[3]
        pieces = _PieceGather(lambda p, lo, hi: w_ref.at[p, :, lo:hi], w_ref, *sems[0:2])
        gathers = (_WeightGather(lambda p, cols: _block(w_ref, me, cols), w_ref, *sems[2:4], REST_COLS),
                   _WeightGather(lambda p, cols: w3_ref.at[me], w3_ref, *sems[4:6], _whole),
                   _WeightGather(lambda p, cols: cw_ref.at[me], cw_ref, *sems[6:8], _whole))

        @pl.when(hp == 0)
        def _():
            pieces.start(later)
            for g in gathers:
                g.start()

        for h in range(hpr):
            @pl.when(hp == h)
            def _(h=h):
                if h > 0:
                    pieces.wait_recv(range(3 * h, 3 * h + 3))
                fetch = []
                for seg in range(3):
                    p, lo = _qkv_piece(h, seg)
                    fetch.append(pltpu.make_async_copy(w_ref.at[p, :, lo:lo + 128], w_tile.at[:, seg * 128:(seg + 1) * 128],
                                                       tile_sems.at[seg]))
                    fetch[-1].start()
                for cp in fetch:
                    cp.wait()

        def project(i, carry):
            rows = pl.ds(pl.multiple_of(i * PROJ_ROWS, PROJ_ROWS), PROJ_ROWS)
            qkv = _dot(u_ref[rows, :], w_tile[...])
            for seg, ref in enumerate((q_ref, k_ref, v_ref)):
                ref[rows, :] = qkv[:, seg * 128:(seg + 1) * 128]
            return carry

        lax.fori_loop(0, S // PROJ_ROWS, project, 0)

        tri_le, dist, low = _fold_masks()
        low_b = low.astype(F32).astype(BF)
        high_b = 1.0 - low_b
        head_sum = _head_sum_matrix()
        ones_b = jnp.ones((2 * QB, QB), BF)
        m_s[...] = jnp.full(m_s.shape, NEG, F32)
        l_s[...] = jnp.zeros(l_s.shape, F32)
        acc[...] = jnp.zeros(acc.shape, F32)

        for d in DILATIONS:
            slope = [sl_ref[2 * hp + a] * float(d) for a in range(2)]
            bias = [slope[a] * dist for a in range(2)]

            def block(b, d=d, slope=slope, bias=bias):
                n, cur, prev = _block_rows(b, d, S)
                has_prev = n > 0
                valid = jnp.logical_or(tri_le, has_prev)
                q2 = (q_ref[cur, :] * 0.125).astype(BF)
                qs = jnp.concatenate([q2 * low_b, q2 * high_b], axis=0)
                vp = v_ref[prev, :]
                kp_b = k_ref[prev, :].astype(BF)
                kcat = jnp.concatenate([kp_b, k_ref[cur, :].astype(BF)], axis=0)
                vcat = jnp.concatenate([vp, v_ref[cur, :]], axis=0).astype(BF)
                s2 = _dot_nt(qs, kcat)
                e2 = _dot(_hi_lo(q2.astype(F32) * kp_b.astype(F32)), head_sum)
                p_rows, alpha_h, pe_h = [], [], []
                for a in range(2):
                    sp, sc = s2[a * QB:(a + 1) * QB, :QB], s2[a * QB:(a + 1) * QB, QB:]
                    comb = jnp.where(valid, jnp.where(tri_le, sc, sp) - bias[a], NEG)
                    e = jnp.where(has_prev, e2[:, a * QB:(a + 1) * QB] - slope[a] * float(QB), NEG)
                    m_old = m_s.at[a][cur, :]
                    m_new = jnp.maximum(jnp.maximum(m_old, jnp.max(comb, axis=-1, keepdims=True)), e)
                    m_s.at[a][cur, :] = m_new
                    p = jnp.exp(comb - m_new)
                    pe_h.append(jnp.exp(e - m_new))
                    alpha_h.append(jnp.exp(m_old - m_new))
                    p_rows.append(jnp.concatenate([jnp.where(tri_le, 0.0, p).astype(BF),
                                                   jnp.where(tri_le, p, 0.0).astype(BF)], axis=1))
                pv = _dot(jnp.concatenate(p_rows, axis=0), jnp.concatenate([vcat, ones_b], axis=1))
                for a in range(2):
                    l_a = l_s.at[a]
                    l_a[cur, :] = alpha_h[a] * l_a[cur, :] + pv[a * QB:(a + 1) * QB, QB:] + pe_h[a]
                acc[cur, :] = (jnp.where(low, alpha_h[0], alpha_h[1]) * acc[cur, :]
                               + jnp.where(low, pv[:QB, :QB], pv[QB:, :QB]) + jnp.where(low, pe_h[0], pe_h[1]) * vp)

            def several(it, carry, block=block):
                for u in range(ATT_UNROLL):
                    block(it * ATT_UNROLL + u)
                return carry

            lax.fori_loop(0, n_blocks // ATT_UNROLL, several, 0)

        def finish(i, carry):
            rows = pl.ds(pl.multiple_of(i * QB, QB), QB)
            l0, l1 = l_s[0, rows, :], l_s[1, rows, :]
            o = acc[rows, :] / jnp.where(low, l0, l1)
            o_ref[rows, :] = o
            lse_ref[0, rows, :] = m_s[0, rows, :] + jnp.log(l0)
            lse_ref[1, rows, :] = m_s[1, rows, :] + jnp.log(l1)
            return carry

        lax.fori_loop(0, n_blocks, finish, 0)

        @pl.when(hp == hpr - 1)
        def _():
            pieces.wait_send(later)
            for g in gathers:
                g.finish()

    col = pl.BlockSpec((S, 128), lambda h: (0, h))
    act = jax.ShapeDtypeStruct((S, D), F32)
    gathered = (w_all, w3_all, cw_all)
    return pl.pallas_call(
        body, name="attn_fwd", grid=(hpr,),
        in_specs=[SMEM_SPEC, VMEM_SPEC, ANY_SPEC, ANY_SPEC, ANY_SPEC],
        out_specs=(col, pl.BlockSpec((2, S, 128), lambda h: (0, 0, h)), col, col, col, ANY_SPEC, ANY_SPEC, ANY_SPEC),
        out_shape=(act, jax.ShapeDtypeStruct((2, S, D), F32), act, act, act,
                   *[jax.ShapeDtypeStruct(t.shape, t.dtype) for t in gathered]),
        scratch_shapes=([pltpu.VMEM((S, 128), F32), pltpu.VMEM((2, S, 128), F32), pltpu.VMEM((2, S, 128), F32),
                         pltpu.VMEM((D, 3 * 128), BF), pltpu.SemaphoreType.DMA((3,))]
                        + _piece_sems(3 * hpr) + WEIGHT_GATHER_SEMS * 3),
        input_output_aliases={2: 5, 3: 6, 4: 7},
        compiler_params=_params(1),
    )(slopes, u, *gathered)


def _set_rows(shape, rows):
    idx = lax.broadcasted_iota(jnp.int32, shape, 0)
    out = jnp.zeros(shape, F32)
    for r, val in rows.items():
        out = out + jnp.where(idx == r, val, 0.0)
    return out


def _mid(yc_in, pa_mid, o, x2, target, b_merge, final_g, w3):
    S = x2.shape[0]
    tm = ROW_TILE
    nsteps = S // tm
    tile = pl.BlockSpec((tm, D), lambda i: (i, 0))

    def body(yc_ref, za_ref, gcp_ref, gap_ref, o_ref, x_ref, t_ref, b_ref, fg_ref, w_ref,
             dh_ref, dmid_ref, do_ref, dyc_ref, gw_ref, small_ref, acc, stage):
        i = pl.program_id(0)

        @pl.when(i == 0)
        def _():
            acc[...] = jnp.zeros_like(acc)
            small_ref[...] = jnp.zeros_like(small_ref)

        wc, wa, wo = w_ref[0], w_ref[1], w_ref[2]
        z = za_ref[...].astype(F32)
        sg = _sigmoid(z)
        ov = o_ref[...]
        yc_in_b, ya_in_b = yc_ref[...], (z * sg * ov).astype(BF)
        yc = _dot(yc_in_b, wc)
        ya = _dot(ya_in_b, wa)
        b = b_ref[...]
        gc = _sigmoid(gcp_ref[...].astype(F32) + b[:, :D])
        ga = _sigmoid(gap_ref[...].astype(F32) + b[:, D:])
        merged = gc * yc + ga * ya
        merged_b = merged.astype(BF)
        h = x_ref[...] + _dot(merged_b, wo)
        r2 = lax.rsqrt(jnp.mean(h * h, axis=-1, keepdims=True) + EPS)
        n = h * r2
        fg = fg_ref[...]
        err = n * fg - t_ref[...]
        loss = 0.5 * jnp.sum(jnp.sum(err * err, axis=-1, keepdims=True) / D, axis=0, keepdims=True)
        dy = err / D
        g_fg = jnp.sum(dy * n, axis=0, keepdims=True)
        dn = dy * fg
        dh = r2 * (dn - n * jnp.mean(dn * n, axis=-1, keepdims=True))
        dh_ref[...] = dh
        dh_b = dh.astype(BF)
        dmerged = _dot_nt(dh_b, wo)
        acc[2] += _dot(merged.T.astype(BF), dh_b)
        dyc = (dmerged * gc).astype(BF)
        dya = (dmerged * ga).astype(BF)
        dgcp = dmerged * yc * gc * (1.0 - gc)
        dgap = dmerged * ya * ga * (1.0 - ga)
        dmid_ref[1] = dgcp.astype(BF)
        dmid_ref[2] = dgap.astype(BF)
        acc[0] += _dot(yc_in_b.astype(F32).T.astype(BF), dyc)
        acc[1] += _dot(ya_in_b.astype(F32).T.astype(BF), dya)
        dyc_ref[...] = _dot_nt(dyc, wc).astype(BF)
        dya_in = _dot_nt(dya, wa)
        do_ref[...] = dya_in * (z * sg)
        dmid_ref[0] = (dya_in * ov * (sg * (1.0 + z * (1.0 - sg)))).astype(BF)
        small_ref[...] += _set_rows((8, D), {
            1: jnp.sum(dgcp, axis=0, keepdims=True), 2: jnp.sum(dgap, axis=0, keepdims=True),
            3: g_fg, 7: jnp.broadcast_to(loss, (1, D))})

        @pl.when(i == nsteps - 1)
        def _():
            for p in range(N_DEV):
                for a in range(3):
                    stage[...] = acc[a, p * ROW_SHARD:(p + 1) * ROW_SHARD, :].astype(BF)
                    pltpu.sync_copy(stage, gw_ref.at[p, a])

    return pl.pallas_call(
        body, name="mid", grid=(nsteps,),
        in_specs=[tile, pl.BlockSpec((tm, D), lambda i: (i, 0)), pl.BlockSpec((tm, D), lambda i: (i, 1)),
                  pl.BlockSpec((tm, D), lambda i: (i, 2)), tile, tile, tile,
                  pl.BlockSpec((1, 2 * D), lambda i: (0, 0)), pl.BlockSpec((1, D), lambda i: (0, 0)), VMEM_SPEC],
        out_specs=(tile, pl.BlockSpec((3, tm, D), lambda i: (0, i, 0)), tile, tile,
                   ANY_SPEC, pl.BlockSpec((8, D), lambda i: (0, 0))),
        out_shape=(jax.ShapeDtypeStruct((S, D), F32), jax.ShapeDtypeStruct((3, S, D), BF),
                   jax.ShapeDtypeStruct((S, D), F32), jax.ShapeDtypeStruct((S, D), BF),
                   jax.ShapeDtypeStruct((N_DEV, 3, ROW_SHARD, D), BF), jax.ShapeDtypeStruct((8, D), F32)),
        scratch_shapes=[pltpu.VMEM((3, D, D), F32), pltpu.VMEM((ROW_SHARD, D), BF)],
        compiler_params=_params(1),
    )(yc_in, pa_mid, pa_mid, pa_mid, o, x2, target, b_merge, final_g, w3)


def _conv_bwd(dyc_in, pa, cw8):
    S = pa.shape[0]
    tm, tc = CONV_TM, CONV_TC
    nct = D // tc
    nrt = S // tm
    last_halo = S // HALO - 1

    def seg(s):
        return pl.BlockSpec((tm, tc), lambda j, i, s=s: (i, s * nct + j))

    def halo_before(s):
        return pl.BlockSpec((HALO, tc), lambda j, i, s=s: (jnp.maximum(i * (tm // HALO) - 1, 0), s * nct + j))

    def halo_after(s):
        return pl.BlockSpec((HALO, tc), lambda j, i, s=s: (jnp.minimum((i + 1) * (tm // HALO), last_halo), s * nct + j))

    def body(dy, xc, bg, cg, zc, xch, cgh, dyn, bgn, zcn, cw, dout, gcw):
        i = pl.program_id(1)

        @pl.when(i == 0)
        def _():
            gcw[...] = jnp.zeros_like(gcw)

        xcv, cgv = xc[...].astype(F32), cg[...].astype(F32)
        a = cgv * xcv
        ah = jnp.where(i > 0, cgh[...].astype(F32) * xch[...].astype(F32), 0.0)
        row = lax.broadcasted_iota(jnp.int32, (tm, tc), 0)
        a1 = jnp.where(row == 0, ah[HALO - 1:HALO, :], pltpu.roll(a, 1, 0))
        a2 = jnp.where(row == 0, ah[HALO - 2:HALO - 1, :],
                       jnp.where(row == 1, ah[HALO - 1:HALO, :], pltpu.roll(a, 2, 0)))
        w = cw[...]
        conv = w[0:1, :] * a2 + w[1:2, :] * a1 + w[2:3, :] * a
        z = zc[...].astype(F32)
        sg = _sigmoid(z)
        silu = z * sg
        bgv = bg[...].astype(F32)
        dyv = dy[...].astype(F32)
        dout[3] = (dyv * bgv * conv * (sg * (1.0 + z * (1.0 - sg)))).astype(BF)
        dout[1] = (dyv * silu * conv).astype(BF)
        dc = dyv * silu * bgv
        zn = zcn[...].astype(F32)
        dcn = dyn[...].astype(F32) * (zn * _sigmoid(zn)) * bgn[...].astype(F32)
        dcn = jnp.where(i < nrt - 1, dcn, 0.0)
        dc1 = jnp.where(row == tm - 1, dcn[0:1, :], pltpu.roll(dc, tm - 1, 0))
        dc2 = jnp.where(row == tm - 1, dcn[1:2, :],
                        jnp.where(row == tm - 2, dcn[0:1, :], pltpu.roll(dc, tm - 2, 0)))
        da = w[2:3, :] * dc + w[1:2, :] * dc1 + w[0:1, :] * dc2
        dout[2] = (da * xcv).astype(BF)
        dout[0] = (da * cgv).astype(BF)
        gcw[...] += _set_rows((8, tc), {
            4: jnp.sum(dc * a2, axis=0, keepdims=True), 5: jnp.sum(dc * a1, axis=0, keepdims=True),
            6: jnp.sum(dc * a, axis=0, keepdims=True)})

    return pl.pallas_call(
        body, name="conv_bwd", grid=(nct, nrt),
        in_specs=[pl.BlockSpec((tm, tc), lambda j, i: (i, j)), seg(0), seg(1), seg(2), seg(3),
                  halo_before(0), halo_before(2),
                  pl.BlockSpec((HALO, tc), lambda j, i: (jnp.minimum((i + 1) * (tm // HALO), last_halo), j)),
                  halo_after(1), halo_after(3), pl.BlockSpec((8, tc), lambda j, i: (0, j))],
        out_specs=(pl.BlockSpec((4, tm, tc), lambda j, i: (0, i, j)), pl.BlockSpec((8, tc), lambda j, i: (0, j))),
        out_shape=(jax.ShapeDtypeStruct((4, S, D), BF), jax.ShapeDtypeStruct((8, D), F32)),
        compiler_params=_params(2),
    )(dyc_in, pa, pa, pa, pa, pa, pa, dyc_in, pa, pa, cw8)


def _attn_bwd(q, k, v, slopes, do, o, lse, g_in, g_3):
    S = q.shape[0]
    hpr = HEAD_PAIRS
    n_blocks = S // QB

    def body(sl_ref, q_ref, k_ref, v_ref, do_ref, o_ref, lse_ref, gin_ref, g3_ref, out_ref, rin_ref, r3_ref,
             dq_s, dk_s, dv_s, dd_s, *sems):
        hp = pl.program_id(0)
        exchanges = (_GradExchange(gin_ref, rin_ref, *sems[:3], _shard_cols((0, SEG0_ATTN * D), (SEG0_MID * D, IN_COLS))),
                     _GradExchange(g3_ref, r3_ref, *sems[3:], lambda p: ()))

        @pl.when(hp == 0)
        def _():
            for ex in exchanges:
                ex.start()

        tri_le, dist, low = _fold_masks()
        low_b = low.astype(F32).astype(BF)
        high_b = 1.0 - low_b
        head_sum = _head_sum_matrix()
        dq_s[...] = jnp.zeros(dq_s.shape, F32)
        dk_s[...] = jnp.zeros(dk_s.shape, F32)
        dv_s[...] = jnp.zeros(dv_s.shape, F32)

        def row_dots(i, carry):
            rows = pl.ds(pl.multiple_of(i * QB, QB), QB)
            dd = _dot(_hi_lo(do_ref[rows, :] * o_ref[rows, :]), head_sum)
            dd_s[0, rows, :] = dd[:, :QB]
            dd_s[1, rows, :] = dd[:, QB:]
            return carry

        lax.fori_loop(0, n_blocks, row_dots, 0)

        for d in DILATIONS:
            slope = [sl_ref[2 * hp + a] * float(d) for a in range(2)]
            bias = [slope[a] * dist for a in range(2)]

            def block(b, d=d, slope=slope, bias=bias):
                n, cur, prev = _block_rows(b, d, S)
                has_prev = n > 0
                valid = jnp.logical_or(tri_le, has_prev)
                q2f = q_ref[cur, :] * 0.125
                q2 = q2f.astype(BF)
                qs = jnp.concatenate([q2 * low_b, q2 * high_b], axis=0)
                kp, vp = k_ref[prev, :], v_ref[prev, :]
                kp_b, vp_b = kp.astype(BF), vp.astype(BF)
                kcat = jnp.concatenate([kp_b, k_ref[cur, :].astype(BF)], axis=0)
                vcat = jnp.concatenate([vp_b, v_ref[cur, :].astype(BF)], axis=0)
                do2f = do_ref[cur, :]
                do2 = do2f.astype(BF)
                dos = jnp.concatenate([do2 * low_b, do2 * high_b], axis=0)
                s2 = _dot_nt(qs, kcat)
                dp2 = _dot_nt(dos, vcat)
                diag2 = _dot(jnp.concatenate([_hi_lo(q2.astype(F32) * kp_b.astype(F32)),
                                              _hi_lo(do2.astype(F32) * vp_b.astype(F32))], axis=0), head_sum)
                p_rows, ds_rows, pe_h, dse_h = [], [], [], []
                for a in range(2):
                    hs = slice(a * QB, (a + 1) * QB)
                    sp, sc = s2[hs, :QB], s2[hs, QB:]
                    dpp, dpc = dp2[hs, :QB], dp2[hs, QB:]
                    lse_a, dd_a = lse_ref.at[a][cur, :], dd_s.at[a][cur, :]
                    comb = jnp.where(tri_le, sc, sp) - bias[a]
                    e = diag2[:QB, hs] - slope[a] * float(QB)
                    p = jnp.where(valid, jnp.exp(comb - lse_a), 0.0)
                    pe = jnp.where(has_prev, jnp.exp(e - lse_a), 0.0)
                    ds = p * (jnp.where(tri_le, dpc, dpp) - dd_a)
                    dse_h.append(pe * (diag2[QB:, hs] - dd_a))
                    pe_h.append(pe)
                    p_rows.append(jnp.concatenate([jnp.where(tri_le, 0.0, p).astype(BF),
                                                   jnp.where(tri_le, p, 0.0).astype(BF)], axis=1))
                    ds_rows.append(jnp.concatenate([jnp.where(tri_le, 0.0, ds).astype(BF),
                                                    jnp.where(tri_le, ds, 0.0).astype(BF)], axis=1))
                pst = jnp.concatenate(p_rows, axis=0)
                dst = jnp.concatenate(ds_rows, axis=0)
                pe2 = jnp.where(low, pe_h[0], pe_h[1])
                dse2 = jnp.where(low, dse_h[0], dse_h[1])
                dq = _dot(dst, kcat)
                dq_s[cur, :] += (jnp.where(low, dq[:QB], dq[QB:]) + dse2 * kp) * 0.125
                dk = _dot_tn(dst, qs)
                dv = _dot_tn(pst, dos)
                dk_s[prev, :] += dk[:QB] + dse2 * q2f
                dk_s[cur, :] += dk[QB:]
                dv_s[prev, :] += dv[:QB] + pe2 * do2f
                dv_s[cur, :] += dv[QB:]

            def several(it, carry, block=block):
                for u in range(ATT_UNROLL):
                    block(it * ATT_UNROLL + u)
                return carry

            lax.fori_loop(0, n_blocks // ATT_UNROLL, several, 0)

        def finish(i, carry):
            rows = pl.ds(pl.multiple_of(i * QB, QB), QB)
            out_ref[0, rows, :] = dq_s[rows, :].astype(BF)
            out_ref[1, rows, :] = dk_s[rows, :].astype(BF)
            out_ref[2, rows, :] = dv_s[rows, :].astype(BF)
            return carry

        lax.fori_loop(0, n_blocks, finish, 0)

        @pl.when(hp == hpr - 1)
        def _():
            for ex in exchanges:
                ex.finish()

    col = pl.BlockSpec((S, 128), lambda h: (0, h))
    return pl.pallas_call(
        body, name="attn_bwd", grid=(hpr,),
        in_specs=[SMEM_SPEC, col, col, col, col, col, pl.BlockSpec((2, S, 128), lambda h: (0, 0, h)),
                  ANY_SPEC, ANY_SPEC],
        out_specs=(pl.BlockSpec((3, S, 128), lambda h: (0, 0, h)), ANY_SPEC, ANY_SPEC),
        out_shape=(jax.ShapeDtypeStruct((3, S, D), BF), jax.ShapeDtypeStruct(g_in.shape, BF),
                   jax.ShapeDtypeStruct(g_3.shape, BF)),
        scratch_shapes=([pltpu.VMEM((S, 128), F32)] * 3 + [pltpu.VMEM((2, S, 128), F32)]
                        + GRAD_EXCHANGE_SEMS + GRAD_EXCHANGE_SEMS),
        compiler_params=_params(1),
    )(slopes, q, k, v, do, o, lse, g_in, g_3)


WG_TN = 256
SEG0_CONV, SEG0_ATTN, SEG0_MID = 0, 4, 7


def _wgrad_in(ut, d_group, seg0, g_in, name):
    S = ut.shape[1]
    tn = WG_TN
    per_seg = D // tn
    per_shard = W_IN_SHARD // tn
    n_tiles = d_group.shape[0] * per_seg
    tile0 = seg0 * per_seg

    def body(ut_ref, d_ref, *rest):
        rest[-1][0] = _dot(ut_ref[...], d_ref[0]).astype(BF)

    operands, in_specs, aliases = [ut, d_group], [VMEM_SPEC, pl.BlockSpec((1, S, tn), lambda t: (t // per_seg, 0, t % per_seg))], {}
    if g_in is not None:
        operands.append(g_in)
        in_specs.append(ANY_SPEC)
        aliases = {2: 0}
    return pl.pallas_call(
        body, name=name, grid=(n_tiles,), in_specs=in_specs,
        out_specs=pl.BlockSpec((1, D, tn), lambda t: ((tile0 + t) // per_shard, 0, (tile0 + t) % per_shard)),
        out_shape=jax.ShapeDtypeStruct((N_DEV, D, W_IN_SHARD), BF),
        input_output_aliases=aliases,
        compiler_params=_params(1),
    )(*operands)


def _dgrad_norm_bwd(d_conv, d_attn, d_mid, w_all, x2, dh, norm_g, g_in, r_in):
    S = x2.shape[0]
    tm = ROW_TILE
    nsteps = S // tm
    tile = pl.BlockSpec((tm, D), lambda i: (i, 0))
    pieces = _proj_pieces()

    def body(a_ref, b_ref, c_ref, w_ref, x_ref, dh_ref, g_ref, gin_ref, rin_in_ref, gx_ref, small_ref, rin_ref, *sems):
        i = pl.program_id(0)
        exchange = _GradExchange(gin_ref, rin_ref, *sems, _shard_cols((SEG0_ATTN * D, SEG0_MID * D)))

        @pl.when(i == 0)
        def _():
            small_ref[...] = jnp.zeros_like(small_ref)
            exchange.start()

        groups = (a_ref, b_ref, c_ref)
        du = jnp.zeros((tm, D), F32)
        for s, sc, p, pc, width in pieces:
            g = 0 if s < 4 else (1 if s < 7 else 2)
            local = s - (0, 4, 7)[g]
            du = du + _dot_nt(groups[g][local, :, sc:sc + width], w_ref[p, :, pc:pc + width])
        xv = x_ref[...]
        r = lax.rsqrt(jnp.mean(xv * xv, axis=-1, keepdims=True) + EPS)
        n = xv * r
        dn = du * g_ref[...]
        gx_ref[...] = dh_ref[...] + r * (dn - n * jnp.mean(dn * n, axis=-1, keepdims=True))
        small_ref[...] += _set_rows((8, D), {0: jnp.sum(du * n, axis=0, keepdims=True)})

        @pl.when(i == nsteps - 1)
        def _():
            exchange.finish()

    return pl.pallas_call(
        body, name="dgrad_norm_bwd", grid=(nsteps,),
        in_specs=[pl.BlockSpec((4, tm, D), lambda i: (0, i, 0)), pl.BlockSpec((3, tm, D), lambda i: (0, i, 0)),
                  pl.BlockSpec((3, tm, D), lambda i: (0, i, 0)), VMEM_SPEC, tile, tile,
                  pl.BlockSpec((1, D), lambda i: (0, 0)), ANY_SPEC, ANY_SPEC],
        out_specs=(tile, pl.BlockSpec((8, D), lambda i: (0, 0)), ANY_SPEC),
        out_shape=(jax.ShapeDtypeStruct((S, D), F32), jax.ShapeDtypeStruct((8, D), F32),
                   jax.ShapeDtypeStruct(r_in.shape, BF)),
        scratch_shapes=GRAD_EXCHANGE_SEMS,
        input_output_aliases={8: 2},
        compiler_params=_params(1),
    )(d_conv, d_attn, d_mid, w_all, x2, dh, norm_g, g_in, r_in)


def _adamw_math(w, g, m, v):
    m = ADAM_B1 * m + (1.0 - ADAM_B1) * g
    v = ADAM_B2 * v + (1.0 - ADAM_B2) * (g * g)
    m_hat = m / (1.0 - ADAM_B1 ** ADAM_STEP)
    v_hat = v / (1.0 - ADAM_B2 ** ADAM_STEP)
    delta = -ADAM_LR * (m_hat / (jnp.sqrt(v_hat) + ADAM_EPS) + ADAM_WD * w)
    return delta, m, v


def _sum_adamw(parts, w, m, v, tm, name):
    R, C = w.shape
    tile = pl.BlockSpec((tm, C), lambda i: (i, 0))

    def body(p_ref, w_ref, m_ref, v_ref, g_out, d_out, m_out, v_out):
        g = p_ref[0].astype(F32)
        for s in range(1, N_DEV):
            g = g + p_ref[s].astype(F32)
        g_out[...] = g
        d_out[...], m_out[...], v_out[...] = _adamw_math(w_ref[...], g, m_ref[...], v_ref[...])

    shape = jax.ShapeDtypeStruct((R, C), F32)
    return pl.pallas_call(
        body, name=name, grid=(R // tm,),
        in_specs=[pl.BlockSpec((N_DEV, tm, C), lambda i: (0, i, 0)), tile, tile, tile],
        out_specs=(tile, tile, tile, tile), out_shape=(shape, shape, shape, shape),
        compiler_params=_params(1),
    )(parts, w, m, v)


def _adamw(g, w, m, v, name):
    def body(g_ref, w_ref, m_ref, v_ref, d_out, m_out, v_out):
        d_out[...], m_out[...], v_out[...] = _adamw_math(w_ref[...], g_ref[...], m_ref[...], v_ref[...])

    shape = jax.ShapeDtypeStruct(w.shape, F32)
    return pl.pallas_call(
        body, name=name, in_specs=[VMEM_SPEC] * 4, out_specs=(VMEM_SPEC,) * 3, out_shape=(shape, shape, shape),
    )(g, w, m, v)


def _alibi_slopes():
    return jnp.exp2(-8.0 * jnp.arange(1, N_HEADS + 1, dtype=F32) / N_HEADS)


def _local_step(x2, target, norm_g, b_merge, final_g, w_all, w3_all, cw_all):
    slopes = _alibi_slopes()
    u, ut = _norm(x2, norm_g)
    o, lse, q, k, v, w_all, w3_all, cw_all = _attn_fwd(u, slopes, w_all, w3_all, cw_all)
    w3 = jnp.transpose(w3_all, (1, 0, 2, 3)).reshape(3, D, D)
    cw8 = jnp.transpose(cw_all, (1, 0, 2)).reshape(8, D)
    pa = _proj_cols(u, w_all, SEG0_CONV, 4, BF, "proj_conv")
    yc_in = _conv_fwd(pa, cw8)
    pa_mid = _proj_cols(u, w_all, SEG0_MID, 3, BF, "proj_mid")
    dh, d_mid, do, dyc_in, g_3, small_mid = _mid(yc_in, pa_mid, o, x2, target, b_merge, final_g, w3)
    g_in = _wgrad_in(ut, d_mid, SEG0_MID, None, "wgrad_in_mid")
    d_conv, small_conv = _conv_bwd(dyc_in, pa, cw8)
    g_in = _wgrad_in(ut, d_conv, SEG0_CONV, g_in, "wgrad_in_conv")
    d_attn, r_in, r_3 = _attn_bwd(q, k, v, slopes, do, o, lse, g_in, g_3)
    g_in = _wgrad_in(ut, d_attn, SEG0_ATTN, g_in, "wgrad_in_attn")
    grad_x, small_norm, r_in = _dgrad_norm_bwd(d_conv, d_attn, d_mid, w_all, x2, dh, norm_g, g_in, r_in)
    return grad_x, r_in, r_3, small_mid, small_conv, small_norm


def kernel(x, norm_g, w_in, b_merge, conv_w, w_out_conv, w_out_attn, w_o, final_g, loss_target, m_norm_g, m_w_in, m_b_merge, m_conv_w, m_w_out_conv, m_w_out_attn, m_w_o, m_final_g, v_norm_g, v_w_in, v_b_merge, v_conv_w, v_w_out_conv, v_w_out_attn, v_w_o, v_final_g):
    me = 4 * lax.axis_index("x") + 2 * lax.axis_index("y") + lax.axis_index("c")
    stack3 = lambda a, b, c: jnp.concatenate([a, b, c], axis=0)
    pad8 = lambda a: jnp.pad(a, ((0, 8 - a.shape[0]), (0, 0)))

    w3_shard = stack3(w_out_conv, w_out_attn, w_o)
    w_all, w3_all, cw_all = _gather_first_weights(w_in[0], w3_shard, pad8(conv_w[0]))

    final_g2 = final_g.reshape(1, D)
    grad_x, r_in, r_3, small_mid, small_conv, small_norm = _local_step(
        x[0], loss_target[0], norm_g, b_merge, final_g2, w_all, w3_all, cw_all)

    small = _allreduce_small(small_mid, small_conv, small_norm)

    g_w_in, d_w_in, nm_w_in, nv_w_in = _sum_adamw(r_in, w_in[0], m_w_in[0], v_w_in[0], 128, "adamw_w_in")
    g_w3, d_w3, nm_w3, nv_w3 = _sum_adamw(
        r_3.reshape(N_DEV, 3 * ROW_SHARD, D), w3_shard.reshape(3 * ROW_SHARD, D),
        stack3(m_w_out_conv, m_w_out_attn, m_w_o).reshape(3 * ROW_SHARD, D),
        stack3(v_w_out_conv, v_w_out_attn, v_w_o).reshape(3 * ROW_SHARD, D), ROW_SHARD, "adamw_w3")

    def pack(ng, bm, fg):
        return pad8(jnp.concatenate([ng, bm.reshape(2, D), fg.reshape(1, D)], axis=0))

    d_s, nm_s, nv_s = _adamw(small, pack(norm_g, b_merge, final_g), pack(m_norm_g, m_b_merge, m_final_g),
                             pack(v_norm_g, v_b_merge, v_final_g), "adamw_small")
    g_cw = lax.dynamic_slice(small, (4, me * ROW_SHARD), (3, ROW_SHARD))
    d_cw, nm_cw, nv_cw = _adamw(g_cw, conv_w[0], m_conv_w[0], v_conv_w[0], "adamw_conv_w")

    loss = small[7, 0]
    split3 = lambda t: tuple(t[a * ROW_SHARD:(a + 1) * ROW_SHARD][None] for a in range(3))
    unpack = lambda t: (t[0:1], t[1:3].reshape(1, 2 * D), t[3])

    def leaves(in_, small_, cw_, w3_):
        ng, bm, fg = unpack(small_)
        wc, wa, wo = split3(w3_)
        return (ng, in_[None], bm, cw_[None], wc, wa, wo, fg)

    return (loss, grad_x[None],
            *leaves(g_w_in, small, g_cw, g_w3),
            *leaves(d_w_in, d_s, d_cw, d_w3),
            *leaves(nm_w_in, nm_s, nm_cw, nm_w3),
            *leaves(nv_w_in, nv_s, nv_cw, nv_w3))
```

```python
import functools

import jax
import jax.numpy as jnp
from jax import lax
from jax.experimental import pallas as pl
from jax.experimental.pallas import tpu as pltpu

D = 1024
N_HEADS = 16
HEAD_DIM = 64
N_SEG = 10
IN_COLS = N_SEG * D
N_DEV = 8
W_IN_SHARD = IN_COLS // N_DEV
ROW_SHARD = D // N_DEV
QB = 128
DILATIONS = (1, 4, 16)
EPS = 1e-6
NEG = -1e30
BF = jnp.bfloat16
F32 = jnp.float32
MESH = pl.DeviceIdType.MESH

ADAM_LR = 0.001
ADAM_B1 = 0.9
ADAM_B2 = 0.999
ADAM_EPS = 1e-08
ADAM_WD = 0.01
ADAM_STEP = 10

V7X_VMEM_BYTES = 64 * 1024 * 1024
VMEM_LIMIT = V7X_VMEM_BYTES - 8 * 1024 * 1024
ROW_TILE = 256

VMEM_SPEC = pl.BlockSpec(memory_space=pltpu.VMEM)
ANY_SPEC = pl.BlockSpec(memory_space=pl.ANY)
SMEM_SPEC = pl.BlockSpec(memory_space=pltpu.SMEM)


def _params(n_grid_axes, vmem=VMEM_LIMIT):
    return pltpu.CompilerParams(dimension_semantics=("arbitrary",) * n_grid_axes, vmem_limit_bytes=vmem)


def _dot(a, b):
    return jnp.dot(a, b, preferred_element_type=F32)


def _dot_nt(a, b):
    return lax.dot_general(a, b, (((1,), (1,)), ((), ())), preferred_element_type=F32)


def _dot_tn(a, b):
    return lax.dot_general(a, b, (((0,), (0,)), ((), ())), preferred_element_type=F32)


def _sigmoid(z):
    return 1.0 / (1.0 + jnp.exp(-z))


def _my_place():
    x, y, c = lax.axis_index("x"), lax.axis_index("y"), lax.axis_index("c")
    return x, y, c, 4 * x + 2 * y + c


def _peers(x, y, c):
    out = []
    for k in range(1, N_DEV):
        px = 1 - x if k & 4 else x
        py = 1 - y if k & 2 else y
        pc = 1 - c if k & 1 else c
        out.append(((px, py, pc), 4 * px + 2 * py + pc))
    return out


def _device(p):
    return (p >> 2, (p >> 1) & 1, p & 1)


def _shard_cols(*ranges):
    def cols(p):
        found = None
        for lo, hi in ranges:
            a, b = max(lo, p * W_IN_SHARD), min(hi, (p + 1) * W_IN_SHARD)
            if a < b:
                assert found is None
                found = (a - p * W_IN_SHARD, b - p * W_IN_SHARD)
        return found

    return cols


def _whole(p):
    return ()


def _block(ref, idx, cols):
    return ref.at[idx] if cols == () else ref.at[idx, :, cols[0]:cols[1]]


class _WeightGather:
    def __init__(self, src, dst, send_sems, recv_sems, cols):
        self.src, self.dst, self.cols = src, dst, cols
        self.send_sems, self.recv_sems = send_sems, recv_sems
        self.me = _my_place()[3]

    def _copy(self, p, target):
        cols = self.cols(p)
        return pltpu.make_async_remote_copy(
            src_ref=self.src(p, cols), dst_ref=_block(self.dst, p, cols), send_sem=self.send_sems.at[target],
            recv_sem=self.recv_sems.at[p], device_id=_device(target), device_id_type=MESH)

    def _each(self, send, receive):
        for p in range(N_DEV):
            if self.cols(p) is None:
                continue

            def sender(p=p):
                for k in range(1, N_DEV):
                    send(self._copy(p, (p + k) % N_DEV))

            pl.when(self.me == p)(sender)
            pl.when(self.me != p)(lambda p=p: receive(self._copy(p, p)))

    def start(self):
        self._each(lambda cp: cp.start(), lambda cp: None)

    def finish(self):
        self._each(lambda cp: cp.wait_send(), lambda cp: cp.wait_recv())


WEIGHT_GATHER_SEMS = [pltpu.SemaphoreType.DMA((N_DEV,)), pltpu.SemaphoreType.DMA((N_DEV,))]
REST_COLS = _shard_cols((0, 4 * D), (7 * D, IN_COLS))
HEAD_PAIRS = D // 128


def _qkv_piece(h, seg):
    col = (4 + seg) * D + 128 * h
    return col // W_IN_SHARD, col % W_IN_SHARD


class _PieceGather:
    def __init__(self, src, dst, send_sems, recv_sems):
        self.src, self.dst, self.send_sems, self.recv_sems = src, dst, send_sems, recv_sems
        self.me = _my_place()[3]

    def _copy(self, i, target):
        p, lo = _qkv_piece(i // 3, i % 3)
        return pltpu.make_async_remote_copy(
            src_ref=self.src(p, lo, lo + 128), dst_ref=self.dst.at[p, :, lo:lo + 128], send_sem=self.send_sems.at[i, target],
            recv_sem=self.recv_sems.at[i], device_id=_device(target), device_id_type=MESH)

    def _owner(self, i, act):
        p = _qkv_piece(i // 3, i % 3)[0]

        def sender():
            for k in range(N_DEV - 1):
                act(self._copy(i, (p + 1 + (k + i) % (N_DEV - 1)) % N_DEV))

        pl.when(self.me == p)(sender)

    def start(self, pieces):
        for i in pieces:
            self._owner(i, lambda cp: cp.start())

    def wait_send(self, pieces):
        for i in pieces:
            self._owner(i, lambda cp: cp.wait_send())

    def wait_recv(self, pieces):
        for i in pieces:
            p = _qkv_piece(i // 3, i % 3)[0]
            pl.when(self.me != p)(lambda i=i, p=p: self._copy(i, p).wait_recv())


def _piece_sems(n):
    return [pltpu.SemaphoreType.DMA((n, N_DEV)), pltpu.SemaphoreType.DMA((n,))]


def _gather_first_weights(w_in, w3, cw):
    def body(w_in_ref, w3_ref, cw_ref, o_in, o_3, o_cw, in_bf, w3_bf, local_sems, *sems):
        me = _my_place()[3]

        def cast_rows(i, carry):
            r = pl.multiple_of(i * 128, 128)
            in_bf[pl.ds(r, 128), :] = w_in_ref[pl.ds(r, 128), :].astype(BF)
            return carry

        lax.fori_loop(0, D // 128, cast_rows, 0)
        for a in range(3):
            w3_bf[a] = w3_ref[a].astype(BF)
        gather = _PieceGather(lambda p, lo, hi: in_bf.at[:, lo:hi], o_in, *sems)
        gather.start(range(3))
        local = [pltpu.make_async_copy(src, dst.at[me], local_sems.at[a])
                 for a, (src, dst) in enumerate(((in_bf, o_in), (w3_bf, o_3), (cw_ref, o_cw)))]
        for cp in local:
            cp.start()
        gather.wait_recv(range(3))
        gather.wait_send(range(3))
        for cp in local:
            cp.wait()

    return pl.pallas_call(
        body, name="gather_first_weights",
        out_shape=(jax.ShapeDtypeStruct((N_DEV, D, W_IN_SHARD), BF),
                   jax.ShapeDtypeStruct((N_DEV, 3, ROW_SHARD, D), BF),
                   jax.ShapeDtypeStruct((N_DEV, 8, 128), F32)),
        in_specs=[VMEM_SPEC, VMEM_SPEC, VMEM_SPEC],
        out_specs=(ANY_SPEC, ANY_SPEC, ANY_SPEC),
        scratch_shapes=[pltpu.VMEM((D, W_IN_SHARD), BF), pltpu.VMEM((3, ROW_SHARD, D), BF),
                        pltpu.SemaphoreType.DMA((3,))] + _piece_sems(3),
        compiler_params=pltpu.CompilerParams(vmem_limit_bytes=VMEM_LIMIT),
    )(w_in, w3, cw)


class _GradExchange:
    def __init__(self, src, dst, send_sems, recv_sems, local_sem, cols):
        self.src, self.dst, self.cols = src, dst, cols
        self.send_sems, self.recv_sems, self.local_sem = send_sems, recv_sems, local_sem
        self.me = _my_place()[3]

    def _remote(self, p, source):
        return pltpu.make_async_remote_copy(
            src_ref=_block(self.src, p, self.cols(p)), dst_ref=_block(self.dst, source, self.cols(p)),
            send_sem=self.send_sems.at[p], recv_sem=self.recv_sems.at[source],
            device_id=_device(p), device_id_type=MESH)

    def _local(self, p):
        return pltpu.make_async_copy(_block(self.src, p, self.cols(p)), _block(self.dst, p, self.cols(p)),
                                     self.local_sem)

    def _as_each_device(self, send, local, receive):
        for m in range(N_DEV):
            def branch(m=m):
                for k in range(1, N_DEV):
                    p = (m + k) % N_DEV
                    if self.cols(p) is not None:
                        send(self._remote(p, m))
                if self.cols(m) is not None:
                    local(self._local(m))
                    for k in range(1, N_DEV):
                        receive(self._remote(m, (m + k) % N_DEV))

            pl.when(self.me == m)(branch)

    def start(self):
        self._as_each_device(lambda cp: cp.start(), lambda cp: cp.start(), lambda cp: None)

    def finish(self):
        self._as_each_device(lambda cp: cp.wait_send(), lambda cp: cp.wait(), lambda cp: cp.wait_recv())


GRAD_EXCHANGE_SEMS = [pltpu.SemaphoreType.DMA((N_DEV,)), pltpu.SemaphoreType.DMA((N_DEV,)), pltpu.SemaphoreType.DMA]


def _allreduce_small(p_mid, p_conv, p_norm):
    def body(a_ref, b_ref, c_ref, out_ref, mine, gathered, send_sems, recv_sems):
        x, y, c, me = _my_place()
        mine[...] = a_ref[...] + b_ref[...] + c_ref[...]
        gathered[me] = mine[...]
        remote = []
        for k, (peer, _) in enumerate(_peers(x, y, c)):
            cp = pltpu.make_async_remote_copy(
                src_ref=mine, dst_ref=gathered.at[me], send_sem=send_sems.at[k], recv_sem=recv_sems.at[k],
                device_id=peer, device_id_type=MESH)
            cp.start()
            remote.append(cp)
        for cp in remote:
            cp.wait()
        total = gathered[0]
        for s in range(1, N_DEV):
            total = total + gathered[s]
        out_ref[...] = total

    return pl.pallas_call(
        body, name="allreduce_small",
        out_shape=jax.ShapeDtypeStruct((8, D), F32),
        in_specs=[VMEM_SPEC, VMEM_SPEC, VMEM_SPEC], out_specs=VMEM_SPEC,
        scratch_shapes=[pltpu.VMEM((8, D), F32), pltpu.VMEM((N_DEV, 8, D), F32),
                        pltpu.SemaphoreType.DMA((N_DEV - 1,)), pltpu.SemaphoreType.DMA((N_DEV - 1,))],
    )(p_mid, p_conv, p_norm)


def _proj_pieces():
    cuts = sorted(set(range(0, IN_COLS + 1, D)) | set(range(0, IN_COLS + 1, W_IN_SHARD)))
    return [(lo // D, lo % D, lo // W_IN_SHARD, lo % W_IN_SHARD, hi - lo) for lo, hi in zip(cuts[:-1], cuts[1:])]


def _norm(x2, norm_g):
    S = x2.shape[0]
    tm = ROW_TILE

    def body(x_ref, g_ref, u_ref, ut_ref):
        xv = x_ref[...]
        r = lax.rsqrt(jnp.mean(xv * xv, axis=-1, keepdims=True) + EPS)
        u = xv * r * g_ref[...]
        u_ref[...] = u.astype(BF)
        ut_ref[...] = u.T.astype(BF)

    return pl.pallas_call(
        body, name="norm", grid=(S // tm,),
        in_specs=[pl.BlockSpec((tm, D), lambda i: (i, 0)), pl.BlockSpec((1, D), lambda i: (0, 0))],
        out_specs=(pl.BlockSpec((tm, D), lambda i: (i, 0)), pl.BlockSpec((D, tm), lambda i: (0, i))),
        out_shape=(jax.ShapeDtypeStruct((S, D), BF), jax.ShapeDtypeStruct((D, S), BF)),
        compiler_params=_params(1),
    )(x2, norm_g)


PROJ_TN = 256


def _proj_cols(u, w_all, seg0, n_seg, dtype, name):
    S = u.shape[0]
    tn = PROJ_TN
    per_shard = W_IN_SHARD // tn
    tile0 = seg0 * D // tn

    def body(u_ref, w_ref, out_ref):
        out_ref[...] = _dot(u_ref[...], w_ref[0]).astype(dtype)

    return pl.pallas_call(
        body, name=name, grid=(n_seg * D // tn,),
        in_specs=[VMEM_SPEC, pl.BlockSpec((1, D, tn), lambda t: ((tile0 + t) // per_shard, 0, (tile0 + t) % per_shard))],
        out_specs=pl.BlockSpec((S, tn), lambda t: (0, t)),
        out_shape=jax.ShapeDtypeStruct((S, n_seg * D), dtype),
        compiler_params=_params(1),
    )(u, w_all)


CONV_TM, CONV_TC = 256, 512
HALO = 16


def _conv_fwd(pa, cw8):
    S = pa.shape[0]
    tm, tc = CONV_TM, CONV_TC
    nct = D // tc

    def seg(s):
        return pl.BlockSpec((tm, tc), lambda i, j, s=s: (i, s * nct + j))

    def halo_before(s):
        return pl.BlockSpec((HALO, tc), lambda i, j, s=s: (jnp.maximum(i * (tm // HALO) - 1, 0), s * nct + j))

    def body(xc, bg, cg, zc, xch, cgh, cw, out):
        i = pl.program_id(0)
        a = cg[...].astype(F32) * xc[...].astype(F32)
        ah = cgh[...].astype(F32) * xch[...].astype(F32)
        ah = jnp.where(i > 0, ah, 0.0)
        row = lax.broadcasted_iota(jnp.int32, (tm, tc), 0)
        a1 = jnp.where(row == 0, ah[HALO - 1:HALO, :], pltpu.roll(a, 1, 0))
        a2 = jnp.where(row == 0, ah[HALO - 2:HALO - 1, :],
                       jnp.where(row == 1, ah[HALO - 1:HALO, :], pltpu.roll(a, 2, 0)))
        w = cw[...]
        conv = w[0:1, :] * a2 + w[1:2, :] * a1 + w[2:3, :] * a
        z = zc[...].astype(F32)
        out[...] = (z * _sigmoid(z) * bg[...].astype(F32) * conv).astype(BF)

    return pl.pallas_call(
        body, name="conv_fwd", grid=(S // tm, nct),
        in_specs=[seg(0), seg(1), seg(2), seg(3), halo_before(0), halo_before(2),
                  pl.BlockSpec((8, tc), lambda i, j: (0, j))],
        out_specs=pl.BlockSpec((tm, tc), lambda i, j: (i, j)),
        out_shape=jax.ShapeDtypeStruct((S, D), BF),
        compiler_params=_params(2),
    )(pa, pa, pa, pa, pa, pa, cw8)


ATT_UNROLL = 2


def _fold_masks():
    row = lax.broadcasted_iota(jnp.int32, (QB, QB), 0)
    lane = lax.broadcasted_iota(jnp.int32, (QB, QB), 1)
    tri_le = lane <= row
    dist = jnp.where(tri_le, row - lane, row - lane + QB).astype(F32)
    return tri_le, dist, lane < HEAD_DIM


def _rows(start, size, d):
    return pl.ds(pl.multiple_of(start, size), size) if d == 1 else pl.ds(start, size, stride=d)


def _block_starts(b, d, S):
    nb = S // (QB * d)
    r, n = b // nb, b % nb
    return n, r + n * (QB * d), r + jnp.maximum(n - 1, 0) * (QB * d)


def _block_rows(b, d, S):
    n, cur0, prev0 = _block_starts(b, d, S)
    return n, _rows(cur0, QB, d), _rows(prev0, QB, d)


def _head_sum_matrix():
    r = lax.broadcasted_iota(jnp.int32, (2 * QB, 2 * QB), 0)
    c = lax.broadcasted_iota(jnp.int32, (2 * QB, 2 * QB), 1)
    return (((r % QB) // HEAD_DIM) == (c // QB)).astype(F32).astype(BF)


def _hi_lo(t):
    hi = t.astype(BF)
    return jnp.concatenate([hi, (t - hi.astype(F32)).astype(BF)], axis=1)


PROJ_ROWS = 512


def _attn_fwd(u, slopes, w_all, w3_all, cw_all):
    S = u.shape[0]
    hpr = HEAD_PAIRS
    n_blocks = S // QB
    later = range(3, 3 * hpr)

    def body(sl_ref, u_ref, w_in_ref, w3_in_ref, cw_in_ref, o_ref, lse_ref, q_ref, k_ref, v_ref, w_ref, w3_ref,
             cw_ref, acc, m_s, l_s, w_tile, tile_sems, *sems):
        hp = pl.program_id(0)
        me = _my_place()[3]
        pieces = _PieceGather(lambda p, lo, hi: w_ref.at[p, :, lo:hi], w_ref, *sems[0:2])
        gathers = (_WeightGather(lambda p, cols: _block(w_ref, me, cols), w_ref, *sems[2:4], REST_COLS),
                   _WeightGather(lambda p, cols: w3_ref.at[me], w3_ref, *sems[4:6], _whole),
                   _WeightGather(lambda p, cols: cw_ref.at[me], cw_ref, *sems[6:8], _whole))

        @pl.when(hp == 0)
        def _():
            pieces.start(later)
            for g in gathers:
                g.start()

        for h in range(hpr):
            @pl.when(hp == h)
            def _(h=h):
                if h > 0:
                    pieces.wait_recv(range(3 * h, 3 * h + 3))
                fetch = []
                for seg in range(3):
                    p, lo = _qkv_piece(h, seg)
                    fetch.append(pltpu.make_async_copy(w_ref.at[p, :, lo:lo + 128], w_tile.at[:, seg * 128:(seg + 1) * 128],
                                                       tile_sems.at[seg]))
                    fetch[-1].start()
                for cp in fetch:
                    cp.wait()

        def project(i, carry):
            rows = pl.ds(pl.multiple_of(i * PROJ_ROWS, PROJ_ROWS), PROJ_ROWS)
            qkv = _dot(u_ref[rows, :], w_tile[...])
            for seg, ref in enumerate((q_ref, k_ref, v_ref)):
                ref[rows, :] = qkv[:, seg * 128:(seg + 1) * 128]
            return carry

        lax.fori_loop(0, S // PROJ_ROWS, project, 0)

        tri_le, dist, low = _fold_masks()
        low_b = low.astype(F32).astype(BF)
        high_b = 1.0 - low_b
        head_sum = _head_sum_matrix()
        ones_b = jnp.ones((2 * QB, QB), BF)
        m_s[...] = jnp.full(m_s.shape, NEG, F32)
        l_s[...] = jnp.zeros(l_s.shape, F32)
        acc[...] = jnp.zeros(acc.shape, F32)

        for d in DILATIONS:
            slope = [sl_ref[2 * hp + a] * float(d) for a in range(2)]
            bias = [slope[a] * dist for a in range(2)]

            def block(b, d=d, slope=slope, bias=bias):
                n, cur, prev = _block_rows(b, d, S)
                has_prev = n > 0
                valid = jnp.logical_or(tri_le, has_prev)
                q2 = (q_ref[cur, :] * 0.125).astype(BF)
                qs = jnp.concatenate([q2 * low_b, q2 * high_b], axis=0)
                vp = v_ref[prev, :]
                kp_b = k_ref[prev, :].astype(BF)
                kcat = jnp.concatenate([kp_b, k_ref[cur, :].astype(BF)], axis=0)
                vcat = jnp.concatenate([vp, v_ref[cur, :]], axis=0).astype(BF)
                s2 = _dot_nt(qs, kcat)
                e2 = _dot(_hi_lo(q2.astype(F32) * kp_b.astype(F32)), head_sum)
                p_rows, alpha_h, pe_h = [], [], []
                for a in range(2):
                    sp, sc = s2[a * QB:(a + 1) * QB, :QB], s2[a * QB:(a + 1) * QB, QB:]
                    comb = jnp.where(valid, jnp.where(tri_le, sc, sp) - bias[a], NEG)
                    e = jnp.where(has_prev, e2[:, a * QB:(a + 1) * QB] - slope[a] * float(QB), NEG)
                    m_old = m_s.at[a][cur, :]
                    m_new = jnp.maximum(jnp.maximum(m_old, jnp.max(comb, axis=-1, keepdims=True)), e)
                    m_s.at[a][cur, :] = m_new
                    p = jnp.exp(comb - m_new)
                    pe_h.append(jnp.exp(e - m_new))
                    alpha_h.append(jnp.exp(m_old - m_new))
                    p_rows.append(jnp.concatenate([jnp.where(tri_le, 0.0, p).astype(BF),
                                                   jnp.where(tri_le, p, 0.0).astype(BF)], axis=1))
                pv = _dot(jnp.concatenate(p_rows, axis=0), jnp.concatenate([vcat, ones_b], axis=1))
                for a in range(2):
                    l_a = l_s.at[a]
                    l_a[cur, :] = alpha_h[a] * l_a[cur, :] + pv[a * QB:(a + 1) * QB, QB:] + pe_h[a]
                acc[cur, :] = (jnp.where(low, alpha_h[0], alpha_h[1]) * acc[cur, :]
                               + jnp.where(low, pv[:QB, :QB], pv[QB:, :QB]) + jnp.where(low, pe_h[0], pe_h[1]) * vp)

            def several(it, carry, block=block):
                for u in range(ATT_UNROLL):
                    block(it * ATT_UNROLL + u)
                return carry

            lax.fori_loop(0, n_blocks // ATT_UNROLL, several, 0)

        def finish(i, carry):
            rows = pl.ds(pl.multiple_of(i * QB, QB), QB)
            l0, l1 = l_s[0, rows, :], l_s[1, rows, :]
            o = acc[rows, :] / jnp.where(low, l0, l1)
            o_ref[rows, :] = o
            lse_ref[0, rows, :] = m_s[0, rows, :] + jnp.log(l0)
            lse_ref[1, rows, :] = m_s[1, rows, :] + jnp.log(l1)
            return carry

        lax.fori_loop(0, n_blocks, finish, 0)

        @pl.when(hp == hpr - 1)
        def _():
            pieces.wait_send(later)
            for g in gathers:
                g.finish()

    col = pl.BlockSpec((S, 128), lambda h: (0, h))
    act = jax.ShapeDtypeStruct((S, D), F32)
    gathered = (w_all, w3_all, cw_all)
    return pl.pallas_call(
        body, name="attn_fwd", grid=(hpr,),
        in_specs=[SMEM_SPEC, VMEM_SPEC, ANY_SPEC, ANY_SPEC, ANY_SPEC],
        out_specs=(col, pl.BlockSpec((2, S, 128), lambda h: (0, 0, h)), col, col, col, ANY_SPEC, ANY_SPEC, ANY_SPEC),
        out_shape=(act, jax.ShapeDtypeStruct((2, S, D), F32), act, act, act,
                   *[jax.ShapeDtypeStruct(t.shape, t.dtype) for t in gathered]),
        scratch_shapes=([pltpu.VMEM((S, 128), F32), pltpu.VMEM((2, S, 128), F32), pltpu.VMEM((2, S, 128), F32),
                         pltpu.VMEM((D, 3 * 128), BF), pltpu.SemaphoreType.DMA((3,))]
                        + _piece_sems(3 * hpr) + WEIGHT_GATHER_SEMS * 3),
        input_output_aliases={2: 5, 3: 6, 4: 7},
        compiler_params=_params(1),
    )(slopes, u, *gathered)


def _set_rows(shape, rows):
    idx = lax.broadcasted_iota(jnp.int32, shape, 0)
    out = jnp.zeros(shape, F32)
    for r, val in rows.items():
        out = out + jnp.where(idx == r, val, 0.0)
    return out


def _mid(yc_in, pa_mid, o, x2, target, b_merge, final_g, w3):
    S = x2.shape[0]
    tm = ROW_TILE
    nsteps = S // tm
    tile = pl.BlockSpec((tm, D), lambda i: (i, 0))

    def body(yc_ref, za_ref, gcp_ref, gap_ref, o_ref, x_ref, t_ref, b_ref, fg_ref, w_ref,
             dh_ref, dmid_ref, do_ref, dyc_ref, gw_ref, small_ref, acc, stage):
        i = pl.program_id(0)

        @pl.when(i == 0)
        def _():
            acc[...] = jnp.zeros_like(acc)
            small_ref[...] = jnp.zeros_like(small_ref)

        wc, wa, wo = w_ref[0], w_ref[1], w_ref[2]
        z = za_ref[...].astype(F32)
        sg = _sigmoid(z)
        ov = o_ref[...]
        yc_in_b, ya_in_b = yc_ref[...], (z * sg * ov).astype(BF)
        yc = _dot(yc_in_b, wc)
        ya = _dot(ya_in_b, wa)
        b = b_ref[...]
        gc = _sigmoid(gcp_ref[...].astype(F32) + b[:, :D])
        ga = _sigmoid(gap_ref[...].astype(F32) + b[:, D:])
        merged = gc * yc + ga * ya
        merged_b = merged.astype(BF)
        h = x_ref[...] + _dot(merged_b, wo)
        r2 = lax.rsqrt(jnp.mean(h * h, axis=-1, keepdims=True) + EPS)
        n = h * r2
        fg = fg_ref[...]
        err = n * fg - t_ref[...]
        loss = 0.5 * jnp.sum(jnp.sum(err * err, axis=-1, keepdims=True) / D, axis=0, keepdims=True)
        dy = err / D
        g_fg = jnp.sum(dy * n, axis=0, keepdims=True)
        dn = dy * fg
        dh = r2 * (dn - n * jnp.mean(dn * n, axis=-1, keepdims=True))
        dh_ref[...] = dh
        dh_b = dh.astype(BF)
        dmerged = _dot_nt(dh_b, wo)
        acc[2] += _dot(merged.T.astype(BF), dh_b)
        dyc = (dmerged * gc).astype(BF)
        dya = (dmerged * ga).astype(BF)
        dgcp = dmerged * yc * gc * (1.0 - gc)
        dgap = dmerged * ya * ga * (1.0 - ga)
        dmid_ref[1] = dgcp.astype(BF)
        dmid_ref[2] = dgap.astype(BF)
        acc[0] += _dot(yc_in_b.astype(F32).T.astype(BF), dyc)
        acc[1] += _dot(ya_in_b.astype(F32).T.astype(BF), dya)
        dyc_ref[...] = _dot_nt(dyc, wc).astype(BF)
        dya_in = _dot_nt(dya, wa)
        do_ref[...] = dya_in * (z * sg)
        dmid_ref[0] = (dya_in * ov * (sg * (1.0 + z * (1.0 - sg)))).astype(BF)
        small_ref[...] += _set_rows((8, D), {
            1: jnp.sum(dgcp, axis=0, keepdims=True), 2: jnp.sum(dgap, axis=0, keepdims=True),
            3: g_fg, 7: jnp.broadcast_to(loss, (1, D))})

        @pl.when(i == nsteps - 1)
        def _():
            for p in range(N_DEV):
                for a in range(3):
                    stage[...] = acc[a, p * ROW_SHARD:(p + 1) * ROW_SHARD, :].astype(BF)
                    pltpu.sync_copy(stage, gw_ref.at[p, a])

    return pl.pallas_call(
        body, name="mid", grid=(nsteps,),
        in_specs=[tile, pl.BlockSpec((tm, D), lambda i: (i, 0)), pl.BlockSpec((tm, D), lambda i: (i, 1)),
                  pl.BlockSpec((tm, D), lambda i: (i, 2)), tile, tile, tile,
                  pl.BlockSpec((1, 2 * D), lambda i: (0, 0)), pl.BlockSpec((1, D), lambda i: (0, 0)), VMEM_SPEC],
        out_specs=(tile, pl.BlockSpec((3, tm, D), lambda i: (0, i, 0)), tile, tile,
                   ANY_SPEC, pl.BlockSpec((8, D), lambda i: (0, 0))),
        out_shape=(jax.ShapeDtypeStruct((S, D), F32), jax.ShapeDtypeStruct((3, S, D), BF),
                   jax.ShapeDtypeStruct((S, D), F32), jax.ShapeDtypeStruct((S, D), BF),
                   jax.ShapeDtypeStruct((N_DEV, 3, ROW_SHARD, D), BF), jax.ShapeDtypeStruct((8, D), F32)),
        scratch_shapes=[pltpu.VMEM((3, D, D), F32), pltpu.VMEM((ROW_SHARD, D), BF)],
        compiler_params=_params(1),
    )(yc_in, pa_mid, pa_mid, pa_mid, o, x2, target, b_merge, final_g, w3)


def _conv_bwd(dyc_in, pa, cw8):
    S = pa.shape[0]
    tm, tc = CONV_TM, CONV_TC
    nct = D // tc
    nrt = S // tm
    last_halo = S // HALO - 1

    def seg(s):
        return pl.BlockSpec((tm, tc), lambda j, i, s=s: (i, s * nct + j))

    def halo_before(s):
        return pl.BlockSpec((HALO, tc), lambda j, i, s=s: (jnp.maximum(i * (tm // HALO) - 1, 0), s * nct + j))

    def halo_after(s):
        return pl.BlockSpec((HALO, tc), lambda j, i, s=s: (jnp.minimum((i + 1) * (tm // HALO), last_halo), s * nct + j))

    def body(dy, xc, bg, cg, zc, xch, cgh, dyn, bgn, zcn, cw, dout, gcw):
        i = pl.program_id(1)

        @pl.when(i == 0)
        def _():
            gcw[...] = jnp.zeros_like(gcw)

        xcv, cgv = xc[...].astype(F32), cg[...].astype(F32)
        a = cgv * xcv
        ah = jnp.where(i > 0, cgh[...].astype(F32) * xch[...].astype(F32), 0.0)
        row = lax.broadcasted_iota(jnp.int32, (tm, tc), 0)
        a1 = jnp.where(row == 0, ah[HALO - 1:HALO, :], pltpu.roll(a, 1, 0))
        a2 = jnp.where(row == 0, ah[HALO - 2:HALO - 1, :],
                       jnp.where(row == 1, ah[HALO - 1:HALO, :], pltpu.roll(a, 2, 0)))
        w = cw[...]
        conv = w[0:1, :] * a2 + w[1:2, :] * a1 + w[2:3, :] * a
        z = zc[...].astype(F32)
        sg = _sigmoid(z)
        silu = z * sg
        bgv = bg[...].astype(F32)
        dyv = dy[...].astype(F32)
        dout[3] = (dyv * bgv * conv * (sg * (1.0 + z * (1.0 - sg)))).astype(BF)
        dout[1] = (dyv * silu * conv).astype(BF)
        dc = dyv * silu * bgv
        zn = zcn[...].astype(F32)
        dcn = dyn[...].astype(F32) * (zn * _sigmoid(zn)) * bgn[...].astype(F32)
        dcn = jnp.where(i < nrt - 1, dcn, 0.0)
        dc1 = jnp.where(row == tm - 1, dcn[0:1, :], pltpu.roll(dc, tm - 1, 0))
        dc2 = jnp.where(row == tm - 1, dcn[1:2, :],
                        jnp.where(row == tm - 2, dcn[0:1, :], pltpu.roll(dc, tm - 2, 0)))
        da = w[2:3, :] * dc + w[1:2, :] * dc1 + w[0:1, :] * dc2
        dout[2] = (da * xcv).astype(BF)
        dout[0] = (da * cgv).astype(BF)
        gcw[...] += _set_rows((8, tc), {
            4: jnp.sum(dc * a2, axis=0, keepdims=True), 5: jnp.sum(dc * a1, axis=0, keepdims=True),
            6: jnp.sum(dc * a, axis=0, keepdims=True)})

    return pl.pallas_call(
        body, name="conv_bwd", grid=(nct, nrt),
        in_specs=[pl.BlockSpec((tm, tc), lambda j, i: (i, j)), seg(0), seg(1), seg(2), seg(3),
                  halo_before(0), halo_before(2),
                  pl.BlockSpec((HALO, tc), lambda j, i: (jnp.minimum((i + 1) * (tm // HALO), last_halo), j)),
                  halo_after(1), halo_after(3), pl.BlockSpec((8, tc), lambda j, i: (0, j))],
        out_specs=(pl.BlockSpec((4, tm, tc), lambda j, i: (0, i, j)), pl.BlockSpec((8, tc), lambda j, i: (0, j))),
        out_shape=(jax.ShapeDtypeStruct((4, S, D), BF), jax.ShapeDtypeStruct((8, D), F32)),
        compiler_params=_params(2),
    )(dyc_in, pa, pa, pa, pa, pa, pa, dyc_in, pa, pa, cw8)


def _attn_bwd(q, k, v, slopes, do, o, lse, g_in, g_3):
    S = q.shape[0]
    hpr = HEAD_PAIRS
    n_blocks = S // QB

    def body(sl_ref, q_ref, k_ref, v_ref, do_ref, o_ref, lse_ref, gin_ref, g3_ref, out_ref, rin_ref, r3_ref,
             dq_s, dk_s, dv_s, dd_s, *sems):
        hp = pl.program_id(0)
        exchanges = (_GradExchange(gin_ref, rin_ref, *sems[:3], _shard_cols((0, SEG0_ATTN * D), (SEG0_MID * D, IN_COLS))),
                     _GradExchange(g3_ref, r3_ref, *sems[3:], lambda p: ()))

        @pl.when(hp == 0)
        def _():
            for ex in exchanges:
                ex.start()

        tri_le, dist, low = _fold_masks()
        low_b = low.astype(F32).astype(BF)
        high_b = 1.0 - low_b
        head_sum = _head_sum_matrix()
        dq_s[...] = jnp.zeros(dq_s.shape, F32)
        dk_s[...] = jnp.zeros(dk_s.shape, F32)
        dv_s[...] = jnp.zeros(dv_s.shape, F32)

        def row_dots(i, carry):
            rows = pl.ds(pl.multiple_of(i * QB, QB), QB)
            dd = _dot(_hi_lo(do_ref[rows, :] * o_ref[rows, :]), head_sum)
            dd_s[0, rows, :] = dd[:, :QB]
            dd_s[1, rows, :] = dd[:, QB:]
            return carry

        lax.fori_loop(0, n_blocks, row_dots, 0)

        for d in DILATIONS:
            slope = [sl_ref[2 * hp + a] * float(d) for a in range(2)]
            bias = [slope[a] * dist for a in range(2)]

            def block(b, d=d, slope=slope, bias=bias):
                n, cur, prev = _block_rows(b, d, S)
                has_prev = n > 0
                valid = jnp.logical_or(tri_le, has_prev)
                q2f = q_ref[cur, :] * 0.125
                q2 = q2f.astype(BF)
                qs = jnp.concatenate([q2 * low_b, q2 * high_b], axis=0)
                kp, vp = k_ref[prev, :], v_ref[prev, :]
                kp_b, vp_b = kp.astype(BF), vp.astype(BF)
                kcat = jnp.concatenate([kp_b, k_ref[cur, :].astype(BF)], axis=0)
                vcat = jnp.concatenate([vp_b, v_ref[cur, :].astype(BF)], axis=0)
                do2f = do_ref[cur, :]
                do2 = do2f.astype(BF)
                dos = jnp.concatenate([do2 * low_b, do2 * high_b], axis=0)
                s2 = _dot_nt(qs, kcat)
                dp2 = _dot_nt(dos, vcat)
                diag2 = _dot(jnp.concatenate([_hi_lo(q2.astype(F32) * kp_b.astype(F32)),
                                              _hi_lo(do2.astype(F32) * vp_b.astype(F32))], axis=0), head_sum)
                p_rows, ds_rows, pe_h, dse_h = [], [], [], []
                for a in range(2):
                    hs = slice(a * QB, (a + 1) * QB)
                    sp, sc = s2[hs, :QB], s2[hs, QB:]
                    dpp, dpc = dp2[hs, :QB], dp2[hs, QB:]
                    lse_a, dd_a = lse_ref.at[a][cur, :], dd_s.at[a][cur, :]
                    comb = jnp.where(tri_le, sc, sp) - bias[a]
                    e = diag2[:QB, hs] - slope[a] * float(QB)
                    p = jnp.where(valid, jnp.exp(comb - lse_a), 0.0)
                    pe = jnp.where(has_prev, jnp.exp(e - lse_a), 0.0)
                    ds = p * (jnp.where(tri_le, dpc, dpp) - dd_a)
                    dse_h.append(pe * (diag2[QB:, hs] - dd_a))
                    pe_h.append(pe)
                    p_rows.append(jnp.concatenate([jnp.where(tri_le, 0.0, p).astype(BF),
                                                   jnp.where(tri_le, p, 0.0).astype(BF)], axis=1))
                    ds_rows.append(jnp.concatenate([jnp.where(tri_le, 0.0, ds).astype(BF),
                                                    jnp.where(tri_le, ds, 0.0).astype(BF)], axis=1))
                pst = jnp.concatenate(p_rows, axis=0)
                dst = jnp.concatenate(ds_rows, axis=0)
                pe2 = jnp.where(low, pe_h[0], pe_h[1])
                dse2 = jnp.where(low, dse_h[0], dse_h[1])
                dq = _dot(dst, kcat)
                dq_s[cur, :] += (jnp.where(low, dq[:QB], dq[QB:]) + dse2 * kp) * 0.125
                dk = _dot_tn(dst, qs)
                dv = _dot_tn(pst, dos)
                dk_s[prev, :] += dk[:QB] + dse2 * q2f
                dk_s[cur, :] += dk[QB:]
                dv_s[prev, :] += dv[:QB] + pe2 * do2f
                dv_s[cur, :] += dv[QB:]

            def several(it, carry, block=block):
                for u in range(ATT_UNROLL):
                    block(it * ATT_UNROLL + u)
                return carry

            lax.fori_loop(0, n_blocks // ATT_UNROLL, several, 0)

        def finish(i, carry):
            rows = pl.ds(pl.multiple_of(i * QB, QB), QB)
            out_ref[0, rows, :] = dq_s[rows, :].astype(BF)
            out_ref[1, rows, :] = dk_s[rows, :].astype(BF)
            out_ref[2, rows, :] = dv_s[rows, :].astype(BF)
            return carry

        lax.fori_loop(0, n_blocks, finish, 0)

        @pl.when(hp == hpr - 1)
        def _():
            for ex in exchanges:
                ex.finish()

    col = pl.BlockSpec((S, 128), lambda h: (0, h))
    return pl.pallas_call(
        body, name="attn_bwd", grid=(hpr,),
        in_specs=[SMEM_SPEC, col, col, col, col, col, pl.BlockSpec((2, S, 128), lambda h: (0, 0, h)),
                  ANY_SPEC, ANY_SPEC],
        out_specs=(pl.BlockSpec((3, S, 128), lambda h: (0, 0, h)), ANY_SPEC, ANY_SPEC),
        out_shape=(jax.ShapeDtypeStruct((3, S, D), BF), jax.ShapeDtypeStruct(g_in.shape, BF),
                   jax.ShapeDtypeStruct(g_3.shape, BF)),
        scratch_shapes=([pltpu.VMEM((S, 128), F32)] * 3 + [pltpu.VMEM((2, S, 128), F32)]
                        + GRAD_EXCHANGE_SEMS + GRAD_EXCHANGE_SEMS),
        compiler_params=_params(1),
    )(slopes, q, k, v, do, o, lse, g_in, g_3)


WG_TN = 256
SEG0_CONV, SEG0_ATTN, SEG0_MID = 0, 4, 7


def _wgrad_in(ut, d_group, seg0, g_in, name):
    S = ut.shape[1]
    tn = WG_TN
    per_seg = D // tn
    per_shard = W_IN_SHARD // tn
    n_tiles = d_group.shape[0] * per_seg
    tile0 = seg0 * per_seg

    def body(ut_ref, d_ref, *rest):
        rest[-1][0] = _dot(ut_ref[...], d_ref[0]).astype(BF)

    operands, in_specs, aliases = [ut, d_group], [VMEM_SPEC, pl.BlockSpec((1, S, tn), lambda t: (t // per_seg, 0, t % per_seg))], {}
    if g_in is not None:
        operands.append(g_in)
        in_specs.append(ANY_SPEC)
        aliases = {2: 0}
    return pl.pallas_call(
        body, name=name, grid=(n_tiles,), in_specs=in_specs,
        out_specs=pl.BlockSpec((1, D, tn), lambda t: ((tile0 + t) // per_shard, 0, (tile0 + t) % per_shard)),
        out_shape=jax.ShapeDtypeStruct((N_DEV, D, W_IN_SHARD), BF),
        input_output_aliases=aliases,
        compiler_params=_params(1),
    )(*operands)


def _dgrad_norm_bwd(d_conv, d_attn, d_mid, w_all, x2, dh, norm_g, g_in, r_in):
    S = x2.shape[0]
    tm = ROW_TILE
    nsteps = S // tm
    tile = pl.BlockSpec((tm, D), lambda i: (i, 0))
    pieces = _proj_pieces()

    def body(a_ref, b_ref, c_ref, w_ref, x_ref, dh_ref, g_ref, gin_ref, rin_in_ref, gx_ref, small_ref, rin_ref, *sems):
        i = pl.program_id(0)
        exchange = _GradExchange(gin_ref, rin_ref, *sems, _shard_cols((SEG0_ATTN * D, SEG0_MID * D)))

        @pl.when(i == 0)
        def _():
            small_ref[...] = jnp.zeros_like(small_ref)
            exchange.start()

        groups = (a_ref, b_ref, c_ref)
        du = jnp.zeros((tm, D), F32)
        for s, sc, p, pc, width in pieces:
            g = 0 if s < 4 else (1 if s < 7 else 2)
            local = s - (0, 4, 7)[g]
            du = du + _dot_nt(groups[g][local, :, sc:sc + width], w_ref[p, :, pc:pc + width])
        xv = x_ref[...]
        r = lax.rsqrt(jnp.mean(xv * xv, axis=-1, keepdims=True) + EPS)
        n = xv * r
        dn = du * g_ref[...]
        gx_ref[...] = dh_ref[...] + r * (dn - n * jnp.mean(dn * n, axis=-1, keepdims=True))
        small_ref[...] += _set_rows((8, D), {0: jnp.sum(du * n, axis=0, keepdims=True)})

        @pl.when(i == nsteps - 1)
        def _():
            exchange.finish()

    return pl.pallas_call(
        body, name="dgrad_norm_bwd", grid=(nsteps,),
        in_specs=[pl.BlockSpec((4, tm, D), lambda i: (0, i, 0)), pl.BlockSpec((3, tm, D), lambda i: (0, i, 0)),
                  pl.BlockSpec((3, tm, D), lambda i: (0, i, 0)), VMEM_SPEC, tile, tile,
                  pl.BlockSpec((1, D), lambda i: (0, 0)), ANY_SPEC, ANY_SPEC],
        out_specs=(tile, pl.BlockSpec((8, D), lambda i: (0, 0)), ANY_SPEC),
        out_shape=(jax.ShapeDtypeStruct((S, D), F32), jax.ShapeDtypeStruct((8, D), F32),
                   jax.ShapeDtypeStruct(r_in.shape, BF)),
        scratch_shapes=GRAD_EXCHANGE_SEMS,
        input_output_aliases={8: 2},
        compiler_params=_params(1),
    )(d_conv, d_attn, d_mid, w_all, x2, dh, norm_g, g_in, r_in)


def _adamw_math(w, g, m, v):
    m = ADAM_B1 * m + (1.0 - ADAM_B1) * g
    v = ADAM_B2 * v + (1.0 - ADAM_B2) * (g * g)
    m_hat = m / (1.0 - ADAM_B1 ** ADAM_STEP)
    v_hat = v / (1.0 - ADAM_B2 ** ADAM_STEP)
    delta = -ADAM_LR * (m_hat / (jnp.sqrt(v_hat) + ADAM_EPS) + ADAM_WD * w)
    return delta, m, v


def _sum_adamw(parts, w, m, v, tm, name):
    R, C = w.shape
    tile = pl.BlockSpec((tm, C), lambda i: (i, 0))

    def body(p_ref, w_ref, m_ref, v_ref, g_out, d_out, m_out, v_out):
        g = p_ref[0].astype(F32)
        for s in range(1, N_DEV):
            g = g + p_ref[s].astype(F32)
        g_out[...] = g
        d_out[...], m_out[...], v_out[...] = _adamw_math(w_ref[...], g, m_ref[...], v_ref[...])

    shape = jax.ShapeDtypeStruct((R, C), F32)
    return pl.pallas_call(
        body, name=name, grid=(R // tm,),
        in_specs=[pl.BlockSpec((N_DEV, tm, C), lambda i: (0, i, 0)), tile, tile, tile],
        out_specs=(tile, tile, tile, tile), out_shape=(shape, shape, shape, shape),
        compiler_params=_params(1),
    )(parts, w, m, v)


def _adamw(g, w, m, v, name):
    def body(g_ref, w_ref, m_ref, v_ref, d_out, m_out, v_out):
        d_out[...], m_out[...], v_out[...] = _adamw_math(w_ref[...], g_ref[...], m_ref[...], v_ref[...])

    shape = jax.ShapeDtypeStruct(w.shape, F32)
    return pl.pallas_call(
        body, name=name, in_specs=[VMEM_SPEC] * 4, out_specs=(VMEM_SPEC,) * 3, out_shape=(shape, shape, shape),
    )(g, w, m, v)


def _alibi_slopes():
    return jnp.exp2(-8.0 * jnp.arange(1, N_HEADS + 1, dtype=F32) / N_HEADS)


def _local_step(x2, target, norm_g, b_merge, final_g, w_all, w3_all, cw_all):
    slopes = _alibi_slopes()
    u, ut = _norm(x2, norm_g)
    o, lse, q, k, v, w_all, w3_all, cw_all = _attn_fwd(u, slopes, w_all, w3_all, cw_all)
    w3 = jnp.transpose(w3_all, (1, 0, 2, 3)).reshape(3, D, D)
    cw8 = jnp.transpose(cw_all, (1, 0, 2)).reshape(8, D)
    pa = _proj_cols(u, w_all, SEG0_CONV, 4, BF, "proj_conv")
    yc_in = _conv_fwd(pa, cw8)
    pa_mid = _proj_cols(u, w_all, SEG0_MID, 3, BF, "proj_mid")
    dh, d_mid, do, dyc_in, g_3, small_mid = _mid(yc_in, pa_mid, o, x2, target, b_merge, final_g, w3)
    g_in = _wgrad_in(ut, d_mid, SEG0_MID, None, "wgrad_in_mid")
    d_conv, small_conv = _conv_bwd(dyc_in, pa, cw8)
    g_in = _wgrad_in(ut, d_conv, SEG0_CONV, g_in, "wgrad_in_conv")
    d_attn, r_in, r_3 = _attn_bwd(q, k, v, slopes, do, o, lse, g_in, g_3)
    g_in = _wgrad_in(ut, d_attn, SEG0_ATTN, g_in, "wgrad_in_attn")
    grad_x, small_norm, r_in = _dgrad_norm_bwd(d_conv, d_attn, d_mid, w_all, x2, dh, norm_g, g_in, r_in)
    return grad_x, r_in, r_3, small_mid, small_conv, small_norm


def kernel(x, norm_g, w_in, b_merge, conv_w, w_out_conv, w_out_attn, w_o, final_g, loss_target, m_norm_g, m_w_in, m_b_merge, m_conv_w, m_w_out_conv, m_w_out_attn, m_w_o, m_final_g, v_norm_g, v_w_in, v_b_merge, v_conv_w, v_w_out_conv, v_w_out_attn, v_w_o, v_final_g):
    me = 4 * lax.axis_index("x") + 2 * lax.axis_index("y") + lax.axis_index("c")
    stack3 = lambda a, b, c: jnp.concatenate([a, b, c], axis=0)
    pad8 = lambda a: jnp.pad(a, ((0, 8 - a.shape[0]), (0, 0)))

    w3_shard = stack3(w_out_conv, w_out_attn, w_o)
    w_all, w3_all, cw_all = _gather_first_weights(w_in[0], w3_shard, pad8(conv_w[0]))

    final_g2 = final_g.reshape(1, D)
    grad_x, r_in, r_3, small_mid, small_conv, small_norm = _local_step(
        x[0], loss_target[0], norm_g, b_merge, final_g2, w_all, w3_all, cw_all)

    small = _allreduce_small(small_mid, small_conv, small_norm)

    g_w_in, d_w_in, nm_w_in, nv_w_in = _sum_adamw(r_in, w_in[0], m_w_in[0], v_w_in[0], 128, "adamw_w_in")
    g_w3, d_w3, nm_w3, nv_w3 = _sum_adamw(
        r_3.reshape(N_DEV, 3 * ROW_SHARD, D), w3_shard.reshape(3 * ROW_SHARD, D),
        stack3(m_w_out_conv, m_w_out_attn, m_w_o).reshape(3 * ROW_SHARD, D),
        stack3(v_w_out_conv, v_w_out_attn, v_w_o).reshape(3 * ROW_SHARD, D), ROW_SHARD, "adamw_w3")

    def pack(ng, bm, fg):
        return pad8(jnp.concatenate([ng, bm.reshape(2, D), fg.reshape(1, D)], axis=0))

    d_s, nm_s, nv_s = _adamw(small, pack(norm_g, b_merge, final_g), pack(m_norm_g, m_b_merge, m_final_g),
                             pack(v_norm_g, v_b_merge, v_final_g), "adamw_small")
    g_cw = lax.dynamic_slice(small, (4, me * ROW_SHARD), (3, ROW_SHARD))
    d_cw, nm_cw, nv_cw = _adamw(g_cw, conv_w[0], m_conv_w[0], v_conv_w[0], "adamw_conv_w")

    loss = small[7, 0]
    split3 = lambda t: tuple(t[a * ROW_SHARD:(a + 1) * ROW_SHARD][None] for a in range(3))
    unpack = lambda t: (t[0:1], t[1:3].reshape(1, 2 * D), t[3])

    def leaves(in_, small_, cw_, w3_):
        ng, bm, fg = unpack(small_)
        wc, wa, wo = split3(w3_)
        return (ng, in_[None], bm, cw_[None], wc, wa, wo, fg)

    return (loss, grad_x[None],
            *leaves(g_w_in, small, g_cw, g_w3),
            *leaves(d_w_in, d_s, d_cw, d_w3),
            *leaves(nm_w_in, nm_s, nm_cw, nm_w3),
            *leaves(nv_w_in, nv_s, nv_cw, nv_w3))
```

```python
import functools

import jax
import jax.numpy as jnp
from jax import lax
from jax.experimental import pallas as pl
from jax.experimental.pallas import tpu as pltpu

D = 1024
N_HEADS = 16
HEAD_DIM = 64
N_SEG = 10
IN_COLS = N_SEG * D
N_DEV = 8
W_IN_SHARD = IN_COLS // N_DEV
ROW_SHARD = D // N_DEV
QB = 128
DILATIONS = (1, 4, 16)
EPS = 1e-6
NEG = -1e30
BF = jnp.bfloat16
F32 = jnp.float32
MESH = pl.DeviceIdType.MESH

ADAM_LR = 0.001
ADAM_B1 = 0.9
ADAM_B2 = 0.999
ADAM_EPS = 1e-08
ADAM_WD = 0.01
ADAM_STEP = 10

V7X_VMEM_BYTES = 64 * 1024 * 1024
VMEM_LIMIT = V7X_VMEM_BYTES - 8 * 1024 * 1024
ROW_TILE = 256

VMEM_SPEC = pl.BlockSpec(memory_space=pltpu.VMEM)
ANY_SPEC = pl.BlockSpec(memory_space=pl.ANY)
SMEM_SPEC = pl.BlockSpec(memory_space=pltpu.SMEM)


def _params(n_grid_axes, vmem=VMEM_LIMIT):
    return pltpu.CompilerParams(dimension_semantics=("arbitrary",) * n_grid_axes, vmem_limit_bytes=vmem)


def _dot(a, b):
    return jnp.dot(a, b, preferred_element_type=F32)


def _dot_nt(a, b):
    return lax.dot_general(a, b, (((1,), (1,)), ((), ())), preferred_element_type=F32)


def _dot_tn(a, b):
    return lax.dot_general(a, b, (((0,), (0,)), ((), ())), preferred_element_type=F32)


def _sigmoid(z):
    return 1.0 / (1.0 + jnp.exp(-z))


def _my_place():
    x, y, c = lax.axis_index("x"), lax.axis_index("y"), lax.axis_index("c")
    return x, y, c, 4 * x + 2 * y + c


def _peers(x, y, c):
    out = []
    for k in range(1, N_DEV):
        px = 1 - x if k & 4 else x
        py = 1 - y if k & 2 else y
        pc = 1 - c if k & 1 else c
        out.append(((px, py, pc), 4 * px + 2 * py + pc))
    return out


def _device(p):
    return (p >> 2, (p >> 1) & 1, p & 1)


def _shard_cols(*ranges):
    def cols(p):
        found = None
        for lo, hi in ranges:
            a, b = max(lo, p * W_IN_SHARD), min(hi, (p + 1) * W_IN_SHARD)
            if a < b:
                assert found is None
                found = (a - p * W_IN_SHARD, b - p * W_IN_SHARD)
        return found

    return cols


def _whole(p):
    return ()


def _block(ref, idx, cols):
    return ref.at[idx] if cols == () else ref.at[idx, :, cols[0]:cols[1]]


class _WeightGather:
    def __init__(self, src, dst, send_sems, recv_sems, cols):
        self.src, self.dst, self.cols = src, dst, cols
        self.send_sems, self.recv_sems = send_sems, recv_sems
        self.me = _my_place()[3]

    def _copy(self, p, target):
        cols = self.cols(p)
        return pltpu.make_async_remote_copy(
            src_ref=self.src(p, cols), dst_ref=_block(self.dst, p, cols), send_sem=self.send_sems.at[target],
            recv_sem=self.recv_sems.at[p], device_id=_device(target), device_id_type=MESH)

    def _each(self, send, receive):
        for p in range(N_DEV):
            if self.cols(p) is None:
                continue

            def sender(p=p):
                for k in range(1, N_DEV):
                    send(self._copy(p, (p + k) % N_DEV))

            pl.when(self.me == p)(sender)
            pl.when(self.me != p)(lambda p=p: receive(self._copy(p, p)))

    def start(self):
        self._each(lambda cp: cp.start(), lambda cp: None)

    def finish(self):
        self._each(lambda cp: cp.wait_send(), lambda cp: cp.wait_recv())


WEIGHT_GATHER_SEMS = [pltpu.SemaphoreType.DMA((N_DEV,)), pltpu.SemaphoreType.DMA((N_DEV,))]
REST_COLS = _shard_cols((0, 4 * D), (7 * D, IN_COLS))
HEAD_PAIRS = D // 128


def _qkv_piece(h, seg):
    col = (4 + seg) * D + 128 * h
    return col // W_IN_SHARD, col % W_IN_SHARD


class _PieceGather:
    def __init__(self, src, dst, send_sems, recv_sems):
        self.src, self.dst, self.send_sems, self.recv_sems = src, dst, send_sems, recv_sems
        self.me = _my_place()[3]

    def _copy(self, i, target):
        p, lo = _qkv_piece(i // 3, i % 3)
        return pltpu.make_async_remote_copy(
            src_ref=self.src(p, lo, lo + 128), dst_ref=self.dst.at[p, :, lo:lo + 128], send_sem=self.send_sems.at[i, target],
            recv_sem=self.recv_sems.at[i], device_id=_device(target), device_id_type=MESH)

    def _owner(self, i, act):
        p = _qkv_piece(i // 3, i % 3)[0]

        def sender():
            for k in range(N_DEV - 1):
                act(self._copy(i, (p + 1 + (k + i) % (N_DEV - 1)) % N_DEV))

        pl.when(self.me == p)(sender)

    def start(self, pieces):
        for i in pieces:
            self._owner(i, lambda cp: cp.start())

    def wait_send(self, pieces):
        for i in pieces:
            self._owner(i, lambda cp: cp.wait_send())

    def wait_recv(self, pieces):
        for i in pieces:
            p = _qkv_piece(i // 3, i % 3)[0]
            pl.when(self.me != p)(lambda i=i, p=p: self._copy(i, p).wait_recv())


def _piece_sems(n):
    return [pltpu.SemaphoreType.DMA((n, N_DEV)), pltpu.SemaphoreType.DMA((n,))]


def _gather_first_weights(w_in, w3, cw):
    def body(w_in_ref, w3_ref, cw_ref, o_in, o_3, o_cw, in_bf, w3_bf, local_sems, *sems):
        me = _my_place()[3]

        def cast_rows(i, carry):
            r = pl.multiple_of(i * 128, 128)
            in_bf[pl.ds(r, 128), :] = w_in_ref[pl.ds(r, 128), :].astype(BF)
            return carry

        lax.fori_loop(0, D // 128, cast_rows, 0)
        for a in range(3):
            w3_bf[a] = w3_ref[a].astype(BF)
        gather = _PieceGather(lambda p, lo, hi: in_bf.at[:, lo:hi], o_in, *sems)
        gather.start(range(3))
        local = [pltpu.make_async_copy(src, dst.at[me], local_sems.at[a])
                 for a, (src, dst) in enumerate(((in_bf, o_in), (w3_bf, o_3), (cw_ref, o_cw)))]
        for cp in local:
            cp.start()
        gather.wait_recv(range(3))
        gather.wait_send(range(3))
        for cp in local:
            cp.wait()

    return pl.pallas_call(
        body, name="gather_first_weights",
        out_shape=(jax.ShapeDtypeStruct((N_DEV, D, W_IN_SHARD), BF),
                   jax.ShapeDtypeStruct((N_DEV, 3, ROW_SHARD, D), BF),
                   jax.ShapeDtypeStruct((N_DEV, 8, 128), F32)),
        in_specs=[VMEM_SPEC, VMEM_SPEC, VMEM_SPEC],
        out_specs=(ANY_SPEC, ANY_SPEC, ANY_SPEC),
        scratch_shapes=[pltpu.VMEM((D, W_IN_SHARD), BF), pltpu.VMEM((3, ROW_SHARD, D), BF),
                        pltpu.SemaphoreType.DMA((3,))] + _piece_sems(3),
        compiler_params=pltpu.CompilerParams(vmem_limit_bytes=VMEM_LIMIT),
    )(w_in, w3, cw)


class _GradExchange:
    def __init__(self, src, dst, send_sems, recv_sems, local_sem, cols):
        self.src, self.dst, self.cols = src, dst, cols
        self.send_sems, self.recv_sems, self.local_sem = send_sems, recv_sems, local_sem
        self.me = _my_place()[3]

    def _remote(self, p, source):
        return pltpu.make_async_remote_copy(
            src_ref=_block(self.src, p, self.cols(p)), dst_ref=_block(self.dst, source, self.cols(p)),
            send_sem=self.send_sems.at[p], recv_sem=self.recv_sems.at[source],
            device_id=_device(p), device_id_type=MESH)

    def _local(self, p):
        return pltpu.make_async_copy(_block(self.src, p, self.cols(p)), _block(self.dst, p, self.cols(p)),
                                     self.local_sem)

    def _as_each_device(self, send, local, receive):
        for m in range(N_DEV):
            def branch(m=m):
                for k in range(1, N_DEV):
                    p = (m + k) % N_DEV
                    if self.cols(p) is not None:
                        send(self._remote(p, m))
                if self.cols(m) is not None:
                    local(self._local(m))
                    for k in range(1, N_DEV):
                        receive(self._remote(m, (m + k) % N_DEV))

            pl.when(self.me == m)(branch)

    def start(self):
        self._as_each_device(lambda cp: cp.start(), lambda cp: cp.start(), lambda cp: None)

    def finish(self):
        self._as_each_device(lambda cp: cp.wait_send(), lambda cp: cp.wait(), lambda cp: cp.wait_recv())


GRAD_EXCHANGE_SEMS = [pltpu.SemaphoreType.DMA((N_DEV,)), pltpu.SemaphoreType.DMA((N_DEV,)), pltpu.SemaphoreType.DMA]


def _allreduce_small(p_mid, p_conv, p_norm):
    def body(a_ref, b_ref, c_ref, out_ref, mine, gathered, send_sems, recv_sems):
        x, y, c, me = _my_place()
        mine[...] = a_ref[...] + b_ref[...] + c_ref[...]
        gathered[me] = mine[...]
        remote = []
        for k, (peer, _) in enumerate(_peers(x, y, c)):
            cp = pltpu.make_async_remote_copy(
                src_ref=mine, dst_ref=gathered.at[me], send_sem=send_sems.at[k], recv_sem=recv_sems.at[k],
                device_id=peer, device_id_type=MESH)
            cp.start()
            remote.append(cp)
        for cp in remote:
            cp.wait()
        total = gathered[0]
        for s in range(1, N_DEV):
            total = total + gathered[s]
        out_ref[...] = total

    return pl.pallas_call(
        body, name="allreduce_small",
        out_shape=jax.ShapeDtypeStruct((8, D), F32),
        in_specs=[VMEM_SPEC, VMEM_SPEC, VMEM_SPEC], out_specs=VMEM_SPEC,
        scratch_shapes=[pltpu.VMEM((8, D), F32), pltpu.VMEM((N_DEV, 8, D), F32),
                        pltpu.SemaphoreType.DMA((N_DEV - 1,)), pltpu.SemaphoreType.DMA((N_DEV - 1,))],
    )(p_mid, p_conv, p_norm)


def _proj_pieces():
    cuts = sorted(set(range(0, IN_COLS + 1, D)) | set(range(0, IN_COLS + 1, W_IN_SHARD)))
    return [(lo // D, lo % D, lo // W_IN_SHARD, lo % W_IN_SHARD, hi - lo) for lo, hi in zip(cuts[:-1], cuts[1:])]


def _norm(x2, norm_g):
    S = x2.shape[0]
    tm = ROW_TILE

    def body(x_ref, g_ref, u_ref, ut_ref):
        xv = x_ref[...]
        r = lax.rsqrt(jnp.mean(xv * xv, axis=-1, keepdims=True) + EPS)
        u = xv * r * g_ref[...]
        u_ref[...] = u.astype(BF)
        ut_ref[...] = u.T.astype(BF)

    return pl.pallas_call(
        body, name="norm", grid=(S // tm,),
        in_specs=[pl.BlockSpec((tm, D), lambda i: (i, 0)), pl.BlockSpec((1, D), lambda i: (0, 0))],
        out_specs=(pl.BlockSpec((tm, D), lambda i: (i, 0)), pl.BlockSpec((D, tm), lambda i: (0, i))),
        out_shape=(jax.ShapeDtypeStruct((S, D), BF), jax.ShapeDtypeStruct((D, S), BF)),
        compiler_params=_params(1),
    )(x2, norm_g)


PROJ_TN = 256


def _proj_cols(u, w_all, seg0, n_seg, dtype, name):
    S = u.shape[0]
    tn = PROJ_TN
    per_shard = W_IN_SHARD // tn
    tile0 = seg0 * D // tn

    def body(u_ref, w_ref, out_ref):
        out_ref[...] = _dot(u_ref[...], w_ref[0]).astype(dtype)

    return pl.pallas_call(
        body, name=name, grid=(n_seg * D // tn,),
        in_specs=[VMEM_SPEC, pl.BlockSpec((1, D, tn), lambda t: ((tile0 + t) // per_shard, 0, (tile0 + t) % per_shard))],
        out_specs=pl.BlockSpec((S, tn), lambda t: (0, t)),
        out_shape=jax.ShapeDtypeStruct((S, n_seg * D), dtype),
        compiler_params=_params(1),
    )(u, w_all)


CONV_TM, CONV_TC = 256, 512
HALO = 16


def _conv_fwd(pa, cw8):
    S = pa.shape[0]
    tm, tc = CONV_TM, CONV_TC
    nct = D // tc

    def seg(s):
        return pl.BlockSpec((tm, tc), lambda i, j, s=s: (i, s * nct + j))

    def halo_before(s):
        return pl.BlockSpec((HALO, tc), lambda i, j, s=s: (jnp.maximum(i * (tm // HALO) - 1, 0), s * nct + j))

    def body(xc, bg, cg, zc, xch, cgh, cw, out):
        i = pl.program_id(0)
        a = cg[...].astype(F32) * xc[...].astype(F32)
        ah = cgh[...].astype(F32) * xch[...].astype(F32)
        ah = jnp.where(i > 0, ah, 0.0)
        row = lax.broadcasted_iota(jnp.int32, (tm, tc), 0)
        a1 = jnp.where(row == 0, ah[HALO - 1:HALO, :], pltpu.roll(a, 1, 0))
        a2 = jnp.where(row == 0, ah[HALO - 2:HALO - 1, :],
                       jnp.where(row == 1, ah[HALO - 1:HALO, :], pltpu.roll(a, 2, 0)))
        w = cw[...]
        conv = w[0:1, :] * a2 + w[1:2, :] * a1 + w[2:3, :] * a
        z = zc[...].astype(F32)
        out[...] = (z * _sigmoid(z) * bg[...].astype(F32) * conv).astype(BF)

    return pl.pallas_call(
        body, name="conv_fwd", grid=(S // tm, nct),
        in_specs=[seg(0), seg(1), seg(2), seg(3), halo_before(0), halo_before(2),
                  pl.BlockSpec((8, tc), lambda i, j: (0, j))],
        out_specs=pl.BlockSpec((tm, tc), lambda i, j: (i, j)),
        out_shape=jax.ShapeDtypeStruct((S, D), BF),
        compiler_params=_params(2),
    )(pa, pa, pa, pa, pa, pa, cw8)


ATT_UNROLL = 2


LAYOUT_MOD = 4
RUN = QB // LAYOUT_MOD


def _fold_masks(d):
    row = lax.broadcasted_iota(jnp.int32, (QB, QB), 0)
    lane = lax.broadcasted_iota(jnp.int32, (QB, QB), 1)
    if d == 1:
        qpos, kpos = LAYOUT_MOD * (row % RUN) + row // RUN, LAYOUT_MOD * (lane % RUN) + lane // RUN
    else:
        qpos, kpos = row, lane
    tri_le = kpos <= qpos
    dist = jnp.where(tri_le, qpos - kpos, qpos - kpos + QB).astype(F32)
    return tri_le, dist, lane < HEAD_DIM


class _Rows:
    def __init__(self, slices):
        self.slices = slices

    def get(self, ref):
        parts = [ref[sl, :] for sl in self.slices]
        return parts[0] if len(parts) == 1 else jnp.concatenate(parts, axis=0)

    def put(self, ref, val):
        size = QB // len(self.slices)
        for g, sl in enumerate(self.slices):
            ref[sl, :] = val if len(self.slices) == 1 else val[g * size:(g + 1) * size]

    def add(self, ref, val):
        self.put(ref, self.get(ref) + val)


def _block_rows(b, d, S):
    quarter = S // LAYOUT_MOD
    nb = S // (QB * d)
    r, n = b // nb, b % nb
    n_prev = jnp.maximum(n - 1, 0)
    if d == 1:
        runs = lambda m: _Rows([pl.ds(pl.multiple_of(g * quarter + RUN * m, RUN), RUN) for g in range(LAYOUT_MOD)])
        return n, runs(n), runs(n_prev)
    if d == LAYOUT_MOD:
        block = lambda m: _Rows([pl.ds(pl.multiple_of(r * quarter + QB * m, QB), QB)])
        return n, block(n), block(n_prev)
    step = d // LAYOUT_MOD
    first = (r % LAYOUT_MOD) * quarter + r // LAYOUT_MOD
    strided = lambda m: _Rows([pl.ds(first + QB * step * m, QB, stride=step)])
    return n, strided(n), strided(n_prev)


def _natural_rows(i, S):
    per = S // LAYOUT_MOD // QB
    return pl.ds(i // per + LAYOUT_MOD * QB * (i % per), QB, stride=LAYOUT_MOD)


def _head_sum_matrix():
    r = lax.broadcasted_iota(jnp.int32, (2 * QB, 2 * QB), 0)
    c = lax.broadcasted_iota(jnp.int32, (2 * QB, 2 * QB), 1)
    return (((r % QB) // HEAD_DIM) == (c // QB)).astype(F32).astype(BF)


def _hi_lo(t):
    hi = t.astype(BF)
    return jnp.concatenate([hi, (t - hi.astype(F32)).astype(BF)], axis=1)


PROJ_ROWS = 512


def _attn_fwd(u, slopes, w_all, w3_all, cw_all):
    S = u.shape[0]
    hpr = HEAD_PAIRS
    n_blocks = S // QB
    later = range(3, 3 * hpr)

    def body(sl_ref, u_ref, w_in_ref, w3_in_ref, cw_in_ref, o_ref, lse_ref, q_ref, k_ref, v_ref, w_ref, w3_ref,
             cw_ref, acc, m_s, l_s, w_tile, staged, tile_sems, *sems):
        hp = pl.program_id(0)
        me = _my_place()[3]
        pieces = _PieceGather(lambda p, lo, hi: w_ref.at[p, :, lo:hi], w_ref, *sems[0:2])
        gathers = (_WeightGather(lambda p, cols: _block(w_ref, me, cols), w_ref, *sems[2:4], REST_COLS),
                   _WeightGather(lambda p, cols: w3_ref.at[me], w3_ref, *sems[4:6], _whole),
                   _WeightGather(lambda p, cols: cw_ref.at[me], cw_ref, *sems[6:8], _whole))

        @pl.when(hp == 0)
        def _():
            pieces.start(later)
            for g in gathers:
                g.start()

        for h in range(hpr):
            @pl.when(hp == h)
            def _(h=h):
                if h > 0:
                    pieces.wait_recv(range(3 * h, 3 * h + 3))
                fetch = []
                for seg in range(3):
                    p, lo = _qkv_piece(h, seg)
                    fetch.append(pltpu.make_async_copy(w_ref.at[p, :, lo:lo + 128], w_tile.at[:, seg * 128:(seg + 1) * 128],
                                                       tile_sems.at[seg]))
                    fetch[-1].start()
                for cp in fetch:
                    cp.wait()

        def project(i, carry):
            rows = pl.ds(pl.multiple_of(i * PROJ_ROWS, PROJ_ROWS), PROJ_ROWS)
            qkv = _dot(u_ref[rows, :], w_tile[...])
            per = PROJ_ROWS // LAYOUT_MOD
            for seg, ref in enumerate((q_ref, k_ref, v_ref)):
                staged[seg] = qkv[:, seg * 128:(seg + 1) * 128]
                for g in range(LAYOUT_MOD):
                    dst = pl.ds(pl.multiple_of(g * (S // LAYOUT_MOD) + i * per, per), per)
                    ref[dst, :] = staged.at[seg][pl.ds(g, per, stride=LAYOUT_MOD), :]
            return carry

        lax.fori_loop(0, S // PROJ_ROWS, project, 0)

        head_sum = _head_sum_matrix()
        ones_b = jnp.ones((2 * QB, QB), BF)
        m_s[...] = jnp.full(m_s.shape, NEG, F32)
        l_s[...] = jnp.zeros(l_s.shape, F32)
        acc[...] = jnp.zeros(acc.shape, F32)

        for d in DILATIONS:
            tri_le, dist, low = _fold_masks(d)
            low_b = low.astype(F32).astype(BF)
            high_b = 1.0 - low_b
            slope = [sl_ref[2 * hp + a] * float(d) for a in range(2)]
            bias = [slope[a] * dist for a in range(2)]

            def block(b, d=d, slope=slope, bias=bias, tri_le=tri_le, low=low, low_b=low_b, high_b=high_b):
                n, cur, prev = _block_rows(b, d, S)
                has_prev = n > 0
                valid = jnp.logical_or(tri_le, has_prev)
                q2 = (cur.get(q_ref) * 0.125).astype(BF)
                qs = jnp.concatenate([q2 * low_b, q2 * high_b], axis=0)
                vp = prev.get(v_ref)
                kp_b = prev.get(k_ref).astype(BF)
                kcat = jnp.concatenate([kp_b, cur.get(k_ref).astype(BF)], axis=0)
                vcat = jnp.concatenate([vp, cur.get(v_ref)], axis=0).astype(BF)
                s2 = _dot_nt(qs, kcat)
                e2 = _dot(_hi_lo(q2.astype(F32) * kp_b.astype(F32)), head_sum)
                p_rows, alpha_h, pe_h = [], [], []
                for a in range(2):
                    sp, sc = s2[a * QB:(a + 1) * QB, :QB], s2[a * QB:(a + 1) * QB, QB:]
                    comb = jnp.where(valid, jnp.where(tri_le, sc, sp) - bias[a], NEG)
                    e = jnp.where(has_prev, e2[:, a * QB:(a + 1) * QB] - slope[a] * float(QB), NEG)
                    m_old = cur.get(m_s.at[a])
                    m_new = jnp.maximum(jnp.maximum(m_old, jnp.max(comb, axis=-1, keepdims=True)), e)
                    cur.put(m_s.at[a], m_new)
                    p = jnp.exp(comb - m_new)
                    pe_h.append(jnp.exp(e - m_new))
                    alpha_h.append(jnp.exp(m_old - m_new))
                    p_rows.append(jnp.concatenate([jnp.where(tri_le, 0.0, p).astype(BF),
                                                   jnp.where(tri_le, p, 0.0).astype(BF)], axis=1))
                pv = _dot(jnp.concatenate(p_rows, axis=0), jnp.concatenate([vcat, ones_b], axis=1))
                for a in range(2):
                    cur.put(l_s.at[a], alpha_h[a] * cur.get(l_s.at[a]) + pv[a * QB:(a + 1) * QB, QB:] + pe_h[a])
                cur.put(acc, jnp.where(low, alpha_h[0], alpha_h[1]) * cur.get(acc)
                        + jnp.where(low, pv[:QB, :QB], pv[QB:, :QB]) + jnp.where(low, pe_h[0], pe_h[1]) * vp)

            def several(it, carry, block=block):
                for u in range(ATT_UNROLL):
                    block(it * ATT_UNROLL + u)
                return carry

            lax.fori_loop(0, n_blocks // ATT_UNROLL, several, 0)

        low = _fold_masks(LAYOUT_MOD)[2]

        def finish(i, carry):
            rows = pl.ds(pl.multiple_of(i * QB, QB), QB)
            l0, l1 = l_s[0, rows, :], l_s[1, rows, :]
            o_ref[_natural_rows(i, S), :] = acc[rows, :] / jnp.where(low, l0, l1)
            lse_ref[0, rows, :] = m_s[0, rows, :] + jnp.log(l0)
            lse_ref[1, rows, :] = m_s[1, rows, :] + jnp.log(l1)
            return carry

        lax.fori_loop(0, n_blocks, finish, 0)

        @pl.when(hp == hpr - 1)
        def _():
            pieces.wait_send(later)
            for g in gathers:
                g.finish()

    col = pl.BlockSpec((S, 128), lambda h: (0, h))
    act = jax.ShapeDtypeStruct((S, D), F32)
    gathered = (w_all, w3_all, cw_all)
    return pl.pallas_call(
        body, name="attn_fwd", grid=(hpr,),
        in_specs=[SMEM_SPEC, VMEM_SPEC, ANY_SPEC, ANY_SPEC, ANY_SPEC],
        out_specs=(col, pl.BlockSpec((2, S, 128), lambda h: (0, 0, h)), col, col, col, ANY_SPEC, ANY_SPEC, ANY_SPEC),
        out_shape=(act, jax.ShapeDtypeStruct((2, S, D), F32), act, act, act,
                   *[jax.ShapeDtypeStruct(t.shape, t.dtype) for t in gathered]),
        scratch_shapes=([pltpu.VMEM((S, 128), F32), pltpu.VMEM((2, S, 128), F32), pltpu.VMEM((2, S, 128), F32),
                         pltpu.VMEM((D, 3 * 128), BF), pltpu.VMEM((3, PROJ_ROWS, 128), F32),
                         pltpu.SemaphoreType.DMA((3,))]
                        + _piece_sems(3 * hpr) + WEIGHT_GATHER_SEMS * 3),
        input_output_aliases={2: 5, 3: 6, 4: 7},
        compiler_params=_params(1),
    )(slopes, u, *gathered)


def _set_rows(shape, rows):
    idx = lax.broadcasted_iota(jnp.int32, shape, 0)
    out = jnp.zeros(shape, F32)
    for r, val in rows.items():
        out = out + jnp.where(idx == r, val, 0.0)
    return out


def _mid(yc_in, pa_mid, o, x2, target, b_merge, final_g, w3):
    S = x2.shape[0]
    tm = ROW_TILE
    nsteps = S // tm
    tile = pl.BlockSpec((tm, D), lambda i: (i, 0))

    def body(yc_ref, za_ref, gcp_ref, gap_ref, o_ref, x_ref, t_ref, b_ref, fg_ref, w_ref,
             dh_ref, dmid_ref, do_ref, dyc_ref, gw_ref, small_ref, acc, stage):
        i = pl.program_id(0)

        @pl.when(i == 0)
        def _():
            acc[...] = jnp.zeros_like(acc)
            small_ref[...] = jnp.zeros_like(small_ref)

        wc, wa, wo = w_ref[0], w_ref[1], w_ref[2]
        z = za_ref[...].astype(F32)
        sg = _sigmoid(z)
        ov = o_ref[...]
        yc_in_b, ya_in_b = yc_ref[...], (z * sg * ov).astype(BF)
        yc = _dot(yc_in_b, wc)
        ya = _dot(ya_in_b, wa)
        b = b_ref[...]
        gc = _sigmoid(gcp_ref[...].astype(F32) + b[:, :D])
        ga = _sigmoid(gap_ref[...].astype(F32) + b[:, D:])
        merged = gc * yc + ga * ya
        merged_b = merged.astype(BF)
        h = x_ref[...] + _dot(merged_b, wo)
        r2 = lax.rsqrt(jnp.mean(h * h, axis=-1, keepdims=True) + EPS)
        n = h * r2
        fg = fg_ref[...]
        err = n * fg - t_ref[...]
        loss = 0.5 * jnp.sum(jnp.sum(err * err, axis=-1, keepdims=True) / D, axis=0, keepdims=True)
        dy = err / D
        g_fg = jnp.sum(dy * n, axis=0, keepdims=True)
        dn = dy * fg
        dh = r2 * (dn - n * jnp.mean(dn * n, axis=-1, keepdims=True))
        dh_ref[...] = dh
        dh_b = dh.astype(BF)
        dmerged = _dot_nt(dh_b, wo)
        acc[2] += _dot(merged.T.astype(BF), dh_b)
        dyc = (dmerged * gc).astype(BF)
        dya = (dmerged * ga).astype(BF)
        dgcp = dmerged * yc * gc * (1.0 - gc)
        dgap = dmerged * ya * ga * (1.0 - ga)
        dmid_ref[1] = dgcp.astype(BF)
        dmid_ref[2] = dgap.astype(BF)
        acc[0] += _dot(yc_in_b.astype(F32).T.astype(BF), dyc)
        acc[1] += _dot(ya_in_b.astype(F32).T.astype(BF), dya)
        dyc_ref[...] = _dot_nt(dyc, wc).astype(BF)
        dya_in = _dot_nt(dya, wa)
        do_ref[...] = dya_in * (z * sg)
        dmid_ref[0] = (dya_in * ov * (sg * (1.0 + z * (1.0 - sg)))).astype(BF)
        small_ref[...] += _set_rows((8, D), {
            1: jnp.sum(dgcp, axis=0, keepdims=True), 2: jnp.sum(dgap, axis=0, keepdims=True),
            3: g_fg, 7: jnp.broadcast_to(loss, (1, D))})

        @pl.when(i == nsteps - 1)
        def _():
            for p in range(N_DEV):
                for a in range(3):
                    stage[...] = acc[a, p * ROW_SHARD:(p + 1) * ROW_SHARD, :].astype(BF)
                    pltpu.sync_copy(stage, gw_ref.at[p, a])

    return pl.pallas_call(
        body, name="mid", grid=(nsteps,),
        in_specs=[tile, pl.BlockSpec((tm, D), lambda i: (i, 0)), pl.BlockSpec((tm, D), lambda i: (i, 1)),
                  pl.BlockSpec((tm, D), lambda i: (i, 2)), tile, tile, tile,
                  pl.BlockSpec((1, 2 * D), lambda i: (0, 0)), pl.BlockSpec((1, D), lambda i: (0, 0)), VMEM_SPEC],
        out_specs=(tile, pl.BlockSpec((3, tm, D), lambda i: (0, i, 0)), tile, tile,
                   ANY_SPEC, pl.BlockSpec((8, D), lambda i: (0, 0))),
        out_shape=(jax.ShapeDtypeStruct((S, D), F32), jax.ShapeDtypeStruct((3, S, D), BF),
                   jax.ShapeDtypeStruct((S, D), F32), jax.ShapeDtypeStruct((S, D), BF),
                   jax.ShapeDtypeStruct((N_DEV, 3, ROW_SHARD, D), BF), jax.ShapeDtypeStruct((8, D), F32)),
        scratch_shapes=[pltpu.VMEM((3, D, D), F32), pltpu.VMEM((ROW_SHARD, D), BF)],
        compiler_params=_params(1),
    )(yc_in, pa_mid, pa_mid, pa_mid, o, x2, target, b_merge, final_g, w3)


def _conv_bwd(dyc_in, pa, cw8):
    S = pa.shape[0]
    tm, tc = CONV_TM, CONV_TC
    nct = D // tc
    nrt = S // tm
    last_halo = S // HALO - 1

    def seg(s):
        return pl.BlockSpec((tm, tc), lambda j, i, s=s: (i, s * nct + j))

    def halo_before(s):
        return pl.BlockSpec((HALO, tc), lambda j, i, s=s: (jnp.maximum(i * (tm // HALO) - 1, 0), s * nct + j))

    def halo_after(s):
        return pl.BlockSpec((HALO, tc), lambda j, i, s=s: (jnp.minimum((i + 1) * (tm // HALO), last_halo), s * nct + j))

    def body(dy, xc, bg, cg, zc, xch, cgh, dyn, bgn, zcn, cw, dout, gcw):
        i = pl.program_id(1)

        @pl.when(i == 0)
        def _():
            gcw[...] = jnp.zeros_like(gcw)

        xcv, cgv = xc[...].astype(F32), cg[...].astype(F32)
        a = cgv * xcv
        ah = jnp.where(i > 0, cgh[...].astype(F32) * xch[...].astype(F32), 0.0)
        row = lax.broadcasted_iota(jnp.int32, (tm, tc), 0)
        a1 = jnp.where(row == 0, ah[HALO - 1:HALO, :], pltpu.roll(a, 1, 0))
        a2 = jnp.where(row == 0, ah[HALO - 2:HALO - 1, :],
                       jnp.where(row == 1, ah[HALO - 1:HALO, :], pltpu.roll(a, 2, 0)))
        w = cw[...]
        conv = w[0:1, :] * a2 + w[1:2, :] * a1 + w[2:3, :] * a
        z = zc[...].astype(F32)
        sg = _sigmoid(z)
        silu = z * sg
        bgv = bg[...].astype(F32)
        dyv = dy[...].astype(F32)
        dout[3] = (dyv * bgv * conv * (sg * (1.0 + z * (1.0 - sg)))).astype(BF)
        dout[1] = (dyv * silu * conv).astype(BF)
        dc = dyv * silu * bgv
        zn = zcn[...].astype(F32)
        dcn = dyn[...].astype(F32) * (zn * _sigmoid(zn)) * bgn[...].astype(F32)
        dcn = jnp.where(i < nrt - 1, dcn, 0.0)
        dc1 = jnp.where(row == tm - 1, dcn[0:1, :], pltpu.roll(dc, tm - 1, 0))
        dc2 = jnp.where(row == tm - 1, dcn[1:2, :],
                        jnp.where(row == tm - 2, dcn[0:1, :], pltpu.roll(dc, tm - 2, 0)))
        da = w[2:3, :] * dc + w[1:2, :] * dc1 + w[0:1, :] * dc2
        dout[2] = (da * xcv).astype(BF)
        dout[0] = (da * cgv).astype(BF)
        gcw[...] += _set_rows((8, tc), {
            4: jnp.sum(dc * a2, axis=0, keepdims=True), 5: jnp.sum(dc * a1, axis=0, keepdims=True),
            6: jnp.sum(dc * a, axis=0, keepdims=True)})

    return pl.pallas_call(
        body, name="conv_bwd", grid=(nct, nrt),
        in_specs=[pl.BlockSpec((tm, tc), lambda j, i: (i, j)), seg(0), seg(1), seg(2), seg(3),
                  halo_before(0), halo_before(2),
                  pl.BlockSpec((HALO, tc), lambda j, i: (jnp.minimum((i + 1) * (tm // HALO), last_halo), j)),
                  halo_after(1), halo_after(3), pl.BlockSpec((8, tc), lambda j, i: (0, j))],
        out_specs=(pl.BlockSpec((4, tm, tc), lambda j, i: (0, i, j)), pl.BlockSpec((8, tc), lambda j, i: (0, j))),
        out_shape=(jax.ShapeDtypeStruct((4, S, D), BF), jax.ShapeDtypeStruct((8, D), F32)),
        compiler_params=_params(2),
    )(dyc_in, pa, pa, pa, pa, pa, pa, dyc_in, pa, pa, cw8)


def _attn_bwd(q, k, v, slopes, do, o, lse, g_in, g_3):
    S = q.shape[0]
    hpr = HEAD_PAIRS
    n_blocks = S // QB

    def body(sl_ref, q_ref, k_ref, v_ref, do_ref, o_ref, lse_ref, gin_ref, g3_ref, out_ref, rin_ref, r3_ref,
             dq_s, dk_s, dv_s, do_s, dd_s, *sems):
        hp = pl.program_id(0)
        exchanges = (_GradExchange(gin_ref, rin_ref, *sems[:3], _shard_cols((0, SEG0_ATTN * D), (SEG0_MID * D, IN_COLS))),
                     _GradExchange(g3_ref, r3_ref, *sems[3:], _whole))

        @pl.when(hp == 0)
        def _():
            for ex in exchanges:
                ex.start()

        head_sum = _head_sum_matrix()
        dq_s[...] = jnp.zeros(dq_s.shape, F32)
        dk_s[...] = jnp.zeros(dk_s.shape, F32)
        dv_s[...] = jnp.zeros(dv_s.shape, F32)

        def row_dots(i, carry):
            rows = pl.ds(pl.multiple_of(i * QB, QB), QB)
            natural = _natural_rows(i, S)
            do_c = do_ref[natural, :]
            do_s[rows, :] = do_c
            dd = _dot(_hi_lo(do_c * o_ref[natural, :]), head_sum)
            dd_s[0, rows, :] = dd[:, :QB]
            dd_s[1, rows, :] = dd[:, QB:]
            return carry

        lax.fori_loop(0, n_blocks, row_dots, 0)

        for d in DILATIONS:
            tri_le, dist, low = _fold_masks(d)
            low_b = low.astype(F32).astype(BF)
            high_b = 1.0 - low_b
            slope = [sl_ref[2 * hp + a] * float(d) for a in range(2)]
            bias = [slope[a] * dist for a in range(2)]

            def block(b, d=d, slope=slope, bias=bias, tri_le=tri_le, low=low, low_b=low_b, high_b=high_b):
                n, cur, prev = _block_rows(b, d, S)
                has_prev = n > 0
                valid = jnp.logical_or(tri_le, has_prev)
                q2f = cur.get(q_ref) * 0.125
                q2 = q2f.astype(BF)
                qs = jnp.concatenate([q2 * low_b, q2 * high_b], axis=0)
                kp, vp = prev.get(k_ref), prev.get(v_ref)
                kp_b, vp_b = kp.astype(BF), vp.astype(BF)
                kcat = jnp.concatenate([kp_b, cur.get(k_ref).astype(BF)], axis=0)
                vcat = jnp.concatenate([vp_b, cur.get(v_ref).astype(BF)], axis=0)
                do2f = cur.get(do_s)
                do2 = do2f.astype(BF)
                dos = jnp.concatenate([do2 * low_b, do2 * high_b], axis=0)
                s2 = _dot_nt(qs, kcat)
                dp2 = _dot_nt(dos, vcat)
                diag2 = _dot(jnp.concatenate([_hi_lo(q2.astype(F32) * kp_b.astype(F32)),
                                              _hi_lo(do2.astype(F32) * vp_b.astype(F32))], axis=0), head_sum)
                p_rows, ds_rows, pe_h, dse_h = [], [], [], []
                for a in range(2):
                    hs = slice(a * QB, (a + 1) * QB)
                    sp, sc = s2[hs, :QB], s2[hs, QB:]
                    dpp, dpc = dp2[hs, :QB], dp2[hs, QB:]
                    lse_a, dd_a = cur.get(lse_ref.at[a]), cur.get(dd_s.at[a])
                    comb = jnp.where(tri_le, sc, sp) - bias[a]
                    e = diag2[:QB, hs] - slope[a] * float(QB)
                    p = jnp.where(valid, jnp.exp(comb - lse_a), 0.0)
                    pe = jnp.where(has_prev, jnp.exp(e - lse_a), 0.0)
                    ds = p * (jnp.where(tri_le, dpc, dpp) - dd_a)
                    dse_h.append(pe * (diag2[QB:, hs] - dd_a))
                    pe_h.append(pe)
                    p_rows.append(jnp.concatenate([jnp.where(tri_le, 0.0, p).astype(BF),
                                                   jnp.where(tri_le, p, 0.0).astype(BF)], axis=1))
                    ds_rows.append(jnp.concatenate([jnp.where(tri_le, 0.0, ds).astype(BF),
                                                    jnp.where(tri_le, ds, 0.0).astype(BF)], axis=1))
                pst = jnp.concatenate(p_rows, axis=0)
                dst = jnp.concatenate(ds_rows, axis=0)
                pe2 = jnp.where(low, pe_h[0], pe_h[1])
                dse2 = jnp.where(low, dse_h[0], dse_h[1])
                dq = _dot(dst, kcat)
                cur.add(dq_s, (jnp.where(low, dq[:QB], dq[QB:]) + dse2 * kp) * 0.125)
                dk = _dot_tn(dst, qs)
                dv = _dot_tn(pst, dos)
                prev.add(dk_s, dk[:QB] + dse2 * q2f)
                cur.add(dk_s, dk[QB:])
                prev.add(dv_s, dv[:QB] + pe2 * do2f)
                cur.add(dv_s, dv[QB:])

            def several(it, carry, block=block):
                for u in range(ATT_UNROLL):
                    block(it * ATT_UNROLL + u)
                return carry

            lax.fori_loop(0, n_blocks // ATT_UNROLL, several, 0)

        def finish(i, carry):
            rows = pl.ds(pl.multiple_of(i * QB, QB), QB)
            natural = _natural_rows(i, S)
            for t, ref in enumerate((dq_s, dk_s, dv_s)):
                out_ref.at[t][natural, :] = ref[rows, :]
            return carry

        lax.fori_loop(0, n_blocks, finish, 0)

        @pl.when(hp == hpr - 1)
        def _():
            for ex in exchanges:
                ex.finish()

    col = pl.BlockSpec((S, 128), lambda h: (0, h))
    return pl.pallas_call(
        body, name="attn_bwd", grid=(hpr,),
        in_specs=[SMEM_SPEC, col, col, col, col, col, pl.BlockSpec((2, S, 128), lambda h: (0, 0, h)),
                  ANY_SPEC, ANY_SPEC],
        out_specs=(pl.BlockSpec((3, S, 128), lambda h: (0, 0, h)), ANY_SPEC, ANY_SPEC),
        out_shape=(jax.ShapeDtypeStruct((3, S, D), F32), jax.ShapeDtypeStruct(g_in.shape, BF),
                   jax.ShapeDtypeStruct(g_3.shape, BF)),
        scratch_shapes=([pltpu.VMEM((S, 128), F32)] * 4 + [pltpu.VMEM((2, S, 128), F32)]
                        + GRAD_EXCHANGE_SEMS + GRAD_EXCHANGE_SEMS),
        compiler_params=_params(1),
    )(slopes, q, k, v, do, o, lse, g_in, g_3)


WG_TN = 256
SEG0_CONV, SEG0_ATTN, SEG0_MID = 0, 4, 7


def _wgrad_in(ut, d_group, seg0, g_in, name):
    S = ut.shape[1]
    tn = WG_TN
    per_seg = D // tn
    per_shard = W_IN_SHARD // tn
    n_tiles = d_group.shape[0] * per_seg
    tile0 = seg0 * per_seg

    def body(ut_ref, d_ref, *rest):
        rest[-1][0] = _dot(ut_ref[...], d_ref[0].astype(BF)).astype(BF)

    operands, in_specs, aliases = [ut, d_group], [VMEM_SPEC, pl.BlockSpec((1, S, tn), lambda t: (t // per_seg, 0, t % per_seg))], {}
    if g_in is not None:
        operands.append(g_in)
        in_specs.append(ANY_SPEC)
        aliases = {2: 0}
    return pl.pallas_call(
        body, name=name, grid=(n_tiles,), in_specs=in_specs,
        out_specs=pl.BlockSpec((1, D, tn), lambda t: ((tile0 + t) // per_shard, 0, (tile0 + t) % per_shard)),
        out_shape=jax.ShapeDtypeStruct((N_DEV, D, W_IN_SHARD), BF),
        input_output_aliases=aliases,
        compiler_params=_params(1),
    )(*operands)


def _dgrad_norm_bwd(d_conv, d_attn, d_mid, w_all, x2, dh, norm_g, g_in, r_in):
    S = x2.shape[0]
    tm = ROW_TILE
    nsteps = S // tm
    tile = pl.BlockSpec((tm, D), lambda i: (i, 0))
    pieces = _proj_pieces()

    def body(a_ref, b_ref, c_ref, w_ref, x_ref, dh_ref, g_ref, gin_ref, rin_in_ref, gx_ref, small_ref, rin_ref, *sems):
        i = pl.program_id(0)
        exchange = _GradExchange(gin_ref, rin_ref, *sems, _shard_cols((SEG0_ATTN * D, SEG0_MID * D)))

        @pl.when(i == 0)
        def _():
            small_ref[...] = jnp.zeros_like(small_ref)
            exchange.start()

        groups = (a_ref, b_ref, c_ref)
        du = jnp.zeros((tm, D), F32)
        for s, sc, p, pc, width in pieces:
            g = 0 if s < 4 else (1 if s < 7 else 2)
            local = s - (0, 4, 7)[g]
            du = du + _dot_nt(groups[g][local, :, sc:sc + width].astype(BF), w_ref[p, :, pc:pc + width])
        xv = x_ref[...]
        r = lax.rsqrt(jnp.mean(xv * xv, axis=-1, keepdims=True) + EPS)
        n = xv * r
        dn = du * g_ref[...]
        gx_ref[...] = dh_ref[...] + r * (dn - n * jnp.mean(dn * n, axis=-1, keepdims=True))
        small_ref[...] += _set_rows((8, D), {0: jnp.sum(du * n, axis=0, keepdims=True)})

        @pl.when(i == nsteps - 1)
        def _():
            exchange.finish()

    return pl.pallas_call(
        body, name="dgrad_norm_bwd", grid=(nsteps,),
        in_specs=[pl.BlockSpec((4, tm, D), lambda i: (0, i, 0)), pl.BlockSpec((3, tm, D), lambda i: (0, i, 0)),
                  pl.BlockSpec((3, tm, D), lambda i: (0, i, 0)), VMEM_SPEC, tile, tile,
                  pl.BlockSpec((1, D), lambda i: (0, 0)), ANY_SPEC, ANY_SPEC],
        out_specs=(tile, pl.BlockSpec((8, D), lambda i: (0, 0)), ANY_SPEC),
        out_shape=(jax.ShapeDtypeStruct((S, D), F32), jax.ShapeDtypeStruct((8, D), F32),
                   jax.ShapeDtypeStruct(r_in.shape, BF)),
        scratch_shapes=GRAD_EXCHANGE_SEMS,
        input_output_aliases={8: 2},
        compiler_params=_params(1),
    )(d_conv, d_attn, d_mid, w_all, x2, dh, norm_g, g_in, r_in)


def _adamw_math(w, g, m, v):
    m = ADAM_B1 * m + (1.0 - ADAM_B1) * g
    v = ADAM_B2 * v + (1.0 - ADAM_B2) * (g * g)
    m_hat = m / (1.0 - ADAM_B1 ** ADAM_STEP)
    v_hat = v / (1.0 - ADAM_B2 ** ADAM_STEP)
    delta = -ADAM_LR * (m_hat / (jnp.sqrt(v_hat) + ADAM_EPS) + ADAM_WD * w)
    return delta, m, v


def _sum_adamw(parts, w, m, v, tm, name):
    R, C = w.shape
    tile = pl.BlockSpec((tm, C), lambda i: (i, 0))

    def body(p_ref, w_ref, m_ref, v_ref, g_out, d_out, m_out, v_out):
        g = p_ref[0].astype(F32)
        for s in range(1, N_DEV):
            g = g + p_ref[s].astype(F32)
        g_out[...] = g
        d_out[...], m_out[...], v_out[...] = _adamw_math(w_ref[...], g, m_ref[...], v_ref[...])

    shape = jax.ShapeDtypeStruct((R, C), F32)
    return pl.pallas_call(
        body, name=name, grid=(R // tm,),
        in_specs=[pl.BlockSpec((N_DEV, tm, C), lambda i: (0, i, 0)), tile, tile, tile],
        out_specs=(tile, tile, tile, tile), out_shape=(shape, shape, shape, shape),
        compiler_params=_params(1),
    )(parts, w, m, v)


def _adamw(g, w, m, v, name):
    def body(g_ref, w_ref, m_ref, v_ref, d_out, m_out, v_out):
        d_out[...], m_out[...], v_out[...] = _adamw_math(w_ref[...], g_ref[...], m_ref[...], v_ref[...])

    shape = jax.ShapeDtypeStruct(w.shape, F32)
    return pl.pallas_call(
        body, name=name, in_specs=[VMEM_SPEC] * 4, out_specs=(VMEM_SPEC,) * 3, out_shape=(shape, shape, shape),
    )(g, w, m, v)


def _alibi_slopes():
    return jnp.exp2(-8.0 * jnp.arange(1, N_HEADS + 1, dtype=F32) / N_HEADS)


def _local_step(x2, target, norm_g, b_merge, final_g, w_all, w3_all, cw_all):
    slopes = _alibi_slopes()
    u, ut = _norm(x2, norm_g)
    o, lse, q, k, v, w_all, w3_all, cw_all = _attn_fwd(u, slopes, w_all, w3_all, cw_all)
    w3 = jnp.transpose(w3_all, (1, 0, 2, 3)).reshape(3, D, D)
    cw8 = jnp.transpose(cw_all, (1, 0, 2)).reshape(8, D)
    pa = _proj_cols(u, w_all, SEG0_CONV, 4, BF, "proj_conv")
    yc_in = _conv_fwd(pa, cw8)
    pa_mid = _proj_cols(u, w_all, SEG0_MID, 3, BF, "proj_mid")
    dh, d_mid, do, dyc_in, g_3, small_mid = _mid(yc_in, pa_mid, o, x2, target, b_merge, final_g, w3)
    g_in = _wgrad_in(ut, d_mid, SEG0_MID, None, "wgrad_in_mid")
    d_conv, small_conv = _conv_bwd(dyc_in, pa, cw8)
    g_in = _wgrad_in(ut, d_conv, SEG0_CONV, g_in, "wgrad_in_conv")
    d_attn, r_in, r_3 = _attn_bwd(q, k, v, slopes, do, o, lse, g_in, g_3)
    g_in = _wgrad_in(ut, d_attn, SEG0_ATTN, g_in, "wgrad_in_attn")
    grad_x, small_norm, r_in = _dgrad_norm_bwd(d_conv, d_attn, d_mid, w_all, x2, dh, norm_g, g_in, r_in)
    return grad_x, r_in, r_3, small_mid, small_conv, small_norm


def kernel(x, norm_g, w_in, b_merge, conv_w, w_out_conv, w_out_attn, w_o, final_g, loss_target, m_norm_g, m_w_in, m_b_merge, m_conv_w, m_w_out_conv, m_w_out_attn, m_w_o, m_final_g, v_norm_g, v_w_in, v_b_merge, v_conv_w, v_w_out_conv, v_w_out_attn, v_w_o, v_final_g):
    me = 4 * lax.axis_index("x") + 2 * lax.axis_index("y") + lax.axis_index("c")
    stack3 = lambda a, b, c: jnp.concatenate([a, b, c], axis=0)
    pad8 = lambda a: jnp.pad(a, ((0, 8 - a.shape[0]), (0, 0)))

    w3_shard = stack3(w_out_conv, w_out_attn, w_o)
    w_all, w3_all, cw_all = _gather_first_weights(w_in[0], w3_shard, pad8(conv_w[0]))

    final_g2 = final_g.reshape(1, D)
    grad_x, r_in, r_3, small_mid, small_conv, small_norm = _local_step(
        x[0], loss_target[0], norm_g, b_merge, final_g2, w_all, w3_all, cw_all)

    small = _allreduce_small(small_mid, small_conv, small_norm)

    g_w_in, d_w_in, nm_w_in, nv_w_in = _sum_adamw(r_in, w_in[0], m_w_in[0], v_w_in[0], 128, "adamw_w_in")
    g_w3, d_w3, nm_w3, nv_w3 = _sum_adamw(
        r_3.reshape(N_DEV, 3 * ROW_SHARD, D), w3_shard.reshape(3 * ROW_SHARD, D),
        stack3(m_w_out_conv, m_w_out_attn, m_w_o).reshape(3 * ROW_SHARD, D),
        stack3(v_w_out_conv, v_w_out_attn, v_w_o).reshape(3 * ROW_SHARD, D), ROW_SHARD, "adamw_w3")

    def pack(ng, bm, fg):
        return pad8(jnp.concatenate([ng, bm.reshape(2, D), fg.reshape(1, D)], axis=0))

    d_s, nm_s, nv_s = _adamw(small, pack(norm_g, b_merge, final_g), pack(m_norm_g, m_b_merge, m_final_g),
                             pack(v_norm_g, v_b_merge, v_final_g), "adamw_small")
    g_cw = lax.dynamic_slice(small, (4, me * ROW_SHARD), (3, ROW_SHARD))
    d_cw, nm_cw, nv_cw = _adamw(g_cw, conv_w[0], m_conv_w[0], v_conv_w[0], "adamw_conv_w")

    loss = small[7, 0]
    split3 = lambda t: tuple(t[a * ROW_SHARD:(a + 1) * ROW_SHARD][None] for a in range(3))
    unpack = lambda t: (t[0:1], t[1:3].reshape(1, 2 * D), t[3])

    def leaves(in_, small_, cw_, w3_):
        ng, bm, fg = unpack(small_)
        wc, wa, wo = split3(w3_)
        return (ng, in_[None], bm, cw_[None], wc, wa, wo, fg)

    return (loss, grad_x[None],
            *leaves(g_w_in, small, g_cw, g_w3),
            *leaves(d_w_in, d_s, d_cw, d_w3),
            *leaves(nm_w_in, nm_s, nm_cw, nm_w3),
            *leaves(nv_w_in, nv_s, nv_cw, nv_w3))
```

```python
import functools

import jax
import jax.numpy as jnp
from jax import lax
from jax.experimental import pallas as pl
from jax.experimental.pallas import tpu as pltpu

D = 1024
N_HEADS = 16
HEAD_DIM = 64
N_SEG = 10
IN_COLS = N_SEG * D
N_DEV = 8
W_IN_SHARD = IN_COLS // N_DEV
ROW_SHARD = D // N_DEV
QB = 128
DILATIONS = (1, 4, 16)
EPS = 1e-6
NEG = -1e30
BF = jnp.bfloat16
F32 = jnp.float32
MESH = pl.DeviceIdType.MESH

ADAM_LR = 0.001
ADAM_B1 = 0.9
ADAM_B2 = 0.999
ADAM_EPS = 1e-08
ADAM_WD = 0.01
ADAM_STEP = 10

V7X_VMEM_BYTES = 64 * 1024 * 1024
VMEM_LIMIT = V7X_VMEM_BYTES - 8 * 1024 * 1024
ROW_TILE = 256

VMEM_SPEC = pl.BlockSpec(memory_space=pltpu.VMEM)
ANY_SPEC = pl.BlockSpec(memory_space=pl.ANY)
SMEM_SPEC = pl.BlockSpec(memory_space=pltpu.SMEM)


def _params(n_grid_axes, vmem=VMEM_LIMIT):
    return pltpu.CompilerParams(dimension_semantics=("arbitrary",) * n_grid_axes, vmem_limit_bytes=vmem)


def _dot(a, b):
    return jnp.dot(a, b, preferred_element_type=F32)


def _dot_nt(a, b):
    return lax.dot_general(a, b, (((1,), (1,)), ((), ())), preferred_element_type=F32)


def _dot_tn(a, b):
    return lax.dot_general(a, b, (((0,), (0,)), ((), ())), preferred_element_type=F32)


def _sigmoid(z):
    return 1.0 / (1.0 + jnp.exp(-z))


def _my_place():
    x, y, c = lax.axis_index("x"), lax.axis_index("y"), lax.axis_index("c")
    return x, y, c, 4 * x + 2 * y + c


def _peers(x, y, c):
    out = []
    for k in range(1, N_DEV):
        px = 1 - x if k & 4 else x
        py = 1 - y if k & 2 else y
        pc = 1 - c if k & 1 else c
        out.append(((px, py, pc), 4 * px + 2 * py + pc))
    return out


def _device(p):
    return (p >> 2, (p >> 1) & 1, p & 1)


def _shard_cols(*ranges):
    def cols(p):
        found = None
        for lo, hi in ranges:
            a, b = max(lo, p * W_IN_SHARD), min(hi, (p + 1) * W_IN_SHARD)
            if a < b:
                assert found is None
                found = (a - p * W_IN_SHARD, b - p * W_IN_SHARD)
        return found

    return cols


def _whole(p):
    return ()


def _block(ref, idx, cols):
    return ref.at[idx] if cols == () else ref.at[idx, :, cols[0]:cols[1]]


class _WeightGather:
    def __init__(self, src, dst, send_sems, recv_sems, cols):
        self.src, self.dst, self.cols = src, dst, cols
        self.send_sems, self.recv_sems = send_sems, recv_sems
        self.me = _my_place()[3]

    def _copy(self, p, target):
        cols = self.cols(p)
        return pltpu.make_async_remote_copy(
            src_ref=self.src(p, cols), dst_ref=_block(self.dst, p, cols), send_sem=self.send_sems.at[target],
            recv_sem=self.recv_sems.at[p], device_id=_device(target), device_id_type=MESH)

    def _each(self, send, receive):
        for p in range(N_DEV):
            if self.cols(p) is None:
                continue

            def sender(p=p):
                for k in range(1, N_DEV):
                    send(self._copy(p, (p + k) % N_DEV))

            pl.when(self.me == p)(sender)
            pl.when(self.me != p)(lambda p=p: receive(self._copy(p, p)))

    def start(self):
        self._each(lambda cp: cp.start(), lambda cp: None)

    def finish(self):
        self._each(lambda cp: cp.wait_send(), lambda cp: cp.wait_recv())


WEIGHT_GATHER_SEMS = [pltpu.SemaphoreType.DMA((N_DEV,)), pltpu.SemaphoreType.DMA((N_DEV,))]
REST_COLS = _shard_cols((0, 4 * D), (7 * D, IN_COLS))
HEAD_PAIRS = D // 128


def _qkv_piece(h, seg):
    col = (4 + seg) * D + 128 * h
    return col // W_IN_SHARD, col % W_IN_SHARD


class _PieceGather:
    def __init__(self, src, dst, send_sems, recv_sems):
        self.src, self.dst, self.send_sems, self.recv_sems = src, dst, send_sems, recv_sems
        self.me = _my_place()[3]

    def _copy(self, i, target):
        p, lo = _qkv_piece(i // 3, i % 3)
        return pltpu.make_async_remote_copy(
            src_ref=self.src(p, lo, lo + 128), dst_ref=self.dst.at[p, :, lo:lo + 128], send_sem=self.send_sems.at[i, target],
            recv_sem=self.recv_sems.at[i], device_id=_device(target), device_id_type=MESH)

    def _owner(self, i, act):
        p = _qkv_piece(i // 3, i % 3)[0]

        def sender():
            for k in range(N_DEV - 1):
                act(self._copy(i, (p + 1 + (k + i) % (N_DEV - 1)) % N_DEV))

        pl.when(self.me == p)(sender)

    def start(self, pieces):
        for i in pieces:
            self._owner(i, lambda cp: cp.start())

    def wait_send(self, pieces):
        for i in pieces:
            self._owner(i, lambda cp: cp.wait_send())

    def wait_recv(self, pieces):
        for i in pieces:
            p = _qkv_piece(i // 3, i % 3)[0]
            pl.when(self.me != p)(lambda i=i, p=p: self._copy(i, p).wait_recv())


def _piece_sems(n):
    return [pltpu.SemaphoreType.DMA((n, N_DEV)), pltpu.SemaphoreType.DMA((n,))]


def _gather_first_weights(w_in, w3, cw):
    def body(w_in_ref, w3_ref, cw_ref, o_in, o_3, o_cw, in_bf, w3_bf, local_sems, *sems):
        me = _my_place()[3]

        def cast_rows(i, carry):
            r = pl.multiple_of(i * 128, 128)
            in_bf[pl.ds(r, 128), :] = w_in_ref[pl.ds(r, 128), :].astype(BF)
            return carry

        lax.fori_loop(0, D // 128, cast_rows, 0)
        for a in range(3):
            w3_bf[a] = w3_ref[a].astype(BF)
        gather = _PieceGather(lambda p, lo, hi: in_bf.at[:, lo:hi], o_in, *sems)
        gather.start(range(3))
        local = [pltpu.make_async_copy(src, dst.at[me], local_sems.at[a])
                 for a, (src, dst) in enumerate(((in_bf, o_in), (w3_bf, o_3), (cw_ref, o_cw)))]
        for cp in local:
            cp.start()
        gather.wait_recv(range(3))
        gather.wait_send(range(3))
        for cp in local:
            cp.wait()

    return pl.pallas_call(
        body, name="gather_first_weights",
        out_shape=(jax.ShapeDtypeStruct((N_DEV, D, W_IN_SHARD), BF),
                   jax.ShapeDtypeStruct((N_DEV, 3, ROW_SHARD, D), BF),
                   jax.ShapeDtypeStruct((N_DEV, 8, 128), F32)),
        in_specs=[VMEM_SPEC, VMEM_SPEC, VMEM_SPEC],
        out_specs=(ANY_SPEC, ANY_SPEC, ANY_SPEC),
        scratch_shapes=[pltpu.VMEM((D, W_IN_SHARD), BF), pltpu.VMEM((3, ROW_SHARD, D), BF),
                        pltpu.SemaphoreType.DMA((3,))] + _piece_sems(3),
        compiler_params=pltpu.CompilerParams(vmem_limit_bytes=VMEM_LIMIT),
    )(w_in, w3, cw)


class _GradExchange:
    def __init__(self, src, dst, send_sems, recv_sems, local_sem, cols):
        self.src, self.dst, self.cols = src, dst, cols
        self.send_sems, self.recv_sems, self.local_sem = send_sems, recv_sems, local_sem
        self.me = _my_place()[3]

    def _remote(self, p, source):
        return pltpu.make_async_remote_copy(
            src_ref=_block(self.src, p, self.cols(p)), dst_ref=_block(self.dst, source, self.cols(p)),
            send_sem=self.send_sems.at[p], recv_sem=self.recv_sems.at[source],
            device_id=_device(p), device_id_type=MESH)

    def _local(self, p):
        return pltpu.make_async_copy(_block(self.src, p, self.cols(p)), _block(self.dst, p, self.cols(p)),
                                     self.local_sem)

    def _as_each_device(self, send, local, receive):
        for m in range(N_DEV):
            def branch(m=m):
                for k in range(1, N_DEV):
                    p = (m + k) % N_DEV
                    if self.cols(p) is not None:
                        send(self._remote(p, m))
                if self.cols(m) is not None:
                    if self.local_sem is not None:
                        local(self._local(m))
                    for k in range(1, N_DEV):
                        receive(self._remote(m, (m + k) % N_DEV))

            pl.when(self.me == m)(branch)

    def start(self):
        self._as_each_device(lambda cp: cp.start(), lambda cp: cp.start(), lambda cp: None)

    def finish(self):
        self._as_each_device(lambda cp: cp.wait_send(), lambda cp: cp.wait(), lambda cp: cp.wait_recv())


GRAD_EXCHANGE_SEMS = [pltpu.SemaphoreType.DMA((N_DEV,)), pltpu.SemaphoreType.DMA((N_DEV,)), pltpu.SemaphoreType.DMA]


def _allreduce_small(p_mid, p_conv, p_norm):
    def body(a_ref, b_ref, c_ref, out_ref, mine, gathered, send_sems, recv_sems):
        x, y, c, me = _my_place()
        mine[...] = a_ref[...] + b_ref[...] + c_ref[...]
        gathered[me] = mine[...]
        remote = []
        for k, (peer, _) in enumerate(_peers(x, y, c)):
            cp = pltpu.make_async_remote_copy(
                src_ref=mine, dst_ref=gathered.at[me], send_sem=send_sems.at[k], recv_sem=recv_sems.at[k],
                device_id=peer, device_id_type=MESH)
            cp.start()
            remote.append(cp)
        for cp in remote:
            cp.wait()
        total = gathered[0]
        for s in range(1, N_DEV):
            total = total + gathered[s]
        out_ref[...] = total

    return pl.pallas_call(
        body, name="allreduce_small",
        out_shape=jax.ShapeDtypeStruct((8, D), F32),
        in_specs=[VMEM_SPEC, VMEM_SPEC, VMEM_SPEC], out_specs=VMEM_SPEC,
        scratch_shapes=[pltpu.VMEM((8, D), F32), pltpu.VMEM((N_DEV, 8, D), F32),
                        pltpu.SemaphoreType.DMA((N_DEV - 1,)), pltpu.SemaphoreType.DMA((N_DEV - 1,))],
    )(p_mid, p_conv, p_norm)


def _proj_pieces():
    cuts = sorted(set(range(0, IN_COLS + 1, D)) | set(range(0, IN_COLS + 1, W_IN_SHARD)))
    return [(lo // D, lo % D, lo // W_IN_SHARD, lo % W_IN_SHARD, hi - lo) for lo, hi in zip(cuts[:-1], cuts[1:])]


def _norm(x2, norm_g):
    S = x2.shape[0]
    tm = ROW_TILE

    def body(x_ref, g_ref, u_ref, ut_ref):
        xv = x_ref[...]
        r = lax.rsqrt(jnp.mean(xv * xv, axis=-1, keepdims=True) + EPS)
        u = xv * r * g_ref[...]
        u_ref[...] = u.astype(BF)
        ut_ref[...] = u.T.astype(BF)

    return pl.pallas_call(
        body, name="norm", grid=(S // tm,),
        in_specs=[pl.BlockSpec((tm, D), lambda i: (i, 0)), pl.BlockSpec((1, D), lambda i: (0, 0))],
        out_specs=(pl.BlockSpec((tm, D), lambda i: (i, 0)), pl.BlockSpec((D, tm), lambda i: (0, i))),
        out_shape=(jax.ShapeDtypeStruct((S, D), BF), jax.ShapeDtypeStruct((D, S), BF)),
        compiler_params=_params(1),
    )(x2, norm_g)


PROJ_TN = 256


def _proj_cols(u, w_all, seg0, n_seg, dtype, name):
    S = u.shape[0]
    tn = PROJ_TN
    per_shard = W_IN_SHARD // tn
    tile0 = seg0 * D // tn

    def body(u_ref, w_ref, out_ref):
        out_ref[...] = _dot(u_ref[...], w_ref[0]).astype(dtype)

    return pl.pallas_call(
        body, name=name, grid=(n_seg * D // tn,),
        in_specs=[VMEM_SPEC, pl.BlockSpec((1, D, tn), lambda t: ((tile0 + t) // per_shard, 0, (tile0 + t) % per_shard))],
        out_specs=pl.BlockSpec((S, tn), lambda t: (0, t)),
        out_shape=jax.ShapeDtypeStruct((S, n_seg * D), dtype),
        compiler_params=_params(1),
    )(u, w_all)


CONV_TM, CONV_TC = 256, 512
HALO = 16


def _conv_fwd(pa, cw8):
    S = pa.shape[0]
    tm, tc = CONV_TM, CONV_TC
    nct = D // tc

    def seg(s):
        return pl.BlockSpec((tm, tc), lambda i, j, s=s: (i, s * nct + j))

    def halo_before(s):
        return pl.BlockSpec((HALO, tc), lambda i, j, s=s: (jnp.maximum(i * (tm // HALO) - 1, 0), s * nct + j))

    def body(xc, bg, cg, zc, xch, cgh, cw, out):
        i = pl.program_id(0)
        a = cg[...].astype(F32) * xc[...].astype(F32)
        ah = cgh[...].astype(F32) * xch[...].astype(F32)
        ah = jnp.where(i > 0, ah, 0.0)
        row = lax.broadcasted_iota(jnp.int32, (tm, tc), 0)
        a1 = jnp.where(row == 0, ah[HALO - 1:HALO, :], pltpu.roll(a, 1, 0))
        a2 = jnp.where(row == 0, ah[HALO - 2:HALO - 1, :],
                       jnp.where(row == 1, ah[HALO - 1:HALO, :], pltpu.roll(a, 2, 0)))
        w = cw[...]
        conv = w[0:1, :] * a2 + w[1:2, :] * a1 + w[2:3, :] * a
        z = zc[...].astype(F32)
        out[...] = (z * _sigmoid(z) * bg[...].astype(F32) * conv).astype(BF)

    return pl.pallas_call(
        body, name="conv_fwd", grid=(S // tm, nct),
        in_specs=[seg(0), seg(1), seg(2), seg(3), halo_before(0), halo_before(2),
                  pl.BlockSpec((8, tc), lambda i, j: (0, j))],
        out_specs=pl.BlockSpec((tm, tc), lambda i, j: (i, j)),
        out_shape=jax.ShapeDtypeStruct((S, D), BF),
        compiler_params=_params(2),
    )(pa, pa, pa, pa, pa, pa, cw8)


ATT_UNROLL = 2


LAYOUT_MOD = 4
RUN = QB // LAYOUT_MOD


def _fold_masks(d):
    row = lax.broadcasted_iota(jnp.int32, (QB, QB), 0)
    lane = lax.broadcasted_iota(jnp.int32, (QB, QB), 1)
    if d == 1:
        qpos, kpos = LAYOUT_MOD * (row % RUN) + row // RUN, LAYOUT_MOD * (lane % RUN) + lane // RUN
    else:
        qpos, kpos = row, lane
    tri_le = kpos <= qpos
    dist = jnp.where(tri_le, qpos - kpos, qpos - kpos + QB).astype(F32)
    return tri_le, dist, lane < HEAD_DIM


class _Rows:
    def __init__(self, slices):
        self.slices = slices

    def get(self, ref):
        parts = [ref[sl, :] for sl in self.slices]
        return parts[0] if len(parts) == 1 else jnp.concatenate(parts, axis=0)

    def put(self, ref, val):
        size = QB // len(self.slices)
        for g, sl in enumerate(self.slices):
            ref[sl, :] = val if len(self.slices) == 1 else val[g * size:(g + 1) * size]

    def add(self, ref, val):
        self.put(ref, self.get(ref) + val)


def _block_rows(b, d, S):
    quarter = S // LAYOUT_MOD
    nb = S // (QB * d)
    r, n = b // nb, b % nb
    n_prev = jnp.maximum(n - 1, 0)
    if d == 1:
        runs = lambda m: _Rows([pl.ds(pl.multiple_of(g * quarter + RUN * m, RUN), RUN) for g in range(LAYOUT_MOD)])
        return n, runs(n), runs(n_prev)
    if d == LAYOUT_MOD:
        block = lambda m: _Rows([pl.ds(pl.multiple_of(r * quarter + QB * m, QB), QB)])
        return n, block(n), block(n_prev)
    step = d // LAYOUT_MOD
    first = (r % LAYOUT_MOD) * quarter + r // LAYOUT_MOD
    strided = lambda m: _Rows([pl.ds(first + QB * step * m, QB, stride=step)])
    return n, strided(n), strided(n_prev)


def _natural_rows(i, S):
    per = S // LAYOUT_MOD // QB
    return pl.ds(i // per + LAYOUT_MOD * QB * (i % per), QB, stride=LAYOUT_MOD)


def _head_sum_matrix():
    r = lax.broadcasted_iota(jnp.int32, (2 * QB, 2 * QB), 0)
    c = lax.broadcasted_iota(jnp.int32, (2 * QB, 2 * QB), 1)
    return (((r % QB) // HEAD_DIM) == (c // QB)).astype(F32).astype(BF)


def _hi_lo(t):
    hi = t.astype(BF)
    return jnp.concatenate([hi, (t - hi.astype(F32)).astype(BF)], axis=1)


PROJ_ROWS = 512


def _attn_fwd(u, slopes, w_all, w3_all, cw_all):
    S = u.shape[0]
    hpr = HEAD_PAIRS
    n_blocks = S // QB
    later = range(3, 3 * hpr)

    def body(sl_ref, u_ref, w_in_ref, w3_in_ref, cw_in_ref, o_ref, lse_ref, q_ref, k_ref, v_ref, w_ref, w3_ref,
             cw_ref, acc, m_s, l_s, w_tile, staged, tile_sems, *sems):
        hp = pl.program_id(0)
        me = _my_place()[3]
        pieces = _PieceGather(lambda p, lo, hi: w_ref.at[p, :, lo:hi], w_ref, *sems[0:2])
        gathers = (_WeightGather(lambda p, cols: _block(w_ref, me, cols), w_ref, *sems[2:4], REST_COLS),
                   _WeightGather(lambda p, cols: w3_ref.at[me], w3_ref, *sems[4:6], _whole),
                   _WeightGather(lambda p, cols: cw_ref.at[me], cw_ref, *sems[6:8], _whole))

        @pl.when(hp == 0)
        def _():
            pieces.start(later)
            for g in gathers:
                g.start()

        for h in range(hpr):
            @pl.when(hp == h)
            def _(h=h):
                if h > 0:
                    pieces.wait_recv(range(3 * h, 3 * h + 3))
                fetch = []
                for seg in range(3):
                    p, lo = _qkv_piece(h, seg)
                    fetch.append(pltpu.make_async_copy(w_ref.at[p, :, lo:lo + 128], w_tile.at[:, seg * 128:(seg + 1) * 128],
                                                       tile_sems.at[seg]))
                    fetch[-1].start()
                for cp in fetch:
                    cp.wait()

        def project(i, carry):
            rows = pl.ds(pl.multiple_of(i * PROJ_ROWS, PROJ_ROWS), PROJ_ROWS)
            qkv = _dot(u_ref[rows, :], w_tile[...])
            per = PROJ_ROWS // LAYOUT_MOD
            for seg, ref in enumerate((q_ref, k_ref, v_ref)):
                staged[seg] = qkv[:, seg * 128:(seg + 1) * 128]
                for g in range(LAYOUT_MOD):
                    dst = pl.ds(pl.multiple_of(g * (S // LAYOUT_MOD) + i * per, per), per)
                    ref[dst, :] = staged.at[seg][pl.ds(g, per, stride=LAYOUT_MOD), :]
            return carry

        lax.fori_loop(0, S // PROJ_ROWS, project, 0)

        head_sum = _head_sum_matrix()
        ones_b = jnp.ones((2 * QB, QB), BF)
        m_s[...] = jnp.full(m_s.shape, NEG, F32)
        l_s[...] = jnp.zeros(l_s.shape, F32)
        acc[...] = jnp.zeros(acc.shape, F32)

        for d in DILATIONS:
            tri_le, dist, low = _fold_masks(d)
            low_b = low.astype(F32).astype(BF)
            high_b = 1.0 - low_b
            slope = [sl_ref[2 * hp + a] * float(d) for a in range(2)]
            bias = [slope[a] * dist for a in range(2)]

            def block(b, d=d, slope=slope, bias=bias, tri_le=tri_le, low=low, low_b=low_b, high_b=high_b):
                n, cur, prev = _block_rows(b, d, S)
                has_prev = n > 0
                valid = jnp.logical_or(tri_le, has_prev)
                q2 = (cur.get(q_ref) * 0.125).astype(BF)
                qs = jnp.concatenate([q2 * low_b, q2 * high_b], axis=0)
                vp = prev.get(v_ref)
                kp_b = prev.get(k_ref).astype(BF)
                kcat = jnp.concatenate([kp_b, cur.get(k_ref).astype(BF)], axis=0)
                vcat = jnp.concatenate([vp, cur.get(v_ref)], axis=0).astype(BF)
                s2 = _dot_nt(qs, kcat)
                e2 = _dot(_hi_lo(q2.astype(F32) * kp_b.astype(F32)), head_sum)
                p_rows, alpha_h, pe_h = [], [], []
                for a in range(2):
                    sp, sc = s2[a * QB:(a + 1) * QB, :QB], s2[a * QB:(a + 1) * QB, QB:]
                    comb = jnp.where(valid, jnp.where(tri_le, sc, sp) - bias[a], NEG)
                    e = jnp.where(has_prev, e2[:, a * QB:(a + 1) * QB] - slope[a] * float(QB), NEG)
                    m_old = cur.get(m_s.at[a])
                    m_new = jnp.maximum(jnp.maximum(m_old, jnp.max(comb, axis=-1, keepdims=True)), e)
                    cur.put(m_s.at[a], m_new)
                    p = jnp.exp(comb - m_new)
                    pe_h.append(jnp.exp(e - m_new))
                    alpha_h.append(jnp.exp(m_old - m_new))
                    p_rows.append(jnp.concatenate([jnp.where(tri_le, 0.0, p).astype(BF),
                                                   jnp.where(tri_le, p, 0.0).astype(BF)], axis=1))
                pv = _dot(jnp.concatenate(p_rows, axis=0), jnp.concatenate([vcat, ones_b], axis=1))
                for a in range(2):
                    cur.put(l_s.at[a], alpha_h[a] * cur.get(l_s.at[a]) + pv[a * QB:(a + 1) * QB, QB:] + pe_h[a])
                cur.put(acc, jnp.where(low, alpha_h[0], alpha_h[1]) * cur.get(acc)
                        + jnp.where(low, pv[:QB, :QB], pv[QB:, :QB]) + jnp.where(low, pe_h[0], pe_h[1]) * vp)

            def several(it, carry, block=block):
                for u in range(ATT_UNROLL):
                    block(it * ATT_UNROLL + u)
                return carry

            lax.fori_loop(0, n_blocks // ATT_UNROLL, several, 0)

        low = _fold_masks(LAYOUT_MOD)[2]

        def finish(i, carry):
            rows = pl.ds(pl.multiple_of(i * QB, QB), QB)
            l0, l1 = l_s[0, rows, :], l_s[1, rows, :]
            o_ref[_natural_rows(i, S), :] = acc[rows, :] / jnp.where(low, l0, l1)
            lse_ref[0, rows, :] = m_s[0, rows, :] + jnp.log(l0)
            lse_ref[1, rows, :] = m_s[1, rows, :] + jnp.log(l1)
            return carry

        lax.fori_loop(0, n_blocks, finish, 0)

        @pl.when(hp == hpr - 1)
        def _():
            pieces.wait_send(later)
            for g in gathers:
                g.finish()

    col = pl.BlockSpec((S, 128), lambda h: (0, h))
    act = jax.ShapeDtypeStruct((S, D), F32)
    gathered = (w_all, w3_all, cw_all)
    return pl.pallas_call(
        body, name="attn_fwd", grid=(hpr,),
        in_specs=[SMEM_SPEC, VMEM_SPEC, ANY_SPEC, ANY_SPEC, ANY_SPEC],
        out_specs=(col, pl.BlockSpec((2, S, 128), lambda h: (0, 0, h)), col, col, col, ANY_SPEC, ANY_SPEC, ANY_SPEC),
        out_shape=(act, jax.ShapeDtypeStruct((2, S, D), F32), act, act, act,
                   *[jax.ShapeDtypeStruct(t.shape, t.dtype) for t in gathered]),
        scratch_shapes=([pltpu.VMEM((S, 128), F32), pltpu.VMEM((2, S, 128), F32), pltpu.VMEM((2, S, 128), F32),
                         pltpu.VMEM((D, 3 * 128), BF), pltpu.VMEM((3, PROJ_ROWS, 128), F32),
                         pltpu.SemaphoreType.DMA((3,))]
                        + _piece_sems(3 * hpr) + WEIGHT_GATHER_SEMS * 3),
        input_output_aliases={2: 5, 3: 6, 4: 7},
        compiler_params=_params(1),
    )(slopes, u, *gathered)


def _set_rows(shape, rows):
    idx = lax.broadcasted_iota(jnp.int32, shape, 0)
    out = jnp.zeros(shape, F32)
    for r, val in rows.items():
        out = out + jnp.where(idx == r, val, 0.0)
    return out


def _mid(yc_in, pa_mid, o, x2, target, b_merge, final_g, w3):
    S = x2.shape[0]
    tm = ROW_TILE
    nsteps = S // tm
    tile = pl.BlockSpec((tm, D), lambda i: (i, 0))

    def body(yc_ref, za_ref, gcp_ref, gap_ref, o_ref, x_ref, t_ref, b_ref, fg_ref, w_ref,
             dh_ref, dmid_ref, do_ref, dyc_ref, gw_ref, small_ref, acc, stage):
        i = pl.program_id(0)

        @pl.when(i == 0)
        def _():
            acc[...] = jnp.zeros_like(acc)
            small_ref[...] = jnp.zeros_like(small_ref)

        wc, wa, wo = w_ref[0], w_ref[1], w_ref[2]
        z = za_ref[...].astype(F32)
        sg = _sigmoid(z)
        ov = o_ref[...]
        yc_in_b, ya_in_b = yc_ref[...], (z * sg * ov).astype(BF)
        yc = _dot(yc_in_b, wc)
        ya = _dot(ya_in_b, wa)
        b = b_ref[...]
        gc = _sigmoid(gcp_ref[...].astype(F32) + b[:, :D])
        ga = _sigmoid(gap_ref[...].astype(F32) + b[:, D:])
        merged = gc * yc + ga * ya
        merged_b = merged.astype(BF)
        h = x_ref[...] + _dot(merged_b, wo)
        r2 = lax.rsqrt(jnp.mean(h * h, axis=-1, keepdims=True) + EPS)
        n = h * r2
        fg = fg_ref[...]
        err = n * fg - t_ref[...]
        loss = 0.5 * jnp.sum(jnp.sum(err * err, axis=-1, keepdims=True) / D, axis=0, keepdims=True)
        dy = err / D
        g_fg = jnp.sum(dy * n, axis=0, keepdims=True)
        dn = dy * fg
        dh = r2 * (dn - n * jnp.mean(dn * n, axis=-1, keepdims=True))
        dh_ref[...] = dh
        dh_b = dh.astype(BF)
        dmerged = _dot_nt(dh_b, wo)
        acc[2] += _dot(merged.T.astype(BF), dh_b)
        dyc = (dmerged * gc).astype(BF)
        dya = (dmerged * ga).astype(BF)
        dgcp = dmerged * yc * gc * (1.0 - gc)
        dgap = dmerged * ya * ga * (1.0 - ga)
        dmid_ref[1] = dgcp.astype(BF)
        dmid_ref[2] = dgap.astype(BF)
        acc[0] += _dot(yc_in_b.astype(F32).T.astype(BF), dyc)
        acc[1] += _dot(ya_in_b.astype(F32).T.astype(BF), dya)
        dyc_ref[...] = _dot_nt(dyc, wc).astype(BF)
        dya_in = _dot_nt(dya, wa)
        do_ref[...] = dya_in * (z * sg)
        dmid_ref[0] = (dya_in * ov * (sg * (1.0 + z * (1.0 - sg)))).astype(BF)
        small_ref[...] += _set_rows((8, D), {
            1: jnp.sum(dgcp, axis=0, keepdims=True), 2: jnp.sum(dgap, axis=0, keepdims=True),
            3: g_fg, 7: jnp.broadcast_to(loss, (1, D))})

        @pl.when(i == nsteps - 1)
        def _():
            for p in range(N_DEV):
                for a in range(3):
                    stage[...] = acc[a, p * ROW_SHARD:(p + 1) * ROW_SHARD, :].astype(BF)
                    pltpu.sync_copy(stage, gw_ref.at[p, a])

    return pl.pallas_call(
        body, name="mid", grid=(nsteps,),
        in_specs=[tile, pl.BlockSpec((tm, D), lambda i: (i, 0)), pl.BlockSpec((tm, D), lambda i: (i, 1)),
                  pl.BlockSpec((tm, D), lambda i: (i, 2)), tile, tile, tile,
                  pl.BlockSpec((1, 2 * D), lambda i: (0, 0)), pl.BlockSpec((1, D), lambda i: (0, 0)), VMEM_SPEC],
        out_specs=(tile, pl.BlockSpec((3, tm, D), lambda i: (0, i, 0)), tile, tile,
                   ANY_SPEC, pl.BlockSpec((8, D), lambda i: (0, 0))),
        out_shape=(jax.ShapeDtypeStruct((S, D), F32), jax.ShapeDtypeStruct((3, S, D), BF),
                   jax.ShapeDtypeStruct((S, D), F32), jax.ShapeDtypeStruct((S, D), BF),
                   jax.ShapeDtypeStruct((N_DEV, 3, ROW_SHARD, D), BF), jax.ShapeDtypeStruct((8, D), F32)),
        scratch_shapes=[pltpu.VMEM((3, D, D), F32), pltpu.VMEM((ROW_SHARD, D), BF)],
        compiler_params=_params(1),
    )(yc_in, pa_mid, pa_mid, pa_mid, o, x2, target, b_merge, final_g, w3)


def _conv_bwd(dyc_in, pa, cw8):
    S = pa.shape[0]
    tm, tc = CONV_TM, CONV_TC
    nct = D // tc
    nrt = S // tm
    last_halo = S // HALO - 1

    def seg(s):
        return pl.BlockSpec((tm, tc), lambda j, i, s=s: (i, s * nct + j))

    def halo_before(s):
        return pl.BlockSpec((HALO, tc), lambda j, i, s=s: (jnp.maximum(i * (tm // HALO) - 1, 0), s * nct + j))

    def halo_after(s):
        return pl.BlockSpec((HALO, tc), lambda j, i, s=s: (jnp.minimum((i + 1) * (tm // HALO), last_halo), s * nct + j))

    def body(dy, xc, bg, cg, zc, xch, cgh, dyn, bgn, zcn, cw, dout, gcw):
        i = pl.program_id(1)

        @pl.when(i == 0)
        def _():
            gcw[...] = jnp.zeros_like(gcw)

        xcv, cgv = xc[...].astype(F32), cg[...].astype(F32)
        a = cgv * xcv
        ah = jnp.where(i > 0, cgh[...].astype(F32) * xch[...].astype(F32), 0.0)
        row = lax.broadcasted_iota(jnp.int32, (tm, tc), 0)
        a1 = jnp.where(row == 0, ah[HALO - 1:HALO, :], pltpu.roll(a, 1, 0))
        a2 = jnp.where(row == 0, ah[HALO - 2:HALO - 1, :],
                       jnp.where(row == 1, ah[HALO - 1:HALO, :], pltpu.roll(a, 2, 0)))
        w = cw[...]
        conv = w[0:1, :] * a2 + w[1:2, :] * a1 + w[2:3, :] * a
        z = zc[...].astype(F32)
        sg = _sigmoid(z)
        silu = z * sg
        bgv = bg[...].astype(F32)
        dyv = dy[...].astype(F32)
        dout[3] = (dyv * bgv * conv * (sg * (1.0 + z * (1.0 - sg)))).astype(BF)
        dout[1] = (dyv * silu * conv).astype(BF)
        dc = dyv * silu * bgv
        zn = zcn[...].astype(F32)
        dcn = dyn[...].astype(F32) * (zn * _sigmoid(zn)) * bgn[...].astype(F32)
        dcn = jnp.where(i < nrt - 1, dcn, 0.0)
        dc1 = jnp.where(row == tm - 1, dcn[0:1, :], pltpu.roll(dc, tm - 1, 0))
        dc2 = jnp.where(row == tm - 1, dcn[1:2, :],
                        jnp.where(row == tm - 2, dcn[0:1, :], pltpu.roll(dc, tm - 2, 0)))
        da = w[2:3, :] * dc + w[1:2, :] * dc1 + w[0:1, :] * dc2
        dout[2] = (da * xcv).astype(BF)
        dout[0] = (da * cgv).astype(BF)
        gcw[...] += _set_rows((8, tc), {
            4: jnp.sum(dc * a2, axis=0, keepdims=True), 5: jnp.sum(dc * a1, axis=0, keepdims=True),
            6: jnp.sum(dc * a, axis=0, keepdims=True)})

    return pl.pallas_call(
        body, name="conv_bwd", grid=(nct, nrt),
        in_specs=[pl.BlockSpec((tm, tc), lambda j, i: (i, j)), seg(0), seg(1), seg(2), seg(3),
                  halo_before(0), halo_before(2),
                  pl.BlockSpec((HALO, tc), lambda j, i: (jnp.minimum((i + 1) * (tm // HALO), last_halo), j)),
                  halo_after(1), halo_after(3), pl.BlockSpec((8, tc), lambda j, i: (0, j))],
        out_specs=(pl.BlockSpec((4, tm, tc), lambda j, i: (0, i, j)), pl.BlockSpec((8, tc), lambda j, i: (0, j))),
        out_shape=(jax.ShapeDtypeStruct((4, S, D), BF), jax.ShapeDtypeStruct((8, D), F32)),
        compiler_params=_params(2),
    )(dyc_in, pa, pa, pa, pa, pa, pa, dyc_in, pa, pa, cw8)


def _attn_bwd(q, k, v, slopes, do, o, lse, g_in, g_3):
    S = q.shape[0]
    hpr = HEAD_PAIRS
    n_blocks = S // QB

    def body(sl_ref, q_ref, k_ref, v_ref, do_ref, o_ref, lse_ref, gin_ref, g3_ref, out_ref, rin_ref, r3_ref,
             dq_s, dk_s, dv_s, do_s, dd_s, *sems):
        hp = pl.program_id(0)
        exchanges = (_GradExchange(gin_ref, rin_ref, *sems[:3], _shard_cols((0, SEG0_ATTN * D), (SEG0_MID * D, IN_COLS))),
                     _GradExchange(g3_ref, r3_ref, *sems[3:], _whole))

        @pl.when(hp == 0)
        def _():
            for ex in exchanges:
                ex.start()

        head_sum = _head_sum_matrix()
        dq_s[...] = jnp.zeros(dq_s.shape, F32)
        dk_s[...] = jnp.zeros(dk_s.shape, F32)
        dv_s[...] = jnp.zeros(dv_s.shape, F32)

        def row_dots(i, carry):
            rows = pl.ds(pl.multiple_of(i * QB, QB), QB)
            natural = _natural_rows(i, S)
            do_c = do_ref[natural, :]
            do_s[rows, :] = do_c
            dd = _dot(_hi_lo(do_c * o_ref[natural, :]), head_sum)
            dd_s[0, rows, :] = dd[:, :QB]
            dd_s[1, rows, :] = dd[:, QB:]
            return carry

        lax.fori_loop(0, n_blocks, row_dots, 0)

        for d in DILATIONS:
            tri_le, dist, low = _fold_masks(d)
            low_b = low.astype(F32).astype(BF)
            high_b = 1.0 - low_b
            slope = [sl_ref[2 * hp + a] * float(d) for a in range(2)]
            bias = [slope[a] * dist for a in range(2)]

            def block(b, d=d, slope=slope, bias=bias, tri_le=tri_le, low=low, low_b=low_b, high_b=high_b):
                n, cur, prev = _block_rows(b, d, S)
                has_prev = n > 0
                valid = jnp.logical_or(tri_le, has_prev)
                q2f = cur.get(q_ref) * 0.125
                q2 = q2f.astype(BF)
                qs = jnp.concatenate([q2 * low_b, q2 * high_b], axis=0)
                kp, vp = prev.get(k_ref), prev.get(v_ref)
                kp_b, vp_b = kp.astype(BF), vp.astype(BF)
                kcat = jnp.concatenate([kp_b, cur.get(k_ref).astype(BF)], axis=0)
                vcat = jnp.concatenate([vp_b, cur.get(v_ref).astype(BF)], axis=0)
                do2f = cur.get(do_s)
                do2 = do2f.astype(BF)
                dos = jnp.concatenate([do2 * low_b, do2 * high_b], axis=0)
                s2 = _dot_nt(qs, kcat)
                dp2 = _dot_nt(dos, vcat)
                diag2 = _dot(jnp.concatenate([_hi_lo(q2.astype(F32) * kp_b.astype(F32)),
                                              _hi_lo(do2.astype(F32) * vp_b.astype(F32))], axis=0), head_sum)
                p_rows, ds_rows, pe_h, dse_h = [], [], [], []
                for a in range(2):
                    hs = slice(a * QB, (a + 1) * QB)
                    sp, sc = s2[hs, :QB], s2[hs, QB:]
                    dpp, dpc = dp2[hs, :QB], dp2[hs, QB:]
                    lse_a, dd_a = cur.get(lse_ref.at[a]), cur.get(dd_s.at[a])
                    comb = jnp.where(tri_le, sc, sp) - bias[a]
                    e = diag2[:QB, hs] - slope[a] * float(QB)
                    p = jnp.where(valid, jnp.exp(comb - lse_a), 0.0)
                    pe = jnp.where(has_prev, jnp.exp(e - lse_a), 0.0)
                    ds = p * (jnp.where(tri_le, dpc, dpp) - dd_a)
                    dse_h.append(pe * (diag2[QB:, hs] - dd_a))
                    pe_h.append(pe)
                    p_rows.append(jnp.concatenate([jnp.where(tri_le, 0.0, p).astype(BF),
                                                   jnp.where(tri_le, p, 0.0).astype(BF)], axis=1))
                    ds_rows.append(jnp.concatenate([jnp.where(tri_le, 0.0, ds).astype(BF),
                                                    jnp.where(tri_le, ds, 0.0).astype(BF)], axis=1))
                pst = jnp.concatenate(p_rows, axis=0)
                dst = jnp.concatenate(ds_rows, axis=0)
                pe2 = jnp.where(low, pe_h[0], pe_h[1])
                dse2 = jnp.where(low, dse_h[0], dse_h[1])
                dq = _dot(dst, kcat)
                cur.add(dq_s, (jnp.where(low, dq[:QB], dq[QB:]) + dse2 * kp) * 0.125)
                dk = _dot_tn(dst, qs)
                dv = _dot_tn(pst, dos)
                prev.add(dk_s, dk[:QB] + dse2 * q2f)
                cur.add(dk_s, dk[QB:])
                prev.add(dv_s, dv[:QB] + pe2 * do2f)
                cur.add(dv_s, dv[QB:])

            def several(it, carry, block=block):
                for u in range(ATT_UNROLL):
                    block(it * ATT_UNROLL + u)
                return carry

            lax.fori_loop(0, n_blocks // ATT_UNROLL, several, 0)

        def finish(i, carry):
            rows = pl.ds(pl.multiple_of(i * QB, QB), QB)
            natural = _natural_rows(i, S)
            for t, ref in enumerate((dq_s, dk_s, dv_s)):
                out_ref.at[t][natural, :] = ref[rows, :]
            return carry

        lax.fori_loop(0, n_blocks, finish, 0)

        @pl.when(hp == hpr - 1)
        def _():
            for ex in exchanges:
                ex.finish()

    col = pl.BlockSpec((S, 128), lambda h: (0, h))
    return pl.pallas_call(
        body, name="attn_bwd", grid=(hpr,),
        in_specs=[SMEM_SPEC, col, col, col, col, col, pl.BlockSpec((2, S, 128), lambda h: (0, 0, h)),
                  ANY_SPEC, ANY_SPEC],
        out_specs=(pl.BlockSpec((3, S, 128), lambda h: (0, 0, h)), ANY_SPEC, ANY_SPEC),
        out_shape=(jax.ShapeDtypeStruct((3, S, D), F32), jax.ShapeDtypeStruct(g_in.shape, BF),
                   jax.ShapeDtypeStruct(g_3.shape, BF)),
        scratch_shapes=([pltpu.VMEM((S, 128), F32)] * 4 + [pltpu.VMEM((2, S, 128), F32)]
                        + GRAD_EXCHANGE_SEMS + GRAD_EXCHANGE_SEMS),
        compiler_params=_params(1),
    )(slopes, q, k, v, do, o, lse, g_in, g_3)


WG_TN = 256
SEG0_CONV, SEG0_ATTN, SEG0_MID = 0, 4, 7


def _wgrad_in(ut, d_group, seg0, g_in, name):
    S = ut.shape[1]
    tn = WG_TN
    per_seg = D // tn
    per_shard = W_IN_SHARD // tn
    n_tiles = d_group.shape[0] * per_seg
    tile0 = seg0 * per_seg

    def body(ut_ref, d_ref, *rest):
        rest[-1][0] = _dot(ut_ref[...], d_ref[0].astype(BF)).astype(BF)

    operands, in_specs, aliases = [ut, d_group], [VMEM_SPEC, pl.BlockSpec((1, S, tn), lambda t: (t // per_seg, 0, t % per_seg))], {}
    if g_in is not None:
        operands.append(g_in)
        in_specs.append(ANY_SPEC)
        aliases = {2: 0}
    return pl.pallas_call(
        body, name=name, grid=(n_tiles,), in_specs=in_specs,
        out_specs=pl.BlockSpec((1, D, tn), lambda t: ((tile0 + t) // per_shard, 0, (tile0 + t) % per_shard)),
        out_shape=jax.ShapeDtypeStruct((N_DEV, D, W_IN_SHARD), BF),
        input_output_aliases=aliases,
        compiler_params=_params(1),
    )(*operands)


def _dgrad_norm_bwd(d_conv, d_attn, d_mid, w_all, x2, dh, norm_g):
    S = x2.shape[0]
    tm = ROW_TILE
    nsteps = S // tm
    tile = pl.BlockSpec((tm, D), lambda i: (i, 0))
    pieces = _proj_pieces()

    def body(a_ref, b_ref, c_ref, w_ref, x_ref, dh_ref, g_ref, gx_ref, small_ref):
        i = pl.program_id(0)

        @pl.when(i == 0)
        def _():
            small_ref[...] = jnp.zeros_like(small_ref)

        groups = (a_ref, b_ref, c_ref)
        du = jnp.zeros((tm, D), F32)
        for s, sc, p, pc, width in pieces:
            g = 0 if s < 4 else (1 if s < 7 else 2)
            local = s - (0, 4, 7)[g]
            du = du + _dot_nt(groups[g][local, :, sc:sc + width].astype(BF), w_ref[p, :, pc:pc + width])
        xv = x_ref[...]
        r = lax.rsqrt(jnp.mean(xv * xv, axis=-1, keepdims=True) + EPS)
        n = xv * r
        dn = du * g_ref[...]
        gx_ref[...] = dh_ref[...] + r * (dn - n * jnp.mean(dn * n, axis=-1, keepdims=True))
        small_ref[...] += _set_rows((8, D), {0: jnp.sum(du * n, axis=0, keepdims=True)})

    return pl.pallas_call(
        body, name="dgrad_norm_bwd", grid=(nsteps,),
        in_specs=[pl.BlockSpec((4, tm, D), lambda i: (0, i, 0)), pl.BlockSpec((3, tm, D), lambda i: (0, i, 0)),
                  pl.BlockSpec((3, tm, D), lambda i: (0, i, 0)), VMEM_SPEC, tile, tile,
                  pl.BlockSpec((1, D), lambda i: (0, 0))],
        out_specs=(tile, pl.BlockSpec((8, D), lambda i: (0, 0))),
        out_shape=(jax.ShapeDtypeStruct((S, D), F32), jax.ShapeDtypeStruct((8, D), F32)),
        compiler_params=_params(1),
    )(d_conv, d_attn, d_mid, w_all, x2, dh, norm_g)


HBM_SPEC = pl.BlockSpec(memory_space=pltpu.HBM)
SEM_SPEC = pl.BlockSpec(memory_space=pltpu.SEMAPHORE)
ATTN_COLS = _shard_cols((SEG0_ATTN * D, SEG0_MID * D))


def _attn_cols_exchange_start(g_in, r_in):
    def body(g_ref, r_ref, send_sems, recv_sems, g_thru, r_thru, token):
        _GradExchange(g_ref, r_ref, send_sems, recv_sems, None, ATTN_COLS).start()
        token[...] = jnp.zeros_like(token)

    hbm = pltpu.with_memory_space_constraint
    return pl.pallas_call(
        body, name="attn_cols_exchange_start",
        out_shape=(pltpu.SemaphoreType.DMA((N_DEV,)), pltpu.SemaphoreType.DMA((N_DEV,)),
                   pltpu.HBM(g_in.shape, g_in.dtype), pltpu.HBM(r_in.shape, r_in.dtype),
                   jax.ShapeDtypeStruct((8, 128), F32)),
        in_specs=(HBM_SPEC, HBM_SPEC), out_specs=(SEM_SPEC, SEM_SPEC, HBM_SPEC, HBM_SPEC, VMEM_SPEC),
        input_output_aliases={0: 2, 1: 3},
        compiler_params=pltpu.CompilerParams(has_side_effects=pltpu.SideEffectType.DATAFLOW_SIDE_EFFECTING),
    )(hbm(g_in, pltpu.HBM), hbm(r_in, pltpu.HBM))


def _attn_cols_exchange_wait(send_sems, recv_sems, g_thru, r_thru, after):
    def body(g_ref, r_ref, send_sems, recv_sems, after_ref, g_dead, r_out):
        _GradExchange(g_ref, r_ref, send_sems, recv_sems, None, ATTN_COLS).finish()

    return pl.pallas_call(
        body, name="attn_cols_exchange_wait",
        out_shape=(pltpu.HBM(g_thru.shape, g_thru.dtype), pltpu.HBM(r_thru.shape, r_thru.dtype)),
        in_specs=(HBM_SPEC, HBM_SPEC, SEM_SPEC, SEM_SPEC, ANY_SPEC), out_specs=(HBM_SPEC, HBM_SPEC),
        input_output_aliases={0: 0, 1: 1},
        compiler_params=pltpu.CompilerParams(has_side_effects=pltpu.SideEffectType.DATAFLOW_SIDE_EFFECTING),
    )(g_thru, r_thru, send_sems, recv_sems, after)


def _adamw_math(w, g, m, v):
    m = ADAM_B1 * m + (1.0 - ADAM_B1) * g
    v = ADAM_B2 * v + (1.0 - ADAM_B2) * (g * g)
    m_hat = m / (1.0 - ADAM_B1 ** ADAM_STEP)
    v_hat = v / (1.0 - ADAM_B2 ** ADAM_STEP)
    delta = -ADAM_LR * (m_hat / (jnp.sqrt(v_hat) + ADAM_EPS) + ADAM_WD * w)
    return delta, m, v


def _sum_adamw(parts, w, m, v, tm, name):
    R, C = w.shape
    tile = pl.BlockSpec((tm, C), lambda i: (i, 0))

    def body(p_ref, w_ref, m_ref, v_ref, g_out, d_out, m_out, v_out):
        g = p_ref[0].astype(F32)
        for s in range(1, N_DEV):
            g = g + p_ref[s].astype(F32)
        g_out[...] = g
        d_out[...], m_out[...], v_out[...] = _adamw_math(w_ref[...], g, m_ref[...], v_ref[...])

    shape = jax.ShapeDtypeStruct((R, C), F32)
    return pl.pallas_call(
        body, name=name, grid=(R // tm,),
        in_specs=[pl.BlockSpec((N_DEV, tm, C), lambda i: (0, i, 0)), tile, tile, tile],
        out_specs=(tile, tile, tile, tile), out_shape=(shape, shape, shape, shape),
        compiler_params=_params(1),
    )(parts, w, m, v)


def _adamw(g, w, m, v, name):
    def body(g_ref, w_ref, m_ref, v_ref, d_out, m_out, v_out):
        d_out[...], m_out[...], v_out[...] = _adamw_math(w_ref[...], g_ref[...], m_ref[...], v_ref[...])

    shape = jax.ShapeDtypeStruct(w.shape, F32)
    return pl.pallas_call(
        body, name=name, in_specs=[VMEM_SPEC] * 4, out_specs=(VMEM_SPEC,) * 3, out_shape=(shape, shape, shape),
    )(g, w, m, v)


def _alibi_slopes():
    return jnp.exp2(-8.0 * jnp.arange(1, N_HEADS + 1, dtype=F32) / N_HEADS)


def _local_step(x2, target, norm_g, b_merge, final_g, w_all, w3_all, cw_all):
    slopes = _alibi_slopes()
    u, ut = _norm(x2, norm_g)
    o, lse, q, k, v, w_all, w3_all, cw_all = _attn_fwd(u, slopes, w_all, w3_all, cw_all)
    w3 = jnp.transpose(w3_all, (1, 0, 2, 3)).reshape(3, D, D)
    cw8 = jnp.transpose(cw_all, (1, 0, 2)).reshape(8, D)
    pa = _proj_cols(u, w_all, SEG0_CONV, 4, BF, "proj_conv")
    yc_in = _conv_fwd(pa, cw8)
    pa_mid = _proj_cols(u, w_all, SEG0_MID, 3, BF, "proj_mid")
    dh, d_mid, do, dyc_in, g_3, small_mid = _mid(yc_in, pa_mid, o, x2, target, b_merge, final_g, w3)
    g_in = _wgrad_in(ut, d_mid, SEG0_MID, None, "wgrad_in_mid")
    d_conv, small_conv = _conv_bwd(dyc_in, pa, cw8)
    g_in = _wgrad_in(ut, d_conv, SEG0_CONV, g_in, "wgrad_in_conv")
    d_attn, r_in, r_3 = _attn_bwd(q, k, v, slopes, do, o, lse, g_in, g_3)
    g_in = _wgrad_in(ut, d_attn, SEG0_ATTN, g_in, "wgrad_in_attn")
    *in_flight, token = _attn_cols_exchange_start(g_in, r_in)
    grad_x, small_norm = _dgrad_norm_bwd(d_conv, d_attn, d_mid, w_all, x2, dh, norm_g + token[0:1, 0:1])
    return grad_x, in_flight, r_3, small_mid, small_conv, small_norm


def kernel(x, norm_g, w_in, b_merge, conv_w, w_out_conv, w_out_attn, w_o, final_g, loss_target, m_norm_g, m_w_in, m_b_merge, m_conv_w, m_w_out_conv, m_w_out_attn, m_w_o, m_final_g, v_norm_g, v_w_in, v_b_merge, v_conv_w, v_w_out_conv, v_w_out_attn, v_w_o, v_final_g):
    me = 4 * lax.axis_index("x") + 2 * lax.axis_index("y") + lax.axis_index("c")
    stack3 = lambda a, b, c: jnp.concatenate([a, b, c], axis=0)
    pad8 = lambda a: jnp.pad(a, ((0, 8 - a.shape[0]), (0, 0)))

    w3_shard = stack3(w_out_conv, w_out_attn, w_o)
    w_all, w3_all, cw_all = _gather_first_weights(w_in[0], w3_shard, pad8(conv_w[0]))

    final_g2 = final_g.reshape(1, D)
    grad_x, in_flight, r_3, small_mid, small_conv, small_norm = _local_step(
        x[0], loss_target[0], norm_g, b_merge, final_g2, w_all, w3_all, cw_all)

    small = _allreduce_small(small_mid, small_conv, small_norm)
    g_in, r_in = _attn_cols_exchange_wait(*in_flight, small)
    own = lax.dynamic_index_in_dim(g_in, me, 0, keepdims=True)
    r_in = lax.dynamic_update_slice(r_in, own, (me, 0, 0))

    g_w_in, d_w_in, nm_w_in, nv_w_in = _sum_adamw(r_in, w_in[0], m_w_in[0], v_w_in[0], 128, "adamw_w_in")
    g_w3, d_w3, nm_w3, nv_w3 = _sum_adamw(
        r_3.reshape(N_DEV, 3 * ROW_SHARD, D), w3_shard.reshape(3 * ROW_SHARD, D),
        stack3(m_w_out_conv, m_w_out_attn, m_w_o).reshape(3 * ROW_SHARD, D),
        stack3(v_w_out_conv, v_w_out_attn, v_w_o).reshape(3 * ROW_SHARD, D), ROW_SHARD, "adamw_w3")

    def pack(ng, bm, fg):
        return pad8(jnp.concatenate([ng, bm.reshape(2, D), fg.reshape(1, D)], axis=0))

    d_s, nm_s, nv_s = _adamw(small, pack(norm_g, b_merge, final_g), pack(m_norm_g, m_b_merge, m_final_g),
                             pack(v_norm_g, v_b_merge, v_final_g), "adamw_small")
    g_cw = lax.dynamic_slice(small, (4, me * ROW_SHARD), (3, ROW_SHARD))
    d_cw, nm_cw, nv_cw = _adamw(g_cw, conv_w[0], m_conv_w[0], v_conv_w[0], "adamw_conv_w")

    loss = small[7, 0]
    split3 = lambda t: tuple(t[a * ROW_SHARD:(a + 1) * ROW_SHARD][None] for a in range(3))
    unpack = lambda t: (t[0:1], t[1:3].reshape(1, 2 * D), t[3])

    def leaves(in_, small_, cw_, w3_):
        ng, bm, fg = unpack(small_)
        wc, wa, wo = split3(w3_)
        return (ng, in_[None], bm, cw_[None], wc, wa, wo, fg)

    return (loss, grad_x[None],
            *leaves(g_w_in, small, g_cw, g_w3),
            *leaves(d_w_in, d_s, d_cw, d_w3),
            *leaves(nm_w_in, nm_s, nm_cw, nm_w3),
            *leaves(nv_w_in, nv_s, nv_cw, nv_w3))
```

```python
import functools

import jax
import jax.numpy as jnp
from jax import lax
from jax.experimental import pallas as pl
from jax.experimental.pallas import tpu as pltpu

D = 1024
N_HEADS = 16
HEAD_DIM = 64
N_SEG = 10
IN_COLS = N_SEG * D
N_DEV = 8
W_IN_SHARD = IN_COLS // N_DEV
ROW_SHARD = D // N_DEV
QB = 128
DILATIONS = (1, 4, 16)
EPS = 1e-6
NEG = -1e30
BF = jnp.bfloat16
F32 = jnp.float32
MESH = pl.DeviceIdType.MESH

ADAM_LR = 0.001
ADAM_B1 = 0.9
ADAM_B2 = 0.999
ADAM_EPS = 1e-08
ADAM_WD = 0.01
ADAM_STEP = 10

V7X_VMEM_BYTES = 64 * 1024 * 1024
VMEM_LIMIT = V7X_VMEM_BYTES - 8 * 1024 * 1024
ROW_TILE = 256

VMEM_SPEC = pl.BlockSpec(memory_space=pltpu.VMEM)
ANY_SPEC = pl.BlockSpec(memory_space=pl.ANY)
SMEM_SPEC = pl.BlockSpec(memory_space=pltpu.SMEM)


def _params(n_grid_axes, vmem=VMEM_LIMIT):
    return pltpu.CompilerParams(dimension_semantics=("arbitrary",) * n_grid_axes, vmem_limit_bytes=vmem)


def _dot(a, b):
    return jnp.dot(a, b, preferred_element_type=F32)


def _dot_nt(a, b):
    return lax.dot_general(a, b, (((1,), (1,)), ((), ())), preferred_element_type=F32)


def _dot_tn(a, b):
    return lax.dot_general(a, b, (((0,), (0,)), ((), ())), preferred_element_type=F32)


def _sigmoid(z):
    return 1.0 / (1.0 + jnp.exp(-z))


def _my_place():
    x, y, c = lax.axis_index("x"), lax.axis_index("y"), lax.axis_index("c")
    return x, y, c, 4 * x + 2 * y + c


def _peers(x, y, c):
    out = []
    for k in range(1, N_DEV):
        px = 1 - x if k & 4 else x
        py = 1 - y if k & 2 else y
        pc = 1 - c if k & 1 else c
        out.append(((px, py, pc), 4 * px + 2 * py + pc))
    return out


def _device(p):
    return (p >> 2, (p >> 1) & 1, p & 1)


def _shard_cols(*ranges):
    def cols(p):
        found = None
        for lo, hi in ranges:
            a, b = max(lo, p * W_IN_SHARD), min(hi, (p + 1) * W_IN_SHARD)
            if a < b:
                assert found is None
                found = (a - p * W_IN_SHARD, b - p * W_IN_SHARD)
        return found

    return cols


def _whole(p):
    return ()


def _block(ref, idx, cols):
    return ref.at[idx] if cols == () else ref.at[idx, :, cols[0]:cols[1]]


class _WeightGather:
    def __init__(self, src, dst, send_sems, recv_sems, cols):
        self.src, self.dst, self.cols = src, dst, cols
        self.send_sems, self.recv_sems = send_sems, recv_sems
        self.me = _my_place()[3]

    def _copy(self, p, target):
        cols = self.cols(p)
        return pltpu.make_async_remote_copy(
            src_ref=self.src(p, cols), dst_ref=_block(self.dst, p, cols), send_sem=self.send_sems.at[target],
            recv_sem=self.recv_sems.at[p], device_id=_device(target), device_id_type=MESH)

    def _each(self, send, receive):
        for p in range(N_DEV):
            if self.cols(p) is None:
                continue

            def sender(p=p):
                for k in range(1, N_DEV):
                    send(self._copy(p, (p + k) % N_DEV))

            pl.when(self.me == p)(sender)
            pl.when(self.me != p)(lambda p=p: receive(self._copy(p, p)))

    def start(self):
        self._each(lambda cp: cp.start(), lambda cp: None)

    def finish(self):
        self._each(lambda cp: cp.wait_send(), lambda cp: cp.wait_recv())


WEIGHT_GATHER_SEMS = [pltpu.SemaphoreType.DMA((N_DEV,)), pltpu.SemaphoreType.DMA((N_DEV,))]
REST_COLS = _shard_cols((0, 4 * D), (7 * D, IN_COLS))
HEAD_PAIRS = D // 128


def _qkv_piece(h, seg):
    col = (4 + seg) * D + 128 * h
    return col // W_IN_SHARD, col % W_IN_SHARD


class _PieceGather:
    def __init__(self, src, dst, send_sems, recv_sems):
        self.src, self.dst, self.send_sems, self.recv_sems = src, dst, send_sems, recv_sems
        self.me = _my_place()[3]

    def _copy(self, i, target):
        p, lo = _qkv_piece(i // 3, i % 3)
        return pltpu.make_async_remote_copy(
            src_ref=self.src(p, lo, lo + 128), dst_ref=self.dst.at[p, :, lo:lo + 128], send_sem=self.send_sems.at[i, target],
            recv_sem=self.recv_sems.at[i], device_id=_device(target), device_id_type=MESH)

    def _owner(self, i, act):
        p = _qkv_piece(i // 3, i % 3)[0]

        def sender():
            for k in range(N_DEV - 1):
                act(self._copy(i, (p + 1 + (k + i) % (N_DEV - 1)) % N_DEV))

        pl.when(self.me == p)(sender)

    def start(self, pieces):
        for i in pieces:
            self._owner(i, lambda cp: cp.start())

    def wait_send(self, pieces):
        for i in pieces:
            self._owner(i, lambda cp: cp.wait_send())

    def wait_recv(self, pieces):
        for i in pieces:
            p = _qkv_piece(i // 3, i % 3)[0]
            pl.when(self.me != p)(lambda i=i, p=p: self._copy(i, p).wait_recv())


def _piece_sems(n):
    return [pltpu.SemaphoreType.DMA((n, N_DEV)), pltpu.SemaphoreType.DMA((n,))]


def _gather_first_weights(w_in, w3, cw):
    def body(w_in_ref, w3_ref, cw_ref, o_in, o_3, o_cw, in_bf, w3_bf, local_sems, *sems):
        me = _my_place()[3]

        def cast_rows(i, carry):
            r = pl.multiple_of(i * 128, 128)
            in_bf[pl.ds(r, 128), :] = w_in_ref[pl.ds(r, 128), :].astype(BF)
            return carry

        lax.fori_loop(0, D // 128, cast_rows, 0)
        for a in range(3):
            w3_bf[a] = w3_ref[a].astype(BF)
        gather = _PieceGather(lambda p, lo, hi: in_bf.at[:, lo:hi], o_in, *sems)
        gather.start(range(3))
        local = [pltpu.make_async_copy(src, dst.at[me], local_sems.at[a])
                 for a, (src, dst) in enumerate(((in_bf, o_in), (w3_bf, o_3), (cw_ref, o_cw)))]
        for cp in local:
            cp.start()
        gather.wait_recv(range(3))
        gather.wait_send(range(3))
        for cp in local:
            cp.wait()

    return pl.pallas_call(
        body, name="gather_first_weights",
        out_shape=(jax.ShapeDtypeStruct((N_DEV, D, W_IN_SHARD), BF),
                   jax.ShapeDtypeStruct((N_DEV, 3, ROW_SHARD, D), BF),
                   jax.ShapeDtypeStruct((N_DEV, 8, 128), F32)),
        in_specs=[VMEM_SPEC, VMEM_SPEC, VMEM_SPEC],
        out_specs=(ANY_SPEC, ANY_SPEC, ANY_SPEC),
        scratch_shapes=[pltpu.VMEM((D, W_IN_SHARD), BF), pltpu.VMEM((3, ROW_SHARD, D), BF),
                        pltpu.SemaphoreType.DMA((3,))] + _piece_sems(3),
        compiler_params=pltpu.CompilerParams(vmem_limit_bytes=VMEM_LIMIT),
    )(w_in, w3, cw)


class _GradExchange:
    def __init__(self, src, dst, send_sems, recv_sems, local_sem, cols):
        self.src, self.dst, self.cols = src, dst, cols
        self.send_sems, self.recv_sems, self.local_sem = send_sems, recv_sems, local_sem
        self.me = _my_place()[3]

    def _remote(self, p, source):
        return pltpu.make_async_remote_copy(
            src_ref=_block(self.src, p, self.cols(p)), dst_ref=_block(self.dst, source, self.cols(p)),
            send_sem=self.send_sems.at[p], recv_sem=self.recv_sems.at[source],
            device_id=_device(p), device_id_type=MESH)

    def _local(self, p):
        return pltpu.make_async_copy(_block(self.src, p, self.cols(p)), _block(self.dst, p, self.cols(p)),
                                     self.local_sem)

    def _as_each_device(self, send, local, receive):
        for m in range(N_DEV):
            def branch(m=m):
                for k in range(1, N_DEV):
                    p = (m + k) % N_DEV
                    if self.cols(p) is not None:
                        send(self._remote(p, m))
                if self.cols(m) is not None:
                    if self.local_sem is not None:
                        local(self._local(m))
                    for k in range(1, N_DEV):
                        receive(self._remote(m, (m + k) % N_DEV))

            pl.when(self.me == m)(branch)

    def start(self):
        self._as_each_device(lambda cp: cp.start(), lambda cp: cp.start(), lambda cp: None)

    def finish(self):
        self._as_each_device(lambda cp: cp.wait_send(), lambda cp: cp.wait(), lambda cp: cp.wait_recv())


GRAD_EXCHANGE_SEMS = [pltpu.SemaphoreType.DMA((N_DEV,)), pltpu.SemaphoreType.DMA((N_DEV,)), pltpu.SemaphoreType.DMA]


def _allreduce_small(p_mid, p_conv, p_norm):
    def body(a_ref, b_ref, c_ref, out_ref, mine, gathered, send_sems, recv_sems):
        x, y, c, me = _my_place()
        mine[...] = a_ref[...] + b_ref[...] + c_ref[...]
        gathered[me] = mine[...]
        remote = []
        for k, (peer, _) in enumerate(_peers(x, y, c)):
            cp = pltpu.make_async_remote_copy(
                src_ref=mine, dst_ref=gathered.at[me], send_sem=send_sems.at[k], recv_sem=recv_sems.at[k],
                device_id=peer, device_id_type=MESH)
            cp.start()
            remote.append(cp)
        for cp in remote:
            cp.wait()
        total = gathered[0]
        for s in range(1, N_DEV):
            total = total + gathered[s]
        out_ref[...] = total

    return pl.pallas_call(
        body, name="allreduce_small",
        out_shape=jax.ShapeDtypeStruct((8, D), F32),
        in_specs=[VMEM_SPEC, VMEM_SPEC, VMEM_SPEC], out_specs=VMEM_SPEC,
        scratch_shapes=[pltpu.VMEM((8, D), F32), pltpu.VMEM((N_DEV, 8, D), F32),
                        pltpu.SemaphoreType.DMA((N_DEV - 1,)), pltpu.SemaphoreType.DMA((N_DEV - 1,))],
    )(p_mid, p_conv, p_norm)


def _proj_pieces():
    cuts = sorted(set(range(0, IN_COLS + 1, D)) | set(range(0, IN_COLS + 1, W_IN_SHARD)))
    return [(lo // D, lo % D, lo // W_IN_SHARD, lo % W_IN_SHARD, hi - lo) for lo, hi in zip(cuts[:-1], cuts[1:])]


def _norm(x2, norm_g):
    S = x2.shape[0]
    tm = ROW_TILE

    def body(x_ref, g_ref, u_ref, ut_ref):
        xv = x_ref[...]
        r = lax.rsqrt(jnp.mean(xv * xv, axis=-1, keepdims=True) + EPS)
        u = xv * r * g_ref[...]
        u_ref[...] = u.astype(BF)
        ut_ref[...] = u.T.astype(BF)

    return pl.pallas_call(
        body, name="norm", grid=(S // tm,),
        in_specs=[pl.BlockSpec((tm, D), lambda i: (i, 0)), pl.BlockSpec((1, D), lambda i: (0, 0))],
        out_specs=(pl.BlockSpec((tm, D), lambda i: (i, 0)), pl.BlockSpec((D, tm), lambda i: (0, i))),
        out_shape=(jax.ShapeDtypeStruct((S, D), BF), jax.ShapeDtypeStruct((D, S), BF)),
        compiler_params=_params(1),
    )(x2, norm_g)


PROJ_TN = 256


def _proj_cols(u, w_all, seg0, n_seg, dtype, name):
    S = u.shape[0]
    tn = PROJ_TN
    per_shard = W_IN_SHARD // tn
    tile0 = seg0 * D // tn

    def body(u_ref, w_ref, out_ref):
        out_ref[...] = _dot(u_ref[...], w_ref[0]).astype(dtype)

    return pl.pallas_call(
        body, name=name, grid=(n_seg * D // tn,),
        in_specs=[VMEM_SPEC, pl.BlockSpec((1, D, tn), lambda t: ((tile0 + t) // per_shard, 0, (tile0 + t) % per_shard))],
        out_specs=pl.BlockSpec((S, tn), lambda t: (0, t)),
        out_shape=jax.ShapeDtypeStruct((S, n_seg * D), dtype),
        compiler_params=_params(1),
    )(u, w_all)


CONV_TM, CONV_TC = 256, 512
HALO = 16


def _conv_fwd(pa, cw8):
    S = pa.shape[0]
    tm, tc = CONV_TM, CONV_TC
    nct = D // tc

    def seg(s):
        return pl.BlockSpec((tm, tc), lambda i, j, s=s: (i, s * nct + j))

    def halo_before(s):
        return pl.BlockSpec((HALO, tc), lambda i, j, s=s: (jnp.maximum(i * (tm // HALO) - 1, 0), s * nct + j))

    def body(xc, bg, cg, zc, xch, cgh, cw, out):
        i = pl.program_id(0)
        a = cg[...].astype(F32) * xc[...].astype(F32)
        ah = cgh[...].astype(F32) * xch[...].astype(F32)
        ah = jnp.where(i > 0, ah, 0.0)
        row = lax.broadcasted_iota(jnp.int32, (tm, tc), 0)
        a1 = jnp.where(row == 0, ah[HALO - 1:HALO, :], pltpu.roll(a, 1, 0))
        a2 = jnp.where(row == 0, ah[HALO - 2:HALO - 1, :],
                       jnp.where(row == 1, ah[HALO - 1:HALO, :], pltpu.roll(a, 2, 0)))
        w = cw[...]
        conv = w[0:1, :] * a2 + w[1:2, :] * a1 + w[2:3, :] * a
        z = zc[...].astype(F32)
        out[...] = (z * _sigmoid(z) * bg[...].astype(F32) * conv).astype(BF)

    return pl.pallas_call(
        body, name="conv_fwd", grid=(S // tm, nct),
        in_specs=[seg(0), seg(1), seg(2), seg(3), halo_before(0), halo_before(2),
                  pl.BlockSpec((8, tc), lambda i, j: (0, j))],
        out_specs=pl.BlockSpec((tm, tc), lambda i, j: (i, j)),
        out_shape=jax.ShapeDtypeStruct((S, D), BF),
        compiler_params=_params(2),
    )(pa, pa, pa, pa, pa, pa, cw8)


ATT_UNROLL = 2


LAYOUT_MOD = 4
RUN = QB // LAYOUT_MOD


def _fold_masks(d):
    row = lax.broadcasted_iota(jnp.int32, (QB, QB), 0)
    lane = lax.broadcasted_iota(jnp.int32, (QB, QB), 1)
    if d == 1:
        qpos, kpos = LAYOUT_MOD * (row % RUN) + row // RUN, LAYOUT_MOD * (lane % RUN) + lane // RUN
    else:
        qpos, kpos = row, lane
    tri_le = kpos <= qpos
    dist = jnp.where(tri_le, qpos - kpos, qpos - kpos + QB).astype(F32)
    return tri_le, dist, lane < HEAD_DIM


class _Rows:
    def __init__(self, slices):
        self.slices = slices

    def get(self, ref):
        parts = [ref[sl, :] for sl in self.slices]
        return parts[0] if len(parts) == 1 else jnp.concatenate(parts, axis=0)

    def put(self, ref, val):
        size = QB // len(self.slices)
        for g, sl in enumerate(self.slices):
            ref[sl, :] = val if len(self.slices) == 1 else val[g * size:(g + 1) * size]

    def add(self, ref, val):
        self.put(ref, self.get(ref) + val)


def _block_rows(b, d, S):
    quarter = S // LAYOUT_MOD
    nb = S // (QB * d)
    r, n = b // nb, b % nb
    n_prev = jnp.maximum(n - 1, 0)
    if d == 1:
        runs = lambda m: _Rows([pl.ds(pl.multiple_of(g * quarter + RUN * m, RUN), RUN) for g in range(LAYOUT_MOD)])
        return n, runs(n), runs(n_prev)
    if d == LAYOUT_MOD:
        block = lambda m: _Rows([pl.ds(pl.multiple_of(r * quarter + QB * m, QB), QB)])
        return n, block(n), block(n_prev)
    step = d // LAYOUT_MOD
    first = (r % LAYOUT_MOD) * quarter + r // LAYOUT_MOD
    strided = lambda m: _Rows([pl.ds(first + QB * step * m, QB, stride=step)])
    return n, strided(n), strided(n_prev)


def _natural_rows(i, S):
    per = S // LAYOUT_MOD // QB
    return pl.ds(i // per + LAYOUT_MOD * QB * (i % per), QB, stride=LAYOUT_MOD)


def _head_sum_matrix():
    r = lax.broadcasted_iota(jnp.int32, (2 * QB, 2 * QB), 0)
    c = lax.broadcasted_iota(jnp.int32, (2 * QB, 2 * QB), 1)
    return (((r % QB) // HEAD_DIM) == (c // QB)).astype(F32).astype(BF)


def _hi_lo(t):
    hi = t.astype(BF)
    return jnp.concatenate([hi, (t - hi.astype(F32)).astype(BF)], axis=1)


PROJ_ROWS = 512


def _attn_fwd(u, slopes, w_all, w3_all, cw_all):
    S = u.shape[0]
    hpr = HEAD_PAIRS
    n_blocks = S // QB
    later = range(3, 3 * hpr)

    def body(sl_ref, u_ref, w_in_ref, w3_in_ref, cw_in_ref, o_ref, lse_ref, q_ref, k_ref, v_ref, w_ref, w3_ref,
             cw_ref, acc, m_s, l_s, w_tile, staged, tile_sems, *sems):
        hp = pl.program_id(0)
        me = _my_place()[3]
        pieces = _PieceGather(lambda p, lo, hi: w_ref.at[p, :, lo:hi], w_ref, *sems[0:2])
        gathers = (_WeightGather(lambda p, cols: _block(w_ref, me, cols), w_ref, *sems[2:4], REST_COLS),
                   _WeightGather(lambda p, cols: w3_ref.at[me], w3_ref, *sems[4:6], _whole),
                   _WeightGather(lambda p, cols: cw_ref.at[me], cw_ref, *sems[6:8], _whole))

        @pl.when(hp == 0)
        def _():
            pieces.start(later)
            for g in gathers:
                g.start()

        for h in range(hpr):
            @pl.when(hp == h)
            def _(h=h):
                if h > 0:
                    pieces.wait_recv(range(3 * h, 3 * h + 3))
                fetch = []
                for seg in range(3):
                    p, lo = _qkv_piece(h, seg)
                    fetch.append(pltpu.make_async_copy(w_ref.at[p, :, lo:lo + 128], w_tile.at[:, seg * 128:(seg + 1) * 128],
                                                       tile_sems.at[seg]))
                    fetch[-1].start()
                for cp in fetch:
                    cp.wait()

        def project(i, carry):
            rows = pl.ds(pl.multiple_of(i * PROJ_ROWS, PROJ_ROWS), PROJ_ROWS)
            qkv = _dot(u_ref[rows, :], w_tile[...])
            per = PROJ_ROWS // LAYOUT_MOD
            for seg, ref in enumerate((q_ref, k_ref, v_ref)):
                staged[seg] = qkv[:, seg * 128:(seg + 1) * 128]
                for g in range(LAYOUT_MOD):
                    dst = pl.ds(pl.multiple_of(g * (S // LAYOUT_MOD) + i * per, per), per)
                    ref[dst, :] = staged.at[seg][pl.ds(g, per, stride=LAYOUT_MOD), :]
            return carry

        lax.fori_loop(0, S // PROJ_ROWS, project, 0)

        head_sum = _head_sum_matrix()
        ones_b = jnp.ones((2 * QB, QB), BF)
        m_s[...] = jnp.full(m_s.shape, NEG, F32)
        l_s[...] = jnp.zeros(l_s.shape, F32)
        acc[...] = jnp.zeros(acc.shape, F32)

        for d in DILATIONS:
            tri_le, dist, low = _fold_masks(d)
            low_b = low.astype(F32).astype(BF)
            high_b = 1.0 - low_b
            slope = [sl_ref[2 * hp + a] * float(d) for a in range(2)]
            bias = [slope[a] * dist for a in range(2)]

            def block(b, d=d, slope=slope, bias=bias, tri_le=tri_le, low=low, low_b=low_b, high_b=high_b):
                n, cur, prev = _block_rows(b, d, S)
                has_prev = n > 0
                valid = jnp.logical_or(tri_le, has_prev)
                q2 = (cur.get(q_ref) * 0.125).astype(BF)
                qs = jnp.concatenate([q2 * low_b, q2 * high_b], axis=0)
                vp = prev.get(v_ref)
                kp_b = prev.get(k_ref).astype(BF)
                kcat = jnp.concatenate([kp_b, cur.get(k_ref).astype(BF)], axis=0)
                vcat = jnp.concatenate([vp, cur.get(v_ref)], axis=0).astype(BF)
                s2 = _dot_nt(qs, kcat)
                e2 = _dot(_hi_lo(q2.astype(F32) * kp_b.astype(F32)), head_sum)
                p_rows, alpha_h, pe_h = [], [], []
                for a in range(2):
                    sp, sc = s2[a * QB:(a + 1) * QB, :QB], s2[a * QB:(a + 1) * QB, QB:]
                    comb = jnp.where(valid, jnp.where(tri_le, sc, sp) - bias[a], NEG)
                    e = jnp.where(has_prev, e2[:, a * QB:(a + 1) * QB] - slope[a] * float(QB), NEG)
                    m_old = cur.get(m_s.at[a])
                    m_new = jnp.maximum(jnp.maximum(m_old, jnp.max(comb, axis=-1, keepdims=True)), e)
                    cur.put(m_s.at[a], m_new)
                    p = jnp.exp(comb - m_new)
                    pe_h.append(jnp.exp(e - m_new))
                    alpha_h.append(jnp.exp(m_old - m_new))
                    p_rows.append(jnp.concatenate([jnp.where(tri_le, 0.0, p).astype(BF),
                                                   jnp.where(tri_le, p, 0.0).astype(BF)], axis=1))
                pv = _dot(jnp.concatenate(p_rows, axis=0), jnp.concatenate([vcat, ones_b], axis=1))
                for a in range(2):
                    cur.put(l_s.at[a], alpha_h[a] * cur.get(l_s.at[a]) + pv[a * QB:(a + 1) * QB, QB:] + pe_h[a])
                cur.put(acc, jnp.where(low, alpha_h[0], alpha_h[1]) * cur.get(acc)
                        + jnp.where(low, pv[:QB, :QB], pv[QB:, :QB]) + jnp.where(low, pe_h[0], pe_h[1]) * vp)

            def several(it, carry, block=block):
                for u in range(ATT_UNROLL):
                    block(it * ATT_UNROLL + u)
                return carry

            lax.fori_loop(0, n_blocks // ATT_UNROLL, several, 0)

        low = _fold_masks(LAYOUT_MOD)[2]

        def finish(i, carry):
            rows = pl.ds(pl.multiple_of(i * QB, QB), QB)
            l0, l1 = l_s[0, rows, :], l_s[1, rows, :]
            o_ref[_natural_rows(i, S), :] = acc[rows, :] / jnp.where(low, l0, l1)
            lse_ref[0, rows, :] = m_s[0, rows, :] + jnp.log(l0)
            lse_ref[1, rows, :] = m_s[1, rows, :] + jnp.log(l1)
            return carry

        lax.fori_loop(0, n_blocks, finish, 0)

        @pl.when(hp == hpr - 1)
        def _():
            pieces.wait_send(later)
            for g in gathers:
                g.finish()

    col = pl.BlockSpec((S, 128), lambda h: (0, h))
    act = jax.ShapeDtypeStruct((S, D), F32)
    gathered = (w_all, w3_all, cw_all)
    return pl.pallas_call(
        body, name="attn_fwd", grid=(hpr,),
        in_specs=[SMEM_SPEC, VMEM_SPEC, ANY_SPEC, ANY_SPEC, ANY_SPEC],
        out_specs=(col, pl.BlockSpec((2, S, 128), lambda h: (0, 0, h)), col, col, col, ANY_SPEC, ANY_SPEC, ANY_SPEC),
        out_shape=(act, jax.ShapeDtypeStruct((2, S, D), F32), act, act, act,
                   *[jax.ShapeDtypeStruct(t.shape, t.dtype) for t in gathered]),
        scratch_shapes=([pltpu.VMEM((S, 128), F32), pltpu.VMEM((2, S, 128), F32), pltpu.VMEM((2, S, 128), F32),
                         pltpu.VMEM((D, 3 * 128), BF), pltpu.VMEM((3, PROJ_ROWS, 128), F32),
                         pltpu.SemaphoreType.DMA((3,))]
                        + _piece_sems(3 * hpr) + WEIGHT_GATHER_SEMS * 3),
        input_output_aliases={2: 5, 3: 6, 4: 7},
        compiler_params=_params(1),
    )(slopes, u, *gathered)


def _set_rows(shape, rows):
    idx = lax.broadcasted_iota(jnp.int32, shape, 0)
    out = jnp.zeros(shape, F32)
    for r, val in rows.items():
        out = out + jnp.where(idx == r, val, 0.0)
    return out


def _mid(yc_in, pa_mid, o, x2, target, b_merge, final_g, w3):
    S = x2.shape[0]
    tm = ROW_TILE
    nsteps = S // tm
    tile = pl.BlockSpec((tm, D), lambda i: (i, 0))

    def body(yc_ref, za_ref, gcp_ref, gap_ref, o_ref, x_ref, t_ref, b_ref, fg_ref, w_ref,
             dh_ref, dmid_ref, do_ref, dyc_ref, gw_ref, small_ref, acc, stage):
        i = pl.program_id(0)

        @pl.when(i == 0)
        def _():
            acc[...] = jnp.zeros_like(acc)
            small_ref[...] = jnp.zeros_like(small_ref)

        wc, wa, wo = w_ref[0], w_ref[1], w_ref[2]
        z = za_ref[...].astype(F32)
        sg = _sigmoid(z)
        ov = o_ref[...]
        yc_in_b, ya_in_b = yc_ref[...], (z * sg * ov).astype(BF)
        yc = _dot(yc_in_b, wc)
        ya = _dot(ya_in_b, wa)
        b = b_ref[...]
        gc = _sigmoid(gcp_ref[...].astype(F32) + b[:, :D])
        ga = _sigmoid(gap_ref[...].astype(F32) + b[:, D:])
        merged = gc * yc + ga * ya
        merged_b = merged.astype(BF)
        h = x_ref[...] + _dot(merged_b, wo)
        r2 = lax.rsqrt(jnp.mean(h * h, axis=-1, keepdims=True) + EPS)
        n = h * r2
        fg = fg_ref[...]
        err = n * fg - t_ref[...]
        loss = 0.5 * jnp.sum(jnp.sum(err * err, axis=-1, keepdims=True) / D, axis=0, keepdims=True)
        dy = err / D
        g_fg = jnp.sum(dy * n, axis=0, keepdims=True)
        dn = dy * fg
        dh = r2 * (dn - n * jnp.mean(dn * n, axis=-1, keepdims=True))
        dh_ref[...] = dh
        dh_b = dh.astype(BF)
        dmerged = _dot_nt(dh_b, wo)
        acc[2] += _dot(merged.T.astype(BF), dh_b)
        dyc = (dmerged * gc).astype(BF)
        dya = (dmerged * ga).astype(BF)
        dgcp = dmerged * yc * gc * (1.0 - gc)
        dgap = dmerged * ya * ga * (1.0 - ga)
        dmid_ref[1] = dgcp.astype(BF)
        dmid_ref[2] = dgap.astype(BF)
        acc[0] += _dot(yc_in_b.astype(F32).T.astype(BF), dyc)
        acc[1] += _dot(ya_in_b.astype(F32).T.astype(BF), dya)
        dyc_ref[...] = _dot_nt(dyc, wc).astype(BF)
        dya_in = _dot_nt(dya, wa)
        do_ref[...] = dya_in * (z * sg)
        dmid_ref[0] = (dya_in * ov * (sg * (1.0 + z * (1.0 - sg)))).astype(BF)
        small_ref[...] += _set_rows((8, D), {
            1: jnp.sum(dgcp, axis=0, keepdims=True), 2: jnp.sum(dgap, axis=0, keepdims=True),
            3: g_fg, 7: jnp.broadcast_to(loss, (1, D))})

        @pl.when(i == nsteps - 1)
        def _():
            for p in range(N_DEV):
                for a in range(3):
                    stage[...] = acc[a, p * ROW_SHARD:(p + 1) * ROW_SHARD, :].astype(BF)
                    pltpu.sync_copy(stage, gw_ref.at[p, a])

    return pl.pallas_call(
        body, name="mid", grid=(nsteps,),
        in_specs=[tile, pl.BlockSpec((tm, D), lambda i: (i, 0)), pl.BlockSpec((tm, D), lambda i: (i, 1)),
                  pl.BlockSpec((tm, D), lambda i: (i, 2)), tile, tile, tile,
                  pl.BlockSpec((1, 2 * D), lambda i: (0, 0)), pl.BlockSpec((1, D), lambda i: (0, 0)), VMEM_SPEC],
        out_specs=(tile, pl.BlockSpec((3, tm, D), lambda i: (0, i, 0)), tile, tile,
                   ANY_SPEC, pl.BlockSpec((8, D), lambda i: (0, 0))),
        out_shape=(jax.ShapeDtypeStruct((S, D), F32), jax.ShapeDtypeStruct((3, S, D), BF),
                   jax.ShapeDtypeStruct((S, D), F32), jax.ShapeDtypeStruct((S, D), BF),
                   jax.ShapeDtypeStruct((N_DEV, 3, ROW_SHARD, D), BF), jax.ShapeDtypeStruct((8, D), F32)),
        scratch_shapes=[pltpu.VMEM((3, D, D), F32), pltpu.VMEM((ROW_SHARD, D), BF)],
        compiler_params=_params(1),
    )(yc_in, pa_mid, pa_mid, pa_mid, o, x2, target, b_merge, final_g, w3)


def _conv_bwd(dyc_in, pa, cw8):
    S = pa.shape[0]
    tm, tc = CONV_TM, CONV_TC
    nct = D // tc
    nrt = S // tm
    last_halo = S // HALO - 1

    def seg(s):
        return pl.BlockSpec((tm, tc), lambda j, i, s=s: (i, s * nct + j))

    def halo_before(s):
        return pl.BlockSpec((HALO, tc), lambda j, i, s=s: (jnp.maximum(i * (tm // HALO) - 1, 0), s * nct + j))

    def halo_after(s):
        return pl.BlockSpec((HALO, tc), lambda j, i, s=s: (jnp.minimum((i + 1) * (tm // HALO), last_halo), s * nct + j))

    def body(dy, xc, bg, cg, zc, xch, cgh, dyn, bgn, zcn, cw, dout, gcw):
        i = pl.program_id(1)

        @pl.when(i == 0)
        def _():
            gcw[...] = jnp.zeros_like(gcw)

        xcv, cgv = xc[...].astype(F32), cg[...].astype(F32)
        a = cgv * xcv
        ah = jnp.where(i > 0, cgh[...].astype(F32) * xch[...].astype(F32), 0.0)
        row = lax.broadcasted_iota(jnp.int32, (tm, tc), 0)
        a1 = jnp.where(row == 0, ah[HALO - 1:HALO, :], pltpu.roll(a, 1, 0))
        a2 = jnp.where(row == 0, ah[HALO - 2:HALO - 1, :],
                       jnp.where(row == 1, ah[HALO - 1:HALO, :], pltpu.roll(a, 2, 0)))
        w = cw[...]
        conv = w[0:1, :] * a2 + w[1:2, :] * a1 + w[2:3, :] * a
        z = zc[...].astype(F32)
        sg = _sigmoid(z)
        silu = z * sg
        bgv = bg[...].astype(F32)
        dyv = dy[...].astype(F32)
        dout[3] = (dyv * bgv * conv * (sg * (1.0 + z * (1.0 - sg)))).astype(BF)
        dout[1] = (dyv * silu * conv).astype(BF)
        dc = dyv * silu * bgv
        zn = zcn[...].astype(F32)
        dcn = dyn[...].astype(F32) * (zn * _sigmoid(zn)) * bgn[...].astype(F32)
        dcn = jnp.where(i < nrt - 1, dcn, 0.0)
        dc1 = jnp.where(row == tm - 1, dcn[0:1, :], pltpu.roll(dc, tm - 1, 0))
        dc2 = jnp.where(row == tm - 1, dcn[1:2, :],
                        jnp.where(row == tm - 2, dcn[0:1, :], pltpu.roll(dc, tm - 2, 0)))
        da = w[2:3, :] * dc + w[1:2, :] * dc1 + w[0:1, :] * dc2
        dout[2] = (da * xcv).astype(BF)
        dout[0] = (da * cgv).astype(BF)
        gcw[...] += _set_rows((8, tc), {
            4: jnp.sum(dc * a2, axis=0, keepdims=True), 5: jnp.sum(dc * a1, axis=0, keepdims=True),
            6: jnp.sum(dc * a, axis=0, keepdims=True)})

    return pl.pallas_call(
        body, name="conv_bwd", grid=(nct, nrt),
        in_specs=[pl.BlockSpec((tm, tc), lambda j, i: (i, j)), seg(0), seg(1), seg(2), seg(3),
                  halo_before(0), halo_before(2),
                  pl.BlockSpec((HALO, tc), lambda j, i: (jnp.minimum((i + 1) * (tm // HALO), last_halo), j)),
                  halo_after(1), halo_after(3), pl.BlockSpec((8, tc), lambda j, i: (0, j))],
        out_specs=(pl.BlockSpec((4, tm, tc), lambda j, i: (0, i, j)), pl.BlockSpec((8, tc), lambda j, i: (0, j))),
        out_shape=(jax.ShapeDtypeStruct((4, S, D), BF), jax.ShapeDtypeStruct((8, D), F32)),
        compiler_params=_params(2),
    )(dyc_in, pa, pa, pa, pa, pa, pa, dyc_in, pa, pa, cw8)


def _attn_bwd(q, k, v, slopes, do, o, lse, g_in, g_3):
    S = q.shape[0]
    hpr = HEAD_PAIRS
    n_blocks = S // QB

    def body(sl_ref, q_ref, k_ref, v_ref, do_ref, o_ref, lse_ref, gin_ref, g3_ref, out_ref, rin_ref, r3_ref,
             dq_s, dk_s, dv_s, do_s, dd_s, *sems):
        hp = pl.program_id(0)
        exchanges = (_GradExchange(gin_ref, rin_ref, *sems[:3], _shard_cols((0, SEG0_ATTN * D), (SEG0_MID * D, IN_COLS))),
                     _GradExchange(g3_ref, r3_ref, *sems[3:], _whole))

        @pl.when(hp == 0)
        def _():
            for ex in exchanges:
                ex.start()

        head_sum = _head_sum_matrix()
        dq_s[...] = jnp.zeros(dq_s.shape, F32)
        dk_s[...] = jnp.zeros(dk_s.shape, F32)
        dv_s[...] = jnp.zeros(dv_s.shape, F32)

        def row_dots(i, carry):
            rows = pl.ds(pl.multiple_of(i * QB, QB), QB)
            natural = _natural_rows(i, S)
            do_c = do_ref[natural, :]
            do_s[rows, :] = do_c
            dd = _dot(_hi_lo(do_c * o_ref[natural, :]), head_sum)
            dd_s[0, rows, :] = dd[:, :QB]
            dd_s[1, rows, :] = dd[:, QB:]
            return carry

        lax.fori_loop(0, n_blocks, row_dots, 0)

        for d in DILATIONS:
            tri_le, dist, low = _fold_masks(d)
            low_b = low.astype(F32).astype(BF)
            high_b = 1.0 - low_b
            slope = [sl_ref[2 * hp + a] * float(d) for a in range(2)]
            bias = [slope[a] * dist for a in range(2)]

            def block(b, d=d, slope=slope, bias=bias, tri_le=tri_le, low=low, low_b=low_b, high_b=high_b):
                n, cur, prev = _block_rows(b, d, S)
                has_prev = n > 0
                valid = jnp.logical_or(tri_le, has_prev)
                q2f = cur.get(q_ref) * 0.125
                q2 = q2f.astype(BF)
                qs = jnp.concatenate([q2 * low_b, q2 * high_b], axis=0)
                kp, vp = prev.get(k_ref), prev.get(v_ref)
                kp_b, vp_b = kp.astype(BF), vp.astype(BF)
                kcat = jnp.concatenate([kp_b, cur.get(k_ref).astype(BF)], axis=0)
                vcat = jnp.concatenate([vp_b, cur.get(v_ref).astype(BF)], axis=0)
                do2f = cur.get(do_s)
                do2 = do2f.astype(BF)
                dos = jnp.concatenate([do2 * low_b, do2 * high_b], axis=0)
                s2 = _dot_nt(qs, kcat)
                dp2 = _dot_nt(dos, vcat)
                diag2 = _dot(jnp.concatenate([_hi_lo(q2.astype(F32) * kp_b.astype(F32)),
                                              _hi_lo(do2.astype(F32) * vp_b.astype(F32))], axis=0), head_sum)
                p_rows, ds_rows, pe_h, dse_h = [], [], [], []
                for a in range(2):
                    hs = slice(a * QB, (a + 1) * QB)
                    sp, sc = s2[hs, :QB], s2[hs, QB:]
                    dpp, dpc = dp2[hs, :QB], dp2[hs, QB:]
                    lse_a, dd_a = cur.get(lse_ref.at[a]), cur.get(dd_s.at[a])
                    comb = jnp.where(tri_le, sc, sp) - bias[a]
                    e = diag2[:QB, hs] - slope[a] * float(QB)
                    p = jnp.where(valid, jnp.exp(comb - lse_a), 0.0)
                    pe = jnp.where(has_prev, jnp.exp(e - lse_a), 0.0)
                    ds = p * (jnp.where(tri_le, dpc, dpp) - dd_a)
                    dse_h.append(pe * (diag2[QB:, hs] - dd_a))
                    pe_h.append(pe)
                    p_rows.append(jnp.concatenate([jnp.where(tri_le, 0.0, p).astype(BF),
                                                   jnp.where(tri_le, p, 0.0).astype(BF)], axis=1))
                    ds_rows.append(jnp.concatenate([jnp.where(tri_le, 0.0, ds).astype(BF),
                                                    jnp.where(tri_le, ds, 0.0).astype(BF)], axis=1))
                pst = jnp.concatenate(p_rows, axis=0)
                dst = jnp.concatenate(ds_rows, axis=0)
                pe2 = jnp.where(low, pe_h[0], pe_h[1])
                dse2 = jnp.where(low, dse_h[0], dse_h[1])
                dq = _dot(dst, kcat)
                cur.add(dq_s, (jnp.where(low, dq[:QB], dq[QB:]) + dse2 * kp) * 0.125)
                dk = _dot_tn(dst, qs)
                dv = _dot_tn(pst, dos)
                prev.add(dk_s, dk[:QB] + dse2 * q2f)
                cur.add(dk_s, dk[QB:])
                prev.add(dv_s, dv[:QB] + pe2 * do2f)
                cur.add(dv_s, dv[QB:])

            def several(it, carry, block=block):
                for u in range(ATT_UNROLL):
                    block(it * ATT_UNROLL + u)
                return carry

            lax.fori_loop(0, n_blocks // ATT_UNROLL, several, 0)

        def finish(i, carry):
            rows = pl.ds(pl.multiple_of(i * QB, QB), QB)
            natural = _natural_rows(i, S)
            for t, ref in enumerate((dq_s, dk_s, dv_s)):
                out_ref.at[t][natural, :] = ref[rows, :]
            return carry

        lax.fori_loop(0, n_blocks, finish, 0)

        @pl.when(hp == hpr - 1)
        def _():
            for ex in exchanges:
                ex.finish()

    col = pl.BlockSpec((S, 128), lambda h: (0, h))
    return pl.pallas_call(
        body, name="attn_bwd", grid=(hpr,),
        in_specs=[SMEM_SPEC, col, col, col, col, col, pl.BlockSpec((2, S, 128), lambda h: (0, 0, h)),
                  ANY_SPEC, ANY_SPEC],
        out_specs=(pl.BlockSpec((3, S, 128), lambda h: (0, 0, h)), ANY_SPEC, ANY_SPEC),
        out_shape=(jax.ShapeDtypeStruct((3, S, D), F32), jax.ShapeDtypeStruct(g_in.shape, BF),
                   jax.ShapeDtypeStruct(g_3.shape, BF)),
        scratch_shapes=([pltpu.VMEM((S, 128), F32)] * 4 + [pltpu.VMEM((2, S, 128), F32)]
                        + GRAD_EXCHANGE_SEMS + GRAD_EXCHANGE_SEMS),
        compiler_params=_params(1),
    )(slopes, q, k, v, do, o, lse, g_in, g_3)


WG_TN = 256
SEG0_CONV, SEG0_ATTN, SEG0_MID = 0, 4, 7


def _wgrad_in(ut, d_group, seg0, g_in, name):
    S = ut.shape[1]
    tn = WG_TN
    per_seg = D // tn
    per_shard = W_IN_SHARD // tn
    n_tiles = d_group.shape[0] * per_seg
    tile0 = seg0 * per_seg

    def body(ut_ref, d_ref, *rest):
        rest[-1][0] = _dot(ut_ref[...], d_ref[0].astype(BF)).astype(BF)

    operands, in_specs, aliases = [ut, d_group], [VMEM_SPEC, pl.BlockSpec((1, S, tn), lambda t: (t // per_seg, 0, t % per_seg))], {}
    if g_in is not None:
        operands.append(g_in)
        in_specs.append(ANY_SPEC)
        aliases = {2: 0}
    return pl.pallas_call(
        body, name=name, grid=(n_tiles,), in_specs=in_specs,
        out_specs=pl.BlockSpec((1, D, tn), lambda t: ((tile0 + t) // per_shard, 0, (tile0 + t) % per_shard)),
        out_shape=jax.ShapeDtypeStruct((N_DEV, D, W_IN_SHARD), BF),
        input_output_aliases=aliases,
        compiler_params=_params(1),
    )(*operands)


def _dgrad_norm_bwd(d_conv, d_attn, d_mid, w_all, x2, dh, norm_g):
    S = x2.shape[0]
    tm = ROW_TILE
    nsteps = S // tm
    tile = pl.BlockSpec((tm, D), lambda i: (i, 0))
    pieces = _proj_pieces()

    def body(a_ref, b_ref, c_ref, w_ref, x_ref, dh_ref, g_ref, gx_ref, small_ref):
        i = pl.program_id(0)

        @pl.when(i == 0)
        def _():
            small_ref[...] = jnp.zeros_like(small_ref)

        groups = (a_ref, b_ref, c_ref)
        du = jnp.zeros((tm, D), F32)
        for s, sc, p, pc, width in pieces:
            g = 0 if s < 4 else (1 if s < 7 else 2)
            local = s - (0, 4, 7)[g]
            du = du + _dot_nt(groups[g][local, :, sc:sc + width].astype(BF), w_ref[p, :, pc:pc + width])
        xv = x_ref[...]
        r = lax.rsqrt(jnp.mean(xv * xv, axis=-1, keepdims=True) + EPS)
        n = xv * r
        dn = du * g_ref[...]
        gx_ref[...] = dh_ref[...] + r * (dn - n * jnp.mean(dn * n, axis=-1, keepdims=True))
        small_ref[...] += _set_rows((8, D), {0: jnp.sum(du * n, axis=0, keepdims=True)})

    return pl.pallas_call(
        body, name="dgrad_norm_bwd", grid=(nsteps,),
        in_specs=[pl.BlockSpec((4, tm, D), lambda i: (0, i, 0)), pl.BlockSpec((3, tm, D), lambda i: (0, i, 0)),
                  pl.BlockSpec((3, tm, D), lambda i: (0, i, 0)), VMEM_SPEC, tile, tile,
                  pl.BlockSpec((1, D), lambda i: (0, 0))],
        out_specs=(tile, pl.BlockSpec((8, D), lambda i: (0, 0))),
        out_shape=(jax.ShapeDtypeStruct((S, D), F32), jax.ShapeDtypeStruct((8, D), F32)),
        compiler_params=_params(1),
    )(d_conv, d_attn, d_mid, w_all, x2, dh, norm_g)


HBM_SPEC = pl.BlockSpec(memory_space=pltpu.HBM)
SEM_SPEC = pl.BlockSpec(memory_space=pltpu.SEMAPHORE)
ATTN_COLS = _shard_cols((SEG0_ATTN * D, SEG0_MID * D))


def _attn_cols_exchange_start(g_in, r_in):
    def body(g_ref, r_ref, send_sems, recv_sems, g_thru, r_thru, token):
        _GradExchange(g_ref, r_ref, send_sems, recv_sems, None, ATTN_COLS).start()
        token[...] = jnp.zeros_like(token)

    hbm = pltpu.with_memory_space_constraint
    return pl.pallas_call(
        body, name="attn_cols_exchange_start",
        out_shape=(pltpu.SemaphoreType.DMA((N_DEV,)), pltpu.SemaphoreType.DMA((N_DEV,)),
                   pltpu.HBM(g_in.shape, g_in.dtype), pltpu.HBM(r_in.shape, r_in.dtype),
                   jax.ShapeDtypeStruct((8, 128), F32)),
        in_specs=(HBM_SPEC, HBM_SPEC), out_specs=(SEM_SPEC, SEM_SPEC, HBM_SPEC, HBM_SPEC, VMEM_SPEC),
        input_output_aliases={0: 2, 1: 3},
        compiler_params=pltpu.CompilerParams(has_side_effects=pltpu.SideEffectType.DATAFLOW_SIDE_EFFECTING),
    )(hbm(g_in, pltpu.HBM), hbm(r_in, pltpu.HBM))


def _attn_cols_exchange_wait(send_sems, recv_sems, g_thru, r_thru, after):
    def body(g_ref, r_ref, send_sems, recv_sems, after_ref, g_dead, r_out, token):
        _GradExchange(g_ref, r_ref, send_sems, recv_sems, None, ATTN_COLS).finish()
        token[...] = jnp.zeros_like(token)

    return pl.pallas_call(
        body, name="attn_cols_exchange_wait",
        out_shape=(pltpu.HBM(g_thru.shape, g_thru.dtype), pltpu.HBM(r_thru.shape, r_thru.dtype),
                   jax.ShapeDtypeStruct((8, 128), F32)),
        in_specs=(HBM_SPEC, HBM_SPEC, SEM_SPEC, SEM_SPEC, ANY_SPEC), out_specs=(HBM_SPEC, HBM_SPEC, VMEM_SPEC),
        input_output_aliases={0: 0, 1: 1},
        compiler_params=pltpu.CompilerParams(has_side_effects=pltpu.SideEffectType.DATAFLOW_SIDE_EFFECTING),
    )(g_thru, r_thru, send_sems, recv_sems, after)


def _adamw_math(w, g, m, v):
    m = ADAM_B1 * m + (1.0 - ADAM_B1) * g
    v = ADAM_B2 * v + (1.0 - ADAM_B2) * (g * g)
    m_hat = m / (1.0 - ADAM_B1 ** ADAM_STEP)
    v_hat = v / (1.0 - ADAM_B2 ** ADAM_STEP)
    delta = -ADAM_LR * (m_hat / (jnp.sqrt(v_hat) + ADAM_EPS) + ADAM_WD * w)
    return delta, m, v


def _sum_adamw(parts, w, m, v, tm, name):
    R, C = w.shape
    tile = pl.BlockSpec((tm, C), lambda i: (i, 0))

    def body(p_ref, w_ref, m_ref, v_ref, g_out, d_out, m_out, v_out):
        g = p_ref[0].astype(F32)
        for s in range(1, N_DEV):
            g = g + p_ref[s].astype(F32)
        g_out[...] = g
        d_out[...], m_out[...], v_out[...] = _adamw_math(w_ref[...], g, m_ref[...], v_ref[...])

    shape = jax.ShapeDtypeStruct((R, C), F32)
    return pl.pallas_call(
        body, name=name, grid=(R // tm,),
        in_specs=[pl.BlockSpec((N_DEV, tm, C), lambda i: (0, i, 0)), tile, tile, tile],
        out_specs=(tile, tile, tile, tile), out_shape=(shape, shape, shape, shape),
        compiler_params=_params(1),
    )(parts, w, m, v)


def _adamw(g, w, m, v, name):
    def body(g_ref, w_ref, m_ref, v_ref, d_out, m_out, v_out):
        d_out[...], m_out[...], v_out[...] = _adamw_math(w_ref[...], g_ref[...], m_ref[...], v_ref[...])

    shape = jax.ShapeDtypeStruct(w.shape, F32)
    return pl.pallas_call(
        body, name=name, in_specs=[VMEM_SPEC] * 4, out_specs=(VMEM_SPEC,) * 3, out_shape=(shape, shape, shape),
    )(g, w, m, v)


def _alibi_slopes():
    return jnp.exp2(-8.0 * jnp.arange(1, N_HEADS + 1, dtype=F32) / N_HEADS)


def _local_step(x2, target, norm_g, b_merge, final_g, w_all, w3_all, cw_all):
    slopes = _alibi_slopes()
    u, ut = _norm(x2, norm_g)
    o, lse, q, k, v, w_all, w3_all, cw_all = _attn_fwd(u, slopes, w_all, w3_all, cw_all)
    w3 = jnp.transpose(w3_all, (1, 0, 2, 3)).reshape(3, D, D)
    cw8 = jnp.transpose(cw_all, (1, 0, 2)).reshape(8, D)
    pa = _proj_cols(u, w_all, SEG0_CONV, 4, BF, "proj_conv")
    yc_in = _conv_fwd(pa, cw8)
    pa_mid = _proj_cols(u, w_all, SEG0_MID, 3, BF, "proj_mid")
    dh, d_mid, do, dyc_in, g_3, small_mid = _mid(yc_in, pa_mid, o, x2, target, b_merge, final_g, w3)
    g_in = _wgrad_in(ut, d_mid, SEG0_MID, None, "wgrad_in_mid")
    d_conv, small_conv = _conv_bwd(dyc_in, pa, cw8)
    g_in = _wgrad_in(ut, d_conv, SEG0_CONV, g_in, "wgrad_in_conv")
    d_attn, r_in, r_3 = _attn_bwd(q, k, v, slopes, do, o, lse, g_in, g_3)
    g_in = _wgrad_in(ut, d_attn, SEG0_ATTN, g_in, "wgrad_in_attn")
    *in_flight, token = _attn_cols_exchange_start(g_in, r_in)
    grad_x, small_norm = _dgrad_norm_bwd(d_conv, d_attn, d_mid, w_all, x2, dh, norm_g + token[0:1, 0:1])
    return grad_x, in_flight, r_3, small_mid, small_conv, small_norm


def kernel(x, norm_g, w_in, b_merge, conv_w, w_out_conv, w_out_attn, w_o, final_g, loss_target, m_norm_g, m_w_in, m_b_merge, m_conv_w, m_w_out_conv, m_w_out_attn, m_w_o, m_final_g, v_norm_g, v_w_in, v_b_merge, v_conv_w, v_w_out_conv, v_w_out_attn, v_w_o, v_final_g):
    me = 4 * lax.axis_index("x") + 2 * lax.axis_index("y") + lax.axis_index("c")
    stack3 = lambda a, b, c: jnp.concatenate([a, b, c], axis=0)
    pad8 = lambda a: jnp.pad(a, ((0, 8 - a.shape[0]), (0, 0)))

    w3_shard = stack3(w_out_conv, w_out_attn, w_o)
    w_all, w3_all, cw_all = _gather_first_weights(w_in[0], w3_shard, pad8(conv_w[0]))

    final_g2 = final_g.reshape(1, D)
    grad_x, in_flight, r_3, small_mid, small_conv, small_norm = _local_step(
        x[0], loss_target[0], norm_g, b_merge, final_g2, w_all, w3_all, cw_all)

    g_in, r_in, token = _attn_cols_exchange_wait(*in_flight, small_norm)
    small = _allreduce_small(small_mid, small_conv, small_norm + token[:, 0:1])
    own = lax.dynamic_index_in_dim(g_in, me, 0, keepdims=True)
    r_in = lax.dynamic_update_slice(r_in, own, (me, 0, 0))

    g_w_in, d_w_in, nm_w_in, nv_w_in = _sum_adamw(r_in, w_in[0], m_w_in[0], v_w_in[0], 128, "adamw_w_in")
    g_w3, d_w3, nm_w3, nv_w3 = _sum_adamw(
        r_3.reshape(N_DEV, 3 * ROW_SHARD, D), w3_shard.reshape(3 * ROW_SHARD, D),
        stack3(m_w_out_conv, m_w_out_attn, m_w_o).reshape(3 * ROW_SHARD, D),
        stack3(v_w_out_conv, v_w_out_attn, v_w_o).reshape(3 * ROW_SHARD, D), ROW_SHARD, "adamw_w3")

    def pack(ng, bm, fg):
        return pad8(jnp.concatenate([ng, bm.reshape(2, D), fg.reshape(1, D)], axis=0))

    d_s, nm_s, nv_s = _adamw(small, pack(norm_g, b_merge, final_g), pack(m_norm_g, m_b_merge, m_final_g),
                             pack(v_norm_g, v_b_merge, v_final_g), "adamw_small")
    g_cw = lax.dynamic_slice(small, (4, me * ROW_SHARD), (3, ROW_SHARD))
    d_cw, nm_cw, nv_cw = _adamw(g_cw, conv_w[0], m_conv_w[0], v_conv_w[0], "adamw_conv_w")

    loss = small[7, 0]
    split3 = lambda t: tuple(t[a * ROW_SHARD:(a + 1) * ROW_SHARD][None] for a in range(3))
    unpack = lambda t: (t[0:1], t[1:3].reshape(1, 2 * D), t[3])

    def leaves(in_, small_, cw_, w3_):
        ng, bm, fg = unpack(small_)
        wc, wa, wo = split3(w3_)
        return (ng, in_[None], bm, cw_[None], wc, wa, wo, fg)

    return (loss, grad_x[None],
            *leaves(g_w_in, small, g_cw, g_w3),
            *leaves(d_w_in, d_s, d_cw, d_w3),
            *leaves(nm_w_in, nm_s, nm_cw, nm_w3),
            *leaves(nv_w_in, nv_s, nv_cw, nv_w3))
```

```python
import functools

import jax
import jax.numpy as jnp
from jax import lax
from jax.experimental import pallas as pl
from jax.experimental.pallas import tpu as pltpu

D = 1024
N_HEADS = 16
HEAD_DIM = 64
N_SEG = 10
IN_COLS = N_SEG * D
N_DEV = 8
W_IN_SHARD = IN_COLS // N_DEV
ROW_SHARD = D // N_DEV
QB = 128
DILATIONS = (1, 4, 16)
EPS = 1e-6
NEG = -1e30
BF = jnp.bfloat16
F32 = jnp.float32
MESH = pl.DeviceIdType.MESH

ADAM_LR = 0.001
ADAM_B1 = 0.9
ADAM_B2 = 0.999
ADAM_EPS = 1e-08
ADAM_WD = 0.01
ADAM_STEP = 10

V7X_VMEM_BYTES = 64 * 1024 * 1024
VMEM_LIMIT = V7X_VMEM_BYTES - 8 * 1024 * 1024
ROW_TILE = 256

VMEM_SPEC = pl.BlockSpec(memory_space=pltpu.VMEM)
ANY_SPEC = pl.BlockSpec(memory_space=pl.ANY)
SMEM_SPEC = pl.BlockSpec(memory_space=pltpu.SMEM)


def _params(n_grid_axes, vmem=VMEM_LIMIT):
    return pltpu.CompilerParams(dimension_semantics=("arbitrary",) * n_grid_axes, vmem_limit_bytes=vmem)


def _dot(a, b):
    return jnp.dot(a, b, preferred_element_type=F32)


def _dot_nt(a, b):
    return lax.dot_general(a, b, (((1,), (1,)), ((), ())), preferred_element_type=F32)


def _dot_tn(a, b):
    return lax.dot_general(a, b, (((0,), (0,)), ((), ())), preferred_element_type=F32)


def _sigmoid(z):
    return 1.0 / (1.0 + jnp.exp(-z))


def _my_place():
    x, y, c = lax.axis_index("x"), lax.axis_index("y"), lax.axis_index("c")
    return x, y, c, 4 * x + 2 * y + c


def _peers(x, y, c):
    out = []
    for k in range(1, N_DEV):
        px = 1 - x if k & 4 else x
        py = 1 - y if k & 2 else y
        pc = 1 - c if k & 1 else c
        out.append(((px, py, pc), 4 * px + 2 * py + pc))
    return out


def _device(p):
    return (p >> 2, (p >> 1) & 1, p & 1)


def _shard_cols(*ranges):
    def cols(p):
        found = None
        for lo, hi in ranges:
            a, b = max(lo, p * W_IN_SHARD), min(hi, (p + 1) * W_IN_SHARD)
            if a < b:
                assert found is None
                found = (a - p * W_IN_SHARD, b - p * W_IN_SHARD)
        return found

    return cols


def _whole(p):
    return ()


def _block(ref, idx, cols):
    return ref.at[idx] if cols == () else ref.at[idx, :, cols[0]:cols[1]]


class _WeightGather:
    def __init__(self, src, dst, send_sems, recv_sems, cols):
        self.src, self.dst, self.cols = src, dst, cols
        self.send_sems, self.recv_sems = send_sems, recv_sems
        self.me = _my_place()[3]

    def _copy(self, p, target):
        cols = self.cols(p)
        return pltpu.make_async_remote_copy(
            src_ref=self.src(p, cols), dst_ref=_block(self.dst, p, cols), send_sem=self.send_sems.at[target],
            recv_sem=self.recv_sems.at[p], device_id=_device(target), device_id_type=MESH)

    def _each(self, send, receive):
        for p in range(N_DEV):
            if self.cols(p) is None:
                continue

            def sender(p=p):
                for k in range(1, N_DEV):
                    send(self._copy(p, (p + k) % N_DEV))

            pl.when(self.me == p)(sender)
            pl.when(self.me != p)(lambda p=p: receive(self._copy(p, p)))

    def start(self):
        self._each(lambda cp: cp.start(), lambda cp: None)

    def finish(self):
        self._each(lambda cp: cp.wait_send(), lambda cp: cp.wait_recv())


WEIGHT_GATHER_SEMS = [pltpu.SemaphoreType.DMA((N_DEV,)), pltpu.SemaphoreType.DMA((N_DEV,))]
REST_COLS = _shard_cols((0, 4 * D), (7 * D, IN_COLS))
HEAD_PAIRS = D // 128


def _qkv_piece(h, seg):
    col = (4 + seg) * D + 128 * h
    return col // W_IN_SHARD, col % W_IN_SHARD


class _PieceGather:
    def __init__(self, src, dst, send_sems, recv_sems):
        self.src, self.dst, self.send_sems, self.recv_sems = src, dst, send_sems, recv_sems
        self.me = _my_place()[3]

    def _copy(self, i, target):
        p, lo = _qkv_piece(i // 3, i % 3)
        return pltpu.make_async_remote_copy(
            src_ref=self.src(p, lo, lo + 128), dst_ref=self.dst.at[p, :, lo:lo + 128], send_sem=self.send_sems.at[i, target],
            recv_sem=self.recv_sems.at[i], device_id=_device(target), device_id_type=MESH)

    def _owner(self, i, act):
        p = _qkv_piece(i // 3, i % 3)[0]

        def sender():
            for k in range(N_DEV - 1):
                act(self._copy(i, (p + 1 + (k + i) % (N_DEV - 1)) % N_DEV))

        pl.when(self.me == p)(sender)

    def start(self, pieces):
        for i in pieces:
            self._owner(i, lambda cp: cp.start())

    def wait_send(self, pieces):
        for i in pieces:
            self._owner(i, lambda cp: cp.wait_send())

    def wait_recv(self, pieces):
        for i in pieces:
            p = _qkv_piece(i // 3, i % 3)[0]
            pl.when(self.me != p)(lambda i=i, p=p: self._copy(i, p).wait_recv())


def _piece_sems(n):
    return [pltpu.SemaphoreType.DMA((n, N_DEV)), pltpu.SemaphoreType.DMA((n,))]


def _gather_first_weights(w_in, w3, cw):
    def body(w_in_ref, w3_ref, cw_ref, o_in, o_3, o_cw, in_bf, w3_bf, local_sems, *sems):
        me = _my_place()[3]

        def cast_rows(i, carry):
            r = pl.multiple_of(i * 128, 128)
            in_bf[pl.ds(r, 128), :] = w_in_ref[pl.ds(r, 128), :].astype(BF)
            return carry

        lax.fori_loop(0, D // 128, cast_rows, 0)
        for a in range(3):
            w3_bf[a] = w3_ref[a].astype(BF)
        gather = _PieceGather(lambda p, lo, hi: in_bf.at[:, lo:hi], o_in, *sems)
        gather.start(range(3))
        local = [pltpu.make_async_copy(src, dst.at[me], local_sems.at[a])
                 for a, (src, dst) in enumerate(((in_bf, o_in), (w3_bf, o_3), (cw_ref, o_cw)))]
        for cp in local:
            cp.start()
        gather.wait_recv(range(3))
        gather.wait_send(range(3))
        for cp in local:
            cp.wait()

    return pl.pallas_call(
        body, name="gather_first_weights",
        out_shape=(jax.ShapeDtypeStruct((N_DEV, D, W_IN_SHARD), BF),
                   jax.ShapeDtypeStruct((N_DEV, 3, ROW_SHARD, D), BF),
                   jax.ShapeDtypeStruct((N_DEV, 8, 128), F32)),
        in_specs=[VMEM_SPEC, VMEM_SPEC, VMEM_SPEC],
        out_specs=(ANY_SPEC, ANY_SPEC, ANY_SPEC),
        scratch_shapes=[pltpu.VMEM((D, W_IN_SHARD), BF), pltpu.VMEM((3, ROW_SHARD, D), BF),
                        pltpu.SemaphoreType.DMA((3,))] + _piece_sems(3),
        compiler_params=pltpu.CompilerParams(vmem_limit_bytes=VMEM_LIMIT),
    )(w_in, w3, cw)


class _GradExchange:
    def __init__(self, src, dst, send_sems, recv_sems, local_sem, cols):
        self.src, self.dst, self.cols = src, dst, cols
        self.send_sems, self.recv_sems, self.local_sem = send_sems, recv_sems, local_sem
        self.me = _my_place()[3]

    def _remote(self, p, source):
        return pltpu.make_async_remote_copy(
            src_ref=_block(self.src, p, self.cols(p)), dst_ref=_block(self.dst, source, self.cols(p)),
            send_sem=self.send_sems.at[p], recv_sem=self.recv_sems.at[source],
            device_id=_device(p), device_id_type=MESH)

    def _local(self, p):
        return pltpu.make_async_copy(_block(self.src, p, self.cols(p)), _block(self.dst, p, self.cols(p)),
                                     self.local_sem)

    def _as_each_device(self, send, local, receive):
        for m in range(N_DEV):
            def branch(m=m):
                for k in range(1, N_DEV):
                    p = (m + k) % N_DEV
                    if self.cols(p) is not None:
                        send(self._remote(p, m))
                if self.cols(m) is not None:
                    if self.local_sem is not None:
                        local(self._local(m))
                    for k in range(1, N_DEV):
                        receive(self._remote(m, (m + k) % N_DEV))

            pl.when(self.me == m)(branch)

    def start(self):
        self._as_each_device(lambda cp: cp.start(), lambda cp: cp.start(), lambda cp: None)

    def finish(self):
        self._as_each_device(lambda cp: cp.wait_send(), lambda cp: cp.wait(), lambda cp: cp.wait_recv())


GRAD_EXCHANGE_SEMS = [pltpu.SemaphoreType.DMA((N_DEV,)), pltpu.SemaphoreType.DMA((N_DEV,)), pltpu.SemaphoreType.DMA]


def _allreduce_small(p_mid, p_conv, p_norm):
    def body(a_ref, b_ref, c_ref, out_ref, mine, gathered, send_sems, recv_sems):
        x, y, c, me = _my_place()
        mine[...] = a_ref[...] + b_ref[...] + c_ref[...]
        gathered[me] = mine[...]
        remote = []
        for k, (peer, _) in enumerate(_peers(x, y, c)):
            cp = pltpu.make_async_remote_copy(
                src_ref=mine, dst_ref=gathered.at[me], send_sem=send_sems.at[k], recv_sem=recv_sems.at[k],
                device_id=peer, device_id_type=MESH)
            cp.start()
            remote.append(cp)
        for cp in remote:
            cp.wait()
        total = gathered[0]
        for s in range(1, N_DEV):
            total = total + gathered[s]
        out_ref[...] = total

    return pl.pallas_call(
        body, name="allreduce_small",
        out_shape=jax.ShapeDtypeStruct((8, D), F32),
        in_specs=[VMEM_SPEC, VMEM_SPEC, VMEM_SPEC], out_specs=VMEM_SPEC,
        scratch_shapes=[pltpu.VMEM((8, D), F32), pltpu.VMEM((N_DEV, 8, D), F32),
                        pltpu.SemaphoreType.DMA((N_DEV - 1,)), pltpu.SemaphoreType.DMA((N_DEV - 1,))],
    )(p_mid, p_conv, p_norm)


def _proj_pieces():
    cuts = sorted(set(range(0, IN_COLS + 1, D)) | set(range(0, IN_COLS + 1, W_IN_SHARD)))
    return [(lo // D, lo % D, lo // W_IN_SHARD, lo % W_IN_SHARD, hi - lo) for lo, hi in zip(cuts[:-1], cuts[1:])]


def _norm(x2, norm_g):
    S = x2.shape[0]
    tm = ROW_TILE

    def body(x_ref, g_ref, u_ref, ut_ref):
        xv = x_ref[...]
        r = lax.rsqrt(jnp.mean(xv * xv, axis=-1, keepdims=True) + EPS)
        u = xv * r * g_ref[...]
        u_ref[...] = u.astype(BF)
        ut_ref[...] = u.T.astype(BF)

    return pl.pallas_call(
        body, name="norm", grid=(S // tm,),
        in_specs=[pl.BlockSpec((tm, D), lambda i: (i, 0)), pl.BlockSpec((1, D), lambda i: (0, 0))],
        out_specs=(pl.BlockSpec((tm, D), lambda i: (i, 0)), pl.BlockSpec((D, tm), lambda i: (0, i))),
        out_shape=(jax.ShapeDtypeStruct((S, D), BF), jax.ShapeDtypeStruct((D, S), BF)),
        compiler_params=_params(1),
    )(x2, norm_g)


PROJ_TN = 256


def _proj_cols(u, w_all, seg0, n_seg, dtype, name):
    S = u.shape[0]
    tn = PROJ_TN
    per_shard = W_IN_SHARD // tn
    tile0 = seg0 * D // tn

    def body(u_ref, w_ref, out_ref):
        out_ref[...] = _dot(u_ref[...], w_ref[0]).astype(dtype)

    return pl.pallas_call(
        body, name=name, grid=(n_seg * D // tn,),
        in_specs=[VMEM_SPEC, pl.BlockSpec((1, D, tn), lambda t: ((tile0 + t) // per_shard, 0, (tile0 + t) % per_shard))],
        out_specs=pl.BlockSpec((S, tn), lambda t: (0, t)),
        out_shape=jax.ShapeDtypeStruct((S, n_seg * D), dtype),
        compiler_params=_params(1),
    )(u, w_all)


CONV_TM, CONV_TC = 256, 512
HALO = 16


def _conv_fwd(pa, cw8):
    S = pa.shape[0]
    tm, tc = CONV_TM, CONV_TC
    nct = D // tc

    def seg(s):
        return pl.BlockSpec((tm, tc), lambda i, j, s=s: (i, s * nct + j))

    def halo_before(s):
        return pl.BlockSpec((HALO, tc), lambda i, j, s=s: (jnp.maximum(i * (tm // HALO) - 1, 0), s * nct + j))

    def body(xc, bg, cg, zc, xch, cgh, cw, out):
        i = pl.program_id(0)
        a = cg[...].astype(F32) * xc[...].astype(F32)
        ah = cgh[...].astype(F32) * xch[...].astype(F32)
        ah = jnp.where(i > 0, ah, 0.0)
        row = lax.broadcasted_iota(jnp.int32, (tm, tc), 0)
        a1 = jnp.where(row == 0, ah[HALO - 1:HALO, :], pltpu.roll(a, 1, 0))
        a2 = jnp.where(row == 0, ah[HALO - 2:HALO - 1, :],
                       jnp.where(row == 1, ah[HALO - 1:HALO, :], pltpu.roll(a, 2, 0)))
        w = cw[...]
        conv = w[0:1, :] * a2 + w[1:2, :] * a1 + w[2:3, :] * a
        z = zc[...].astype(F32)
        out[...] = (z * _sigmoid(z) * bg[...].astype(F32) * conv).astype(BF)

    return pl.pallas_call(
        body, name="conv_fwd", grid=(S // tm, nct),
        in_specs=[seg(0), seg(1), seg(2), seg(3), halo_before(0), halo_before(2),
                  pl.BlockSpec((8, tc), lambda i, j: (0, j))],
        out_specs=pl.BlockSpec((tm, tc), lambda i, j: (i, j)),
        out_shape=jax.ShapeDtypeStruct((S, D), BF),
        compiler_params=_params(2),
    )(pa, pa, pa, pa, pa, pa, cw8)


ATT_UNROLL = 8


LAYOUT_MOD = 4
RUN = QB // LAYOUT_MOD


def _fold_masks(d):
    row = lax.broadcasted_iota(jnp.int32, (QB, QB), 0)
    lane = lax.broadcasted_iota(jnp.int32, (QB, QB), 1)
    if d == 1:
        qpos, kpos = LAYOUT_MOD * (row % RUN) + row // RUN, LAYOUT_MOD * (lane % RUN) + lane // RUN
    else:
        qpos, kpos = row, lane
    tri_le = kpos <= qpos
    dist = jnp.where(tri_le, qpos - kpos, qpos - kpos + QB).astype(F32)
    return tri_le, dist, lane < HEAD_DIM


class _Rows:
    def __init__(self, slices):
        self.slices = slices

    def get(self, ref):
        parts = [ref[sl, :] for sl in self.slices]
        return parts[0] if len(parts) == 1 else jnp.concatenate(parts, axis=0)

    def put(self, ref, val):
        size = QB // len(self.slices)
        for g, sl in enumerate(self.slices):
            ref[sl, :] = val if len(self.slices) == 1 else val[g * size:(g + 1) * size]

    def add(self, ref, val):
        self.put(ref, self.get(ref) + val)


def _block_rows(b, d, S):
    quarter = S // LAYOUT_MOD
    nb = S // (QB * d)
    r, n = b // nb, b % nb
    n_prev = jnp.maximum(n - 1, 0)
    if d == 1:
        runs = lambda m: _Rows([pl.ds(pl.multiple_of(g * quarter + RUN * m, RUN), RUN) for g in range(LAYOUT_MOD)])
        return n, runs(n), runs(n_prev)
    if d == LAYOUT_MOD:
        block = lambda m: _Rows([pl.ds(pl.multiple_of(r * quarter + QB * m, QB), QB)])
        return n, block(n), block(n_prev)
    step = d // LAYOUT_MOD
    first = (r % LAYOUT_MOD) * quarter + r // LAYOUT_MOD
    strided = lambda m: _Rows([pl.ds(first + QB * step * m, QB, stride=step)])
    return n, strided(n), strided(n_prev)


def _natural_rows(i, S):
    per = S // LAYOUT_MOD // QB
    return pl.ds(i // per + LAYOUT_MOD * QB * (i % per), QB, stride=LAYOUT_MOD)


def _head_sum_matrix():
    r = lax.broadcasted_iota(jnp.int32, (2 * QB, 2 * QB), 0)
    c = lax.broadcasted_iota(jnp.int32, (2 * QB, 2 * QB), 1)
    return (((r % QB) // HEAD_DIM) == (c // QB)).astype(F32).astype(BF)


def _hi_lo(t):
    hi = t.astype(BF)
    return jnp.concatenate([hi, (t - hi.astype(F32)).astype(BF)], axis=1)


PROJ_ROWS = 512


def _attn_fwd(u, slopes, w_all, w3_all, cw_all):
    S = u.shape[0]
    hpr = HEAD_PAIRS
    n_blocks = S // QB
    later = range(3, 3 * hpr)

    def body(sl_ref, u_ref, w_in_ref, w3_in_ref, cw_in_ref, o_ref, lse_ref, q_ref, k_ref, v_ref, w_ref, w3_ref,
             cw_ref, acc, m_s, l_s, w_tile, staged, tile_sems, *sems):
        hp = pl.program_id(0)
        me = _my_place()[3]
        pieces = _PieceGather(lambda p, lo, hi: w_ref.at[p, :, lo:hi], w_ref, *sems[0:2])
        gathers = (_WeightGather(lambda p, cols: _block(w_ref, me, cols), w_ref, *sems[2:4], REST_COLS),
                   _WeightGather(lambda p, cols: w3_ref.at[me], w3_ref, *sems[4:6], _whole),
                   _WeightGather(lambda p, cols: cw_ref.at[me], cw_ref, *sems[6:8], _whole))

        @pl.when(hp == 0)
        def _():
            pieces.start(later)
            for g in gathers:
                g.start()

        for h in range(hpr):
            @pl.when(hp == h)
            def _(h=h):
                if h > 0:
                    pieces.wait_recv(range(3 * h, 3 * h + 3))
                fetch = []
                for seg in range(3):
                    p, lo = _qkv_piece(h, seg)
                    fetch.append(pltpu.make_async_copy(w_ref.at[p, :, lo:lo + 128], w_tile.at[:, seg * 128:(seg + 1) * 128],
                                                       tile_sems.at[seg]))
                    fetch[-1].start()
                for cp in fetch:
                    cp.wait()

        def project(i, carry):
            rows = pl.ds(pl.multiple_of(i * PROJ_ROWS, PROJ_ROWS), PROJ_ROWS)
            qkv = _dot(u_ref[rows, :], w_tile[...])
            per = PROJ_ROWS // LAYOUT_MOD
            for seg, ref in enumerate((q_ref, k_ref, v_ref)):
                staged[seg] = qkv[:, seg * 128:(seg + 1) * 128]
                for g in range(LAYOUT_MOD):
                    dst = pl.ds(pl.multiple_of(g * (S // LAYOUT_MOD) + i * per, per), per)
                    ref[dst, :] = staged.at[seg][pl.ds(g, per, stride=LAYOUT_MOD), :]
            return carry

        lax.fori_loop(0, S // PROJ_ROWS, project, 0)

        head_sum = _head_sum_matrix()
        ones_b = jnp.ones((2 * QB, QB), BF)
        m_s[...] = jnp.full(m_s.shape, NEG, F32)
        l_s[...] = jnp.zeros(l_s.shape, F32)
        acc[...] = jnp.zeros(acc.shape, F32)

        for d in DILATIONS:
            tri_le, dist, low = _fold_masks(d)
            low_b = low.astype(F32).astype(BF)
            high_b = 1.0 - low_b
            slope = [sl_ref[2 * hp + a] * float(d) for a in range(2)]
            bias = [slope[a] * dist for a in range(2)]

            def block(b, d=d, slope=slope, bias=bias, tri_le=tri_le, low=low, low_b=low_b, high_b=high_b):
                n, cur, prev = _block_rows(b, d, S)
                has_prev = n > 0
                valid = jnp.logical_or(tri_le, has_prev)
                q2 = (cur.get(q_ref) * 0.125).astype(BF)
                qs = jnp.concatenate([q2 * low_b, q2 * high_b], axis=0)
                vp = prev.get(v_ref)
                kp_b = prev.get(k_ref).astype(BF)
                kcat = jnp.concatenate([kp_b, cur.get(k_ref).astype(BF)], axis=0)
                vcat = jnp.concatenate([vp, cur.get(v_ref)], axis=0).astype(BF)
                s2 = _dot_nt(qs, kcat)
                e2 = _dot(_hi_lo(q2.astype(F32) * kp_b.astype(F32)), head_sum)
                p_rows, alpha_h, pe_h = [], [], []
                for a in range(2):
                    sp, sc = s2[a * QB:(a + 1) * QB, :QB], s2[a * QB:(a + 1) * QB, QB:]
                    comb = jnp.where(valid, jnp.where(tri_le, sc, sp) - bias[a], NEG)
                    e = jnp.where(has_prev, e2[:, a * QB:(a + 1) * QB] - slope[a] * float(QB), NEG)
                    m_old = cur.get(m_s.at[a])
                    m_new = jnp.maximum(jnp.maximum(m_old, jnp.max(comb, axis=-1, keepdims=True)), e)
                    cur.put(m_s.at[a], m_new)
                    p = jnp.exp(comb - m_new)
                    pe_h.append(jnp.exp(e - m_new))
                    alpha_h.append(jnp.exp(m_old - m_new))
                    p_rows.append(jnp.concatenate([jnp.where(tri_le, 0.0, p).astype(BF),
                                                   jnp.where(tri_le, p, 0.0).astype(BF)], axis=1))
                pv = _dot(jnp.concatenate(p_rows, axis=0), jnp.concatenate([vcat, ones_b], axis=1))
                for a in range(2):
                    cur.put(l_s.at[a], alpha_h[a] * cur.get(l_s.at[a]) + pv[a * QB:(a + 1) * QB, QB:] + pe_h[a])
                cur.put(acc, jnp.where(low, alpha_h[0], alpha_h[1]) * cur.get(acc)
                        + jnp.where(low, pv[:QB, :QB], pv[QB:, :QB]) + jnp.where(low, pe_h[0], pe_h[1]) * vp)

            def several(it, carry, block=block):
                for u in range(ATT_UNROLL):
                    block(it * ATT_UNROLL + u)
                return carry

            lax.fori_loop(0, n_blocks // ATT_UNROLL, several, 0)

        low = _fold_masks(LAYOUT_MOD)[2]

        def finish(i, carry):
            rows = pl.ds(pl.multiple_of(i * QB, QB), QB)
            l0, l1 = l_s[0, rows, :], l_s[1, rows, :]
            o_ref[_natural_rows(i, S), :] = acc[rows, :] / jnp.where(low, l0, l1)
            lse_ref[0, rows, :] = m_s[0, rows, :] + jnp.log(l0)
            lse_ref[1, rows, :] = m_s[1, rows, :] + jnp.log(l1)
            return carry

        lax.fori_loop(0, n_blocks, finish, 0)

        @pl.when(hp == hpr - 1)
        def _():
            pieces.wait_send(later)
            for g in gathers:
                g.finish()

    col = pl.BlockSpec((S, 128), lambda h: (0, h))
    act = jax.ShapeDtypeStruct((S, D), F32)
    gathered = (w_all, w3_all, cw_all)
    return pl.pallas_call(
        body, name="attn_fwd", grid=(hpr,),
        in_specs=[SMEM_SPEC, VMEM_SPEC, ANY_SPEC, ANY_SPEC, ANY_SPEC],
        out_specs=(col, pl.BlockSpec((2, S, 128), lambda h: (0, 0, h)), col, col, col, ANY_SPEC, ANY_SPEC, ANY_SPEC),
        out_shape=(act, jax.ShapeDtypeStruct((2, S, D), F32), act, act, act,
                   *[jax.ShapeDtypeStruct(t.shape, t.dtype) for t in gathered]),
        scratch_shapes=([pltpu.VMEM((S, 128), F32), pltpu.VMEM((2, S, 128), F32), pltpu.VMEM((2, S, 128), F32),
                         pltpu.VMEM((D, 3 * 128), BF), pltpu.VMEM((3, PROJ_ROWS, 128), F32),
                         pltpu.SemaphoreType.DMA((3,))]
                        + _piece_sems(3 * hpr) + WEIGHT_GATHER_SEMS * 3),
        input_output_aliases={2: 5, 3: 6, 4: 7},
        compiler_params=_params(1),
    )(slopes, u, *gathered)


def _set_rows(shape, rows):
    idx = lax.broadcasted_iota(jnp.int32, shape, 0)
    out = jnp.zeros(shape, F32)
    for r, val in rows.items():
        out = out + jnp.where(idx == r, val, 0.0)
    return out


def _mid(yc_in, pa_mid, o, x2, target, b_merge, final_g, w3):
    S = x2.shape[0]
    tm = ROW_TILE
    nsteps = S // tm
    tile = pl.BlockSpec((tm, D), lambda i: (i, 0))

    def body(yc_ref, za_ref, gcp_ref, gap_ref, o_ref, x_ref, t_ref, b_ref, fg_ref, w_ref,
             dh_ref, dmid_ref, do_ref, dyc_ref, gw_ref, small_ref, acc, stage):
        i = pl.program_id(0)

        @pl.when(i == 0)
        def _():
            acc[...] = jnp.zeros_like(acc)
            small_ref[...] = jnp.zeros_like(small_ref)

        wc, wa, wo = w_ref[0], w_ref[1], w_ref[2]
        z = za_ref[...].astype(F32)
        sg = _sigmoid(z)
        ov = o_ref[...]
        yc_in_b, ya_in_b = yc_ref[...], (z * sg * ov).astype(BF)
        yc = _dot(yc_in_b, wc)
        ya = _dot(ya_in_b, wa)
        b = b_ref[...]
        gc = _sigmoid(gcp_ref[...].astype(F32) + b[:, :D])
        ga = _sigmoid(gap_ref[...].astype(F32) + b[:, D:])
        merged = gc * yc + ga * ya
        merged_b = merged.astype(BF)
        h = x_ref[...] + _dot(merged_b, wo)
        r2 = lax.rsqrt(jnp.mean(h * h, axis=-1, keepdims=True) + EPS)
        n = h * r2
        fg = fg_ref[...]
        err = n * fg - t_ref[...]
        loss = 0.5 * jnp.sum(jnp.sum(err * err, axis=-1, keepdims=True) / D, axis=0, keepdims=True)
        dy = err / D
        g_fg = jnp.sum(dy * n, axis=0, keepdims=True)
        dn = dy * fg
        dh = r2 * (dn - n * jnp.mean(dn * n, axis=-1, keepdims=True))
        dh_ref[...] = dh
        dh_b = dh.astype(BF)
        dmerged = _dot_nt(dh_b, wo)
        acc[2] += _dot(merged.T.astype(BF), dh_b)
        dyc = (dmerged * gc).astype(BF)
        dya = (dmerged * ga).astype(BF)
        dgcp = dmerged * yc * gc * (1.0 - gc)
        dgap = dmerged * ya * ga * (1.0 - ga)
        dmid_ref[1] = dgcp.astype(BF)
        dmid_ref[2] = dgap.astype(BF)
        acc[0] += _dot(yc_in_b.astype(F32).T.astype(BF), dyc)
        acc[1] += _dot(ya_in_b.astype(F32).T.astype(BF), dya)
        dyc_ref[...] = _dot_nt(dyc, wc).astype(BF)
        dya_in = _dot_nt(dya, wa)
        do_ref[...] = dya_in * (z * sg)
        dmid_ref[0] = (dya_in * ov * (sg * (1.0 + z * (1.0 - sg)))).astype(BF)
        small_ref[...] += _set_rows((8, D), {
            1: jnp.sum(dgcp, axis=0, keepdims=True), 2: jnp.sum(dgap, axis=0, keepdims=True),
            3: g_fg, 7: jnp.broadcast_to(loss, (1, D))})

        @pl.when(i == nsteps - 1)
        def _():
            for p in range(N_DEV):
                for a in range(3):
                    stage[...] = acc[a, p * ROW_SHARD:(p + 1) * ROW_SHARD, :].astype(BF)
                    pltpu.sync_copy(stage, gw_ref.at[p, a])

    return pl.pallas_call(
        body, name="mid", grid=(nsteps,),
        in_specs=[tile, pl.BlockSpec((tm, D), lambda i: (i, 0)), pl.BlockSpec((tm, D), lambda i: (i, 1)),
                  pl.BlockSpec((tm, D), lambda i: (i, 2)), tile, tile, tile,
                  pl.BlockSpec((1, 2 * D), lambda i: (0, 0)), pl.BlockSpec((1, D), lambda i: (0, 0)), VMEM_SPEC],
        out_specs=(tile, pl.BlockSpec((3, tm, D), lambda i: (0, i, 0)), tile, tile,
                   ANY_SPEC, pl.BlockSpec((8, D), lambda i: (0, 0))),
        out_shape=(jax.ShapeDtypeStruct((S, D), F32), jax.ShapeDtypeStruct((3, S, D), BF),
                   jax.ShapeDtypeStruct((S, D), F32), jax.ShapeDtypeStruct((S, D), BF),
                   jax.ShapeDtypeStruct((N_DEV, 3, ROW_SHARD, D), BF), jax.ShapeDtypeStruct((8, D), F32)),
        scratch_shapes=[pltpu.VMEM((3, D, D), F32), pltpu.VMEM((ROW_SHARD, D), BF)],
        compiler_params=_params(1),
    )(yc_in, pa_mid, pa_mid, pa_mid, o, x2, target, b_merge, final_g, w3)


def _conv_bwd(dyc_in, pa, cw8):
    S = pa.shape[0]
    tm, tc = CONV_TM, CONV_TC
    nct = D // tc
    nrt = S // tm
    last_halo = S // HALO - 1

    def seg(s):
        return pl.BlockSpec((tm, tc), lambda j, i, s=s: (i, s * nct + j))

    def halo_before(s):
        return pl.BlockSpec((HALO, tc), lambda j, i, s=s: (jnp.maximum(i * (tm // HALO) - 1, 0), s * nct + j))

    def halo_after(s):
        return pl.BlockSpec((HALO, tc), lambda j, i, s=s: (jnp.minimum((i + 1) * (tm // HALO), last_halo), s * nct + j))

    def body(dy, xc, bg, cg, zc, xch, cgh, dyn, bgn, zcn, cw, dout, gcw):
        i = pl.program_id(1)

        @pl.when(i == 0)
        def _():
            gcw[...] = jnp.zeros_like(gcw)

        xcv, cgv = xc[...].astype(F32), cg[...].astype(F32)
        a = cgv * xcv
        ah = jnp.where(i > 0, cgh[...].astype(F32) * xch[...].astype(F32), 0.0)
        row = lax.broadcasted_iota(jnp.int32, (tm, tc), 0)
        a1 = jnp.where(row == 0, ah[HALO - 1:HALO, :], pltpu.roll(a, 1, 0))
        a2 = jnp.where(row == 0, ah[HALO - 2:HALO - 1, :],
                       jnp.where(row == 1, ah[HALO - 1:HALO, :], pltpu.roll(a, 2, 0)))
        w = cw[...]
        conv = w[0:1, :] * a2 + w[1:2, :] * a1 + w[2:3, :] * a
        z = zc[...].astype(F32)
        sg = _sigmoid(z)
        silu = z * sg
        bgv = bg[...].astype(F32)
        dyv = dy[...].astype(F32)
        dout[3] = (dyv * bgv * conv * (sg * (1.0 + z * (1.0 - sg)))).astype(BF)
        dout[1] = (dyv * silu * conv).astype(BF)
        dc = dyv * silu * bgv
        zn = zcn[...].astype(F32)
        dcn = dyn[...].astype(F32) * (zn * _sigmoid(zn)) * bgn[...].astype(F32)
        dcn = jnp.where(i < nrt - 1, dcn, 0.0)
        dc1 = jnp.where(row == tm - 1, dcn[0:1, :], pltpu.roll(dc, tm - 1, 0))
        dc2 = jnp.where(row == tm - 1, dcn[1:2, :],
                        jnp.where(row == tm - 2, dcn[0:1, :], pltpu.roll(dc, tm - 2, 0)))
        da = w[2:3, :] * dc + w[1:2, :] * dc1 + w[0:1, :] * dc2
        dout[2] = (da * xcv).astype(BF)
        dout[0] = (da * cgv).astype(BF)
        gcw[...] += _set_rows((8, tc), {
            4: jnp.sum(dc * a2, axis=0, keepdims=True), 5: jnp.sum(dc * a1, axis=0, keepdims=True),
            6: jnp.sum(dc * a, axis=0, keepdims=True)})

    return pl.pallas_call(
        body, name="conv_bwd", grid=(nct, nrt),
        in_specs=[pl.BlockSpec((tm, tc), lambda j, i: (i, j)), seg(0), seg(1), seg(2), seg(3),
                  halo_before(0), halo_before(2),
                  pl.BlockSpec((HALO, tc), lambda j, i: (jnp.minimum((i + 1) * (tm // HALO), last_halo), j)),
                  halo_after(1), halo_after(3), pl.BlockSpec((8, tc), lambda j, i: (0, j))],
        out_specs=(pl.BlockSpec((4, tm, tc), lambda j, i: (0, i, j)), pl.BlockSpec((8, tc), lambda j, i: (0, j))),
        out_shape=(jax.ShapeDtypeStruct((4, S, D), BF), jax.ShapeDtypeStruct((8, D), F32)),
        compiler_params=_params(2),
    )(dyc_in, pa, pa, pa, pa, pa, pa, dyc_in, pa, pa, cw8)


def _attn_bwd(q, k, v, slopes, do, o, lse, g_in, g_3):
    S = q.shape[0]
    hpr = HEAD_PAIRS
    n_blocks = S // QB

    def body(sl_ref, q_ref, k_ref, v_ref, do_ref, o_ref, lse_ref, gin_ref, g3_ref, out_ref, rin_ref, r3_ref,
             dq_s, dk_s, dv_s, do_s, dd_s, *sems):
        hp = pl.program_id(0)
        exchanges = (_GradExchange(gin_ref, rin_ref, *sems[:3], _shard_cols((0, SEG0_ATTN * D), (SEG0_MID * D, IN_COLS))),
                     _GradExchange(g3_ref, r3_ref, *sems[3:], _whole))

        @pl.when(hp == 0)
        def _():
            for ex in exchanges:
                ex.start()

        head_sum = _head_sum_matrix()
        dq_s[...] = jnp.zeros(dq_s.shape, F32)
        dk_s[...] = jnp.zeros(dk_s.shape, F32)
        dv_s[...] = jnp.zeros(dv_s.shape, F32)

        def row_dots(i, carry):
            rows = pl.ds(pl.multiple_of(i * QB, QB), QB)
            natural = _natural_rows(i, S)
            do_c = do_ref[natural, :]
            do_s[rows, :] = do_c
            dd = _dot(_hi_lo(do_c * o_ref[natural, :]), head_sum)
            dd_s[0, rows, :] = dd[:, :QB]
            dd_s[1, rows, :] = dd[:, QB:]
            return carry

        lax.fori_loop(0, n_blocks, row_dots, 0)

        for d in DILATIONS:
            tri_le, dist, low = _fold_masks(d)
            low_b = low.astype(F32).astype(BF)
            high_b = 1.0 - low_b
            slope = [sl_ref[2 * hp + a] * float(d) for a in range(2)]
            bias = [slope[a] * dist for a in range(2)]

            def block(b, d=d, slope=slope, bias=bias, tri_le=tri_le, low=low, low_b=low_b, high_b=high_b):
                n, cur, prev = _block_rows(b, d, S)
                has_prev = n > 0
                valid = jnp.logical_or(tri_le, has_prev)
                q2f = cur.get(q_ref) * 0.125
                q2 = q2f.astype(BF)
                qs = jnp.concatenate([q2 * low_b, q2 * high_b], axis=0)
                kp, vp = prev.get(k_ref), prev.get(v_ref)
                kp_b, vp_b = kp.astype(BF), vp.astype(BF)
                kcat = jnp.concatenate([kp_b, cur.get(k_ref).astype(BF)], axis=0)
                vcat = jnp.concatenate([vp_b, cur.get(v_ref).astype(BF)], axis=0)
                do2f = cur.get(do_s)
                do2 = do2f.astype(BF)
                dos = jnp.concatenate([do2 * low_b, do2 * high_b], axis=0)
                s2 = _dot_nt(qs, kcat)
                dp2 = _dot_nt(dos, vcat)
                diag2 = _dot(jnp.concatenate([_hi_lo(q2.astype(F32) * kp_b.astype(F32)),
                                              _hi_lo(do2.astype(F32) * vp_b.astype(F32))], axis=0), head_sum)
                p_rows, ds_rows, pe_h, dse_h = [], [], [], []
                for a in range(2):
                    hs = slice(a * QB, (a + 1) * QB)
                    sp, sc = s2[hs, :QB], s2[hs, QB:]
                    dpp, dpc = dp2[hs, :QB], dp2[hs, QB:]
                    lse_a, dd_a = cur.get(lse_ref.at[a]), cur.get(dd_s.at[a])
                    comb = jnp.where(tri_le, sc, sp) - bias[a]
                    e = diag2[:QB, hs] - slope[a] * float(QB)
                    p = jnp.where(valid, jnp.exp(comb - lse_a), 0.0)
                    pe = jnp.where(has_prev, jnp.exp(e - lse_a), 0.0)
                    ds = p * (jnp.where(tri_le, dpc, dpp) - dd_a)
                    dse_h.append(pe * (diag2[QB:, hs] - dd_a))
                    pe_h.append(pe)
                    p_rows.append(jnp.concatenate([jnp.where(tri_le, 0.0, p).astype(BF),
                                                   jnp.where(tri_le, p, 0.0).astype(BF)], axis=1))
                    ds_rows.append(jnp.concatenate([jnp.where(tri_le, 0.0, ds).astype(BF),
                                                    jnp.where(tri_le, ds, 0.0).astype(BF)], axis=1))
                pst = jnp.concatenate(p_rows, axis=0)
                dst = jnp.concatenate(ds_rows, axis=0)
                pe2 = jnp.where(low, pe_h[0], pe_h[1])
                dse2 = jnp.where(low, dse_h[0], dse_h[1])
                dq = _dot(dst, kcat)
                cur.add(dq_s, (jnp.where(low, dq[:QB], dq[QB:]) + dse2 * kp) * 0.125)
                dk = _dot_tn(dst, qs)
                dv = _dot_tn(pst, dos)
                prev.add(dk_s, dk[:QB] + dse2 * q2f)
                cur.add(dk_s, dk[QB:])
                prev.add(dv_s, dv[:QB] + pe2 * do2f)
                cur.add(dv_s, dv[QB:])

            def several(it, carry, block=block):
                for u in range(ATT_UNROLL):
                    block(it * ATT_UNROLL + u)
                return carry

            lax.fori_loop(0, n_blocks // ATT_UNROLL, several, 0)

        def finish(i, carry):
            rows = pl.ds(pl.multiple_of(i * QB, QB), QB)
            natural = _natural_rows(i, S)
            for t, ref in enumerate((dq_s, dk_s, dv_s)):
                out_ref.at[t][natural, :] = ref[rows, :]
            return carry

        lax.fori_loop(0, n_blocks, finish, 0)

        @pl.when(hp == hpr - 1)
        def _():
            for ex in exchanges:
                ex.finish()

    col = pl.BlockSpec((S, 128), lambda h: (0, h))
    return pl.pallas_call(
        body, name="attn_bwd", grid=(hpr,),
        in_specs=[SMEM_SPEC, col, col, col, col, col, pl.BlockSpec((2, S, 128), lambda h: (0, 0, h)),
                  ANY_SPEC, ANY_SPEC],
        out_specs=(pl.BlockSpec((3, S, 128), lambda h: (0, 0, h)), ANY_SPEC, ANY_SPEC),
        out_shape=(jax.ShapeDtypeStruct((3, S, D), F32), jax.ShapeDtypeStruct(g_in.shape, BF),
                   jax.ShapeDtypeStruct(g_3.shape, BF)),
        scratch_shapes=([pltpu.VMEM((S, 128), F32)] * 4 + [pltpu.VMEM((2, S, 128), F32)]
                        + GRAD_EXCHANGE_SEMS + GRAD_EXCHANGE_SEMS),
        compiler_params=_params(1),
    )(slopes, q, k, v, do, o, lse, g_in, g_3)


WG_TN = 256
SEG0_CONV, SEG0_ATTN, SEG0_MID = 0, 4, 7


def _wgrad_in(ut, d_group, seg0, g_in, name):
    S = ut.shape[1]
    tn = WG_TN
    per_seg = D // tn
    per_shard = W_IN_SHARD // tn
    n_tiles = d_group.shape[0] * per_seg
    tile0 = seg0 * per_seg

    def body(ut_ref, d_ref, *rest):
        rest[-1][0] = _dot(ut_ref[...], d_ref[0].astype(BF)).astype(BF)

    operands, in_specs, aliases = [ut, d_group], [VMEM_SPEC, pl.BlockSpec((1, S, tn), lambda t: (t // per_seg, 0, t % per_seg))], {}
    if g_in is not None:
        operands.append(g_in)
        in_specs.append(ANY_SPEC)
        aliases = {2: 0}
    return pl.pallas_call(
        body, name=name, grid=(n_tiles,), in_specs=in_specs,
        out_specs=pl.BlockSpec((1, D, tn), lambda t: ((tile0 + t) // per_shard, 0, (tile0 + t) % per_shard)),
        out_shape=jax.ShapeDtypeStruct((N_DEV, D, W_IN_SHARD), BF),
        input_output_aliases=aliases,
        compiler_params=_params(1),
    )(*operands)


def _dgrad_norm_bwd(d_conv, d_attn, d_mid, w_all, x2, dh, norm_g):
    S = x2.shape[0]
    tm = ROW_TILE
    nsteps = S // tm
    tile = pl.BlockSpec((tm, D), lambda i: (i, 0))
    pieces = _proj_pieces()

    def body(a_ref, b_ref, c_ref, w_ref, x_ref, dh_ref, g_ref, gx_ref, small_ref):
        i = pl.program_id(0)

        @pl.when(i == 0)
        def _():
            small_ref[...] = jnp.zeros_like(small_ref)

        groups = (a_ref, b_ref, c_ref)
        du = jnp.zeros((tm, D), F32)
        for s, sc, p, pc, width in pieces:
            g = 0 if s < 4 else (1 if s < 7 else 2)
            local = s - (0, 4, 7)[g]
            du = du + _dot_nt(groups[g][local, :, sc:sc + width].astype(BF), w_ref[p, :, pc:pc + width])
        xv = x_ref[...]
        r = lax.rsqrt(jnp.mean(xv * xv, axis=-1, keepdims=True) + EPS)
        n = xv * r
        dn = du * g_ref[...]
        gx_ref[...] = dh_ref[...] + r * (dn - n * jnp.mean(dn * n, axis=-1, keepdims=True))
        small_ref[...] += _set_rows((8, D), {0: jnp.sum(du * n, axis=0, keepdims=True)})

    return pl.pallas_call(
        body, name="dgrad_norm_bwd", grid=(nsteps,),
        in_specs=[pl.BlockSpec((4, tm, D), lambda i: (0, i, 0)), pl.BlockSpec((3, tm, D), lambda i: (0, i, 0)),
                  pl.BlockSpec((3, tm, D), lambda i: (0, i, 0)), VMEM_SPEC, tile, tile,
                  pl.BlockSpec((1, D), lambda i: (0, 0))],
        out_specs=(tile, pl.BlockSpec((8, D), lambda i: (0, 0))),
        out_shape=(jax.ShapeDtypeStruct((S, D), F32), jax.ShapeDtypeStruct((8, D), F32)),
        compiler_params=_params(1),
    )(d_conv, d_attn, d_mid, w_all, x2, dh, norm_g)


HBM_SPEC = pl.BlockSpec(memory_space=pltpu.HBM)
SEM_SPEC = pl.BlockSpec(memory_space=pltpu.SEMAPHORE)
ATTN_COLS = _shard_cols((SEG0_ATTN * D, SEG0_MID * D))


def _attn_cols_exchange_start(g_in, r_in):
    def body(g_ref, r_ref, send_sems, recv_sems, g_thru, r_thru, token):
        _GradExchange(g_ref, r_ref, send_sems, recv_sems, None, ATTN_COLS).start()
        token[...] = jnp.zeros_like(token)

    hbm = pltpu.with_memory_space_constraint
    return pl.pallas_call(
        body, name="attn_cols_exchange_start",
        out_shape=(pltpu.SemaphoreType.DMA((N_DEV,)), pltpu.SemaphoreType.DMA((N_DEV,)),
                   pltpu.HBM(g_in.shape, g_in.dtype), pltpu.HBM(r_in.shape, r_in.dtype),
                   jax.ShapeDtypeStruct((8, 128), F32)),
        in_specs=(HBM_SPEC, HBM_SPEC), out_specs=(SEM_SPEC, SEM_SPEC, HBM_SPEC, HBM_SPEC, VMEM_SPEC),
        input_output_aliases={0: 2, 1: 3},
        compiler_params=pltpu.CompilerParams(has_side_effects=pltpu.SideEffectType.DATAFLOW_SIDE_EFFECTING),
    )(hbm(g_in, pltpu.HBM), hbm(r_in, pltpu.HBM))


def _attn_cols_exchange_wait(send_sems, recv_sems, g_thru, r_thru, after):
    def body(g_ref, r_ref, send_sems, recv_sems, after_ref, g_dead, r_out):
        _GradExchange(g_ref, r_ref, send_sems, recv_sems, None, ATTN_COLS).finish()

    return pl.pallas_call(
        body, name="attn_cols_exchange_wait",
        out_shape=(pltpu.HBM(g_thru.shape, g_thru.dtype), pltpu.HBM(r_thru.shape, r_thru.dtype)),
        in_specs=(HBM_SPEC, HBM_SPEC, SEM_SPEC, SEM_SPEC, ANY_SPEC), out_specs=(HBM_SPEC, HBM_SPEC),
        input_output_aliases={0: 0, 1: 1},
        compiler_params=pltpu.CompilerParams(has_side_effects=pltpu.SideEffectType.DATAFLOW_SIDE_EFFECTING),
    )(g_thru, r_thru, send_sems, recv_sems, after)


def _adamw_math(w, g, m, v):
    m = ADAM_B1 * m + (1.0 - ADAM_B1) * g
    v = ADAM_B2 * v + (1.0 - ADAM_B2) * (g * g)
    m_hat = m / (1.0 - ADAM_B1 ** ADAM_STEP)
    v_hat = v / (1.0 - ADAM_B2 ** ADAM_STEP)
    delta = -ADAM_LR * (m_hat / (jnp.sqrt(v_hat) + ADAM_EPS) + ADAM_WD * w)
    return delta, m, v


def _sum_adamw(parts, w, m, v, tm, name):
    R, C = w.shape
    tile = pl.BlockSpec((tm, C), lambda i: (i, 0))

    def body(p_ref, w_ref, m_ref, v_ref, g_out, d_out, m_out, v_out):
        g = p_ref[0].astype(F32)
        for s in range(1, N_DEV):
            g = g + p_ref[s].astype(F32)
        g_out[...] = g
        d_out[...], m_out[...], v_out[...] = _adamw_math(w_ref[...], g, m_ref[...], v_ref[...])

    shape = jax.ShapeDtypeStruct((R, C), F32)
    return pl.pallas_call(
        body, name=name, grid=(R // tm,),
        in_specs=[pl.BlockSpec((N_DEV, tm, C), lambda i: (0, i, 0)), tile, tile, tile],
        out_specs=(tile, tile, tile, tile), out_shape=(shape, shape, shape, shape),
        compiler_params=_params(1),
    )(parts, w, m, v)


def _adamw(g, w, m, v, name):
    def body(g_ref, w_ref, m_ref, v_ref, d_out, m_out, v_out):
        d_out[...], m_out[...], v_out[...] = _adamw_math(w_ref[...], g_ref[...], m_ref[...], v_ref[...])

    shape = jax.ShapeDtypeStruct(w.shape, F32)
    return pl.pallas_call(
        body, name=name, in_specs=[VMEM_SPEC] * 4, out_specs=(VMEM_SPEC,) * 3, out_shape=(shape, shape, shape),
    )(g, w, m, v)


def _alibi_slopes():
    return jnp.exp2(-8.0 * jnp.arange(1, N_HEADS + 1, dtype=F32) / N_HEADS)


def _local_step(x2, target, norm_g, b_merge, final_g, w_all, w3_all, cw_all):
    slopes = _alibi_slopes()
    u, ut = _norm(x2, norm_g)
    o, lse, q, k, v, w_all, w3_all, cw_all = _attn_fwd(u, slopes, w_all, w3_all, cw_all)
    w3 = jnp.transpose(w3_all, (1, 0, 2, 3)).reshape(3, D, D)
    cw8 = jnp.transpose(cw_all, (1, 0, 2)).reshape(8, D)
    pa = _proj_cols(u, w_all, SEG0_CONV, 4, BF, "proj_conv")
    yc_in = _conv_fwd(pa, cw8)
    pa_mid = _proj_cols(u, w_all, SEG0_MID, 3, BF, "proj_mid")
    dh, d_mid, do, dyc_in, g_3, small_mid = _mid(yc_in, pa_mid, o, x2, target, b_merge, final_g, w3)
    g_in = _wgrad_in(ut, d_mid, SEG0_MID, None, "wgrad_in_mid")
    d_conv, small_conv = _conv_bwd(dyc_in, pa, cw8)
    g_in = _wgrad_in(ut, d_conv, SEG0_CONV, g_in, "wgrad_in_conv")
    d_attn, r_in, r_3 = _attn_bwd(q, k, v, slopes, do, o, lse, g_in, g_3)
    g_in = _wgrad_in(ut, d_attn, SEG0_ATTN, g_in, "wgrad_in_attn")
    *in_flight, token = _attn_cols_exchange_start(g_in, r_in)
    grad_x, small_norm = _dgrad_norm_bwd(d_conv, d_attn, d_mid, w_all, x2, dh, norm_g + token[0:1, 0:1])
    return grad_x, in_flight, r_3, small_mid, small_conv, small_norm


def kernel(x, norm_g, w_in, b_merge, conv_w, w_out_conv, w_out_attn, w_o, final_g, loss_target, m_norm_g, m_w_in, m_b_merge, m_conv_w, m_w_out_conv, m_w_out_attn, m_w_o, m_final_g, v_norm_g, v_w_in, v_b_merge, v_conv_w, v_w_out_conv, v_w_out_attn, v_w_o, v_final_g):
    me = 4 * lax.axis_index("x") + 2 * lax.axis_index("y") + lax.axis_index("c")
    stack3 = lambda a, b, c: jnp.concatenate([a, b, c], axis=0)
    pad8 = lambda a: jnp.pad(a, ((0, 8 - a.shape[0]), (0, 0)))

    w3_shard = stack3(w_out_conv, w_out_attn, w_o)
    w_all, w3_all, cw_all = _gather_first_weights(w_in[0], w3_shard, pad8(conv_w[0]))

    final_g2 = final_g.reshape(1, D)
    grad_x, in_flight, r_3, small_mid, small_conv, small_norm = _local_step(
        x[0], loss_target[0], norm_g, b_merge, final_g2, w_all, w3_all, cw_all)

    small = _allreduce_small(small_mid, small_conv, small_norm)
    g_in, r_in = _attn_cols_exchange_wait(*in_flight, small)
    own = lax.dynamic_index_in_dim(g_in, me, 0, keepdims=True)
    r_in = lax.dynamic_update_slice(r_in, own, (me, 0, 0))

    g_w_in, d_w_in, nm_w_in, nv_w_in = _sum_adamw(r_in, w_in[0], m_w_in[0], v_w_in[0], 128, "adamw_w_in")
    g_w3, d_w3, nm_w3, nv_w3 = _sum_adamw(
        r_3.reshape(N_DEV, 3 * ROW_SHARD, D), w3_shard.reshape(3 * ROW_SHARD, D),
        stack3(m_w_out_conv, m_w_out_attn, m_w_o).reshape(3 * ROW_SHARD, D),
        stack3(v_w_out_conv, v_w_out_attn, v_w_o).reshape(3 * ROW_SHARD, D), ROW_SHARD, "adamw_w3")

    def pack(ng, bm, fg):
        return pad8(jnp.concatenate([ng, bm.reshape(2, D), fg.reshape(1, D)], axis=0))

    d_s, nm_s, nv_s = _adamw(small, pack(norm_g, b_merge, final_g), pack(m_norm_g, m_b_merge, m_final_g),
                             pack(v_norm_g, v_b_merge, v_final_g), "adamw_small")
    g_cw = lax.dynamic_slice(small, (4, me * ROW_SHARD), (3, ROW_SHARD))
    d_cw, nm_cw, nv_cw = _adamw(g_cw, conv_w[0], m_conv_w[0], v_conv_w[0], "adamw_conv_w")

    loss = small[7, 0]
    split3 = lambda t: tuple(t[a * ROW_SHARD:(a + 1) * ROW_SHARD][None] for a in range(3))
    unpack = lambda t: (t[0:1], t[1:3].reshape(1, 2 * D), t[3])

    def leaves(in_, small_, cw_, w3_):
        ng, bm, fg = unpack(small_)
        wc, wa, wo = split3(w3_)
        return (ng, in_[None], bm, cw_[None], wc, wa, wo, fg)

    return (loss, grad_x[None],
            *leaves(g_w_in, small, g_cw, g_w3),
            *leaves(d_w_in, d_s, d_cw, d_w3),
            *leaves(nm_w_in, nm_s, nm_cw, nm_w3),
            *leaves(nv_w_in, nv_s, nv_cw, nv_w3))
```

```python
import functools

import jax
import jax.numpy as jnp
from jax import lax
from jax.experimental import pallas as pl
from jax.experimental.pallas import tpu as pltpu

D = 1024
N_HEADS = 16
HEAD_DIM = 64
N_SEG = 10
IN_COLS = N_SEG * D
N_DEV = 8
W_IN_SHARD = IN_COLS // N_DEV
ROW_SHARD = D // N_DEV
QB = 128
DILATIONS = (1, 4, 16)
EPS = 1e-6
NEG = -1e30
BF = jnp.bfloat16
F32 = jnp.float32
MESH = pl.DeviceIdType.MESH

ADAM_LR = 0.001
ADAM_B1 = 0.9
ADAM_B2 = 0.999
ADAM_EPS = 1e-08
ADAM_WD = 0.01
ADAM_STEP = 10

V7X_VMEM_BYTES = 64 * 1024 * 1024
VMEM_LIMIT = V7X_VMEM_BYTES - 8 * 1024 * 1024
ROW_TILE = 256

VMEM_SPEC = pl.BlockSpec(memory_space=pltpu.VMEM)
ANY_SPEC = pl.BlockSpec(memory_space=pl.ANY)
SMEM_SPEC = pl.BlockSpec(memory_space=pltpu.SMEM)


def _params(n_grid_axes, vmem=VMEM_LIMIT):
    return pltpu.CompilerParams(dimension_semantics=("arbitrary",) * n_grid_axes, vmem_limit_bytes=vmem)


def _dot(a, b):
    return jnp.dot(a, b, preferred_element_type=F32)


def _dot_nt(a, b):
    return lax.dot_general(a, b, (((1,), (1,)), ((), ())), preferred_element_type=F32)


def _dot_tn(a, b):
    return lax.dot_general(a, b, (((0,), (0,)), ((), ())), preferred_element_type=F32)


def _sigmoid(z):
    return 1.0 / (1.0 + jnp.exp(-z))


def _my_place():
    x, y, c = lax.axis_index("x"), lax.axis_index("y"), lax.axis_index("c")
    return x, y, c, 4 * x + 2 * y + c


def _peers(x, y, c):
    out = []
    for k in range(1, N_DEV):
        px = 1 - x if k & 4 else x
        py = 1 - y if k & 2 else y
        pc = 1 - c if k & 1 else c
        out.append(((px, py, pc), 4 * px + 2 * py + pc))
    return out


def _device(p):
    return (p >> 2, (p >> 1) & 1, p & 1)


def _shard_cols(*ranges):
    def cols(p):
        found = None
        for lo, hi in ranges:
            a, b = max(lo, p * W_IN_SHARD), min(hi, (p + 1) * W_IN_SHARD)
            if a < b:
                assert found is None
                found = (a - p * W_IN_SHARD, b - p * W_IN_SHARD)
        return found

    return cols


def _whole(p):
    return ()


def _block(ref, idx, cols):
    return ref.at[idx] if cols == () else ref.at[idx, :, cols[0]:cols[1]]


class _WeightGather:
    def __init__(self, src, dst, send_sems, recv_sems, cols):
        self.src, self.dst, self.cols = src, dst, cols
        self.send_sems, self.recv_sems = send_sems, recv_sems
        self.me = _my_place()[3]

    def _copy(self, p, target):
        cols = self.cols(p)
        return pltpu.make_async_remote_copy(
            src_ref=self.src(p, cols), dst_ref=_block(self.dst, p, cols), send_sem=self.send_sems.at[target],
            recv_sem=self.recv_sems.at[p], device_id=_device(target), device_id_type=MESH)

    def _each(self, send, receive):
        for p in range(N_DEV):
            if self.cols(p) is None:
                continue

            def sender(p=p):
                for k in range(1, N_DEV):
                    send(self._copy(p, (p + k) % N_DEV))

            pl.when(self.me == p)(sender)
            pl.when(self.me != p)(lambda p=p: receive(self._copy(p, p)))

    def start(self):
        self._each(lambda cp: cp.start(), lambda cp: None)

    def finish(self):
        self._each(lambda cp: cp.wait_send(), lambda cp: cp.wait_recv())


WEIGHT_GATHER_SEMS = [pltpu.SemaphoreType.DMA((N_DEV,)), pltpu.SemaphoreType.DMA((N_DEV,))]
REST_COLS = _shard_cols((0, 4 * D), (7 * D, IN_COLS))
HEAD_PAIRS = D // 128


def _qkv_piece(h, seg):
    col = (4 + seg) * D + 128 * h
    return col // W_IN_SHARD, col % W_IN_SHARD


class _PieceGather:
    def __init__(self, src, dst, send_sems, recv_sems):
        self.src, self.dst, self.send_sems, self.recv_sems = src, dst, send_sems, recv_sems
        self.me = _my_place()[3]

    def _copy(self, i, target):
        p, lo = _qkv_piece(i // 3, i % 3)
        return pltpu.make_async_remote_copy(
            src_ref=self.src(p, lo, lo + 128), dst_ref=self.dst.at[p, :, lo:lo + 128], send_sem=self.send_sems.at[i, target],
            recv_sem=self.recv_sems.at[i], device_id=_device(target), device_id_type=MESH)

    def _owner(self, i, act):
        p = _qkv_piece(i // 3, i % 3)[0]

        def sender():
            for k in range(N_DEV - 1):
                act(self._copy(i, (p + 1 + (k + i) % (N_DEV - 1)) % N_DEV))

        pl.when(self.me == p)(sender)

    def start(self, pieces):
        for i in pieces:
            self._owner(i, lambda cp: cp.start())

    def wait_send(self, pieces):
        for i in pieces:
            self._owner(i, lambda cp: cp.wait_send())

    def wait_recv(self, pieces):
        for i in pieces:
            p = _qkv_piece(i // 3, i % 3)[0]
            pl.when(self.me != p)(lambda i=i, p=p: self._copy(i, p).wait_recv())


def _piece_sems(n):
    return [pltpu.SemaphoreType.DMA((n, N_DEV)), pltpu.SemaphoreType.DMA((n,))]


def _gather_first_weights(w_in, w3, cw):
    def body(w_in_ref, w3_ref, cw_ref, o_in, o_3, o_cw, in_bf, w3_bf, local_sems, *sems):
        me = _my_place()[3]

        def cast_rows(i, carry):
            r = pl.multiple_of(i * 128, 128)
            in_bf[pl.ds(r, 128), :] = w_in_ref[pl.ds(r, 128), :].astype(BF)
            return carry

        lax.fori_loop(0, D // 128, cast_rows, 0)
        for a in range(3):
            w3_bf[a] = w3_ref[a].astype(BF)
        gather = _PieceGather(lambda p, lo, hi: in_bf.at[:, lo:hi], o_in, *sems)
        gather.start(range(3))
        local = [pltpu.make_async_copy(src, dst.at[me], local_sems.at[a])
                 for a, (src, dst) in enumerate(((in_bf, o_in), (w3_bf, o_3), (cw_ref, o_cw)))]
        for cp in local:
            cp.start()
        gather.wait_recv(range(3))
        gather.wait_send(range(3))
        for cp in local:
            cp.wait()

    return pl.pallas_call(
        body, name="gather_first_weights",
        out_shape=(jax.ShapeDtypeStruct((N_DEV, D, W_IN_SHARD), BF),
                   jax.ShapeDtypeStruct((N_DEV, 3, ROW_SHARD, D), BF),
                   jax.ShapeDtypeStruct((N_DEV, 8, 128), F32)),
        in_specs=[VMEM_SPEC, VMEM_SPEC, VMEM_SPEC],
        out_specs=(ANY_SPEC, ANY_SPEC, ANY_SPEC),
        scratch_shapes=[pltpu.VMEM((D, W_IN_SHARD), BF), pltpu.VMEM((3, ROW_SHARD, D), BF),
                        pltpu.SemaphoreType.DMA((3,))] + _piece_sems(3),
        compiler_params=pltpu.CompilerParams(vmem_limit_bytes=VMEM_LIMIT),
    )(w_in, w3, cw)


class _GradExchange:
    def __init__(self, src, dst, send_sems, recv_sems, local_sem, cols):
        self.src, self.dst, self.cols = src, dst, cols
        self.send_sems, self.recv_sems, self.local_sem = send_sems, recv_sems, local_sem
        self.me = _my_place()[3]

    def _remote(self, p, source):
        return pltpu.make_async_remote_copy(
            src_ref=_block(self.src, p, self.cols(p)), dst_ref=_block(self.dst, source, self.cols(p)),
            send_sem=self.send_sems.at[p], recv_sem=self.recv_sems.at[source],
            device_id=_device(p), device_id_type=MESH)

    def _local(self, p):
        return pltpu.make_async_copy(_block(self.src, p, self.cols(p)), _block(self.dst, p, self.cols(p)),
                                     self.local_sem)

    def _as_each_device(self, send, local, receive):
        for m in range(N_DEV):
            def branch(m=m):
                for k in range(1, N_DEV):
                    p = (m + k) % N_DEV
                    if self.cols(p) is not None:
                        send(self._remote(p, m))
                if self.cols(m) is not None:
                    if self.local_sem is not None:
                        local(self._local(m))
                    for k in range(1, N_DEV):
                        receive(self._remote(m, (m + k) % N_DEV))

            pl.when(self.me == m)(branch)

    def start(self):
        self._as_each_device(lambda cp: cp.start(), lambda cp: cp.start(), lambda cp: None)

    def finish(self):
        self._as_each_device(lambda cp: cp.wait_send(), lambda cp: cp.wait(), lambda cp: cp.wait_recv())


GRAD_EXCHANGE_SEMS = [pltpu.SemaphoreType.DMA((N_DEV,)), pltpu.SemaphoreType.DMA((N_DEV,)), pltpu.SemaphoreType.DMA]


def _allreduce_small(p_mid, p_conv, p_norm):
    def body(a_ref, b_ref, c_ref, out_ref, mine, gathered, send_sems, recv_sems):
        x, y, c, me = _my_place()
        mine[...] = a_ref[...] + b_ref[...] + c_ref[...]
        gathered[me] = mine[...]
        remote = []
        for k, (peer, _) in enumerate(_peers(x, y, c)):
            cp = pltpu.make_async_remote_copy(
                src_ref=mine, dst_ref=gathered.at[me], send_sem=send_sems.at[k], recv_sem=recv_sems.at[k],
                device_id=peer, device_id_type=MESH)
            cp.start()
            remote.append(cp)
        for cp in remote:
            cp.wait()
        total = gathered[0]
        for s in range(1, N_DEV):
            total = total + gathered[s]
        out_ref[...] = total

    return pl.pallas_call(
        body, name="allreduce_small",
        out_shape=jax.ShapeDtypeStruct((8, D), F32),
        in_specs=[VMEM_SPEC, VMEM_SPEC, VMEM_SPEC], out_specs=VMEM_SPEC,
        scratch_shapes=[pltpu.VMEM((8, D), F32), pltpu.VMEM((N_DEV, 8, D), F32),
                        pltpu.SemaphoreType.DMA((N_DEV - 1,)), pltpu.SemaphoreType.DMA((N_DEV - 1,))],
    )(p_mid, p_conv, p_norm)


def _proj_pieces():
    cuts = sorted(set(range(0, IN_COLS + 1, D)) | set(range(0, IN_COLS + 1, W_IN_SHARD)))
    return [(lo // D, lo % D, lo // W_IN_SHARD, lo % W_IN_SHARD, hi - lo) for lo, hi in zip(cuts[:-1], cuts[1:])]


def _norm(x2, norm_g):
    S = x2.shape[0]
    tm = ROW_TILE

    def body(x_ref, g_ref, u_ref, ut_ref):
        xv = x_ref[...]
        r = lax.rsqrt(jnp.mean(xv * xv, axis=-1, keepdims=True) + EPS)
        u = xv * r * g_ref[...]
        u_ref[...] = u.astype(BF)
        ut_ref[...] = u.T.astype(BF)

    return pl.pallas_call(
        body, name="norm", grid=(S // tm,),
        in_specs=[pl.BlockSpec((tm, D), lambda i: (i, 0)), pl.BlockSpec((1, D), lambda i: (0, 0))],
        out_specs=(pl.BlockSpec((tm, D), lambda i: (i, 0)), pl.BlockSpec((D, tm), lambda i: (0, i))),
        out_shape=(jax.ShapeDtypeStruct((S, D), BF), jax.ShapeDtypeStruct((D, S), BF)),
        compiler_params=_params(1),
    )(x2, norm_g)


PROJ_TN = 256


def _proj_cols(u, w_all, seg0, n_seg, dtype, name):
    S = u.shape[0]
    tn = PROJ_TN
    per_shard = W_IN_SHARD // tn
    tile0 = seg0 * D // tn

    def body(u_ref, w_ref, out_ref):
        out_ref[...] = _dot(u_ref[...], w_ref[0]).astype(dtype)

    return pl.pallas_call(
        body, name=name, grid=(n_seg * D // tn,),
        in_specs=[VMEM_SPEC, pl.BlockSpec((1, D, tn), lambda t: ((tile0 + t) // per_shard, 0, (tile0 + t) % per_shard))],
        out_specs=pl.BlockSpec((S, tn), lambda t: (0, t)),
        out_shape=jax.ShapeDtypeStruct((S, n_seg * D), dtype),
        compiler_params=_params(1),
    )(u, w_all)


CONV_TM, CONV_TC = 256, 512
HALO = 16


def _conv_fwd(pa, cw8):
    S = pa.shape[0]
    tm, tc = CONV_TM, CONV_TC
    nct = D // tc

    def seg(s):
        return pl.BlockSpec((tm, tc), lambda i, j, s=s: (i, s * nct + j))

    def halo_before(s):
        return pl.BlockSpec((HALO, tc), lambda i, j, s=s: (jnp.maximum(i * (tm // HALO) - 1, 0), s * nct + j))

    def body(xc, bg, cg, zc, xch, cgh, cw, out):
        i = pl.program_id(0)
        a = cg[...].astype(F32) * xc[...].astype(F32)
        ah = cgh[...].astype(F32) * xch[...].astype(F32)
        ah = jnp.where(i > 0, ah, 0.0)
        row = lax.broadcasted_iota(jnp.int32, (tm, tc), 0)
        a1 = jnp.where(row == 0, ah[HALO - 1:HALO, :], pltpu.roll(a, 1, 0))
        a2 = jnp.where(row == 0, ah[HALO - 2:HALO - 1, :],
                       jnp.where(row == 1, ah[HALO - 1:HALO, :], pltpu.roll(a, 2, 0)))
        w = cw[...]
        conv = w[0:1, :] * a2 + w[1:2, :] * a1 + w[2:3, :] * a
        z = zc[...].astype(F32)
        out[...] = (z * _sigmoid(z) * bg[...].astype(F32) * conv).astype(BF)

    return pl.pallas_call(
        body, name="conv_fwd", grid=(S // tm, nct),
        in_specs=[seg(0), seg(1), seg(2), seg(3), halo_before(0), halo_before(2),
                  pl.BlockSpec((8, tc), lambda i, j: (0, j))],
        out_specs=pl.BlockSpec((tm, tc), lambda i, j: (i, j)),
        out_shape=jax.ShapeDtypeStruct((S, D), BF),
        compiler_params=_params(2),
    )(pa, pa, pa, pa, pa, pa, cw8)


ATT_UNROLL = 16


LAYOUT_MOD = 4
RUN = QB // LAYOUT_MOD


def _fold_masks(d):
    row = lax.broadcasted_iota(jnp.int32, (QB, QB), 0)
    lane = lax.broadcasted_iota(jnp.int32, (QB, QB), 1)
    if d == 1:
        qpos, kpos = LAYOUT_MOD * (row % RUN) + row // RUN, LAYOUT_MOD * (lane % RUN) + lane // RUN
    else:
        qpos, kpos = row, lane
    tri_le = kpos <= qpos
    dist = jnp.where(tri_le, qpos - kpos, qpos - kpos + QB).astype(F32)
    return tri_le, dist, lane < HEAD_DIM


class _Rows:
    def __init__(self, slices):
        self.slices = slices

    def get(self, ref):
        parts = [ref[sl, :] for sl in self.slices]
        return parts[0] if len(parts) == 1 else jnp.concatenate(parts, axis=0)

    def put(self, ref, val):
        size = QB // len(self.slices)
        for g, sl in enumerate(self.slices):
            ref[sl, :] = val if len(self.slices) == 1 else val[g * size:(g + 1) * size]

    def add(self, ref, val):
        self.put(ref, self.get(ref) + val)


def _block_rows(b, d, S):
    quarter = S // LAYOUT_MOD
    nb = S // (QB * d)
    r, n = b // nb, b % nb
    n_prev = jnp.maximum(n - 1, 0)
    if d == 1:
        runs = lambda m: _Rows([pl.ds(pl.multiple_of(g * quarter + RUN * m, RUN), RUN) for g in range(LAYOUT_MOD)])
        return n, runs(n), runs(n_prev)
    if d == LAYOUT_MOD:
        block = lambda m: _Rows([pl.ds(pl.multiple_of(r * quarter + QB * m, QB), QB)])
        return n, block(n), block(n_prev)
    step = d // LAYOUT_MOD
    first = (r % LAYOUT_MOD) * quarter + r // LAYOUT_MOD
    strided = lambda m: _Rows([pl.ds(first + QB * step * m, QB, stride=step)])
    return n, strided(n), strided(n_prev)


def _natural_rows(i, S):
    per = S // LAYOUT_MOD // QB
    return pl.ds(i // per + LAYOUT_MOD * QB * (i % per), QB, stride=LAYOUT_MOD)


def _head_sum_matrix():
    r = lax.broadcasted_iota(jnp.int32, (2 * QB, 2 * QB), 0)
    c = lax.broadcasted_iota(jnp.int32, (2 * QB, 2 * QB), 1)
    return (((r % QB) // HEAD_DIM) == (c // QB)).astype(F32).astype(BF)


def _hi_lo(t):
    hi = t.astype(BF)
    return jnp.concatenate([hi, (t - hi.astype(F32)).astype(BF)], axis=1)


PROJ_ROWS = 512


def _attn_fwd(u, slopes, w_all, w3_all, cw_all):
    S = u.shape[0]
    hpr = HEAD_PAIRS
    n_blocks = S // QB
    later = range(3, 3 * hpr)

    def body(sl_ref, u_ref, w_in_ref, w3_in_ref, cw_in_ref, o_ref, lse_ref, q_ref, k_ref, v_ref, w_ref, w3_ref,
             cw_ref, acc, m_s, l_s, w_tile, staged, tile_sems, *sems):
        hp = pl.program_id(0)
        me = _my_place()[3]
        pieces = _PieceGather(lambda p, lo, hi: w_ref.at[p, :, lo:hi], w_ref, *sems[0:2])
        gathers = (_WeightGather(lambda p, cols: _block(w_ref, me, cols), w_ref, *sems[2:4], REST_COLS),
                   _WeightGather(lambda p, cols: w3_ref.at[me], w3_ref, *sems[4:6], _whole),
                   _WeightGather(lambda p, cols: cw_ref.at[me], cw_ref, *sems[6:8], _whole))

        @pl.when(hp == 0)
        def _():
            pieces.start(later)
            for g in gathers:
                g.start()

        for h in range(hpr):
            @pl.when(hp == h)
            def _(h=h):
                if h > 0:
                    pieces.wait_recv(range(3 * h, 3 * h + 3))
                fetch = []
                for seg in range(3):
                    p, lo = _qkv_piece(h, seg)
                    fetch.append(pltpu.make_async_copy(w_ref.at[p, :, lo:lo + 128], w_tile.at[:, seg * 128:(seg + 1) * 128],
                                                       tile_sems.at[seg]))
                    fetch[-1].start()
                for cp in fetch:
                    cp.wait()

        def project(i, carry):
            rows = pl.ds(pl.multiple_of(i * PROJ_ROWS, PROJ_ROWS), PROJ_ROWS)
            qkv = _dot(u_ref[rows, :], w_tile[...])
            per = PROJ_ROWS // LAYOUT_MOD
            for seg, ref in enumerate((q_ref, k_ref, v_ref)):
                staged[seg] = qkv[:, seg * 128:(seg + 1) * 128]
                for g in range(LAYOUT_MOD):
                    dst = pl.ds(pl.multiple_of(g * (S // LAYOUT_MOD) + i * per, per), per)
                    ref[dst, :] = staged.at[seg][pl.ds(g, per, stride=LAYOUT_MOD), :]
            return carry

        lax.fori_loop(0, S // PROJ_ROWS, project, 0)

        head_sum = _head_sum_matrix()
        ones_b = jnp.ones((2 * QB, QB), BF)
        m_s[...] = jnp.full(m_s.shape, NEG, F32)
        l_s[...] = jnp.zeros(l_s.shape, F32)
        acc[...] = jnp.zeros(acc.shape, F32)

        for d in DILATIONS:
            tri_le, dist, low = _fold_masks(d)
            low_b = low.astype(F32).astype(BF)
            high_b = 1.0 - low_b
            slope = [sl_ref[2 * hp + a] * float(d) for a in range(2)]
            bias = [slope[a] * dist for a in range(2)]

            def block(b, d=d, slope=slope, bias=bias, tri_le=tri_le, low=low, low_b=low_b, high_b=high_b):
                n, cur, prev = _block_rows(b, d, S)
                has_prev = n > 0
                valid = jnp.logical_or(tri_le, has_prev)
                q2 = (cur.get(q_ref) * 0.125).astype(BF)
                qs = jnp.concatenate([q2 * low_b, q2 * high_b], axis=0)
                vp = prev.get(v_ref)
                kp_b = prev.get(k_ref).astype(BF)
                kcat = jnp.concatenate([kp_b, cur.get(k_ref).astype(BF)], axis=0)
                vcat = jnp.concatenate([vp, cur.get(v_ref)], axis=0).astype(BF)
                s2 = _dot_nt(qs, kcat)
                e2 = _dot(_hi_lo(q2.astype(F32) * kp_b.astype(F32)), head_sum)
                p_rows, alpha_h, pe_h = [], [], []
                for a in range(2):
                    sp, sc = s2[a * QB:(a + 1) * QB, :QB], s2[a * QB:(a + 1) * QB, QB:]
                    comb = jnp.where(valid, jnp.where(tri_le, sc, sp) - bias[a], NEG)
                    e = jnp.where(has_prev, e2[:, a * QB:(a + 1) * QB] - slope[a] * float(QB), NEG)
                    m_old = cur.get(m_s.at[a])
                    m_new = jnp.maximum(jnp.maximum(m_old, jnp.max(comb, axis=-1, keepdims=True)), e)
                    cur.put(m_s.at[a], m_new)
                    p = jnp.exp(comb - m_new)
                    pe_h.append(jnp.exp(e - m_new))
                    alpha_h.append(jnp.exp(m_old - m_new))
                    p_rows.append(jnp.concatenate([jnp.where(tri_le, 0.0, p).astype(BF),
                                                   jnp.where(tri_le, p, 0.0).astype(BF)], axis=1))
                pv = _dot(jnp.concatenate(p_rows, axis=0), jnp.concatenate([vcat, ones_b], axis=1))
                for a in range(2):
                    cur.put(l_s.at[a], alpha_h[a] * cur.get(l_s.at[a]) + pv[a * QB:(a + 1) * QB, QB:] + pe_h[a])
                cur.put(acc, jnp.where(low, alpha_h[0], alpha_h[1]) * cur.get(acc)
                        + jnp.where(low, pv[:QB, :QB], pv[QB:, :QB]) + jnp.where(low, pe_h[0], pe_h[1]) * vp)

            def several(it, carry, block=block):
                for u in range(ATT_UNROLL):
                    block(it * ATT_UNROLL + u)
                return carry

            lax.fori_loop(0, n_blocks // ATT_UNROLL, several, 0)

        low = _fold_masks(LAYOUT_MOD)[2]

        def finish(i, carry):
            rows = pl.ds(pl.multiple_of(i * QB, QB), QB)
            l0, l1 = l_s[0, rows, :], l_s[1, rows, :]
            o_ref[_natural_rows(i, S), :] = acc[rows, :] / jnp.where(low, l0, l1)
            lse_ref[0, rows, :] = m_s[0, rows, :] + jnp.log(l0)
            lse_ref[1, rows, :] = m_s[1, rows, :] + jnp.log(l1)
            return carry

        lax.fori_loop(0, n_blocks, finish, 0)

        @pl.when(hp == hpr - 1)
        def _():
            pieces.wait_send(later)
            for g in gathers:
                g.finish()

    col = pl.BlockSpec((S, 128), lambda h: (0, h))
    act = jax.ShapeDtypeStruct((S, D), F32)
    gathered = (w_all, w3_all, cw_all)
    return pl.pallas_call(
        body, name="attn_fwd", grid=(hpr,),
        in_specs=[SMEM_SPEC, VMEM_SPEC, ANY_SPEC, ANY_SPEC, ANY_SPEC],
        out_specs=(col, pl.BlockSpec((2, S, 128), lambda h: (0, 0, h)), col, col, col, ANY_SPEC, ANY_SPEC, ANY_SPEC),
        out_shape=(act, jax.ShapeDtypeStruct((2, S, D), F32), act, act, act,
                   *[jax.ShapeDtypeStruct(t.shape, t.dtype) for t in gathered]),
        scratch_shapes=([pltpu.VMEM((S, 128), F32), pltpu.VMEM((2, S, 128), F32), pltpu.VMEM((2, S, 128), F32),
                         pltpu.VMEM((D, 3 * 128), BF), pltpu.VMEM((3, PROJ_ROWS, 128), F32),
                         pltpu.SemaphoreType.DMA((3,))]
                        + _piece_sems(3 * hpr) + WEIGHT_GATHER_SEMS * 3),
        input_output_aliases={2: 5, 3: 6, 4: 7},
        compiler_params=_params(1),
    )(slopes, u, *gathered)


def _set_rows(shape, rows):
    idx = lax.broadcasted_iota(jnp.int32, shape, 0)
    out = jnp.zeros(shape, F32)
    for r, val in rows.items():
        out = out + jnp.where(idx == r, val, 0.0)
    return out


def _mid(yc_in, pa_mid, o, x2, target, b_merge, final_g, w3):
    S = x2.shape[0]
    tm = ROW_TILE
    nsteps = S // tm
    tile = pl.BlockSpec((tm, D), lambda i: (i, 0))

    def body(yc_ref, za_ref, gcp_ref, gap_ref, o_ref, x_ref, t_ref, b_ref, fg_ref, w_ref,
             dh_ref, dmid_ref, do_ref, dyc_ref, gw_ref, small_ref, acc, stage):
        i = pl.program_id(0)

        @pl.when(i == 0)
        def _():
            acc[...] = jnp.zeros_like(acc)
            small_ref[...] = jnp.zeros_like(small_ref)

        wc, wa, wo = w_ref[0], w_ref[1], w_ref[2]
        z = za_ref[...].astype(F32)
        sg = _sigmoid(z)
        ov = o_ref[...]
        yc_in_b, ya_in_b = yc_ref[...], (z * sg * ov).astype(BF)
        yc = _dot(yc_in_b, wc)
        ya = _dot(ya_in_b, wa)
        b = b_ref[...]
        gc = _sigmoid(gcp_ref[...].astype(F32) + b[:, :D])
        ga = _sigmoid(gap_ref[...].astype(F32) + b[:, D:])
        merged = gc * yc + ga * ya
        merged_b = merged.astype(BF)
        h = x_ref[...] + _dot(merged_b, wo)
        r2 = lax.rsqrt(jnp.mean(h * h, axis=-1, keepdims=True) + EPS)
        n = h * r2
        fg = fg_ref[...]
        err = n * fg - t_ref[...]
        loss = 0.5 * jnp.sum(jnp.sum(err * err, axis=-1, keepdims=True) / D, axis=0, keepdims=True)
        dy = err / D
        g_fg = jnp.sum(dy * n, axis=0, keepdims=True)
        dn = dy * fg
        dh = r2 * (dn - n * jnp.mean(dn * n, axis=-1, keepdims=True))
        dh_ref[...] = dh
        dh_b = dh.astype(BF)
        dmerged = _dot_nt(dh_b, wo)
        acc[2] += _dot(merged.T.astype(BF), dh_b)
        dyc = (dmerged * gc).astype(BF)
        dya = (dmerged * ga).astype(BF)
        dgcp = dmerged * yc * gc * (1.0 - gc)
        dgap = dmerged * ya * ga * (1.0 - ga)
        dmid_ref[1] = dgcp.astype(BF)
        dmid_ref[2] = dgap.astype(BF)
        acc[0] += _dot(yc_in_b.astype(F32).T.astype(BF), dyc)
        acc[1] += _dot(ya_in_b.astype(F32).T.astype(BF), dya)
        dyc_ref[...] = _dot_nt(dyc, wc).astype(BF)
        dya_in = _dot_nt(dya, wa)
        do_ref[...] = dya_in * (z * sg)
        dmid_ref[0] = (dya_in * ov * (sg * (1.0 + z * (1.0 - sg)))).astype(BF)
        small_ref[...] += _set_rows((8, D), {
            1: jnp.sum(dgcp, axis=0, keepdims=True), 2: jnp.sum(dgap, axis=0, keepdims=True),
            3: g_fg, 7: jnp.broadcast_to(loss, (1, D))})

        @pl.when(i == nsteps - 1)
        def _():
            for p in range(N_DEV):
                for a in range(3):
                    stage[...] = acc[a, p * ROW_SHARD:(p + 1) * ROW_SHARD, :].astype(BF)
                    pltpu.sync_copy(stage, gw_ref.at[p, a])

    return pl.pallas_call(
        body, name="mid", grid=(nsteps,),
        in_specs=[tile, pl.BlockSpec((tm, D), lambda i: (i, 0)), pl.BlockSpec((tm, D), lambda i: (i, 1)),
                  pl.BlockSpec((tm, D), lambda i: (i, 2)), tile, tile, tile,
                  pl.BlockSpec((1, 2 * D), lambda i: (0, 0)), pl.BlockSpec((1, D), lambda i: (0, 0)), VMEM_SPEC],
        out_specs=(tile, pl.BlockSpec((3, tm, D), lambda i: (0, i, 0)), tile, tile,
                   ANY_SPEC, pl.BlockSpec((8, D), lambda i: (0, 0))),
        out_shape=(jax.ShapeDtypeStruct((S, D), F32), jax.ShapeDtypeStruct((3, S, D), BF),
                   jax.ShapeDtypeStruct((S, D), F32), jax.ShapeDtypeStruct((S, D), BF),
                   jax.ShapeDtypeStruct((N_DEV, 3, ROW_SHARD, D), BF), jax.ShapeDtypeStruct((8, D), F32)),
        scratch_shapes=[pltpu.VMEM((3, D, D), F32), pltpu.VMEM((ROW_SHARD, D), BF)],
        compiler_params=_params(1),
    )(yc_in, pa_mid, pa_mid, pa_mid, o, x2, target, b_merge, final_g, w3)


def _conv_bwd(dyc_in, pa, cw8):
    S = pa.shape[0]
    tm, tc = CONV_TM, CONV_TC
    nct = D // tc
    nrt = S // tm
    last_halo = S // HALO - 1

    def seg(s):
        return pl.BlockSpec((tm, tc), lambda j, i, s=s: (i, s * nct + j))

    def halo_before(s):
        return pl.BlockSpec((HALO, tc), lambda j, i, s=s: (jnp.maximum(i * (tm // HALO) - 1, 0), s * nct + j))

    def halo_after(s):
        return pl.BlockSpec((HALO, tc), lambda j, i, s=s: (jnp.minimum((i + 1) * (tm // HALO), last_halo), s * nct + j))

    def body(dy, xc, bg, cg, zc, xch, cgh, dyn, bgn, zcn, cw, dout, gcw):
        i = pl.program_id(1)

        @pl.when(i == 0)
        def _():
            gcw[...] = jnp.zeros_like(gcw)

        xcv, cgv = xc[...].astype(F32), cg[...].astype(F32)
        a = cgv * xcv
        ah = jnp.where(i > 0, cgh[...].astype(F32) * xch[...].astype(F32), 0.0)
        row = lax.broadcasted_iota(jnp.int32, (tm, tc), 0)
        a1 = jnp.where(row == 0, ah[HALO - 1:HALO, :], pltpu.roll(a, 1, 0))
        a2 = jnp.where(row == 0, ah[HALO - 2:HALO - 1, :],
                       jnp.where(row == 1, ah[HALO - 1:HALO, :], pltpu.roll(a, 2, 0)))
        w = cw[...]
        conv = w[0:1, :] * a2 + w[1:2, :] * a1 + w[2:3, :] * a
        z = zc[...].astype(F32)
        sg = _sigmoid(z)
        silu = z * sg
        bgv = bg[...].astype(F32)
        dyv = dy[...].astype(F32)
        dout[3] = (dyv * bgv * conv * (sg * (1.0 + z * (1.0 - sg)))).astype(BF)
        dout[1] = (dyv * silu * conv).astype(BF)
        dc = dyv * silu * bgv
        zn = zcn[...].astype(F32)
        dcn = dyn[...].astype(F32) * (zn * _sigmoid(zn)) * bgn[...].astype(F32)
        dcn = jnp.where(i < nrt - 1, dcn, 0.0)
        dc1 = jnp.where(row == tm - 1, dcn[0:1, :], pltpu.roll(dc, tm - 1, 0))
        dc2 = jnp.where(row == tm - 1, dcn[1:2, :],
                        jnp.where(row == tm - 2, dcn[0:1, :], pltpu.roll(dc, tm - 2, 0)))
        da = w[2:3, :] * dc + w[1:2, :] * dc1 + w[0:1, :] * dc2
        dout[2] = (da * xcv).astype(BF)
        dout[0] = (da * cgv).astype(BF)
        gcw[...] += _set_rows((8, tc), {
            4: jnp.sum(dc * a2, axis=0, keepdims=True), 5: jnp.sum(dc * a1, axis=0, keepdims=True),
            6: jnp.sum(dc * a, axis=0, keepdims=True)})

    return pl.pallas_call(
        body, name="conv_bwd", grid=(nct, nrt),
        in_specs=[pl.BlockSpec((tm, tc), lambda j, i: (i, j)), seg(0), seg(1), seg(2), seg(3),
                  halo_before(0), halo_before(2),
                  pl.BlockSpec((HALO, tc), lambda j, i: (jnp.minimum((i + 1) * (tm // HALO), last_halo), j)),
                  halo_after(1), halo_after(3), pl.BlockSpec((8, tc), lambda j, i: (0, j))],
        out_specs=(pl.BlockSpec((4, tm, tc), lambda j, i: (0, i, j)), pl.BlockSpec((8, tc), lambda j, i: (0, j))),
        out_shape=(jax.ShapeDtypeStruct((4, S, D), BF), jax.ShapeDtypeStruct((8, D), F32)),
        compiler_params=_params(2),
    )(dyc_in, pa, pa, pa, pa, pa, pa, dyc_in, pa, pa, cw8)


def _attn_bwd(q, k, v, slopes, do, o, lse, g_in, g_3):
    S = q.shape[0]
    hpr = HEAD_PAIRS
    n_blocks = S // QB

    def body(sl_ref, q_ref, k_ref, v_ref, do_ref, o_ref, lse_ref, gin_ref, g3_ref, out_ref, rin_ref, r3_ref,
             dq_s, dk_s, dv_s, do_s, dd_s, *sems):
        hp = pl.program_id(0)
        exchanges = (_GradExchange(gin_ref, rin_ref, *sems[:3], _shard_cols((0, SEG0_ATTN * D), (SEG0_MID * D, IN_COLS))),
                     _GradExchange(g3_ref, r3_ref, *sems[3:], _whole))

        @pl.when(hp == 0)
        def _():
            for ex in exchanges:
                ex.start()

        head_sum = _head_sum_matrix()
        dq_s[...] = jnp.zeros(dq_s.shape, F32)
        dk_s[...] = jnp.zeros(dk_s.shape, F32)
        dv_s[...] = jnp.zeros(dv_s.shape, F32)

        def row_dots(i, carry):
            rows = pl.ds(pl.multiple_of(i * QB, QB), QB)
            natural = _natural_rows(i, S)
            do_c = do_ref[natural, :]
            do_s[rows, :] = do_c
            dd = _dot(_hi_lo(do_c * o_ref[natural, :]), head_sum)
            dd_s[0, rows, :] = dd[:, :QB]
            dd_s[1, rows, :] = dd[:, QB:]
            return carry

        lax.fori_loop(0, n_blocks, row_dots, 0)

        for d in DILATIONS:
            tri_le, dist, low = _fold_masks(d)
            low_b = low.astype(F32).astype(BF)
            high_b = 1.0 - low_b
            slope = [sl_ref[2 * hp + a] * float(d) for a in range(2)]
            bias = [slope[a] * dist for a in range(2)]

            def block(b, d=d, slope=slope, bias=bias, tri_le=tri_le, low=low, low_b=low_b, high_b=high_b):
                n, cur, prev = _block_rows(b, d, S)
                has_prev = n > 0
                valid = jnp.logical_or(tri_le, has_prev)
                q2f = cur.get(q_ref) * 0.125
                q2 = q2f.astype(BF)
                qs = jnp.concatenate([q2 * low_b, q2 * high_b], axis=0)
                kp, vp = prev.get(k_ref), prev.get(v_ref)
                kp_b, vp_b = kp.astype(BF), vp.astype(BF)
                kcat = jnp.concatenate([kp_b, cur.get(k_ref).astype(BF)], axis=0)
                vcat = jnp.concatenate([vp_b, cur.get(v_ref).astype(BF)], axis=0)
                do2f = cur.get(do_s)
                do2 = do2f.astype(BF)
                dos = jnp.concatenate([do2 * low_b, do2 * high_b], axis=0)
                s2 = _dot_nt(qs, kcat)
                dp2 = _dot_nt(dos, vcat)
                diag2 = _dot(jnp.concatenate([_hi_lo(q2.astype(F32) * kp_b.astype(F32)),
                                              _hi_lo(do2.astype(F32) * vp_b.astype(F32))], axis=0), head_sum)
                p_rows, ds_rows, pe_h, dse_h = [], [], [], []
                for a in range(2):
                    hs = slice(a * QB, (a + 1) * QB)
                    sp, sc = s2[hs, :QB], s2[hs, QB:]
                    dpp, dpc = dp2[hs, :QB], dp2[hs, QB:]
                    lse_a, dd_a = cur.get(lse_ref.at[a]), cur.get(dd_s.at[a])
                    comb = jnp.where(tri_le, sc, sp) - bias[a]
                    e = diag2[:QB, hs] - slope[a] * float(QB)
                    p = jnp.where(valid, jnp.exp(comb - lse_a), 0.0)
                    pe = jnp.where(has_prev, jnp.exp(e - lse_a), 0.0)
                    ds = p * (jnp.where(tri_le, dpc, dpp) - dd_a)
                    dse_h.append(pe * (diag2[QB:, hs] - dd_a))
                    pe_h.append(pe)
                    p_rows.append(jnp.concatenate([jnp.where(tri_le, 0.0, p).astype(BF),
                                                   jnp.where(tri_le, p, 0.0).astype(BF)], axis=1))
                    ds_rows.append(jnp.concatenate([jnp.where(tri_le, 0.0, ds).astype(BF),
                                                    jnp.where(tri_le, ds, 0.0).astype(BF)], axis=1))
                pst = jnp.concatenate(p_rows, axis=0)
                dst = jnp.concatenate(ds_rows, axis=0)
                pe2 = jnp.where(low, pe_h[0], pe_h[1])
                dse2 = jnp.where(low, dse_h[0], dse_h[1])
                dq = _dot(dst, kcat)
                cur.add(dq_s, (jnp.where(low, dq[:QB], dq[QB:]) + dse2 * kp) * 0.125)
                dk = _dot_tn(dst, qs)
                dv = _dot_tn(pst, dos)
                prev.add(dk_s, dk[:QB] + dse2 * q2f)
                cur.add(dk_s, dk[QB:])
                prev.add(dv_s, dv[:QB] + pe2 * do2f)
                cur.add(dv_s, dv[QB:])

            def several(it, carry, block=block):
                for u in range(ATT_UNROLL):
                    block(it * ATT_UNROLL + u)
                return carry

            lax.fori_loop(0, n_blocks // ATT_UNROLL, several, 0)

        def finish(i, carry):
            rows = pl.ds(pl.multiple_of(i * QB, QB), QB)
            natural = _natural_rows(i, S)
            for t, ref in enumerate((dq_s, dk_s, dv_s)):
                out_ref.at[t][natural, :] = ref[rows, :]
            return carry

        lax.fori_loop(0, n_blocks, finish, 0)

        @pl.when(hp == hpr - 1)
        def _():
            for ex in exchanges:
                ex.finish()

    col = pl.BlockSpec((S, 128), lambda h: (0, h))
    return pl.pallas_call(
        body, name="attn_bwd", grid=(hpr,),
        in_specs=[SMEM_SPEC, col, col, col, col, col, pl.BlockSpec((2, S, 128), lambda h: (0, 0, h)),
                  ANY_SPEC, ANY_SPEC],
        out_specs=(pl.BlockSpec((3, S, 128), lambda h: (0, 0, h)), ANY_SPEC, ANY_SPEC),
        out_shape=(jax.ShapeDtypeStruct((3, S, D), F32), jax.ShapeDtypeStruct(g_in.shape, BF),
                   jax.ShapeDtypeStruct(g_3.shape, BF)),
        scratch_shapes=([pltpu.VMEM((S, 128), F32)] * 4 + [pltpu.VMEM((2, S, 128), F32)]
                        + GRAD_EXCHANGE_SEMS + GRAD_EXCHANGE_SEMS),
        compiler_params=_params(1),
    )(slopes, q, k, v, do, o, lse, g_in, g_3)


WG_TN = 256
SEG0_CONV, SEG0_ATTN, SEG0_MID = 0, 4, 7


def _wgrad_in(ut, d_group, seg0, g_in, name):
    S = ut.shape[1]
    tn = WG_TN
    per_seg = D // tn
    per_shard = W_IN_SHARD // tn
    n_tiles = d_group.shape[0] * per_seg
    tile0 = seg0 * per_seg

    def body(ut_ref, d_ref, *rest):
        rest[-1][0] = _dot(ut_ref[...], d_ref[0].astype(BF)).astype(BF)

    operands, in_specs, aliases = [ut, d_group], [VMEM_SPEC, pl.BlockSpec((1, S, tn), lambda t: (t // per_seg, 0, t % per_seg))], {}
    if g_in is not None:
        operands.append(g_in)
        in_specs.append(ANY_SPEC)
        aliases = {2: 0}
    return pl.pallas_call(
        body, name=name, grid=(n_tiles,), in_specs=in_specs,
        out_specs=pl.BlockSpec((1, D, tn), lambda t: ((tile0 + t) // per_shard, 0, (tile0 + t) % per_shard)),
        out_shape=jax.ShapeDtypeStruct((N_DEV, D, W_IN_SHARD), BF),
        input_output_aliases=aliases,
        compiler_params=_params(1),
    )(*operands)


def _dgrad_norm_bwd(d_conv, d_attn, d_mid, w_all, x2, dh, norm_g):
    S = x2.shape[0]
    tm = ROW_TILE
    nsteps = S // tm
    tile = pl.BlockSpec((tm, D), lambda i: (i, 0))
    pieces = _proj_pieces()

    def body(a_ref, b_ref, c_ref, w_ref, x_ref, dh_ref, g_ref, gx_ref, small_ref):
        i = pl.program_id(0)

        @pl.when(i == 0)
        def _():
            small_ref[...] = jnp.zeros_like(small_ref)

        groups = (a_ref, b_ref, c_ref)
        du = jnp.zeros((tm, D), F32)
        for s, sc, p, pc, width in pieces:
            g = 0 if s < 4 else (1 if s < 7 else 2)
            local = s - (0, 4, 7)[g]
            du = du + _dot_nt(groups[g][local, :, sc:sc + width].astype(BF), w_ref[p, :, pc:pc + width])
        xv = x_ref[...]
        r = lax.rsqrt(jnp.mean(xv * xv, axis=-1, keepdims=True) + EPS)
        n = xv * r
        dn = du * g_ref[...]
        gx_ref[...] = dh_ref[...] + r * (dn - n * jnp.mean(dn * n, axis=-1, keepdims=True))
        small_ref[...] += _set_rows((8, D), {0: jnp.sum(du * n, axis=0, keepdims=True)})

    return pl.pallas_call(
        body, name="dgrad_norm_bwd", grid=(nsteps,),
        in_specs=[pl.BlockSpec((4, tm, D), lambda i: (0, i, 0)), pl.BlockSpec((3, tm, D), lambda i: (0, i, 0)),
                  pl.BlockSpec((3, tm, D), lambda i: (0, i, 0)), VMEM_SPEC, tile, tile,
                  pl.BlockSpec((1, D), lambda i: (0, 0))],
        out_specs=(tile, pl.BlockSpec((8, D), lambda i: (0, 0))),
        out_shape=(jax.ShapeDtypeStruct((S, D), F32), jax.ShapeDtypeStruct((8, D), F32)),
        compiler_params=_params(1),
    )(d_conv, d_attn, d_mid, w_all, x2, dh, norm_g)


HBM_SPEC = pl.BlockSpec(memory_space=pltpu.HBM)
SEM_SPEC = pl.BlockSpec(memory_space=pltpu.SEMAPHORE)
ATTN_COLS = _shard_cols((SEG0_ATTN * D, SEG0_MID * D))


def _attn_cols_exchange_start(g_in, r_in):
    def body(g_ref, r_ref, send_sems, recv_sems, g_thru, r_thru, token):
        _GradExchange(g_ref, r_ref, send_sems, recv_sems, None, ATTN_COLS).start()
        token[...] = jnp.zeros_like(token)

    hbm = pltpu.with_memory_space_constraint
    return pl.pallas_call(
        body, name="attn_cols_exchange_start",
        out_shape=(pltpu.SemaphoreType.DMA((N_DEV,)), pltpu.SemaphoreType.DMA((N_DEV,)),
                   pltpu.HBM(g_in.shape, g_in.dtype), pltpu.HBM(r_in.shape, r_in.dtype),
                   jax.ShapeDtypeStruct((8, 128), F32)),
        in_specs=(HBM_SPEC, HBM_SPEC), out_specs=(SEM_SPEC, SEM_SPEC, HBM_SPEC, HBM_SPEC, VMEM_SPEC),
        input_output_aliases={0: 2, 1: 3},
        compiler_params=pltpu.CompilerParams(has_side_effects=pltpu.SideEffectType.DATAFLOW_SIDE_EFFECTING),
    )(hbm(g_in, pltpu.HBM), hbm(r_in, pltpu.HBM))


def _attn_cols_exchange_wait(send_sems, recv_sems, g_thru, r_thru, after):
    def body(g_ref, r_ref, send_sems, recv_sems, after_ref, g_dead, r_out):
        _GradExchange(g_ref, r_ref, send_sems, recv_sems, None, ATTN_COLS).finish()

    return pl.pallas_call(
        body, name="attn_cols_exchange_wait",
        out_shape=(pltpu.HBM(g_thru.shape, g_thru.dtype), pltpu.HBM(r_thru.shape, r_thru.dtype)),
        in_specs=(HBM_SPEC, HBM_SPEC, SEM_SPEC, SEM_SPEC, ANY_SPEC), out_specs=(HBM_SPEC, HBM_SPEC),
        input_output_aliases={0: 0, 1: 1},
        compiler_params=pltpu.CompilerParams(has_side_effects=pltpu.SideEffectType.DATAFLOW_SIDE_EFFECTING),
    )(g_thru, r_thru, send_sems, recv_sems, after)


def _adamw_math(w, g, m, v):
    m = ADAM_B1 * m + (1.0 - ADAM_B1) * g
    v = ADAM_B2 * v + (1.0 - ADAM_B2) * (g * g)
    m_hat = m / (1.0 - ADAM_B1 ** ADAM_STEP)
    v_hat = v / (1.0 - ADAM_B2 ** ADAM_STEP)
    delta = -ADAM_LR * (m_hat / (jnp.sqrt(v_hat) + ADAM_EPS) + ADAM_WD * w)
    return delta, m, v


def _sum_adamw(parts, w, m, v, tm, name):
    R, C = w.shape
    tile = pl.BlockSpec((tm, C), lambda i: (i, 0))

    def body(p_ref, w_ref, m_ref, v_ref, g_out, d_out, m_out, v_out):
        g = p_ref[0].astype(F32)
        for s in range(1, N_DEV):
            g = g + p_ref[s].astype(F32)
        g_out[...] = g
        d_out[...], m_out[...], v_out[...] = _adamw_math(w_ref[...], g, m_ref[...], v_ref[...])

    shape = jax.ShapeDtypeStruct((R, C), F32)
    return pl.pallas_call(
        body, name=name, grid=(R // tm,),
        in_specs=[pl.BlockSpec((N_DEV, tm, C), lambda i: (0, i, 0)), tile, tile, tile],
        out_specs=(tile, tile, tile, tile), out_shape=(shape, shape, shape, shape),
        compiler_params=_params(1),
    )(parts, w, m, v)


def _adamw(g, w, m, v, name):
    def body(g_ref, w_ref, m_ref, v_ref, d_out, m_out, v_out):
        d_out[...], m_out[...], v_out[...] = _adamw_math(w_ref[...], g_ref[...], m_ref[...], v_ref[...])

    shape = jax.ShapeDtypeStruct(w.shape, F32)
    return pl.pallas_call(
        body, name=name, in_specs=[VMEM_SPEC] * 4, out_specs=(VMEM_SPEC,) * 3, out_shape=(shape, shape, shape),
    )(g, w, m, v)


def _alibi_slopes():
    return jnp.exp2(-8.0 * jnp.arange(1, N_HEADS + 1, dtype=F32) / N_HEADS)


def _local_step(x2, target, norm_g, b_merge, final_g, w_all, w3_all, cw_all):
    slopes = _alibi_slopes()
    u, ut = _norm(x2, norm_g)
    o, lse, q, k, v, w_all, w3_all, cw_all = _attn_fwd(u, slopes, w_all, w3_all, cw_all)
    w3 = jnp.transpose(w3_all, (1, 0, 2, 3)).reshape(3, D, D)
    cw8 = jnp.transpose(cw_all, (1, 0, 2)).reshape(8, D)
    pa = _proj_cols(u, w_all, SEG0_CONV, 4, BF, "proj_conv")
    yc_in = _conv_fwd(pa, cw8)
    pa_mid = _proj_cols(u, w_all, SEG0_MID, 3, BF, "proj_mid")
    dh, d_mid, do, dyc_in, g_3, small_mid = _mid(yc_in, pa_mid, o, x2, target, b_merge, final_g, w3)
    g_in = _wgrad_in(ut, d_mid, SEG0_MID, None, "wgrad_in_mid")
    d_conv, small_conv = _conv_bwd(dyc_in, pa, cw8)
    g_in = _wgrad_in(ut, d_conv, SEG0_CONV, g_in, "wgrad_in_conv")
    d_attn, r_in, r_3 = _attn_bwd(q, k, v, slopes, do, o, lse, g_in, g_3)
    g_in = _wgrad_in(ut, d_attn, SEG0_ATTN, g_in, "wgrad_in_attn")
    *in_flight, token = _attn_cols_exchange_start(g_in, r_in)
    grad_x, small_norm = _dgrad_norm_bwd(d_conv, d_attn, d_mid, w_all, x2, dh, norm_g + token[0:1, 0:1])
    return grad_x, in_flight, r_3, small_mid, small_conv, small_norm


def kernel(x, norm_g, w_in, b_merge, conv_w, w_out_conv, w_out_attn, w_o, final_g, loss_target, m_norm_g, m_w_in, m_b_merge, m_conv_w, m_w_out_conv, m_w_out_attn, m_w_o, m_final_g, v_norm_g, v_w_in, v_b_merge, v_conv_w, v_w_out_conv, v_w_out_attn, v_w_o, v_final_g):
    me = 4 * lax.axis_index("x") + 2 * lax.axis_index("y") + lax.axis_index("c")
    stack3 = lambda a, b, c: jnp.concatenate([a, b, c], axis=0)
    pad8 = lambda a: jnp.pad(a, ((0, 8 - a.shape[0]), (0, 0)))

    w3_shard = stack3(w_out_conv, w_out_attn, w_o)
    w_all, w3_all, cw_all = _gather_first_weights(w_in[0], w3_shard, pad8(conv_w[0]))

    final_g2 = final_g.reshape(1, D)
    grad_x, in_flight, r_3, small_mid, small_conv, small_norm = _local_step(
        x[0], loss_target[0], norm_g, b_merge, final_g2, w_all, w3_all, cw_all)

    small = _allreduce_small(small_mid, small_conv, small_norm)
    g_in, r_in = _attn_cols_exchange_wait(*in_flight, small)
    own = lax.dynamic_index_in_dim(g_in, me, 0, keepdims=True)
    r_in = lax.dynamic_update_slice(r_in, own, (me, 0, 0))

    g_w_in, d_w_in, nm_w_in, nv_w_in = _sum_adamw(r_in, w_in[0], m_w_in[0], v_w_in[0], 128, "adamw_w_in")
    g_w3, d_w3, nm_w3, nv_w3 = _sum_adamw(
        r_3.reshape(N_DEV, 3 * ROW_SHARD, D), w3_shard.reshape(3 * ROW_SHARD, D),
        stack3(m_w_out_conv, m_w_out_attn, m_w_o).reshape(3 * ROW_SHARD, D),
        stack3(v_w_out_conv, v_w_out_attn, v_w_o).reshape(3 * ROW_SHARD, D), ROW_SHARD, "adamw_w3")

    def pack(ng, bm, fg):
        return pad8(jnp.concatenate([ng, bm.reshape(2, D), fg.reshape(1, D)], axis=0))

    d_s, nm_s, nv_s = _adamw(small, pack(norm_g, b_merge, final_g), pack(m_norm_g, m_b_merge, m_final_g),
                             pack(v_norm_g, v_b_merge, v_final_g), "adamw_small")
    g_cw = lax.dynamic_slice(small, (4, me * ROW_SHARD), (3, ROW_SHARD))
    d_cw, nm_cw, nv_cw = _adamw(g_cw, conv_w[0], m_conv_w[0], v_conv_w[0], "adamw_conv_w")

    loss = small[7, 0]
    split3 = lambda t: tuple(t[a * ROW_SHARD:(a + 1) * ROW_SHARD][None] for a in range(3))
    unpack = lambda t: (t[0:1], t[1:3].reshape(1, 2 * D), t[3])

    def leaves(in_, small_, cw_, w3_):
        ng, bm, fg = unpack(small_)
        wc, wa, wo = split3(w3_)
        return (ng, in_[None], bm, cw_[None], wc, wa, wo, fg)

    return (loss, grad_x[None],
            *leaves(g_w_in, small, g_cw, g_w3),
            *leaves(d_w_in, d_s, d_cw, d_w3),
            *leaves(nm_w_in, nm_s, nm_cw, nm_w3),
            *leaves(nv_w_in, nv_s, nv_cw, nv_w3))
```

```python
import functools

import jax
import jax.numpy as jnp
from jax import lax
from jax.experimental import pallas as pl
from jax.experimental.pallas import tpu as pltpu

D = 1024
N_HEADS = 16
HEAD_DIM = 64
N_SEG = 10
IN_COLS = N_SEG * D
N_DEV = 8
W_IN_SHARD = IN_COLS // N_DEV
ROW_SHARD = D // N_DEV
QB = 128
DILATIONS = (1, 4, 16)
EPS = 1e-6
NEG = -1e30
BF = jnp.bfloat16
F32 = jnp.float32
MESH = pl.DeviceIdType.MESH

ADAM_LR = 0.001
ADAM_B1 = 0.9
ADAM_B2 = 0.999
ADAM_EPS = 1e-08
ADAM_WD = 0.01
ADAM_STEP = 10

V7X_VMEM_BYTES = 64 * 1024 * 1024
VMEM_LIMIT = V7X_VMEM_BYTES - 8 * 1024 * 1024
ROW_TILE = 256

VMEM_SPEC = pl.BlockSpec(memory_space=pltpu.VMEM)
ANY_SPEC = pl.BlockSpec(memory_space=pl.ANY)
SMEM_SPEC = pl.BlockSpec(memory_space=pltpu.SMEM)


def _params(n_grid_axes, vmem=VMEM_LIMIT):
    return pltpu.CompilerParams(dimension_semantics=("arbitrary",) * n_grid_axes, vmem_limit_bytes=vmem)


def _dot(a, b):
    return jnp.dot(a, b, preferred_element_type=F32)


def _dot_nt(a, b):
    return lax.dot_general(a, b, (((1,), (1,)), ((), ())), preferred_element_type=F32)


def _dot_tn(a, b):
    return lax.dot_general(a, b, (((0,), (0,)), ((), ())), preferred_element_type=F32)


def _sigmoid(z):
    return 1.0 / (1.0 + jnp.exp(-z))


def _my_place():
    x, y, c = lax.axis_index("x"), lax.axis_index("y"), lax.axis_index("c")
    return x, y, c, 4 * x + 2 * y + c


def _peers(x, y, c):
    out = []
    for k in range(1, N_DEV):
        px = 1 - x if k & 4 else x
        py = 1 - y if k & 2 else y
        pc = 1 - c if k & 1 else c
        out.append(((px, py, pc), 4 * px + 2 * py + pc))
    return out


def _device(p):
    return (p >> 2, (p >> 1) & 1, p & 1)


def _shard_cols(*ranges):
    def cols(p):
        found = None
        for lo, hi in ranges:
            a, b = max(lo, p * W_IN_SHARD), min(hi, (p + 1) * W_IN_SHARD)
            if a < b:
                assert found is None
                found = (a - p * W_IN_SHARD, b - p * W_IN_SHARD)
        return found

    return cols


def _whole(p):
    return ()


def _block(ref, idx, cols):
    return ref.at[idx] if cols == () else ref.at[idx, :, cols[0]:cols[1]]


class _WeightGather:
    def __init__(self, src, dst, send_sems, forward_sems, recv_sems, cols):
        self.src, self.dst, self.cols = src, dst, cols
        self.send_sems, self.forward_sems, self.recv_sems = send_sems, forward_sems, recv_sems
        self.me = _my_place()[3]

    def _copy(self, p, target, passing_on=False):
        cols = self.cols(p)
        return pltpu.make_async_remote_copy(
            src_ref=_block(self.dst, p, cols) if passing_on else self.src(p, cols), dst_ref=_block(self.dst, p, cols),
            send_sem=self.forward_sems.at[p] if passing_on else self.send_sems.at[target],
            recv_sem=self.recv_sems.at[p], device_id=_device(target), device_id_type=MESH)

    def _as_each_device(self, own, relayed, other):
        for m in range(N_DEV):
            def branch(m=m):
                for p in range(N_DEV):
                    if self.cols(p) is None:
                        continue
                    if p == m:
                        for t in [m ^ 1] + [q for q in range(N_DEV) if q >> 1 != m >> 1 and q & 1 == m & 1]:
                            own(self._copy(m, t))
                    elif p >> 1 != m >> 1 and p & 1 == m & 1:
                        relayed(p, m ^ 1)
                    else:
                        other(p)

            pl.when(self.me == m)(branch)

    def start(self):
        self._as_each_device(lambda cp: cp.start(), lambda p, t: None, lambda p: None)

    def forward(self):
        def pass_on(p, t):
            self._copy(p, p).wait_recv()
            self._copy(p, t, passing_on=True).start()

        self._as_each_device(lambda cp: None, pass_on, lambda p: None)

    def finish(self):
        self._as_each_device(lambda cp: cp.wait_send(), lambda p, t: self._copy(p, t, passing_on=True).wait_send(),
                             lambda p: self._copy(p, p).wait_recv())


WEIGHT_GATHER_SEMS = [pltpu.SemaphoreType.DMA((N_DEV,))] * 3
FORWARD_STEP = 4
REST_COLS = _shard_cols((0, 4 * D), (7 * D, IN_COLS))
HEAD_PAIRS = D // 128


def _qkv_piece(h, seg):
    col = (4 + seg) * D + 128 * h
    return col // W_IN_SHARD, col % W_IN_SHARD


class _PieceGather:
    def __init__(self, src, dst, send_sems, recv_sems):
        self.src, self.dst, self.send_sems, self.recv_sems = src, dst, send_sems, recv_sems
        self.me = _my_place()[3]

    def _copy(self, i, target):
        p, lo = _qkv_piece(i // 3, i % 3)
        return pltpu.make_async_remote_copy(
            src_ref=self.src(p, lo, lo + 128), dst_ref=self.dst.at[p, :, lo:lo + 128], send_sem=self.send_sems.at[i, target],
            recv_sem=self.recv_sems.at[i], device_id=_device(target), device_id_type=MESH)

    def _owner(self, i, act):
        p = _qkv_piece(i // 3, i % 3)[0]

        def sender():
            for k in range(N_DEV - 1):
                act(self._copy(i, (p + 1 + (k + i) % (N_DEV - 1)) % N_DEV))

        pl.when(self.me == p)(sender)

    def start(self, pieces):
        for i in pieces:
            self._owner(i, lambda cp: cp.start())

    def wait_send(self, pieces):
        for i in pieces:
            self._owner(i, lambda cp: cp.wait_send())

    def wait_recv(self, pieces):
        for i in pieces:
            p = _qkv_piece(i // 3, i % 3)[0]
            pl.when(self.me != p)(lambda i=i, p=p: self._copy(i, p).wait_recv())


def _piece_sems(n):
    return [pltpu.SemaphoreType.DMA((n, N_DEV)), pltpu.SemaphoreType.DMA((n,))]


def _gather_first_weights(w_in, w3, cw):
    def body(w_in_ref, w3_ref, cw_ref, o_in, o_3, o_cw, in_bf, w3_bf, local_sems, *sems):
        me = _my_place()[3]

        def cast_rows(i, carry):
            r = pl.multiple_of(i * 128, 128)
            in_bf[pl.ds(r, 128), :] = w_in_ref[pl.ds(r, 128), :].astype(BF)
            return carry

        lax.fori_loop(0, D // 128, cast_rows, 0)
        for a in range(3):
            w3_bf[a] = w3_ref[a].astype(BF)
        gather = _PieceGather(lambda p, lo, hi: in_bf.at[:, lo:hi], o_in, *sems)
        gather.start(range(3))
        local = [pltpu.make_async_copy(src, dst.at[me], local_sems.at[a])
                 for a, (src, dst) in enumerate(((in_bf, o_in), (w3_bf, o_3), (cw_ref, o_cw)))]
        for cp in local:
            cp.start()
        gather.wait_recv(range(3))
        gather.wait_send(range(3))
        for cp in local:
            cp.wait()

    return pl.pallas_call(
        body, name="gather_first_weights",
        out_shape=(jax.ShapeDtypeStruct((N_DEV, D, W_IN_SHARD), BF),
                   jax.ShapeDtypeStruct((N_DEV, 3, ROW_SHARD, D), BF),
                   jax.ShapeDtypeStruct((N_DEV, 8, 128), F32)),
        in_specs=[VMEM_SPEC, VMEM_SPEC, VMEM_SPEC],
        out_specs=(ANY_SPEC, ANY_SPEC, ANY_SPEC),
        scratch_shapes=[pltpu.VMEM((D, W_IN_SHARD), BF), pltpu.VMEM((3, ROW_SHARD, D), BF),
                        pltpu.SemaphoreType.DMA((3,))] + _piece_sems(3),
        compiler_params=pltpu.CompilerParams(vmem_limit_bytes=VMEM_LIMIT),
    )(w_in, w3, cw)


class _GradExchange:
    def __init__(self, src, dst, send_sems, recv_sems, local_sem, cols):
        self.src, self.dst, self.cols = src, dst, cols
        self.send_sems, self.recv_sems, self.local_sem = send_sems, recv_sems, local_sem
        self.me = _my_place()[3]

    def _remote(self, p, source):
        return pltpu.make_async_remote_copy(
            src_ref=_block(self.src, p, self.cols(p)), dst_ref=_block(self.dst, source, self.cols(p)),
            send_sem=self.send_sems.at[p], recv_sem=self.recv_sems.at[source],
            device_id=_device(p), device_id_type=MESH)

    def _local(self, p):
        return pltpu.make_async_copy(_block(self.src, p, self.cols(p)), _block(self.dst, p, self.cols(p)),
                                     self.local_sem)

    def _as_each_device(self, send, local, receive):
        for m in range(N_DEV):
            def branch(m=m):
                for k in range(1, N_DEV):
                    p = (m + k) % N_DEV
                    if self.cols(p) is not None:
                        send(self._remote(p, m))
                if self.cols(m) is not None:
                    if self.local_sem is not None:
                        local(self._local(m))
                    for k in range(1, N_DEV):
                        receive(self._remote(m, (m + k) % N_DEV))

            pl.when(self.me == m)(branch)

    def start(self):
        self._as_each_device(lambda cp: cp.start(), lambda cp: cp.start(), lambda cp: None)

    def finish(self):
        self._as_each_device(lambda cp: cp.wait_send(), lambda cp: cp.wait(), lambda cp: cp.wait_recv())


GRAD_EXCHANGE_SEMS = [pltpu.SemaphoreType.DMA((N_DEV,)), pltpu.SemaphoreType.DMA((N_DEV,)), pltpu.SemaphoreType.DMA]


def _allreduce_small(p_mid, p_conv, p_norm):
    def body(a_ref, b_ref, c_ref, out_ref, mine, gathered, send_sems, recv_sems):
        x, y, c, me = _my_place()
        mine[...] = a_ref[...] + b_ref[...] + c_ref[...]
        gathered[me] = mine[...]
        remote = []
        for k, (peer, _) in enumerate(_peers(x, y, c)):
            cp = pltpu.make_async_remote_copy(
                src_ref=mine, dst_ref=gathered.at[me], send_sem=send_sems.at[k], recv_sem=recv_sems.at[k],
                device_id=peer, device_id_type=MESH)
            cp.start()
            remote.append(cp)
        for cp in remote:
            cp.wait()
        total = gathered[0]
        for s in range(1, N_DEV):
            total = total + gathered[s]
        out_ref[...] = total

    return pl.pallas_call(
        body, name="allreduce_small",
        out_shape=jax.ShapeDtypeStruct((8, D), F32),
        in_specs=[VMEM_SPEC, VMEM_SPEC, VMEM_SPEC], out_specs=VMEM_SPEC,
        scratch_shapes=[pltpu.VMEM((8, D), F32), pltpu.VMEM((N_DEV, 8, D), F32),
                        pltpu.SemaphoreType.DMA((N_DEV - 1,)), pltpu.SemaphoreType.DMA((N_DEV - 1,))],
    )(p_mid, p_conv, p_norm)


def _proj_pieces():
    cuts = sorted(set(range(0, IN_COLS + 1, D)) | set(range(0, IN_COLS + 1, W_IN_SHARD)))
    return [(lo // D, lo % D, lo // W_IN_SHARD, lo % W_IN_SHARD, hi - lo) for lo, hi in zip(cuts[:-1], cuts[1:])]


def _norm(x2, norm_g):
    S = x2.shape[0]
    tm = ROW_TILE

    def body(x_ref, g_ref, u_ref, ut_ref):
        xv = x_ref[...]
        r = lax.rsqrt(jnp.mean(xv * xv, axis=-1, keepdims=True) + EPS)
        u = xv * r * g_ref[...]
        u_ref[...] = u.astype(BF)
        ut_ref[...] = u.T.astype(BF)

    return pl.pallas_call(
        body, name="norm", grid=(S // tm,),
        in_specs=[pl.BlockSpec((tm, D), lambda i: (i, 0)), pl.BlockSpec((1, D), lambda i: (0, 0))],
        out_specs=(pl.BlockSpec((tm, D), lambda i: (i, 0)), pl.BlockSpec((D, tm), lambda i: (0, i))),
        out_shape=(jax.ShapeDtypeStruct((S, D), BF), jax.ShapeDtypeStruct((D, S), BF)),
        compiler_params=_params(1),
    )(x2, norm_g)


PROJ_TN = 256


def _proj_cols(u, w_all, seg0, n_seg, dtype, name):
    S = u.shape[0]
    tn = PROJ_TN
    per_shard = W_IN_SHARD // tn
    tile0 = seg0 * D // tn

    def body(u_ref, w_ref, out_ref):
        out_ref[...] = _dot(u_ref[...], w_ref[0]).astype(dtype)

    return pl.pallas_call(
        body, name=name, grid=(n_seg * D // tn,),
        in_specs=[VMEM_SPEC, pl.BlockSpec((1, D, tn), lambda t: ((tile0 + t) // per_shard, 0, (tile0 + t) % per_shard))],
        out_specs=pl.BlockSpec((S, tn), lambda t: (0, t)),
        out_shape=jax.ShapeDtypeStruct((S, n_seg * D), dtype),
        compiler_params=_params(1),
    )(u, w_all)


CONV_TM, CONV_TC = 256, 512
HALO = 16


def _conv_fwd(pa, cw8):
    S = pa.shape[0]
    tm, tc = CONV_TM, CONV_TC
    nct = D // tc

    def seg(s):
        return pl.BlockSpec((tm, tc), lambda i, j, s=s: (i, s * nct + j))

    def halo_before(s):
        return pl.BlockSpec((HALO, tc), lambda i, j, s=s: (jnp.maximum(i * (tm // HALO) - 1, 0), s * nct + j))

    def body(xc, bg, cg, zc, xch, cgh, cw, out):
        i = pl.program_id(0)
        a = cg[...].astype(F32) * xc[...].astype(F32)
        ah = cgh[...].astype(F32) * xch[...].astype(F32)
        ah = jnp.where(i > 0, ah, 0.0)
        row = lax.broadcasted_iota(jnp.int32, (tm, tc), 0)
        a1 = jnp.where(row == 0, ah[HALO - 1:HALO, :], pltpu.roll(a, 1, 0))
        a2 = jnp.where(row == 0, ah[HALO - 2:HALO - 1, :],
                       jnp.where(row == 1, ah[HALO - 1:HALO, :], pltpu.roll(a, 2, 0)))
        w = cw[...]
        conv = w[0:1, :] * a2 + w[1:2, :] * a1 + w[2:3, :] * a
        z = zc[...].astype(F32)
        out[...] = (z * _sigmoid(z) * bg[...].astype(F32) * conv).astype(BF)

    return pl.pallas_call(
        body, name="conv_fwd", grid=(S // tm, nct),
        in_specs=[seg(0), seg(1), seg(2), seg(3), halo_before(0), halo_before(2),
                  pl.BlockSpec((8, tc), lambda i, j: (0, j))],
        out_specs=pl.BlockSpec((tm, tc), lambda i, j: (i, j)),
        out_shape=jax.ShapeDtypeStruct((S, D), BF),
        compiler_params=_params(2),
    )(pa, pa, pa, pa, pa, pa, cw8)


ATT_UNROLL = 16


LAYOUT_MOD = 4
RUN = QB // LAYOUT_MOD


def _fold_masks(d):
    row = lax.broadcasted_iota(jnp.int32, (QB, QB), 0)
    lane = lax.broadcasted_iota(jnp.int32, (QB, QB), 1)
    if d == 1:
        qpos, kpos = LAYOUT_MOD * (row % RUN) + row // RUN, LAYOUT_MOD * (lane % RUN) + lane // RUN
    else:
        qpos, kpos = row, lane
    tri_le = kpos <= qpos
    dist = jnp.where(tri_le, qpos - kpos, qpos - kpos + QB).astype(F32)
    return tri_le, dist, lane < HEAD_DIM


class _Rows:
    def __init__(self, slices):
        self.slices = slices

    def get(self, ref):
        parts = [ref[sl, :] for sl in self.slices]
        return parts[0] if len(parts) == 1 else jnp.concatenate(parts, axis=0)

    def put(self, ref, val):
        size = QB // len(self.slices)
        for g, sl in enumerate(self.slices):
            ref[sl, :] = val if len(self.slices) == 1 else val[g * size:(g + 1) * size]

    def add(self, ref, val):
        self.put(ref, self.get(ref) + val)


def _block_rows(b, d, S):
    quarter = S // LAYOUT_MOD
    nb = S // (QB * d)
    r, n = b // nb, b % nb
    n_prev = jnp.maximum(n - 1, 0)
    if d == 1:
        runs = lambda m: _Rows([pl.ds(pl.multiple_of(g * quarter + RUN * m, RUN), RUN) for g in range(LAYOUT_MOD)])
        return n, runs(n), runs(n_prev)
    if d == LAYOUT_MOD:
        block = lambda m: _Rows([pl.ds(pl.multiple_of(r * quarter + QB * m, QB), QB)])
        return n, block(n), block(n_prev)
    step = d // LAYOUT_MOD
    first = (r % LAYOUT_MOD) * quarter + r // LAYOUT_MOD
    strided = lambda m: _Rows([pl.ds(first + QB * step * m, QB, stride=step)])
    return n, strided(n), strided(n_prev)


def _natural_rows(i, S):
    per = S // LAYOUT_MOD // QB
    return pl.ds(i // per + LAYOUT_MOD * QB * (i % per), QB, stride=LAYOUT_MOD)


def _head_sum_matrix():
    r = lax.broadcasted_iota(jnp.int32, (2 * QB, 2 * QB), 0)
    c = lax.broadcasted_iota(jnp.int32, (2 * QB, 2 * QB), 1)
    return (((r % QB) // HEAD_DIM) == (c // QB)).astype(F32).astype(BF)


def _hi_lo(t):
    hi = t.astype(BF)
    return jnp.concatenate([hi, (t - hi.astype(F32)).astype(BF)], axis=1)


PROJ_ROWS = 512


def _attn_fwd(u, slopes, w_all, w3_all, cw_all):
    S = u.shape[0]
    hpr = HEAD_PAIRS
    n_blocks = S // QB
    later = range(3, 3 * hpr)

    def body(sl_ref, u_ref, w_in_ref, w3_in_ref, cw_in_ref, o_ref, lse_ref, q_ref, k_ref, v_ref, w_ref, w3_ref,
             cw_ref, acc, m_s, l_s, w_tile, staged, tile_sems, *sems):
        hp = pl.program_id(0)
        me = _my_place()[3]
        pieces = _PieceGather(lambda p, lo, hi: w_ref.at[p, :, lo:hi], w_ref, *sems[0:2])
        gathers = (_WeightGather(lambda p, cols: _block(w_ref, me, cols), w_ref, *sems[2:5], REST_COLS),
                   _WeightGather(lambda p, cols: w3_ref.at[me], w3_ref, *sems[5:8], _whole),
                   _WeightGather(lambda p, cols: cw_ref.at[me], cw_ref, *sems[8:11], _whole))

        @pl.when(hp == 0)
        def _():
            pieces.start(later)
            for g in gathers:
                g.start()

        @pl.when(hp == FORWARD_STEP)
        def _():
            for g in gathers:
                g.forward()

        for h in range(hpr):
            @pl.when(hp == h)
            def _(h=h):
                if h > 0:
                    pieces.wait_recv(range(3 * h, 3 * h + 3))
                fetch = []
                for seg in range(3):
                    p, lo = _qkv_piece(h, seg)
                    fetch.append(pltpu.make_async_copy(w_ref.at[p, :, lo:lo + 128], w_tile.at[:, seg * 128:(seg + 1) * 128],
                                                       tile_sems.at[seg]))
                    fetch[-1].start()
                for cp in fetch:
                    cp.wait()

        def project(i, carry):
            rows = pl.ds(pl.multiple_of(i * PROJ_ROWS, PROJ_ROWS), PROJ_ROWS)
            qkv = _dot(u_ref[rows, :], w_tile[...])
            per = PROJ_ROWS // LAYOUT_MOD
            for seg, ref in enumerate((q_ref, k_ref, v_ref)):
                staged[seg] = qkv[:, seg * 128:(seg + 1) * 128]
                for g in range(LAYOUT_MOD):
                    dst = pl.ds(pl.multiple_of(g * (S // LAYOUT_MOD) + i * per, per), per)
                    ref[dst, :] = staged.at[seg][pl.ds(g, per, stride=LAYOUT_MOD), :]
            return carry

        lax.fori_loop(0, S // PROJ_ROWS, project, 0)

        head_sum = _head_sum_matrix()
        ones_b = jnp.ones((2 * QB, QB), BF)
        m_s[...] = jnp.full(m_s.shape, NEG, F32)
        l_s[...] = jnp.zeros(l_s.shape, F32)
        acc[...] = jnp.zeros(acc.shape, F32)

        for d in DILATIONS:
            tri_le, dist, low = _fold_masks(d)
            low_b = low.astype(F32).astype(BF)
            high_b = 1.0 - low_b
            slope = [sl_ref[2 * hp + a] * float(d) for a in range(2)]
            bias = [slope[a] * dist for a in range(2)]

            def block(b, d=d, slope=slope, bias=bias, tri_le=tri_le, low=low, low_b=low_b, high_b=high_b):
                n, cur, prev = _block_rows(b, d, S)
                has_prev = n > 0
                valid = jnp.logical_or(tri_le, has_prev)
                q2 = (cur.get(q_ref) * 0.125).astype(BF)
                qs = jnp.concatenate([q2 * low_b, q2 * high_b], axis=0)
                vp = prev.get(v_ref)
                kp_b = prev.get(k_ref).astype(BF)
                kcat = jnp.concatenate([kp_b, cur.get(k_ref).astype(BF)], axis=0)
                vcat = jnp.concatenate([vp, cur.get(v_ref)], axis=0).astype(BF)
                s2 = _dot_nt(qs, kcat)
                e2 = _dot(_hi_lo(q2.astype(F32) * kp_b.astype(F32)), head_sum)
                p_rows, alpha_h, pe_h = [], [], []
                for a in range(2):
                    sp, sc = s2[a * QB:(a + 1) * QB, :QB], s2[a * QB:(a + 1) * QB, QB:]
                    comb = jnp.where(valid, jnp.where(tri_le, sc, sp) - bias[a], NEG)
                    e = jnp.where(has_prev, e2[:, a * QB:(a + 1) * QB] - slope[a] * float(QB), NEG)
                    m_old = cur.get(m_s.at[a])
                    m_new = jnp.maximum(jnp.maximum(m_old, jnp.max(comb, axis=-1, keepdims=True)), e)
                    cur.put(m_s.at[a], m_new)
                    p = jnp.exp(comb - m_new)
                    pe_h.append(jnp.exp(e - m_new))
                    alpha_h.append(jnp.exp(m_old - m_new))
                    p_rows.append(jnp.concatenate([jnp.where(tri_le, 0.0, p).astype(BF),
                                                   jnp.where(tri_le, p, 0.0).astype(BF)], axis=1))
                pv = _dot(jnp.concatenate(p_rows, axis=0), jnp.concatenate([vcat, ones_b], axis=1))
                for a in range(2):
                    cur.put(l_s.at[a], alpha_h[a] * cur.get(l_s.at[a]) + pv[a * QB:(a + 1) * QB, QB:] + pe_h[a])
                cur.put(acc, jnp.where(low, alpha_h[0], alpha_h[1]) * cur.get(acc)
                        + jnp.where(low, pv[:QB, :QB], pv[QB:, :QB]) + jnp.where(low, pe_h[0], pe_h[1]) * vp)

            def several(it, carry, block=block):
                for u in range(ATT_UNROLL):
                    block(it * ATT_UNROLL + u)
                return carry

            lax.fori_loop(0, n_blocks // ATT_UNROLL, several, 0)

        low = _fold_masks(LAYOUT_MOD)[2]

        def finish(i, carry):
            rows = pl.ds(pl.multiple_of(i * QB, QB), QB)
            l0, l1 = l_s[0, rows, :], l_s[1, rows, :]
            o_ref[_natural_rows(i, S), :] = acc[rows, :] / jnp.where(low, l0, l1)
            lse_ref[0, rows, :] = m_s[0, rows, :] + jnp.log(l0)
            lse_ref[1, rows, :] = m_s[1, rows, :] + jnp.log(l1)
            return carry

        lax.fori_loop(0, n_blocks, finish, 0)

        @pl.when(hp == hpr - 1)
        def _():
            pieces.wait_send(later)
            for g in gathers:
                g.finish()

    col = pl.BlockSpec((S, 128), lambda h: (0, h))
    act = jax.ShapeDtypeStruct((S, D), F32)
    gathered = (w_all, w3_all, cw_all)
    return pl.pallas_call(
        body, name="attn_fwd", grid=(hpr,),
        in_specs=[SMEM_SPEC, VMEM_SPEC, ANY_SPEC, ANY_SPEC, ANY_SPEC],
        out_specs=(col, pl.BlockSpec((2, S, 128), lambda h: (0, 0, h)), col, col, col, ANY_SPEC, ANY_SPEC, ANY_SPEC),
        out_shape=(act, jax.ShapeDtypeStruct((2, S, D), F32), act, act, act,
                   *[jax.ShapeDtypeStruct(t.shape, t.dtype) for t in gathered]),
        scratch_shapes=([pltpu.VMEM((S, 128), F32), pltpu.VMEM((2, S, 128), F32), pltpu.VMEM((2, S, 128), F32),
                         pltpu.VMEM((D, 3 * 128), BF), pltpu.VMEM((3, PROJ_ROWS, 128), F32),
                         pltpu.SemaphoreType.DMA((3,))]
                        + _piece_sems(3 * hpr) + WEIGHT_GATHER_SEMS * 3),
        input_output_aliases={2: 5, 3: 6, 4: 7},
        compiler_params=_params(1),
    )(slopes, u, *gathered)


def _set_rows(shape, rows):
    idx = lax.broadcasted_iota(jnp.int32, shape, 0)
    out = jnp.zeros(shape, F32)
    for r, val in rows.items():
        out = out + jnp.where(idx == r, val, 0.0)
    return out


def _mid(yc_in, pa_mid, o, x2, target, b_merge, final_g, w3):
    S = x2.shape[0]
    tm = ROW_TILE
    nsteps = S // tm
    tile = pl.BlockSpec((tm, D), lambda i: (i, 0))

    def body(yc_ref, za_ref, gcp_ref, gap_ref, o_ref, x_ref, t_ref, b_ref, fg_ref, w_ref,
             dh_ref, dmid_ref, do_ref, dyc_ref, gw_ref, small_ref, acc, stage):
        i = pl.program_id(0)

        @pl.when(i == 0)
        def _():
            acc[...] = jnp.zeros_like(acc)
            small_ref[...] = jnp.zeros_like(small_ref)

        wc, wa, wo = w_ref[0], w_ref[1], w_ref[2]
        z = za_ref[...].astype(F32)
        sg = _sigmoid(z)
        ov = o_ref[...]
        yc_in_b, ya_in_b = yc_ref[...], (z * sg * ov).astype(BF)
        yc = _dot(yc_in_b, wc)
        ya = _dot(ya_in_b, wa)
        b = b_ref[...]
        gc = _sigmoid(gcp_ref[...].astype(F32) + b[:, :D])
        ga = _sigmoid(gap_ref[...].astype(F32) + b[:, D:])
        merged = gc * yc + ga * ya
        merged_b = merged.astype(BF)
        h = x_ref[...] + _dot(merged_b, wo)
        r2 = lax.rsqrt(jnp.mean(h * h, axis=-1, keepdims=True) + EPS)
        n = h * r2
        fg = fg_ref[...]
        err = n * fg - t_ref[...]
        loss = 0.5 * jnp.sum(jnp.sum(err * err, axis=-1, keepdims=True) / D, axis=0, keepdims=True)
        dy = err / D
        g_fg = jnp.sum(dy * n, axis=0, keepdims=True)
        dn = dy * fg
        dh = r2 * (dn - n * jnp.mean(dn * n, axis=-1, keepdims=True))
        dh_ref[...] = dh
        dh_b = dh.astype(BF)
        dmerged = _dot_nt(dh_b, wo)
        acc[2] += _dot(merged.T.astype(BF), dh_b)
        dyc = (dmerged * gc).astype(BF)
        dya = (dmerged * ga).astype(BF)
        dgcp = dmerged * yc * gc * (1.0 - gc)
        dgap = dmerged * ya * ga * (1.0 - ga)
        dmid_ref[1] = dgcp.astype(BF)
        dmid_ref[2] = dgap.astype(BF)
        acc[0] += _dot(yc_in_b.astype(F32).T.astype(BF), dyc)
        acc[1] += _dot(ya_in_b.astype(F32).T.astype(BF), dya)
        dyc_ref[...] = _dot_nt(dyc, wc).astype(BF)
        dya_in = _dot_nt(dya, wa)
        do_ref[...] = dya_in * (z * sg)
        dmid_ref[0] = (dya_in * ov * (sg * (1.0 + z * (1.0 - sg)))).astype(BF)
        small_ref[...] += _set_rows((8, D), {
            1: jnp.sum(dgcp, axis=0, keepdims=True), 2: jnp.sum(dgap, axis=0, keepdims=True),
            3: g_fg, 7: jnp.broadcast_to(loss, (1, D))})

        @pl.when(i == nsteps - 1)
        def _():
            for p in range(N_DEV):
                for a in range(3):
                    stage[...] = acc[a, p * ROW_SHARD:(p + 1) * ROW_SHARD, :].astype(BF)
                    pltpu.sync_copy(stage, gw_ref.at[p, a])

    return pl.pallas_call(
        body, name="mid", grid=(nsteps,),
        in_specs=[tile, pl.BlockSpec((tm, D), lambda i: (i, 0)), pl.BlockSpec((tm, D), lambda i: (i, 1)),
                  pl.BlockSpec((tm, D), lambda i: (i, 2)), tile, tile, tile,
                  pl.BlockSpec((1, 2 * D), lambda i: (0, 0)), pl.BlockSpec((1, D), lambda i: (0, 0)), VMEM_SPEC],
        out_specs=(tile, pl.BlockSpec((3, tm, D), lambda i: (0, i, 0)), tile, tile,
                   ANY_SPEC, pl.BlockSpec((8, D), lambda i: (0, 0))),
        out_shape=(jax.ShapeDtypeStruct((S, D), F32), jax.ShapeDtypeStruct((3, S, D), BF),
                   jax.ShapeDtypeStruct((S, D), F32), jax.ShapeDtypeStruct((S, D), BF),
                   jax.ShapeDtypeStruct((N_DEV, 3, ROW_SHARD, D), BF), jax.ShapeDtypeStruct((8, D), F32)),
        scratch_shapes=[pltpu.VMEM((3, D, D), F32), pltpu.VMEM((ROW_SHARD, D), BF)],
        compiler_params=_params(1),
    )(yc_in, pa_mid, pa_mid, pa_mid, o, x2, target, b_merge, final_g, w3)


def _conv_bwd(dyc_in, pa, cw8):
    S = pa.shape[0]
    tm, tc = CONV_TM, CONV_TC
    nct = D // tc
    nrt = S // tm
    last_halo = S // HALO - 1

    def seg(s):
        return pl.BlockSpec((tm, tc), lambda j, i, s=s: (i, s * nct + j))

    def halo_before(s):
        return pl.BlockSpec((HALO, tc), lambda j, i, s=s: (jnp.maximum(i * (tm // HALO) - 1, 0), s * nct + j))

    def halo_after(s):
        return pl.BlockSpec((HALO, tc), lambda j, i, s=s: (jnp.minimum((i + 1) * (tm // HALO), last_halo), s * nct + j))

    def body(dy, xc, bg, cg, zc, xch, cgh, dyn, bgn, zcn, cw, dout, gcw):
        i = pl.program_id(1)

        @pl.when(i == 0)
        def _():
            gcw[...] = jnp.zeros_like(gcw)

        xcv, cgv = xc[...].astype(F32), cg[...].astype(F32)
        a = cgv * xcv
        ah = jnp.where(i > 0, cgh[...].astype(F32) * xch[...].astype(F32), 0.0)
        row = lax.broadcasted_iota(jnp.int32, (tm, tc), 0)
        a1 = jnp.where(row == 0, ah[HALO - 1:HALO, :], pltpu.roll(a, 1, 0))
        a2 = jnp.where(row == 0, ah[HALO - 2:HALO - 1, :],
                       jnp.where(row == 1, ah[HALO - 1:HALO, :], pltpu.roll(a, 2, 0)))
        w = cw[...]
        conv = w[0:1, :] * a2 + w[1:2, :] * a1 + w[2:3, :] * a
        z = zc[...].astype(F32)
        sg = _sigmoid(z)
        silu = z * sg
        bgv = bg[...].astype(F32)
        dyv = dy[...].astype(F32)
        dout[3] = (dyv * bgv * conv * (sg * (1.0 + z * (1.0 - sg)))).astype(BF)
        dout[1] = (dyv * silu * conv).astype(BF)
        dc = dyv * silu * bgv
        zn = zcn[...].astype(F32)
        dcn = dyn[...].astype(F32) * (zn * _sigmoid(zn)) * bgn[...].astype(F32)
        dcn = jnp.where(i < nrt - 1, dcn, 0.0)
        dc1 = jnp.where(row == tm - 1, dcn[0:1, :], pltpu.roll(dc, tm - 1, 0))
        dc2 = jnp.where(row == tm - 1, dcn[1:2, :],
                        jnp.where(row == tm - 2, dcn[0:1, :], pltpu.roll(dc, tm - 2, 0)))
        da = w[2:3, :] * dc + w[1:2, :] * dc1 + w[0:1, :] * dc2
        dout[2] = (da * xcv).astype(BF)
        dout[0] = (da * cgv).astype(BF)
        gcw[...] += _set_rows((8, tc), {
            4: jnp.sum(dc * a2, axis=0, keepdims=True), 5: jnp.sum(dc * a1, axis=0, keepdims=True),
            6: jnp.sum(dc * a, axis=0, keepdims=True)})

    return pl.pallas_call(
        body, name="conv_bwd", grid=(nct, nrt),
        in_specs=[pl.BlockSpec((tm, tc), lambda j, i: (i, j)), seg(0), seg(1), seg(2), seg(3),
                  halo_before(0), halo_before(2),
                  pl.BlockSpec((HALO, tc), lambda j, i: (jnp.minimum((i + 1) * (tm // HALO), last_halo), j)),
                  halo_after(1), halo_after(3), pl.BlockSpec((8, tc), lambda j, i: (0, j))],
        out_specs=(pl.BlockSpec((4, tm, tc), lambda j, i: (0, i, j)), pl.BlockSpec((8, tc), lambda j, i: (0, j))),
        out_shape=(jax.ShapeDtypeStruct((4, S, D), BF), jax.ShapeDtypeStruct((8, D), F32)),
        compiler_params=_params(2),
    )(dyc_in, pa, pa, pa, pa, pa, pa, dyc_in, pa, pa, cw8)


def _attn_bwd(q, k, v, slopes, do, o, lse, g_in, g_3):
    S = q.shape[0]
    hpr = HEAD_PAIRS
    n_blocks = S // QB

    def body(sl_ref, q_ref, k_ref, v_ref, do_ref, o_ref, lse_ref, gin_ref, g3_ref, out_ref, rin_ref, r3_ref,
             dq_s, dk_s, dv_s, do_s, dd_s, *sems):
        hp = pl.program_id(0)
        exchanges = (_GradExchange(gin_ref, rin_ref, *sems[:3], _shard_cols((0, SEG0_ATTN * D), (SEG0_MID * D, IN_COLS))),
                     _GradExchange(g3_ref, r3_ref, *sems[3:], _whole))

        @pl.when(hp == 0)
        def _():
            for ex in exchanges:
                ex.start()

        head_sum = _head_sum_matrix()
        dq_s[...] = jnp.zeros(dq_s.shape, F32)
        dk_s[...] = jnp.zeros(dk_s.shape, F32)
        dv_s[...] = jnp.zeros(dv_s.shape, F32)

        def row_dots(i, carry):
            rows = pl.ds(pl.multiple_of(i * QB, QB), QB)
            natural = _natural_rows(i, S)
            do_c = do_ref[natural, :]
            do_s[rows, :] = do_c
            dd = _dot(_hi_lo(do_c * o_ref[natural, :]), head_sum)
            dd_s[0, rows, :] = dd[:, :QB]
            dd_s[1, rows, :] = dd[:, QB:]
            return carry

        lax.fori_loop(0, n_blocks, row_dots, 0)

        for d in DILATIONS:
            tri_le, dist, low = _fold_masks(d)
            low_b = low.astype(F32).astype(BF)
            high_b = 1.0 - low_b
            slope = [sl_ref[2 * hp + a] * float(d) for a in range(2)]
            bias = [slope[a] * dist for a in range(2)]

            def block(b, d=d, slope=slope, bias=bias, tri_le=tri_le, low=low, low_b=low_b, high_b=high_b):
                n, cur, prev = _block_rows(b, d, S)
                has_prev = n > 0
                valid = jnp.logical_or(tri_le, has_prev)
                q2f = cur.get(q_ref) * 0.125
                q2 = q2f.astype(BF)
                qs = jnp.concatenate([q2 * low_b, q2 * high_b], axis=0)
                kp, vp = prev.get(k_ref), prev.get(v_ref)
                kp_b, vp_b = kp.astype(BF), vp.astype(BF)
                kcat = jnp.concatenate([kp_b, cur.get(k_ref).astype(BF)], axis=0)
                vcat = jnp.concatenate([vp_b, cur.get(v_ref).astype(BF)], axis=0)
                do2f = cur.get(do_s)
                do2 = do2f.astype(BF)
                dos = jnp.concatenate([do2 * low_b, do2 * high_b], axis=0)
                s2 = _dot_nt(qs, kcat)
                dp2 = _dot_nt(dos, vcat)
                diag2 = _dot(jnp.concatenate([_hi_lo(q2.astype(F32) * kp_b.astype(F32)),
                                              _hi_lo(do2.astype(F32) * vp_b.astype(F32))], axis=0), head_sum)
                p_rows, ds_rows, pe_h, dse_h = [], [], [], []
                for a in range(2):
                    hs = slice(a * QB, (a + 1) * QB)
                    sp, sc = s2[hs, :QB], s2[hs, QB:]
                    dpp, dpc = dp2[hs, :QB], dp2[hs, QB:]
                    lse_a, dd_a = cur.get(lse_ref.at[a]), cur.get(dd_s.at[a])
                    comb = jnp.where(tri_le, sc, sp) - bias[a]
                    e = diag2[:QB, hs] - slope[a] * float(QB)
                    p = jnp.where(valid, jnp.exp(comb - lse_a), 0.0)
                    pe = jnp.where(has_prev, jnp.exp(e - lse_a), 0.0)
                    ds = p * (jnp.where(tri_le, dpc, dpp) - dd_a)
                    dse_h.append(pe * (diag2[QB:, hs] - dd_a))
                    pe_h.append(pe)
                    p_rows.append(jnp.concatenate([jnp.where(tri_le, 0.0, p).astype(BF),
                                                   jnp.where(tri_le, p, 0.0).astype(BF)], axis=1))
                    ds_rows.append(jnp.concatenate([jnp.where(tri_le, 0.0, ds).astype(BF),
                                                    jnp.where(tri_le, ds, 0.0).astype(BF)], axis=1))
                pst = jnp.concatenate(p_rows, axis=0)
                dst = jnp.concatenate(ds_rows, axis=0)
                pe2 = jnp.where(low, pe_h[0], pe_h[1])
                dse2 = jnp.where(low, dse_h[0], dse_h[1])
                dq = _dot(dst, kcat)
                cur.add(dq_s, (jnp.where(low, dq[:QB], dq[QB:]) + dse2 * kp) * 0.125)
                dk = _dot_tn(dst, qs)
                dv = _dot_tn(pst, dos)
                prev.add(dk_s, dk[:QB] + dse2 * q2f)
                cur.add(dk_s, dk[QB:])
                prev.add(dv_s, dv[:QB] + pe2 * do2f)
                cur.add(dv_s, dv[QB:])

            def several(it, carry, block=block):
                for u in range(ATT_UNROLL):
                    block(it * ATT_UNROLL + u)
                return carry

            lax.fori_loop(0, n_blocks // ATT_UNROLL, several, 0)

        def finish(i, carry):
            rows = pl.ds(pl.multiple_of(i * QB, QB), QB)
            natural = _natural_rows(i, S)
            for t, ref in enumerate((dq_s, dk_s, dv_s)):
                out_ref.at[t][natural, :] = ref[rows, :]
            return carry

        lax.fori_loop(0, n_blocks, finish, 0)

        @pl.when(hp == hpr - 1)
        def _():
            for ex in exchanges:
                ex.finish()

    col = pl.BlockSpec((S, 128), lambda h: (0, h))
    return pl.pallas_call(
        body, name="attn_bwd", grid=(hpr,),
        in_specs=[SMEM_SPEC, col, col, col, col, col, pl.BlockSpec((2, S, 128), lambda h: (0, 0, h)),
                  ANY_SPEC, ANY_SPEC],
        out_specs=(pl.BlockSpec((3, S, 128), lambda h: (0, 0, h)), ANY_SPEC, ANY_SPEC),
        out_shape=(jax.ShapeDtypeStruct((3, S, D), F32), jax.ShapeDtypeStruct(g_in.shape, BF),
                   jax.ShapeDtypeStruct(g_3.shape, BF)),
        scratch_shapes=([pltpu.VMEM((S, 128), F32)] * 4 + [pltpu.VMEM((2, S, 128), F32)]
                        + GRAD_EXCHANGE_SEMS + GRAD_EXCHANGE_SEMS),
        compiler_params=_params(1),
    )(slopes, q, k, v, do, o, lse, g_in, g_3)


WG_TN = 256
SEG0_CONV, SEG0_ATTN, SEG0_MID = 0, 4, 7


def _wgrad_in(ut, d_group, seg0, g_in, name):
    S = ut.shape[1]
    tn = WG_TN
    per_seg = D // tn
    per_shard = W_IN_SHARD // tn
    n_tiles = d_group.shape[0] * per_seg
    tile0 = seg0 * per_seg

    def body(ut_ref, d_ref, *rest):
        rest[-1][0] = _dot(ut_ref[...], d_ref[0].astype(BF)).astype(BF)

    operands, in_specs, aliases = [ut, d_group], [VMEM_SPEC, pl.BlockSpec((1, S, tn), lambda t: (t // per_seg, 0, t % per_seg))], {}
    if g_in is not None:
        operands.append(g_in)
        in_specs.append(ANY_SPEC)
        aliases = {2: 0}
    return pl.pallas_call(
        body, name=name, grid=(n_tiles,), in_specs=in_specs,
        out_specs=pl.BlockSpec((1, D, tn), lambda t: ((tile0 + t) // per_shard, 0, (tile0 + t) % per_shard)),
        out_shape=jax.ShapeDtypeStruct((N_DEV, D, W_IN_SHARD), BF),
        input_output_aliases=aliases,
        compiler_params=_params(1),
    )(*operands)


def _dgrad_norm_bwd(d_conv, d_attn, d_mid, w_all, x2, dh, norm_g):
    S = x2.shape[0]
    tm = ROW_TILE
    nsteps = S // tm
    tile = pl.BlockSpec((tm, D), lambda i: (i, 0))
    pieces = _proj_pieces()

    def body(a_ref, b_ref, c_ref, w_ref, x_ref, dh_ref, g_ref, gx_ref, small_ref):
        i = pl.program_id(0)

        @pl.when(i == 0)
        def _():
            small_ref[...] = jnp.zeros_like(small_ref)

        groups = (a_ref, b_ref, c_ref)
        du = jnp.zeros((tm, D), F32)
        for s, sc, p, pc, width in pieces:
            g = 0 if s < 4 else (1 if s < 7 else 2)
            local = s - (0, 4, 7)[g]
            du = du + _dot_nt(groups[g][local, :, sc:sc + width].astype(BF), w_ref[p, :, pc:pc + width])
        xv = x_ref[...]
        r = lax.rsqrt(jnp.mean(xv * xv, axis=-1, keepdims=True) + EPS)
        n = xv * r
        dn = du * g_ref[...]
        gx_ref[...] = dh_ref[...] + r * (dn - n * jnp.mean(dn * n, axis=-1, keepdims=True))
        small_ref[...] += _set_rows((8, D), {0: jnp.sum(du * n, axis=0, keepdims=True)})

    return pl.pallas_call(
        body, name="dgrad_norm_bwd", grid=(nsteps,),
        in_specs=[pl.BlockSpec((4, tm, D), lambda i: (0, i, 0)), pl.BlockSpec((3, tm, D), lambda i: (0, i, 0)),
                  pl.BlockSpec((3, tm, D), lambda i: (0, i, 0)), VMEM_SPEC, tile, tile,
                  pl.BlockSpec((1, D), lambda i: (0, 0))],
        out_specs=(tile, pl.BlockSpec((8, D), lambda i: (0, 0))),
        out_shape=(jax.ShapeDtypeStruct((S, D), F32), jax.ShapeDtypeStruct((8, D), F32)),
        compiler_params=_params(1),
    )(d_conv, d_attn, d_mid, w_all, x2, dh, norm_g)


HBM_SPEC = pl.BlockSpec(memory_space=pltpu.HBM)
SEM_SPEC = pl.BlockSpec(memory_space=pltpu.SEMAPHORE)
ATTN_COLS = _shard_cols((SEG0_ATTN * D, SEG0_MID * D))


def _attn_cols_exchange_start(g_in, r_in):
    def body(g_ref, r_ref, send_sems, recv_sems, g_thru, r_thru, token):
        _GradExchange(g_ref, r_ref, send_sems, recv_sems, None, ATTN_COLS).start()
        token[...] = jnp.zeros_like(token)

    hbm = pltpu.with_memory_space_constraint
    return pl.pallas_call(
        body, name="attn_cols_exchange_start",
        out_shape=(pltpu.SemaphoreType.DMA((N_DEV,)), pltpu.SemaphoreType.DMA((N_DEV,)),
                   pltpu.HBM(g_in.shape, g_in.dtype), pltpu.HBM(r_in.shape, r_in.dtype),
                   jax.ShapeDtypeStruct((8, 128), F32)),
        in_specs=(HBM_SPEC, HBM_SPEC), out_specs=(SEM_SPEC, SEM_SPEC, HBM_SPEC, HBM_SPEC, VMEM_SPEC),
        input_output_aliases={0: 2, 1: 3},
        compiler_params=pltpu.CompilerParams(has_side_effects=pltpu.SideEffectType.DATAFLOW_SIDE_EFFECTING),
    )(hbm(g_in, pltpu.HBM), hbm(r_in, pltpu.HBM))


def _attn_cols_exchange_wait(send_sems, recv_sems, g_thru, r_thru, after):
    def body(g_ref, r_ref, send_sems, recv_sems, after_ref, g_dead, r_out):
        _GradExchange(g_ref, r_ref, send_sems, recv_sems, None, ATTN_COLS).finish()

    return pl.pallas_call(
        body, name="attn_cols_exchange_wait",
        out_shape=(pltpu.HBM(g_thru.shape, g_thru.dtype), pltpu.HBM(r_thru.shape, r_thru.dtype)),
        in_specs=(HBM_SPEC, HBM_SPEC, SEM_SPEC, SEM_SPEC, ANY_SPEC), out_specs=(HBM_SPEC, HBM_SPEC),
        input_output_aliases={0: 0, 1: 1},
        compiler_params=pltpu.CompilerParams(has_side_effects=pltpu.SideEffectType.DATAFLOW_SIDE_EFFECTING),
    )(g_thru, r_thru, send_sems, recv_sems, after)


def _adamw_math(w, g, m, v):
    m = ADAM_B1 * m + (1.0 - ADAM_B1) * g
    v = ADAM_B2 * v + (1.0 - ADAM_B2) * (g * g)
    m_hat = m / (1.0 - ADAM_B1 ** ADAM_STEP)
    v_hat = v / (1.0 - ADAM_B2 ** ADAM_STEP)
    delta = -ADAM_LR * (m_hat / (jnp.sqrt(v_hat) + ADAM_EPS) + ADAM_WD * w)
    return delta, m, v


def _sum_adamw(parts, w, m, v, tm, name):
    R, C = w.shape
    tile = pl.BlockSpec((tm, C), lambda i: (i, 0))

    def body(p_ref, w_ref, m_ref, v_ref, g_out, d_out, m_out, v_out):
        g = p_ref[0].astype(F32)
        for s in range(1, N_DEV):
            g = g + p_ref[s].astype(F32)
        g_out[...] = g
        d_out[...], m_out[...], v_out[...] = _adamw_math(w_ref[...], g, m_ref[...], v_ref[...])

    shape = jax.ShapeDtypeStruct((R, C), F32)
    return pl.pallas_call(
        body, name=name, grid=(R // tm,),
        in_specs=[pl.BlockSpec((N_DEV, tm, C), lambda i: (0, i, 0)), tile, tile, tile],
        out_specs=(tile, tile, tile, tile), out_shape=(shape, shape, shape, shape),
        compiler_params=_params(1),
    )(parts, w, m, v)


def _adamw(g, w, m, v, name):
    def body(g_ref, w_ref, m_ref, v_ref, d_out, m_out, v_out):
        d_out[...], m_out[...], v_out[...] = _adamw_math(w_ref[...], g_ref[...], m_ref[...], v_ref[...])

    shape = jax.ShapeDtypeStruct(w.shape, F32)
    return pl.pallas_call(
        body, name=name, in_specs=[VMEM_SPEC] * 4, out_specs=(VMEM_SPEC,) * 3, out_shape=(shape, shape, shape),
    )(g, w, m, v)


def _alibi_slopes():
    return jnp.exp2(-8.0 * jnp.arange(1, N_HEADS + 1, dtype=F32) / N_HEADS)


def _local_step(x2, target, norm_g, b_merge, final_g, w_all, w3_all, cw_all):
    slopes = _alibi_slopes()
    u, ut = _norm(x2, norm_g)
    o, lse, q, k, v, w_all, w3_all, cw_all = _attn_fwd(u, slopes, w_all, w3_all, cw_all)
    w3 = jnp.transpose(w3_all, (1, 0, 2, 3)).reshape(3, D, D)
    cw8 = jnp.transpose(cw_all, (1, 0, 2)).reshape(8, D)
    pa = _proj_cols(u, w_all, SEG0_CONV, 4, BF, "proj_conv")
    yc_in = _conv_fwd(pa, cw8)
    pa_mid = _proj_cols(u, w_all, SEG0_MID, 3, BF, "proj_mid")
    dh, d_mid, do, dyc_in, g_3, small_mid = _mid(yc_in, pa_mid, o, x2, target, b_merge, final_g, w3)
    g_in = _wgrad_in(ut, d_mid, SEG0_MID, None, "wgrad_in_mid")
    d_conv, small_conv = _conv_bwd(dyc_in, pa, cw8)
    g_in = _wgrad_in(ut, d_conv, SEG0_CONV, g_in, "wgrad_in_conv")
    d_attn, r_in, r_3 = _attn_bwd(q, k, v, slopes, do, o, lse, g_in, g_3)
    g_in = _wgrad_in(ut, d_attn, SEG0_ATTN, g_in, "wgrad_in_attn")
    *in_flight, token = _attn_cols_exchange_start(g_in, r_in)
    grad_x, small_norm = _dgrad_norm_bwd(d_conv, d_attn, d_mid, w_all, x2, dh, norm_g + token[0:1, 0:1])
    return grad_x, in_flight, r_3, small_mid, small_conv, small_norm


def kernel(x, norm_g, w_in, b_merge, conv_w, w_out_conv, w_out_attn, w_o, final_g, loss_target, m_norm_g, m_w_in, m_b_merge, m_conv_w, m_w_out_conv, m_w_out_attn, m_w_o, m_final_g, v_norm_g, v_w_in, v_b_merge, v_conv_w, v_w_out_conv, v_w_out_attn, v_w_o, v_final_g):
    me = 4 * lax.axis_index("x") + 2 * lax.axis_index("y") + lax.axis_index("c")
    stack3 = lambda a, b, c: jnp.concatenate([a, b, c], axis=0)
    pad8 = lambda a: jnp.pad(a, ((0, 8 - a.shape[0]), (0, 0)))

    w3_shard = stack3(w_out_conv, w_out_attn, w_o)
    w_all, w3_all, cw_all = _gather_first_weights(w_in[0], w3_shard, pad8(conv_w[0]))

    final_g2 = final_g.reshape(1, D)
    grad_x, in_flight, r_3, small_mid, small_conv, small_norm = _local_step(
        x[0], loss_target[0], norm_g, b_merge, final_g2, w_all, w3_all, cw_all)

    small = _allreduce_small(small_mid, small_conv, small_norm)
    g_in, r_in = _attn_cols_exchange_wait(*in_flight, small)
    own = lax.dynamic_index_in_dim(g_in, me, 0, keepdims=True)
    r_in = lax.dynamic_update_slice(r_in, own, (me, 0, 0))

    g_w_in, d_w_in, nm_w_in, nv_w_in = _sum_adamw(r_in, w_in[0], m_w_in[0], v_w_in[0], 128, "adamw_w_in")
    g_w3, d_w3, nm_w3, nv_w3 = _sum_adamw(
        r_3.reshape(N_DEV, 3 * ROW_SHARD, D), w3_shard.reshape(3 * ROW_SHARD, D),
        stack3(m_w_out_conv, m_w_out_attn, m_w_o).reshape(3 * ROW_SHARD, D),
        stack3(v_w_out_conv, v_w_out_attn, v_w_o).reshape(3 * ROW_SHARD, D), ROW_SHARD, "adamw_w3")

    def pack(ng, bm, fg):
        return pad8(jnp.concatenate([ng, bm.reshape(2, D), fg.reshape(1, D)], axis=0))

    d_s, nm_s, nv_s = _adamw(small, pack(norm_g, b_merge, final_g), pack(m_norm_g, m_b_merge, m_final_g),
                             pack(v_norm_g, v_b_merge, v_final_g), "adamw_small")
    g_cw = lax.dynamic_slice(small, (4, me * ROW_SHARD), (3, ROW_SHARD))
    d_cw, nm_cw, nv_cw = _adamw(g_cw, conv_w[0], m_conv_w[0], v_conv_w[0], "adamw_conv_w")

    loss = small[7, 0]
    split3 = lambda t: tuple(t[a * ROW_SHARD:(a + 1) * ROW_SHARD][None] for a in range(3))
    unpack = lambda t: (t[0:1], t[1:3].reshape(1, 2 * D), t[3])

    def leaves(in_, small_, cw_, w3_):
        ng, bm, fg = unpack(small_)
        wc, wa, wo = split3(w3_)
        return (ng, in_[None], bm, cw_[None], wc, wa, wo, fg)

    return (loss, grad_x[None],
            *leaves(g_w_in, small, g_cw, g_w3),
            *leaves(d_w_in, d_s, d_cw, d_w3),
            *leaves(nm_w_in, nm_s, nm_cw, nm_w3),
            *leaves(nv_w_in, nv_s, nv_cw, nv_w3))
```

```python
import functools

import jax
import jax.numpy as jnp
from jax import lax
from jax.experimental import pallas as pl
from jax.experimental.pallas import tpu as pltpu

D = 1024
N_HEADS = 16
HEAD_DIM = 64
N_SEG = 10
IN_COLS = N_SEG * D
N_DEV = 8
W_IN_SHARD = IN_COLS // N_DEV
ROW_SHARD = D // N_DEV
QB = 128
DILATIONS = (1, 4, 16)
EPS = 1e-6
NEG = -1e30
BF = jnp.bfloat16
F32 = jnp.float32
MESH = pl.DeviceIdType.MESH

ADAM_LR = 0.001
ADAM_B1 = 0.9
ADAM_B2 = 0.999
ADAM_EPS = 1e-08
ADAM_WD = 0.01
ADAM_STEP = 10

V7X_VMEM_BYTES = 64 * 1024 * 1024
VMEM_LIMIT = V7X_VMEM_BYTES - 8 * 1024 * 1024
ROW_TILE = 256

VMEM_SPEC = pl.BlockSpec(memory_space=pltpu.VMEM)
ANY_SPEC = pl.BlockSpec(memory_space=pl.ANY)
SMEM_SPEC = pl.BlockSpec(memory_space=pltpu.SMEM)


def _params(n_grid_axes, vmem=VMEM_LIMIT):
    return pltpu.CompilerParams(dimension_semantics=("arbitrary",) * n_grid_axes, vmem_limit_bytes=vmem)


def _dot(a, b):
    return jnp.dot(a, b, preferred_element_type=F32)


def _dot_nt(a, b):
    return lax.dot_general(a, b, (((1,), (1,)), ((), ())), preferred_element_type=F32)


def _dot_tn(a, b):
    return lax.dot_general(a, b, (((0,), (0,)), ((), ())), preferred_element_type=F32)


def _sigmoid(z):
    return 1.0 / (1.0 + jnp.exp(-z))


def _my_place():
    x, y, c = lax.axis_index("x"), lax.axis_index("y"), lax.axis_index("c")
    return x, y, c, 4 * x + 2 * y + c


def _peers(x, y, c):
    out = []
    for k in range(1, N_DEV):
        px = 1 - x if k & 4 else x
        py = 1 - y if k & 2 else y
        pc = 1 - c if k & 1 else c
        out.append(((px, py, pc), 4 * px + 2 * py + pc))
    return out


def _device(p):
    return (p >> 2, (p >> 1) & 1, p & 1)


def _shard_cols(*ranges):
    def cols(p):
        found = None
        for lo, hi in ranges:
            a, b = max(lo, p * W_IN_SHARD), min(hi, (p + 1) * W_IN_SHARD)
            if a < b:
                assert found is None
                found = (a - p * W_IN_SHARD, b - p * W_IN_SHARD)
        return found

    return cols


def _whole(p):
    return ()


def _block(ref, idx, cols):
    return ref.at[idx] if cols == () else ref.at[idx, :, cols[0]:cols[1]]


class _WeightGather:
    def __init__(self, src, dst, send_sems, forward_sems, recv_sems, cols):
        self.src, self.dst, self.cols = src, dst, cols
        self.send_sems, self.forward_sems, self.recv_sems = send_sems, forward_sems, recv_sems
        self.me = _my_place()[3]

    def _copy(self, p, target, passing_on=False):
        cols = self.cols(p)
        return pltpu.make_async_remote_copy(
            src_ref=_block(self.dst, p, cols) if passing_on else self.src(p, cols), dst_ref=_block(self.dst, p, cols),
            send_sem=self.forward_sems.at[p] if passing_on else self.send_sems.at[target],
            recv_sem=self.recv_sems.at[p], device_id=_device(target), device_id_type=MESH)

    def _as_each_device(self, own, relayed, other):
        for m in range(N_DEV):
            def branch(m=m):
                for p in range(N_DEV):
                    if self.cols(p) is None:
                        continue
                    if p == m:
                        for t in [m ^ 1] + [q for q in range(N_DEV) if q >> 1 != m >> 1 and q & 1 == m & 1]:
                            own(self._copy(m, t))
                    elif p >> 1 != m >> 1 and p & 1 == m & 1:
                        relayed(p, m ^ 1)
                    else:
                        other(p)

            pl.when(self.me == m)(branch)

    def start(self):
        self._as_each_device(lambda cp: cp.start(), lambda p, t: None, lambda p: None)

    def forward(self):
        def pass_on(p, t):
            self._copy(p, p).wait_recv()
            self._copy(p, t, passing_on=True).start()

        self._as_each_device(lambda cp: None, pass_on, lambda p: None)

    def finish(self):
        self._as_each_device(lambda cp: cp.wait_send(), lambda p, t: self._copy(p, t, passing_on=True).wait_send(),
                             lambda p: self._copy(p, p).wait_recv())


WEIGHT_GATHER_SEMS = [pltpu.SemaphoreType.DMA((N_DEV,))] * 3
FORWARD_STEP = 6
REST_COLS = _shard_cols((0, 4 * D), (7 * D, IN_COLS))
HEAD_PAIRS = D // 128


def _qkv_piece(h, seg):
    col = (4 + seg) * D + 128 * h
    return col // W_IN_SHARD, col % W_IN_SHARD


class _PieceGather:
    def __init__(self, src, dst, send_sems, recv_sems):
        self.src, self.dst, self.send_sems, self.recv_sems = src, dst, send_sems, recv_sems
        self.me = _my_place()[3]

    def _copy(self, i, target):
        p, lo = _qkv_piece(i // 3, i % 3)
        return pltpu.make_async_remote_copy(
            src_ref=self.src(p, lo, lo + 128), dst_ref=self.dst.at[p, :, lo:lo + 128], send_sem=self.send_sems.at[i, target],
            recv_sem=self.recv_sems.at[i], device_id=_device(target), device_id_type=MESH)

    def _owner(self, i, act):
        p = _qkv_piece(i // 3, i % 3)[0]

        def sender():
            for k in range(N_DEV - 1):
                act(self._copy(i, (p + 1 + (k + i) % (N_DEV - 1)) % N_DEV))

        pl.when(self.me == p)(sender)

    def start(self, pieces):
        for i in pieces:
            self._owner(i, lambda cp: cp.start())

    def wait_send(self, pieces):
        for i in pieces:
            self._owner(i, lambda cp: cp.wait_send())

    def wait_recv(self, pieces):
        for i in pieces:
            p = _qkv_piece(i // 3, i % 3)[0]
            pl.when(self.me != p)(lambda i=i, p=p: self._copy(i, p).wait_recv())


def _piece_sems(n):
    return [pltpu.SemaphoreType.DMA((n, N_DEV)), pltpu.SemaphoreType.DMA((n,))]


def _gather_first_weights(w_in, w3, cw):
    def body(w_in_ref, w3_ref, cw_ref, o_in, o_3, o_cw, in_bf, w3_bf, local_sems, *sems):
        me = _my_place()[3]

        def cast_rows(i, carry):
            r = pl.multiple_of(i * 128, 128)
            in_bf[pl.ds(r, 128), :] = w_in_ref[pl.ds(r, 128), :].astype(BF)
            return carry

        lax.fori_loop(0, D // 128, cast_rows, 0)
        for a in range(3):
            w3_bf[a] = w3_ref[a].astype(BF)
        gather = _PieceGather(lambda p, lo, hi: in_bf.at[:, lo:hi], o_in, *sems)
        gather.start(range(3))
        local = [pltpu.make_async_copy(src, dst.at[me], local_sems.at[a])
                 for a, (src, dst) in enumerate(((in_bf, o_in), (w3_bf, o_3), (cw_ref, o_cw)))]
        for cp in local:
            cp.start()
        gather.wait_recv(range(3))
        gather.wait_send(range(3))
        for cp in local:
            cp.wait()

    return pl.pallas_call(
        body, name="gather_first_weights",
        out_shape=(jax.ShapeDtypeStruct((N_DEV, D, W_IN_SHARD), BF),
                   jax.ShapeDtypeStruct((N_DEV, 3, ROW_SHARD, D), BF),
                   jax.ShapeDtypeStruct((N_DEV, 8, 128), F32)),
        in_specs=[VMEM_SPEC, VMEM_SPEC, VMEM_SPEC],
        out_specs=(ANY_SPEC, ANY_SPEC, ANY_SPEC),
        scratch_shapes=[pltpu.VMEM((D, W_IN_SHARD), BF), pltpu.VMEM((3, ROW_SHARD, D), BF),
                        pltpu.SemaphoreType.DMA((3,))] + _piece_sems(3),
        compiler_params=pltpu.CompilerParams(vmem_limit_bytes=VMEM_LIMIT),
    )(w_in, w3, cw)


class _GradExchange:
    def __init__(self, src, dst, send_sems, recv_sems, local_sem, cols):
        self.src, self.dst, self.cols = src, dst, cols
        self.send_sems, self.recv_sems, self.local_sem = send_sems, recv_sems, local_sem
        self.me = _my_place()[3]

    def _remote(self, p, source):
        return pltpu.make_async_remote_copy(
            src_ref=_block(self.src, p, self.cols(p)), dst_ref=_block(self.dst, source, self.cols(p)),
            send_sem=self.send_sems.at[p], recv_sem=self.recv_sems.at[source],
            device_id=_device(p), device_id_type=MESH)

    def _local(self, p):
        return pltpu.make_async_copy(_block(self.src, p, self.cols(p)), _block(self.dst, p, self.cols(p)),
                                     self.local_sem)

    def _as_each_device(self, send, local, receive):
        for m in range(N_DEV):
            def branch(m=m):
                for k in range(1, N_DEV):
                    p = (m + k) % N_DEV
                    if self.cols(p) is not None:
                        send(self._remote(p, m))
                if self.cols(m) is not None:
                    if self.local_sem is not None:
                        local(self._local(m))
                    for k in range(1, N_DEV):
                        receive(self._remote(m, (m + k) % N_DEV))

            pl.when(self.me == m)(branch)

    def start(self):
        self._as_each_device(lambda cp: cp.start(), lambda cp: cp.start(), lambda cp: None)

    def finish(self):
        self._as_each_device(lambda cp: cp.wait_send(), lambda cp: cp.wait(), lambda cp: cp.wait_recv())


GRAD_EXCHANGE_SEMS = [pltpu.SemaphoreType.DMA((N_DEV,)), pltpu.SemaphoreType.DMA((N_DEV,)), pltpu.SemaphoreType.DMA]


def _allreduce_small(p_mid, p_conv, p_norm):
    def body(a_ref, b_ref, c_ref, out_ref, mine, gathered, send_sems, recv_sems):
        x, y, c, me = _my_place()
        mine[...] = a_ref[...] + b_ref[...] + c_ref[...]
        gathered[me] = mine[...]
        remote = []
        for k, (peer, _) in enumerate(_peers(x, y, c)):
            cp = pltpu.make_async_remote_copy(
                src_ref=mine, dst_ref=gathered.at[me], send_sem=send_sems.at[k], recv_sem=recv_sems.at[k],
                device_id=peer, device_id_type=MESH)
            cp.start()
            remote.append(cp)
        for cp in remote:
            cp.wait()
        total = gathered[0]
        for s in range(1, N_DEV):
            total = total + gathered[s]
        out_ref[...] = total

    return pl.pallas_call(
        body, name="allreduce_small",
        out_shape=jax.ShapeDtypeStruct((8, D), F32),
        in_specs=[VMEM_SPEC, VMEM_SPEC, VMEM_SPEC], out_specs=VMEM_SPEC,
        scratch_shapes=[pltpu.VMEM((8, D), F32), pltpu.VMEM((N_DEV, 8, D), F32),
                        pltpu.SemaphoreType.DMA((N_DEV - 1,)), pltpu.SemaphoreType.DMA((N_DEV - 1,))],
    )(p_mid, p_conv, p_norm)


def _proj_pieces():
    cuts = sorted(set(range(0, IN_COLS + 1, D)) | set(range(0, IN_COLS + 1, W_IN_SHARD)))
    return [(lo // D, lo % D, lo // W_IN_SHARD, lo % W_IN_SHARD, hi - lo) for lo, hi in zip(cuts[:-1], cuts[1:])]


def _norm(x2, norm_g):
    S = x2.shape[0]
    tm = ROW_TILE

    def body(x_ref, g_ref, u_ref, ut_ref):
        xv = x_ref[...]
        r = lax.rsqrt(jnp.mean(xv * xv, axis=-1, keepdims=True) + EPS)
        u = xv * r * g_ref[...]
        u_ref[...] = u.astype(BF)
        ut_ref[...] = u.T.astype(BF)

    return pl.pallas_call(
        body, name="norm", grid=(S // tm,),
        in_specs=[pl.BlockSpec((tm, D), lambda i: (i, 0)), pl.BlockSpec((1, D), lambda i: (0, 0))],
        out_specs=(pl.BlockSpec((tm, D), lambda i: (i, 0)), pl.BlockSpec((D, tm), lambda i: (0, i))),
        out_shape=(jax.ShapeDtypeStruct((S, D), BF), jax.ShapeDtypeStruct((D, S), BF)),
        compiler_params=_params(1),
    )(x2, norm_g)


PROJ_TN = 256


def _proj_cols(u, w_all, seg0, n_seg, dtype, name):
    S = u.shape[0]
    tn = PROJ_TN
    per_shard = W_IN_SHARD // tn
    tile0 = seg0 * D // tn

    def body(u_ref, w_ref, out_ref):
        out_ref[...] = _dot(u_ref[...], w_ref[0]).astype(dtype)

    return pl.pallas_call(
        body, name=name, grid=(n_seg * D // tn,),
        in_specs=[VMEM_SPEC, pl.BlockSpec((1, D, tn), lambda t: ((tile0 + t) // per_shard, 0, (tile0 + t) % per_shard))],
        out_specs=pl.BlockSpec((S, tn), lambda t: (0, t)),
        out_shape=jax.ShapeDtypeStruct((S, n_seg * D), dtype),
        compiler_params=_params(1),
    )(u, w_all)


CONV_TM, CONV_TC = 256, 512
HALO = 16


def _conv_fwd(pa, cw8):
    S = pa.shape[0]
    tm, tc = CONV_TM, CONV_TC
    nct = D // tc

    def seg(s):
        return pl.BlockSpec((tm, tc), lambda i, j, s=s: (i, s * nct + j))

    def halo_before(s):
        return pl.BlockSpec((HALO, tc), lambda i, j, s=s: (jnp.maximum(i * (tm // HALO) - 1, 0), s * nct + j))

    def body(xc, bg, cg, zc, xch, cgh, cw, out):
        i = pl.program_id(0)
        a = cg[...].astype(F32) * xc[...].astype(F32)
        ah = cgh[...].astype(F32) * xch[...].astype(F32)
        ah = jnp.where(i > 0, ah, 0.0)
        row = lax.broadcasted_iota(jnp.int32, (tm, tc), 0)
        a1 = jnp.where(row == 0, ah[HALO - 1:HALO, :], pltpu.roll(a, 1, 0))
        a2 = jnp.where(row == 0, ah[HALO - 2:HALO - 1, :],
                       jnp.where(row == 1, ah[HALO - 1:HALO, :], pltpu.roll(a, 2, 0)))
        w = cw[...]
        conv = w[0:1, :] * a2 + w[1:2, :] * a1 + w[2:3, :] * a
        z = zc[...].astype(F32)
        out[...] = (z * _sigmoid(z) * bg[...].astype(F32) * conv).astype(BF)

    return pl.pallas_call(
        body, name="conv_fwd", grid=(S // tm, nct),
        in_specs=[seg(0), seg(1), seg(2), seg(3), halo_before(0), halo_before(2),
                  pl.BlockSpec((8, tc), lambda i, j: (0, j))],
        out_specs=pl.BlockSpec((tm, tc), lambda i, j: (i, j)),
        out_shape=jax.ShapeDtypeStruct((S, D), BF),
        compiler_params=_params(2),
    )(pa, pa, pa, pa, pa, pa, cw8)


ATT_UNROLL = 16


LAYOUT_MOD = 4
RUN = QB // LAYOUT_MOD


def _fold_masks(d):
    row = lax.broadcasted_iota(jnp.int32, (QB, QB), 0)
    lane = lax.broadcasted_iota(jnp.int32, (QB, QB), 1)
    if d == 1:
        qpos, kpos = LAYOUT_MOD * (row % RUN) + row // RUN, LAYOUT_MOD * (lane % RUN) + lane // RUN
    else:
        qpos, kpos = row, lane
    tri_le = kpos <= qpos
    dist = jnp.where(tri_le, qpos - kpos, qpos - kpos + QB).astype(F32)
    return tri_le, dist, lane < HEAD_DIM


class _Rows:
    def __init__(self, slices):
        self.slices = slices

    def get(self, ref):
        parts = [ref[sl, :] for sl in self.slices]
        return parts[0] if len(parts) == 1 else jnp.concatenate(parts, axis=0)

    def put(self, ref, val):
        size = QB // len(self.slices)
        for g, sl in enumerate(self.slices):
            ref[sl, :] = val if len(self.slices) == 1 else val[g * size:(g + 1) * size]

    def add(self, ref, val):
        self.put(ref, self.get(ref) + val)


def _block_rows(b, d, S):
    quarter = S // LAYOUT_MOD
    nb = S // (QB * d)
    r, n = b // nb, b % nb
    n_prev = jnp.maximum(n - 1, 0)
    if d == 1:
        runs = lambda m: _Rows([pl.ds(pl.multiple_of(g * quarter + RUN * m, RUN), RUN) for g in range(LAYOUT_MOD)])
        return n, runs(n), runs(n_prev)
    if d == LAYOUT_MOD:
        block = lambda m: _Rows([pl.ds(pl.multiple_of(r * quarter + QB * m, QB), QB)])
        return n, block(n), block(n_prev)
    step = d // LAYOUT_MOD
    first = (r % LAYOUT_MOD) * quarter + r // LAYOUT_MOD
    strided = lambda m: _Rows([pl.ds(first + QB * step * m, QB, stride=step)])
    return n, strided(n), strided(n_prev)


def _natural_rows(i, S):
    per = S // LAYOUT_MOD // QB
    return pl.ds(i // per + LAYOUT_MOD * QB * (i % per), QB, stride=LAYOUT_MOD)


def _head_sum_matrix():
    r = lax.broadcasted_iota(jnp.int32, (2 * QB, 2 * QB), 0)
    c = lax.broadcasted_iota(jnp.int32, (2 * QB, 2 * QB), 1)
    return (((r % QB) // HEAD_DIM) == (c // QB)).astype(F32).astype(BF)


def _hi_lo(t):
    hi = t.astype(BF)
    return jnp.concatenate([hi, (t - hi.astype(F32)).astype(BF)], axis=1)


PROJ_ROWS = 512


def _attn_fwd(u, slopes, w_all, w3_all, cw_all):
    S = u.shape[0]
    hpr = HEAD_PAIRS
    n_blocks = S // QB
    later = range(3, 3 * hpr)

    def body(sl_ref, u_ref, w_in_ref, w3_in_ref, cw_in_ref, o_ref, lse_ref, q_ref, k_ref, v_ref, w_ref, w3_ref,
             cw_ref, acc, m_s, l_s, w_tile, staged, tile_sems, *sems):
        hp = pl.program_id(0)
        me = _my_place()[3]
        pieces = _PieceGather(lambda p, lo, hi: w_ref.at[p, :, lo:hi], w_ref, *sems[0:2])
        gathers = (_WeightGather(lambda p, cols: _block(w_ref, me, cols), w_ref, *sems[2:5], REST_COLS),
                   _WeightGather(lambda p, cols: w3_ref.at[me], w3_ref, *sems[5:8], _whole),
                   _WeightGather(lambda p, cols: cw_ref.at[me], cw_ref, *sems[8:11], _whole))

        @pl.when(hp == 0)
        def _():
            pieces.start(later)
            for g in gathers:
                g.start()

        @pl.when(hp == FORWARD_STEP)
        def _():
            for g in gathers:
                g.forward()

        for h in range(hpr):
            @pl.when(hp == h)
            def _(h=h):
                if h > 0:
                    pieces.wait_recv(range(3 * h, 3 * h + 3))
                fetch = []
                for seg in range(3):
                    p, lo = _qkv_piece(h, seg)
                    fetch.append(pltpu.make_async_copy(w_ref.at[p, :, lo:lo + 128], w_tile.at[:, seg * 128:(seg + 1) * 128],
                                                       tile_sems.at[seg]))
                    fetch[-1].start()
                for cp in fetch:
                    cp.wait()

        def project(i, carry):
            rows = pl.ds(pl.multiple_of(i * PROJ_ROWS, PROJ_ROWS), PROJ_ROWS)
            qkv = _dot(u_ref[rows, :], w_tile[...])
            per = PROJ_ROWS // LAYOUT_MOD
            for seg, ref in enumerate((q_ref, k_ref, v_ref)):
                staged[seg] = qkv[:, seg * 128:(seg + 1) * 128]
                for g in range(LAYOUT_MOD):
                    dst = pl.ds(pl.multiple_of(g * (S // LAYOUT_MOD) + i * per, per), per)
                    ref[dst, :] = staged.at[seg][pl.ds(g, per, stride=LAYOUT_MOD), :]
            return carry

        lax.fori_loop(0, S // PROJ_ROWS, project, 0)

        head_sum = _head_sum_matrix()
        ones_b = jnp.ones((2 * QB, QB), BF)
        m_s[...] = jnp.full(m_s.shape, NEG, F32)
        l_s[...] = jnp.zeros(l_s.shape, F32)
        acc[...] = jnp.zeros(acc.shape, F32)

        for d in DILATIONS:
            tri_le, dist, low = _fold_masks(d)
            low_b = low.astype(F32).astype(BF)
            high_b = 1.0 - low_b
            slope = [sl_ref[2 * hp + a] * float(d) for a in range(2)]
            bias = [slope[a] * dist for a in range(2)]

            def block(b, d=d, slope=slope, bias=bias, tri_le=tri_le, low=low, low_b=low_b, high_b=high_b):
                n, cur, prev = _block_rows(b, d, S)
                has_prev = n > 0
                valid = jnp.logical_or(tri_le, has_prev)
                q2 = (cur.get(q_ref) * 0.125).astype(BF)
                qs = jnp.concatenate([q2 * low_b, q2 * high_b], axis=0)
                vp = prev.get(v_ref)
                kp_b = prev.get(k_ref).astype(BF)
                kcat = jnp.concatenate([kp_b, cur.get(k_ref).astype(BF)], axis=0)
                vcat = jnp.concatenate([vp, cur.get(v_ref)], axis=0).astype(BF)
                s2 = _dot_nt(qs, kcat)
                e2 = _dot(_hi_lo(q2.astype(F32) * kp_b.astype(F32)), head_sum)
                p_rows, alpha_h, pe_h = [], [], []
                for a in range(2):
                    sp, sc = s2[a * QB:(a + 1) * QB, :QB], s2[a * QB:(a + 1) * QB, QB:]
                    comb = jnp.where(valid, jnp.where(tri_le, sc, sp) - bias[a], NEG)
                    e = jnp.where(has_prev, e2[:, a * QB:(a + 1) * QB] - slope[a] * float(QB), NEG)
                    m_old = cur.get(m_s.at[a])
                    m_new = jnp.maximum(jnp.maximum(m_old, jnp.max(comb, axis=-1, keepdims=True)), e)
                    cur.put(m_s.at[a], m_new)
                    p = jnp.exp(comb - m_new)
                    pe_h.append(jnp.exp(e - m_new))
                    alpha_h.append(jnp.exp(m_old - m_new))
                    p_rows.append(jnp.concatenate([jnp.where(tri_le, 0.0, p).astype(BF),
                                                   jnp.where(tri_le, p, 0.0).astype(BF)], axis=1))
                pv = _dot(jnp.concatenate(p_rows, axis=0), jnp.concatenate([vcat, ones_b], axis=1))
                for a in range(2):
                    cur.put(l_s.at[a], alpha_h[a] * cur.get(l_s.at[a]) + pv[a * QB:(a + 1) * QB, QB:] + pe_h[a])
                cur.put(acc, jnp.where(low, alpha_h[0], alpha_h[1]) * cur.get(acc)
                        + jnp.where(low, pv[:QB, :QB], pv[QB:, :QB]) + jnp.where(low, pe_h[0], pe_h[1]) * vp)

            def several(it, carry, block=block):
                for u in range(ATT_UNROLL):
                    block(it * ATT_UNROLL + u)
                return carry

            lax.fori_loop(0, n_blocks // ATT_UNROLL, several, 0)

        low = _fold_masks(LAYOUT_MOD)[2]

        def finish(i, carry):
            rows = pl.ds(pl.multiple_of(i * QB, QB), QB)
            l0, l1 = l_s[0, rows, :], l_s[1, rows, :]
            o_ref[_natural_rows(i, S), :] = acc[rows, :] / jnp.where(low, l0, l1)
            lse_ref[0, rows, :] = m_s[0, rows, :] + jnp.log(l0)
            lse_ref[1, rows, :] = m_s[1, rows, :] + jnp.log(l1)
            return carry

        lax.fori_loop(0, n_blocks, finish, 0)

        @pl.when(hp == hpr - 1)
        def _():
            pieces.wait_send(later)
            for g in gathers:
                g.finish()

    col = pl.BlockSpec((S, 128), lambda h: (0, h))
    act = jax.ShapeDtypeStruct((S, D), F32)
    gathered = (w_all, w3_all, cw_all)
    return pl.pallas_call(
        body, name="attn_fwd", grid=(hpr,),
        in_specs=[SMEM_SPEC, VMEM_SPEC, ANY_SPEC, ANY_SPEC, ANY_SPEC],
        out_specs=(col, pl.BlockSpec((2, S, 128), lambda h: (0, 0, h)), col, col, col, ANY_SPEC, ANY_SPEC, ANY_SPEC),
        out_shape=(act, jax.ShapeDtypeStruct((2, S, D), F32), act, act, act,
                   *[jax.ShapeDtypeStruct(t.shape, t.dtype) for t in gathered]),
        scratch_shapes=([pltpu.VMEM((S, 128), F32), pltpu.VMEM((2, S, 128), F32), pltpu.VMEM((2, S, 128), F32),
                         pltpu.VMEM((D, 3 * 128), BF), pltpu.VMEM((3, PROJ_ROWS, 128), F32),
                         pltpu.SemaphoreType.DMA((3,))]
                        + _piece_sems(3 * hpr) + WEIGHT_GATHER_SEMS * 3),
        input_output_aliases={2: 5, 3: 6, 4: 7},
        compiler_params=_params(1),
    )(slopes, u, *gathered)


def _set_rows(shape, rows):
    idx = lax.broadcasted_iota(jnp.int32, shape, 0)
    out = jnp.zeros(shape, F32)
    for r, val in rows.items():
        out = out + jnp.where(idx == r, val, 0.0)
    return out


def _mid(yc_in, pa_mid, o, x2, target, b_merge, final_g, w3):
    S = x2.shape[0]
    tm = ROW_TILE
    nsteps = S // tm
    tile = pl.BlockSpec((tm, D), lambda i: (i, 0))

    def body(yc_ref, za_ref, gcp_ref, gap_ref, o_ref, x_ref, t_ref, b_ref, fg_ref, w_ref,
             dh_ref, dmid_ref, do_ref, dyc_ref, gw_ref, small_ref, acc, stage):
        i = pl.program_id(0)

        @pl.when(i == 0)
        def _():
            acc[...] = jnp.zeros_like(acc)
            small_ref[...] = jnp.zeros_like(small_ref)

        wc, wa, wo = w_ref[0], w_ref[1], w_ref[2]
        z = za_ref[...].astype(F32)
        sg = _sigmoid(z)
        ov = o_ref[...]
        yc_in_b, ya_in_b = yc_ref[...], (z * sg * ov).astype(BF)
        yc = _dot(yc_in_b, wc)
        ya = _dot(ya_in_b, wa)
        b = b_ref[...]
        gc = _sigmoid(gcp_ref[...].astype(F32) + b[:, :D])
        ga = _sigmoid(gap_ref[...].astype(F32) + b[:, D:])
        merged = gc * yc + ga * ya
        merged_b = merged.astype(BF)
        h = x_ref[...] + _dot(merged_b, wo)
        r2 = lax.rsqrt(jnp.mean(h * h, axis=-1, keepdims=True) + EPS)
        n = h * r2
        fg = fg_ref[...]
        err = n * fg - t_ref[...]
        loss = 0.5 * jnp.sum(jnp.sum(err * err, axis=-1, keepdims=True) / D, axis=0, keepdims=True)
        dy = err / D
        g_fg = jnp.sum(dy * n, axis=0, keepdims=True)
        dn = dy * fg
        dh = r2 * (dn - n * jnp.mean(dn * n, axis=-1, keepdims=True))
        dh_ref[...] = dh
        dh_b = dh.astype(BF)
        dmerged = _dot_nt(dh_b, wo)
        acc[2] += _dot(merged.T.astype(BF), dh_b)
        dyc = (dmerged * gc).astype(BF)
        dya = (dmerged * ga).astype(BF)
        dgcp = dmerged * yc * gc * (1.0 - gc)
        dgap = dmerged * ya * ga * (1.0 - ga)
        dmid_ref[1] = dgcp.astype(BF)
        dmid_ref[2] = dgap.astype(BF)
        acc[0] += _dot(yc_in_b.astype(F32).T.astype(BF), dyc)
        acc[1] += _dot(ya_in_b.astype(F32).T.astype(BF), dya)
        dyc_ref[...] = _dot_nt(dyc, wc).astype(BF)
        dya_in = _dot_nt(dya, wa)
        do_ref[...] = dya_in * (z * sg)
        dmid_ref[0] = (dya_in * ov * (sg * (1.0 + z * (1.0 - sg)))).astype(BF)
        small_ref[...] += _set_rows((8, D), {
            1: jnp.sum(dgcp, axis=0, keepdims=True), 2: jnp.sum(dgap, axis=0, keepdims=True),
            3: g_fg, 7: jnp.broadcast_to(loss, (1, D))})

        @pl.when(i == nsteps - 1)
        def _():
            for p in range(N_DEV):
                for a in range(3):
                    stage[...] = acc[a, p * ROW_SHARD:(p + 1) * ROW_SHARD, :].astype(BF)
                    pltpu.sync_copy(stage, gw_ref.at[p, a])

    return pl.pallas_call(
        body, name="mid", grid=(nsteps,),
        in_specs=[tile, pl.BlockSpec((tm, D), lambda i: (i, 0)), pl.BlockSpec((tm, D), lambda i: (i, 1)),
                  pl.BlockSpec((tm, D), lambda i: (i, 2)), tile, tile, tile,
                  pl.BlockSpec((1, 2 * D), lambda i: (0, 0)), pl.BlockSpec((1, D), lambda i: (0, 0)), VMEM_SPEC],
        out_specs=(tile, pl.BlockSpec((3, tm, D), lambda i: (0, i, 0)), tile, tile,
                   ANY_SPEC, pl.BlockSpec((8, D), lambda i: (0, 0))),
        out_shape=(jax.ShapeDtypeStruct((S, D), F32), jax.ShapeDtypeStruct((3, S, D), BF),
                   jax.ShapeDtypeStruct((S, D), F32), jax.ShapeDtypeStruct((S, D), BF),
                   jax.ShapeDtypeStruct((N_DEV, 3, ROW_SHARD, D), BF), jax.ShapeDtypeStruct((8, D), F32)),
        scratch_shapes=[pltpu.VMEM((3, D, D), F32), pltpu.VMEM((ROW_SHARD, D), BF)],
        compiler_params=_params(1),
    )(yc_in, pa_mid, pa_mid, pa_mid, o, x2, target, b_merge, final_g, w3)


def _conv_bwd(dyc_in, pa, cw8):
    S = pa.shape[0]
    tm, tc = CONV_TM, CONV_TC
    nct = D // tc
    nrt = S // tm
    last_halo = S // HALO - 1

    def seg(s):
        return pl.BlockSpec((tm, tc), lambda j, i, s=s: (i, s * nct + j))

    def halo_before(s):
        return pl.BlockSpec((HALO, tc), lambda j, i, s=s: (jnp.maximum(i * (tm // HALO) - 1, 0), s * nct + j))

    def halo_after(s):
        return pl.BlockSpec((HALO, tc), lambda j, i, s=s: (jnp.minimum((i + 1) * (tm // HALO), last_halo), s * nct + j))

    def body(dy, xc, bg, cg, zc, xch, cgh, dyn, bgn, zcn, cw, dout, gcw):
        i = pl.program_id(1)

        @pl.when(i == 0)
        def _():
            gcw[...] = jnp.zeros_like(gcw)

        xcv, cgv = xc[...].astype(F32), cg[...].astype(F32)
        a = cgv * xcv
        ah = jnp.where(i > 0, cgh[...].astype(F32) * xch[...].astype(F32), 0.0)
        row = lax.broadcasted_iota(jnp.int32, (tm, tc), 0)
        a1 = jnp.where(row == 0, ah[HALO - 1:HALO, :], pltpu.roll(a, 1, 0))
        a2 = jnp.where(row == 0, ah[HALO - 2:HALO - 1, :],
                       jnp.where(row == 1, ah[HALO - 1:HALO, :], pltpu.roll(a, 2, 0)))
        w = cw[...]
        conv = w[0:1, :] * a2 + w[1:2, :] * a1 + w[2:3, :] * a
        z = zc[...].astype(F32)
        sg = _sigmoid(z)
        silu = z * sg
        bgv = bg[...].astype(F32)
        dyv = dy[...].astype(F32)
        dout[3] = (dyv * bgv * conv * (sg * (1.0 + z * (1.0 - sg)))).astype(BF)
        dout[1] = (dyv * silu * conv).astype(BF)
        dc = dyv * silu * bgv
        zn = zcn[...].astype(F32)
        dcn = dyn[...].astype(F32) * (zn * _sigmoid(zn)) * bgn[...].astype(F32)
        dcn = jnp.where(i < nrt - 1, dcn, 0.0)
        dc1 = jnp.where(row == tm - 1, dcn[0:1, :], pltpu.roll(dc, tm - 1, 0))
        dc2 = jnp.where(row == tm - 1, dcn[1:2, :],
                        jnp.where(row == tm - 2, dcn[0:1, :], pltpu.roll(dc, tm - 2, 0)))
        da = w[2:3, :] * dc + w[1:2, :] * dc1 + w[0:1, :] * dc2
        dout[2] = (da * xcv).astype(BF)
        dout[0] = (da * cgv).astype(BF)
        gcw[...] += _set_rows((8, tc), {
            4: jnp.sum(dc * a2, axis=0, keepdims=True), 5: jnp.sum(dc * a1, axis=0, keepdims=True),
            6: jnp.sum(dc * a, axis=0, keepdims=True)})

    return pl.pallas_call(
        body, name="conv_bwd", grid=(nct, nrt),
        in_specs=[pl.BlockSpec((tm, tc), lambda j, i: (i, j)), seg(0), seg(1), seg(2), seg(3),
                  halo_before(0), halo_before(2),
                  pl.BlockSpec((HALO, tc), lambda j, i: (jnp.minimum((i + 1) * (tm // HALO), last_halo), j)),
                  halo_after(1), halo_after(3), pl.BlockSpec((8, tc), lambda j, i: (0, j))],
        out_specs=(pl.BlockSpec((4, tm, tc), lambda j, i: (0, i, j)), pl.BlockSpec((8, tc), lambda j, i: (0, j))),
        out_shape=(jax.ShapeDtypeStruct((4, S, D), BF), jax.ShapeDtypeStruct((8, D), F32)),
        compiler_params=_params(2),
    )(dyc_in, pa, pa, pa, pa, pa, pa, dyc_in, pa, pa, cw8)


def _attn_bwd(q, k, v, slopes, do, o, lse, g_in, g_3):
    S = q.shape[0]
    hpr = HEAD_PAIRS
    n_blocks = S // QB

    def body(sl_ref, q_ref, k_ref, v_ref, do_ref, o_ref, lse_ref, gin_ref, g3_ref, out_ref, rin_ref, r3_ref,
             dq_s, dk_s, dv_s, do_s, dd_s, *sems):
        hp = pl.program_id(0)
        exchanges = (_GradExchange(gin_ref, rin_ref, *sems[:3], _shard_cols((0, SEG0_ATTN * D), (SEG0_MID * D, IN_COLS))),
                     _GradExchange(g3_ref, r3_ref, *sems[3:], _whole))

        @pl.when(hp == 0)
        def _():
            for ex in exchanges:
                ex.start()

        head_sum = _head_sum_matrix()
        dq_s[...] = jnp.zeros(dq_s.shape, F32)
        dk_s[...] = jnp.zeros(dk_s.shape, F32)
        dv_s[...] = jnp.zeros(dv_s.shape, F32)

        def row_dots(i, carry):
            rows = pl.ds(pl.multiple_of(i * QB, QB), QB)
            natural = _natural_rows(i, S)
            do_c = do_ref[natural, :]
            do_s[rows, :] = do_c
            dd = _dot(_hi_lo(do_c * o_ref[natural, :]), head_sum)
            dd_s[0, rows, :] = dd[:, :QB]
            dd_s[1, rows, :] = dd[:, QB:]
            return carry

        lax.fori_loop(0, n_blocks, row_dots, 0)

        for d in DILATIONS:
            tri_le, dist, low = _fold_masks(d)
            low_b = low.astype(F32).astype(BF)
            high_b = 1.0 - low_b
            slope = [sl_ref[2 * hp + a] * float(d) for a in range(2)]
            bias = [slope[a] * dist for a in range(2)]

            def block(b, d=d, slope=slope, bias=bias, tri_le=tri_le, low=low, low_b=low_b, high_b=high_b):
                n, cur, prev = _block_rows(b, d, S)
                has_prev = n > 0
                valid = jnp.logical_or(tri_le, has_prev)
                q2f = cur.get(q_ref) * 0.125
                q2 = q2f.astype(BF)
                qs = jnp.concatenate([q2 * low_b, q2 * high_b], axis=0)
                kp, vp = prev.get(k_ref), prev.get(v_ref)
                kp_b, vp_b = kp.astype(BF), vp.astype(BF)
                kcat = jnp.concatenate([kp_b, cur.get(k_ref).astype(BF)], axis=0)
                vcat = jnp.concatenate([vp_b, cur.get(v_ref).astype(BF)], axis=0)
                do2f = cur.get(do_s)
                do2 = do2f.astype(BF)
                dos = jnp.concatenate([do2 * low_b, do2 * high_b], axis=0)
                s2 = _dot_nt(qs, kcat)
                dp2 = _dot_nt(dos, vcat)
                diag2 = _dot(jnp.concatenate([_hi_lo(q2.astype(F32) * kp_b.astype(F32)),
                                              _hi_lo(do2.astype(F32) * vp_b.astype(F32))], axis=0), head_sum)
                p_rows, ds_rows, pe_h, dse_h = [], [], [], []
                for a in range(2):
                    hs = slice(a * QB, (a + 1) * QB)
                    sp, sc = s2[hs, :QB], s2[hs, QB:]
                    dpp, dpc = dp2[hs, :QB], dp2[hs, QB:]
                    lse_a, dd_a = cur.get(lse_ref.at[a]), cur.get(dd_s.at[a])
                    comb = jnp.where(tri_le, sc, sp) - bias[a]
                    e = diag2[:QB, hs] - slope[a] * float(QB)
                    p = jnp.where(valid, jnp.exp(comb - lse_a), 0.0)
                    pe = jnp.where(has_prev, jnp.exp(e - lse_a), 0.0)
                    ds = p * (jnp.where(tri_le, dpc, dpp) - dd_a)
                    dse_h.append(pe * (diag2[QB:, hs] - dd_a))
                    pe_h.append(pe)
                    p_rows.append(jnp.concatenate([jnp.where(tri_le, 0.0, p).astype(BF),
                                                   jnp.where(tri_le, p, 0.0).astype(BF)], axis=1))
                    ds_rows.append(jnp.concatenate([jnp.where(tri_le, 0.0, ds).astype(BF),
                                                    jnp.where(tri_le, ds, 0.0).astype(BF)], axis=1))
                pst = jnp.concatenate(p_rows, axis=0)
                dst = jnp.concatenate(ds_rows, axis=0)
                pe2 = jnp.where(low, pe_h[0], pe_h[1])
                dse2 = jnp.where(low, dse_h[0], dse_h[1])
                dq = _dot(dst, kcat)
                cur.add(dq_s, (jnp.where(low, dq[:QB], dq[QB:]) + dse2 * kp) * 0.125)
                dk = _dot_tn(dst, qs)
                dv = _dot_tn(pst, dos)
                prev.add(dk_s, dk[:QB] + dse2 * q2f)
                cur.add(dk_s, dk[QB:])
                prev.add(dv_s, dv[:QB] + pe2 * do2f)
                cur.add(dv_s, dv[QB:])

            def several(it, carry, block=block):
                for u in range(ATT_UNROLL):
                    block(it * ATT_UNROLL + u)
                return carry

            lax.fori_loop(0, n_blocks // ATT_UNROLL, several, 0)

        def finish(i, carry):
            rows = pl.ds(pl.multiple_of(i * QB, QB), QB)
            natural = _natural_rows(i, S)
            for t, ref in enumerate((dq_s, dk_s, dv_s)):
                out_ref.at[t][natural, :] = ref[rows, :]
            return carry

        lax.fori_loop(0, n_blocks, finish, 0)

        @pl.when(hp == hpr - 1)
        def _():
            for ex in exchanges:
                ex.finish()

    col = pl.BlockSpec((S, 128), lambda h: (0, h))
    return pl.pallas_call(
        body, name="attn_bwd", grid=(hpr,),
        in_specs=[SMEM_SPEC, col, col, col, col, col, pl.BlockSpec((2, S, 128), lambda h: (0, 0, h)),
                  ANY_SPEC, ANY_SPEC],
        out_specs=(pl.BlockSpec((3, S, 128), lambda h: (0, 0, h)), ANY_SPEC, ANY_SPEC),
        out_shape=(jax.ShapeDtypeStruct((3, S, D), F32), jax.ShapeDtypeStruct(g_in.shape, BF),
                   jax.ShapeDtypeStruct(g_3.shape, BF)),
        scratch_shapes=([pltpu.VMEM((S, 128), F32)] * 4 + [pltpu.VMEM((2, S, 128), F32)]
                        + GRAD_EXCHANGE_SEMS + GRAD_EXCHANGE_SEMS),
        compiler_params=_params(1),
    )(slopes, q, k, v, do, o, lse, g_in, g_3)


WG_TN = 256
SEG0_CONV, SEG0_ATTN, SEG0_MID = 0, 4, 7


def _wgrad_in(ut, d_group, seg0, g_in, name):
    S = ut.shape[1]
    tn = WG_TN
    per_seg = D // tn
    per_shard = W_IN_SHARD // tn
    n_tiles = d_group.shape[0] * per_seg
    tile0 = seg0 * per_seg

    def body(ut_ref, d_ref, *rest):
        rest[-1][0] = _dot(ut_ref[...], d_ref[0].astype(BF)).astype(BF)

    operands, in_specs, aliases = [ut, d_group], [VMEM_SPEC, pl.BlockSpec((1, S, tn), lambda t: (t // per_seg, 0, t % per_seg))], {}
    if g_in is not None:
        operands.append(g_in)
        in_specs.append(ANY_SPEC)
        aliases = {2: 0}
    return pl.pallas_call(
        body, name=name, grid=(n_tiles,), in_specs=in_specs,
        out_specs=pl.BlockSpec((1, D, tn), lambda t: ((tile0 + t) // per_shard, 0, (tile0 + t) % per_shard)),
        out_shape=jax.ShapeDtypeStruct((N_DEV, D, W_IN_SHARD), BF),
        input_output_aliases=aliases,
        compiler_params=_params(1),
    )(*operands)


def _dgrad_norm_bwd(d_conv, d_attn, d_mid, w_all, x2, dh, norm_g):
    S = x2.shape[0]
    tm = ROW_TILE
    nsteps = S // tm
    tile = pl.BlockSpec((tm, D), lambda i: (i, 0))
    pieces = _proj_pieces()

    def body(a_ref, b_ref, c_ref, w_ref, x_ref, dh_ref, g_ref, gx_ref, small_ref):
        i = pl.program_id(0)

        @pl.when(i == 0)
        def _():
            small_ref[...] = jnp.zeros_like(small_ref)

        groups = (a_ref, b_ref, c_ref)
        du = jnp.zeros((tm, D), F32)
        for s, sc, p, pc, width in pieces:
            g = 0 if s < 4 else (1 if s < 7 else 2)
            local = s - (0, 4, 7)[g]
            du = du + _dot_nt(groups[g][local, :, sc:sc + width].astype(BF), w_ref[p, :, pc:pc + width])
        xv = x_ref[...]
        r = lax.rsqrt(jnp.mean(xv * xv, axis=-1, keepdims=True) + EPS)
        n = xv * r
        dn = du * g_ref[...]
        gx_ref[...] = dh_ref[...] + r * (dn - n * jnp.mean(dn * n, axis=-1, keepdims=True))
        small_ref[...] += _set_rows((8, D), {0: jnp.sum(du * n, axis=0, keepdims=True)})

    return pl.pallas_call(
        body, name="dgrad_norm_bwd", grid=(nsteps,),
        in_specs=[pl.BlockSpec((4, tm, D), lambda i: (0, i, 0)), pl.BlockSpec((3, tm, D), lambda i: (0, i, 0)),
                  pl.BlockSpec((3, tm, D), lambda i: (0, i, 0)), VMEM_SPEC, tile, tile,
                  pl.BlockSpec((1, D), lambda i: (0, 0))],
        out_specs=(tile, pl.BlockSpec((8, D), lambda i: (0, 0))),
        out_shape=(jax.ShapeDtypeStruct((S, D), F32), jax.ShapeDtypeStruct((8, D), F32)),
        compiler_params=_params(1),
    )(d_conv, d_attn, d_mid, w_all, x2, dh, norm_g)


HBM_SPEC = pl.BlockSpec(memory_space=pltpu.HBM)
SEM_SPEC = pl.BlockSpec(memory_space=pltpu.SEMAPHORE)
ATTN_COLS = _shard_cols((SEG0_ATTN * D, SEG0_MID * D))


def _attn_cols_exchange_start(g_in, r_in):
    def body(g_ref, r_ref, send_sems, recv_sems, g_thru, r_thru, token):
        _GradExchange(g_ref, r_ref, send_sems, recv_sems, None, ATTN_COLS).start()
        token[...] = jnp.zeros_like(token)

    hbm = pltpu.with_memory_space_constraint
    return pl.pallas_call(
        body, name="attn_cols_exchange_start",
        out_shape=(pltpu.SemaphoreType.DMA((N_DEV,)), pltpu.SemaphoreType.DMA((N_DEV,)),
                   pltpu.HBM(g_in.shape, g_in.dtype), pltpu.HBM(r_in.shape, r_in.dtype),
                   jax.ShapeDtypeStruct((8, 128), F32)),
        in_specs=(HBM_SPEC, HBM_SPEC), out_specs=(SEM_SPEC, SEM_SPEC, HBM_SPEC, HBM_SPEC, VMEM_SPEC),
        input_output_aliases={0: 2, 1: 3},
        compiler_params=pltpu.CompilerParams(has_side_effects=pltpu.SideEffectType.DATAFLOW_SIDE_EFFECTING),
    )(hbm(g_in, pltpu.HBM), hbm(r_in, pltpu.HBM))


def _attn_cols_exchange_wait(send_sems, recv_sems, g_thru, r_thru, after):
    def body(g_ref, r_ref, send_sems, recv_sems, after_ref, g_dead, r_out):
        _GradExchange(g_ref, r_ref, send_sems, recv_sems, None, ATTN_COLS).finish()

    return pl.pallas_call(
        body, name="attn_cols_exchange_wait",
        out_shape=(pltpu.HBM(g_thru.shape, g_thru.dtype), pltpu.HBM(r_thru.shape, r_thru.dtype)),
        in_specs=(HBM_SPEC, HBM_SPEC, SEM_SPEC, SEM_SPEC, ANY_SPEC), out_specs=(HBM_SPEC, HBM_SPEC),
        input_output_aliases={0: 0, 1: 1},
        compiler_params=pltpu.CompilerParams(has_side_effects=pltpu.SideEffectType.DATAFLOW_SIDE_EFFECTING),
    )(g_thru, r_thru, send_sems, recv_sems, after)


def _adamw_math(w, g, m, v):
    m = ADAM_B1 * m + (1.0 - ADAM_B1) * g
    v = ADAM_B2 * v + (1.0 - ADAM_B2) * (g * g)
    m_hat = m / (1.0 - ADAM_B1 ** ADAM_STEP)
    v_hat = v / (1.0 - ADAM_B2 ** ADAM_STEP)
    delta = -ADAM_LR * (m_hat / (jnp.sqrt(v_hat) + ADAM_EPS) + ADAM_WD * w)
    return delta, m, v


def _sum_adamw(parts, w, m, v, tm, name):
    R, C = w.shape
    tile = pl.BlockSpec((tm, C), lambda i: (i, 0))

    def body(p_ref, w_ref, m_ref, v_ref, g_out, d_out, m_out, v_out):
        g = p_ref[0].astype(F32)
        for s in range(1, N_DEV):
            g = g + p_ref[s].astype(F32)
        g_out[...] = g
        d_out[...], m_out[...], v_out[...] = _adamw_math(w_ref[...], g, m_ref[...], v_ref[...])

    shape = jax.ShapeDtypeStruct((R, C), F32)
    return pl.pallas_call(
        body, name=name, grid=(R // tm,),
        in_specs=[pl.BlockSpec((N_DEV, tm, C), lambda i: (0, i, 0)), tile, tile, tile],
        out_specs=(tile, tile, tile, tile), out_shape=(shape, shape, shape, shape),
        compiler_params=_params(1),
    )(parts, w, m, v)


def _adamw(g, w, m, v, name):
    def body(g_ref, w_ref, m_ref, v_ref, d_out, m_out, v_out):
        d_out[...], m_out[...], v_out[...] = _adamw_math(w_ref[...], g_ref[...], m_ref[...], v_ref[...])

    shape = jax.ShapeDtypeStruct(w.shape, F32)
    return pl.pallas_call(
        body, name=name, in_specs=[VMEM_SPEC] * 4, out_specs=(VMEM_SPEC,) * 3, out_shape=(shape, shape, shape),
    )(g, w, m, v)


def _alibi_slopes():
    return jnp.exp2(-8.0 * jnp.arange(1, N_HEADS + 1, dtype=F32) / N_HEADS)


def _local_step(x2, target, norm_g, b_merge, final_g, w_all, w3_all, cw_all):
    slopes = _alibi_slopes()
    u, ut = _norm(x2, norm_g)
    o, lse, q, k, v, w_all, w3_all, cw_all = _attn_fwd(u, slopes, w_all, w3_all, cw_all)
    w3 = jnp.transpose(w3_all, (1, 0, 2, 3)).reshape(3, D, D)
    cw8 = jnp.transpose(cw_all, (1, 0, 2)).reshape(8, D)
    pa = _proj_cols(u, w_all, SEG0_CONV, 4, BF, "proj_conv")
    yc_in = _conv_fwd(pa, cw8)
    pa_mid = _proj_cols(u, w_all, SEG0_MID, 3, BF, "proj_mid")
    dh, d_mid, do, dyc_in, g_3, small_mid = _mid(yc_in, pa_mid, o, x2, target, b_merge, final_g, w3)
    g_in = _wgrad_in(ut, d_mid, SEG0_MID, None, "wgrad_in_mid")
    d_conv, small_conv = _conv_bwd(dyc_in, pa, cw8)
    g_in = _wgrad_in(ut, d_conv, SEG0_CONV, g_in, "wgrad_in_conv")
    d_attn, r_in, r_3 = _attn_bwd(q, k, v, slopes, do, o, lse, g_in, g_3)
    g_in = _wgrad_in(ut, d_attn, SEG0_ATTN, g_in, "wgrad_in_attn")
    *in_flight, token = _attn_cols_exchange_start(g_in, r_in)
    grad_x, small_norm = _dgrad_norm_bwd(d_conv, d_attn, d_mid, w_all, x2, dh, norm_g + token[0:1, 0:1])
    return grad_x, in_flight, r_3, small_mid, small_conv, small_norm


def kernel(x, norm_g, w_in, b_merge, conv_w, w_out_conv, w_out_attn, w_o, final_g, loss_target, m_norm_g, m_w_in, m_b_merge, m_conv_w, m_w_out_conv, m_w_out_attn, m_w_o, m_final_g, v_norm_g, v_w_in, v_b_merge, v_conv_w, v_w_out_conv, v_w_out_attn, v_w_o, v_final_g):
    me = 4 * lax.axis_index("x") + 2 * lax.axis_index("y") + lax.axis_index("c")
    stack3 = lambda a, b, c: jnp.concatenate([a, b, c], axis=0)
    pad8 = lambda a: jnp.pad(a, ((0, 8 - a.shape[0]), (0, 0)))

    w3_shard = stack3(w_out_conv, w_out_attn, w_o)
    w_all, w3_all, cw_all = _gather_first_weights(w_in[0], w3_shard, pad8(conv_w[0]))

    final_g2 = final_g.reshape(1, D)
    grad_x, in_flight, r_3, small_mid, small_conv, small_norm = _local_step(
        x[0], loss_target[0], norm_g, b_merge, final_g2, w_all, w3_all, cw_all)

    small = _allreduce_small(small_mid, small_conv, small_norm)
    g_in, r_in = _attn_cols_exchange_wait(*in_flight, small)
    own = lax.dynamic_index_in_dim(g_in, me, 0, keepdims=True)
    r_in = lax.dynamic_update_slice(r_in, own, (me, 0, 0))

    g_w_in, d_w_in, nm_w_in, nv_w_in = _sum_adamw(r_in, w_in[0], m_w_in[0], v_w_in[0], 128, "adamw_w_in")
    g_w3, d_w3, nm_w3, nv_w3 = _sum_adamw(
        r_3.reshape(N_DEV, 3 * ROW_SHARD, D), w3_shard.reshape(3 * ROW_SHARD, D),
        stack3(m_w_out_conv, m_w_out_attn, m_w_o).reshape(3 * ROW_SHARD, D),
        stack3(v_w_out_conv, v_w_out_attn, v_w_o).reshape(3 * ROW_SHARD, D), ROW_SHARD, "adamw_w3")

    def pack(ng, bm, fg):
        return pad8(jnp.concatenate([ng, bm.reshape(2, D), fg.reshape(1, D)], axis=0))

    d_s, nm_s, nv_s = _adamw(small, pack(norm_g, b_merge, final_g), pack(m_norm_g, m_b_merge, m_final_g),
                             pack(v_norm_g, v_b_merge, v_final_g), "adamw_small")
    g_cw = lax.dynamic_slice(small, (4, me * ROW_SHARD), (3, ROW_SHARD))
    d_cw, nm_cw, nv_cw = _adamw(g_cw, conv_w[0], m_conv_w[0], v_conv_w[0], "adamw_conv_w")

    loss = small[7, 0]
    split3 = lambda t: tuple(t[a * ROW_SHARD:(a + 1) * ROW_SHARD][None] for a in range(3))
    unpack = lambda t: (t[0:1], t[1:3].reshape(1, 2 * D), t[3])

    def leaves(in_, small_, cw_, w3_):
        ng, bm, fg = unpack(small_)
        wc, wa, wo = split3(w3_)
        return (ng, in_[None], bm, cw_[None], wc, wa, wo, fg)

    return (loss, grad_x[None],
            *leaves(g_w_in, small, g_cw, g_w3),
            *leaves(d_w_in, d_s, d_cw, d_w3),
            *leaves(nm_w_in, nm_s, nm_cw, nm_w3),
            *leaves(nv_w_in, nv_s, nv_cw, nv_w3))
```

```python
import functools

import jax
import jax.numpy as jnp
from jax import lax
from jax.experimental import pallas as pl
from jax.experimental.pallas import tpu as pltpu

D = 1024
N_HEADS = 16
HEAD_DIM = 64
N_SEG = 10
IN_COLS = N_SEG * D
N_DEV = 8
W_IN_SHARD = IN_COLS // N_DEV
ROW_SHARD = D // N_DEV
QB = 128
DILATIONS = (1, 4, 16)
EPS = 1e-6
NEG = -1e30
BF = jnp.bfloat16
F32 = jnp.float32
MESH = pl.DeviceIdType.MESH

ADAM_LR = 0.001
ADAM_B1 = 0.9
ADAM_B2 = 0.999
ADAM_EPS = 1e-08
ADAM_WD = 0.01
ADAM_STEP = 10

V7X_VMEM_BYTES = 64 * 1024 * 1024
VMEM_LIMIT = V7X_VMEM_BYTES - 8 * 1024 * 1024
ROW_TILE = 256

VMEM_SPEC = pl.BlockSpec(memory_space=pltpu.VMEM)
ANY_SPEC = pl.BlockSpec(memory_space=pl.ANY)
SMEM_SPEC = pl.BlockSpec(memory_space=pltpu.SMEM)


def _params(n_grid_axes, vmem=VMEM_LIMIT):
    return pltpu.CompilerParams(dimension_semantics=("arbitrary",) * n_grid_axes, vmem_limit_bytes=vmem)


def _dot(a, b):
    return jnp.dot(a, b, preferred_element_type=F32)


def _dot_nt(a, b):
    return lax.dot_general(a, b, (((1,), (1,)), ((), ())), preferred_element_type=F32)


def _dot_tn(a, b):
    return lax.dot_general(a, b, (((0,), (0,)), ((), ())), preferred_element_type=F32)


def _sigmoid(z):
    return 1.0 / (1.0 + jnp.exp(-z))


def _my_place():
    x, y, c = lax.axis_index("x"), lax.axis_index("y"), lax.axis_index("c")
    return x, y, c, 4 * x + 2 * y + c


def _peers(x, y, c):
    out = []
    for k in range(1, N_DEV):
        px = 1 - x if k & 4 else x
        py = 1 - y if k & 2 else y
        pc = 1 - c if k & 1 else c
        out.append(((px, py, pc), 4 * px + 2 * py + pc))
    return out


def _device(p):
    return (p >> 2, (p >> 1) & 1, p & 1)


def _shard_cols(*ranges):
    def cols(p):
        found = None
        for lo, hi in ranges:
            a, b = max(lo, p * W_IN_SHARD), min(hi, (p + 1) * W_IN_SHARD)
            if a < b:
                assert found is None
                found = (a - p * W_IN_SHARD, b - p * W_IN_SHARD)
        return found

    return cols


def _whole(p):
    return ()


def _block(ref, idx, cols):
    return ref.at[idx] if cols == () else ref.at[idx, :, cols[0]:cols[1]]


class _WeightGather:
    def __init__(self, src, dst, send_sems, forward_sems, recv_sems, cols):
        self.src, self.dst, self.cols = src, dst, cols
        self.send_sems, self.forward_sems, self.recv_sems = send_sems, forward_sems, recv_sems
        self.me = _my_place()[3]

    def _copy(self, p, target, passing_on=False):
        cols = self.cols(p)
        return pltpu.make_async_remote_copy(
            src_ref=_block(self.dst, p, cols) if passing_on else self.src(p, cols), dst_ref=_block(self.dst, p, cols),
            send_sem=self.forward_sems.at[p] if passing_on else self.send_sems.at[target],
            recv_sem=self.recv_sems.at[p], device_id=_device(target), device_id_type=MESH)

    def _as_each_device(self, own, relayed, other):
        for m in range(N_DEV):
            def branch(m=m):
                for p in range(N_DEV):
                    if self.cols(p) is None:
                        continue
                    if p == m:
                        for t in [m ^ 1] + [q for q in range(N_DEV) if q >> 1 != m >> 1 and q & 1 == m & 1]:
                            own(self._copy(m, t))
                    elif p >> 1 != m >> 1 and p & 1 == m & 1:
                        relayed(p, m ^ 1)
                    else:
                        other(p)

            pl.when(self.me == m)(branch)

    def start(self):
        self._as_each_device(lambda cp: cp.start(), lambda p, t: None, lambda p: None)

    def forward(self):
        def pass_on(p, t):
            self._copy(p, p).wait_recv()
            self._copy(p, t, passing_on=True).start()

        self._as_each_device(lambda cp: None, pass_on, lambda p: None)

    def finish(self):
        self._as_each_device(lambda cp: cp.wait_send(), lambda p, t: self._copy(p, t, passing_on=True).wait_send(),
                             lambda p: self._copy(p, p).wait_recv())


WEIGHT_GATHER_SEMS = [pltpu.SemaphoreType.DMA((N_DEV,))] * 3
FORWARD_STEP = 6
REST_COLS = _shard_cols((0, 4 * D), (7 * D, IN_COLS))
HEAD_PAIRS = D // 128


def _qkv_piece(h, seg):
    col = (4 + seg) * D + 128 * h
    return col // W_IN_SHARD, col % W_IN_SHARD


class _PieceGather:
    def __init__(self, src, dst, send_sems, recv_sems):
        self.src, self.dst, self.send_sems, self.recv_sems = src, dst, send_sems, recv_sems
        self.me = _my_place()[3]

    def _copy(self, i, target):
        p, lo = _qkv_piece(i // 3, i % 3)
        return pltpu.make_async_remote_copy(
            src_ref=self.src(p, lo, lo + 128), dst_ref=self.dst.at[p, :, lo:lo + 128], send_sem=self.send_sems.at[i, target],
            recv_sem=self.recv_sems.at[i], device_id=_device(target), device_id_type=MESH)

    def _owner(self, i, act):
        p = _qkv_piece(i // 3, i % 3)[0]

        def sender():
            for k in range(N_DEV - 1):
                act(self._copy(i, (p + 1 + (k + i) % (N_DEV - 1)) % N_DEV))

        pl.when(self.me == p)(sender)

    def start(self, pieces):
        for i in pieces:
            self._owner(i, lambda cp: cp.start())

    def wait_send(self, pieces):
        for i in pieces:
            self._owner(i, lambda cp: cp.wait_send())

    def wait_recv(self, pieces):
        for i in pieces:
            p = _qkv_piece(i // 3, i % 3)[0]
            pl.when(self.me != p)(lambda i=i, p=p: self._copy(i, p).wait_recv())


def _piece_sems(n):
    return [pltpu.SemaphoreType.DMA((n, N_DEV)), pltpu.SemaphoreType.DMA((n,))]


def _norm_gather_first_weights(x2, norm_g, w_in, w3, cw):
    S = x2.shape[0]
    tm = ROW_TILE
    nsteps = S // tm

    def body(x_ref, g_ref, w_in_ref, w3_ref, cw_ref, u_ref, ut_ref, o_in, o_3, o_cw, in_bf, w3_bf, local_sems, *sems):
        i = pl.program_id(0)
        me = _my_place()[3]
        gather = _PieceGather(lambda p, lo, hi: in_bf.at[:, lo:hi], o_in, *sems)
        local = [pltpu.make_async_copy(src, dst.at[me], local_sems.at[a])
                 for a, (src, dst) in enumerate(((in_bf, o_in), (w3_bf, o_3), (cw_ref, o_cw)))]

        @pl.when(i == 0)
        def _():
            def cast_rows(r, carry):
                rows = pl.ds(pl.multiple_of(r * 128, 128), 128)
                in_bf[rows, :] = w_in_ref[rows, :].astype(BF)
                return carry

            lax.fori_loop(0, D // 128, cast_rows, 0)
            for a in range(3):
                w3_bf[a] = w3_ref[a].astype(BF)
            gather.start(range(3))
            for cp in local:
                cp.start()

        xv = x_ref[...]
        r = lax.rsqrt(jnp.mean(xv * xv, axis=-1, keepdims=True) + EPS)
        u = xv * r * g_ref[...]
        u_ref[...] = u.astype(BF)
        ut_ref[...] = u.T.astype(BF)

        @pl.when(i == nsteps - 1)
        def _():
            gather.wait_recv(range(3))
            gather.wait_send(range(3))
            for cp in local:
                cp.wait()

    return pl.pallas_call(
        body, name="norm_gather_first_weights", grid=(nsteps,),
        out_shape=(jax.ShapeDtypeStruct((S, D), BF), jax.ShapeDtypeStruct((D, S), BF),
                   jax.ShapeDtypeStruct((N_DEV, D, W_IN_SHARD), BF),
                   jax.ShapeDtypeStruct((N_DEV, 3, ROW_SHARD, D), BF),
                   jax.ShapeDtypeStruct((N_DEV, 8, 128), F32)),
        in_specs=[pl.BlockSpec((tm, D), lambda i: (i, 0)), pl.BlockSpec((1, D), lambda i: (0, 0)),
                  VMEM_SPEC, VMEM_SPEC, VMEM_SPEC],
        out_specs=(pl.BlockSpec((tm, D), lambda i: (i, 0)), pl.BlockSpec((D, tm), lambda i: (0, i)),
                   ANY_SPEC, ANY_SPEC, ANY_SPEC),
        scratch_shapes=[pltpu.VMEM((D, W_IN_SHARD), BF), pltpu.VMEM((3, ROW_SHARD, D), BF),
                        pltpu.SemaphoreType.DMA((3,))] + _piece_sems(3),
        compiler_params=_params(1),
    )(x2, norm_g, w_in, w3, cw)


class _GradExchange:
    def __init__(self, src, dst, send_sems, recv_sems, local_sem, cols):
        self.src, self.dst, self.cols = src, dst, cols
        self.send_sems, self.recv_sems, self.local_sem = send_sems, recv_sems, local_sem
        self.me = _my_place()[3]

    def _remote(self, p, source):
        return pltpu.make_async_remote_copy(
            src_ref=_block(self.src, p, self.cols(p)), dst_ref=_block(self.dst, source, self.cols(p)),
            send_sem=self.send_sems.at[p], recv_sem=self.recv_sems.at[source],
            device_id=_device(p), device_id_type=MESH)

    def _local(self, p):
        return pltpu.make_async_copy(_block(self.src, p, self.cols(p)), _block(self.dst, p, self.cols(p)),
                                     self.local_sem)

    def _as_each_device(self, send, local, receive):
        for m in range(N_DEV):
            def branch(m=m):
                for k in range(1, N_DEV):
                    p = (m + k) % N_DEV
                    if self.cols(p) is not None:
                        send(self._remote(p, m))
                if self.cols(m) is not None:
                    if self.local_sem is not None:
                        local(self._local(m))
                    for k in range(1, N_DEV):
                        receive(self._remote(m, (m + k) % N_DEV))

            pl.when(self.me == m)(branch)

    def start(self):
        self._as_each_device(lambda cp: cp.start(), lambda cp: cp.start(), lambda cp: None)

    def finish(self):
        self._as_each_device(lambda cp: cp.wait_send(), lambda cp: cp.wait(), lambda cp: cp.wait_recv())


GRAD_EXCHANGE_SEMS = [pltpu.SemaphoreType.DMA((N_DEV,)), pltpu.SemaphoreType.DMA((N_DEV,)), pltpu.SemaphoreType.DMA]


def _allreduce_small(p_mid, p_conv, p_norm):
    def body(a_ref, b_ref, c_ref, out_ref, mine, gathered, send_sems, recv_sems):
        x, y, c, me = _my_place()
        mine[...] = a_ref[...] + b_ref[...] + c_ref[...]
        gathered[me] = mine[...]
        remote = []
        for k, (peer, _) in enumerate(_peers(x, y, c)):
            cp = pltpu.make_async_remote_copy(
                src_ref=mine, dst_ref=gathered.at[me], send_sem=send_sems.at[k], recv_sem=recv_sems.at[k],
                device_id=peer, device_id_type=MESH)
            cp.start()
            remote.append(cp)
        for cp in remote:
            cp.wait()
        total = gathered[0]
        for s in range(1, N_DEV):
            total = total + gathered[s]
        out_ref[...] = total

    return pl.pallas_call(
        body, name="allreduce_small",
        out_shape=jax.ShapeDtypeStruct((8, D), F32),
        in_specs=[VMEM_SPEC, VMEM_SPEC, VMEM_SPEC], out_specs=VMEM_SPEC,
        scratch_shapes=[pltpu.VMEM((8, D), F32), pltpu.VMEM((N_DEV, 8, D), F32),
                        pltpu.SemaphoreType.DMA((N_DEV - 1,)), pltpu.SemaphoreType.DMA((N_DEV - 1,))],
    )(p_mid, p_conv, p_norm)


def _proj_pieces():
    cuts = sorted(set(range(0, IN_COLS + 1, D)) | set(range(0, IN_COLS + 1, W_IN_SHARD)))
    return [(lo // D, lo % D, lo // W_IN_SHARD, lo % W_IN_SHARD, hi - lo) for lo, hi in zip(cuts[:-1], cuts[1:])]


PROJ_TN = 256


def _proj_cols(u, w_all, seg0, n_seg, dtype, name):
    S = u.shape[0]
    tn = PROJ_TN
    per_shard = W_IN_SHARD // tn
    tile0 = seg0 * D // tn

    def body(u_ref, w_ref, out_ref):
        out_ref[...] = _dot(u_ref[...], w_ref[0]).astype(dtype)

    return pl.pallas_call(
        body, name=name, grid=(n_seg * D // tn,),
        in_specs=[VMEM_SPEC, pl.BlockSpec((1, D, tn), lambda t: ((tile0 + t) // per_shard, 0, (tile0 + t) % per_shard))],
        out_specs=pl.BlockSpec((S, tn), lambda t: (0, t)),
        out_shape=jax.ShapeDtypeStruct((S, n_seg * D), dtype),
        compiler_params=_params(1),
    )(u, w_all)


CONV_TM, CONV_TC = 256, 512
HALO = 16


def _conv_fwd(pa, cw8):
    S = pa.shape[0]
    tm, tc = CONV_TM, CONV_TC
    nct = D // tc

    def seg(s):
        return pl.BlockSpec((tm, tc), lambda i, j, s=s: (i, s * nct + j))

    def halo_before(s):
        return pl.BlockSpec((HALO, tc), lambda i, j, s=s: (jnp.maximum(i * (tm // HALO) - 1, 0), s * nct + j))

    def body(xc, bg, cg, zc, xch, cgh, cw, out):
        i = pl.program_id(0)
        a = cg[...].astype(F32) * xc[...].astype(F32)
        ah = cgh[...].astype(F32) * xch[...].astype(F32)
        ah = jnp.where(i > 0, ah, 0.0)
        row = lax.broadcasted_iota(jnp.int32, (tm, tc), 0)
        a1 = jnp.where(row == 0, ah[HALO - 1:HALO, :], pltpu.roll(a, 1, 0))
        a2 = jnp.where(row == 0, ah[HALO - 2:HALO - 1, :],
                       jnp.where(row == 1, ah[HALO - 1:HALO, :], pltpu.roll(a, 2, 0)))
        w = cw[...]
        conv = w[0:1, :] * a2 + w[1:2, :] * a1 + w[2:3, :] * a
        z = zc[...].astype(F32)
        out[...] = (z * _sigmoid(z) * bg[...].astype(F32) * conv).astype(BF)

    return pl.pallas_call(
        body, name="conv_fwd", grid=(S // tm, nct),
        in_specs=[seg(0), seg(1), seg(2), seg(3), halo_before(0), halo_before(2),
                  pl.BlockSpec((8, tc), lambda i, j: (0, j))],
        out_specs=pl.BlockSpec((tm, tc), lambda i, j: (i, j)),
        out_shape=jax.ShapeDtypeStruct((S, D), BF),
        compiler_params=_params(2),
    )(pa, pa, pa, pa, pa, pa, cw8)


ATT_UNROLL = 16


LAYOUT_MOD = 4
RUN = QB // LAYOUT_MOD


def _fold_masks(d):
    row = lax.broadcasted_iota(jnp.int32, (QB, QB), 0)
    lane = lax.broadcasted_iota(jnp.int32, (QB, QB), 1)
    if d == 1:
        qpos, kpos = LAYOUT_MOD * (row % RUN) + row // RUN, LAYOUT_MOD * (lane % RUN) + lane // RUN
    else:
        qpos, kpos = row, lane
    tri_le = kpos <= qpos
    dist = jnp.where(tri_le, qpos - kpos, qpos - kpos + QB).astype(F32)
    return tri_le, dist, lane < HEAD_DIM


class _Rows:
    def __init__(self, slices):
        self.slices = slices

    def get(self, ref):
        parts = [ref[sl, :] for sl in self.slices]
        return parts[0] if len(parts) == 1 else jnp.concatenate(parts, axis=0)

    def put(self, ref, val):
        size = QB // len(self.slices)
        for g, sl in enumerate(self.slices):
            ref[sl, :] = val if len(self.slices) == 1 else val[g * size:(g + 1) * size]

    def add(self, ref, val):
        self.put(ref, self.get(ref) + val)


def _block_rows(b, d, S):
    quarter = S // LAYOUT_MOD
    nb = S // (QB * d)
    r, n = b // nb, b % nb
    n_prev = jnp.maximum(n - 1, 0)
    if d == 1:
        runs = lambda m: _Rows([pl.ds(pl.multiple_of(g * quarter + RUN * m, RUN), RUN) for g in range(LAYOUT_MOD)])
        return n, runs(n), runs(n_prev)
    if d == LAYOUT_MOD:
        block = lambda m: _Rows([pl.ds(pl.multiple_of(r * quarter + QB * m, QB), QB)])
        return n, block(n), block(n_prev)
    step = d // LAYOUT_MOD
    first = (r % LAYOUT_MOD) * quarter + r // LAYOUT_MOD
    strided = lambda m: _Rows([pl.ds(first + QB * step * m, QB, stride=step)])
    return n, strided(n), strided(n_prev)


def _natural_rows(i, S):
    per = S // LAYOUT_MOD // QB
    return pl.ds(i // per + LAYOUT_MOD * QB * (i % per), QB, stride=LAYOUT_MOD)


def _head_sum_matrix():
    r = lax.broadcasted_iota(jnp.int32, (2 * QB, 2 * QB), 0)
    c = lax.broadcasted_iota(jnp.int32, (2 * QB, 2 * QB), 1)
    return (((r % QB) // HEAD_DIM) == (c // QB)).astype(F32).astype(BF)


def _hi_lo(t):
    hi = t.astype(BF)
    return jnp.concatenate([hi, (t - hi.astype(F32)).astype(BF)], axis=1)


PROJ_ROWS = 512


def _attn_fwd(u, slopes, w_all, w3_all, cw_all):
    S = u.shape[0]
    hpr = HEAD_PAIRS
    n_blocks = S // QB
    later = range(3, 3 * hpr)

    def body(sl_ref, u_ref, w_in_ref, w3_in_ref, cw_in_ref, o_ref, lse_ref, q_ref, k_ref, v_ref, w_ref, w3_ref,
             cw_ref, acc, m_s, l_s, w_tile, staged, tile_sems, *sems):
        hp = pl.program_id(0)
        me = _my_place()[3]
        pieces = _PieceGather(lambda p, lo, hi: w_ref.at[p, :, lo:hi], w_ref, *sems[0:2])
        gathers = (_WeightGather(lambda p, cols: _block(w_ref, me, cols), w_ref, *sems[2:5], REST_COLS),
                   _WeightGather(lambda p, cols: w3_ref.at[me], w3_ref, *sems[5:8], _whole),
                   _WeightGather(lambda p, cols: cw_ref.at[me], cw_ref, *sems[8:11], _whole))

        @pl.when(hp == 0)
        def _():
            pieces.start(later)
            for g in gathers:
                g.start()

        @pl.when(hp == FORWARD_STEP)
        def _():
            for g in gathers:
                g.forward()

        for h in range(hpr):
            @pl.when(hp == h)
            def _(h=h):
                if h > 0:
                    pieces.wait_recv(range(3 * h, 3 * h + 3))
                fetch = []
                for seg in range(3):
                    p, lo = _qkv_piece(h, seg)
                    fetch.append(pltpu.make_async_copy(w_ref.at[p, :, lo:lo + 128], w_tile.at[:, seg * 128:(seg + 1) * 128],
                                                       tile_sems.at[seg]))
                    fetch[-1].start()
                for cp in fetch:
                    cp.wait()

        def project(i, carry):
            rows = pl.ds(pl.multiple_of(i * PROJ_ROWS, PROJ_ROWS), PROJ_ROWS)
            qkv = _dot(u_ref[rows, :], w_tile[...])
            per = PROJ_ROWS // LAYOUT_MOD
            for seg, ref in enumerate((q_ref, k_ref, v_ref)):
                staged[seg] = qkv[:, seg * 128:(seg + 1) * 128]
                for g in range(LAYOUT_MOD):
                    dst = pl.ds(pl.multiple_of(g * (S // LAYOUT_MOD) + i * per, per), per)
                    ref[dst, :] = staged.at[seg][pl.ds(g, per, stride=LAYOUT_MOD), :]
            return carry

        lax.fori_loop(0, S // PROJ_ROWS, project, 0)

        head_sum = _head_sum_matrix()
        ones_b = jnp.ones((2 * QB, QB), BF)
        m_s[...] = jnp.full(m_s.shape, NEG, F32)
        l_s[...] = jnp.zeros(l_s.shape, F32)
        acc[...] = jnp.zeros(acc.shape, F32)

        for d in DILATIONS:
            tri_le, dist, low = _fold_masks(d)
            low_b = low.astype(F32).astype(BF)
            high_b = 1.0 - low_b
            slope = [sl_ref[2 * hp + a] * float(d) for a in range(2)]
            bias = [slope[a] * dist for a in range(2)]

            def block(b, d=d, slope=slope, bias=bias, tri_le=tri_le, low=low, low_b=low_b, high_b=high_b):
                n, cur, prev = _block_rows(b, d, S)
                has_prev = n > 0
                valid = jnp.logical_or(tri_le, has_prev)
                q2 = (cur.get(q_ref) * 0.125).astype(BF)
                qs = jnp.concatenate([q2 * low_b, q2 * high_b], axis=0)
                vp = prev.get(v_ref)
                kp_b = prev.get(k_ref).astype(BF)
                kcat = jnp.concatenate([kp_b, cur.get(k_ref).astype(BF)], axis=0)
                vcat = jnp.concatenate([vp, cur.get(v_ref)], axis=0).astype(BF)
                s2 = _dot_nt(qs, kcat)
                e2 = _dot(_hi_lo(q2.astype(F32) * kp_b.astype(F32)), head_sum)
                p_rows, alpha_h, pe_h = [], [], []
                for a in range(2):
                    sp, sc = s2[a * QB:(a + 1) * QB, :QB], s2[a * QB:(a + 1) * QB, QB:]
                    comb = jnp.where(valid, jnp.where(tri_le, sc, sp) - bias[a], NEG)
                    e = jnp.where(has_prev, e2[:, a * QB:(a + 1) * QB] - slope[a] * float(QB), NEG)
                    m_old = cur.get(m_s.at[a])
                    m_new = jnp.maximum(jnp.maximum(m_old, jnp.max(comb, axis=-1, keepdims=True)), e)
                    cur.put(m_s.at[a], m_new)
                    p = jnp.exp(comb - m_new)
                    pe_h.append(jnp.exp(e - m_new))
                    alpha_h.append(jnp.exp(m_old - m_new))
                    p_rows.append(jnp.concatenate([jnp.where(tri_le, 0.0, p).astype(BF),
                                                   jnp.where(tri_le, p, 0.0).astype(BF)], axis=1))
                pv = _dot(jnp.concatenate(p_rows, axis=0), jnp.concatenate([vcat, ones_b], axis=1))
                for a in range(2):
                    cur.put(l_s.at[a], alpha_h[a] * cur.get(l_s.at[a]) + pv[a * QB:(a + 1) * QB, QB:] + pe_h[a])
                cur.put(acc, jnp.where(low, alpha_h[0], alpha_h[1]) * cur.get(acc)
                        + jnp.where(low, pv[:QB, :QB], pv[QB:, :QB]) + jnp.where(low, pe_h[0], pe_h[1]) * vp)

            def several(it, carry, block=block):
                for u in range(ATT_UNROLL):
                    block(it * ATT_UNROLL + u)
                return carry

            lax.fori_loop(0, n_blocks // ATT_UNROLL, several, 0)

        low = _fold_masks(LAYOUT_MOD)[2]

        def finish(i, carry):
            rows = pl.ds(pl.multiple_of(i * QB, QB), QB)
            l0, l1 = l_s[0, rows, :], l_s[1, rows, :]
            o_ref[_natural_rows(i, S), :] = acc[rows, :] / jnp.where(low, l0, l1)
            lse_ref[0, rows, :] = m_s[0, rows, :] + jnp.log(l0)
            lse_ref[1, rows, :] = m_s[1, rows, :] + jnp.log(l1)
            return carry

        lax.fori_loop(0, n_blocks, finish, 0)

        @pl.when(hp == hpr - 1)
        def _():
            pieces.wait_send(later)
            for g in gathers:
                g.finish()

    col = pl.BlockSpec((S, 128), lambda h: (0, h))
    act = jax.ShapeDtypeStruct((S, D), F32)
    gathered = (w_all, w3_all, cw_all)
    return pl.pallas_call(
        body, name="attn_fwd", grid=(hpr,),
        in_specs=[SMEM_SPEC, VMEM_SPEC, ANY_SPEC, ANY_SPEC, ANY_SPEC],
        out_specs=(col, pl.BlockSpec((2, S, 128), lambda h: (0, 0, h)), col, col, col, ANY_SPEC, ANY_SPEC, ANY_SPEC),
        out_shape=(act, jax.ShapeDtypeStruct((2, S, D), F32), act, act, act,
                   *[jax.ShapeDtypeStruct(t.shape, t.dtype) for t in gathered]),
        scratch_shapes=([pltpu.VMEM((S, 128), F32), pltpu.VMEM((2, S, 128), F32), pltpu.VMEM((2, S, 128), F32),
                         pltpu.VMEM((D, 3 * 128), BF), pltpu.VMEM((3, PROJ_ROWS, 128), F32),
                         pltpu.SemaphoreType.DMA((3,))]
                        + _piece_sems(3 * hpr) + WEIGHT_GATHER_SEMS * 3),
        input_output_aliases={2: 5, 3: 6, 4: 7},
        compiler_params=_params(1),
    )(slopes, u, *gathered)


def _set_rows(shape, rows):
    idx = lax.broadcasted_iota(jnp.int32, shape, 0)
    out = jnp.zeros(shape, F32)
    for r, val in rows.items():
        out = out + jnp.where(idx == r, val, 0.0)
    return out


def _mid(yc_in, pa_mid, o, x2, target, b_merge, final_g, w3):
    S = x2.shape[0]
    tm = ROW_TILE
    nsteps = S // tm
    tile = pl.BlockSpec((tm, D), lambda i: (i, 0))

    def body(yc_ref, za_ref, gcp_ref, gap_ref, o_ref, x_ref, t_ref, b_ref, fg_ref, w_ref,
             dh_ref, dmid_ref, do_ref, dyc_ref, gw_ref, small_ref, acc, stage):
        i = pl.program_id(0)

        @pl.when(i == 0)
        def _():
            acc[...] = jnp.zeros_like(acc)
            small_ref[...] = jnp.zeros_like(small_ref)

        wc, wa, wo = w_ref[0], w_ref[1], w_ref[2]
        z = za_ref[...].astype(F32)
        sg = _sigmoid(z)
        ov = o_ref[...]
        yc_in_b, ya_in_b = yc_ref[...], (z * sg * ov).astype(BF)
        yc = _dot(yc_in_b, wc)
        ya = _dot(ya_in_b, wa)
        b = b_ref[...]
        gc = _sigmoid(gcp_ref[...].astype(F32) + b[:, :D])
        ga = _sigmoid(gap_ref[...].astype(F32) + b[:, D:])
        merged = gc * yc + ga * ya
        merged_b = merged.astype(BF)
        h = x_ref[...] + _dot(merged_b, wo)
        r2 = lax.rsqrt(jnp.mean(h * h, axis=-1, keepdims=True) + EPS)
        n = h * r2
        fg = fg_ref[...]
        err = n * fg - t_ref[...]
        loss = 0.5 * jnp.sum(jnp.sum(err * err, axis=-1, keepdims=True) / D, axis=0, keepdims=True)
        dy = err / D
        g_fg = jnp.sum(dy * n, axis=0, keepdims=True)
        dn = dy * fg
        dh = r2 * (dn - n * jnp.mean(dn * n, axis=-1, keepdims=True))
        dh_ref[...] = dh
        dh_b = dh.astype(BF)
        dmerged = _dot_nt(dh_b, wo)
        acc[2] += _dot(merged.T.astype(BF), dh_b)
        dyc = (dmerged * gc).astype(BF)
        dya = (dmerged * ga).astype(BF)
        dgcp = dmerged * yc * gc * (1.0 - gc)
        dgap = dmerged * ya * ga * (1.0 - ga)
        dmid_ref[1] = dgcp.astype(BF)
        dmid_ref[2] = dgap.astype(BF)
        acc[0] += _dot(yc_in_b.astype(F32).T.astype(BF), dyc)
        acc[1] += _dot(ya_in_b.astype(F32).T.astype(BF), dya)
        dyc_ref[...] = _dot_nt(dyc, wc).astype(BF)
        dya_in = _dot_nt(dya, wa)
        do_ref[...] = dya_in * (z * sg)
        dmid_ref[0] = (dya_in * ov * (sg * (1.0 + z * (1.0 - sg)))).astype(BF)
        small_ref[...] += _set_rows((8, D), {
            1: jnp.sum(dgcp, axis=0, keepdims=True), 2: jnp.sum(dgap, axis=0, keepdims=True),
            3: g_fg, 7: jnp.broadcast_to(loss, (1, D))})

        @pl.when(i == nsteps - 1)
        def _():
            for p in range(N_DEV):
                for a in range(3):
                    stage[...] = acc[a, p * ROW_SHARD:(p + 1) * ROW_SHARD, :].astype(BF)
                    pltpu.sync_copy(stage, gw_ref.at[p, a])

    return pl.pallas_call(
        body, name="mid", grid=(nsteps,),
        in_specs=[tile, pl.BlockSpec((tm, D), lambda i: (i, 0)), pl.BlockSpec((tm, D), lambda i: (i, 1)),
                  pl.BlockSpec((tm, D), lambda i: (i, 2)), tile, tile, tile,
                  pl.BlockSpec((1, 2 * D), lambda i: (0, 0)), pl.BlockSpec((1, D), lambda i: (0, 0)), VMEM_SPEC],
        out_specs=(tile, pl.BlockSpec((3, tm, D), lambda i: (0, i, 0)), tile, tile,
                   ANY_SPEC, pl.BlockSpec((8, D), lambda i: (0, 0))),
        out_shape=(jax.ShapeDtypeStruct((S, D), F32), jax.ShapeDtypeStruct((3, S, D), BF),
                   jax.ShapeDtypeStruct((S, D), F32), jax.ShapeDtypeStruct((S, D), BF),
                   jax.ShapeDtypeStruct((N_DEV, 3, ROW_SHARD, D), BF), jax.ShapeDtypeStruct((8, D), F32)),
        scratch_shapes=[pltpu.VMEM((3, D, D), F32), pltpu.VMEM((ROW_SHARD, D), BF)],
        compiler_params=_params(1),
    )(yc_in, pa_mid, pa_mid, pa_mid, o, x2, target, b_merge, final_g, w3)


def _conv_bwd(dyc_in, pa, cw8):
    S = pa.shape[0]
    tm, tc = CONV_TM, CONV_TC
    nct = D // tc
    nrt = S // tm
    last_halo = S // HALO - 1

    def seg(s):
        return pl.BlockSpec((tm, tc), lambda j, i, s=s: (i, s * nct + j))

    def halo_before(s):
        return pl.BlockSpec((HALO, tc), lambda j, i, s=s: (jnp.maximum(i * (tm // HALO) - 1, 0), s * nct + j))

    def halo_after(s):
        return pl.BlockSpec((HALO, tc), lambda j, i, s=s: (jnp.minimum((i + 1) * (tm // HALO), last_halo), s * nct + j))

    def body(dy, xc, bg, cg, zc, xch, cgh, dyn, bgn, zcn, cw, dout, gcw):
        i = pl.program_id(1)

        @pl.when(i == 0)
        def _():
            gcw[...] = jnp.zeros_like(gcw)

        xcv, cgv = xc[...].astype(F32), cg[...].astype(F32)
        a = cgv * xcv
        ah = jnp.where(i > 0, cgh[...].astype(F32) * xch[...].astype(F32), 0.0)
        row = lax.broadcasted_iota(jnp.int32, (tm, tc), 0)
        a1 = jnp.where(row == 0, ah[HALO - 1:HALO, :], pltpu.roll(a, 1, 0))
        a2 = jnp.where(row == 0, ah[HALO - 2:HALO - 1, :],
                       jnp.where(row == 1, ah[HALO - 1:HALO, :], pltpu.roll(a, 2, 0)))
        w = cw[...]
        conv = w[0:1, :] * a2 + w[1:2, :] * a1 + w[2:3, :] * a
        z = zc[...].astype(F32)
        sg = _sigmoid(z)
        silu = z * sg
        bgv = bg[...].astype(F32)
        dyv = dy[...].astype(F32)
        dout[3] = (dyv * bgv * conv * (sg * (1.0 + z * (1.0 - sg)))).astype(BF)
        dout[1] = (dyv * silu * conv).astype(BF)
        dc = dyv * silu * bgv
        zn = zcn[...].astype(F32)
        dcn = dyn[...].astype(F32) * (zn * _sigmoid(zn)) * bgn[...].astype(F32)
        dcn = jnp.where(i < nrt - 1, dcn, 0.0)
        dc1 = jnp.where(row == tm - 1, dcn[0:1, :], pltpu.roll(dc, tm - 1, 0))
        dc2 = jnp.where(row == tm - 1, dcn[1:2, :],
                        jnp.where(row == tm - 2, dcn[0:1, :], pltpu.roll(dc, tm - 2, 0)))
        da = w[2:3, :] * dc + w[1:2, :] * dc1 + w[0:1, :] * dc2
        dout[2] = (da * xcv).astype(BF)
        dout[0] = (da * cgv).astype(BF)
        gcw[...] += _set_rows((8, tc), {
            4: jnp.sum(dc * a2, axis=0, keepdims=True), 5: jnp.sum(dc * a1, axis=0, keepdims=True),
            6: jnp.sum(dc * a, axis=0, keepdims=True)})

    return pl.pallas_call(
        body, name="conv_bwd", grid=(nct, nrt),
        in_specs=[pl.BlockSpec((tm, tc), lambda j, i: (i, j)), seg(0), seg(1), seg(2), seg(3),
                  halo_before(0), halo_before(2),
                  pl.BlockSpec((HALO, tc), lambda j, i: (jnp.minimum((i + 1) * (tm // HALO), last_halo), j)),
                  halo_after(1), halo_after(3), pl.BlockSpec((8, tc), lambda j, i: (0, j))],
        out_specs=(pl.BlockSpec((4, tm, tc), lambda j, i: (0, i, j)), pl.BlockSpec((8, tc), lambda j, i: (0, j))),
        out_shape=(jax.ShapeDtypeStruct((4, S, D), BF), jax.ShapeDtypeStruct((8, D), F32)),
        compiler_params=_params(2),
    )(dyc_in, pa, pa, pa, pa, pa, pa, dyc_in, pa, pa, cw8)


def _attn_bwd(q, k, v, slopes, do, o, lse, g_in, g_3):
    S = q.shape[0]
    hpr = HEAD_PAIRS
    n_blocks = S // QB

    def body(sl_ref, q_ref, k_ref, v_ref, do_ref, o_ref, lse_ref, gin_ref, g3_ref, out_ref, rin_ref, r3_ref,
             dq_s, dk_s, dv_s, do_s, dd_s, *sems):
        hp = pl.program_id(0)
        exchanges = (_GradExchange(gin_ref, rin_ref, *sems[:3], _shard_cols((0, SEG0_ATTN * D), (SEG0_MID * D, IN_COLS))),
                     _GradExchange(g3_ref, r3_ref, *sems[3:], _whole))

        @pl.when(hp == 0)
        def _():
            for ex in exchanges:
                ex.start()

        head_sum = _head_sum_matrix()
        dq_s[...] = jnp.zeros(dq_s.shape, F32)
        dk_s[...] = jnp.zeros(dk_s.shape, F32)
        dv_s[...] = jnp.zeros(dv_s.shape, F32)

        def row_dots(i, carry):
            rows = pl.ds(pl.multiple_of(i * QB, QB), QB)
            natural = _natural_rows(i, S)
            do_c = do_ref[natural, :]
            do_s[rows, :] = do_c
            dd = _dot(_hi_lo(do_c * o_ref[natural, :]), head_sum)
            dd_s[0, rows, :] = dd[:, :QB]
            dd_s[1, rows, :] = dd[:, QB:]
            return carry

        lax.fori_loop(0, n_blocks, row_dots, 0)

        for d in DILATIONS:
            tri_le, dist, low = _fold_masks(d)
            low_b = low.astype(F32).astype(BF)
            high_b = 1.0 - low_b
            slope = [sl_ref[2 * hp + a] * float(d) for a in range(2)]
            bias = [slope[a] * dist for a in range(2)]

            def block(b, d=d, slope=slope, bias=bias, tri_le=tri_le, low=low, low_b=low_b, high_b=high_b):
                n, cur, prev = _block_rows(b, d, S)
                has_prev = n > 0
                valid = jnp.logical_or(tri_le, has_prev)
                q2f = cur.get(q_ref) * 0.125
                q2 = q2f.astype(BF)
                qs = jnp.concatenate([q2 * low_b, q2 * high_b], axis=0)
                kp, vp = prev.get(k_ref), prev.get(v_ref)
                kp_b, vp_b = kp.astype(BF), vp.astype(BF)
                kcat = jnp.concatenate([kp_b, cur.get(k_ref).astype(BF)], axis=0)
                vcat = jnp.concatenate([vp_b, cur.get(v_ref).astype(BF)], axis=0)
                do2f = cur.get(do_s)
                do2 = do2f.astype(BF)
                dos = jnp.concatenate([do2 * low_b, do2 * high_b], axis=0)
                s2 = _dot_nt(qs, kcat)
                dp2 = _dot_nt(dos, vcat)
                diag2 = _dot(jnp.concatenate([_hi_lo(q2.astype(F32) * kp_b.astype(F32)),
                                              _hi_lo(do2.astype(F32) * vp_b.astype(F32))], axis=0), head_sum)
                p_rows, ds_rows, pe_h, dse_h = [], [], [], []
                for a in range(2):
                    hs = slice(a * QB, (a + 1) * QB)
                    sp, sc = s2[hs, :QB], s2[hs, QB:]
                    dpp, dpc = dp2[hs, :QB], dp2[hs, QB:]
                    lse_a, dd_a = cur.get(lse_ref.at[a]), cur.get(dd_s.at[a])
                    comb = jnp.where(tri_le, sc, sp) - bias[a]
                    e = diag2[:QB, hs] - slope[a] * float(QB)
                    p = jnp.where(valid, jnp.exp(comb - lse_a), 0.0)
                    pe = jnp.where(has_prev, jnp.exp(e - lse_a), 0.0)
                    ds = p * (jnp.where(tri_le, dpc, dpp) - dd_a)
                    dse_h.append(pe * (diag2[QB:, hs] - dd_a))
                    pe_h.append(pe)
                    p_rows.append(jnp.concatenate([jnp.where(tri_le, 0.0, p).astype(BF),
                                                   jnp.where(tri_le, p, 0.0).astype(BF)], axis=1))
                    ds_rows.append(jnp.concatenate([jnp.where(tri_le, 0.0, ds).astype(BF),
                                                    jnp.where(tri_le, ds, 0.0).astype(BF)], axis=1))
                pst = jnp.concatenate(p_rows, axis=0)
                dst = jnp.concatenate(ds_rows, axis=0)
                pe2 = jnp.where(low, pe_h[0], pe_h[1])
                dse2 = jnp.where(low, dse_h[0], dse_h[1])
                dq = _dot(dst, kcat)
                cur.add(dq_s, (jnp.where(low, dq[:QB], dq[QB:]) + dse2 * kp) * 0.125)
                dk = _dot_tn(dst, qs)
                dv = _dot_tn(pst, dos)
                prev.add(dk_s, dk[:QB] + dse2 * q2f)
                cur.add(dk_s, dk[QB:])
                prev.add(dv_s, dv[:QB] + pe2 * do2f)
                cur.add(dv_s, dv[QB:])

            def several(it, carry, block=block):
                for u in range(ATT_UNROLL):
                    block(it * ATT_UNROLL + u)
                return carry

            lax.fori_loop(0, n_blocks // ATT_UNROLL, several, 0)

        def finish(i, carry):
            rows = pl.ds(pl.multiple_of(i * QB, QB), QB)
            natural = _natural_rows(i, S)
            for t, ref in enumerate((dq_s, dk_s, dv_s)):
                out_ref.at[t][natural, :] = ref[rows, :]
            return carry

        lax.fori_loop(0, n_blocks, finish, 0)

        @pl.when(hp == hpr - 1)
        def _():
            for ex in exchanges:
                ex.finish()

    col = pl.BlockSpec((S, 128), lambda h: (0, h))
    return pl.pallas_call(
        body, name="attn_bwd", grid=(hpr,),
        in_specs=[SMEM_SPEC, col, col, col, col, col, pl.BlockSpec((2, S, 128), lambda h: (0, 0, h)),
                  ANY_SPEC, ANY_SPEC],
        out_specs=(pl.BlockSpec((3, S, 128), lambda h: (0, 0, h)), ANY_SPEC, ANY_SPEC),
        out_shape=(jax.ShapeDtypeStruct((3, S, D), F32), jax.ShapeDtypeStruct(g_in.shape, BF),
                   jax.ShapeDtypeStruct(g_3.shape, BF)),
        scratch_shapes=([pltpu.VMEM((S, 128), F32)] * 4 + [pltpu.VMEM((2, S, 128), F32)]
                        + GRAD_EXCHANGE_SEMS + GRAD_EXCHANGE_SEMS),
        compiler_params=_params(1),
    )(slopes, q, k, v, do, o, lse, g_in, g_3)


WG_TN = 256
SEG0_CONV, SEG0_ATTN, SEG0_MID = 0, 4, 7


def _wgrad_in(ut, d_group, seg0, g_in, name):
    S = ut.shape[1]
    tn = WG_TN
    per_seg = D // tn
    per_shard = W_IN_SHARD // tn
    n_tiles = d_group.shape[0] * per_seg
    tile0 = seg0 * per_seg

    def body(ut_ref, d_ref, *rest):
        rest[-1][0] = _dot(ut_ref[...], d_ref[0].astype(BF)).astype(BF)

    operands, in_specs, aliases = [ut, d_group], [VMEM_SPEC, pl.BlockSpec((1, S, tn), lambda t: (t // per_seg, 0, t % per_seg))], {}
    if g_in is not None:
        operands.append(g_in)
        in_specs.append(ANY_SPEC)
        aliases = {2: 0}
    return pl.pallas_call(
        body, name=name, grid=(n_tiles,), in_specs=in_specs,
        out_specs=pl.BlockSpec((1, D, tn), lambda t: ((tile0 + t) // per_shard, 0, (tile0 + t) % per_shard)),
        out_shape=jax.ShapeDtypeStruct((N_DEV, D, W_IN_SHARD), BF),
        input_output_aliases=aliases,
        compiler_params=_params(1),
    )(*operands)


def _dgrad_norm_bwd(d_conv, d_attn, d_mid, w_all, x2, dh, norm_g):
    S = x2.shape[0]
    tm = ROW_TILE
    nsteps = S // tm
    tile = pl.BlockSpec((tm, D), lambda i: (i, 0))
    pieces = _proj_pieces()

    def body(a_ref, b_ref, c_ref, w_ref, x_ref, dh_ref, g_ref, gx_ref, small_ref):
        i = pl.program_id(0)

        @pl.when(i == 0)
        def _():
            small_ref[...] = jnp.zeros_like(small_ref)

        groups = (a_ref, b_ref, c_ref)
        du = jnp.zeros((tm, D), F32)
        for s, sc, p, pc, width in pieces:
            g = 0 if s < 4 else (1 if s < 7 else 2)
            local = s - (0, 4, 7)[g]
            du = du + _dot_nt(groups[g][local, :, sc:sc + width].astype(BF), w_ref[p, :, pc:pc + width])
        xv = x_ref[...]
        r = lax.rsqrt(jnp.mean(xv * xv, axis=-1, keepdims=True) + EPS)
        n = xv * r
        dn = du * g_ref[...]
        gx_ref[...] = dh_ref[...] + r * (dn - n * jnp.mean(dn * n, axis=-1, keepdims=True))
        small_ref[...] += _set_rows((8, D), {0: jnp.sum(du * n, axis=0, keepdims=True)})

    return pl.pallas_call(
        body, name="dgrad_norm_bwd", grid=(nsteps,),
        in_specs=[pl.BlockSpec((4, tm, D), lambda i: (0, i, 0)), pl.BlockSpec((3, tm, D), lambda i: (0, i, 0)),
                  pl.BlockSpec((3, tm, D), lambda i: (0, i, 0)), VMEM_SPEC, tile, tile,
                  pl.BlockSpec((1, D), lambda i: (0, 0))],
        out_specs=(tile, pl.BlockSpec((8, D), lambda i: (0, 0))),
        out_shape=(jax.ShapeDtypeStruct((S, D), F32), jax.ShapeDtypeStruct((8, D), F32)),
        compiler_params=_params(1),
    )(d_conv, d_attn, d_mid, w_all, x2, dh, norm_g)


HBM_SPEC = pl.BlockSpec(memory_space=pltpu.HBM)
SEM_SPEC = pl.BlockSpec(memory_space=pltpu.SEMAPHORE)
ATTN_COLS = _shard_cols((SEG0_ATTN * D, SEG0_MID * D))


def _attn_cols_exchange_start(g_in, r_in):
    def body(g_ref, r_ref, send_sems, recv_sems, g_thru, r_thru, token):
        _GradExchange(g_ref, r_ref, send_sems, recv_sems, None, ATTN_COLS).start()
        token[...] = jnp.zeros_like(token)

    hbm = pltpu.with_memory_space_constraint
    return pl.pallas_call(
        body, name="attn_cols_exchange_start",
        out_shape=(pltpu.SemaphoreType.DMA((N_DEV,)), pltpu.SemaphoreType.DMA((N_DEV,)),
                   pltpu.HBM(g_in.shape, g_in.dtype), pltpu.HBM(r_in.shape, r_in.dtype),
                   jax.ShapeDtypeStruct((8, 128), F32)),
        in_specs=(HBM_SPEC, HBM_SPEC), out_specs=(SEM_SPEC, SEM_SPEC, HBM_SPEC, HBM_SPEC, VMEM_SPEC),
        input_output_aliases={0: 2, 1: 3},
        compiler_params=pltpu.CompilerParams(has_side_effects=pltpu.SideEffectType.DATAFLOW_SIDE_EFFECTING),
    )(hbm(g_in, pltpu.HBM), hbm(r_in, pltpu.HBM))


def _attn_cols_exchange_wait(send_sems, recv_sems, g_thru, r_thru, after):
    def body(g_ref, r_ref, send_sems, recv_sems, after_ref, g_dead, r_out):
        _GradExchange(g_ref, r_ref, send_sems, recv_sems, None, ATTN_COLS).finish()

    return pl.pallas_call(
        body, name="attn_cols_exchange_wait",
        out_shape=(pltpu.HBM(g_thru.shape, g_thru.dtype), pltpu.HBM(r_thru.shape, r_thru.dtype)),
        in_specs=(HBM_SPEC, HBM_SPEC, SEM_SPEC, SEM_SPEC, ANY_SPEC), out_specs=(HBM_SPEC, HBM_SPEC),
        input_output_aliases={0: 0, 1: 1},
        compiler_params=pltpu.CompilerParams(has_side_effects=pltpu.SideEffectType.DATAFLOW_SIDE_EFFECTING),
    )(g_thru, r_thru, send_sems, recv_sems, after)


def _adamw_math(w, g, m, v):
    m = ADAM_B1 * m + (1.0 - ADAM_B1) * g
    v = ADAM_B2 * v + (1.0 - ADAM_B2) * (g * g)
    m_hat = m / (1.0 - ADAM_B1 ** ADAM_STEP)
    v_hat = v / (1.0 - ADAM_B2 ** ADAM_STEP)
    delta = -ADAM_LR * (m_hat / (jnp.sqrt(v_hat) + ADAM_EPS) + ADAM_WD * w)
    return delta, m, v


def _sum_adamw(parts, w, m, v, tm, name):
    R, C = w.shape
    tile = pl.BlockSpec((tm, C), lambda i: (i, 0))

    def body(p_ref, w_ref, m_ref, v_ref, g_out, d_out, m_out, v_out):
        g = p_ref[0].astype(F32)
        for s in range(1, N_DEV):
            g = g + p_ref[s].astype(F32)
        g_out[...] = g
        d_out[...], m_out[...], v_out[...] = _adamw_math(w_ref[...], g, m_ref[...], v_ref[...])

    shape = jax.ShapeDtypeStruct((R, C), F32)
    return pl.pallas_call(
        body, name=name, grid=(R // tm,),
        in_specs=[pl.BlockSpec((N_DEV, tm, C), lambda i: (0, i, 0)), tile, tile, tile],
        out_specs=(tile, tile, tile, tile), out_shape=(shape, shape, shape, shape),
        compiler_params=_params(1),
    )(parts, w, m, v)


def _adamw(g, w, m, v, name):
    def body(g_ref, w_ref, m_ref, v_ref, d_out, m_out, v_out):
        d_out[...], m_out[...], v_out[...] = _adamw_math(w_ref[...], g_ref[...], m_ref[...], v_ref[...])

    shape = jax.ShapeDtypeStruct(w.shape, F32)
    return pl.pallas_call(
        body, name=name, in_specs=[VMEM_SPEC] * 4, out_specs=(VMEM_SPEC,) * 3, out_shape=(shape, shape, shape),
    )(g, w, m, v)


def _alibi_slopes():
    return jnp.exp2(-8.0 * jnp.arange(1, N_HEADS + 1, dtype=F32) / N_HEADS)


def _local_step(x2, target, norm_g, b_merge, final_g, w_in, w3_shard, cw_shard):
    slopes = _alibi_slopes()
    u, ut, w_all, w3_all, cw_all = _norm_gather_first_weights(x2, norm_g, w_in, w3_shard, cw_shard)
    o, lse, q, k, v, w_all, w3_all, cw_all = _attn_fwd(u, slopes, w_all, w3_all, cw_all)
    w3 = jnp.transpose(w3_all, (1, 0, 2, 3)).reshape(3, D, D)
    cw8 = jnp.transpose(cw_all, (1, 0, 2)).reshape(8, D)
    pa = _proj_cols(u, w_all, SEG0_CONV, 4, BF, "proj_conv")
    yc_in = _conv_fwd(pa, cw8)
    pa_mid = _proj_cols(u, w_all, SEG0_MID, 3, BF, "proj_mid")
    dh, d_mid, do, dyc_in, g_3, small_mid = _mid(yc_in, pa_mid, o, x2, target, b_merge, final_g, w3)
    g_in = _wgrad_in(ut, d_mid, SEG0_MID, None, "wgrad_in_mid")
    d_conv, small_conv = _conv_bwd(dyc_in, pa, cw8)
    g_in = _wgrad_in(ut, d_conv, SEG0_CONV, g_in, "wgrad_in_conv")
    d_attn, r_in, r_3 = _attn_bwd(q, k, v, slopes, do, o, lse, g_in, g_3)
    g_in = _wgrad_in(ut, d_attn, SEG0_ATTN, g_in, "wgrad_in_attn")
    *in_flight, token = _attn_cols_exchange_start(g_in, r_in)
    grad_x, small_norm = _dgrad_norm_bwd(d_conv, d_attn, d_mid, w_all, x2, dh, norm_g + token[0:1, 0:1])
    return grad_x, in_flight, r_3, small_mid, small_conv, small_norm


def kernel(x, norm_g, w_in, b_merge, conv_w, w_out_conv, w_out_attn, w_o, final_g, loss_target, m_norm_g, m_w_in, m_b_merge, m_conv_w, m_w_out_conv, m_w_out_attn, m_w_o, m_final_g, v_norm_g, v_w_in, v_b_merge, v_conv_w, v_w_out_conv, v_w_out_attn, v_w_o, v_final_g):
    me = 4 * lax.axis_index("x") + 2 * lax.axis_index("y") + lax.axis_index("c")
    stack3 = lambda a, b, c: jnp.concatenate([a, b, c], axis=0)
    pad8 = lambda a: jnp.pad(a, ((0, 8 - a.shape[0]), (0, 0)))

    w3_shard = stack3(w_out_conv, w_out_attn, w_o)
    final_g2 = final_g.reshape(1, D)
    grad_x, in_flight, r_3, small_mid, small_conv, small_norm = _local_step(
        x[0], loss_target[0], norm_g, b_merge, final_g2, w_in[0], w3_shard, pad8(conv_w[0]))

    small = _allreduce_small(small_mid, small_conv, small_norm)
    g_in, r_in = _attn_cols_exchange_wait(*in_flight, small)
    own = lax.dynamic_index_in_dim(g_in, me, 0, keepdims=True)
    r_in = lax.dynamic_update_slice(r_in, own, (me, 0, 0))

    g_w_in, d_w_in, nm_w_in, nv_w_in = _sum_adamw(r_in, w_in[0], m_w_in[0], v_w_in[0], 128, "adamw_w_in")
    g_w3, d_w3, nm_w3, nv_w3 = _sum_adamw(
        r_3.reshape(N_DEV, 3 * ROW_SHARD, D), w3_shard.reshape(3 * ROW_SHARD, D),
        stack3(m_w_out_conv, m_w_out_attn, m_w_o).reshape(3 * ROW_SHARD, D),
        stack3(v_w_out_conv, v_w_out_attn, v_w_o).reshape(3 * ROW_SHARD, D), ROW_SHARD, "adamw_w3")

    def pack(ng, bm, fg):
        return pad8(jnp.concatenate([ng, bm.reshape(2, D), fg.reshape(1, D)], axis=0))

    d_s, nm_s, nv_s = _adamw(small, pack(norm_g, b_merge, final_g), pack(m_norm_g, m_b_merge, m_final_g),
                             pack(v_norm_g, v_b_merge, v_final_g), "adamw_small")
    g_cw = lax.dynamic_slice(small, (4, me * ROW_SHARD), (3, ROW_SHARD))
    d_cw, nm_cw, nv_cw = _adamw(g_cw, conv_w[0], m_conv_w[0], v_conv_w[0], "adamw_conv_w")

    loss = small[7, 0]
    split3 = lambda t: tuple(t[a * ROW_SHARD:(a + 1) * ROW_SHARD][None] for a in range(3))
    unpack = lambda t: (t[0:1], t[1:3].reshape(1, 2 * D), t[3])

    def leaves(in_, small_, cw_, w3_):
        ng, bm, fg = unpack(small_)
        wc, wa, wo = split3(w3_)
        return (ng, in_[None], bm, cw_[None], wc, wa, wo, fg)

    return (loss, grad_x[None],
            *leaves(g_w_in, small, g_cw, g_w3),
            *leaves(d_w_in, d_s, d_cw, d_w3),
            *leaves(nm_w_in, nm_s, nm_cw, nm_w3),
            *leaves(nv_w_in, nv_s, nv_cw, nv_w3))
```

```python
import functools

import jax
import jax.numpy as jnp
from jax import lax
from jax.experimental import pallas as pl
from jax.experimental.pallas import tpu as pltpu

D = 1024
N_HEADS = 16
HEAD_DIM = 64
N_SEG = 10
IN_COLS = N_SEG * D
N_DEV = 8
W_IN_SHARD = IN_COLS // N_DEV
ROW_SHARD = D // N_DEV
QB = 128
DILATIONS = (1, 4, 16)
EPS = 1e-6
NEG = -1e30
BF = jnp.bfloat16
F32 = jnp.float32
MESH = pl.DeviceIdType.MESH

ADAM_LR = 0.001
ADAM_B1 = 0.9
ADAM_B2 = 0.999
ADAM_EPS = 1e-08
ADAM_WD = 0.01
ADAM_STEP = 10

V7X_VMEM_BYTES = 64 * 1024 * 1024
VMEM_LIMIT = V7X_VMEM_BYTES - 8 * 1024 * 1024
ROW_TILE = 256

VMEM_SPEC = pl.BlockSpec(memory_space=pltpu.VMEM)
ANY_SPEC = pl.BlockSpec(memory_space=pl.ANY)
SMEM_SPEC = pl.BlockSpec(memory_space=pltpu.SMEM)


def _params(n_grid_axes, vmem=VMEM_LIMIT):
    return pltpu.CompilerParams(dimension_semantics=("arbitrary",) * n_grid_axes, vmem_limit_bytes=vmem)


def _dot(a, b):
    return jnp.dot(a, b, preferred_element_type=F32)


def _dot_nt(a, b):
    return lax.dot_general(a, b, (((1,), (1,)), ((), ())), preferred_element_type=F32)


def _dot_tn(a, b):
    return lax.dot_general(a, b, (((0,), (0,)), ((), ())), preferred_element_type=F32)


def _sigmoid(z):
    return 1.0 / (1.0 + jnp.exp(-z))


def _my_place():
    x, y, c = lax.axis_index("x"), lax.axis_index("y"), lax.axis_index("c")
    return x, y, c, 4 * x + 2 * y + c


def _peers(x, y, c):
    out = []
    for k in range(1, N_DEV):
        px = 1 - x if k & 4 else x
        py = 1 - y if k & 2 else y
        pc = 1 - c if k & 1 else c
        out.append(((px, py, pc), 4 * px + 2 * py + pc))
    return out


def _device(p):
    return (p >> 2, (p >> 1) & 1, p & 1)


def _shard_cols(*ranges):
    def cols(p):
        found = None
        for lo, hi in ranges:
            a, b = max(lo, p * W_IN_SHARD), min(hi, (p + 1) * W_IN_SHARD)
            if a < b:
                assert found is None
                found = (a - p * W_IN_SHARD, b - p * W_IN_SHARD)
        return found

    return cols


def _whole(p):
    return ()


def _block(ref, idx, cols):
    return ref.at[idx] if cols == () else ref.at[idx, :, cols[0]:cols[1]]


class _WeightGather:
    def __init__(self, src, dst, send_sems, forward_sems, recv_sems, cols):
        self.src, self.dst, self.cols = src, dst, cols
        self.send_sems, self.forward_sems, self.recv_sems = send_sems, forward_sems, recv_sems
        self.me = _my_place()[3]

    def _copy(self, p, target, passing_on=False):
        cols = self.cols(p)
        return pltpu.make_async_remote_copy(
            src_ref=_block(self.dst, p, cols) if passing_on else self.src(p, cols), dst_ref=_block(self.dst, p, cols),
            send_sem=self.forward_sems.at[p] if passing_on else self.send_sems.at[target],
            recv_sem=self.recv_sems.at[p], device_id=_device(target), device_id_type=MESH)

    def _as_each_device(self, own, relayed, other):
        for m in range(N_DEV):
            def branch(m=m):
                for p in range(N_DEV):
                    if self.cols(p) is None:
                        continue
                    if p == m:
                        for t in [m ^ 1] + [q for q in range(N_DEV) if q >> 1 != m >> 1 and q & 1 == m & 1]:
                            own(self._copy(m, t))
                    elif p >> 1 != m >> 1 and p & 1 == m & 1:
                        relayed(p, m ^ 1)
                    else:
                        other(p)

            pl.when(self.me == m)(branch)

    def start(self):
        self._as_each_device(lambda cp: cp.start(), lambda p, t: None, lambda p: None)

    def forward(self):
        def pass_on(p, t):
            self._copy(p, p).wait_recv()
            self._copy(p, t, passing_on=True).start()

        self._as_each_device(lambda cp: None, pass_on, lambda p: None)

    def finish(self):
        self._as_each_device(lambda cp: cp.wait_send(), lambda p, t: self._copy(p, t, passing_on=True).wait_send(),
                             lambda p: self._copy(p, p).wait_recv())


WEIGHT_GATHER_SEMS = [pltpu.SemaphoreType.DMA((N_DEV,))] * 3
FORWARD_STEP = 6
REST_COLS = _shard_cols((0, 4 * D), (7 * D, IN_COLS))
HEAD_PAIRS = D // 128


def _qkv_piece(h, seg):
    col = (4 + seg) * D + 128 * h
    return col // W_IN_SHARD, col % W_IN_SHARD


class _PieceGather:
    def __init__(self, src, dst, send_sems, recv_sems):
        self.src, self.dst, self.send_sems, self.recv_sems = src, dst, send_sems, recv_sems
        self.me = _my_place()[3]

    def _copy(self, i, target):
        p, lo = _qkv_piece(i // 3, i % 3)
        return pltpu.make_async_remote_copy(
            src_ref=self.src(p, lo, lo + 128), dst_ref=self.dst.at[p, :, lo:lo + 128], send_sem=self.send_sems.at[i, target],
            recv_sem=self.recv_sems.at[i], device_id=_device(target), device_id_type=MESH)

    def _owner(self, i, act):
        p = _qkv_piece(i // 3, i % 3)[0]

        def sender():
            for k in range(N_DEV - 1):
                act(self._copy(i, (p + 1 + (k + i) % (N_DEV - 1)) % N_DEV))

        pl.when(self.me == p)(sender)

    def start(self, pieces):
        for i in pieces:
            self._owner(i, lambda cp: cp.start())

    def wait_send(self, pieces):
        for i in pieces:
            self._owner(i, lambda cp: cp.wait_send())

    def wait_recv(self, pieces):
        for i in pieces:
            p = _qkv_piece(i // 3, i % 3)[0]
            pl.when(self.me != p)(lambda i=i, p=p: self._copy(i, p).wait_recv())


def _piece_sems(n):
    return [pltpu.SemaphoreType.DMA((n, N_DEV)), pltpu.SemaphoreType.DMA((n,))]


def _norm_gather_first_weights(x2, norm_g, w_in, w3, cw):
    S = x2.shape[0]
    tm = ROW_TILE
    nsteps = S // tm

    def body(x_ref, g_ref, w_in_ref, w3_ref, cw_ref, u_ref, ut_ref, o_in, o_3, o_cw, in_bf, w3_bf, local_sems, *sems):
        i = pl.program_id(0)
        me = _my_place()[3]
        gather = _PieceGather(lambda p, lo, hi: in_bf.at[:, lo:hi], o_in, *sems)
        local = [pltpu.make_async_copy(src, dst.at[me], local_sems.at[a])
                 for a, (src, dst) in enumerate(((in_bf, o_in), (w3_bf, o_3), (cw_ref, o_cw)))]

        @pl.when(i == 0)
        def _():
            def cast_rows(r, carry):
                rows = pl.ds(pl.multiple_of(r * 128, 128), 128)
                in_bf[rows, :] = w_in_ref[rows, :].astype(BF)
                return carry

            lax.fori_loop(0, D // 128, cast_rows, 0)
            for a in range(3):
                w3_bf[a] = w3_ref[a].astype(BF)
            gather.start(range(3))
            for cp in local:
                cp.start()

        xv = x_ref[...]
        r = lax.rsqrt(jnp.mean(xv * xv, axis=-1, keepdims=True) + EPS)
        u = xv * r * g_ref[...]
        u_ref[...] = u.astype(BF)
        ut_ref[...] = u.T.astype(BF)

        @pl.when(i == nsteps - 1)
        def _():
            gather.wait_recv(range(3))
            gather.wait_send(range(3))
            for cp in local:
                cp.wait()

    return pl.pallas_call(
        body, name="norm_gather_first_weights", grid=(nsteps,),
        out_shape=(jax.ShapeDtypeStruct((S, D), BF), jax.ShapeDtypeStruct((D, S), BF),
                   jax.ShapeDtypeStruct((N_DEV, D, W_IN_SHARD), BF),
                   jax.ShapeDtypeStruct((N_DEV, 3, ROW_SHARD, D), BF),
                   jax.ShapeDtypeStruct((N_DEV, 8, 128), F32)),
        in_specs=[pl.BlockSpec((tm, D), lambda i: (i, 0)), pl.BlockSpec((1, D), lambda i: (0, 0)),
                  VMEM_SPEC, VMEM_SPEC, VMEM_SPEC],
        out_specs=(pl.BlockSpec((tm, D), lambda i: (i, 0)), pl.BlockSpec((D, tm), lambda i: (0, i)),
                   ANY_SPEC, ANY_SPEC, ANY_SPEC),
        scratch_shapes=[pltpu.VMEM((D, W_IN_SHARD), BF), pltpu.VMEM((3, ROW_SHARD, D), BF),
                        pltpu.SemaphoreType.DMA((3,))] + _piece_sems(3),
        compiler_params=_params(1),
    )(x2, norm_g, w_in, w3, cw)


class _GradExchange:
    def __init__(self, src, dst, send_sems, recv_sems, local_sem, cols):
        self.src, self.dst, self.cols = src, dst, cols
        self.send_sems, self.recv_sems, self.local_sem = send_sems, recv_sems, local_sem
        self.me = _my_place()[3]

    def _remote(self, p, source):
        return pltpu.make_async_remote_copy(
            src_ref=_block(self.src, p, self.cols(p)), dst_ref=_block(self.dst, source, self.cols(p)),
            send_sem=self.send_sems.at[p], recv_sem=self.recv_sems.at[source],
            device_id=_device(p), device_id_type=MESH)

    def _local(self, p):
        return pltpu.make_async_copy(_block(self.src, p, self.cols(p)), _block(self.dst, p, self.cols(p)),
                                     self.local_sem)

    def _as_each_device(self, send, local, receive):
        for m in range(N_DEV):
            def branch(m=m):
                for k in range(1, N_DEV):
                    p = (m + k) % N_DEV
                    if self.cols(p) is not None:
                        send(self._remote(p, m))
                if self.cols(m) is not None:
                    if self.local_sem is not None:
                        local(self._local(m))
                    for k in range(1, N_DEV):
                        receive(self._remote(m, (m + k) % N_DEV))

            pl.when(self.me == m)(branch)

    def start(self):
        self._as_each_device(lambda cp: cp.start(), lambda cp: cp.start(), lambda cp: None)

    def finish(self):
        self._as_each_device(lambda cp: cp.wait_send(), lambda cp: cp.wait(), lambda cp: cp.wait_recv())


GRAD_EXCHANGE_SEMS = [pltpu.SemaphoreType.DMA((N_DEV,)), pltpu.SemaphoreType.DMA((N_DEV,)), pltpu.SemaphoreType.DMA]


def _allreduce_small(p_mid, p_conv, p_norm):
    def body(a_ref, b_ref, c_ref, out_ref, mine, gathered, send_sems, recv_sems):
        x, y, c, me = _my_place()
        mine[...] = a_ref[...] + b_ref[...] + c_ref[...]
        gathered[me] = mine[...]
        remote = []
        for k, (peer, _) in enumerate(_peers(x, y, c)):
            cp = pltpu.make_async_remote_copy(
                src_ref=mine, dst_ref=gathered.at[me], send_sem=send_sems.at[k], recv_sem=recv_sems.at[k],
                device_id=peer, device_id_type=MESH)
            cp.start()
            remote.append(cp)
        for cp in remote:
            cp.wait()
        total = gathered[0]
        for s in range(1, N_DEV):
            total = total + gathered[s]
        out_ref[...] = total

    return pl.pallas_call(
        body, name="allreduce_small",
        out_shape=jax.ShapeDtypeStruct((8, D), F32),
        in_specs=[VMEM_SPEC, VMEM_SPEC, VMEM_SPEC], out_specs=VMEM_SPEC,
        scratch_shapes=[pltpu.VMEM((8, D), F32), pltpu.VMEM((N_DEV, 8, D), F32),
                        pltpu.SemaphoreType.DMA((N_DEV - 1,)), pltpu.SemaphoreType.DMA((N_DEV - 1,))],
    )(p_mid, p_conv, p_norm)


def _proj_pieces():
    cuts = sorted(set(range(0, IN_COLS + 1, D)) | set(range(0, IN_COLS + 1, W_IN_SHARD)))
    return [(lo // D, lo % D, lo // W_IN_SHARD, lo % W_IN_SHARD, hi - lo) for lo, hi in zip(cuts[:-1], cuts[1:])]


PROJ_TN = 256


def _proj_cols(u, w_all, seg0, n_seg, dtype, name):
    S = u.shape[0]
    tn = PROJ_TN
    per_shard = W_IN_SHARD // tn
    tile0 = seg0 * D // tn

    def body(u_ref, w_ref, out_ref):
        out_ref[...] = _dot(u_ref[...], w_ref[0]).astype(dtype)

    return pl.pallas_call(
        body, name=name, grid=(n_seg * D // tn,),
        in_specs=[VMEM_SPEC, pl.BlockSpec((1, D, tn), lambda t: ((tile0 + t) // per_shard, 0, (tile0 + t) % per_shard))],
        out_specs=pl.BlockSpec((S, tn), lambda t: (0, t)),
        out_shape=jax.ShapeDtypeStruct((S, n_seg * D), dtype),
        compiler_params=_params(1),
    )(u, w_all)


CONV_TM, CONV_TC = 256, 512
HALO = 16


def _conv_fwd(pa, cw8):
    S = pa.shape[0]
    tm, tc = CONV_TM, CONV_TC
    nct = D // tc

    def seg(s):
        return pl.BlockSpec((tm, tc), lambda i, j, s=s: (i, s * nct + j))

    def halo_before(s):
        return pl.BlockSpec((HALO, tc), lambda i, j, s=s: (jnp.maximum(i * (tm // HALO) - 1, 0), s * nct + j))

    def body(xc, bg, cg, zc, xch, cgh, cw, out):
        i = pl.program_id(0)
        a = cg[...].astype(F32) * xc[...].astype(F32)
        ah = cgh[...].astype(F32) * xch[...].astype(F32)
        ah = jnp.where(i > 0, ah, 0.0)
        row = lax.broadcasted_iota(jnp.int32, (tm, tc), 0)
        a1 = jnp.where(row == 0, ah[HALO - 1:HALO, :], pltpu.roll(a, 1, 0))
        a2 = jnp.where(row == 0, ah[HALO - 2:HALO - 1, :],
                       jnp.where(row == 1, ah[HALO - 1:HALO, :], pltpu.roll(a, 2, 0)))
        w = cw[...]
        conv = w[0:1, :] * a2 + w[1:2, :] * a1 + w[2:3, :] * a
        z = zc[...].astype(F32)
        out[...] = (z * _sigmoid(z) * bg[...].astype(F32) * conv).astype(BF)

    return pl.pallas_call(
        body, name="conv_fwd", grid=(S // tm, nct),
        in_specs=[seg(0), seg(1), seg(2), seg(3), halo_before(0), halo_before(2),
                  pl.BlockSpec((8, tc), lambda i, j: (0, j))],
        out_specs=pl.BlockSpec((tm, tc), lambda i, j: (i, j)),
        out_shape=jax.ShapeDtypeStruct((S, D), BF),
        compiler_params=_params(2),
    )(pa, pa, pa, pa, pa, pa, cw8)


ATT_UNROLL = 32


LAYOUT_MOD = 4
RUN = QB // LAYOUT_MOD


def _fold_masks(d):
    row = lax.broadcasted_iota(jnp.int32, (QB, QB), 0)
    lane = lax.broadcasted_iota(jnp.int32, (QB, QB), 1)
    if d == 1:
        qpos, kpos = LAYOUT_MOD * (row % RUN) + row // RUN, LAYOUT_MOD * (lane % RUN) + lane // RUN
    else:
        qpos, kpos = row, lane
    tri_le = kpos <= qpos
    dist = jnp.where(tri_le, qpos - kpos, qpos - kpos + QB).astype(F32)
    return tri_le, dist, lane < HEAD_DIM


class _Rows:
    def __init__(self, slices):
        self.slices = slices

    def get(self, ref):
        parts = [ref[sl, :] for sl in self.slices]
        return parts[0] if len(parts) == 1 else jnp.concatenate(parts, axis=0)

    def put(self, ref, val):
        size = QB // len(self.slices)
        for g, sl in enumerate(self.slices):
            ref[sl, :] = val if len(self.slices) == 1 else val[g * size:(g + 1) * size]

    def add(self, ref, val):
        self.put(ref, self.get(ref) + val)


def _block_rows(b, d, S):
    quarter = S // LAYOUT_MOD
    nb = S // (QB * d)
    r, n = b // nb, b % nb
    n_prev = jnp.maximum(n - 1, 0)
    if d == 1:
        runs = lambda m: _Rows([pl.ds(pl.multiple_of(g * quarter + RUN * m, RUN), RUN) for g in range(LAYOUT_MOD)])
        return n, runs(n), runs(n_prev)
    if d == LAYOUT_MOD:
        block = lambda m: _Rows([pl.ds(pl.multiple_of(r * quarter + QB * m, QB), QB)])
        return n, block(n), block(n_prev)
    step = d // LAYOUT_MOD
    first = (r % LAYOUT_MOD) * quarter + r // LAYOUT_MOD
    strided = lambda m: _Rows([pl.ds(first + QB * step * m, QB, stride=step)])
    return n, strided(n), strided(n_prev)


def _natural_rows(i, S):
    per = S // LAYOUT_MOD // QB
    return pl.ds(i // per + LAYOUT_MOD * QB * (i % per), QB, stride=LAYOUT_MOD)


def _head_sum_matrix():
    r = lax.broadcasted_iota(jnp.int32, (2 * QB, 2 * QB), 0)
    c = lax.broadcasted_iota(jnp.int32, (2 * QB, 2 * QB), 1)
    return (((r % QB) // HEAD_DIM) == (c // QB)).astype(F32).astype(BF)


def _hi_lo(t):
    hi = t.astype(BF)
    return jnp.concatenate([hi, (t - hi.astype(F32)).astype(BF)], axis=1)


PROJ_ROWS = 512


def _attn_fwd(u, slopes, w_all, w3_all, cw_all):
    S = u.shape[0]
    hpr = HEAD_PAIRS
    n_blocks = S // QB
    later = range(3, 3 * hpr)

    def body(sl_ref, u_ref, w_in_ref, w3_in_ref, cw_in_ref, o_ref, lse_ref, q_ref, k_ref, v_ref, w_ref, w3_ref,
             cw_ref, acc, m_s, l_s, w_tile, staged, tile_sems, *sems):
        hp = pl.program_id(0)
        me = _my_place()[3]
        pieces = _PieceGather(lambda p, lo, hi: w_ref.at[p, :, lo:hi], w_ref, *sems[0:2])
        gathers = (_WeightGather(lambda p, cols: _block(w_ref, me, cols), w_ref, *sems[2:5], REST_COLS),
                   _WeightGather(lambda p, cols: w3_ref.at[me], w3_ref, *sems[5:8], _whole),
                   _WeightGather(lambda p, cols: cw_ref.at[me], cw_ref, *sems[8:11], _whole))

        @pl.when(hp == 0)
        def _():
            pieces.start(later)
            for g in gathers:
                g.start()

        @pl.when(hp == FORWARD_STEP)
        def _():
            for g in gathers:
                g.forward()

        for h in range(hpr):
            @pl.when(hp == h)
            def _(h=h):
                if h > 0:
                    pieces.wait_recv(range(3 * h, 3 * h + 3))
                fetch = []
                for seg in range(3):
                    p, lo = _qkv_piece(h, seg)
                    fetch.append(pltpu.make_async_copy(w_ref.at[p, :, lo:lo + 128], w_tile.at[:, seg * 128:(seg + 1) * 128],
                                                       tile_sems.at[seg]))
                    fetch[-1].start()
                for cp in fetch:
                    cp.wait()

        def project(i, carry):
            rows = pl.ds(pl.multiple_of(i * PROJ_ROWS, PROJ_ROWS), PROJ_ROWS)
            qkv = _dot(u_ref[rows, :], w_tile[...])
            per = PROJ_ROWS // LAYOUT_MOD
            for seg, ref in enumerate((q_ref, k_ref, v_ref)):
                staged[seg] = qkv[:, seg * 128:(seg + 1) * 128]
                for g in range(LAYOUT_MOD):
                    dst = pl.ds(pl.multiple_of(g * (S // LAYOUT_MOD) + i * per, per), per)
                    ref[dst, :] = staged.at[seg][pl.ds(g, per, stride=LAYOUT_MOD), :]
            return carry

        lax.fori_loop(0, S // PROJ_ROWS, project, 0)

        head_sum = _head_sum_matrix()
        ones_b = jnp.ones((2 * QB, QB), BF)
        m_s[...] = jnp.full(m_s.shape, NEG, F32)
        l_s[...] = jnp.zeros(l_s.shape, F32)
        acc[...] = jnp.zeros(acc.shape, F32)

        for d in DILATIONS:
            tri_le, dist, low = _fold_masks(d)
            low_b = low.astype(F32).astype(BF)
            high_b = 1.0 - low_b
            slope = [sl_ref[2 * hp + a] * float(d) for a in range(2)]
            bias = [slope[a] * dist for a in range(2)]

            def block(b, d=d, slope=slope, bias=bias, tri_le=tri_le, low=low, low_b=low_b, high_b=high_b):
                n, cur, prev = _block_rows(b, d, S)
                has_prev = n > 0
                valid = jnp.logical_or(tri_le, has_prev)
                q2 = (cur.get(q_ref) * 0.125).astype(BF)
                qs = jnp.concatenate([q2 * low_b, q2 * high_b], axis=0)
                vp = prev.get(v_ref)
                kp_b = prev.get(k_ref).astype(BF)
                kcat = jnp.concatenate([kp_b, cur.get(k_ref).astype(BF)], axis=0)
                vcat = jnp.concatenate([vp, cur.get(v_ref)], axis=0).astype(BF)
                s2 = _dot_nt(qs, kcat)
                e2 = _dot(_hi_lo(q2.astype(F32) * kp_b.astype(F32)), head_sum)
                p_rows, alpha_h, pe_h = [], [], []
                for a in range(2):
                    sp, sc = s2[a * QB:(a + 1) * QB, :QB], s2[a * QB:(a + 1) * QB, QB:]
                    comb = jnp.where(valid, jnp.where(tri_le, sc, sp) - bias[a], NEG)
                    e = jnp.where(has_prev, e2[:, a * QB:(a + 1) * QB] - slope[a] * float(QB), NEG)
                    m_old = cur.get(m_s.at[a])
                    m_new = jnp.maximum(jnp.maximum(m_old, jnp.max(comb, axis=-1, keepdims=True)), e)
                    cur.put(m_s.at[a], m_new)
                    p = jnp.exp(comb - m_new)
                    pe_h.append(jnp.exp(e - m_new))
                    alpha_h.append(jnp.exp(m_old - m_new))
                    p_rows.append(jnp.concatenate([jnp.where(tri_le, 0.0, p).astype(BF),
                                                   jnp.where(tri_le, p, 0.0).astype(BF)], axis=1))
                pv = _dot(jnp.concatenate(p_rows, axis=0), jnp.concatenate([vcat, ones_b], axis=1))
                for a in range(2):
                    cur.put(l_s.at[a], alpha_h[a] * cur.get(l_s.at[a]) + pv[a * QB:(a + 1) * QB, QB:] + pe_h[a])
                cur.put(acc, jnp.where(low, alpha_h[0], alpha_h[1]) * cur.get(acc)
                        + jnp.where(low, pv[:QB, :QB], pv[QB:, :QB]) + jnp.where(low, pe_h[0], pe_h[1]) * vp)

            def several(it, carry, block=block):
                for u in range(ATT_UNROLL):
                    block(it * ATT_UNROLL + u)
                return carry

            lax.fori_loop(0, n_blocks // ATT_UNROLL, several, 0)

        low = _fold_masks(LAYOUT_MOD)[2]

        def finish(i, carry):
            rows = pl.ds(pl.multiple_of(i * QB, QB), QB)
            l0, l1 = l_s[0, rows, :], l_s[1, rows, :]
            o_ref[_natural_rows(i, S), :] = acc[rows, :] / jnp.where(low, l0, l1)
            lse_ref[0, rows, :] = m_s[0, rows, :] + jnp.log(l0)
            lse_ref[1, rows, :] = m_s[1, rows, :] + jnp.log(l1)
            return carry

        lax.fori_loop(0, n_blocks, finish, 0)

        @pl.when(hp == hpr - 1)
        def _():
            pieces.wait_send(later)
            for g in gathers:
                g.finish()

    col = pl.BlockSpec((S, 128), lambda h: (0, h))
    act = jax.ShapeDtypeStruct((S, D), F32)
    gathered = (w_all, w3_all, cw_all)
    return pl.pallas_call(
        body, name="attn_fwd", grid=(hpr,),
        in_specs=[SMEM_SPEC, VMEM_SPEC, ANY_SPEC, ANY_SPEC, ANY_SPEC],
        out_specs=(col, pl.BlockSpec((2, S, 128), lambda h: (0, 0, h)), col, col, col, ANY_SPEC, ANY_SPEC, ANY_SPEC),
        out_shape=(act, jax.ShapeDtypeStruct((2, S, D), F32), act, act, act,
                   *[jax.ShapeDtypeStruct(t.shape, t.dtype) for t in gathered]),
        scratch_shapes=([pltpu.VMEM((S, 128), F32), pltpu.VMEM((2, S, 128), F32), pltpu.VMEM((2, S, 128), F32),
                         pltpu.VMEM((D, 3 * 128), BF), pltpu.VMEM((3, PROJ_ROWS, 128), F32),
                         pltpu.SemaphoreType.DMA((3,))]
                        + _piece_sems(3 * hpr) + WEIGHT_GATHER_SEMS * 3),
        input_output_aliases={2: 5, 3: 6, 4: 7},
        compiler_params=_params(1),
    )(slopes, u, *gathered)


def _set_rows(shape, rows):
    idx = lax.broadcasted_iota(jnp.int32, shape, 0)
    out = jnp.zeros(shape, F32)
    for r, val in rows.items():
        out = out + jnp.where(idx == r, val, 0.0)
    return out


def _mid(yc_in, pa_mid, o, x2, target, b_merge, final_g, w3):
    S = x2.shape[0]
    tm = ROW_TILE
    nsteps = S // tm
    tile = pl.BlockSpec((tm, D), lambda i: (i, 0))

    def body(yc_ref, za_ref, gcp_ref, gap_ref, o_ref, x_ref, t_ref, b_ref, fg_ref, w_ref,
             dh_ref, dmid_ref, do_ref, dyc_ref, gw_ref, small_ref, acc, stage):
        i = pl.program_id(0)

        @pl.when(i == 0)
        def _():
            acc[...] = jnp.zeros_like(acc)
            small_ref[...] = jnp.zeros_like(small_ref)

        wc, wa, wo = w_ref[0], w_ref[1], w_ref[2]
        z = za_ref[...].astype(F32)
        sg = _sigmoid(z)
        ov = o_ref[...]
        yc_in_b, ya_in_b = yc_ref[...], (z * sg * ov).astype(BF)
        yc = _dot(yc_in_b, wc)
        ya = _dot(ya_in_b, wa)
        b = b_ref[...]
        gc = _sigmoid(gcp_ref[...].astype(F32) + b[:, :D])
        ga = _sigmoid(gap_ref[...].astype(F32) + b[:, D:])
        merged = gc * yc + ga * ya
        merged_b = merged.astype(BF)
        h = x_ref[...] + _dot(merged_b, wo)
        r2 = lax.rsqrt(jnp.mean(h * h, axis=-1, keepdims=True) + EPS)
        n = h * r2
        fg = fg_ref[...]
        err = n * fg - t_ref[...]
        loss = 0.5 * jnp.sum(jnp.sum(err * err, axis=-1, keepdims=True) / D, axis=0, keepdims=True)
        dy = err / D
        g_fg = jnp.sum(dy * n, axis=0, keepdims=True)
        dn = dy * fg
        dh = r2 * (dn - n * jnp.mean(dn * n, axis=-1, keepdims=True))
        dh_ref[...] = dh
        dh_b = dh.astype(BF)
        dmerged = _dot_nt(dh_b, wo)
        acc[2] += _dot(merged.T.astype(BF), dh_b)
        dyc = (dmerged * gc).astype(BF)
        dya = (dmerged * ga).astype(BF)
        dgcp = dmerged * yc * gc * (1.0 - gc)
        dgap = dmerged * ya * ga * (1.0 - ga)
        dmid_ref[1] = dgcp.astype(BF)
        dmid_ref[2] = dgap.astype(BF)
        acc[0] += _dot(yc_in_b.astype(F32).T.astype(BF), dyc)
        acc[1] += _dot(ya_in_b.astype(F32).T.astype(BF), dya)
        dyc_ref[...] = _dot_nt(dyc, wc).astype(BF)
        dya_in = _dot_nt(dya, wa)
        do_ref[...] = dya_in * (z * sg)
        dmid_ref[0] = (dya_in * ov * (sg * (1.0 + z * (1.0 - sg)))).astype(BF)
        small_ref[...] += _set_rows((8, D), {
            1: jnp.sum(dgcp, axis=0, keepdims=True), 2: jnp.sum(dgap, axis=0, keepdims=True),
            3: g_fg, 7: jnp.broadcast_to(loss, (1, D))})

        @pl.when(i == nsteps - 1)
        def _():
            for p in range(N_DEV):
                for a in range(3):
                    stage[...] = acc[a, p * ROW_SHARD:(p + 1) * ROW_SHARD, :].astype(BF)
                    pltpu.sync_copy(stage, gw_ref.at[p, a])

    return pl.pallas_call(
        body, name="mid", grid=(nsteps,),
        in_specs=[tile, pl.BlockSpec((tm, D), lambda i: (i, 0)), pl.BlockSpec((tm, D), lambda i: (i, 1)),
                  pl.BlockSpec((tm, D), lambda i: (i, 2)), tile, tile, tile,
                  pl.BlockSpec((1, 2 * D), lambda i: (0, 0)), pl.BlockSpec((1, D), lambda i: (0, 0)), VMEM_SPEC],
        out_specs=(tile, pl.BlockSpec((3, tm, D), lambda i: (0, i, 0)), tile, tile,
                   ANY_SPEC, pl.BlockSpec((8, D), lambda i: (0, 0))),
        out_shape=(jax.ShapeDtypeStruct((S, D), F32), jax.ShapeDtypeStruct((3, S, D), BF),
                   jax.ShapeDtypeStruct((S, D), F32), jax.ShapeDtypeStruct((S, D), BF),
                   jax.ShapeDtypeStruct((N_DEV, 3, ROW_SHARD, D), BF), jax.ShapeDtypeStruct((8, D), F32)),
        scratch_shapes=[pltpu.VMEM((3, D, D), F32), pltpu.VMEM((ROW_SHARD, D), BF)],
        compiler_params=_params(1),
    )(yc_in, pa_mid, pa_mid, pa_mid, o, x2, target, b_merge, final_g, w3)


def _conv_bwd(dyc_in, pa, cw8):
    S = pa.shape[0]
    tm, tc = CONV_TM, CONV_TC
    nct = D // tc
    nrt = S // tm
    last_halo = S // HALO - 1

    def seg(s):
        return pl.BlockSpec((tm, tc), lambda j, i, s=s: (i, s * nct + j))

    def halo_before(s):
        return pl.BlockSpec((HALO, tc), lambda j, i, s=s: (jnp.maximum(i * (tm // HALO) - 1, 0), s * nct + j))

    def halo_after(s):
        return pl.BlockSpec((HALO, tc), lambda j, i, s=s: (jnp.minimum((i + 1) * (tm // HALO), last_halo), s * nct + j))

    def body(dy, xc, bg, cg, zc, xch, cgh, dyn, bgn, zcn, cw, dout, gcw):
        i = pl.program_id(1)

        @pl.when(i == 0)
        def _():
            gcw[...] = jnp.zeros_like(gcw)

        xcv, cgv = xc[...].astype(F32), cg[...].astype(F32)
        a = cgv * xcv
        ah = jnp.where(i > 0, cgh[...].astype(F32) * xch[...].astype(F32), 0.0)
        row = lax.broadcasted_iota(jnp.int32, (tm, tc), 0)
        a1 = jnp.where(row == 0, ah[HALO - 1:HALO, :], pltpu.roll(a, 1, 0))
        a2 = jnp.where(row == 0, ah[HALO - 2:HALO - 1, :],
                       jnp.where(row == 1, ah[HALO - 1:HALO, :], pltpu.roll(a, 2, 0)))
        w = cw[...]
        conv = w[0:1, :] * a2 + w[1:2, :] * a1 + w[2:3, :] * a
        z = zc[...].astype(F32)
        sg = _sigmoid(z)
        silu = z * sg
        bgv = bg[...].astype(F32)
        dyv = dy[...].astype(F32)
        dout[3] = (dyv * bgv * conv * (sg * (1.0 + z * (1.0 - sg)))).astype(BF)
        dout[1] = (dyv * silu * conv).astype(BF)
        dc = dyv * silu * bgv
        zn = zcn[...].astype(F32)
        dcn = dyn[...].astype(F32) * (zn * _sigmoid(zn)) * bgn[...].astype(F32)
        dcn = jnp.where(i < nrt - 1, dcn, 0.0)
        dc1 = jnp.where(row == tm - 1, dcn[0:1, :], pltpu.roll(dc, tm - 1, 0))
        dc2 = jnp.where(row == tm - 1, dcn[1:2, :],
                        jnp.where(row == tm - 2, dcn[0:1, :], pltpu.roll(dc, tm - 2, 0)))
        da = w[2:3, :] * dc + w[1:2, :] * dc1 + w[0:1, :] * dc2
        dout[2] = (da * xcv).astype(BF)
        dout[0] = (da * cgv).astype(BF)
        gcw[...] += _set_rows((8, tc), {
            4: jnp.sum(dc * a2, axis=0, keepdims=True), 5: jnp.sum(dc * a1, axis=0, keepdims=True),
            6: jnp.sum(dc * a, axis=0, keepdims=True)})

    return pl.pallas_call(
        body, name="conv_bwd", grid=(nct, nrt),
        in_specs=[pl.BlockSpec((tm, tc), lambda j, i: (i, j)), seg(0), seg(1), seg(2), seg(3),
                  halo_before(0), halo_before(2),
                  pl.BlockSpec((HALO, tc), lambda j, i: (jnp.minimum((i + 1) * (tm // HALO), last_halo), j)),
                  halo_after(1), halo_after(3), pl.BlockSpec((8, tc), lambda j, i: (0, j))],
        out_specs=(pl.BlockSpec((4, tm, tc), lambda j, i: (0, i, j)), pl.BlockSpec((8, tc), lambda j, i: (0, j))),
        out_shape=(jax.ShapeDtypeStruct((4, S, D), BF), jax.ShapeDtypeStruct((8, D), F32)),
        compiler_params=_params(2),
    )(dyc_in, pa, pa, pa, pa, pa, pa, dyc_in, pa, pa, cw8)


def _attn_bwd(q, k, v, slopes, do, o, lse, g_in, g_3):
    S = q.shape[0]
    hpr = HEAD_PAIRS
    n_blocks = S // QB

    def body(sl_ref, q_ref, k_ref, v_ref, do_ref, o_ref, lse_ref, gin_ref, g3_ref, out_ref, rin_ref, r3_ref,
             dq_s, dk_s, dv_s, do_s, dd_s, *sems):
        hp = pl.program_id(0)
        exchanges = (_GradExchange(gin_ref, rin_ref, *sems[:3], _shard_cols((0, SEG0_ATTN * D), (SEG0_MID * D, IN_COLS))),
                     _GradExchange(g3_ref, r3_ref, *sems[3:], _whole))

        @pl.when(hp == 0)
        def _():
            for ex in exchanges:
                ex.start()

        head_sum = _head_sum_matrix()
        dq_s[...] = jnp.zeros(dq_s.shape, F32)
        dk_s[...] = jnp.zeros(dk_s.shape, F32)
        dv_s[...] = jnp.zeros(dv_s.shape, F32)

        def row_dots(i, carry):
            rows = pl.ds(pl.multiple_of(i * QB, QB), QB)
            natural = _natural_rows(i, S)
            do_c = do_ref[natural, :]
            do_s[rows, :] = do_c
            dd = _dot(_hi_lo(do_c * o_ref[natural, :]), head_sum)
            dd_s[0, rows, :] = dd[:, :QB]
            dd_s[1, rows, :] = dd[:, QB:]
            return carry

        lax.fori_loop(0, n_blocks, row_dots, 0)

        for d in DILATIONS:
            tri_le, dist, low = _fold_masks(d)
            low_b = low.astype(F32).astype(BF)
            high_b = 1.0 - low_b
            slope = [sl_ref[2 * hp + a] * float(d) for a in range(2)]
            bias = [slope[a] * dist for a in range(2)]

            def block(b, d=d, slope=slope, bias=bias, tri_le=tri_le, low=low, low_b=low_b, high_b=high_b):
                n, cur, prev = _block_rows(b, d, S)
                has_prev = n > 0
                valid = jnp.logical_or(tri_le, has_prev)
                q2f = cur.get(q_ref) * 0.125
                q2 = q2f.astype(BF)
                qs = jnp.concatenate([q2 * low_b, q2 * high_b], axis=0)
                kp, vp = prev.get(k_ref), prev.get(v_ref)
                kp_b, vp_b = kp.astype(BF), vp.astype(BF)
                kcat = jnp.concatenate([kp_b, cur.get(k_ref).astype(BF)], axis=0)
                vcat = jnp.concatenate([vp_b, cur.get(v_ref).astype(BF)], axis=0)
                do2f = cur.get(do_s)
                do2 = do2f.astype(BF)
                dos = jnp.concatenate([do2 * low_b, do2 * high_b], axis=0)
                s2 = _dot_nt(qs, kcat)
                dp2 = _dot_nt(dos, vcat)
                diag2 = _dot(jnp.concatenate([_hi_lo(q2.astype(F32) * kp_b.astype(F32)),
                                              _hi_lo(do2.astype(F32) * vp_b.astype(F32))], axis=0), head_sum)
                p_rows, ds_rows, pe_h, dse_h = [], [], [], []
                for a in range(2):
                    hs = slice(a * QB, (a + 1) * QB)
                    sp, sc = s2[hs, :QB], s2[hs, QB:]
                    dpp, dpc = dp2[hs, :QB], dp2[hs, QB:]
                    lse_a, dd_a = cur.get(lse_ref.at[a]), cur.get(dd_s.at[a])
                    comb = jnp.where(tri_le, sc, sp) - bias[a]
                    e = diag2[:QB, hs] - slope[a] * float(QB)
                    p = jnp.where(valid, jnp.exp(comb - lse_a), 0.0)
                    pe = jnp.where(has_prev, jnp.exp(e - lse_a), 0.0)
                    ds = p * (jnp.where(tri_le, dpc, dpp) - dd_a)
                    dse_h.append(pe * (diag2[QB:, hs] - dd_a))
                    pe_h.append(pe)
                    p_rows.append(jnp.concatenate([jnp.where(tri_le, 0.0, p).astype(BF),
                                                   jnp.where(tri_le, p, 0.0).astype(BF)], axis=1))
                    ds_rows.append(jnp.concatenate([jnp.where(tri_le, 0.0, ds).astype(BF),
                                                    jnp.where(tri_le, ds, 0.0).astype(BF)], axis=1))
                pst = jnp.concatenate(p_rows, axis=0)
                dst = jnp.concatenate(ds_rows, axis=0)
                pe2 = jnp.where(low, pe_h[0], pe_h[1])
                dse2 = jnp.where(low, dse_h[0], dse_h[1])
                dq = _dot(dst, kcat)
                cur.add(dq_s, (jnp.where(low, dq[:QB], dq[QB:]) + dse2 * kp) * 0.125)
                dk = _dot_tn(dst, qs)
                dv = _dot_tn(pst, dos)
                prev.add(dk_s, dk[:QB] + dse2 * q2f)
                cur.add(dk_s, dk[QB:])
                prev.add(dv_s, dv[:QB] + pe2 * do2f)
                cur.add(dv_s, dv[QB:])

            def several(it, carry, block=block):
                for u in range(ATT_UNROLL):
                    block(it * ATT_UNROLL + u)
                return carry

            lax.fori_loop(0, n_blocks // ATT_UNROLL, several, 0)

        def finish(i, carry):
            rows = pl.ds(pl.multiple_of(i * QB, QB), QB)
            natural = _natural_rows(i, S)
            for t, ref in enumerate((dq_s, dk_s, dv_s)):
                out_ref.at[t][natural, :] = ref[rows, :]
            return carry

        lax.fori_loop(0, n_blocks, finish, 0)

        @pl.when(hp == hpr - 1)
        def _():
            for ex in exchanges:
                ex.finish()

    col = pl.BlockSpec((S, 128), lambda h: (0, h))
    return pl.pallas_call(
        body, name="attn_bwd", grid=(hpr,),
        in_specs=[SMEM_SPEC, col, col, col, col, col, pl.BlockSpec((2, S, 128), lambda h: (0, 0, h)),
                  ANY_SPEC, ANY_SPEC],
        out_specs=(pl.BlockSpec((3, S, 128), lambda h: (0, 0, h)), ANY_SPEC, ANY_SPEC),
        out_shape=(jax.ShapeDtypeStruct((3, S, D), F32), jax.ShapeDtypeStruct(g_in.shape, BF),
                   jax.ShapeDtypeStruct(g_3.shape, BF)),
        scratch_shapes=([pltpu.VMEM((S, 128), F32)] * 4 + [pltpu.VMEM((2, S, 128), F32)]
                        + GRAD_EXCHANGE_SEMS + GRAD_EXCHANGE_SEMS),
        compiler_params=_params(1),
    )(slopes, q, k, v, do, o, lse, g_in, g_3)


WG_TN = 256
SEG0_CONV, SEG0_ATTN, SEG0_MID = 0, 4, 7


def _wgrad_in(ut, d_group, seg0, g_in, name):
    S = ut.shape[1]
    tn = WG_TN
    per_seg = D // tn
    per_shard = W_IN_SHARD // tn
    n_tiles = d_group.shape[0] * per_seg
    tile0 = seg0 * per_seg

    def body(ut_ref, d_ref, *rest):
        rest[-1][0] = _dot(ut_ref[...], d_ref[0].astype(BF)).astype(BF)

    operands, in_specs, aliases = [ut, d_group], [VMEM_SPEC, pl.BlockSpec((1, S, tn), lambda t: (t // per_seg, 0, t % per_seg))], {}
    if g_in is not None:
        operands.append(g_in)
        in_specs.append(ANY_SPEC)
        aliases = {2: 0}
    return pl.pallas_call(
        body, name=name, grid=(n_tiles,), in_specs=in_specs,
        out_specs=pl.BlockSpec((1, D, tn), lambda t: ((tile0 + t) // per_shard, 0, (tile0 + t) % per_shard)),
        out_shape=jax.ShapeDtypeStruct((N_DEV, D, W_IN_SHARD), BF),
        input_output_aliases=aliases,
        compiler_params=_params(1),
    )(*operands)


def _dgrad_norm_bwd(d_conv, d_attn, d_mid, w_all, x2, dh, norm_g):
    S = x2.shape[0]
    tm = ROW_TILE
    nsteps = S // tm
    tile = pl.BlockSpec((tm, D), lambda i: (i, 0))
    pieces = _proj_pieces()

    def body(a_ref, b_ref, c_ref, w_ref, x_ref, dh_ref, g_ref, gx_ref, small_ref):
        i = pl.program_id(0)

        @pl.when(i == 0)
        def _():
            small_ref[...] = jnp.zeros_like(small_ref)

        groups = (a_ref, b_ref, c_ref)
        du = jnp.zeros((tm, D), F32)
        for s, sc, p, pc, width in pieces:
            g = 0 if s < 4 else (1 if s < 7 else 2)
            local = s - (0, 4, 7)[g]
            du = du + _dot_nt(groups[g][local, :, sc:sc + width].astype(BF), w_ref[p, :, pc:pc + width])
        xv = x_ref[...]
        r = lax.rsqrt(jnp.mean(xv * xv, axis=-1, keepdims=True) + EPS)
        n = xv * r
        dn = du * g_ref[...]
        gx_ref[...] = dh_ref[...] + r * (dn - n * jnp.mean(dn * n, axis=-1, keepdims=True))
        small_ref[...] += _set_rows((8, D), {0: jnp.sum(du * n, axis=0, keepdims=True)})

    return pl.pallas_call(
        body, name="dgrad_norm_bwd", grid=(nsteps,),
        in_specs=[pl.BlockSpec((4, tm, D), lambda i: (0, i, 0)), pl.BlockSpec((3, tm, D), lambda i: (0, i, 0)),
                  pl.BlockSpec((3, tm, D), lambda i: (0, i, 0)), VMEM_SPEC, tile, tile,
                  pl.BlockSpec((1, D), lambda i: (0, 0))],
        out_specs=(tile, pl.BlockSpec((8, D), lambda i: (0, 0))),
        out_shape=(jax.ShapeDtypeStruct((S, D), F32), jax.ShapeDtypeStruct((8, D), F32)),
        compiler_params=_params(1),
    )(d_conv, d_attn, d_mid, w_all, x2, dh, norm_g)


HBM_SPEC = pl.BlockSpec(memory_space=pltpu.HBM)
SEM_SPEC = pl.BlockSpec(memory_space=pltpu.SEMAPHORE)
ATTN_COLS = _shard_cols((SEG0_ATTN * D, SEG0_MID * D))


def _attn_cols_exchange_start(g_in, r_in):
    def body(g_ref, r_ref, send_sems, recv_sems, g_thru, r_thru, token):
        _GradExchange(g_ref, r_ref, send_sems, recv_sems, None, ATTN_COLS).start()
        token[...] = jnp.zeros_like(token)

    hbm = pltpu.with_memory_space_constraint
    return pl.pallas_call(
        body, name="attn_cols_exchange_start",
        out_shape=(pltpu.SemaphoreType.DMA((N_DEV,)), pltpu.SemaphoreType.DMA((N_DEV,)),
                   pltpu.HBM(g_in.shape, g_in.dtype), pltpu.HBM(r_in.shape, r_in.dtype),
                   jax.ShapeDtypeStruct((8, 128), F32)),
        in_specs=(HBM_SPEC, HBM_SPEC), out_specs=(SEM_SPEC, SEM_SPEC, HBM_SPEC, HBM_SPEC, VMEM_SPEC),
        input_output_aliases={0: 2, 1: 3},
        compiler_params=pltpu.CompilerParams(has_side_effects=pltpu.SideEffectType.DATAFLOW_SIDE_EFFECTING),
    )(hbm(g_in, pltpu.HBM), hbm(r_in, pltpu.HBM))


def _attn_cols_exchange_wait(send_sems, recv_sems, g_thru, r_thru, after):
    def body(g_ref, r_ref, send_sems, recv_sems, after_ref, g_dead, r_out):
        _GradExchange(g_ref, r_ref, send_sems, recv_sems, None, ATTN_COLS).finish()

    return pl.pallas_call(
        body, name="attn_cols_exchange_wait",
        out_shape=(pltpu.HBM(g_thru.shape, g_thru.dtype), pltpu.HBM(r_thru.shape, r_thru.dtype)),
        in_specs=(HBM_SPEC, HBM_SPEC, SEM_SPEC, SEM_SPEC, ANY_SPEC), out_specs=(HBM_SPEC, HBM_SPEC),
        input_output_aliases={0: 0, 1: 1},
        compiler_params=pltpu.CompilerParams(has_side_effects=pltpu.SideEffectType.DATAFLOW_SIDE_EFFECTING),
    )(g_thru, r_thru, send_sems, recv_sems, after)


def _adamw_math(w, g, m, v):
    m = ADAM_B1 * m + (1.0 - ADAM_B1) * g
    v = ADAM_B2 * v + (1.0 - ADAM_B2) * (g * g)
    m_hat = m / (1.0 - ADAM_B1 ** ADAM_STEP)
    v_hat = v / (1.0 - ADAM_B2 ** ADAM_STEP)
    delta = -ADAM_LR * (m_hat / (jnp.sqrt(v_hat) + ADAM_EPS) + ADAM_WD * w)
    return delta, m, v


def _sum_adamw(parts, w, m, v, tm, name):
    R, C = w.shape
    tile = pl.BlockSpec((tm, C), lambda i: (i, 0))

    def body(p_ref, w_ref, m_ref, v_ref, g_out, d_out, m_out, v_out):
        g = p_ref[0].astype(F32)
        for s in range(1, N_DEV):
            g = g + p_ref[s].astype(F32)
        g_out[...] = g
        d_out[...], m_out[...], v_out[...] = _adamw_math(w_ref[...], g, m_ref[...], v_ref[...])

    shape = jax.ShapeDtypeStruct((R, C), F32)
    return pl.pallas_call(
        body, name=name, grid=(R // tm,),
        in_specs=[pl.BlockSpec((N_DEV, tm, C), lambda i: (0, i, 0)), tile, tile, tile],
        out_specs=(tile, tile, tile, tile), out_shape=(shape, shape, shape, shape),
        compiler_params=_params(1),
    )(parts, w, m, v)


def _adamw(g, w, m, v, name):
    def body(g_ref, w_ref, m_ref, v_ref, d_out, m_out, v_out):
        d_out[...], m_out[...], v_out[...] = _adamw_math(w_ref[...], g_ref[...], m_ref[...], v_ref[...])

    shape = jax.ShapeDtypeStruct(w.shape, F32)
    return pl.pallas_call(
        body, name=name, in_specs=[VMEM_SPEC] * 4, out_specs=(VMEM_SPEC,) * 3, out_shape=(shape, shape, shape),
    )(g, w, m, v)


def _alibi_slopes():
    return jnp.exp2(-8.0 * jnp.arange(1, N_HEADS + 1, dtype=F32) / N_HEADS)


def _local_step(x2, target, norm_g, b_merge, final_g, w_in, w3_shard, cw_shard):
    slopes = _alibi_slopes()
    u, ut, w_all, w3_all, cw_all = _norm_gather_first_weights(x2, norm_g, w_in, w3_shard, cw_shard)
    o, lse, q, k, v, w_all, w3_all, cw_all = _attn_fwd(u, slopes, w_all, w3_all, cw_all)
    w3 = jnp.transpose(w3_all, (1, 0, 2, 3)).reshape(3, D, D)
    cw8 = jnp.transpose(cw_all, (1, 0, 2)).reshape(8, D)
    pa = _proj_cols(u, w_all, SEG0_CONV, 4, BF, "proj_conv")
    yc_in = _conv_fwd(pa, cw8)
    pa_mid = _proj_cols(u, w_all, SEG0_MID, 3, BF, "proj_mid")
    dh, d_mid, do, dyc_in, g_3, small_mid = _mid(yc_in, pa_mid, o, x2, target, b_merge, final_g, w3)
    g_in = _wgrad_in(ut, d_mid, SEG0_MID, None, "wgrad_in_mid")
    d_conv, small_conv = _conv_bwd(dyc_in, pa, cw8)
    g_in = _wgrad_in(ut, d_conv, SEG0_CONV, g_in, "wgrad_in_conv")
    d_attn, r_in, r_3 = _attn_bwd(q, k, v, slopes, do, o, lse, g_in, g_3)
    g_in = _wgrad_in(ut, d_attn, SEG0_ATTN, g_in, "wgrad_in_attn")
    *in_flight, token = _attn_cols_exchange_start(g_in, r_in)
    grad_x, small_norm = _dgrad_norm_bwd(d_conv, d_attn, d_mid, w_all, x2, dh, norm_g + token[0:1, 0:1])
    return grad_x, in_flight, r_3, small_mid, small_conv, small_norm


def kernel(x, norm_g, w_in, b_merge, conv_w, w_out_conv, w_out_attn, w_o, final_g, loss_target, m_norm_g, m_w_in, m_b_merge, m_conv_w, m_w_out_conv, m_w_out_attn, m_w_o, m_final_g, v_norm_g, v_w_in, v_b_merge, v_conv_w, v_w_out_conv, v_w_out_attn, v_w_o, v_final_g):
    me = 4 * lax.axis_index("x") + 2 * lax.axis_index("y") + lax.axis_index("c")
    stack3 = lambda a, b, c: jnp.concatenate([a, b, c], axis=0)
    pad8 = lambda a: jnp.pad(a, ((0, 8 - a.shape[0]), (0, 0)))

    w3_shard = stack3(w_out_conv, w_out_attn, w_o)
    final_g2 = final_g.reshape(1, D)
    grad_x, in_flight, r_3, small_mid, small_conv, small_norm = _local_step(
        x[0], loss_target[0], norm_g, b_merge, final_g2, w_in[0], w3_shard, pad8(conv_w[0]))

    small = _allreduce_small(small_mid, small_conv, small_norm)
    g_in, r_in = _attn_cols_exchange_wait(*in_flight, small)
    own = lax.dynamic_index_in_dim(g_in, me, 0, keepdims=True)
    r_in = lax.dynamic_update_slice(r_in, own, (me, 0, 0))

    g_w_in, d_w_in, nm_w_in, nv_w_in = _sum_adamw(r_in, w_in[0], m_w_in[0], v_w_in[0], 128, "adamw_w_in")
    g_w3, d_w3, nm_w3, nv_w3 = _sum_adamw(
        r_3.reshape(N_DEV, 3 * ROW_SHARD, D), w3_shard.reshape(3 * ROW_SHARD, D),
        stack3(m_w_out_conv, m_w_out_attn, m_w_o).reshape(3 * ROW_SHARD, D),
        stack3(v_w_out_conv, v_w_out_attn, v_w_o).reshape(3 * ROW_SHARD, D), ROW_SHARD, "adamw_w3")

    def pack(ng, bm, fg):
        return pad8(jnp.concatenate([ng, bm.reshape(2, D), fg.reshape(1, D)], axis=0))

    d_s, nm_s, nv_s = _adamw(small, pack(norm_g, b_merge, final_g), pack(m_norm_g, m_b_merge, m_final_g),
                             pack(v_norm_g, v_b_merge, v_final_g), "adamw_small")
    g_cw = lax.dynamic_slice(small, (4, me * ROW_SHARD), (3, ROW_SHARD))
    d_cw, nm_cw, nv_cw = _adamw(g_cw, conv_w[0], m_conv_w[0], v_conv_w[0], "adamw_conv_w")

    loss = small[7, 0]
    split3 = lambda t: tuple(t[a * ROW_SHARD:(a + 1) * ROW_SHARD][None] for a in range(3))
    unpack = lambda t: (t[0:1], t[1:3].reshape(1, 2 * D), t[3])

    def leaves(in_, small_, cw_, w3_):
        ng, bm, fg = unpack(small_)
        wc, wa, wo = split3(w3_)
        return (ng, in_[None], bm, cw_[None], wc, wa, wo, fg)

    return (loss, grad_x[None],
            *leaves(g_w_in, small, g_cw, g_w3),
            *leaves(d_w_in, d_s, d_cw, d_w3),
            *leaves(nm_w_in, nm_s, nm_cw, nm_w3),
            *leaves(nv_w_in, nv_s, nv_cw, nv_w3))
```

```python
import functools

import jax
import jax.numpy as jnp
from jax import lax
from jax.experimental import pallas as pl
from jax.experimental.pallas import tpu as pltpu

D = 1024
N_HEADS = 16
HEAD_DIM = 64
N_SEG = 10
IN_COLS = N_SEG * D
N_DEV = 8
W_IN_SHARD = IN_COLS // N_DEV
ROW_SHARD = D // N_DEV
QB = 128
DILATIONS = (1, 4, 16)
EPS = 1e-6
NEG = -1e30
BF = jnp.bfloat16
F32 = jnp.float32
MESH = pl.DeviceIdType.MESH

ADAM_LR = 0.001
ADAM_B1 = 0.9
ADAM_B2 = 0.999
ADAM_EPS = 1e-08
ADAM_WD = 0.01
ADAM_STEP = 10

V7X_VMEM_BYTES = 64 * 1024 * 1024
VMEM_LIMIT = V7X_VMEM_BYTES - 8 * 1024 * 1024
ROW_TILE = 256

VMEM_SPEC = pl.BlockSpec(memory_space=pltpu.VMEM)
ANY_SPEC = pl.BlockSpec(memory_space=pl.ANY)
SMEM_SPEC = pl.BlockSpec(memory_space=pltpu.SMEM)


def _params(n_grid_axes, vmem=VMEM_LIMIT):
    return pltpu.CompilerParams(dimension_semantics=("arbitrary",) * n_grid_axes, vmem_limit_bytes=vmem)


def _dot(a, b):
    return jnp.dot(a, b, preferred_element_type=F32)


def _dot_nt(a, b):
    return lax.dot_general(a, b, (((1,), (1,)), ((), ())), preferred_element_type=F32)


def _dot_tn(a, b):
    return lax.dot_general(a, b, (((0,), (0,)), ((), ())), preferred_element_type=F32)


def _sigmoid(z):
    return 1.0 / (1.0 + jnp.exp(-z))


def _my_place():
    x, y, c = lax.axis_index("x"), lax.axis_index("y"), lax.axis_index("c")
    return x, y, c, 4 * x + 2 * y + c


def _peers(x, y, c):
    out = []
    for k in range(1, N_DEV):
        px = 1 - x if k & 4 else x
        py = 1 - y if k & 2 else y
        pc = 1 - c if k & 1 else c
        out.append(((px, py, pc), 4 * px + 2 * py + pc))
    return out


def _device(p):
    return (p >> 2, (p >> 1) & 1, p & 1)


def _shard_cols(*ranges):
    def cols(p):
        found = None
        for lo, hi in ranges:
            a, b = max(lo, p * W_IN_SHARD), min(hi, (p + 1) * W_IN_SHARD)
            if a < b:
                assert found is None
                found = (a - p * W_IN_SHARD, b - p * W_IN_SHARD)
        return found

    return cols


def _whole(p):
    return ()


def _block(ref, idx, cols):
    return ref.at[idx] if cols == () else ref.at[idx, :, cols[0]:cols[1]]


class _WeightGather:
    def __init__(self, src, dst, send_sems, forward_sems, recv_sems, cols):
        self.src, self.dst, self.cols = src, dst, cols
        self.send_sems, self.forward_sems, self.recv_sems = send_sems, forward_sems, recv_sems
        self.me = _my_place()[3]

    def _copy(self, p, target, passing_on=False):
        cols = self.cols(p)
        return pltpu.make_async_remote_copy(
            src_ref=_block(self.dst, p, cols) if passing_on else self.src(p, cols), dst_ref=_block(self.dst, p, cols),
            send_sem=self.forward_sems.at[p] if passing_on else self.send_sems.at[target],
            recv_sem=self.recv_sems.at[p], device_id=_device(target), device_id_type=MESH)

    def _as_each_device(self, own, relayed, other):
        for m in range(N_DEV):
            def branch(m=m):
                for p in range(N_DEV):
                    if self.cols(p) is None:
                        continue
                    if p == m:
                        for t in [m ^ 1] + [q for q in range(N_DEV) if q >> 1 != m >> 1 and q & 1 == m & 1]:
                            own(self._copy(m, t))
                    elif p >> 1 != m >> 1 and p & 1 == m & 1:
                        relayed(p, m ^ 1)
                    else:
                        other(p)

            pl.when(self.me == m)(branch)

    def start(self):
        self._as_each_device(lambda cp: cp.start(), lambda p, t: None, lambda p: None)

    def forward(self):
        def pass_on(p, t):
            self._copy(p, p).wait_recv()
            self._copy(p, t, passing_on=True).start()

        self._as_each_device(lambda cp: None, pass_on, lambda p: None)

    def finish(self):
        self._as_each_device(lambda cp: cp.wait_send(), lambda p, t: self._copy(p, t, passing_on=True).wait_send(),
                             lambda p: self._copy(p, p).wait_recv())


WEIGHT_GATHER_SEMS = [pltpu.SemaphoreType.DMA((N_DEV,))] * 3
FORWARD_STEP = 6
REST_COLS = _shard_cols((0, 4 * D), (7 * D, IN_COLS))
HEAD_PAIRS = D // 128


def _qkv_piece(h, seg):
    col = (4 + seg) * D + 128 * h
    return col // W_IN_SHARD, col % W_IN_SHARD


class _PieceGather:
    def __init__(self, src, dst, send_sems, recv_sems, relay):
        self.src, self.dst, self.send_sems, self.recv_sems, self.relay = src, dst, send_sems, recv_sems, relay
        self.me = _my_place()[3]

    def _copy(self, i, target, passing_on=False):
        p, lo = _qkv_piece(i // 3, i % 3)
        block = self.dst.at[p, :, lo:lo + 128]
        return pltpu.make_async_remote_copy(
            src_ref=block if passing_on else self.src(p, lo, lo + 128), dst_ref=block,
            send_sem=self.send_sems.at[i, target], recv_sem=self.recv_sems.at[i],
            device_id=_device(target), device_id_type=MESH)

    def _as_each_device(self, pieces, own, relayed, other):
        for m in range(N_DEV):
            def branch(m=m):
                for i in pieces:
                    p = _qkv_piece(i // 3, i % 3)[0]
                    same_core_elsewhere = p >> 1 != m >> 1 and p & 1 == m & 1
                    if p == m:
                        targets = [(p + 1 + (k + i) % (N_DEV - 1)) % N_DEV for k in range(N_DEV - 1)]
                        for t in targets:
                            if not self.relay or t == m ^ 1 or (t >> 1 != m >> 1 and t & 1 == m & 1):
                                own(self._copy(i, t))
                    elif self.relay and same_core_elsewhere:
                        relayed(i, m ^ 1)
                    else:
                        other(i)

            pl.when(self.me == m)(branch)

    def start(self, pieces):
        self._as_each_device(pieces, lambda cp: cp.start(), lambda i, t: None, lambda i: None)

    def forward(self, pieces):
        def pass_on(i, t):
            self._copy(i, t).wait_recv()
            self._copy(i, t, passing_on=True).start()

        self._as_each_device(pieces, lambda cp: None, pass_on, lambda i: None)

    def wait_recv(self, pieces):
        self._as_each_device(pieces, lambda cp: None, lambda i, t: None, lambda i: self._copy(i, 0).wait_recv())

    def wait_send(self, pieces):
        self._as_each_device(pieces, lambda cp: cp.wait_send(),
                             lambda i, t: self._copy(i, t, passing_on=True).wait_send(), lambda i: None)


FIRST_PIECES = 6


def _piece_sems(n):
    return [pltpu.SemaphoreType.DMA((n, N_DEV)), pltpu.SemaphoreType.DMA((n,))]


def _norm_gather_first_weights(x2, norm_g, w_in, w3, cw):
    S = x2.shape[0]
    tm = ROW_TILE
    nsteps = S // tm

    def body(x_ref, g_ref, w_in_ref, w3_ref, cw_ref, u_ref, ut_ref, o_in, o_3, o_cw, in_bf, w3_bf, local_sems, *sems):
        i = pl.program_id(0)
        me = _my_place()[3]
        gather = _PieceGather(lambda p, lo, hi: in_bf.at[:, lo:hi], o_in, *sems, relay=False)
        local = [pltpu.make_async_copy(src, dst.at[me], local_sems.at[a])
                 for a, (src, dst) in enumerate(((in_bf, o_in), (w3_bf, o_3), (cw_ref, o_cw)))]

        @pl.when(i == 0)
        def _():
            def cast_rows(r, carry):
                rows = pl.ds(pl.multiple_of(r * 128, 128), 128)
                in_bf[rows, :] = w_in_ref[rows, :].astype(BF)
                return carry

            lax.fori_loop(0, D // 128, cast_rows, 0)
            for a in range(3):
                w3_bf[a] = w3_ref[a].astype(BF)
            gather.start(range(FIRST_PIECES))
            for cp in local:
                cp.start()

        xv = x_ref[...]
        r = lax.rsqrt(jnp.mean(xv * xv, axis=-1, keepdims=True) + EPS)
        u = xv * r * g_ref[...]
        u_ref[...] = u.astype(BF)
        ut_ref[...] = u.T.astype(BF)

        @pl.when(i == nsteps - 1)
        def _():
            gather.wait_recv(range(FIRST_PIECES))
            gather.wait_send(range(FIRST_PIECES))
            for cp in local:
                cp.wait()

    return pl.pallas_call(
        body, name="norm_gather_first_weights", grid=(nsteps,),
        out_shape=(jax.ShapeDtypeStruct((S, D), BF), jax.ShapeDtypeStruct((D, S), BF),
                   jax.ShapeDtypeStruct((N_DEV, D, W_IN_SHARD), BF),
                   jax.ShapeDtypeStruct((N_DEV, 3, ROW_SHARD, D), BF),
                   jax.ShapeDtypeStruct((N_DEV, 8, 128), F32)),
        in_specs=[pl.BlockSpec((tm, D), lambda i: (i, 0)), pl.BlockSpec((1, D), lambda i: (0, 0)),
                  VMEM_SPEC, VMEM_SPEC, VMEM_SPEC],
        out_specs=(pl.BlockSpec((tm, D), lambda i: (i, 0)), pl.BlockSpec((D, tm), lambda i: (0, i)),
                   ANY_SPEC, ANY_SPEC, ANY_SPEC),
        scratch_shapes=[pltpu.VMEM((D, W_IN_SHARD), BF), pltpu.VMEM((3, ROW_SHARD, D), BF),
                        pltpu.SemaphoreType.DMA((3,))] + _piece_sems(FIRST_PIECES),
        compiler_params=_params(1),
    )(x2, norm_g, w_in, w3, cw)


class _GradExchange:
    def __init__(self, src, dst, send_sems, recv_sems, local_sem, cols):
        self.src, self.dst, self.cols = src, dst, cols
        self.send_sems, self.recv_sems, self.local_sem = send_sems, recv_sems, local_sem
        self.me = _my_place()[3]

    def _remote(self, p, source):
        return pltpu.make_async_remote_copy(
            src_ref=_block(self.src, p, self.cols(p)), dst_ref=_block(self.dst, source, self.cols(p)),
            send_sem=self.send_sems.at[p], recv_sem=self.recv_sems.at[source],
            device_id=_device(p), device_id_type=MESH)

    def _local(self, p):
        return pltpu.make_async_copy(_block(self.src, p, self.cols(p)), _block(self.dst, p, self.cols(p)),
                                     self.local_sem)

    def _as_each_device(self, send, local, receive):
        for m in range(N_DEV):
            def branch(m=m):
                for k in range(1, N_DEV):
                    p = (m + k) % N_DEV
                    if self.cols(p) is not None:
                        send(self._remote(p, m))
                if self.cols(m) is not None:
                    if self.local_sem is not None:
                        local(self._local(m))
                    for k in range(1, N_DEV):
                        receive(self._remote(m, (m + k) % N_DEV))

            pl.when(self.me == m)(branch)

    def start(self):
        self._as_each_device(lambda cp: cp.start(), lambda cp: cp.start(), lambda cp: None)

    def finish(self):
        self._as_each_device(lambda cp: cp.wait_send(), lambda cp: cp.wait(), lambda cp: cp.wait_recv())


GRAD_EXCHANGE_SEMS = [pltpu.SemaphoreType.DMA((N_DEV,)), pltpu.SemaphoreType.DMA((N_DEV,)), pltpu.SemaphoreType.DMA]


def _allreduce_small(p_mid, p_conv, p_norm):
    def body(a_ref, b_ref, c_ref, out_ref, mine, gathered, send_sems, recv_sems):
        x, y, c, me = _my_place()
        mine[...] = a_ref[...] + b_ref[...] + c_ref[...]
        gathered[me] = mine[...]
        remote = []
        for k, (peer, _) in enumerate(_peers(x, y, c)):
            cp = pltpu.make_async_remote_copy(
                src_ref=mine, dst_ref=gathered.at[me], send_sem=send_sems.at[k], recv_sem=recv_sems.at[k],
                device_id=peer, device_id_type=MESH)
            cp.start()
            remote.append(cp)
        for cp in remote:
            cp.wait()
        total = gathered[0]
        for s in range(1, N_DEV):
            total = total + gathered[s]
        out_ref[...] = total

    return pl.pallas_call(
        body, name="allreduce_small",
        out_shape=jax.ShapeDtypeStruct((8, D), F32),
        in_specs=[VMEM_SPEC, VMEM_SPEC, VMEM_SPEC], out_specs=VMEM_SPEC,
        scratch_shapes=[pltpu.VMEM((8, D), F32), pltpu.VMEM((N_DEV, 8, D), F32),
                        pltpu.SemaphoreType.DMA((N_DEV - 1,)), pltpu.SemaphoreType.DMA((N_DEV - 1,))],
    )(p_mid, p_conv, p_norm)


def _proj_pieces():
    cuts = sorted(set(range(0, IN_COLS + 1, D)) | set(range(0, IN_COLS + 1, W_IN_SHARD)))
    return [(lo // D, lo % D, lo // W_IN_SHARD, lo % W_IN_SHARD, hi - lo) for lo, hi in zip(cuts[:-1], cuts[1:])]


PROJ_TN = 256


def _proj_cols(u, w_all, seg0, n_seg, dtype, name):
    S = u.shape[0]
    tn = PROJ_TN
    per_shard = W_IN_SHARD // tn
    tile0 = seg0 * D // tn

    def body(u_ref, w_ref, out_ref):
        out_ref[...] = _dot(u_ref[...], w_ref[0]).astype(dtype)

    return pl.pallas_call(
        body, name=name, grid=(n_seg * D // tn,),
        in_specs=[VMEM_SPEC, pl.BlockSpec((1, D, tn), lambda t: ((tile0 + t) // per_shard, 0, (tile0 + t) % per_shard))],
        out_specs=pl.BlockSpec((S, tn), lambda t: (0, t)),
        out_shape=jax.ShapeDtypeStruct((S, n_seg * D), dtype),
        compiler_params=_params(1),
    )(u, w_all)


CONV_TM, CONV_TC = 256, 512
HALO = 16


def _conv_fwd(pa, cw8):
    S = pa.shape[0]
    tm, tc = CONV_TM, CONV_TC
    nct = D // tc

    def seg(s):
        return pl.BlockSpec((tm, tc), lambda i, j, s=s: (i, s * nct + j))

    def halo_before(s):
        return pl.BlockSpec((HALO, tc), lambda i, j, s=s: (jnp.maximum(i * (tm // HALO) - 1, 0), s * nct + j))

    def body(xc, bg, cg, zc, xch, cgh, cw, out):
        i = pl.program_id(0)
        a = cg[...].astype(F32) * xc[...].astype(F32)
        ah = cgh[...].astype(F32) * xch[...].astype(F32)
        ah = jnp.where(i > 0, ah, 0.0)
        row = lax.broadcasted_iota(jnp.int32, (tm, tc), 0)
        a1 = jnp.where(row == 0, ah[HALO - 1:HALO, :], pltpu.roll(a, 1, 0))
        a2 = jnp.where(row == 0, ah[HALO - 2:HALO - 1, :],
                       jnp.where(row == 1, ah[HALO - 1:HALO, :], pltpu.roll(a, 2, 0)))
        w = cw[...]
        conv = w[0:1, :] * a2 + w[1:2, :] * a1 + w[2:3, :] * a
        z = zc[...].astype(F32)
        out[...] = (z * _sigmoid(z) * bg[...].astype(F32) * conv).astype(BF)

    return pl.pallas_call(
        body, name="conv_fwd", grid=(S // tm, nct),
        in_specs=[seg(0), seg(1), seg(2), seg(3), halo_before(0), halo_before(2),
                  pl.BlockSpec((8, tc), lambda i, j: (0, j))],
        out_specs=pl.BlockSpec((tm, tc), lambda i, j: (i, j)),
        out_shape=jax.ShapeDtypeStruct((S, D), BF),
        compiler_params=_params(2),
    )(pa, pa, pa, pa, pa, pa, cw8)


ATT_UNROLL = 32


LAYOUT_MOD = 4
RUN = QB // LAYOUT_MOD


def _fold_masks(d):
    row = lax.broadcasted_iota(jnp.int32, (QB, QB), 0)
    lane = lax.broadcasted_iota(jnp.int32, (QB, QB), 1)
    if d == 1:
        qpos, kpos = LAYOUT_MOD * (row % RUN) + row // RUN, LAYOUT_MOD * (lane % RUN) + lane // RUN
    else:
        qpos, kpos = row, lane
    tri_le = kpos <= qpos
    dist = jnp.where(tri_le, qpos - kpos, qpos - kpos + QB).astype(F32)
    return tri_le, dist, lane < HEAD_DIM


class _Rows:
    def __init__(self, slices):
        self.slices = slices

    def get(self, ref):
        parts = [ref[sl, :] for sl in self.slices]
        return parts[0] if len(parts) == 1 else jnp.concatenate(parts, axis=0)

    def put(self, ref, val):
        size = QB // len(self.slices)
        for g, sl in enumerate(self.slices):
            ref[sl, :] = val if len(self.slices) == 1 else val[g * size:(g + 1) * size]

    def add(self, ref, val):
        self.put(ref, self.get(ref) + val)


def _block_rows(b, d, S):
    quarter = S // LAYOUT_MOD
    nb = S // (QB * d)
    r, n = b // nb, b % nb
    n_prev = jnp.maximum(n - 1, 0)
    if d == 1:
        runs = lambda m: _Rows([pl.ds(pl.multiple_of(g * quarter + RUN * m, RUN), RUN) for g in range(LAYOUT_MOD)])
        return n, runs(n), runs(n_prev)
    if d == LAYOUT_MOD:
        block = lambda m: _Rows([pl.ds(pl.multiple_of(r * quarter + QB * m, QB), QB)])
        return n, block(n), block(n_prev)
    step = d // LAYOUT_MOD
    first = (r % LAYOUT_MOD) * quarter + r // LAYOUT_MOD
    strided = lambda m: _Rows([pl.ds(first + QB * step * m, QB, stride=step)])
    return n, strided(n), strided(n_prev)


def _natural_rows(i, S):
    per = S // LAYOUT_MOD // QB
    return pl.ds(i // per + LAYOUT_MOD * QB * (i % per), QB, stride=LAYOUT_MOD)


def _head_sum_matrix():
    r = lax.broadcasted_iota(jnp.int32, (2 * QB, 2 * QB), 0)
    c = lax.broadcasted_iota(jnp.int32, (2 * QB, 2 * QB), 1)
    return (((r % QB) // HEAD_DIM) == (c // QB)).astype(F32).astype(BF)


def _hi_lo(t):
    hi = t.astype(BF)
    return jnp.concatenate([hi, (t - hi.astype(F32)).astype(BF)], axis=1)


PROJ_ROWS = 512


def _attn_fwd(u, slopes, w_all, w3_all, cw_all):
    S = u.shape[0]
    hpr = HEAD_PAIRS
    n_blocks = S // QB
    later = range(FIRST_PIECES, 3 * hpr)

    def body(sl_ref, u_ref, w_in_ref, w3_in_ref, cw_in_ref, o_ref, lse_ref, q_ref, k_ref, v_ref, w_ref, w3_ref,
             cw_ref, acc, m_s, l_s, w_tile, staged, tile_sems, *sems):
        hp = pl.program_id(0)
        me = _my_place()[3]
        pieces = _PieceGather(lambda p, lo, hi: w_ref.at[p, :, lo:hi], w_ref, *sems[0:2], relay=True)
        gathers = (_WeightGather(lambda p, cols: _block(w_ref, me, cols), w_ref, *sems[2:5], REST_COLS),
                   _WeightGather(lambda p, cols: w3_ref.at[me], w3_ref, *sems[5:8], _whole),
                   _WeightGather(lambda p, cols: cw_ref.at[me], cw_ref, *sems[8:11], _whole))

        @pl.when(hp == 0)
        def _():
            pieces.start(later)
            for g in gathers:
                g.start()

        @pl.when(hp == FORWARD_STEP)
        def _():
            for g in gathers:
                g.forward()

        for h in range(hpr):
            @pl.when(hp == h)
            def _(h=h):
                if 3 * h >= FIRST_PIECES:
                    pieces.wait_recv(range(3 * h, 3 * h + 3))
                if 3 * h + 3 >= FIRST_PIECES and h + 1 < hpr:
                    pieces.forward(range(3 * h + 3, 3 * h + 6))
                fetch = []
                for seg in range(3):
                    p, lo = _qkv_piece(h, seg)
                    fetch.append(pltpu.make_async_copy(w_ref.at[p, :, lo:lo + 128], w_tile.at[:, seg * 128:(seg + 1) * 128],
                                                       tile_sems.at[seg]))
                    fetch[-1].start()
                for cp in fetch:
                    cp.wait()

        def project(i, carry):
            rows = pl.ds(pl.multiple_of(i * PROJ_ROWS, PROJ_ROWS), PROJ_ROWS)
            qkv = _dot(u_ref[rows, :], w_tile[...])
            per = PROJ_ROWS // LAYOUT_MOD
            for seg, ref in enumerate((q_ref, k_ref, v_ref)):
                staged[seg] = qkv[:, seg * 128:(seg + 1) * 128]
                for g in range(LAYOUT_MOD):
                    dst = pl.ds(pl.multiple_of(g * (S // LAYOUT_MOD) + i * per, per), per)
                    ref[dst, :] = staged.at[seg][pl.ds(g, per, stride=LAYOUT_MOD), :]
            return carry

        lax.fori_loop(0, S // PROJ_ROWS, project, 0)

        head_sum = _head_sum_matrix()
        ones_b = jnp.ones((2 * QB, QB), BF)
        m_s[...] = jnp.full(m_s.shape, NEG, F32)
        l_s[...] = jnp.zeros(l_s.shape, F32)
        acc[...] = jnp.zeros(acc.shape, F32)

        for d in DILATIONS:
            tri_le, dist, low = _fold_masks(d)
            low_b = low.astype(F32).astype(BF)
            high_b = 1.0 - low_b
            slope = [sl_ref[2 * hp + a] * float(d) for a in range(2)]
            bias = [slope[a] * dist for a in range(2)]

            def block(b, d=d, slope=slope, bias=bias, tri_le=tri_le, low=low, low_b=low_b, high_b=high_b):
                n, cur, prev = _block_rows(b, d, S)
                has_prev = n > 0
                valid = jnp.logical_or(tri_le, has_prev)
                q2 = (cur.get(q_ref) * 0.125).astype(BF)
                qs = jnp.concatenate([q2 * low_b, q2 * high_b], axis=0)
                vp = prev.get(v_ref)
                kp_b = prev.get(k_ref).astype(BF)
                kcat = jnp.concatenate([kp_b, cur.get(k_ref).astype(BF)], axis=0)
                vcat = jnp.concatenate([vp, cur.get(v_ref)], axis=0).astype(BF)
                s2 = _dot_nt(qs, kcat)
                e2 = _dot(_hi_lo(q2.astype(F32) * kp_b.astype(F32)), head_sum)
                p_rows, alpha_h, pe_h = [], [], []
                for a in range(2):
                    sp, sc = s2[a * QB:(a + 1) * QB, :QB], s2[a * QB:(a + 1) * QB, QB:]
                    comb = jnp.where(valid, jnp.where(tri_le, sc, sp) - bias[a], NEG)
                    e = jnp.where(has_prev, e2[:, a * QB:(a + 1) * QB] - slope[a] * float(QB), NEG)
                    m_old = cur.get(m_s.at[a])
                    m_new = jnp.maximum(jnp.maximum(m_old, jnp.max(comb, axis=-1, keepdims=True)), e)
                    cur.put(m_s.at[a], m_new)
                    p = jnp.exp(comb - m_new)
                    pe_h.append(jnp.exp(e - m_new))
                    alpha_h.append(jnp.exp(m_old - m_new))
                    p_rows.append(jnp.concatenate([jnp.where(tri_le, 0.0, p).astype(BF),
                                                   jnp.where(tri_le, p, 0.0).astype(BF)], axis=1))
                pv = _dot(jnp.concatenate(p_rows, axis=0), jnp.concatenate([vcat, ones_b], axis=1))
                for a in range(2):
                    cur.put(l_s.at[a], alpha_h[a] * cur.get(l_s.at[a]) + pv[a * QB:(a + 1) * QB, QB:] + pe_h[a])
                cur.put(acc, jnp.where(low, alpha_h[0], alpha_h[1]) * cur.get(acc)
                        + jnp.where(low, pv[:QB, :QB], pv[QB:, :QB]) + jnp.where(low, pe_h[0], pe_h[1]) * vp)

            def several(it, carry, block=block):
                for u in range(ATT_UNROLL):
                    block(it * ATT_UNROLL + u)
                return carry

            lax.fori_loop(0, n_blocks // ATT_UNROLL, several, 0)

        low = _fold_masks(LAYOUT_MOD)[2]

        def finish(i, carry):
            rows = pl.ds(pl.multiple_of(i * QB, QB), QB)
            l0, l1 = l_s[0, rows, :], l_s[1, rows, :]
            o_ref[_natural_rows(i, S), :] = acc[rows, :] / jnp.where(low, l0, l1)
            lse_ref[0, rows, :] = m_s[0, rows, :] + jnp.log(l0)
            lse_ref[1, rows, :] = m_s[1, rows, :] + jnp.log(l1)
            return carry

        lax.fori_loop(0, n_blocks, finish, 0)

        @pl.when(hp == hpr - 1)
        def _():
            pieces.wait_send(later)
            for g in gathers:
                g.finish()

    col = pl.BlockSpec((S, 128), lambda h: (0, h))
    act = jax.ShapeDtypeStruct((S, D), F32)
    gathered = (w_all, w3_all, cw_all)
    return pl.pallas_call(
        body, name="attn_fwd", grid=(hpr,),
        in_specs=[SMEM_SPEC, VMEM_SPEC, ANY_SPEC, ANY_SPEC, ANY_SPEC],
        out_specs=(col, pl.BlockSpec((2, S, 128), lambda h: (0, 0, h)), col, col, col, ANY_SPEC, ANY_SPEC, ANY_SPEC),
        out_shape=(act, jax.ShapeDtypeStruct((2, S, D), F32), act, act, act,
                   *[jax.ShapeDtypeStruct(t.shape, t.dtype) for t in gathered]),
        scratch_shapes=([pltpu.VMEM((S, 128), F32), pltpu.VMEM((2, S, 128), F32), pltpu.VMEM((2, S, 128), F32),
                         pltpu.VMEM((D, 3 * 128), BF), pltpu.VMEM((3, PROJ_ROWS, 128), F32),
                         pltpu.SemaphoreType.DMA((3,))]
                        + _piece_sems(3 * hpr) + WEIGHT_GATHER_SEMS * 3),
        input_output_aliases={2: 5, 3: 6, 4: 7},
        compiler_params=_params(1),
    )(slopes, u, *gathered)


def _set_rows(shape, rows):
    idx = lax.broadcasted_iota(jnp.int32, shape, 0)
    out = jnp.zeros(shape, F32)
    for r, val in rows.items():
        out = out + jnp.where(idx == r, val, 0.0)
    return out


def _mid(yc_in, pa_mid, o, x2, target, b_merge, final_g, w3):
    S = x2.shape[0]
    tm = ROW_TILE
    nsteps = S // tm
    tile = pl.BlockSpec((tm, D), lambda i: (i, 0))

    def body(yc_ref, za_ref, gcp_ref, gap_ref, o_ref, x_ref, t_ref, b_ref, fg_ref, w_ref,
             dh_ref, dmid_ref, do_ref, dyc_ref, gw_ref, small_ref, acc, stage):
        i = pl.program_id(0)

        @pl.when(i == 0)
        def _():
            acc[...] = jnp.zeros_like(acc)
            small_ref[...] = jnp.zeros_like(small_ref)

        wc, wa, wo = w_ref[0], w_ref[1], w_ref[2]
        z = za_ref[...].astype(F32)
        sg = _sigmoid(z)
        ov = o_ref[...]
        yc_in_b, ya_in_b = yc_ref[...], (z * sg * ov).astype(BF)
        yc = _dot(yc_in_b, wc)
        ya = _dot(ya_in_b, wa)
        b = b_ref[...]
        gc = _sigmoid(gcp_ref[...].astype(F32) + b[:, :D])
        ga = _sigmoid(gap_ref[...].astype(F32) + b[:, D:])
        merged = gc * yc + ga * ya
        merged_b = merged.astype(BF)
        h = x_ref[...] + _dot(merged_b, wo)
        r2 = lax.rsqrt(jnp.mean(h * h, axis=-1, keepdims=True) + EPS)
        n = h * r2
        fg = fg_ref[...]
        err = n * fg - t_ref[...]
        loss = 0.5 * jnp.sum(jnp.sum(err * err, axis=-1, keepdims=True) / D, axis=0, keepdims=True)
        dy = err / D
        g_fg = jnp.sum(dy * n, axis=0, keepdims=True)
        dn = dy * fg
        dh = r2 * (dn - n * jnp.mean(dn * n, axis=-1, keepdims=True))
        dh_ref[...] = dh
        dh_b = dh.astype(BF)
        dmerged = _dot_nt(dh_b, wo)
        acc[2] += _dot(merged.T.astype(BF), dh_b)
        dyc = (dmerged * gc).astype(BF)
        dya = (dmerged * ga).astype(BF)
        dgcp = dmerged * yc * gc * (1.0 - gc)
        dgap = dmerged * ya * ga * (1.0 - ga)
        dmid_ref[1] = dgcp.astype(BF)
        dmid_ref[2] = dgap.astype(BF)
        acc[0] += _dot(yc_in_b.astype(F32).T.astype(BF), dyc)
        acc[1] += _dot(ya_in_b.astype(F32).T.astype(BF), dya)
        dyc_ref[...] = _dot_nt(dyc, wc).astype(BF)
        dya_in = _dot_nt(dya, wa)
        do_ref[...] = dya_in * (z * sg)
        dmid_ref[0] = (dya_in * ov * (sg * (1.0 + z * (1.0 - sg)))).astype(BF)
        small_ref[...] += _set_rows((8, D), {
            1: jnp.sum(dgcp, axis=0, keepdims=True), 2: jnp.sum(dgap, axis=0, keepdims=True),
            3: g_fg, 7: jnp.broadcast_to(loss, (1, D))})

        @pl.when(i == nsteps - 1)
        def _():
            for p in range(N_DEV):
                for a in range(3):
                    stage[...] = acc[a, p * ROW_SHARD:(p + 1) * ROW_SHARD, :].astype(BF)
                    pltpu.sync_copy(stage, gw_ref.at[p, a])

    return pl.pallas_call(
        body, name="mid", grid=(nsteps,),
        in_specs=[tile, pl.BlockSpec((tm, D), lambda i: (i, 0)), pl.BlockSpec((tm, D), lambda i: (i, 1)),
                  pl.BlockSpec((tm, D), lambda i: (i, 2)), tile, tile, tile,
                  pl.BlockSpec((1, 2 * D), lambda i: (0, 0)), pl.BlockSpec((1, D), lambda i: (0, 0)), VMEM_SPEC],
        out_specs=(tile, pl.BlockSpec((3, tm, D), lambda i: (0, i, 0)), tile, tile,
                   ANY_SPEC, pl.BlockSpec((8, D), lambda i: (0, 0))),
        out_shape=(jax.ShapeDtypeStruct((S, D), F32), jax.ShapeDtypeStruct((3, S, D), BF),
                   jax.ShapeDtypeStruct((S, D), F32), jax.ShapeDtypeStruct((S, D), BF),
                   jax.ShapeDtypeStruct((N_DEV, 3, ROW_SHARD, D), BF), jax.ShapeDtypeStruct((8, D), F32)),
        scratch_shapes=[pltpu.VMEM((3, D, D), F32), pltpu.VMEM((ROW_SHARD, D), BF)],
        compiler_params=_params(1),
    )(yc_in, pa_mid, pa_mid, pa_mid, o, x2, target, b_merge, final_g, w3)


def _conv_bwd(dyc_in, pa, cw8):
    S = pa.shape[0]
    tm, tc = CONV_TM, CONV_TC
    nct = D // tc
    nrt = S // tm
    last_halo = S // HALO - 1

    def seg(s):
        return pl.BlockSpec((tm, tc), lambda j, i, s=s: (i, s * nct + j))

    def halo_before(s):
        return pl.BlockSpec((HALO, tc), lambda j, i, s=s: (jnp.maximum(i * (tm // HALO) - 1, 0), s * nct + j))

    def halo_after(s):
        return pl.BlockSpec((HALO, tc), lambda j, i, s=s: (jnp.minimum((i + 1) * (tm // HALO), last_halo), s * nct + j))

    def body(dy, xc, bg, cg, zc, xch, cgh, dyn, bgn, zcn, cw, dout, gcw):
        i = pl.program_id(1)

        @pl.when(i == 0)
        def _():
            gcw[...] = jnp.zeros_like(gcw)

        xcv, cgv = xc[...].astype(F32), cg[...].astype(F32)
        a = cgv * xcv
        ah = jnp.where(i > 0, cgh[...].astype(F32) * xch[...].astype(F32), 0.0)
        row = lax.broadcasted_iota(jnp.int32, (tm, tc), 0)
        a1 = jnp.where(row == 0, ah[HALO - 1:HALO, :], pltpu.roll(a, 1, 0))
        a2 = jnp.where(row == 0, ah[HALO - 2:HALO - 1, :],
                       jnp.where(row == 1, ah[HALO - 1:HALO, :], pltpu.roll(a, 2, 0)))
        w = cw[...]
        conv = w[0:1, :] * a2 + w[1:2, :] * a1 + w[2:3, :] * a
        z = zc[...].astype(F32)
        sg = _sigmoid(z)
        silu = z * sg
        bgv = bg[...].astype(F32)
        dyv = dy[...].astype(F32)
        dout[3] = (dyv * bgv * conv * (sg * (1.0 + z * (1.0 - sg)))).astype(BF)
        dout[1] = (dyv * silu * conv).astype(BF)
        dc = dyv * silu * bgv
        zn = zcn[...].astype(F32)
        dcn = dyn[...].astype(F32) * (zn * _sigmoid(zn)) * bgn[...].astype(F32)
        dcn = jnp.where(i < nrt - 1, dcn, 0.0)
        dc1 = jnp.where(row == tm - 1, dcn[0:1, :], pltpu.roll(dc, tm - 1, 0))
        dc2 = jnp.where(row == tm - 1, dcn[1:2, :],
                        jnp.where(row == tm - 2, dcn[0:1, :], pltpu.roll(dc, tm - 2, 0)))
        da = w[2:3, :] * dc + w[1:2, :] * dc1 + w[0:1, :] * dc2
        dout[2] = (da * xcv).astype(BF)
        dout[0] = (da * cgv).astype(BF)
        gcw[...] += _set_rows((8, tc), {
            4: jnp.sum(dc * a2, axis=0, keepdims=True), 5: jnp.sum(dc * a1, axis=0, keepdims=True),
            6: jnp.sum(dc * a, axis=0, keepdims=True)})

    return pl.pallas_call(
        body, name="conv_bwd", grid=(nct, nrt),
        in_specs=[pl.BlockSpec((tm, tc), lambda j, i: (i, j)), seg(0), seg(1), seg(2), seg(3),
                  halo_before(0), halo_before(2),
                  pl.BlockSpec((HALO, tc), lambda j, i: (jnp.minimum((i + 1) * (tm // HALO), last_halo), j)),
                  halo_after(1), halo_after(3), pl.BlockSpec((8, tc), lambda j, i: (0, j))],
        out_specs=(pl.BlockSpec((4, tm, tc), lambda j, i: (0, i, j)), pl.BlockSpec((8, tc), lambda j, i: (0, j))),
        out_shape=(jax.ShapeDtypeStruct((4, S, D), BF), jax.ShapeDtypeStruct((8, D), F32)),
        compiler_params=_params(2),
    )(dyc_in, pa, pa, pa, pa, pa, pa, dyc_in, pa, pa, cw8)


def _attn_bwd(q, k, v, slopes, do, o, lse, g_in, g_3):
    S = q.shape[0]
    hpr = HEAD_PAIRS
    n_blocks = S // QB

    def body(sl_ref, q_ref, k_ref, v_ref, do_ref, o_ref, lse_ref, gin_ref, g3_ref, out_ref, rin_ref, r3_ref,
             dq_s, dk_s, dv_s, do_s, dd_s, *sems):
        hp = pl.program_id(0)
        exchanges = (_GradExchange(gin_ref, rin_ref, *sems[:3], _shard_cols((0, SEG0_ATTN * D), (SEG0_MID * D, IN_COLS))),
                     _GradExchange(g3_ref, r3_ref, *sems[3:], _whole))

        @pl.when(hp == 0)
        def _():
            for ex in exchanges:
                ex.start()

        head_sum = _head_sum_matrix()
        dq_s[...] = jnp.zeros(dq_s.shape, F32)
        dk_s[...] = jnp.zeros(dk_s.shape, F32)
        dv_s[...] = jnp.zeros(dv_s.shape, F32)

        def row_dots(i, carry):
            rows = pl.ds(pl.multiple_of(i * QB, QB), QB)
            natural = _natural_rows(i, S)
            do_c = do_ref[natural, :]
            do_s[rows, :] = do_c
            dd = _dot(_hi_lo(do_c * o_ref[natural, :]), head_sum)
            dd_s[0, rows, :] = dd[:, :QB]
            dd_s[1, rows, :] = dd[:, QB:]
            return carry

        lax.fori_loop(0, n_blocks, row_dots, 0)

        for d in DILATIONS:
            tri_le, dist, low = _fold_masks(d)
            low_b = low.astype(F32).astype(BF)
            high_b = 1.0 - low_b
            slope = [sl_ref[2 * hp + a] * float(d) for a in range(2)]
            bias = [slope[a] * dist for a in range(2)]

            def block(b, d=d, slope=slope, bias=bias, tri_le=tri_le, low=low, low_b=low_b, high_b=high_b):
                n, cur, prev = _block_rows(b, d, S)
                has_prev = n > 0
                valid = jnp.logical_or(tri_le, has_prev)
                q2f = cur.get(q_ref) * 0.125
                q2 = q2f.astype(BF)
                qs = jnp.concatenate([q2 * low_b, q2 * high_b], axis=0)
                kp, vp = prev.get(k_ref), prev.get(v_ref)
                kp_b, vp_b = kp.astype(BF), vp.astype(BF)
                kcat = jnp.concatenate([kp_b, cur.get(k_ref).astype(BF)], axis=0)
                vcat = jnp.concatenate([vp_b, cur.get(v_ref).astype(BF)], axis=0)
                do2f = cur.get(do_s)
                do2 = do2f.astype(BF)
                dos = jnp.concatenate([do2 * low_b, do2 * high_b], axis=0)
                s2 = _dot_nt(qs, kcat)
                dp2 = _dot_nt(dos, vcat)
                diag2 = _dot(jnp.concatenate([_hi_lo(q2.astype(F32) * kp_b.astype(F32)),
                                              _hi_lo(do2.astype(F32) * vp_b.astype(F32))], axis=0), head_sum)
                p_rows, ds_rows, pe_h, dse_h = [], [], [], []
                for a in range(2):
                    hs = slice(a * QB, (a + 1) * QB)
                    sp, sc = s2[hs, :QB], s2[hs, QB:]
                    dpp, dpc = dp2[hs, :QB], dp2[hs, QB:]
                    lse_a, dd_a = cur.get(lse_ref.at[a]), cur.get(dd_s.at[a])
                    comb = jnp.where(tri_le, sc, sp) - bias[a]
                    e = diag2[:QB, hs] - slope[a] * float(QB)
                    p = jnp.where(valid, jnp.exp(comb - lse_a), 0.0)
                    pe = jnp.where(has_prev, jnp.exp(e - lse_a), 0.0)
                    ds = p * (jnp.where(tri_le, dpc, dpp) - dd_a)
                    dse_h.append(pe * (diag2[QB:, hs] - dd_a))
                    pe_h.append(pe)
                    p_rows.append(jnp.concatenate([jnp.where(tri_le, 0.0, p).astype(BF),
                                                   jnp.where(tri_le, p, 0.0).astype(BF)], axis=1))
                    ds_rows.append(jnp.concatenate([jnp.where(tri_le, 0.0, ds).astype(BF),
                                                    jnp.where(tri_le, ds, 0.0).astype(BF)], axis=1))
                pst = jnp.concatenate(p_rows, axis=0)
                dst = jnp.concatenate(ds_rows, axis=0)
                pe2 = jnp.where(low, pe_h[0], pe_h[1])
                dse2 = jnp.where(low, dse_h[0], dse_h[1])
                dq = _dot(dst, kcat)
                cur.add(dq_s, (jnp.where(low, dq[:QB], dq[QB:]) + dse2 * kp) * 0.125)
                dk = _dot_tn(dst, qs)
                dv = _dot_tn(pst, dos)
                prev.add(dk_s, dk[:QB] + dse2 * q2f)
                cur.add(dk_s, dk[QB:])
                prev.add(dv_s, dv[:QB] + pe2 * do2f)
                cur.add(dv_s, dv[QB:])

            def several(it, carry, block=block):
                for u in range(ATT_UNROLL):
                    block(it * ATT_UNROLL + u)
                return carry

            lax.fori_loop(0, n_blocks // ATT_UNROLL, several, 0)

        def finish(i, carry):
            rows = pl.ds(pl.multiple_of(i * QB, QB), QB)
            natural = _natural_rows(i, S)
            for t, ref in enumerate((dq_s, dk_s, dv_s)):
                out_ref.at[t][natural, :] = ref[rows, :]
            return carry

        lax.fori_loop(0, n_blocks, finish, 0)

        @pl.when(hp == hpr - 1)
        def _():
            for ex in exchanges:
                ex.finish()

    col = pl.BlockSpec((S, 128), lambda h: (0, h))
    return pl.pallas_call(
        body, name="attn_bwd", grid=(hpr,),
        in_specs=[SMEM_SPEC, col, col, col, col, col, pl.BlockSpec((2, S, 128), lambda h: (0, 0, h)),
                  ANY_SPEC, ANY_SPEC],
        out_specs=(pl.BlockSpec((3, S, 128), lambda h: (0, 0, h)), ANY_SPEC, ANY_SPEC),
        out_shape=(jax.ShapeDtypeStruct((3, S, D), F32), jax.ShapeDtypeStruct(g_in.shape, BF),
                   jax.ShapeDtypeStruct(g_3.shape, BF)),
        scratch_shapes=([pltpu.VMEM((S, 128), F32)] * 4 + [pltpu.VMEM((2, S, 128), F32)]
                        + GRAD_EXCHANGE_SEMS + GRAD_EXCHANGE_SEMS),
        compiler_params=_params(1),
    )(slopes, q, k, v, do, o, lse, g_in, g_3)


WG_TN = 256
SEG0_CONV, SEG0_ATTN, SEG0_MID = 0, 4, 7


def _wgrad_in(ut, d_group, seg0, g_in, name):
    S = ut.shape[1]
    tn = WG_TN
    per_seg = D // tn
    per_shard = W_IN_SHARD // tn
    n_tiles = d_group.shape[0] * per_seg
    tile0 = seg0 * per_seg

    def body(ut_ref, d_ref, *rest):
        rest[-1][0] = _dot(ut_ref[...], d_ref[0].astype(BF)).astype(BF)

    operands, in_specs, aliases = [ut, d_group], [VMEM_SPEC, pl.BlockSpec((1, S, tn), lambda t: (t // per_seg, 0, t % per_seg))], {}
    if g_in is not None:
        operands.append(g_in)
        in_specs.append(ANY_SPEC)
        aliases = {2: 0}
    return pl.pallas_call(
        body, name=name, grid=(n_tiles,), in_specs=in_specs,
        out_specs=pl.BlockSpec((1, D, tn), lambda t: ((tile0 + t) // per_shard, 0, (tile0 + t) % per_shard)),
        out_shape=jax.ShapeDtypeStruct((N_DEV, D, W_IN_SHARD), BF),
        input_output_aliases=aliases,
        compiler_params=_params(1),
    )(*operands)


def _dgrad_norm_bwd(d_conv, d_attn, d_mid, w_all, x2, dh, norm_g):
    S = x2.shape[0]
    tm = ROW_TILE
    nsteps = S // tm
    tile = pl.BlockSpec((tm, D), lambda i: (i, 0))
    pieces = _proj_pieces()

    def body(a_ref, b_ref, c_ref, w_ref, x_ref, dh_ref, g_ref, gx_ref, small_ref):
        i = pl.program_id(0)

        @pl.when(i == 0)
        def _():
            small_ref[...] = jnp.zeros_like(small_ref)

        groups = (a_ref, b_ref, c_ref)
        du = jnp.zeros((tm, D), F32)
        for s, sc, p, pc, width in pieces:
            g = 0 if s < 4 else (1 if s < 7 else 2)
            local = s - (0, 4, 7)[g]
            du = du + _dot_nt(groups[g][local, :, sc:sc + width].astype(BF), w_ref[p, :, pc:pc + width])
        xv = x_ref[...]
        r = lax.rsqrt(jnp.mean(xv * xv, axis=-1, keepdims=True) + EPS)
        n = xv * r
        dn = du * g_ref[...]
        gx_ref[...] = dh_ref[...] + r * (dn - n * jnp.mean(dn * n, axis=-1, keepdims=True))
        small_ref[...] += _set_rows((8, D), {0: jnp.sum(du * n, axis=0, keepdims=True)})

    return pl.pallas_call(
        body, name="dgrad_norm_bwd", grid=(nsteps,),
        in_specs=[pl.BlockSpec((4, tm, D), lambda i: (0, i, 0)), pl.BlockSpec((3, tm, D), lambda i: (0, i, 0)),
                  pl.BlockSpec((3, tm, D), lambda i: (0, i, 0)), VMEM_SPEC, tile, tile,
                  pl.BlockSpec((1, D), lambda i: (0, 0))],
        out_specs=(tile, pl.BlockSpec((8, D), lambda i: (0, 0))),
        out_shape=(jax.ShapeDtypeStruct((S, D), F32), jax.ShapeDtypeStruct((8, D), F32)),
        compiler_params=_params(1),
    )(d_conv, d_attn, d_mid, w_all, x2, dh, norm_g)


HBM_SPEC = pl.BlockSpec(memory_space=pltpu.HBM)
SEM_SPEC = pl.BlockSpec(memory_space=pltpu.SEMAPHORE)
ATTN_COLS = _shard_cols((SEG0_ATTN * D, SEG0_MID * D))


def _attn_cols_exchange_start(g_in, r_in):
    def body(g_ref, r_ref, send_sems, recv_sems, g_thru, r_thru, token):
        _GradExchange(g_ref, r_ref, send_sems, recv_sems, None, ATTN_COLS).start()
        token[...] = jnp.zeros_like(token)

    hbm = pltpu.with_memory_space_constraint
    return pl.pallas_call(
        body, name="attn_cols_exchange_start",
        out_shape=(pltpu.SemaphoreType.DMA((N_DEV,)), pltpu.SemaphoreType.DMA((N_DEV,)),
                   pltpu.HBM(g_in.shape, g_in.dtype), pltpu.HBM(r_in.shape, r_in.dtype),
                   jax.ShapeDtypeStruct((8, 128), F32)),
        in_specs=(HBM_SPEC, HBM_SPEC), out_specs=(SEM_SPEC, SEM_SPEC, HBM_SPEC, HBM_SPEC, VMEM_SPEC),
        input_output_aliases={0: 2, 1: 3},
        compiler_params=pltpu.CompilerParams(has_side_effects=pltpu.SideEffectType.DATAFLOW_SIDE_EFFECTING),
    )(hbm(g_in, pltpu.HBM), hbm(r_in, pltpu.HBM))


def _attn_cols_exchange_wait(send_sems, recv_sems, g_thru, r_thru, after):
    def body(g_ref, r_ref, send_sems, recv_sems, after_ref, g_dead, r_out):
        _GradExchange(g_ref, r_ref, send_sems, recv_sems, None, ATTN_COLS).finish()

    return pl.pallas_call(
        body, name="attn_cols_exchange_wait",
        out_shape=(pltpu.HBM(g_thru.shape, g_thru.dtype), pltpu.HBM(r_thru.shape, r_thru.dtype)),
        in_specs=(HBM_SPEC, HBM_SPEC, SEM_SPEC, SEM_SPEC, ANY_SPEC), out_specs=(HBM_SPEC, HBM_SPEC),
        input_output_aliases={0: 0, 1: 1},
        compiler_params=pltpu.CompilerParams(has_side_effects=pltpu.SideEffectType.DATAFLOW_SIDE_EFFECTING),
    )(g_thru, r_thru, send_sems, recv_sems, after)


def _adamw_math(w, g, m, v):
    m = ADAM_B1 * m + (1.0 - ADAM_B1) * g
    v = ADAM_B2 * v + (1.0 - ADAM_B2) * (g * g)
    m_hat = m / (1.0 - ADAM_B1 ** ADAM_STEP)
    v_hat = v / (1.0 - ADAM_B2 ** ADAM_STEP)
    delta = -ADAM_LR * (m_hat / (jnp.sqrt(v_hat) + ADAM_EPS) + ADAM_WD * w)
    return delta, m, v


def _sum_adamw(parts, w, m, v, tm, name):
    R, C = w.shape
    tile = pl.BlockSpec((tm, C), lambda i: (i, 0))

    def body(p_ref, w_ref, m_ref, v_ref, g_out, d_out, m_out, v_out):
        g = p_ref[0].astype(F32)
        for s in range(1, N_DEV):
            g = g + p_ref[s].astype(F32)
        g_out[...] = g
        d_out[...], m_out[...], v_out[...] = _adamw_math(w_ref[...], g, m_ref[...], v_ref[...])

    shape = jax.ShapeDtypeStruct((R, C), F32)
    return pl.pallas_call(
        body, name=name, grid=(R // tm,),
        in_specs=[pl.BlockSpec((N_DEV, tm, C), lambda i: (0, i, 0)), tile, tile, tile],
        out_specs=(tile, tile, tile, tile), out_shape=(shape, shape, shape, shape),
        compiler_params=_params(1),
    )(parts, w, m, v)


def _adamw(g, w, m, v, name):
    def body(g_ref, w_ref, m_ref, v_ref, d_out, m_out, v_out):
        d_out[...], m_out[...], v_out[...] = _adamw_math(w_ref[...], g_ref[...], m_ref[...], v_ref[...])

    shape = jax.ShapeDtypeStruct(w.shape, F32)
    return pl.pallas_call(
        body, name=name, in_specs=[VMEM_SPEC] * 4, out_specs=(VMEM_SPEC,) * 3, out_shape=(shape, shape, shape),
    )(g, w, m, v)


def _alibi_slopes():
    return jnp.exp2(-8.0 * jnp.arange(1, N_HEADS + 1, dtype=F32) / N_HEADS)


def _local_step(x2, target, norm_g, b_merge, final_g, w_in, w3_shard, cw_shard):
    slopes = _alibi_slopes()
    u, ut, w_all, w3_all, cw_all = _norm_gather_first_weights(x2, norm_g, w_in, w3_shard, cw_shard)
    o, lse, q, k, v, w_all, w3_all, cw_all = _attn_fwd(u, slopes, w_all, w3_all, cw_all)
    w3 = jnp.transpose(w3_all, (1, 0, 2, 3)).reshape(3, D, D)
    cw8 = jnp.transpose(cw_all, (1, 0, 2)).reshape(8, D)
    pa = _proj_cols(u, w_all, SEG0_CONV, 4, BF, "proj_conv")
    yc_in = _conv_fwd(pa, cw8)
    pa_mid = _proj_cols(u, w_all, SEG0_MID, 3, BF, "proj_mid")
    dh, d_mid, do, dyc_in, g_3, small_mid = _mid(yc_in, pa_mid, o, x2, target, b_merge, final_g, w3)
    g_in = _wgrad_in(ut, d_mid, SEG0_MID, None, "wgrad_in_mid")
    d_conv, small_conv = _conv_bwd(dyc_in, pa, cw8)
    g_in = _wgrad_in(ut, d_conv, SEG0_CONV, g_in, "wgrad_in_conv")
    d_attn, r_in, r_3 = _attn_bwd(q, k, v, slopes, do, o, lse, g_in, g_3)
    g_in = _wgrad_in(ut, d_attn, SEG0_ATTN, g_in, "wgrad_in_attn")
    *in_flight, token = _attn_cols_exchange_start(g_in, r_in)
    grad_x, small_norm = _dgrad_norm_bwd(d_conv, d_attn, d_mid, w_all, x2, dh, norm_g + token[0:1, 0:1])
    return grad_x, in_flight, r_3, small_mid, small_conv, small_norm


def kernel(x, norm_g, w_in, b_merge, conv_w, w_out_conv, w_out_attn, w_o, final_g, loss_target, m_norm_g, m_w_in, m_b_merge, m_conv_w, m_w_out_conv, m_w_out_attn, m_w_o, m_final_g, v_norm_g, v_w_in, v_b_merge, v_conv_w, v_w_out_conv, v_w_out_attn, v_w_o, v_final_g):
    me = 4 * lax.axis_index("x") + 2 * lax.axis_index("y") + lax.axis_index("c")
    stack3 = lambda a, b, c: jnp.concatenate([a, b, c], axis=0)
    pad8 = lambda a: jnp.pad(a, ((0, 8 - a.shape[0]), (0, 0)))

    w3_shard = stack3(w_out_conv, w_out_attn, w_o)
    final_g2 = final_g.reshape(1, D)
    grad_x, in_flight, r_3, small_mid, small_conv, small_norm = _local_step(
        x[0], loss_target[0], norm_g, b_merge, final_g2, w_in[0], w3_shard, pad8(conv_w[0]))

    small = _allreduce_small(small_mid, small_conv, small_norm)
    g_in, r_in = _attn_cols_exchange_wait(*in_flight, small)
    own = lax.dynamic_index_in_dim(g_in, me, 0, keepdims=True)
    r_in = lax.dynamic_update_slice(r_in, own, (me, 0, 0))

    g_w_in, d_w_in, nm_w_in, nv_w_in = _sum_adamw(r_in, w_in[0], m_w_in[0], v_w_in[0], 128, "adamw_w_in")
    g_w3, d_w3, nm_w3, nv_w3 = _sum_adamw(
        r_3.reshape(N_DEV, 3 * ROW_SHARD, D), w3_shard.reshape(3 * ROW_SHARD, D),
        stack3(m_w_out_conv, m_w_out_attn, m_w_o).reshape(3 * ROW_SHARD, D),
        stack3(v_w_out_conv, v_w_out_attn, v_w_o).reshape(3 * ROW_SHARD, D), ROW_SHARD, "adamw_w3")

    def pack(ng, bm, fg):
        return pad8(jnp.concatenate([ng, bm.reshape(2, D), fg.reshape(1, D)], axis=0))

    d_s, nm_s, nv_s = _adamw(small, pack(norm_g, b_merge, final_g), pack(m_norm_g, m_b_merge, m_final_g),
                             pack(v_norm_g, v_b_merge, v_final_g), "adamw_small")
    g_cw = lax.dynamic_slice(small, (4, me * ROW_SHARD), (3, ROW_SHARD))
    d_cw, nm_cw, nv_cw = _adamw(g_cw, conv_w[0], m_conv_w[0], v_conv_w[0], "adamw_conv_w")

    loss = small[7, 0]
    split3 = lambda t: tuple(t[a * ROW_SHARD:(a + 1) * ROW_SHARD][None] for a in range(3))
    unpack = lambda t: (t[0:1], t[1:3].reshape(1, 2 * D), t[3])

    def leaves(in_, small_, cw_, w3_):
        ng, bm, fg = unpack(small_)
        wc, wa, wo = split3(w3_)
        return (ng, in_[None], bm, cw_[None], wc, wa, wo, fg)

    return (loss, grad_x[None],
            *leaves(g_w_in, small, g_cw, g_w3),
            *leaves(d_w_in, d_s, d_cw, d_w3),
            *leaves(nm_w_in, nm_s, nm_cw, nm_w3),
            *leaves(nv_w_in, nv_s, nv_cw, nv_w3))
```

```python
import functools

import jax
import jax.numpy as jnp
from jax import lax
from jax.experimental import pallas as pl
from jax.experimental.pallas import tpu as pltpu

D = 1024
N_HEADS = 16
HEAD_DIM = 64
N_SEG = 10
IN_COLS = N_SEG * D
N_DEV = 8
W_IN_SHARD = IN_COLS // N_DEV
ROW_SHARD = D // N_DEV
QB = 128
DILATIONS = (1, 4, 16)
EPS = 1e-6
NEG = -1e30
BF = jnp.bfloat16
F32 = jnp.float32
MESH = pl.DeviceIdType.MESH

ADAM_LR = 0.001
ADAM_B1 = 0.9
ADAM_B2 = 0.999
ADAM_EPS = 1e-08
ADAM_WD = 0.01
ADAM_STEP = 10

V7X_VMEM_BYTES = 64 * 1024 * 1024
VMEM_LIMIT = V7X_VMEM_BYTES - 8 * 1024 * 1024
ROW_TILE = 256

VMEM_SPEC = pl.BlockSpec(memory_space=pltpu.VMEM)
ANY_SPEC = pl.BlockSpec(memory_space=pl.ANY)
SMEM_SPEC = pl.BlockSpec(memory_space=pltpu.SMEM)


def _params(n_grid_axes, vmem=VMEM_LIMIT):
    return pltpu.CompilerParams(dimension_semantics=("arbitrary",) * n_grid_axes, vmem_limit_bytes=vmem)


def _dot(a, b):
    return jnp.dot(a, b, preferred_element_type=F32)


def _dot_nt(a, b):
    return lax.dot_general(a, b, (((1,), (1,)), ((), ())), preferred_element_type=F32)


def _dot_tn(a, b):
    return lax.dot_general(a, b, (((0,), (0,)), ((), ())), preferred_element_type=F32)


def _sigmoid(z):
    return 1.0 / (1.0 + jnp.exp(-z))


def _my_place():
    x, y, c = lax.axis_index("x"), lax.axis_index("y"), lax.axis_index("c")
    return x, y, c, 4 * x + 2 * y + c


def _peers(x, y, c):
    out = []
    for k in range(1, N_DEV):
        px = 1 - x if k & 4 else x
        py = 1 - y if k & 2 else y
        pc = 1 - c if k & 1 else c
        out.append(((px, py, pc), 4 * px + 2 * py + pc))
    return out


def _device(p):
    return (p >> 2, (p >> 1) & 1, p & 1)


def _shard_cols(*ranges):
    def cols(p):
        found = None
        for lo, hi in ranges:
            a, b = max(lo, p * W_IN_SHARD), min(hi, (p + 1) * W_IN_SHARD)
            if a < b:
                assert found is None
                found = (a - p * W_IN_SHARD, b - p * W_IN_SHARD)
        return found

    return cols


def _whole(p):
    return ()


def _block(ref, idx, cols):
    return ref.at[idx] if cols == () else ref.at[idx, :, cols[0]:cols[1]]


class _WeightGather:
    def __init__(self, src, dst, send_sems, forward_sems, recv_sems, cols):
        self.src, self.dst, self.cols = src, dst, cols
        self.send_sems, self.forward_sems, self.recv_sems = send_sems, forward_sems, recv_sems
        self.me = _my_place()[3]

    def _copy(self, p, target, passing_on=False):
        cols = self.cols(p)
        return pltpu.make_async_remote_copy(
            src_ref=_block(self.dst, p, cols) if passing_on else self.src(p, cols), dst_ref=_block(self.dst, p, cols),
            send_sem=self.forward_sems.at[p] if passing_on else self.send_sems.at[target],
            recv_sem=self.recv_sems.at[p], device_id=_device(target), device_id_type=MESH)

    def _as_each_device(self, own, relayed, other):
        for m in range(N_DEV):
            def branch(m=m):
                for p in range(N_DEV):
                    if self.cols(p) is None:
                        continue
                    if p == m:
                        for t in [m ^ 1] + [q for q in range(N_DEV) if q >> 1 != m >> 1 and q & 1 == m & 1]:
                            own(self._copy(m, t))
                    elif p >> 1 != m >> 1 and p & 1 == m & 1:
                        relayed(p, m ^ 1)
                    else:
                        other(p)

            pl.when(self.me == m)(branch)

    def start(self):
        self._as_each_device(lambda cp: cp.start(), lambda p, t: None, lambda p: None)

    def forward(self):
        def pass_on(p, t):
            self._copy(p, p).wait_recv()
            self._copy(p, t, passing_on=True).start()

        self._as_each_device(lambda cp: None, pass_on, lambda p: None)

    def finish(self):
        self._as_each_device(lambda cp: cp.wait_send(), lambda p, t: self._copy(p, t, passing_on=True).wait_send(),
                             lambda p: self._copy(p, p).wait_recv())


WEIGHT_GATHER_SEMS = [pltpu.SemaphoreType.DMA((N_DEV,))] * 3
FORWARD_STEP = 6
REST_COLS = _shard_cols((0, 4 * D), (7 * D, IN_COLS))
HEAD_PAIRS = D // 128


def _qkv_piece(h, seg):
    col = (4 + seg) * D + 128 * h
    return col // W_IN_SHARD, col % W_IN_SHARD


class _PieceGather:
    def __init__(self, src, dst, send_sems, recv_sems, relay):
        self.src, self.dst, self.send_sems, self.recv_sems, self.relay = src, dst, send_sems, recv_sems, relay
        self.me = _my_place()[3]

    def _copy(self, i, target, passing_on=False):
        p, lo = _qkv_piece(i // 3, i % 3)
        block = self.dst.at[p, :, lo:lo + 128]
        return pltpu.make_async_remote_copy(
            src_ref=block if passing_on else self.src(p, lo, lo + 128), dst_ref=block,
            send_sem=self.send_sems.at[i, target], recv_sem=self.recv_sems.at[i],
            device_id=_device(target), device_id_type=MESH)

    def _as_each_device(self, pieces, own, relayed, other):
        for m in range(N_DEV):
            def branch(m=m):
                for i in pieces:
                    p = _qkv_piece(i // 3, i % 3)[0]
                    same_core_elsewhere = p >> 1 != m >> 1 and p & 1 == m & 1
                    if p == m:
                        targets = [(p + 1 + (k + i) % (N_DEV - 1)) % N_DEV for k in range(N_DEV - 1)]
                        for t in targets:
                            if not self.relay or t == m ^ 1 or (t >> 1 != m >> 1 and t & 1 == m & 1):
                                own(self._copy(i, t))
                    elif self.relay and same_core_elsewhere:
                        relayed(i, m ^ 1)
                    else:
                        other(i)

            pl.when(self.me == m)(branch)

    def start(self, pieces):
        self._as_each_device(pieces, lambda cp: cp.start(), lambda i, t: None, lambda i: None)

    def forward(self, pieces):
        def pass_on(i, t):
            self._copy(i, t).wait_recv()
            self._copy(i, t, passing_on=True).start()

        self._as_each_device(pieces, lambda cp: None, pass_on, lambda i: None)

    def wait_recv(self, pieces):
        self._as_each_device(pieces, lambda cp: None, lambda i, t: None, lambda i: self._copy(i, 0).wait_recv())

    def wait_send(self, pieces):
        self._as_each_device(pieces, lambda cp: cp.wait_send(),
                             lambda i, t: self._copy(i, t, passing_on=True).wait_send(), lambda i: None)


FIRST_PIECES = 3


def _piece_sems(n):
    return [pltpu.SemaphoreType.DMA((n, N_DEV)), pltpu.SemaphoreType.DMA((n,))]


def _norm_gather_first_weights(x2, norm_g, w_in, w3, cw):
    S = x2.shape[0]
    tm = ROW_TILE
    nsteps = S // tm

    def body(x_ref, g_ref, w_in_ref, w3_ref, cw_ref, u_ref, ut_ref, o_in, o_3, o_cw, in_bf, w3_bf, local_sems, *sems):
        i = pl.program_id(0)
        me = _my_place()[3]
        gather = _PieceGather(lambda p, lo, hi: in_bf.at[:, lo:hi], o_in, *sems, relay=False)
        local = [pltpu.make_async_copy(src, dst.at[me], local_sems.at[a])
                 for a, (src, dst) in enumerate(((in_bf, o_in), (w3_bf, o_3), (cw_ref, o_cw)))]

        @pl.when(i == 0)
        def _():
            def cast_rows(r, carry):
                rows = pl.ds(pl.multiple_of(r * 128, 128), 128)
                in_bf[rows, :] = w_in_ref[rows, :].astype(BF)
                return carry

            lax.fori_loop(0, D // 128, cast_rows, 0)
            for a in range(3):
                w3_bf[a] = w3_ref[a].astype(BF)
            gather.start(range(FIRST_PIECES))
            for cp in local:
                cp.start()

        xv = x_ref[...]
        r = lax.rsqrt(jnp.mean(xv * xv, axis=-1, keepdims=True) + EPS)
        u = xv * r * g_ref[...]
        u_ref[...] = u.astype(BF)
        ut_ref[...] = u.T.astype(BF)

        @pl.when(i == nsteps - 1)
        def _():
            gather.wait_recv(range(FIRST_PIECES))
            gather.wait_send(range(FIRST_PIECES))
            for cp in local:
                cp.wait()

    return pl.pallas_call(
        body, name="norm_gather_first_weights", grid=(nsteps,),
        out_shape=(jax.ShapeDtypeStruct((S, D), BF), jax.ShapeDtypeStruct((D, S), BF),
                   jax.ShapeDtypeStruct((N_DEV, D, W_IN_SHARD), BF),
                   jax.ShapeDtypeStruct((N_DEV, 3, ROW_SHARD, D), BF),
                   jax.ShapeDtypeStruct((N_DEV, 8, 128), F32)),
        in_specs=[pl.BlockSpec((tm, D), lambda i: (i, 0)), pl.BlockSpec((1, D), lambda i: (0, 0)),
                  VMEM_SPEC, VMEM_SPEC, VMEM_SPEC],
        out_specs=(pl.BlockSpec((tm, D), lambda i: (i, 0)), pl.BlockSpec((D, tm), lambda i: (0, i)),
                   ANY_SPEC, ANY_SPEC, ANY_SPEC),
        scratch_shapes=[pltpu.VMEM((D, W_IN_SHARD), BF), pltpu.VMEM((3, ROW_SHARD, D), BF),
                        pltpu.SemaphoreType.DMA((3,))] + _piece_sems(FIRST_PIECES),
        compiler_params=_params(1),
    )(x2, norm_g, w_in, w3, cw)


class _GradExchange:
    def __init__(self, src, dst, send_sems, recv_sems, local_sem, cols):
        self.src, self.dst, self.cols = src, dst, cols
        self.send_sems, self.recv_sems, self.local_sem = send_sems, recv_sems, local_sem
        self.me = _my_place()[3]

    def _remote(self, p, source):
        return pltpu.make_async_remote_copy(
            src_ref=_block(self.src, p, self.cols(p)), dst_ref=_block(self.dst, source, self.cols(p)),
            send_sem=self.send_sems.at[p], recv_sem=self.recv_sems.at[source],
            device_id=_device(p), device_id_type=MESH)

    def _local(self, p):
        return pltpu.make_async_copy(_block(self.src, p, self.cols(p)), _block(self.dst, p, self.cols(p)),
                                     self.local_sem)

    def _as_each_device(self, send, local, receive):
        for m in range(N_DEV):
            def branch(m=m):
                for k in range(1, N_DEV):
                    p = (m + k) % N_DEV
                    if self.cols(p) is not None:
                        send(self._remote(p, m))
                if self.cols(m) is not None:
                    if self.local_sem is not None:
                        local(self._local(m))
                    for k in range(1, N_DEV):
                        receive(self._remote(m, (m + k) % N_DEV))

            pl.when(self.me == m)(branch)

    def start(self):
        self._as_each_device(lambda cp: cp.start(), lambda cp: cp.start(), lambda cp: None)

    def finish(self):
        self._as_each_device(lambda cp: cp.wait_send(), lambda cp: cp.wait(), lambda cp: cp.wait_recv())


GRAD_EXCHANGE_SEMS = [pltpu.SemaphoreType.DMA((N_DEV,)), pltpu.SemaphoreType.DMA((N_DEV,)), pltpu.SemaphoreType.DMA]


def _allreduce_small(p_mid, p_conv, p_norm):
    def body(a_ref, b_ref, c_ref, out_ref, mine, gathered, send_sems, recv_sems):
        x, y, c, me = _my_place()
        mine[...] = a_ref[...] + b_ref[...] + c_ref[...]
        gathered[me] = mine[...]
        remote = []
        for k, (peer, _) in enumerate(_peers(x, y, c)):
            cp = pltpu.make_async_remote_copy(
                src_ref=mine, dst_ref=gathered.at[me], send_sem=send_sems.at[k], recv_sem=recv_sems.at[k],
                device_id=peer, device_id_type=MESH)
            cp.start()
            remote.append(cp)
        for cp in remote:
            cp.wait()
        total = gathered[0]
        for s in range(1, N_DEV):
            total = total + gathered[s]
        out_ref[...] = total

    return pl.pallas_call(
        body, name="allreduce_small",
        out_shape=jax.ShapeDtypeStruct((8, D), F32),
        in_specs=[VMEM_SPEC, VMEM_SPEC, VMEM_SPEC], out_specs=VMEM_SPEC,
        scratch_shapes=[pltpu.VMEM((8, D), F32), pltpu.VMEM((N_DEV, 8, D), F32),
                        pltpu.SemaphoreType.DMA((N_DEV - 1,)), pltpu.SemaphoreType.DMA((N_DEV - 1,))],
    )(p_mid, p_conv, p_norm)


def _proj_pieces():
    cuts = sorted(set(range(0, IN_COLS + 1, D)) | set(range(0, IN_COLS + 1, W_IN_SHARD)))
    return [(lo // D, lo % D, lo // W_IN_SHARD, lo % W_IN_SHARD, hi - lo) for lo, hi in zip(cuts[:-1], cuts[1:])]


PROJ_TN = 256


def _proj_cols(u, w_all, seg0, n_seg, dtype, name):
    S = u.shape[0]
    tn = PROJ_TN
    per_shard = W_IN_SHARD // tn
    tile0 = seg0 * D // tn

    def body(u_ref, w_ref, out_ref):
        out_ref[...] = _dot(u_ref[...], w_ref[0]).astype(dtype)

    return pl.pallas_call(
        body, name=name, grid=(n_seg * D // tn,),
        in_specs=[VMEM_SPEC, pl.BlockSpec((1, D, tn), lambda t: ((tile0 + t) // per_shard, 0, (tile0 + t) % per_shard))],
        out_specs=pl.BlockSpec((S, tn), lambda t: (0, t)),
        out_shape=jax.ShapeDtypeStruct((S, n_seg * D), dtype),
        compiler_params=_params(1),
    )(u, w_all)


CONV_TM, CONV_TC = 256, 512
HALO = 16


def _conv_fwd(pa, cw8):
    S = pa.shape[0]
    tm, tc = CONV_TM, CONV_TC
    nct = D // tc

    def seg(s):
        return pl.BlockSpec((tm, tc), lambda i, j, s=s: (i, s * nct + j))

    def halo_before(s):
        return pl.BlockSpec((HALO, tc), lambda i, j, s=s: (jnp.maximum(i * (tm // HALO) - 1, 0), s * nct + j))

    def body(xc, bg, cg, zc, xch, cgh, cw, out):
        i = pl.program_id(0)
        a = cg[...].astype(F32) * xc[...].astype(F32)
        ah = cgh[...].astype(F32) * xch[...].astype(F32)
        ah = jnp.where(i > 0, ah, 0.0)
        row = lax.broadcasted_iota(jnp.int32, (tm, tc), 0)
        a1 = jnp.where(row == 0, ah[HALO - 1:HALO, :], pltpu.roll(a, 1, 0))
        a2 = jnp.where(row == 0, ah[HALO - 2:HALO - 1, :],
                       jnp.where(row == 1, ah[HALO - 1:HALO, :], pltpu.roll(a, 2, 0)))
        w = cw[...]
        conv = w[0:1, :] * a2 + w[1:2, :] * a1 + w[2:3, :] * a
        z = zc[...].astype(F32)
        out[...] = (z * _sigmoid(z) * bg[...].astype(F32) * conv).astype(BF)

    return pl.pallas_call(
        body, name="conv_fwd", grid=(S // tm, nct),
        in_specs=[seg(0), seg(1), seg(2), seg(3), halo_before(0), halo_before(2),
                  pl.BlockSpec((8, tc), lambda i, j: (0, j))],
        out_specs=pl.BlockSpec((tm, tc), lambda i, j: (i, j)),
        out_shape=jax.ShapeDtypeStruct((S, D), BF),
        compiler_params=_params(2),
    )(pa, pa, pa, pa, pa, pa, cw8)


ATT_UNROLL = 32


LAYOUT_MOD = 4
RUN = QB // LAYOUT_MOD


def _fold_masks(d):
    row = lax.broadcasted_iota(jnp.int32, (QB, QB), 0)
    lane = lax.broadcasted_iota(jnp.int32, (QB, QB), 1)
    if d == 1:
        qpos, kpos = LAYOUT_MOD * (row % RUN) + row // RUN, LAYOUT_MOD * (lane % RUN) + lane // RUN
    else:
        qpos, kpos = row, lane
    tri_le = kpos <= qpos
    dist = jnp.where(tri_le, qpos - kpos, qpos - kpos + QB).astype(F32)
    return tri_le, dist, lane < HEAD_DIM


class _Rows:
    def __init__(self, slices):
        self.slices = slices

    def get(self, ref):
        parts = [ref[sl, :] for sl in self.slices]
        return parts[0] if len(parts) == 1 else jnp.concatenate(parts, axis=0)

    def put(self, ref, val):
        size = QB // len(self.slices)
        for g, sl in enumerate(self.slices):
            ref[sl, :] = val if len(self.slices) == 1 else val[g * size:(g + 1) * size]

    def add(self, ref, val):
        self.put(ref, self.get(ref) + val)


def _block_rows(b, d, S):
    quarter = S // LAYOUT_MOD
    nb = S // (QB * d)
    r, n = b // nb, b % nb
    n_prev = jnp.maximum(n - 1, 0)
    if d == 1:
        runs = lambda m: _Rows([pl.ds(pl.multiple_of(g * quarter + RUN * m, RUN), RUN) for g in range(LAYOUT_MOD)])
        return n, runs(n), runs(n_prev)
    if d == LAYOUT_MOD:
        block = lambda m: _Rows([pl.ds(pl.multiple_of(r * quarter + QB * m, QB), QB)])
        return n, block(n), block(n_prev)
    step = d // LAYOUT_MOD
    first = (r % LAYOUT_MOD) * quarter + r // LAYOUT_MOD
    strided = lambda m: _Rows([pl.ds(first + QB * step * m, QB, stride=step)])
    return n, strided(n), strided(n_prev)


def _natural_rows(i, S):
    per = S // LAYOUT_MOD // QB
    return pl.ds(i // per + LAYOUT_MOD * QB * (i % per), QB, stride=LAYOUT_MOD)


def _head_sum_matrix():
    r = lax.broadcasted_iota(jnp.int32, (2 * QB, 2 * QB), 0)
    c = lax.broadcasted_iota(jnp.int32, (2 * QB, 2 * QB), 1)
    return (((r % QB) // HEAD_DIM) == (c // QB)).astype(F32).astype(BF)


def _hi_lo(t):
    hi = t.astype(BF)
    return jnp.concatenate([hi, (t - hi.astype(F32)).astype(BF)], axis=1)


PROJ_ROWS = 512


def _attn_fwd(u, slopes, w_all, w3_all, cw_all):
    S = u.shape[0]
    hpr = HEAD_PAIRS
    n_blocks = S // QB
    later = range(FIRST_PIECES, 3 * hpr)

    def body(sl_ref, u_ref, w_in_ref, w3_in_ref, cw_in_ref, o_ref, lse_ref, q_ref, k_ref, v_ref, w_ref, w3_ref,
             cw_ref, acc, m_s, l_s, w_tile, staged, tile_sems, *sems):
        hp = pl.program_id(0)
        me = _my_place()[3]
        pieces = _PieceGather(lambda p, lo, hi: w_ref.at[p, :, lo:hi], w_ref, *sems[0:2], relay=True)
        gathers = (_WeightGather(lambda p, cols: _block(w_ref, me, cols), w_ref, *sems[2:5], REST_COLS),
                   _WeightGather(lambda p, cols: w3_ref.at[me], w3_ref, *sems[5:8], _whole),
                   _WeightGather(lambda p, cols: cw_ref.at[me], cw_ref, *sems[8:11], _whole))

        @pl.when(hp == 0)
        def _():
            pieces.start(later)
            for g in gathers:
                g.start()

        @pl.when(hp == FORWARD_STEP)
        def _():
            for g in gathers:
                g.forward()

        for h in range(hpr):
            @pl.when(hp == h)
            def _(h=h):
                if 3 * h >= FIRST_PIECES:
                    pieces.wait_recv(range(3 * h, 3 * h + 3))
                if 3 * h + 3 >= FIRST_PIECES and h + 1 < hpr:
                    pieces.forward(range(3 * h + 3, 3 * h + 6))
                fetch = []
                for seg in range(3):
                    p, lo = _qkv_piece(h, seg)
                    fetch.append(pltpu.make_async_copy(w_ref.at[p, :, lo:lo + 128], w_tile.at[:, seg * 128:(seg + 1) * 128],
                                                       tile_sems.at[seg]))
                    fetch[-1].start()
                for cp in fetch:
                    cp.wait()

        def project(i, carry):
            rows = pl.ds(pl.multiple_of(i * PROJ_ROWS, PROJ_ROWS), PROJ_ROWS)
            qkv = _dot(u_ref[rows, :], w_tile[...])
            per = PROJ_ROWS // LAYOUT_MOD
            for seg, ref in enumerate((q_ref, k_ref, v_ref)):
                staged[seg] = qkv[:, seg * 128:(seg + 1) * 128]
                for g in range(LAYOUT_MOD):
                    dst = pl.ds(pl.multiple_of(g * (S // LAYOUT_MOD) + i * per, per), per)
                    ref[dst, :] = staged.at[seg][pl.ds(g, per, stride=LAYOUT_MOD), :]
            return carry

        lax.fori_loop(0, S // PROJ_ROWS, project, 0)

        head_sum = _head_sum_matrix()
        ones_b = jnp.ones((2 * QB, QB), BF)
        m_s[...] = jnp.full(m_s.shape, NEG, F32)
        l_s[...] = jnp.zeros(l_s.shape, F32)
        acc[...] = jnp.zeros(acc.shape, F32)

        for d in DILATIONS:
            tri_le, dist, low = _fold_masks(d)
            low_b = low.astype(F32).astype(BF)
            high_b = 1.0 - low_b
            slope = [sl_ref[2 * hp + a] * float(d) for a in range(2)]
            bias = [slope[a] * dist for a in range(2)]

            def block(b, d=d, slope=slope, bias=bias, tri_le=tri_le, low=low, low_b=low_b, high_b=high_b):
                n, cur, prev = _block_rows(b, d, S)
                has_prev = n > 0
                valid = jnp.logical_or(tri_le, has_prev)
                q2 = (cur.get(q_ref) * 0.125).astype(BF)
                qs = jnp.concatenate([q2 * low_b, q2 * high_b], axis=0)
                vp = prev.get(v_ref)
                kp_b = prev.get(k_ref).astype(BF)
                kcat = jnp.concatenate([kp_b, cur.get(k_ref).astype(BF)], axis=0)
                vcat = jnp.concatenate([vp, cur.get(v_ref)], axis=0).astype(BF)
                s2 = _dot_nt(qs, kcat)
                e2 = _dot(_hi_lo(q2.astype(F32) * kp_b.astype(F32)), head_sum)
                p_rows, alpha_h, pe_h = [], [], []
                for a in range(2):
                    sp, sc = s2[a * QB:(a + 1) * QB, :QB], s2[a * QB:(a + 1) * QB, QB:]
                    comb = jnp.where(valid, jnp.where(tri_le, sc, sp) - bias[a], NEG)
                    e = jnp.where(has_prev, e2[:, a * QB:(a + 1) * QB] - slope[a] * float(QB), NEG)
                    m_old = cur.get(m_s.at[a])
                    m_new = jnp.maximum(jnp.maximum(m_old, jnp.max(comb, axis=-1, keepdims=True)), e)
                    cur.put(m_s.at[a], m_new)
                    p = jnp.exp(comb - m_new)
                    pe_h.append(jnp.exp(e - m_new))
                    alpha_h.append(jnp.exp(m_old - m_new))
                    p_rows.append(jnp.concatenate([jnp.where(tri_le, 0.0, p).astype(BF),
                                                   jnp.where(tri_le, p, 0.0).astype(BF)], axis=1))
                pv = _dot(jnp.concatenate(p_rows, axis=0), jnp.concatenate([vcat, ones_b], axis=1))
                for a in range(2):
                    cur.put(l_s.at[a], alpha_h[a] * cur.get(l_s.at[a]) + pv[a * QB:(a + 1) * QB, QB:] + pe_h[a])
                cur.put(acc, jnp.where(low, alpha_h[0], alpha_h[1]) * cur.get(acc)
                        + jnp.where(low, pv[:QB, :QB], pv[QB:, :QB]) + jnp.where(low, pe_h[0], pe_h[1]) * vp)

            def several(it, carry, block=block):
                for u in range(ATT_UNROLL):
                    block(it * ATT_UNROLL + u)
                return carry

            lax.fori_loop(0, n_blocks // ATT_UNROLL, several, 0)

        low = _fold_masks(LAYOUT_MOD)[2]

        def finish(i, carry):
            rows = pl.ds(pl.multiple_of(i * QB, QB), QB)
            l0, l1 = l_s[0, rows, :], l_s[1, rows, :]
            o_ref[_natural_rows(i, S), :] = acc[rows, :] / jnp.where(low, l0, l1)
            lse_ref[0, rows, :] = m_s[0, rows, :] + jnp.log(l0)
            lse_ref[1, rows, :] = m_s[1, rows, :] + jnp.log(l1)
            return carry

        lax.fori_loop(0, n_blocks, finish, 0)

        @pl.when(hp == hpr - 1)
        def _():
            pieces.wait_send(later)
            for g in gathers:
                g.finish()

    col = pl.BlockSpec((S, 128), lambda h: (0, h))
    act = jax.ShapeDtypeStruct((S, D), F32)
    gathered = (w_all, w3_all, cw_all)
    return pl.pallas_call(
        body, name="attn_fwd", grid=(hpr,),
        in_specs=[SMEM_SPEC, VMEM_SPEC, ANY_SPEC, ANY_SPEC, ANY_SPEC],
        out_specs=(col, pl.BlockSpec((2, S, 128), lambda h: (0, 0, h)), col, col, col, ANY_SPEC, ANY_SPEC, ANY_SPEC),
        out_shape=(act, jax.ShapeDtypeStruct((2, S, D), F32), act, act, act,
                   *[jax.ShapeDtypeStruct(t.shape, t.dtype) for t in gathered]),
        scratch_shapes=([pltpu.VMEM((S, 128), F32), pltpu.VMEM((2, S, 128), F32), pltpu.VMEM((2, S, 128), F32),
                         pltpu.VMEM((D, 3 * 128), BF), pltpu.VMEM((3, PROJ_ROWS, 128), F32),
                         pltpu.SemaphoreType.DMA((3,))]
                        + _piece_sems(3 * hpr) + WEIGHT_GATHER_SEMS * 3),
        input_output_aliases={2: 5, 3: 6, 4: 7},
        compiler_params=_params(1),
    )(slopes, u, *gathered)


def _set_rows(shape, rows):
    idx = lax.broadcasted_iota(jnp.int32, shape, 0)
    out = jnp.zeros(shape, F32)
    for r, val in rows.items():
        out = out + jnp.where(idx == r, val, 0.0)
    return out


def _mid(yc_in, pa_mid, o, x2, target, b_merge, final_g, w3):
    S = x2.shape[0]
    tm = ROW_TILE
    nsteps = S // tm
    tile = pl.BlockSpec((tm, D), lambda i: (i, 0))

    def body(yc_ref, za_ref, gcp_ref, gap_ref, o_ref, x_ref, t_ref, b_ref, fg_ref, w_ref,
             dh_ref, dmid_ref, do_ref, dyc_ref, gw_ref, small_ref, acc, stage):
        i = pl.program_id(0)

        @pl.when(i == 0)
        def _():
            acc[...] = jnp.zeros_like(acc)
            small_ref[...] = jnp.zeros_like(small_ref)

        wc, wa, wo = w_ref[0], w_ref[1], w_ref[2]
        z = za_ref[...].astype(F32)
        sg = _sigmoid(z)
        ov = o_ref[...]
        yc_in_b, ya_in_b = yc_ref[...], (z * sg * ov).astype(BF)
        yc = _dot(yc_in_b, wc)
        ya = _dot(ya_in_b, wa)
        b = b_ref[...]
        gc = _sigmoid(gcp_ref[...].astype(F32) + b[:, :D])
        ga = _sigmoid(gap_ref[...].astype(F32) + b[:, D:])
        merged = gc * yc + ga * ya
        merged_b = merged.astype(BF)
        h = x_ref[...] + _dot(merged_b, wo)
        r2 = lax.rsqrt(jnp.mean(h * h, axis=-1, keepdims=True) + EPS)
        n = h * r2
        fg = fg_ref[...]
        err = n * fg - t_ref[...]
        loss = 0.5 * jnp.sum(jnp.sum(err * err, axis=-1, keepdims=True) / D, axis=0, keepdims=True)
        dy = err / D
        g_fg = jnp.sum(dy * n, axis=0, keepdims=True)
        dn = dy * fg
        dh = r2 * (dn - n * jnp.mean(dn * n, axis=-1, keepdims=True))
        dh_ref[...] = dh
        dh_b = dh.astype(BF)
        dmerged = _dot_nt(dh_b, wo)
        acc[2] += _dot(merged.T.astype(BF), dh_b)
        dyc = (dmerged * gc).astype(BF)
        dya = (dmerged * ga).astype(BF)
        dgcp = dmerged * yc * gc * (1.0 - gc)
        dgap = dmerged * ya * ga * (1.0 - ga)
        dmid_ref[1] = dgcp.astype(BF)
        dmid_ref[2] = dgap.astype(BF)
        acc[0] += _dot(yc_in_b.astype(F32).T.astype(BF), dyc)
        acc[1] += _dot(ya_in_b.astype(F32).T.astype(BF), dya)
        dyc_ref[...] = _dot_nt(dyc, wc).astype(BF)
        dya_in = _dot_nt(dya, wa)
        do_ref[...] = dya_in * (z * sg)
        dmid_ref[0] = (dya_in * ov * (sg * (1.0 + z * (1.0 - sg)))).astype(BF)
        small_ref[...] += _set_rows((8, D), {
            1: jnp.sum(dgcp, axis=0, keepdims=True), 2: jnp.sum(dgap, axis=0, keepdims=True),
            3: g_fg, 7: jnp.broadcast_to(loss, (1, D))})

        @pl.when(i == nsteps - 1)
        def _():
            for p in range(N_DEV):
                for a in range(3):
                    stage[...] = acc[a, p * ROW_SHARD:(p + 1) * ROW_SHARD, :].astype(BF)
                    pltpu.sync_copy(stage, gw_ref.at[p, a])

    return pl.pallas_call(
        body, name="mid", grid=(nsteps,),
        in_specs=[tile, pl.BlockSpec((tm, D), lambda i: (i, 0)), pl.BlockSpec((tm, D), lambda i: (i, 1)),
                  pl.BlockSpec((tm, D), lambda i: (i, 2)), tile, tile, tile,
                  pl.BlockSpec((1, 2 * D), lambda i: (0, 0)), pl.BlockSpec((1, D), lambda i: (0, 0)), VMEM_SPEC],
        out_specs=(tile, pl.BlockSpec((3, tm, D), lambda i: (0, i, 0)), tile, tile,
                   ANY_SPEC, pl.BlockSpec((8, D), lambda i: (0, 0))),
        out_shape=(jax.ShapeDtypeStruct((S, D), F32), jax.ShapeDtypeStruct((3, S, D), BF),
                   jax.ShapeDtypeStruct((S, D), F32), jax.ShapeDtypeStruct((S, D), BF),
                   jax.ShapeDtypeStruct((N_DEV, 3, ROW_SHARD, D), BF), jax.ShapeDtypeStruct((8, D), F32)),
        scratch_shapes=[pltpu.VMEM((3, D, D), F32), pltpu.VMEM((ROW_SHARD, D), BF)],
        compiler_params=_params(1),
    )(yc_in, pa_mid, pa_mid, pa_mid, o, x2, target, b_merge, final_g, w3)


def _conv_bwd(dyc_in, pa, cw8):
    S = pa.shape[0]
    tm, tc = CONV_TM, CONV_TC
    nct = D // tc
    nrt = S // tm
    last_halo = S // HALO - 1

    def seg(s):
        return pl.BlockSpec((tm, tc), lambda j, i, s=s: (i, s * nct + j))

    def halo_before(s):
        return pl.BlockSpec((HALO, tc), lambda j, i, s=s: (jnp.maximum(i * (tm // HALO) - 1, 0), s * nct + j))

    def halo_after(s):
        return pl.BlockSpec((HALO, tc), lambda j, i, s=s: (jnp.minimum((i + 1) * (tm // HALO), last_halo), s * nct + j))

    def body(dy, xc, bg, cg, zc, xch, cgh, dyn, bgn, zcn, cw, dout, gcw):
        i = pl.program_id(1)

        @pl.when(i == 0)
        def _():
            gcw[...] = jnp.zeros_like(gcw)

        xcv, cgv = xc[...].astype(F32), cg[...].astype(F32)
        a = cgv * xcv
        ah = jnp.where(i > 0, cgh[...].astype(F32) * xch[...].astype(F32), 0.0)
        row = lax.broadcasted_iota(jnp.int32, (tm, tc), 0)
        a1 = jnp.where(row == 0, ah[HALO - 1:HALO, :], pltpu.roll(a, 1, 0))
        a2 = jnp.where(row == 0, ah[HALO - 2:HALO - 1, :],
                       jnp.where(row == 1, ah[HALO - 1:HALO, :], pltpu.roll(a, 2, 0)))
        w = cw[...]
        conv = w[0:1, :] * a2 + w[1:2, :] * a1 + w[2:3, :] * a
        z = zc[...].astype(F32)
        sg = _sigmoid(z)
        silu = z * sg
        bgv = bg[...].astype(F32)
        dyv = dy[...].astype(F32)
        dout[3] = (dyv * bgv * conv * (sg * (1.0 + z * (1.0 - sg)))).astype(BF)
        dout[1] = (dyv * silu * conv).astype(BF)
        dc = dyv * silu * bgv
        zn = zcn[...].astype(F32)
        dcn = dyn[...].astype(F32) * (zn * _sigmoid(zn)) * bgn[...].astype(F32)
        dcn = jnp.where(i < nrt - 1, dcn, 0.0)
        dc1 = jnp.where(row == tm - 1, dcn[0:1, :], pltpu.roll(dc, tm - 1, 0))
        dc2 = jnp.where(row == tm - 1, dcn[1:2, :],
                        jnp.where(row == tm - 2, dcn[0:1, :], pltpu.roll(dc, tm - 2, 0)))
        da = w[2:3, :] * dc + w[1:2, :] * dc1 + w[0:1, :] * dc2
        dout[2] = (da * xcv).astype(BF)
        dout[0] = (da * cgv).astype(BF)
        gcw[...] += _set_rows((8, tc), {
            4: jnp.sum(dc * a2, axis=0, keepdims=True), 5: jnp.sum(dc * a1, axis=0, keepdims=True),
            6: jnp.sum(dc * a, axis=0, keepdims=True)})

    return pl.pallas_call(
        body, name="conv_bwd", grid=(nct, nrt),
        in_specs=[pl.BlockSpec((tm, tc), lambda j, i: (i, j)), seg(0), seg(1), seg(2), seg(3),
                  halo_before(0), halo_before(2),
                  pl.BlockSpec((HALO, tc), lambda j, i: (jnp.minimum((i + 1) * (tm // HALO), last_halo), j)),
                  halo_after(1), halo_after(3), pl.BlockSpec((8, tc), lambda j, i: (0, j))],
        out_specs=(pl.BlockSpec((4, tm, tc), lambda j, i: (0, i, j)), pl.BlockSpec((8, tc), lambda j, i: (0, j))),
        out_shape=(jax.ShapeDtypeStruct((4, S, D), BF), jax.ShapeDtypeStruct((8, D), F32)),
        compiler_params=_params(2),
    )(dyc_in, pa, pa, pa, pa, pa, pa, dyc_in, pa, pa, cw8)


def _attn_bwd(q, k, v, slopes, do, o, lse, g_in, g_3):
    S = q.shape[0]
    hpr = HEAD_PAIRS
    n_blocks = S // QB

    def body(sl_ref, q_ref, k_ref, v_ref, do_ref, o_ref, lse_ref, gin_ref, g3_ref, out_ref, rin_ref, r3_ref,
             dq_s, dk_s, dv_s, do_s, dd_s, *sems):
        hp = pl.program_id(0)
        exchanges = (_GradExchange(gin_ref, rin_ref, *sems[:3], _shard_cols((0, SEG0_ATTN * D), (SEG0_MID * D, IN_COLS))),
                     _GradExchange(g3_ref, r3_ref, *sems[3:], _whole))

        @pl.when(hp == 0)
        def _():
            for ex in exchanges:
                ex.start()

        head_sum = _head_sum_matrix()
        dq_s[...] = jnp.zeros(dq_s.shape, F32)
        dk_s[...] = jnp.zeros(dk_s.shape, F32)
        dv_s[...] = jnp.zeros(dv_s.shape, F32)

        def row_dots(i, carry):
            rows = pl.ds(pl.multiple_of(i * QB, QB), QB)
            natural = _natural_rows(i, S)
            do_c = do_ref[natural, :]
            do_s[rows, :] = do_c
            dd = _dot(_hi_lo(do_c * o_ref[natural, :]), head_sum)
            dd_s[0, rows, :] = dd[:, :QB]
            dd_s[1, rows, :] = dd[:, QB:]
            return carry

        lax.fori_loop(0, n_blocks, row_dots, 0)

        for d in DILATIONS:
            tri_le, dist, low = _fold_masks(d)
            low_b = low.astype(F32).astype(BF)
            high_b = 1.0 - low_b
            slope = [sl_ref[2 * hp + a] * float(d) for a in range(2)]
            bias = [slope[a] * dist for a in range(2)]

            def block(b, d=d, slope=slope, bias=bias, tri_le=tri_le, low=low, low_b=low_b, high_b=high_b):
                n, cur, prev = _block_rows(b, d, S)
                has_prev = n > 0
                valid = jnp.logical_or(tri_le, has_prev)
                q2f = cur.get(q_ref) * 0.125
                q2 = q2f.astype(BF)
                qs = jnp.concatenate([q2 * low_b, q2 * high_b], axis=0)
                kp, vp = prev.get(k_ref), prev.get(v_ref)
                kp_b, vp_b = kp.astype(BF), vp.astype(BF)
                kcat = jnp.concatenate([kp_b, cur.get(k_ref).astype(BF)], axis=0)
                vcat = jnp.concatenate([vp_b, cur.get(v_ref).astype(BF)], axis=0)
                do2f = cur.get(do_s)
                do2 = do2f.astype(BF)
                dos = jnp.concatenate([do2 * low_b, do2 * high_b], axis=0)
                s2 = _dot_nt(qs, kcat)
                dp2 = _dot_nt(dos, vcat)
                diag2 = _dot(jnp.concatenate([_hi_lo(q2.astype(F32) * kp_b.astype(F32)),
                                              _hi_lo(do2.astype(F32) * vp_b.astype(F32))], axis=0), head_sum)
                p_rows, ds_rows, pe_h, dse_h = [], [], [], []
                for a in range(2):
                    hs = slice(a * QB, (a + 1) * QB)
                    sp, sc = s2[hs, :QB], s2[hs, QB:]
                    dpp, dpc = dp2[hs, :QB], dp2[hs, QB:]
                    lse_a, dd_a = cur.get(lse_ref.at[a]), cur.get(dd_s.at[a])
                    comb = jnp.where(tri_le, sc, sp) - bias[a]
                    e = diag2[:QB, hs] - slope[a] * float(QB)
                    p = jnp.where(valid, jnp.exp(comb - lse_a), 0.0)
                    pe = jnp.where(has_prev, jnp.exp(e - lse_a), 0.0)
                    ds = p * (jnp.where(tri_le, dpc, dpp) - dd_a)
                    dse_h.append(pe * (diag2[QB:, hs] - dd_a))
                    pe_h.append(pe)
                    p_rows.append(jnp.concatenate([jnp.where(tri_le, 0.0, p).astype(BF),
                                                   jnp.where(tri_le, p, 0.0).astype(BF)], axis=1))
                    ds_rows.append(jnp.concatenate([jnp.where(tri_le, 0.0, ds).astype(BF),
                                                    jnp.where(tri_le, ds, 0.0).astype(BF)], axis=1))
                pst = jnp.concatenate(p_rows, axis=0)
                dst = jnp.concatenate(ds_rows, axis=0)
                pe2 = jnp.where(low, pe_h[0], pe_h[1])
                dse2 = jnp.where(low, dse_h[0], dse_h[1])
                dq = _dot(dst, kcat)
                cur.add(dq_s, (jnp.where(low, dq[:QB], dq[QB:]) + dse2 * kp) * 0.125)
                dk = _dot_tn(dst, qs)
                dv = _dot_tn(pst, dos)
                prev.add(dk_s, dk[:QB] + dse2 * q2f)
                cur.add(dk_s, dk[QB:])
                prev.add(dv_s, dv[:QB] + pe2 * do2f)
                cur.add(dv_s, dv[QB:])

            def several(it, carry, block=block):
                for u in range(ATT_UNROLL):
                    block(it * ATT_UNROLL + u)
                return carry

            lax.fori_loop(0, n_blocks // ATT_UNROLL, several, 0)

        def finish(i, carry):
            rows = pl.ds(pl.multiple_of(i * QB, QB), QB)
            natural = _natural_rows(i, S)
            for t, ref in enumerate((dq_s, dk_s, dv_s)):
                out_ref.at[t][natural, :] = ref[rows, :]
            return carry

        lax.fori_loop(0, n_blocks, finish, 0)

        @pl.when(hp == hpr - 1)
        def _():
            for ex in exchanges:
                ex.finish()

    col = pl.BlockSpec((S, 128), lambda h: (0, h))
    return pl.pallas_call(
        body, name="attn_bwd", grid=(hpr,),
        in_specs=[SMEM_SPEC, col, col, col, col, col, pl.BlockSpec((2, S, 128), lambda h: (0, 0, h)),
                  ANY_SPEC, ANY_SPEC],
        out_specs=(pl.BlockSpec((3, S, 128), lambda h: (0, 0, h)), ANY_SPEC, ANY_SPEC),
        out_shape=(jax.ShapeDtypeStruct((3, S, D), F32), jax.ShapeDtypeStruct(g_in.shape, BF),
                   jax.ShapeDtypeStruct(g_3.shape, BF)),
        scratch_shapes=([pltpu.VMEM((S, 128), F32)] * 4 + [pltpu.VMEM((2, S, 128), F32)]
                        + GRAD_EXCHANGE_SEMS + GRAD_EXCHANGE_SEMS),
        compiler_params=_params(1),
    )(slopes, q, k, v, do, o, lse, g_in, g_3)


WG_TN = 256
SEG0_CONV, SEG0_ATTN, SEG0_MID = 0, 4, 7


def _wgrad_in(ut, d_group, seg0, g_in, name):
    S = ut.shape[1]
    tn = WG_TN
    per_seg = D // tn
    per_shard = W_IN_SHARD // tn
    n_tiles = d_group.shape[0] * per_seg
    tile0 = seg0 * per_seg

    def body(ut_ref, d_ref, *rest):
        rest[-1][0] = _dot(ut_ref[...], d_ref[0].astype(BF)).astype(BF)

    operands, in_specs, aliases = [ut, d_group], [VMEM_SPEC, pl.BlockSpec((1, S, tn), lambda t: (t // per_seg, 0, t % per_seg))], {}
    if g_in is not None:
        operands.append(g_in)
        in_specs.append(ANY_SPEC)
        aliases = {2: 0}
    return pl.pallas_call(
        body, name=name, grid=(n_tiles,), in_specs=in_specs,
        out_specs=pl.BlockSpec((1, D, tn), lambda t: ((tile0 + t) // per_shard, 0, (tile0 + t) % per_shard)),
        out_shape=jax.ShapeDtypeStruct((N_DEV, D, W_IN_SHARD), BF),
        input_output_aliases=aliases,
        compiler_params=_params(1),
    )(*operands)


def _dgrad_norm_bwd(d_conv, d_attn, d_mid, w_all, x2, dh, norm_g):
    S = x2.shape[0]
    tm = ROW_TILE
    nsteps = S // tm
    tile = pl.BlockSpec((tm, D), lambda i: (i, 0))
    pieces = _proj_pieces()

    def body(a_ref, b_ref, c_ref, w_ref, x_ref, dh_ref, g_ref, gx_ref, small_ref):
        i = pl.program_id(0)

        @pl.when(i == 0)
        def _():
            small_ref[...] = jnp.zeros_like(small_ref)

        groups = (a_ref, b_ref, c_ref)
        du = jnp.zeros((tm, D), F32)
        for s, sc, p, pc, width in pieces:
            g = 0 if s < 4 else (1 if s < 7 else 2)
            local = s - (0, 4, 7)[g]
            du = du + _dot_nt(groups[g][local, :, sc:sc + width].astype(BF), w_ref[p, :, pc:pc + width])
        xv = x_ref[...]
        r = lax.rsqrt(jnp.mean(xv * xv, axis=-1, keepdims=True) + EPS)
        n = xv * r
        dn = du * g_ref[...]
        gx_ref[...] = dh_ref[...] + r * (dn - n * jnp.mean(dn * n, axis=-1, keepdims=True))
        small_ref[...] += _set_rows((8, D), {0: jnp.sum(du * n, axis=0, keepdims=True)})

    return pl.pallas_call(
        body, name="dgrad_norm_bwd", grid=(nsteps,),
        in_specs=[pl.BlockSpec((4, tm, D), lambda i: (0, i, 0)), pl.BlockSpec((3, tm, D), lambda i: (0, i, 0)),
                  pl.BlockSpec((3, tm, D), lambda i: (0, i, 0)), VMEM_SPEC, tile, tile,
                  pl.BlockSpec((1, D), lambda i: (0, 0))],
        out_specs=(tile, pl.BlockSpec((8, D), lambda i: (0, 0))),
        out_shape=(jax.ShapeDtypeStruct((S, D), F32), jax.ShapeDtypeStruct((8, D), F32)),
        compiler_params=_params(1),
    )(d_conv, d_attn, d_mid, w_all, x2, dh, norm_g)


HBM_SPEC = pl.BlockSpec(memory_space=pltpu.HBM)
SEM_SPEC = pl.BlockSpec(memory_space=pltpu.SEMAPHORE)
ATTN_COLS = _shard_cols((SEG0_ATTN * D, SEG0_MID * D))


def _attn_cols_exchange_start(g_in, r_in):
    def body(g_ref, r_ref, send_sems, recv_sems, g_thru, r_thru, token):
        _GradExchange(g_ref, r_ref, send_sems, recv_sems, None, ATTN_COLS).start()
        token[...] = jnp.zeros_like(token)

    hbm = pltpu.with_memory_space_constraint
    return pl.pallas_call(
        body, name="attn_cols_exchange_start",
        out_shape=(pltpu.SemaphoreType.DMA((N_DEV,)), pltpu.SemaphoreType.DMA((N_DEV,)),
                   pltpu.HBM(g_in.shape, g_in.dtype), pltpu.HBM(r_in.shape, r_in.dtype),
                   jax.ShapeDtypeStruct((8, 128), F32)),
        in_specs=(HBM_SPEC, HBM_SPEC), out_specs=(SEM_SPEC, SEM_SPEC, HBM_SPEC, HBM_SPEC, VMEM_SPEC),
        input_output_aliases={0: 2, 1: 3},
        compiler_params=pltpu.CompilerParams(has_side_effects=pltpu.SideEffectType.DATAFLOW_SIDE_EFFECTING),
    )(hbm(g_in, pltpu.HBM), hbm(r_in, pltpu.HBM))


def _attn_cols_exchange_wait(send_sems, recv_sems, g_thru, r_thru, after):
    def body(g_ref, r_ref, send_sems, recv_sems, after_ref, g_dead, r_out):
        _GradExchange(g_ref, r_ref, send_sems, recv_sems, None, ATTN_COLS).finish()

    return pl.pallas_call(
        body, name="attn_cols_exchange_wait",
        out_shape=(pltpu.HBM(g_thru.shape, g_thru.dtype), pltpu.HBM(r_thru.shape, r_thru.dtype)),
        in_specs=(HBM_SPEC, HBM_SPEC, SEM_SPEC, SEM_SPEC, ANY_SPEC), out_specs=(HBM_SPEC, HBM_SPEC),
        input_output_aliases={0: 0, 1: 1},
        compiler_params=pltpu.CompilerParams(has_side_effects=pltpu.SideEffectType.DATAFLOW_SIDE_EFFECTING),
    )(g_thru, r_thru, send_sems, recv_sems, after)


def _adamw_math(w, g, m, v):
    m = ADAM_B1 * m + (1.0 - ADAM_B1) * g
    v = ADAM_B2 * v + (1.0 - ADAM_B2) * (g * g)
    m_hat = m / (1.0 - ADAM_B1 ** ADAM_STEP)
    v_hat = v / (1.0 - ADAM_B2 ** ADAM_STEP)
    delta = -ADAM_LR * (m_hat / (jnp.sqrt(v_hat) + ADAM_EPS) + ADAM_WD * w)
    return delta, m, v


def _sum_adamw(parts, w, m, v, tm, name):
    R, C = w.shape
    tile = pl.BlockSpec((tm, C), lambda i: (i, 0))

    def body(p_ref, w_ref, m_ref, v_ref, g_out, d_out, m_out, v_out):
        g = p_ref[0].astype(F32)
        for s in range(1, N_DEV):
            g = g + p_ref[s].astype(F32)
        g_out[...] = g
        d_out[...], m_out[...], v_out[...] = _adamw_math(w_ref[...], g, m_ref[...], v_ref[...])

    shape = jax.ShapeDtypeStruct((R, C), F32)
    return pl.pallas_call(
        body, name=name, grid=(R // tm,),
        in_specs=[pl.BlockSpec((N_DEV, tm, C), lambda i: (0, i, 0)), tile, tile, tile],
        out_specs=(tile, tile, tile, tile), out_shape=(shape, shape, shape, shape),
        compiler_params=_params(1),
    )(parts, w, m, v)


def _adamw(g, w, m, v, name):
    def body(g_ref, w_ref, m_ref, v_ref, d_out, m_out, v_out):
        d_out[...], m_out[...], v_out[...] = _adamw_math(w_ref[...], g_ref[...], m_ref[...], v_ref[...])

    shape = jax.ShapeDtypeStruct(w.shape, F32)
    return pl.pallas_call(
        body, name=name, in_specs=[VMEM_SPEC] * 4, out_specs=(VMEM_SPEC,) * 3, out_shape=(shape, shape, shape),
    )(g, w, m, v)


def _alibi_slopes():
    return jnp.exp2(-8.0 * jnp.arange(1, N_HEADS + 1, dtype=F32) / N_HEADS)


def _local_step(x2, target, norm_g, b_merge, final_g, w_in, w3_shard, cw_shard):
    slopes = _alibi_slopes()
    u, ut, w_all, w3_all, cw_all = _norm_gather_first_weights(x2, norm_g, w_in, w3_shard, cw_shard)
    o, lse, q, k, v, w_all, w3_all, cw_all = _attn_fwd(u, slopes, w_all, w3_all, cw_all)
    w3 = jnp.transpose(w3_all, (1, 0, 2, 3)).reshape(3, D, D)
    cw8 = jnp.transpose(cw_all, (1, 0, 2)).reshape(8, D)
    pa = _proj_cols(u, w_all, SEG0_CONV, 4, BF, "proj_conv")
    yc_in = _conv_fwd(pa, cw8)
    pa_mid = _proj_cols(u, w_all, SEG0_MID, 3, BF, "proj_mid")
    dh, d_mid, do, dyc_in, g_3, small_mid = _mid(yc_in, pa_mid, o, x2, target, b_merge, final_g, w3)
    g_in = _wgrad_in(ut, d_mid, SEG0_MID, None, "wgrad_in_mid")
    d_conv, small_conv = _conv_bwd(dyc_in, pa, cw8)
    g_in = _wgrad_in(ut, d_conv, SEG0_CONV, g_in, "wgrad_in_conv")
    d_attn, r_in, r_3 = _attn_bwd(q, k, v, slopes, do, o, lse, g_in, g_3)
    g_in = _wgrad_in(ut, d_attn, SEG0_ATTN, g_in, "wgrad_in_attn")
    *in_flight, token = _attn_cols_exchange_start(g_in, r_in)
    grad_x, small_norm = _dgrad_norm_bwd(d_conv, d_attn, d_mid, w_all, x2, dh, norm_g + token[0:1, 0:1])
    return grad_x, in_flight, r_3, small_mid, small_conv, small_norm


def kernel(x, norm_g, w_in, b_merge, conv_w, w_out_conv, w_out_attn, w_o, final_g, loss_target, m_norm_g, m_w_in, m_b_merge, m_conv_w, m_w_out_conv, m_w_out_attn, m_w_o, m_final_g, v_norm_g, v_w_in, v_b_merge, v_conv_w, v_w_out_conv, v_w_out_attn, v_w_o, v_final_g):
    me = 4 * lax.axis_index("x") + 2 * lax.axis_index("y") + lax.axis_index("c")
    stack3 = lambda a, b, c: jnp.concatenate([a, b, c], axis=0)
    pad8 = lambda a: jnp.pad(a, ((0, 8 - a.shape[0]), (0, 0)))

    w3_shard = stack3(w_out_conv, w_out_attn, w_o)
    final_g2 = final_g.reshape(1, D)
    grad_x, in_flight, r_3, small_mid, small_conv, small_norm = _local_step(
        x[0], loss_target[0], norm_g, b_merge, final_g2, w_in[0], w3_shard, pad8(conv_w[0]))

    small = _allreduce_small(small_mid, small_conv, small_norm)
    g_in, r_in = _attn_cols_exchange_wait(*in_flight, small)
    own = lax.dynamic_index_in_dim(g_in, me, 0, keepdims=True)
    r_in = lax.dynamic_update_slice(r_in, own, (me, 0, 0))

    g_w_in, d_w_in, nm_w_in, nv_w_in = _sum_adamw(r_in, w_in[0], m_w_in[0], v_w_in[0], 128, "adamw_w_in")
    g_w3, d_w3, nm_w3, nv_w3 = _sum_adamw(
        r_3.reshape(N_DEV, 3 * ROW_SHARD, D), w3_shard.reshape(3 * ROW_SHARD, D),
        stack3(m_w_out_conv, m_w_out_attn, m_w_o).reshape(3 * ROW_SHARD, D),
        stack3(v_w_out_conv, v_w_out_attn, v_w_o).reshape(3 * ROW_SHARD, D), ROW_SHARD, "adamw_w3")

    def pack(ng, bm, fg):
        return pad8(jnp.concatenate([ng, bm.reshape(2, D), fg.reshape(1, D)], axis=0))

    d_s, nm_s, nv_s = _adamw(small, pack(norm_g, b_merge, final_g), pack(m_norm_g, m_b_merge, m_final_g),
                             pack(v_norm_g, v_b_merge, v_final_g), "adamw_small")
    g_cw = lax.dynamic_slice(small, (4, me * ROW_SHARD), (3, ROW_SHARD))
    d_cw, nm_cw, nv_cw = _adamw(g_cw, conv_w[0], m_conv_w[0], v_conv_w[0], "adamw_conv_w")

    loss = small[7, 0]
    split3 = lambda t: tuple(t[a * ROW_SHARD:(a + 1) * ROW_SHARD][None] for a in range(3))
    unpack = lambda t: (t[0:1], t[1:3].reshape(1, 2 * D), t[3])

    def leaves(in_, small_, cw_, w3_):
        ng, bm, fg = unpack(small_)
        wc, wa, wo = split3(w3_)
        return (ng, in_[None], bm, cw_[None], wc, wa, wo, fg)

    return (loss, grad_x[None],
            *leaves(g_w_in, small, g_cw, g_w3),
            *leaves(d_w_in, d_s, d_cw, d_w3),
            *leaves(nm_w_in, nm_s, nm_cw, nm_w3),
            *leaves(nv_w_in, nv_s, nv_cw, nv_w3))
```

```python
import functools

import jax
import jax.numpy as jnp
from jax import lax
from jax.experimental import pallas as pl
from jax.experimental.pallas import tpu as pltpu

D = 1024
N_HEADS = 16
HEAD_DIM = 64
N_SEG = 10
IN_COLS = N_SEG * D
N_DEV = 8
W_IN_SHARD = IN_COLS // N_DEV
ROW_SHARD = D // N_DEV
QB = 128
DILATIONS = (1, 4, 16)
EPS = 1e-6
NEG = -1e30
BF = jnp.bfloat16
F32 = jnp.float32
MESH = pl.DeviceIdType.MESH

ADAM_LR = 0.001
ADAM_B1 = 0.9
ADAM_B2 = 0.999
ADAM_EPS = 1e-08
ADAM_WD = 0.01
ADAM_STEP = 10

V7X_VMEM_BYTES = 64 * 1024 * 1024
VMEM_LIMIT = V7X_VMEM_BYTES - 8 * 1024 * 1024
ROW_TILE = 256

VMEM_SPEC = pl.BlockSpec(memory_space=pltpu.VMEM)
ANY_SPEC = pl.BlockSpec(memory_space=pl.ANY)
SMEM_SPEC = pl.BlockSpec(memory_space=pltpu.SMEM)


def _params(n_grid_axes, vmem=VMEM_LIMIT):
    return pltpu.CompilerParams(dimension_semantics=("arbitrary",) * n_grid_axes, vmem_limit_bytes=vmem)


def _dot(a, b):
    return jnp.dot(a, b, preferred_element_type=F32)


def _dot_nt(a, b):
    return lax.dot_general(a, b, (((1,), (1,)), ((), ())), preferred_element_type=F32)


def _dot_tn(a, b):
    return lax.dot_general(a, b, (((0,), (0,)), ((), ())), preferred_element_type=F32)


def _sigmoid(z):
    return 1.0 / (1.0 + jnp.exp(-z))


def _my_place():
    x, y, c = lax.axis_index("x"), lax.axis_index("y"), lax.axis_index("c")
    return x, y, c, 4 * x + 2 * y + c


def _peers(x, y, c):
    out = []
    for k in range(1, N_DEV):
        px = 1 - x if k & 4 else x
        py = 1 - y if k & 2 else y
        pc = 1 - c if k & 1 else c
        out.append(((px, py, pc), 4 * px + 2 * py + pc))
    return out


def _device(p):
    return (p >> 2, (p >> 1) & 1, p & 1)


def _shard_cols(*ranges):
    def cols(p):
        found = None
        for lo, hi in ranges:
            a, b = max(lo, p * W_IN_SHARD), min(hi, (p + 1) * W_IN_SHARD)
            if a < b:
                assert found is None
                found = (a - p * W_IN_SHARD, b - p * W_IN_SHARD)
        return found

    return cols


def _whole(p):
    return ()


def _block(ref, idx, cols):
    return ref.at[idx] if cols == () else ref.at[idx, :, cols[0]:cols[1]]


class _WeightGather:
    def __init__(self, src, dst, send_sems, forward_sems, recv_sems, cols):
        self.src, self.dst, self.cols = src, dst, cols
        self.send_sems, self.forward_sems, self.recv_sems = send_sems, forward_sems, recv_sems
        self.me = _my_place()[3]

    def _copy(self, p, target, passing_on=False):
        cols = self.cols(p)
        return pltpu.make_async_remote_copy(
            src_ref=_block(self.dst, p, cols) if passing_on else self.src(p, cols), dst_ref=_block(self.dst, p, cols),
            send_sem=self.forward_sems.at[p] if passing_on else self.send_sems.at[target],
            recv_sem=self.recv_sems.at[p], device_id=_device(target), device_id_type=MESH)

    def _as_each_device(self, own, relayed, other):
        for m in range(N_DEV):
            def branch(m=m):
                for p in range(N_DEV):
                    if self.cols(p) is None:
                        continue
                    if p == m:
                        for t in [m ^ 1] + [q for q in range(N_DEV) if q >> 1 != m >> 1 and q & 1 == m & 1]:
                            own(self._copy(m, t))
                    elif p >> 1 != m >> 1 and p & 1 == m & 1:
                        relayed(p, m ^ 1)
                    else:
                        other(p)

            pl.when(self.me == m)(branch)

    def start(self):
        self._as_each_device(lambda cp: cp.start(), lambda p, t: None, lambda p: None)

    def forward(self):
        def pass_on(p, t):
            self._copy(p, p).wait_recv()
            self._copy(p, t, passing_on=True).start()

        self._as_each_device(lambda cp: None, pass_on, lambda p: None)

    def finish(self):
        self._as_each_device(lambda cp: cp.wait_send(), lambda p, t: self._copy(p, t, passing_on=True).wait_send(),
                             lambda p: self._copy(p, p).wait_recv())


WEIGHT_GATHER_SEMS = [pltpu.SemaphoreType.DMA((N_DEV,))] * 3
FORWARD_STEP = 6
REST_COLS = _shard_cols((0, 4 * D), (7 * D, IN_COLS))
HEAD_PAIRS = D // 128


def _qkv_piece(h, seg):
    col = (4 + seg) * D + 128 * h
    return col // W_IN_SHARD, col % W_IN_SHARD


class _PieceGather:
    def __init__(self, src, dst, send_sems, recv_sems):
        self.src, self.dst, self.send_sems, self.recv_sems = src, dst, send_sems, recv_sems
        self.me = _my_place()[3]

    def _copy(self, i, target):
        p, lo = _qkv_piece(i // 3, i % 3)
        return pltpu.make_async_remote_copy(
            src_ref=self.src(p, lo, lo + 128), dst_ref=self.dst.at[p, :, lo:lo + 128], send_sem=self.send_sems.at[i, target],
            recv_sem=self.recv_sems.at[i], device_id=_device(target), device_id_type=MESH)

    def _owner(self, i, act):
        p = _qkv_piece(i // 3, i % 3)[0]

        def sender():
            for k in range(N_DEV - 1):
                act(self._copy(i, (p + 1 + (k + i) % (N_DEV - 1)) % N_DEV))

        pl.when(self.me == p)(sender)

    def start(self, pieces):
        for i in pieces:
            self._owner(i, lambda cp: cp.start())

    def wait_send(self, pieces):
        for i in pieces:
            self._owner(i, lambda cp: cp.wait_send())

    def wait_recv(self, pieces):
        for i in pieces:
            p = _qkv_piece(i // 3, i % 3)[0]
            pl.when(self.me != p)(lambda i=i, p=p: self._copy(i, p).wait_recv())


def _piece_sems(n):
    return [pltpu.SemaphoreType.DMA((n, N_DEV)), pltpu.SemaphoreType.DMA((n,))]


def _norm_gather_first_weights(x2, norm_g, w_in, w3, cw):
    S = x2.shape[0]
    tm = ROW_TILE
    nsteps = S // tm

    def body(x_ref, g_ref, w_in_ref, w3_ref, cw_ref, u_ref, ut_ref, o_in, o_3, o_cw, in_bf, w3_bf, local_sems, *sems):
        i = pl.program_id(0)
        me = _my_place()[3]
        gather = _PieceGather(lambda p, lo, hi: in_bf.at[:, lo:hi], o_in, *sems)
        local = [pltpu.make_async_copy(src, dst.at[me], local_sems.at[a])
                 for a, (src, dst) in enumerate(((in_bf, o_in), (w3_bf, o_3), (cw_ref, o_cw)))]

        @pl.when(i == 0)
        def _():
            def cast_rows(r, carry):
                rows = pl.ds(pl.multiple_of(r * 128, 128), 128)
                in_bf[rows, :] = w_in_ref[rows, :].astype(BF)
                return carry

            lax.fori_loop(0, D // 128, cast_rows, 0)
            for a in range(3):
                w3_bf[a] = w3_ref[a].astype(BF)
            gather.start(range(3))
            for cp in local:
                cp.start()

        xv = x_ref[...]
        r = lax.rsqrt(jnp.mean(xv * xv, axis=-1, keepdims=True) + EPS)
        u = xv * r * g_ref[...]
        u_ref[...] = u.astype(BF)
        ut_ref[...] = u.T.astype(BF)

        @pl.when(i == nsteps - 1)
        def _():
            gather.wait_recv(range(3))
            gather.wait_send(range(3))
            for cp in local:
                cp.wait()

    return pl.pallas_call(
        body, name="norm_gather_first_weights", grid=(nsteps,),
        out_shape=(jax.ShapeDtypeStruct((S, D), BF), jax.ShapeDtypeStruct((D, S), BF),
                   jax.ShapeDtypeStruct((N_DEV, D, W_IN_SHARD), BF),
                   jax.ShapeDtypeStruct((N_DEV, 3, ROW_SHARD, D), BF),
                   jax.ShapeDtypeStruct((N_DEV, 8, 128), F32)),
        in_specs=[pl.BlockSpec((tm, D), lambda i: (i, 0)), pl.BlockSpec((1, D), lambda i: (0, 0)),
                  VMEM_SPEC, VMEM_SPEC, VMEM_SPEC],
        out_specs=(pl.BlockSpec((tm, D), lambda i: (i, 0)), pl.BlockSpec((D, tm), lambda i: (0, i)),
                   ANY_SPEC, ANY_SPEC, ANY_SPEC),
        scratch_shapes=[pltpu.VMEM((D, W_IN_SHARD), BF), pltpu.VMEM((3, ROW_SHARD, D), BF),
                        pltpu.SemaphoreType.DMA((3,))] + _piece_sems(3),
        compiler_params=_params(1),
    )(x2, norm_g, w_in, w3, cw)


class _GradExchange:
    def __init__(self, src, dst, send_sems, recv_sems, local_sem, cols):
        self.src, self.dst, self.cols = src, dst, cols
        self.send_sems, self.recv_sems, self.local_sem = send_sems, recv_sems, local_sem
        self.me = _my_place()[3]

    def _remote(self, p, source):
        return pltpu.make_async_remote_copy(
            src_ref=_block(self.src, p, self.cols(p)), dst_ref=_block(self.dst, source, self.cols(p)),
            send_sem=self.send_sems.at[p], recv_sem=self.recv_sems.at[source],
            device_id=_device(p), device_id_type=MESH)

    def _local(self, p):
        return pltpu.make_async_copy(_block(self.src, p, self.cols(p)), _block(self.dst, p, self.cols(p)),
                                     self.local_sem)

    def _as_each_device(self, send, local, receive):
        for m in range(N_DEV):
            def branch(m=m):
                for k in range(1, N_DEV):
                    p = (m + k) % N_DEV
                    if self.cols(p) is not None:
                        send(self._remote(p, m))
                if self.cols(m) is not None:
                    if self.local_sem is not None:
                        local(self._local(m))
                    for k in range(1, N_DEV):
                        receive(self._remote(m, (m + k) % N_DEV))

            pl.when(self.me == m)(branch)

    def start(self):
        self._as_each_device(lambda cp: cp.start(), lambda cp: cp.start(), lambda cp: None)

    def finish(self):
        self._as_each_device(lambda cp: cp.wait_send(), lambda cp: cp.wait(), lambda cp: cp.wait_recv())


GRAD_EXCHANGE_SEMS = [pltpu.SemaphoreType.DMA((N_DEV,)), pltpu.SemaphoreType.DMA((N_DEV,)), pltpu.SemaphoreType.DMA]


def _allreduce_small(p_mid, p_conv, p_norm):
    def body(a_ref, b_ref, c_ref, out_ref, mine, gathered, send_sems, recv_sems):
        x, y, c, me = _my_place()
        mine[...] = a_ref[...] + b_ref[...] + c_ref[...]
        gathered[me] = mine[...]
        remote = []
        for k, (peer, _) in enumerate(_peers(x, y, c)):
            cp = pltpu.make_async_remote_copy(
                src_ref=mine, dst_ref=gathered.at[me], send_sem=send_sems.at[k], recv_sem=recv_sems.at[k],
                device_id=peer, device_id_type=MESH)
            cp.start()
            remote.append(cp)
        for cp in remote:
            cp.wait()
        total = gathered[0]
        for s in range(1, N_DEV):
            total = total + gathered[s]
        out_ref[...] = total

    return pl.pallas_call(
        body, name="allreduce_small",
        out_shape=jax.ShapeDtypeStruct((8, D), F32),
        in_specs=[VMEM_SPEC, VMEM_SPEC, VMEM_SPEC], out_specs=VMEM_SPEC,
        scratch_shapes=[pltpu.VMEM((8, D), F32), pltpu.VMEM((N_DEV, 8, D), F32),
                        pltpu.SemaphoreType.DMA((N_DEV - 1,)), pltpu.SemaphoreType.DMA((N_DEV - 1,))],
    )(p_mid, p_conv, p_norm)


def _proj_pieces():
    cuts = sorted(set(range(0, IN_COLS + 1, D)) | set(range(0, IN_COLS + 1, W_IN_SHARD)))
    return [(lo // D, lo % D, lo // W_IN_SHARD, lo % W_IN_SHARD, hi - lo) for lo, hi in zip(cuts[:-1], cuts[1:])]


PROJ_TN = 256


def _proj_cols(u, w_all, seg0, n_seg, dtype, name):
    S = u.shape[0]
    tn = PROJ_TN
    per_shard = W_IN_SHARD // tn
    tile0 = seg0 * D // tn

    def body(u_ref, w_ref, out_ref):
        out_ref[...] = _dot(u_ref[...], w_ref[0]).astype(dtype)

    return pl.pallas_call(
        body, name=name, grid=(n_seg * D // tn,),
        in_specs=[VMEM_SPEC, pl.BlockSpec((1, D, tn), lambda t: ((tile0 + t) // per_shard, 0, (tile0 + t) % per_shard))],
        out_specs=pl.BlockSpec((S, tn), lambda t: (0, t)),
        out_shape=jax.ShapeDtypeStruct((S, n_seg * D), dtype),
        compiler_params=_params(1),
    )(u, w_all)


CONV_TM, CONV_TC = 256, 512
HALO = 16


def _conv_fwd(pa, cw8):
    S = pa.shape[0]
    tm, tc = CONV_TM, CONV_TC
    nct = D // tc

    def seg(s):
        return pl.BlockSpec((tm, tc), lambda i, j, s=s: (i, s * nct + j))

    def halo_before(s):
        return pl.BlockSpec((HALO, tc), lambda i, j, s=s: (jnp.maximum(i * (tm // HALO) - 1, 0), s * nct + j))

    def body(xc, bg, cg, zc, xch, cgh, cw, out):
        i = pl.program_id(0)
        a = cg[...].astype(F32) * xc[...].astype(F32)
        ah = cgh[...].astype(F32) * xch[...].astype(F32)
        ah = jnp.where(i > 0, ah, 0.0)
        row = lax.broadcasted_iota(jnp.int32, (tm, tc), 0)
        a1 = jnp.where(row == 0, ah[HALO - 1:HALO, :], pltpu.roll(a, 1, 0))
        a2 = jnp.where(row == 0, ah[HALO - 2:HALO - 1, :],
                       jnp.where(row == 1, ah[HALO - 1:HALO, :], pltpu.roll(a, 2, 0)))
        w = cw[...]
        conv = w[0:1, :] * a2 + w[1:2, :] * a1 + w[2:3, :] * a
        z = zc[...].astype(F32)
        out[...] = (z * _sigmoid(z) * bg[...].astype(F32) * conv).astype(BF)

    return pl.pallas_call(
        body, name="conv_fwd", grid=(S // tm, nct),
        in_specs=[seg(0), seg(1), seg(2), seg(3), halo_before(0), halo_before(2),
                  pl.BlockSpec((8, tc), lambda i, j: (0, j))],
        out_specs=pl.BlockSpec((tm, tc), lambda i, j: (i, j)),
        out_shape=jax.ShapeDtypeStruct((S, D), BF),
        compiler_params=_params(2),
    )(pa, pa, pa, pa, pa, pa, cw8)


ATT_UNROLL = 32


LAYOUT_MOD = 4
RUN = QB // LAYOUT_MOD


def _fold_masks(d):
    row = lax.broadcasted_iota(jnp.int32, (QB, QB), 0)
    lane = lax.broadcasted_iota(jnp.int32, (QB, QB), 1)
    if d == 1:
        qpos, kpos = LAYOUT_MOD * (row % RUN) + row // RUN, LAYOUT_MOD * (lane % RUN) + lane // RUN
    else:
        qpos, kpos = row, lane
    tri_le = kpos <= qpos
    dist = jnp.where(tri_le, qpos - kpos, qpos - kpos + QB).astype(F32)
    return tri_le, dist, lane < HEAD_DIM


class _Rows:
    def __init__(self, slices):
        self.slices = slices

    def get(self, ref):
        parts = [ref[sl, :] for sl in self.slices]
        return parts[0] if len(parts) == 1 else jnp.concatenate(parts, axis=0)

    def put(self, ref, val):
        size = QB // len(self.slices)
        for g, sl in enumerate(self.slices):
            ref[sl, :] = val if len(self.slices) == 1 else val[g * size:(g + 1) * size]

    def add(self, ref, val):
        self.put(ref, self.get(ref) + val)


def _block_rows(b, d, S):
    quarter = S // LAYOUT_MOD
    nb = S // (QB * d)
    r, n = b // nb, b % nb
    n_prev = jnp.maximum(n - 1, 0)
    if d == 1:
        runs = lambda m: _Rows([pl.ds(pl.multiple_of(g * quarter + RUN * m, RUN), RUN) for g in range(LAYOUT_MOD)])
        return n, runs(n), runs(n_prev)
    if d == LAYOUT_MOD:
        block = lambda m: _Rows([pl.ds(pl.multiple_of(r * quarter + QB * m, QB), QB)])
        return n, block(n), block(n_prev)
    step = d // LAYOUT_MOD
    first = (r % LAYOUT_MOD) * quarter + r // LAYOUT_MOD
    strided = lambda m: _Rows([pl.ds(first + QB * step * m, QB, stride=step)])
    return n, strided(n), strided(n_prev)


def _natural_rows(i, S):
    per = S // LAYOUT_MOD // QB
    return pl.ds(i // per + LAYOUT_MOD * QB * (i % per), QB, stride=LAYOUT_MOD)


def _head_sum_matrix():
    r = lax.broadcasted_iota(jnp.int32, (2 * QB, 2 * QB), 0)
    c = lax.broadcasted_iota(jnp.int32, (2 * QB, 2 * QB), 1)
    return (((r % QB) // HEAD_DIM) == (c // QB)).astype(F32).astype(BF)


def _hi_lo(t):
    hi = t.astype(BF)
    return jnp.concatenate([hi, (t - hi.astype(F32)).astype(BF)], axis=1)


PROJ_ROWS = 512


def _attn_fwd(u, slopes, w_all, w3_all, cw_all):
    S = u.shape[0]
    hpr = HEAD_PAIRS
    n_blocks = S // QB
    later = range(3, 3 * hpr)

    def body(sl_ref, u_ref, w_in_ref, w3_in_ref, cw_in_ref, o_ref, lse_ref, q_ref, k_ref, v_ref, w_ref, w3_ref,
             cw_ref, acc, m_s, l_s, w_tile, staged, tile_sems, *sems):
        hp = pl.program_id(0)
        me = _my_place()[3]
        pieces = _PieceGather(lambda p, lo, hi: w_ref.at[p, :, lo:hi], w_ref, *sems[0:2])
        gathers = (_WeightGather(lambda p, cols: _block(w_ref, me, cols), w_ref, *sems[2:5], REST_COLS),
                   _WeightGather(lambda p, cols: w3_ref.at[me], w3_ref, *sems[5:8], _whole),
                   _WeightGather(lambda p, cols: cw_ref.at[me], cw_ref, *sems[8:11], _whole))

        @pl.when(hp == 0)
        def _():
            pieces.start(later[:6])
            for g in gathers:
                g.start()
            pieces.start(later[6:])

        @pl.when(hp == FORWARD_STEP)
        def _():
            for g in gathers:
                g.forward()

        for h in range(hpr):
            @pl.when(hp == h)
            def _(h=h):
                if h > 0:
                    pieces.wait_recv(range(3 * h, 3 * h + 3))
                fetch = []
                for seg in range(3):
                    p, lo = _qkv_piece(h, seg)
                    fetch.append(pltpu.make_async_copy(w_ref.at[p, :, lo:lo + 128], w_tile.at[:, seg * 128:(seg + 1) * 128],
                                                       tile_sems.at[seg]))
                    fetch[-1].start()
                for cp in fetch:
                    cp.wait()

        def project(i, carry):
            rows = pl.ds(pl.multiple_of(i * PROJ_ROWS, PROJ_ROWS), PROJ_ROWS)
            qkv = _dot(u_ref[rows, :], w_tile[...])
            per = PROJ_ROWS // LAYOUT_MOD
            for seg, ref in enumerate((q_ref, k_ref, v_ref)):
                staged[seg] = qkv[:, seg * 128:(seg + 1) * 128]
                for g in range(LAYOUT_MOD):
                    dst = pl.ds(pl.multiple_of(g * (S // LAYOUT_MOD) + i * per, per), per)
                    ref[dst, :] = staged.at[seg][pl.ds(g, per, stride=LAYOUT_MOD), :]
            return carry

        lax.fori_loop(0, S // PROJ_ROWS, project, 0)

        head_sum = _head_sum_matrix()
        ones_b = jnp.ones((2 * QB, QB), BF)
        m_s[...] = jnp.full(m_s.shape, NEG, F32)
        l_s[...] = jnp.zeros(l_s.shape, F32)
        acc[...] = jnp.zeros(acc.shape, F32)

        for d in DILATIONS:
            tri_le, dist, low = _fold_masks(d)
            low_b = low.astype(F32).astype(BF)
            high_b = 1.0 - low_b
            slope = [sl_ref[2 * hp + a] * float(d) for a in range(2)]
            bias = [slope[a] * dist for a in range(2)]

            def block(b, d=d, slope=slope, bias=bias, tri_le=tri_le, low=low, low_b=low_b, high_b=high_b):
                n, cur, prev = _block_rows(b, d, S)
                has_prev = n > 0
                valid = jnp.logical_or(tri_le, has_prev)
                q2 = (cur.get(q_ref) * 0.125).astype(BF)
                qs = jnp.concatenate([q2 * low_b, q2 * high_b], axis=0)
                vp = prev.get(v_ref)
                kp_b = prev.get(k_ref).astype(BF)
                kcat = jnp.concatenate([kp_b, cur.get(k_ref).astype(BF)], axis=0)
                vcat = jnp.concatenate([vp, cur.get(v_ref)], axis=0).astype(BF)
                s2 = _dot_nt(qs, kcat)
                e2 = _dot(_hi_lo(q2.astype(F32) * kp_b.astype(F32)), head_sum)
                p_rows, alpha_h, pe_h = [], [], []
                for a in range(2):
                    sp, sc = s2[a * QB:(a + 1) * QB, :QB], s2[a * QB:(a + 1) * QB, QB:]
                    comb = jnp.where(valid, jnp.where(tri_le, sc, sp) - bias[a], NEG)
                    e = jnp.where(has_prev, e2[:, a * QB:(a + 1) * QB] - slope[a] * float(QB), NEG)
                    m_old = cur.get(m_s.at[a])
                    m_new = jnp.maximum(jnp.maximum(m_old, jnp.max(comb, axis=-1, keepdims=True)), e)
                    cur.put(m_s.at[a], m_new)
                    p = jnp.exp(comb - m_new)
                    pe_h.append(jnp.exp(e - m_new))
                    alpha_h.append(jnp.exp(m_old - m_new))
                    p_rows.append(jnp.concatenate([jnp.where(tri_le, 0.0, p).astype(BF),
                                                   jnp.where(tri_le, p, 0.0).astype(BF)], axis=1))
                pv = _dot(jnp.concatenate(p_rows, axis=0), jnp.concatenate([vcat, ones_b], axis=1))
                for a in range(2):
                    cur.put(l_s.at[a], alpha_h[a] * cur.get(l_s.at[a]) + pv[a * QB:(a + 1) * QB, QB:] + pe_h[a])
                cur.put(acc, jnp.where(low, alpha_h[0], alpha_h[1]) * cur.get(acc)
                        + jnp.where(low, pv[:QB, :QB], pv[QB:, :QB]) + jnp.where(low, pe_h[0], pe_h[1]) * vp)

            def several(it, carry, block=block):
                for u in range(ATT_UNROLL):
                    block(it * ATT_UNROLL + u)
                return carry

            lax.fori_loop(0, n_blocks // ATT_UNROLL, several, 0)

        low = _fold_masks(LAYOUT_MOD)[2]

        def finish(i, carry):
            rows = pl.ds(pl.multiple_of(i * QB, QB), QB)
            l0, l1 = l_s[0, rows, :], l_s[1, rows, :]
            o_ref[_natural_rows(i, S), :] = acc[rows, :] / jnp.where(low, l0, l1)
            lse_ref[0, rows, :] = m_s[0, rows, :] + jnp.log(l0)
            lse_ref[1, rows, :] = m_s[1, rows, :] + jnp.log(l1)
            return carry

        lax.fori_loop(0, n_blocks, finish, 0)

        @pl.when(hp == hpr - 1)
        def _():
            pieces.wait_send(later)
            for g in gathers:
                g.finish()

    col = pl.BlockSpec((S, 128), lambda h: (0, h))
    act = jax.ShapeDtypeStruct((S, D), F32)
    gathered = (w_all, w3_all, cw_all)
    return pl.pallas_call(
        body, name="attn_fwd", grid=(hpr,),
        in_specs=[SMEM_SPEC, VMEM_SPEC, ANY_SPEC, ANY_SPEC, ANY_SPEC],
        out_specs=(col, pl.BlockSpec((2, S, 128), lambda h: (0, 0, h)), col, col, col, ANY_SPEC, ANY_SPEC, ANY_SPEC),
        out_shape=(act, jax.ShapeDtypeStruct((2, S, D), F32), act, act, act,
                   *[jax.ShapeDtypeStruct(t.shape, t.dtype) for t in gathered]),
        scratch_shapes=([pltpu.VMEM((S, 128), F32), pltpu.VMEM((2, S, 128), F32), pltpu.VMEM((2, S, 128), F32),
                         pltpu.VMEM((D, 3 * 128), BF), pltpu.VMEM((3, PROJ_ROWS, 128), F32),
                         pltpu.SemaphoreType.DMA((3,))]
                        + _piece_sems(3 * hpr) + WEIGHT_GATHER_SEMS * 3),
        input_output_aliases={2: 5, 3: 6, 4: 7},
        compiler_params=_params(1),
    )(slopes, u, *gathered)


def _set_rows(shape, rows):
    idx = lax.broadcasted_iota(jnp.int32, shape, 0)
    out = jnp.zeros(shape, F32)
    for r, val in rows.items():
        out = out + jnp.where(idx == r, val, 0.0)
    return out


def _mid(yc_in, pa_mid, o, x2, target, b_merge, final_g, w3):
    S = x2.shape[0]
    tm = ROW_TILE
    nsteps = S // tm
    tile = pl.BlockSpec((tm, D), lambda i: (i, 0))

    def body(yc_ref, za_ref, gcp_ref, gap_ref, o_ref, x_ref, t_ref, b_ref, fg_ref, w_ref,
             dh_ref, dmid_ref, do_ref, dyc_ref, gw_ref, small_ref, acc, stage):
        i = pl.program_id(0)

        @pl.when(i == 0)
        def _():
            acc[...] = jnp.zeros_like(acc)
            small_ref[...] = jnp.zeros_like(small_ref)

        wc, wa, wo = w_ref[0], w_ref[1], w_ref[2]
        z = za_ref[...].astype(F32)
        sg = _sigmoid(z)
        ov = o_ref[...]
        yc_in_b, ya_in_b = yc_ref[...], (z * sg * ov).astype(BF)
        yc = _dot(yc_in_b, wc)
        ya = _dot(ya_in_b, wa)
        b = b_ref[...]
        gc = _sigmoid(gcp_ref[...].astype(F32) + b[:, :D])
        ga = _sigmoid(gap_ref[...].astype(F32) + b[:, D:])
        merged = gc * yc + ga * ya
        merged_b = merged.astype(BF)
        h = x_ref[...] + _dot(merged_b, wo)
        r2 = lax.rsqrt(jnp.mean(h * h, axis=-1, keepdims=True) + EPS)
        n = h * r2
        fg = fg_ref[...]
        err = n * fg - t_ref[...]
        loss = 0.5 * jnp.sum(jnp.sum(err * err, axis=-1, keepdims=True) / D, axis=0, keepdims=True)
        dy = err / D
        g_fg = jnp.sum(dy * n, axis=0, keepdims=True)
        dn = dy * fg
        dh = r2 * (dn - n * jnp.mean(dn * n, axis=-1, keepdims=True))
        dh_ref[...] = dh
        dh_b = dh.astype(BF)
        dmerged = _dot_nt(dh_b, wo)
        acc[2] += _dot(merged.T.astype(BF), dh_b)
        dyc = (dmerged * gc).astype(BF)
        dya = (dmerged * ga).astype(BF)
        dgcp = dmerged * yc * gc * (1.0 - gc)
        dgap = dmerged * ya * ga * (1.0 - ga)
        dmid_ref[1] = dgcp.astype(BF)
        dmid_ref[2] = dgap.astype(BF)
        acc[0] += _dot(yc_in_b.astype(F32).T.astype(BF), dyc)
        acc[1] += _dot(ya_in_b.astype(F32).T.astype(BF), dya)
        dyc_ref[...] = _dot_nt(dyc, wc).astype(BF)
        dya_in = _dot_nt(dya, wa)
        do_ref[...] = dya_in * (z * sg)
        dmid_ref[0] = (dya_in * ov * (sg * (1.0 + z * (1.0 - sg)))).astype(BF)
        small_ref[...] += _set_rows((8, D), {
            1: jnp.sum(dgcp, axis=0, keepdims=True), 2: jnp.sum(dgap, axis=0, keepdims=True),
            3: g_fg, 7: jnp.broadcast_to(loss, (1, D))})

        @pl.when(i == nsteps - 1)
        def _():
            for p in range(N_DEV):
                for a in range(3):
                    stage[...] = acc[a, p * ROW_SHARD:(p + 1) * ROW_SHARD, :].astype(BF)
                    pltpu.sync_copy(stage, gw_ref.at[p, a])

    return pl.pallas_call(
        body, name="mid", grid=(nsteps,),
        in_specs=[tile, pl.BlockSpec((tm, D), lambda i: (i, 0)), pl.BlockSpec((tm, D), lambda i: (i, 1)),
                  pl.BlockSpec((tm, D), lambda i: (i, 2)), tile, tile, tile,
                  pl.BlockSpec((1, 2 * D), lambda i: (0, 0)), pl.BlockSpec((1, D), lambda i: (0, 0)), VMEM_SPEC],
        out_specs=(tile, pl.BlockSpec((3, tm, D), lambda i: (0, i, 0)), tile, tile,
                   ANY_SPEC, pl.BlockSpec((8, D), lambda i: (0, 0))),
        out_shape=(jax.ShapeDtypeStruct((S, D), F32), jax.ShapeDtypeStruct((3, S, D), BF),
                   jax.ShapeDtypeStruct((S, D), F32), jax.ShapeDtypeStruct((S, D), BF),
                   jax.ShapeDtypeStruct((N_DEV, 3, ROW_SHARD, D), BF), jax.ShapeDtypeStruct((8, D), F32)),
        scratch_shapes=[pltpu.VMEM((3, D, D), F32), pltpu.VMEM((ROW_SHARD, D), BF)],
        compiler_params=_params(1),
    )(yc_in, pa_mid, pa_mid, pa_mid, o, x2, target, b_merge, final_g, w3)


def _conv_bwd(dyc_in, pa, cw8):
    S = pa.shape[0]
    tm, tc = CONV_TM, CONV_TC
    nct = D // tc
    nrt = S // tm
    last_halo = S // HALO - 1

    def seg(s):
        return pl.BlockSpec((tm, tc), lambda j, i, s=s: (i, s * nct + j))

    def halo_before(s):
        return pl.BlockSpec((HALO, tc), lambda j, i, s=s: (jnp.maximum(i * (tm // HALO) - 1, 0), s * nct + j))

    def halo_after(s):
        return pl.BlockSpec((HALO, tc), lambda j, i, s=s: (jnp.minimum((i + 1) * (tm // HALO), last_halo), s * nct + j))

    def body(dy, xc, bg, cg, zc, xch, cgh, dyn, bgn, zcn, cw, dout, gcw):
        i = pl.program_id(1)

        @pl.when(i == 0)
        def _():
            gcw[...] = jnp.zeros_like(gcw)

        xcv, cgv = xc[...].astype(F32), cg[...].astype(F32)
        a = cgv * xcv
        ah = jnp.where(i > 0, cgh[...].astype(F32) * xch[...].astype(F32), 0.0)
        row = lax.broadcasted_iota(jnp.int32, (tm, tc), 0)
        a1 = jnp.where(row == 0, ah[HALO - 1:HALO, :], pltpu.roll(a, 1, 0))
        a2 = jnp.where(row == 0, ah[HALO - 2:HALO - 1, :],
                       jnp.where(row == 1, ah[HALO - 1:HALO, :], pltpu.roll(a, 2, 0)))
        w = cw[...]
        conv = w[0:1, :] * a2 + w[1:2, :] * a1 + w[2:3, :] * a
        z = zc[...].astype(F32)
        sg = _sigmoid(z)
        silu = z * sg
        bgv = bg[...].astype(F32)
        dyv = dy[...].astype(F32)
        dout[3] = (dyv * bgv * conv * (sg * (1.0 + z * (1.0 - sg)))).astype(BF)
        dout[1] = (dyv * silu * conv).astype(BF)
        dc = dyv * silu * bgv
        zn = zcn[...].astype(F32)
        dcn = dyn[...].astype(F32) * (zn * _sigmoid(zn)) * bgn[...].astype(F32)
        dcn = jnp.where(i < nrt - 1, dcn, 0.0)
        dc1 = jnp.where(row == tm - 1, dcn[0:1, :], pltpu.roll(dc, tm - 1, 0))
        dc2 = jnp.where(row == tm - 1, dcn[1:2, :],
                        jnp.where(row == tm - 2, dcn[0:1, :], pltpu.roll(dc, tm - 2, 0)))
        da = w[2:3, :] * dc + w[1:2, :] * dc1 + w[0:1, :] * dc2
        dout[2] = (da * xcv).astype(BF)
        dout[0] = (da * cgv).astype(BF)
        gcw[...] += _set_rows((8, tc), {
            4: jnp.sum(dc * a2, axis=0, keepdims=True), 5: jnp.sum(dc * a1, axis=0, keepdims=True),
            6: jnp.sum(dc * a, axis=0, keepdims=True)})

    return pl.pallas_call(
        body, name="conv_bwd", grid=(nct, nrt),
        in_specs=[pl.BlockSpec((tm, tc), lambda j, i: (i, j)), seg(0), seg(1), seg(2), seg(3),
                  halo_before(0), halo_before(2),
                  pl.BlockSpec((HALO, tc), lambda j, i: (jnp.minimum((i + 1) * (tm // HALO), last_halo), j)),
                  halo_after(1), halo_after(3), pl.BlockSpec((8, tc), lambda j, i: (0, j))],
        out_specs=(pl.BlockSpec((4, tm, tc), lambda j, i: (0, i, j)), pl.BlockSpec((8, tc), lambda j, i: (0, j))),
        out_shape=(jax.ShapeDtypeStruct((4, S, D), BF), jax.ShapeDtypeStruct((8, D), F32)),
        compiler_params=_params(2),
    )(dyc_in, pa, pa, pa, pa, pa, pa, dyc_in, pa, pa, cw8)


def _attn_bwd(q, k, v, slopes, do, o, lse, g_in, g_3):
    S = q.shape[0]
    hpr = HEAD_PAIRS
    n_blocks = S // QB

    def body(sl_ref, q_ref, k_ref, v_ref, do_ref, o_ref, lse_ref, gin_ref, g3_ref, out_ref, rin_ref, r3_ref,
             dq_s, dk_s, dv_s, do_s, dd_s, *sems):
        hp = pl.program_id(0)
        exchanges = (_GradExchange(gin_ref, rin_ref, *sems[:3], _shard_cols((0, SEG0_ATTN * D), (SEG0_MID * D, IN_COLS))),
                     _GradExchange(g3_ref, r3_ref, *sems[3:], _whole))

        @pl.when(hp == 0)
        def _():
            for ex in exchanges:
                ex.start()

        head_sum = _head_sum_matrix()
        dq_s[...] = jnp.zeros(dq_s.shape, F32)
        dk_s[...] = jnp.zeros(dk_s.shape, F32)
        dv_s[...] = jnp.zeros(dv_s.shape, F32)

        def row_dots(i, carry):
            rows = pl.ds(pl.multiple_of(i * QB, QB), QB)
            natural = _natural_rows(i, S)
            do_c = do_ref[natural, :]
            do_s[rows, :] = do_c
            dd = _dot(_hi_lo(do_c * o_ref[natural, :]), head_sum)
            dd_s[0, rows, :] = dd[:, :QB]
            dd_s[1, rows, :] = dd[:, QB:]
            return carry

        lax.fori_loop(0, n_blocks, row_dots, 0)

        for d in DILATIONS:
            tri_le, dist, low = _fold_masks(d)
            low_b = low.astype(F32).astype(BF)
            high_b = 1.0 - low_b
            slope = [sl_ref[2 * hp + a] * float(d) for a in range(2)]
            bias = [slope[a] * dist for a in range(2)]

            def block(b, d=d, slope=slope, bias=bias, tri_le=tri_le, low=low, low_b=low_b, high_b=high_b):
                n, cur, prev = _block_rows(b, d, S)
                has_prev = n > 0
                valid = jnp.logical_or(tri_le, has_prev)
                q2f = cur.get(q_ref) * 0.125
                q2 = q2f.astype(BF)
                qs = jnp.concatenate([q2 * low_b, q2 * high_b], axis=0)
                kp, vp = prev.get(k_ref), prev.get(v_ref)
                kp_b, vp_b = kp.astype(BF), vp.astype(BF)
                kcat = jnp.concatenate([kp_b, cur.get(k_ref).astype(BF)], axis=0)
                vcat = jnp.concatenate([vp_b, cur.get(v_ref).astype(BF)], axis=0)
                do2f = cur.get(do_s)
                do2 = do2f.astype(BF)
                dos = jnp.concatenate([do2 * low_b, do2 * high_b], axis=0)
                s2 = _dot_nt(qs, kcat)
                dp2 = _dot_nt(dos, vcat)
                diag2 = _dot(jnp.concatenate([_hi_lo(q2.astype(F32) * kp_b.astype(F32)),
                                              _hi_lo(do2.astype(F32) * vp_b.astype(F32))], axis=0), head_sum)
                p_rows, ds_rows, pe_h, dse_h = [], [], [], []
                for a in range(2):
                    hs = slice(a * QB, (a + 1) * QB)
                    sp, sc = s2[hs, :QB], s2[hs, QB:]
                    dpp, dpc = dp2[hs, :QB], dp2[hs, QB:]
                    lse_a, dd_a = cur.get(lse_ref.at[a]), cur.get(dd_s.at[a])
                    comb = jnp.where(tri_le, sc, sp) - bias[a]
                    e = diag2[:QB, hs] - slope[a] * float(QB)
                    p = jnp.where(valid, jnp.exp(comb - lse_a), 0.0)
                    pe = jnp.where(has_prev, jnp.exp(e - lse_a), 0.0)
                    ds = p * (jnp.where(tri_le, dpc, dpp) - dd_a)
                    dse_h.append(pe * (diag2[QB:, hs] - dd_a))
                    pe_h.append(pe)
                    p_rows.append(jnp.concatenate([jnp.where(tri_le, 0.0, p).astype(BF),
                                                   jnp.where(tri_le, p, 0.0).astype(BF)], axis=1))
                    ds_rows.append(jnp.concatenate([jnp.where(tri_le, 0.0, ds).astype(BF),
                                                    jnp.where(tri_le, ds, 0.0).astype(BF)], axis=1))
                pst = jnp.concatenate(p_rows, axis=0)
                dst = jnp.concatenate(ds_rows, axis=0)
                pe2 = jnp.where(low, pe_h[0], pe_h[1])
                dse2 = jnp.where(low, dse_h[0], dse_h[1])
                dq = _dot(dst, kcat)
                cur.add(dq_s, (jnp.where(low, dq[:QB], dq[QB:]) + dse2 * kp) * 0.125)
                dk = _dot_tn(dst, qs)
                dv = _dot_tn(pst, dos)
                prev.add(dk_s, dk[:QB] + dse2 * q2f)
                cur.add(dk_s, dk[QB:])
                prev.add(dv_s, dv[:QB] + pe2 * do2f)
                cur.add(dv_s, dv[QB:])

            def several(it, carry, block=block):
                for u in range(ATT_UNROLL):
                    block(it * ATT_UNROLL + u)
                return carry

            lax.fori_loop(0, n_blocks // ATT_UNROLL, several, 0)

        def finish(i, carry):
            rows = pl.ds(pl.multiple_of(i * QB, QB), QB)
            natural = _natural_rows(i, S)
            for t, ref in enumerate((dq_s, dk_s, dv_s)):
                out_ref.at[t][natural, :] = ref[rows, :]
            return carry

        lax.fori_loop(0, n_blocks, finish, 0)

        @pl.when(hp == hpr - 1)
        def _():
            for ex in exchanges:
                ex.finish()

    col = pl.BlockSpec((S, 128), lambda h: (0, h))
    return pl.pallas_call(
        body, name="attn_bwd", grid=(hpr,),
        in_specs=[SMEM_SPEC, col, col, col, col, col, pl.BlockSpec((2, S, 128), lambda h: (0, 0, h)),
                  ANY_SPEC, ANY_SPEC],
        out_specs=(pl.BlockSpec((3, S, 128), lambda h: (0, 0, h)), ANY_SPEC, ANY_SPEC),
        out_shape=(jax.ShapeDtypeStruct((3, S, D), F32), jax.ShapeDtypeStruct(g_in.shape, BF),
                   jax.ShapeDtypeStruct(g_3.shape, BF)),
        scratch_shapes=([pltpu.VMEM((S, 128), F32)] * 4 + [pltpu.VMEM((2, S, 128), F32)]
                        + GRAD_EXCHANGE_SEMS + GRAD_EXCHANGE_SEMS),
        compiler_params=_params(1),
    )(slopes, q, k, v, do, o, lse, g_in, g_3)


WG_TN = 256
SEG0_CONV, SEG0_ATTN, SEG0_MID = 0, 4, 7


def _wgrad_in(ut, d_group, seg0, g_in, name):
    S = ut.shape[1]
    tn = WG_TN
    per_seg = D // tn
    per_shard = W_IN_SHARD // tn
    n_tiles = d_group.shape[0] * per_seg
    tile0 = seg0 * per_seg

    def body(ut_ref, d_ref, *rest):
        rest[-1][0] = _dot(ut_ref[...], d_ref[0].astype(BF)).astype(BF)

    operands, in_specs, aliases = [ut, d_group], [VMEM_SPEC, pl.BlockSpec((1, S, tn), lambda t: (t // per_seg, 0, t % per_seg))], {}
    if g_in is not None:
        operands.append(g_in)
        in_specs.append(ANY_SPEC)
        aliases = {2: 0}
    return pl.pallas_call(
        body, name=name, grid=(n_tiles,), in_specs=in_specs,
        out_specs=pl.BlockSpec((1, D, tn), lambda t: ((tile0 + t) // per_shard, 0, (tile0 + t) % per_shard)),
        out_shape=jax.ShapeDtypeStruct((N_DEV, D, W_IN_SHARD), BF),
        input_output_aliases=aliases,
        compiler_params=_params(1),
    )(*operands)


def _dgrad_norm_bwd(d_conv, d_attn, d_mid, w_all, x2, dh, norm_g):
    S = x2.shape[0]
    tm = ROW_TILE
    nsteps = S // tm
    tile = pl.BlockSpec((tm, D), lambda i: (i, 0))
    pieces = _proj_pieces()

    def body(a_ref, b_ref, c_ref, w_ref, x_ref, dh_ref, g_ref, gx_ref, small_ref):
        i = pl.program_id(0)

        @pl.when(i == 0)
        def _():
            small_ref[...] = jnp.zeros_like(small_ref)

        groups = (a_ref, b_ref, c_ref)
        du = jnp.zeros((tm, D), F32)
        for s, sc, p, pc, width in pieces:
            g = 0 if s < 4 else (1 if s < 7 else 2)
            local = s - (0, 4, 7)[g]
            du = du + _dot_nt(groups[g][local, :, sc:sc + width].astype(BF), w_ref[p, :, pc:pc + width])
        xv = x_ref[...]
        r = lax.rsqrt(jnp.mean(xv * xv, axis=-1, keepdims=True) + EPS)
        n = xv * r
        dn = du * g_ref[...]
        gx_ref[...] = dh_ref[...] + r * (dn - n * jnp.mean(dn * n, axis=-1, keepdims=True))
        small_ref[...] += _set_rows((8, D), {0: jnp.sum(du * n, axis=0, keepdims=True)})

    return pl.pallas_call(
        body, name="dgrad_norm_bwd", grid=(nsteps,),
        in_specs=[pl.BlockSpec((4, tm, D), lambda i: (0, i, 0)), pl.BlockSpec((3, tm, D), lambda i: (0, i, 0)),
                  pl.BlockSpec((3, tm, D), lambda i: (0, i, 0)), VMEM_SPEC, tile, tile,
                  pl.BlockSpec((1, D), lambda i: (0, 0))],
        out_specs=(tile, pl.BlockSpec((8, D), lambda i: (0, 0))),
        out_shape=(jax.ShapeDtypeStruct((S, D), F32), jax.ShapeDtypeStruct((8, D), F32)),
        compiler_params=_params(1),
    )(d_conv, d_attn, d_mid, w_all, x2, dh, norm_g)


HBM_SPEC = pl.BlockSpec(memory_space=pltpu.HBM)
SEM_SPEC = pl.BlockSpec(memory_space=pltpu.SEMAPHORE)
ATTN_COLS = _shard_cols((SEG0_ATTN * D, SEG0_MID * D))


def _attn_cols_exchange_start(g_in, r_in):
    def body(g_ref, r_ref, send_sems, recv_sems, g_thru, r_thru, token):
        _GradExchange(g_ref, r_ref, send_sems, recv_sems, None, ATTN_COLS).start()
        token[...] = jnp.zeros_like(token)

    hbm = pltpu.with_memory_space_constraint
    return pl.pallas_call(
        body, name="attn_cols_exchange_start",
        out_shape=(pltpu.SemaphoreType.DMA((N_DEV,)), pltpu.SemaphoreType.DMA((N_DEV,)),
                   pltpu.HBM(g_in.shape, g_in.dtype), pltpu.HBM(r_in.shape, r_in.dtype),
                   jax.ShapeDtypeStruct((8, 128), F32)),
        in_specs=(HBM_SPEC, HBM_SPEC), out_specs=(SEM_SPEC, SEM_SPEC, HBM_SPEC, HBM_SPEC, VMEM_SPEC),
        input_output_aliases={0: 2, 1: 3},
        compiler_params=pltpu.CompilerParams(has_side_effects=pltpu.SideEffectType.DATAFLOW_SIDE_EFFECTING),
    )(hbm(g_in, pltpu.HBM), hbm(r_in, pltpu.HBM))


def _attn_cols_exchange_wait(send_sems, recv_sems, g_thru, r_thru, after):
    def body(g_ref, r_ref, send_sems, recv_sems, after_ref, g_dead, r_out):
        _GradExchange(g_ref, r_ref, send_sems, recv_sems, None, ATTN_COLS).finish()

    return pl.pallas_call(
        body, name="attn_cols_exchange_wait",
        out_shape=(pltpu.HBM(g_thru.shape, g_thru.dtype), pltpu.HBM(r_thru.shape, r_thru.dtype)),
        in_specs=(HBM_SPEC, HBM_SPEC, SEM_SPEC, SEM_SPEC, ANY_SPEC), out_specs=(HBM_SPEC, HBM_SPEC),
        input_output_aliases={0: 0, 1: 1},
        compiler_params=pltpu.CompilerParams(has_side_effects=pltpu.SideEffectType.DATAFLOW_SIDE_EFFECTING),
    )(g_thru, r_thru, send_sems, recv_sems, after)


def _adamw_math(w, g, m, v):
    m = ADAM_B1 * m + (1.0 - ADAM_B1) * g
    v = ADAM_B2 * v + (1.0 - ADAM_B2) * (g * g)
    m_hat = m / (1.0 - ADAM_B1 ** ADAM_STEP)
    v_hat = v / (1.0 - ADAM_B2 ** ADAM_STEP)
    delta = -ADAM_LR * (m_hat / (jnp.sqrt(v_hat) + ADAM_EPS) + ADAM_WD * w)
    return delta, m, v


def _sum_adamw(parts, w, m, v, tm, name):
    R, C = w.shape
    tile = pl.BlockSpec((tm, C), lambda i: (i, 0))

    def body(p_ref, w_ref, m_ref, v_ref, g_out, d_out, m_out, v_out):
        g = p_ref[0].astype(F32)
        for s in range(1, N_DEV):
            g = g + p_ref[s].astype(F32)
        g_out[...] = g
        d_out[...], m_out[...], v_out[...] = _adamw_math(w_ref[...], g, m_ref[...], v_ref[...])

    shape = jax.ShapeDtypeStruct((R, C), F32)
    return pl.pallas_call(
        body, name=name, grid=(R // tm,),
        in_specs=[pl.BlockSpec((N_DEV, tm, C), lambda i: (0, i, 0)), tile, tile, tile],
        out_specs=(tile, tile, tile, tile), out_shape=(shape, shape, shape, shape),
        compiler_params=_params(1),
    )(parts, w, m, v)


def _adamw(g, w, m, v, name):
    def body(g_ref, w_ref, m_ref, v_ref, d_out, m_out, v_out):
        d_out[...], m_out[...], v_out[...] = _adamw_math(w_ref[...], g_ref[...], m_ref[...], v_ref[...])

    shape = jax.ShapeDtypeStruct(w.shape, F32)
    return pl.pallas_call(
        body, name=name, in_specs=[VMEM_SPEC] * 4, out_specs=(VMEM_SPEC,) * 3, out_shape=(shape, shape, shape),
    )(g, w, m, v)


def _alibi_slopes():
    return jnp.exp2(-8.0 * jnp.arange(1, N_HEADS + 1, dtype=F32) / N_HEADS)


def _local_step(x2, target, norm_g, b_merge, final_g, w_in, w3_shard, cw_shard):
    slopes = _alibi_slopes()
    u, ut, w_all, w3_all, cw_all = _norm_gather_first_weights(x2, norm_g, w_in, w3_shard, cw_shard)
    o, lse, q, k, v, w_all, w3_all, cw_all = _attn_fwd(u, slopes, w_all, w3_all, cw_all)
    w3 = jnp.transpose(w3_all, (1, 0, 2, 3)).reshape(3, D, D)
    cw8 = jnp.transpose(cw_all, (1, 0, 2)).reshape(8, D)
    pa = _proj_cols(u, w_all, SEG0_CONV, 4, BF, "proj_conv")
    yc_in = _conv_fwd(pa, cw8)
    pa_mid = _proj_cols(u, w_all, SEG0_MID, 3, BF, "proj_mid")
    dh, d_mid, do, dyc_in, g_3, small_mid = _mid(yc_in, pa_mid, o, x2, target, b_merge, final_g, w3)
    g_in = _wgrad_in(ut, d_mid, SEG0_MID, None, "wgrad_in_mid")
    d_conv, small_conv = _conv_bwd(dyc_in, pa, cw8)
    g_in = _wgrad_in(ut, d_conv, SEG0_CONV, g_in, "wgrad_in_conv")
    d_attn, r_in, r_3 = _attn_bwd(q, k, v, slopes, do, o, lse, g_in, g_3)
    g_in = _wgrad_in(ut, d_attn, SEG0_ATTN, g_in, "wgrad_in_attn")
    *in_flight, token = _attn_cols_exchange_start(g_in, r_in)
    grad_x, small_norm = _dgrad_norm_bwd(d_conv, d_attn, d_mid, w_all, x2, dh, norm_g + token[0:1, 0:1])
    return grad_x, in_flight, r_3, small_mid, small_conv, small_norm


def kernel(x, norm_g, w_in, b_merge, conv_w, w_out_conv, w_out_attn, w_o, final_g, loss_target, m_norm_g, m_w_in, m_b_merge, m_conv_w, m_w_out_conv, m_w_out_attn, m_w_o, m_final_g, v_norm_g, v_w_in, v_b_merge, v_conv_w, v_w_out_conv, v_w_out_attn, v_w_o, v_final_g):
    me = 4 * lax.axis_index("x") + 2 * lax.axis_index("y") + lax.axis_index("c")
    stack3 = lambda a, b, c: jnp.concatenate([a, b, c], axis=0)
    pad8 = lambda a: jnp.pad(a, ((0, 8 - a.shape[0]), (0, 0)))

    w3_shard = stack3(w_out_conv, w_out_attn, w_o)
    final_g2 = final_g.reshape(1, D)
    grad_x, in_flight, r_3, small_mid, small_conv, small_norm = _local_step(
        x[0], loss_target[0], norm_g, b_merge, final_g2, w_in[0], w3_shard, pad8(conv_w[0]))

    small = _allreduce_small(small_mid, small_conv, small_norm)
    g_in, r_in = _attn_cols_exchange_wait(*in_flight, small)
    own = lax.dynamic_index_in_dim(g_in, me, 0, keepdims=True)
    r_in = lax.dynamic_update_slice(r_in, own, (me, 0, 0))

    g_w_in, d_w_in, nm_w_in, nv_w_in = _sum_adamw(r_in, w_in[0], m_w_in[0], v_w_in[0], 128, "adamw_w_in")
    g_w3, d_w3, nm_w3, nv_w3 = _sum_adamw(
        r_3.reshape(N_DEV, 3 * ROW_SHARD, D), w3_shard.reshape(3 * ROW_SHARD, D),
        stack3(m_w_out_conv, m_w_out_attn, m_w_o).reshape(3 * ROW_SHARD, D),
        stack3(v_w_out_conv, v_w_out_attn, v_w_o).reshape(3 * ROW_SHARD, D), ROW_SHARD, "adamw_w3")

    def pack(ng, bm, fg):
        return pad8(jnp.concatenate([ng, bm.reshape(2, D), fg.reshape(1, D)], axis=0))

    d_s, nm_s, nv_s = _adamw(small, pack(norm_g, b_merge, final_g), pack(m_norm_g, m_b_merge, m_final_g),
                             pack(v_norm_g, v_b_merge, v_final_g), "adamw_small")
    g_cw = lax.dynamic_slice(small, (4, me * ROW_SHARD), (3, ROW_SHARD))
    d_cw, nm_cw, nv_cw = _adamw(g_cw, conv_w[0], m_conv_w[0], v_conv_w[0], "adamw_conv_w")

    loss = small[7, 0]
    split3 = lambda t: tuple(t[a * ROW_SHARD:(a + 1) * ROW_SHARD][None] for a in range(3))
    unpack = lambda t: (t[0:1], t[1:3].reshape(1, 2 * D), t[3])

    def leaves(in_, small_, cw_, w3_):
        ng, bm, fg = unpack(small_)
        wc, wa, wo = split3(w3_)
        return (ng, in_[None], bm, cw_[None], wc, wa, wo, fg)

    return (loss, grad_x[None],
            *leaves(g_w_in, small, g_cw, g_w3),
            *leaves(d_w_in, d_s, d_cw, d_w3),
            *leaves(nm_w_in, nm_s, nm_cw, nm_w3),
            *leaves(nv_w_in, nv_s, nv_cw, nv_w3))
```

```python
import functools

import jax
import jax.numpy as jnp
from jax import lax
from jax.experimental import pallas as pl
from jax.experimental.pallas import tpu as pltpu

D = 1024
N_HEADS = 16
HEAD_DIM = 64
N_SEG = 10
IN_COLS = N_SEG * D
N_DEV = 8
W_IN_SHARD = IN_COLS // N_DEV
ROW_SHARD = D // N_DEV
QB = 128
DILATIONS = (1, 4, 16)
EPS = 1e-6
NEG = -1e30
BF = jnp.bfloat16
F32 = jnp.float32
MESH = pl.DeviceIdType.MESH

ADAM_LR = 0.001
ADAM_B1 = 0.9
ADAM_B2 = 0.999
ADAM_EPS = 1e-08
ADAM_WD = 0.01
ADAM_STEP = 10

V7X_VMEM_BYTES = 64 * 1024 * 1024
VMEM_LIMIT = V7X_VMEM_BYTES - 8 * 1024 * 1024
ROW_TILE = 256

VMEM_SPEC = pl.BlockSpec(memory_space=pltpu.VMEM)
ANY_SPEC = pl.BlockSpec(memory_space=pl.ANY)
SMEM_SPEC = pl.BlockSpec(memory_space=pltpu.SMEM)


def _params(n_grid_axes, vmem=VMEM_LIMIT):
    return pltpu.CompilerParams(dimension_semantics=("arbitrary",) * n_grid_axes, vmem_limit_bytes=vmem)


def _dot(a, b):
    return jnp.dot(a, b, preferred_element_type=F32)


def _dot_nt(a, b):
    return lax.dot_general(a, b, (((1,), (1,)), ((), ())), preferred_element_type=F32)


def _dot_tn(a, b):
    return lax.dot_general(a, b, (((0,), (0,)), ((), ())), preferred_element_type=F32)


def _sigmoid(z):
    return 1.0 / (1.0 + jnp.exp(-z))


def _my_place():
    x, y, c = lax.axis_index("x"), lax.axis_index("y"), lax.axis_index("c")
    return x, y, c, 4 * x + 2 * y + c


def _peers(x, y, c):
    out = []
    for k in range(1, N_DEV):
        px = 1 - x if k & 4 else x
        py = 1 - y if k & 2 else y
        pc = 1 - c if k & 1 else c
        out.append(((px, py, pc), 4 * px + 2 * py + pc))
    return out


def _device(p):
    return (p >> 2, (p >> 1) & 1, p & 1)


def _shard_cols(*ranges):
    def cols(p):
        found = None
        for lo, hi in ranges:
            a, b = max(lo, p * W_IN_SHARD), min(hi, (p + 1) * W_IN_SHARD)
            if a < b:
                assert found is None
                found = (a - p * W_IN_SHARD, b - p * W_IN_SHARD)
        return found

    return cols


def _whole(p):
    return ()


def _block(ref, idx, cols):
    return ref.at[idx] if cols == () else ref.at[idx, :, cols[0]:cols[1]]


class _WeightGather:
    def __init__(self, src, dst, send_sems, forward_sems, recv_sems, cols):
        self.src, self.dst, self.cols = src, dst, cols
        self.send_sems, self.forward_sems, self.recv_sems = send_sems, forward_sems, recv_sems
        self.me = _my_place()[3]

    def _copy(self, p, target, passing_on=False):
        cols = self.cols(p)
        return pltpu.make_async_remote_copy(
            src_ref=_block(self.dst, p, cols) if passing_on else self.src(p, cols), dst_ref=_block(self.dst, p, cols),
            send_sem=self.forward_sems.at[p] if passing_on else self.send_sems.at[target],
            recv_sem=self.recv_sems.at[p], device_id=_device(target), device_id_type=MESH)

    def _as_each_device(self, own, relayed, other):
        for m in range(N_DEV):
            def branch(m=m):
                for p in range(N_DEV):
                    if self.cols(p) is None:
                        continue
                    if p == m:
                        for t in [m ^ 1] + [q for q in range(N_DEV) if q >> 1 != m >> 1 and q & 1 == m & 1]:
                            own(self._copy(m, t))
                    elif p >> 1 != m >> 1 and p & 1 == m & 1:
                        relayed(p, m ^ 1)
                    else:
                        other(p)

            pl.when(self.me == m)(branch)

    def start(self):
        self._as_each_device(lambda cp: cp.start(), lambda p, t: None, lambda p: None)

    def forward(self):
        def pass_on(p, t):
            self._copy(p, p).wait_recv()
            self._copy(p, t, passing_on=True).start()

        self._as_each_device(lambda cp: None, pass_on, lambda p: None)

    def finish(self):
        self._as_each_device(lambda cp: cp.wait_send(), lambda p, t: self._copy(p, t, passing_on=True).wait_send(),
                             lambda p: self._copy(p, p).wait_recv())


WEIGHT_GATHER_SEMS = [pltpu.SemaphoreType.DMA((N_DEV,))] * 3
FORWARD_STEP = 6
REST_COLS = _shard_cols((0, 4 * D), (7 * D, IN_COLS))
HEAD_PAIRS = D // 128


def _qkv_piece(h, seg):
    col = (4 + seg) * D + 128 * h
    return col // W_IN_SHARD, col % W_IN_SHARD


class _PieceGather:
    def __init__(self, src, dst, send_sems, recv_sems):
        self.src, self.dst, self.send_sems, self.recv_sems = src, dst, send_sems, recv_sems
        self.me = _my_place()[3]

    def _copy(self, i, target):
        p, lo = _qkv_piece(i // 3, i % 3)
        return pltpu.make_async_remote_copy(
            src_ref=self.src(p, lo, lo + 128), dst_ref=self.dst.at[p, :, lo:lo + 128], send_sem=self.send_sems.at[i, target],
            recv_sem=self.recv_sems.at[i], device_id=_device(target), device_id_type=MESH)

    def _owner(self, i, act):
        p = _qkv_piece(i // 3, i % 3)[0]

        def sender():
            for k in range(N_DEV - 1):
                act(self._copy(i, (p + 1 + (k + i) % (N_DEV - 1)) % N_DEV))

        pl.when(self.me == p)(sender)

    def start(self, pieces):
        for i in pieces:
            self._owner(i, lambda cp: cp.start())

    def wait_send(self, pieces):
        for i in pieces:
            self._owner(i, lambda cp: cp.wait_send())

    def wait_recv(self, pieces):
        for i in pieces:
            p = _qkv_piece(i // 3, i % 3)[0]
            pl.when(self.me != p)(lambda i=i, p=p: self._copy(i, p).wait_recv())


def _piece_sems(n):
    return [pltpu.SemaphoreType.DMA((n, N_DEV)), pltpu.SemaphoreType.DMA((n,))]


def _norm_gather_first_weights(x2, norm_g, w_in, w3, cw):
    S = x2.shape[0]
    tm = ROW_TILE
    nsteps = S // tm

    def body(x_ref, g_ref, w_in_ref, w3_ref, cw_ref, u_ref, ut_ref, o_in, o_3, o_cw, in_bf, w3_bf, local_sems, *sems):
        i = pl.program_id(0)
        me = _my_place()[3]
        gather = _PieceGather(lambda p, lo, hi: in_bf.at[:, lo:hi], o_in, *sems)
        local = [pltpu.make_async_copy(src, dst.at[me], local_sems.at[a])
                 for a, (src, dst) in enumerate(((in_bf, o_in), (w3_bf, o_3), (cw_ref, o_cw)))]

        @pl.when(i == 0)
        def _():
            def cast_rows(r, carry):
                rows = pl.ds(pl.multiple_of(r * 128, 128), 128)
                in_bf[rows, :] = w_in_ref[rows, :].astype(BF)
                return carry

            lax.fori_loop(0, D // 128, cast_rows, 0)
            for a in range(3):
                w3_bf[a] = w3_ref[a].astype(BF)
            gather.start(range(3))
            for cp in local:
                cp.start()

        xv = x_ref[...]
        r = lax.rsqrt(jnp.mean(xv * xv, axis=-1, keepdims=True) + EPS)
        u = xv * r * g_ref[...]
        u_ref[...] = u.astype(BF)
        ut_ref[...] = u.T.astype(BF)

        @pl.when(i == nsteps - 1)
        def _():
            gather.wait_recv(range(3))
            gather.wait_send(range(3))
            for cp in local:
                cp.wait()

    return pl.pallas_call(
        body, name="norm_gather_first_weights", grid=(nsteps,),
        out_shape=(jax.ShapeDtypeStruct((S, D), BF), jax.ShapeDtypeStruct((D, S), BF),
                   jax.ShapeDtypeStruct((N_DEV, D, W_IN_SHARD), BF),
                   jax.ShapeDtypeStruct((N_DEV, 3, ROW_SHARD, D), BF),
                   jax.ShapeDtypeStruct((N_DEV, 8, 128), F32)),
        in_specs=[pl.BlockSpec((tm, D), lambda i: (i, 0)), pl.BlockSpec((1, D), lambda i: (0, 0)),
                  VMEM_SPEC, VMEM_SPEC, VMEM_SPEC],
        out_specs=(pl.BlockSpec((tm, D), lambda i: (i, 0)), pl.BlockSpec((D, tm), lambda i: (0, i)),
                   ANY_SPEC, ANY_SPEC, ANY_SPEC),
        scratch_shapes=[pltpu.VMEM((D, W_IN_SHARD), BF), pltpu.VMEM((3, ROW_SHARD, D), BF),
                        pltpu.SemaphoreType.DMA((3,))] + _piece_sems(3),
        compiler_params=_params(1),
    )(x2, norm_g, w_in, w3, cw)


class _GradExchange:
    def __init__(self, src, dst, send_sems, recv_sems, local_sem, cols):
        self.src, self.dst, self.cols = src, dst, cols
        self.send_sems, self.recv_sems, self.local_sem = send_sems, recv_sems, local_sem
        self.me = _my_place()[3]

    def _remote(self, p, source):
        return pltpu.make_async_remote_copy(
            src_ref=_block(self.src, p, self.cols(p)), dst_ref=_block(self.dst, source, self.cols(p)),
            send_sem=self.send_sems.at[p], recv_sem=self.recv_sems.at[source],
            device_id=_device(p), device_id_type=MESH)

    def _local(self, p):
        return pltpu.make_async_copy(_block(self.src, p, self.cols(p)), _block(self.dst, p, self.cols(p)),
                                     self.local_sem)

    def _as_each_device(self, send, local, receive):
        for m in range(N_DEV):
            def branch(m=m):
                for k in range(1, N_DEV):
                    p = (m + k) % N_DEV
                    if self.cols(p) is not None:
                        send(self._remote(p, m))
                if self.cols(m) is not None:
                    if self.local_sem is not None:
                        local(self._local(m))
                    for k in range(1, N_DEV):
                        receive(self._remote(m, (m + k) % N_DEV))

            pl.when(self.me == m)(branch)

    def start(self):
        self._as_each_device(lambda cp: cp.start(), lambda cp: cp.start(), lambda cp: None)

    def finish(self):
        self._as_each_device(lambda cp: cp.wait_send(), lambda cp: cp.wait(), lambda cp: cp.wait_recv())


GRAD_EXCHANGE_SEMS = [pltpu.SemaphoreType.DMA((N_DEV,)), pltpu.SemaphoreType.DMA((N_DEV,)), pltpu.SemaphoreType.DMA]


def _allreduce_small(p_mid, p_conv, p_norm):
    def body(a_ref, b_ref, c_ref, out_ref, mine, gathered, send_sems, recv_sems):
        x, y, c, me = _my_place()
        mine[...] = a_ref[...] + b_ref[...] + c_ref[...]
        gathered[me] = mine[...]
        remote = []
        for k, (peer, _) in enumerate(_peers(x, y, c)):
            cp = pltpu.make_async_remote_copy(
                src_ref=mine, dst_ref=gathered.at[me], send_sem=send_sems.at[k], recv_sem=recv_sems.at[k],
                device_id=peer, device_id_type=MESH)
            cp.start()
            remote.append(cp)
        for cp in remote:
            cp.wait()
        total = gathered[0]
        for s in range(1, N_DEV):
            total = total + gathered[s]
        out_ref[...] = total

    return pl.pallas_call(
        body, name="allreduce_small",
        out_shape=jax.ShapeDtypeStruct((8, D), F32),
        in_specs=[VMEM_SPEC, VMEM_SPEC, VMEM_SPEC], out_specs=VMEM_SPEC,
        scratch_shapes=[pltpu.VMEM((8, D), F32), pltpu.VMEM((N_DEV, 8, D), F32),
                        pltpu.SemaphoreType.DMA((N_DEV - 1,)), pltpu.SemaphoreType.DMA((N_DEV - 1,))],
    )(p_mid, p_conv, p_norm)


def _proj_pieces():
    cuts = sorted(set(range(0, IN_COLS + 1, D)) | set(range(0, IN_COLS + 1, W_IN_SHARD)))
    return [(lo // D, lo % D, lo // W_IN_SHARD, lo % W_IN_SHARD, hi - lo) for lo, hi in zip(cuts[:-1], cuts[1:])]


PROJ_TN = 256


def _proj_cols(u, w_all, seg0, n_seg, dtype, name):
    S = u.shape[0]
    tn = PROJ_TN
    per_shard = W_IN_SHARD // tn
    tile0 = seg0 * D // tn

    def body(u_ref, w_ref, out_ref):
        out_ref[...] = _dot(u_ref[...], w_ref[0]).astype(dtype)

    return pl.pallas_call(
        body, name=name, grid=(n_seg * D // tn,),
        in_specs=[VMEM_SPEC, pl.BlockSpec((1, D, tn), lambda t: ((tile0 + t) // per_shard, 0, (tile0 + t) % per_shard))],
        out_specs=pl.BlockSpec((S, tn), lambda t: (0, t)),
        out_shape=jax.ShapeDtypeStruct((S, n_seg * D), dtype),
        compiler_params=_params(1),
    )(u, w_all)


CONV_TM, CONV_TC = 256, 512
HALO = 16


def _conv_fwd(pa, cw8):
    S = pa.shape[0]
    tm, tc = CONV_TM, CONV_TC
    nct = D // tc

    def seg(s):
        return pl.BlockSpec((tm, tc), lambda i, j, s=s: (i, s * nct + j))

    def halo_before(s):
        return pl.BlockSpec((HALO, tc), lambda i, j, s=s: (jnp.maximum(i * (tm // HALO) - 1, 0), s * nct + j))

    def body(xc, bg, cg, zc, xch, cgh, cw, out):
        i = pl.program_id(0)
        a = cg[...].astype(F32) * xc[...].astype(F32)
        ah = cgh[...].astype(F32) * xch[...].astype(F32)
        ah = jnp.where(i > 0, ah, 0.0)
        row = lax.broadcasted_iota(jnp.int32, (tm, tc), 0)
        a1 = jnp.where(row == 0, ah[HALO - 1:HALO, :], pltpu.roll(a, 1, 0))
        a2 = jnp.where(row == 0, ah[HALO - 2:HALO - 1, :],
                       jnp.where(row == 1, ah[HALO - 1:HALO, :], pltpu.roll(a, 2, 0)))
        w = cw[...]
        conv = w[0:1, :] * a2 + w[1:2, :] * a1 + w[2:3, :] * a
        z = zc[...].astype(F32)
        out[...] = (z * _sigmoid(z) * bg[...].astype(F32) * conv).astype(BF)

    return pl.pallas_call(
        body, name="conv_fwd", grid=(S // tm, nct),
        in_specs=[seg(0), seg(1), seg(2), seg(3), halo_before(0), halo_before(2),
                  pl.BlockSpec((8, tc), lambda i, j: (0, j))],
        out_specs=pl.BlockSpec((tm, tc), lambda i, j: (i, j)),
        out_shape=jax.ShapeDtypeStruct((S, D), BF),
        compiler_params=_params(2),
    )(pa, pa, pa, pa, pa, pa, cw8)


ATT_UNROLL = 32


LAYOUT_MOD = 4
RUN = QB // LAYOUT_MOD


def _fold_masks(d):
    row = lax.broadcasted_iota(jnp.int32, (QB, QB), 0)
    lane = lax.broadcasted_iota(jnp.int32, (QB, QB), 1)
    if d == 1:
        qpos, kpos = LAYOUT_MOD * (row % RUN) + row // RUN, LAYOUT_MOD * (lane % RUN) + lane // RUN
    else:
        qpos, kpos = row, lane
    tri_le = kpos <= qpos
    dist = jnp.where(tri_le, qpos - kpos, qpos - kpos + QB).astype(F32)
    return tri_le, dist, lane < HEAD_DIM


class _Rows:
    def __init__(self, slices):
        self.slices = slices

    def get(self, ref):
        parts = [ref[sl, :] for sl in self.slices]
        return parts[0] if len(parts) == 1 else jnp.concatenate(parts, axis=0)

    def put(self, ref, val):
        size = QB // len(self.slices)
        for g, sl in enumerate(self.slices):
            ref[sl, :] = val if len(self.slices) == 1 else val[g * size:(g + 1) * size]

    def add(self, ref, val):
        self.put(ref, self.get(ref) + val)


def _block_rows(b, d, S):
    quarter = S // LAYOUT_MOD
    nb = S // (QB * d)
    r, n = b // nb, b % nb
    n_prev = jnp.maximum(n - 1, 0)
    if d == 1:
        runs = lambda m: _Rows([pl.ds(pl.multiple_of(g * quarter + RUN * m, RUN), RUN) for g in range(LAYOUT_MOD)])
        return n, runs(n), runs(n_prev)
    if d == LAYOUT_MOD:
        block = lambda m: _Rows([pl.ds(pl.multiple_of(r * quarter + QB * m, QB), QB)])
        return n, block(n), block(n_prev)
    step = d // LAYOUT_MOD
    first = (r % LAYOUT_MOD) * quarter + r // LAYOUT_MOD
    strided = lambda m: _Rows([pl.ds(first + QB * step * m, QB, stride=step)])
    return n, strided(n), strided(n_prev)


def _natural_rows(i, S):
    per = S // LAYOUT_MOD // QB
    return pl.ds(i // per + LAYOUT_MOD * QB * (i % per), QB, stride=LAYOUT_MOD)


def _head_sum_matrix():
    r = lax.broadcasted_iota(jnp.int32, (2 * QB, 2 * QB), 0)
    c = lax.broadcasted_iota(jnp.int32, (2 * QB, 2 * QB), 1)
    return (((r % QB) // HEAD_DIM) == (c // QB)).astype(F32).astype(BF)


def _hi_lo(t):
    hi = t.astype(BF)
    return jnp.concatenate([hi, (t - hi.astype(F32)).astype(BF)], axis=1)


PROJ_ROWS = 512


def _attn_fwd(u, slopes, w_all, cw_all):
    S = u.shape[0]
    hpr = HEAD_PAIRS
    n_blocks = S // QB
    later = range(3, 3 * hpr)

    def body(sl_ref, u_ref, w_in_ref, cw_in_ref, o_ref, lse_ref, q_ref, k_ref, v_ref, w_ref,
             cw_ref, acc, m_s, l_s, w_tile, staged, tile_sems, *sems):
        hp = pl.program_id(0)
        me = _my_place()[3]
        pieces = _PieceGather(lambda p, lo, hi: w_ref.at[p, :, lo:hi], w_ref, *sems[0:2])
        gathers = (_WeightGather(lambda p, cols: _block(w_ref, me, cols), w_ref, *sems[2:5], REST_COLS),
                   _WeightGather(lambda p, cols: cw_ref.at[me], cw_ref, *sems[5:8], _whole))

        @pl.when(hp == 0)
        def _():
            pieces.start(later)
            for g in gathers:
                g.start()

        @pl.when(hp == FORWARD_STEP)
        def _():
            for g in gathers:
                g.forward()

        for h in range(hpr):
            @pl.when(hp == h)
            def _(h=h):
                if h > 0:
                    pieces.wait_recv(range(3 * h, 3 * h + 3))
                fetch = []
                for seg in range(3):
                    p, lo = _qkv_piece(h, seg)
                    fetch.append(pltpu.make_async_copy(w_ref.at[p, :, lo:lo + 128], w_tile.at[:, seg * 128:(seg + 1) * 128],
                                                       tile_sems.at[seg]))
                    fetch[-1].start()
                for cp in fetch:
                    cp.wait()

        def project(i, carry):
            rows = pl.ds(pl.multiple_of(i * PROJ_ROWS, PROJ_ROWS), PROJ_ROWS)
            qkv = _dot(u_ref[rows, :], w_tile[...])
            per = PROJ_ROWS // LAYOUT_MOD
            for seg, ref in enumerate((q_ref, k_ref, v_ref)):
                staged[seg] = qkv[:, seg * 128:(seg + 1) * 128]
                for g in range(LAYOUT_MOD):
                    dst = pl.ds(pl.multiple_of(g * (S // LAYOUT_MOD) + i * per, per), per)
                    ref[dst, :] = staged.at[seg][pl.ds(g, per, stride=LAYOUT_MOD), :]
            return carry

        lax.fori_loop(0, S // PROJ_ROWS, project, 0)

        head_sum = _head_sum_matrix()
        ones_b = jnp.ones((2 * QB, QB), BF)
        m_s[...] = jnp.full(m_s.shape, NEG, F32)
        l_s[...] = jnp.zeros(l_s.shape, F32)
        acc[...] = jnp.zeros(acc.shape, F32)

        for d in DILATIONS:
            tri_le, dist, low = _fold_masks(d)
            low_b = low.astype(F32).astype(BF)
            high_b = 1.0 - low_b
            slope = [sl_ref[2 * hp + a] * float(d) for a in range(2)]
            bias = [slope[a] * dist for a in range(2)]

            def block(b, d=d, slope=slope, bias=bias, tri_le=tri_le, low=low, low_b=low_b, high_b=high_b):
                n, cur, prev = _block_rows(b, d, S)
                has_prev = n > 0
                valid = jnp.logical_or(tri_le, has_prev)
                q2 = (cur.get(q_ref) * 0.125).astype(BF)
                qs = jnp.concatenate([q2 * low_b, q2 * high_b], axis=0)
                vp = prev.get(v_ref)
                kp_b = prev.get(k_ref).astype(BF)
                kcat = jnp.concatenate([kp_b, cur.get(k_ref).astype(BF)], axis=0)
                vcat = jnp.concatenate([vp, cur.get(v_ref)], axis=0).astype(BF)
                s2 = _dot_nt(qs, kcat)
                e2 = _dot(_hi_lo(q2.astype(F32) * kp_b.astype(F32)), head_sum)
                p_rows, alpha_h, pe_h = [], [], []
                for a in range(2):
                    sp, sc = s2[a * QB:(a + 1) * QB, :QB], s2[a * QB:(a + 1) * QB, QB:]
                    comb = jnp.where(valid, jnp.where(tri_le, sc, sp) - bias[a], NEG)
                    e = jnp.where(has_prev, e2[:, a * QB:(a + 1) * QB] - slope[a] * float(QB), NEG)
                    m_old = cur.get(m_s.at[a])
                    m_new = jnp.maximum(jnp.maximum(m_old, jnp.max(comb, axis=-1, keepdims=True)), e)
                    cur.put(m_s.at[a], m_new)
                    p = jnp.exp(comb - m_new)
                    pe_h.append(jnp.exp(e - m_new))
                    alpha_h.append(jnp.exp(m_old - m_new))
                    p_rows.append(jnp.concatenate([jnp.where(tri_le, 0.0, p).astype(BF),
                                                   jnp.where(tri_le, p, 0.0).astype(BF)], axis=1))
                pv = _dot(jnp.concatenate(p_rows, axis=0), jnp.concatenate([vcat, ones_b], axis=1))
                for a in range(2):
                    cur.put(l_s.at[a], alpha_h[a] * cur.get(l_s.at[a]) + pv[a * QB:(a + 1) * QB, QB:] + pe_h[a])
                cur.put(acc, jnp.where(low, alpha_h[0], alpha_h[1]) * cur.get(acc)
                        + jnp.where(low, pv[:QB, :QB], pv[QB:, :QB]) + jnp.where(low, pe_h[0], pe_h[1]) * vp)

            def several(it, carry, block=block):
                for u in range(ATT_UNROLL):
                    block(it * ATT_UNROLL + u)
                return carry

            lax.fori_loop(0, n_blocks // ATT_UNROLL, several, 0)

        low = _fold_masks(LAYOUT_MOD)[2]

        def finish(i, carry):
            rows = pl.ds(pl.multiple_of(i * QB, QB), QB)
            l0, l1 = l_s[0, rows, :], l_s[1, rows, :]
            o_ref[_natural_rows(i, S), :] = acc[rows, :] / jnp.where(low, l0, l1)
            lse_ref[0, rows, :] = m_s[0, rows, :] + jnp.log(l0)
            lse_ref[1, rows, :] = m_s[1, rows, :] + jnp.log(l1)
            return carry

        lax.fori_loop(0, n_blocks, finish, 0)

        @pl.when(hp == hpr - 1)
        def _():
            pieces.wait_send(later)
            for g in gathers:
                g.finish()

    col = pl.BlockSpec((S, 128), lambda h: (0, h))
    act = jax.ShapeDtypeStruct((S, D), F32)
    gathered = (w_all, cw_all)
    return pl.pallas_call(
        body, name="attn_fwd", grid=(hpr,),
        in_specs=[SMEM_SPEC, VMEM_SPEC, ANY_SPEC, ANY_SPEC],
        out_specs=(col, pl.BlockSpec((2, S, 128), lambda h: (0, 0, h)), col, col, col, ANY_SPEC, ANY_SPEC),
        out_shape=(act, jax.ShapeDtypeStruct((2, S, D), F32), act, act, act,
                   *[jax.ShapeDtypeStruct(t.shape, t.dtype) for t in gathered]),
        scratch_shapes=([pltpu.VMEM((S, 128), F32), pltpu.VMEM((2, S, 128), F32), pltpu.VMEM((2, S, 128), F32),
                         pltpu.VMEM((D, 3 * 128), BF), pltpu.VMEM((3, PROJ_ROWS, 128), F32),
                         pltpu.SemaphoreType.DMA((3,))]
                        + _piece_sems(3 * hpr) + WEIGHT_GATHER_SEMS * 2),
        input_output_aliases={2: 5, 3: 6},
        compiler_params=_params(1),
    )(slopes, u, *gathered)


def _set_rows(shape, rows):
    idx = lax.broadcasted_iota(jnp.int32, shape, 0)
    out = jnp.zeros(shape, F32)
    for r, val in rows.items():
        out = out + jnp.where(idx == r, val, 0.0)
    return out


def _mid(yc_in, pa_mid, o, x2, target, b_merge, final_g, w3):
    S = x2.shape[0]
    tm = ROW_TILE
    nsteps = S // tm
    tile = pl.BlockSpec((tm, D), lambda i: (i, 0))

    def body(yc_ref, za_ref, gcp_ref, gap_ref, o_ref, x_ref, t_ref, b_ref, fg_ref, w_ref,
             dh_ref, dmid_ref, do_ref, dyc_ref, gw_ref, small_ref, acc, stage):
        i = pl.program_id(0)

        @pl.when(i == 0)
        def _():
            acc[...] = jnp.zeros_like(acc)
            small_ref[...] = jnp.zeros_like(small_ref)

        wc, wa, wo = w_ref[0], w_ref[1], w_ref[2]
        z = za_ref[...].astype(F32)
        sg = _sigmoid(z)
        ov = o_ref[...]
        yc_in_b, ya_in_b = yc_ref[...], (z * sg * ov).astype(BF)
        yc = _dot(yc_in_b, wc)
        ya = _dot(ya_in_b, wa)
        b = b_ref[...]
        gc = _sigmoid(gcp_ref[...].astype(F32) + b[:, :D])
        ga = _sigmoid(gap_ref[...].astype(F32) + b[:, D:])
        merged = gc * yc + ga * ya
        merged_b = merged.astype(BF)
        h = x_ref[...] + _dot(merged_b, wo)
        r2 = lax.rsqrt(jnp.mean(h * h, axis=-1, keepdims=True) + EPS)
        n = h * r2
        fg = fg_ref[...]
        err = n * fg - t_ref[...]
        loss = 0.5 * jnp.sum(jnp.sum(err * err, axis=-1, keepdims=True) / D, axis=0, keepdims=True)
        dy = err / D
        g_fg = jnp.sum(dy * n, axis=0, keepdims=True)
        dn = dy * fg
        dh = r2 * (dn - n * jnp.mean(dn * n, axis=-1, keepdims=True))
        dh_ref[...] = dh
        dh_b = dh.astype(BF)
        dmerged = _dot_nt(dh_b, wo)
        acc[2] += _dot(merged.T.astype(BF), dh_b)
        dyc = (dmerged * gc).astype(BF)
        dya = (dmerged * ga).astype(BF)
        dgcp = dmerged * yc * gc * (1.0 - gc)
        dgap = dmerged * ya * ga * (1.0 - ga)
        dmid_ref[1] = dgcp.astype(BF)
        dmid_ref[2] = dgap.astype(BF)
        acc[0] += _dot(yc_in_b.astype(F32).T.astype(BF), dyc)
        acc[1] += _dot(ya_in_b.astype(F32).T.astype(BF), dya)
        dyc_ref[...] = _dot_nt(dyc, wc).astype(BF)
        dya_in = _dot_nt(dya, wa)
        do_ref[...] = dya_in * (z * sg)
        dmid_ref[0] = (dya_in * ov * (sg * (1.0 + z * (1.0 - sg)))).astype(BF)
        small_ref[...] += _set_rows((8, D), {
            1: jnp.sum(dgcp, axis=0, keepdims=True), 2: jnp.sum(dgap, axis=0, keepdims=True),
            3: g_fg, 7: jnp.broadcast_to(loss, (1, D))})

        @pl.when(i == nsteps - 1)
        def _():
            for p in range(N_DEV):
                for a in range(3):
                    stage[...] = acc[a, p * ROW_SHARD:(p + 1) * ROW_SHARD, :].astype(BF)
                    pltpu.sync_copy(stage, gw_ref.at[p, a])

    return pl.pallas_call(
        body, name="mid", grid=(nsteps,),
        in_specs=[tile, pl.BlockSpec((tm, D), lambda i: (i, 0)), pl.BlockSpec((tm, D), lambda i: (i, 1)),
                  pl.BlockSpec((tm, D), lambda i: (i, 2)), tile, tile, tile,
                  pl.BlockSpec((1, 2 * D), lambda i: (0, 0)), pl.BlockSpec((1, D), lambda i: (0, 0)), VMEM_SPEC],
        out_specs=(tile, pl.BlockSpec((3, tm, D), lambda i: (0, i, 0)), tile, tile,
                   ANY_SPEC, pl.BlockSpec((8, D), lambda i: (0, 0))),
        out_shape=(jax.ShapeDtypeStruct((S, D), F32), jax.ShapeDtypeStruct((3, S, D), BF),
                   jax.ShapeDtypeStruct((S, D), F32), jax.ShapeDtypeStruct((S, D), BF),
                   jax.ShapeDtypeStruct((N_DEV, 3, ROW_SHARD, D), BF), jax.ShapeDtypeStruct((8, D), F32)),
        scratch_shapes=[pltpu.VMEM((3, D, D), F32), pltpu.VMEM((ROW_SHARD, D), BF)],
        compiler_params=_params(1),
    )(yc_in, pa_mid, pa_mid, pa_mid, o, x2, target, b_merge, final_g, w3)


def _conv_bwd(dyc_in, pa, cw8):
    S = pa.shape[0]
    tm, tc = CONV_TM, CONV_TC
    nct = D // tc
    nrt = S // tm
    last_halo = S // HALO - 1

    def seg(s):
        return pl.BlockSpec((tm, tc), lambda j, i, s=s: (i, s * nct + j))

    def halo_before(s):
        return pl.BlockSpec((HALO, tc), lambda j, i, s=s: (jnp.maximum(i * (tm // HALO) - 1, 0), s * nct + j))

    def halo_after(s):
        return pl.BlockSpec((HALO, tc), lambda j, i, s=s: (jnp.minimum((i + 1) * (tm // HALO), last_halo), s * nct + j))

    def body(dy, xc, bg, cg, zc, xch, cgh, dyn, bgn, zcn, cw, dout, gcw):
        i = pl.program_id(1)

        @pl.when(i == 0)
        def _():
            gcw[...] = jnp.zeros_like(gcw)

        xcv, cgv = xc[...].astype(F32), cg[...].astype(F32)
        a = cgv * xcv
        ah = jnp.where(i > 0, cgh[...].astype(F32) * xch[...].astype(F32), 0.0)
        row = lax.broadcasted_iota(jnp.int32, (tm, tc), 0)
        a1 = jnp.where(row == 0, ah[HALO - 1:HALO, :], pltpu.roll(a, 1, 0))
        a2 = jnp.where(row == 0, ah[HALO - 2:HALO - 1, :],
                       jnp.where(row == 1, ah[HALO - 1:HALO, :], pltpu.roll(a, 2, 0)))
        w = cw[...]
        conv = w[0:1, :] * a2 + w[1:2, :] * a1 + w[2:3, :] * a
        z = zc[...].astype(F32)
        sg = _sigmoid(z)
        silu = z * sg
        bgv = bg[...].astype(F32)
        dyv = dy[...].astype(F32)
        dout[3] = (dyv * bgv * conv * (sg * (1.0 + z * (1.0 - sg)))).astype(BF)
        dout[1] = (dyv * silu * conv).astype(BF)
        dc = dyv * silu * bgv
        zn = zcn[...].astype(F32)
        dcn = dyn[...].astype(F32) * (zn * _sigmoid(zn)) * bgn[...].astype(F32)
        dcn = jnp.where(i < nrt - 1, dcn, 0.0)
        dc1 = jnp.where(row == tm - 1, dcn[0:1, :], pltpu.roll(dc, tm - 1, 0))
        dc2 = jnp.where(row == tm - 1, dcn[1:2, :],
                        jnp.where(row == tm - 2, dcn[0:1, :], pltpu.roll(dc, tm - 2, 0)))
        da = w[2:3, :] * dc + w[1:2, :] * dc1 + w[0:1, :] * dc2
        dout[2] = (da * xcv).astype(BF)
        dout[0] = (da * cgv).astype(BF)
        gcw[...] += _set_rows((8, tc), {
            4: jnp.sum(dc * a2, axis=0, keepdims=True), 5: jnp.sum(dc * a1, axis=0, keepdims=True),
            6: jnp.sum(dc * a, axis=0, keepdims=True)})

    return pl.pallas_call(
        body, name="conv_bwd", grid=(nct, nrt),
        in_specs=[pl.BlockSpec((tm, tc), lambda j, i: (i, j)), seg(0), seg(1), seg(2), seg(3),
                  halo_before(0), halo_before(2),
                  pl.BlockSpec((HALO, tc), lambda j, i: (jnp.minimum((i + 1) * (tm // HALO), last_halo), j)),
                  halo_after(1), halo_after(3), pl.BlockSpec((8, tc), lambda j, i: (0, j))],
        out_specs=(pl.BlockSpec((4, tm, tc), lambda j, i: (0, i, j)), pl.BlockSpec((8, tc), lambda j, i: (0, j))),
        out_shape=(jax.ShapeDtypeStruct((4, S, D), BF), jax.ShapeDtypeStruct((8, D), F32)),
        compiler_params=_params(2),
    )(dyc_in, pa, pa, pa, pa, pa, pa, dyc_in, pa, pa, cw8)


def _attn_bwd(q, k, v, slopes, do, o, lse, g_in, g_3):
    S = q.shape[0]
    hpr = HEAD_PAIRS
    n_blocks = S // QB

    def body(sl_ref, q_ref, k_ref, v_ref, do_ref, o_ref, lse_ref, gin_ref, g3_ref, out_ref, rin_ref, r3_ref,
             dq_s, dk_s, dv_s, do_s, dd_s, *sems):
        hp = pl.program_id(0)
        exchanges = (_GradExchange(gin_ref, rin_ref, *sems[:3], _shard_cols((0, SEG0_ATTN * D), (SEG0_MID * D, IN_COLS))),
                     _GradExchange(g3_ref, r3_ref, *sems[3:], _whole))

        @pl.when(hp == 0)
        def _():
            for ex in exchanges:
                ex.start()

        head_sum = _head_sum_matrix()
        dq_s[...] = jnp.zeros(dq_s.shape, F32)
        dk_s[...] = jnp.zeros(dk_s.shape, F32)
        dv_s[...] = jnp.zeros(dv_s.shape, F32)

        def row_dots(i, carry):
            rows = pl.ds(pl.multiple_of(i * QB, QB), QB)
            natural = _natural_rows(i, S)
            do_c = do_ref[natural, :]
            do_s[rows, :] = do_c
            dd = _dot(_hi_lo(do_c * o_ref[natural, :]), head_sum)
            dd_s[0, rows, :] = dd[:, :QB]
            dd_s[1, rows, :] = dd[:, QB:]
            return carry

        lax.fori_loop(0, n_blocks, row_dots, 0)

        for d in DILATIONS:
            tri_le, dist, low = _fold_masks(d)
            low_b = low.astype(F32).astype(BF)
            high_b = 1.0 - low_b
            slope = [sl_ref[2 * hp + a] * float(d) for a in range(2)]
            bias = [slope[a] * dist for a in range(2)]

            def block(b, d=d, slope=slope, bias=bias, tri_le=tri_le, low=low, low_b=low_b, high_b=high_b):
                n, cur, prev = _block_rows(b, d, S)
                has_prev = n > 0
                valid = jnp.logical_or(tri_le, has_prev)
                q2f = cur.get(q_ref) * 0.125
                q2 = q2f.astype(BF)
                qs = jnp.concatenate([q2 * low_b, q2 * high_b], axis=0)
                kp, vp = prev.get(k_ref), prev.get(v_ref)
                kp_b, vp_b = kp.astype(BF), vp.astype(BF)
                kcat = jnp.concatenate([kp_b, cur.get(k_ref).astype(BF)], axis=0)
                vcat = jnp.concatenate([vp_b, cur.get(v_ref).astype(BF)], axis=0)
                do2f = cur.get(do_s)
                do2 = do2f.astype(BF)
                dos = jnp.concatenate([do2 * low_b, do2 * high_b], axis=0)
                s2 = _dot_nt(qs, kcat)
                dp2 = _dot_nt(dos, vcat)
                diag2 = _dot(jnp.concatenate([_hi_lo(q2.astype(F32) * kp_b.astype(F32)),
                                              _hi_lo(do2.astype(F32) * vp_b.astype(F32))], axis=0), head_sum)
                p_rows, ds_rows, pe_h, dse_h = [], [], [], []
                for a in range(2):
                    hs = slice(a * QB, (a + 1) * QB)
                    sp, sc = s2[hs, :QB], s2[hs, QB:]
                    dpp, dpc = dp2[hs, :QB], dp2[hs, QB:]
                    lse_a, dd_a = cur.get(lse_ref.at[a]), cur.get(dd_s.at[a])
                    comb = jnp.where(tri_le, sc, sp) - bias[a]
                    e = diag2[:QB, hs] - slope[a] * float(QB)
                    p = jnp.where(valid, jnp.exp(comb - lse_a), 0.0)
                    pe = jnp.where(has_prev, jnp.exp(e - lse_a), 0.0)
                    ds = p * (jnp.where(tri_le, dpc, dpp) - dd_a)
                    dse_h.append(pe * (diag2[QB:, hs] - dd_a))
                    pe_h.append(pe)
                    p_rows.append(jnp.concatenate([jnp.where(tri_le, 0.0, p).astype(BF),
                                                   jnp.where(tri_le, p, 0.0).astype(BF)], axis=1))
                    ds_rows.append(jnp.concatenate([jnp.where(tri_le, 0.0, ds).astype(BF),
                                                    jnp.where(tri_le, ds, 0.0).astype(BF)], axis=1))
                pst = jnp.concatenate(p_rows, axis=0)
                dst = jnp.concatenate(ds_rows, axis=0)
                pe2 = jnp.where(low, pe_h[0], pe_h[1])
                dse2 = jnp.where(low, dse_h[0], dse_h[1])
                dq = _dot(dst, kcat)
                cur.add(dq_s, (jnp.where(low, dq[:QB], dq[QB:]) + dse2 * kp) * 0.125)
                dk = _dot_tn(dst, qs)
                dv = _dot_tn(pst, dos)
                prev.add(dk_s, dk[:QB] + dse2 * q2f)
                cur.add(dk_s, dk[QB:])
                prev.add(dv_s, dv[:QB] + pe2 * do2f)
                cur.add(dv_s, dv[QB:])

            def several(it, carry, block=block):
                for u in range(ATT_UNROLL):
                    block(it * ATT_UNROLL + u)
                return carry

            lax.fori_loop(0, n_blocks // ATT_UNROLL, several, 0)

        def finish(i, carry):
            rows = pl.ds(pl.multiple_of(i * QB, QB), QB)
            natural = _natural_rows(i, S)
            for t, ref in enumerate((dq_s, dk_s, dv_s)):
                out_ref.at[t][natural, :] = ref[rows, :]
            return carry

        lax.fori_loop(0, n_blocks, finish, 0)

        @pl.when(hp == hpr - 1)
        def _():
            for ex in exchanges:
                ex.finish()

    col = pl.BlockSpec((S, 128), lambda h: (0, h))
    return pl.pallas_call(
        body, name="attn_bwd", grid=(hpr,),
        in_specs=[SMEM_SPEC, col, col, col, col, col, pl.BlockSpec((2, S, 128), lambda h: (0, 0, h)),
                  ANY_SPEC, ANY_SPEC],
        out_specs=(pl.BlockSpec((3, S, 128), lambda h: (0, 0, h)), ANY_SPEC, ANY_SPEC),
        out_shape=(jax.ShapeDtypeStruct((3, S, D), F32), jax.ShapeDtypeStruct(g_in.shape, BF),
                   jax.ShapeDtypeStruct(g_3.shape, BF)),
        scratch_shapes=([pltpu.VMEM((S, 128), F32)] * 4 + [pltpu.VMEM((2, S, 128), F32)]
                        + GRAD_EXCHANGE_SEMS + GRAD_EXCHANGE_SEMS),
        compiler_params=_params(1),
    )(slopes, q, k, v, do, o, lse, g_in, g_3)


WG_TN = 256
SEG0_CONV, SEG0_ATTN, SEG0_MID = 0, 4, 7


def _wgrad_in(ut, d_group, seg0, g_in, name):
    S = ut.shape[1]
    tn = WG_TN
    per_seg = D // tn
    per_shard = W_IN_SHARD // tn
    n_tiles = d_group.shape[0] * per_seg
    tile0 = seg0 * per_seg

    def body(ut_ref, d_ref, *rest):
        rest[-1][0] = _dot(ut_ref[...], d_ref[0].astype(BF)).astype(BF)

    operands, in_specs, aliases = [ut, d_group], [VMEM_SPEC, pl.BlockSpec((1, S, tn), lambda t: (t // per_seg, 0, t % per_seg))], {}
    if g_in is not None:
        operands.append(g_in)
        in_specs.append(ANY_SPEC)
        aliases = {2: 0}
    return pl.pallas_call(
        body, name=name, grid=(n_tiles,), in_specs=in_specs,
        out_specs=pl.BlockSpec((1, D, tn), lambda t: ((tile0 + t) // per_shard, 0, (tile0 + t) % per_shard)),
        out_shape=jax.ShapeDtypeStruct((N_DEV, D, W_IN_SHARD), BF),
        input_output_aliases=aliases,
        compiler_params=_params(1),
    )(*operands)


def _dgrad_norm_bwd(d_conv, d_attn, d_mid, w_all, x2, dh, norm_g):
    S = x2.shape[0]
    tm = ROW_TILE
    nsteps = S // tm
    tile = pl.BlockSpec((tm, D), lambda i: (i, 0))
    pieces = _proj_pieces()

    def body(a_ref, b_ref, c_ref, w_ref, x_ref, dh_ref, g_ref, gx_ref, small_ref):
        i = pl.program_id(0)

        @pl.when(i == 0)
        def _():
            small_ref[...] = jnp.zeros_like(small_ref)

        groups = (a_ref, b_ref, c_ref)
        du = jnp.zeros((tm, D), F32)
        for s, sc, p, pc, width in pieces:
            g = 0 if s < 4 else (1 if s < 7 else 2)
            local = s - (0, 4, 7)[g]
            du = du + _dot_nt(groups[g][local, :, sc:sc + width].astype(BF), w_ref[p, :, pc:pc + width])
        xv = x_ref[...]
        r = lax.rsqrt(jnp.mean(xv * xv, axis=-1, keepdims=True) + EPS)
        n = xv * r
        dn = du * g_ref[...]
        gx_ref[...] = dh_ref[...] + r * (dn - n * jnp.mean(dn * n, axis=-1, keepdims=True))
        small_ref[...] += _set_rows((8, D), {0: jnp.sum(du * n, axis=0, keepdims=True)})

    return pl.pallas_call(
        body, name="dgrad_norm_bwd", grid=(nsteps,),
        in_specs=[pl.BlockSpec((4, tm, D), lambda i: (0, i, 0)), pl.BlockSpec((3, tm, D), lambda i: (0, i, 0)),
                  pl.BlockSpec((3, tm, D), lambda i: (0, i, 0)), VMEM_SPEC, tile, tile,
                  pl.BlockSpec((1, D), lambda i: (0, 0))],
        out_specs=(tile, pl.BlockSpec((8, D), lambda i: (0, 0))),
        out_shape=(jax.ShapeDtypeStruct((S, D), F32), jax.ShapeDtypeStruct((8, D), F32)),
        compiler_params=_params(1),
    )(d_conv, d_attn, d_mid, w_all, x2, dh, norm_g)


HBM_SPEC = pl.BlockSpec(memory_space=pltpu.HBM)
SEM_SPEC = pl.BlockSpec(memory_space=pltpu.SEMAPHORE)
ATTN_COLS = _shard_cols((SEG0_ATTN * D, SEG0_MID * D))


def _attn_cols_exchange_start(g_in, r_in):
    def body(g_ref, r_ref, send_sems, recv_sems, g_thru, r_thru, token):
        _GradExchange(g_ref, r_ref, send_sems, recv_sems, None, ATTN_COLS).start()
        token[...] = jnp.zeros_like(token)

    hbm = pltpu.with_memory_space_constraint
    return pl.pallas_call(
        body, name="attn_cols_exchange_start",
        out_shape=(pltpu.SemaphoreType.DMA((N_DEV,)), pltpu.SemaphoreType.DMA((N_DEV,)),
                   pltpu.HBM(g_in.shape, g_in.dtype), pltpu.HBM(r_in.shape, r_in.dtype),
                   jax.ShapeDtypeStruct((8, 128), F32)),
        in_specs=(HBM_SPEC, HBM_SPEC), out_specs=(SEM_SPEC, SEM_SPEC, HBM_SPEC, HBM_SPEC, VMEM_SPEC),
        input_output_aliases={0: 2, 1: 3},
        compiler_params=pltpu.CompilerParams(has_side_effects=pltpu.SideEffectType.DATAFLOW_SIDE_EFFECTING),
    )(hbm(g_in, pltpu.HBM), hbm(r_in, pltpu.HBM))


def _attn_cols_exchange_wait(send_sems, recv_sems, g_thru, r_thru, after):
    def body(g_ref, r_ref, send_sems, recv_sems, after_ref, g_dead, r_out):
        _GradExchange(g_ref, r_ref, send_sems, recv_sems, None, ATTN_COLS).finish()

    return pl.pallas_call(
        body, name="attn_cols_exchange_wait",
        out_shape=(pltpu.HBM(g_thru.shape, g_thru.dtype), pltpu.HBM(r_thru.shape, r_thru.dtype)),
        in_specs=(HBM_SPEC, HBM_SPEC, SEM_SPEC, SEM_SPEC, ANY_SPEC), out_specs=(HBM_SPEC, HBM_SPEC),
        input_output_aliases={0: 0, 1: 1},
        compiler_params=pltpu.CompilerParams(has_side_effects=pltpu.SideEffectType.DATAFLOW_SIDE_EFFECTING),
    )(g_thru, r_thru, send_sems, recv_sems, after)


class _DirectGather:
    def __init__(self, ref, send_sems, recv_sems):
        self.ref, self.send_sems, self.recv_sems = ref, send_sems, recv_sems
        self.me = _my_place()[3]

    def _copy(self, p, target):
        return pltpu.make_async_remote_copy(
            src_ref=self.ref.at[p], dst_ref=self.ref.at[p], send_sem=self.send_sems.at[target],
            recv_sem=self.recv_sems.at[p], device_id=_device(target), device_id_type=MESH)

    def _as_each_device(self, mine, theirs):
        for m in range(N_DEV):
            def branch(m=m):
                for k in range(1, N_DEV):
                    mine(self._copy(m, (m + k) % N_DEV))
                    theirs(self._copy((m + k) % N_DEV, m))

            pl.when(self.me == m)(branch)

    def start(self):
        self._as_each_device(lambda cp: cp.start(), lambda cp: None)

    def finish(self):
        self._as_each_device(lambda cp: cp.wait_send(), lambda cp: cp.wait_recv())


def _square_weights_gather_start(w3_all):
    def body(w_ref, send_sems, recv_sems, w_thru, token):
        _DirectGather(w_ref, send_sems, recv_sems).start()
        token[...] = jnp.zeros_like(token)

    return pl.pallas_call(
        body, name="square_weights_gather_start",
        out_shape=(pltpu.SemaphoreType.DMA((N_DEV,)), pltpu.SemaphoreType.DMA((N_DEV,)),
                   pltpu.HBM(w3_all.shape, w3_all.dtype), jax.ShapeDtypeStruct((8, 128), F32)),
        in_specs=(HBM_SPEC,), out_specs=(SEM_SPEC, SEM_SPEC, HBM_SPEC, VMEM_SPEC),
        input_output_aliases={0: 2},
        compiler_params=pltpu.CompilerParams(has_side_effects=pltpu.SideEffectType.DATAFLOW_SIDE_EFFECTING),
    )(pltpu.with_memory_space_constraint(w3_all, pltpu.HBM))


def _square_weights_gather_wait(send_sems, recv_sems, w_thru, after):
    def body(w_ref, send_sems, recv_sems, after_ref, w_out):
        _DirectGather(w_ref, send_sems, recv_sems).finish()

    return pl.pallas_call(
        body, name="square_weights_gather_wait",
        out_shape=pltpu.HBM(w_thru.shape, w_thru.dtype),
        in_specs=(HBM_SPEC, SEM_SPEC, SEM_SPEC, ANY_SPEC), out_specs=HBM_SPEC,
        input_output_aliases={0: 0},
        compiler_params=pltpu.CompilerParams(has_side_effects=pltpu.SideEffectType.DATAFLOW_SIDE_EFFECTING),
    )(w_thru, send_sems, recv_sems, after)


def _adamw_math(w, g, m, v):
    m = ADAM_B1 * m + (1.0 - ADAM_B1) * g
    v = ADAM_B2 * v + (1.0 - ADAM_B2) * (g * g)
    m_hat = m / (1.0 - ADAM_B1 ** ADAM_STEP)
    v_hat = v / (1.0 - ADAM_B2 ** ADAM_STEP)
    delta = -ADAM_LR * (m_hat / (jnp.sqrt(v_hat) + ADAM_EPS) + ADAM_WD * w)
    return delta, m, v


def _sum_adamw(parts, w, m, v, tm, name):
    R, C = w.shape
    tile = pl.BlockSpec((tm, C), lambda i: (i, 0))

    def body(p_ref, w_ref, m_ref, v_ref, g_out, d_out, m_out, v_out):
        g = p_ref[0].astype(F32)
        for s in range(1, N_DEV):
            g = g + p_ref[s].astype(F32)
        g_out[...] = g
        d_out[...], m_out[...], v_out[...] = _adamw_math(w_ref[...], g, m_ref[...], v_ref[...])

    shape = jax.ShapeDtypeStruct((R, C), F32)
    return pl.pallas_call(
        body, name=name, grid=(R // tm,),
        in_specs=[pl.BlockSpec((N_DEV, tm, C), lambda i: (0, i, 0)), tile, tile, tile],
        out_specs=(tile, tile, tile, tile), out_shape=(shape, shape, shape, shape),
        compiler_params=_params(1),
    )(parts, w, m, v)


def _adamw(g, w, m, v, name):
    def body(g_ref, w_ref, m_ref, v_ref, d_out, m_out, v_out):
        d_out[...], m_out[...], v_out[...] = _adamw_math(w_ref[...], g_ref[...], m_ref[...], v_ref[...])

    shape = jax.ShapeDtypeStruct(w.shape, F32)
    return pl.pallas_call(
        body, name=name, in_specs=[VMEM_SPEC] * 4, out_specs=(VMEM_SPEC,) * 3, out_shape=(shape, shape, shape),
    )(g, w, m, v)


def _alibi_slopes():
    return jnp.exp2(-8.0 * jnp.arange(1, N_HEADS + 1, dtype=F32) / N_HEADS)


def _local_step(x2, target, norm_g, b_merge, final_g, w_in, w3_shard, cw_shard):
    slopes = _alibi_slopes()
    u, ut, w_all, w3_all, cw_all = _norm_gather_first_weights(x2, norm_g, w_in, w3_shard, cw_shard)
    o, lse, q, k, v, w_all, cw_all = _attn_fwd(u, slopes, w_all, cw_all)
    *w3_in_flight, token = _square_weights_gather_start(w3_all)
    cw8 = jnp.transpose(cw_all, (1, 0, 2)).reshape(8, D) + token[:, 0:1]
    pa = _proj_cols(u, w_all, SEG0_CONV, 4, BF, "proj_conv")
    yc_in = _conv_fwd(pa, cw8)
    pa_mid = _proj_cols(u, w_all, SEG0_MID, 3, BF, "proj_mid")
    w3_all = _square_weights_gather_wait(*w3_in_flight, pa_mid)
    w3 = jnp.transpose(w3_all, (1, 0, 2, 3)).reshape(3, D, D)
    dh, d_mid, do, dyc_in, g_3, small_mid = _mid(yc_in, pa_mid, o, x2, target, b_merge, final_g, w3)
    g_in = _wgrad_in(ut, d_mid, SEG0_MID, None, "wgrad_in_mid")
    d_conv, small_conv = _conv_bwd(dyc_in, pa, cw8)
    g_in = _wgrad_in(ut, d_conv, SEG0_CONV, g_in, "wgrad_in_conv")
    d_attn, r_in, r_3 = _attn_bwd(q, k, v, slopes, do, o, lse, g_in, g_3)
    g_in = _wgrad_in(ut, d_attn, SEG0_ATTN, g_in, "wgrad_in_attn")
    *in_flight, token = _attn_cols_exchange_start(g_in, r_in)
    grad_x, small_norm = _dgrad_norm_bwd(d_conv, d_attn, d_mid, w_all, x2, dh, norm_g + token[0:1, 0:1])
    return grad_x, in_flight, r_3, small_mid, small_conv, small_norm


def kernel(x, norm_g, w_in, b_merge, conv_w, w_out_conv, w_out_attn, w_o, final_g, loss_target, m_norm_g, m_w_in, m_b_merge, m_conv_w, m_w_out_conv, m_w_out_attn, m_w_o, m_final_g, v_norm_g, v_w_in, v_b_merge, v_conv_w, v_w_out_conv, v_w_out_attn, v_w_o, v_final_g):
    me = 4 * lax.axis_index("x") + 2 * lax.axis_index("y") + lax.axis_index("c")
    stack3 = lambda a, b, c: jnp.concatenate([a, b, c], axis=0)
    pad8 = lambda a: jnp.pad(a, ((0, 8 - a.shape[0]), (0, 0)))

    w3_shard = stack3(w_out_conv, w_out_attn, w_o)
    final_g2 = final_g.reshape(1, D)
    grad_x, in_flight, r_3, small_mid, small_conv, small_norm = _local_step(
        x[0], loss_target[0], norm_g, b_merge, final_g2, w_in[0], w3_shard, pad8(conv_w[0]))

    small = _allreduce_small(small_mid, small_conv, small_norm)
    g_in, r_in = _attn_cols_exchange_wait(*in_flight, small)
    own = lax.dynamic_index_in_dim(g_in, me, 0, keepdims=True)
    r_in = lax.dynamic_update_slice(r_in, own, (me, 0, 0))

    g_w_in, d_w_in, nm_w_in, nv_w_in = _sum_adamw(r_in, w_in[0], m_w_in[0], v_w_in[0], 128, "adamw_w_in")
    g_w3, d_w3, nm_w3, nv_w3 = _sum_adamw(
        r_3.reshape(N_DEV, 3 * ROW_SHARD, D), w3_shard.reshape(3 * ROW_SHARD, D),
        stack3(m_w_out_conv, m_w_out_attn, m_w_o).reshape(3 * ROW_SHARD, D),
        stack3(v_w_out_conv, v_w_out_attn, v_w_o).reshape(3 * ROW_SHARD, D), ROW_SHARD, "adamw_w3")

    def pack(ng, bm, fg):
        return pad8(jnp.concatenate([ng, bm.reshape(2, D), fg.reshape(1, D)], axis=0))

    d_s, nm_s, nv_s = _adamw(small, pack(norm_g, b_merge, final_g), pack(m_norm_g, m_b_merge, m_final_g),
                             pack(v_norm_g, v_b_merge, v_final_g), "adamw_small")
    g_cw = lax.dynamic_slice(small, (4, me * ROW_SHARD), (3, ROW_SHARD))
    d_cw, nm_cw, nv_cw = _adamw(g_cw, conv_w[0], m_conv_w[0], v_conv_w[0], "adamw_conv_w")

    loss = small[7, 0]
    split3 = lambda t: tuple(t[a * ROW_SHARD:(a + 1) * ROW_SHARD][None] for a in range(3))
    unpack = lambda t: (t[0:1], t[1:3].reshape(1, 2 * D), t[3])

    def leaves(in_, small_, cw_, w3_):
        ng, bm, fg = unpack(small_)
        wc, wa, wo = split3(w3_)
        return (ng, in_[None], bm, cw_[None], wc, wa, wo, fg)

    return (loss, grad_x[None],
            *leaves(g_w_in, small, g_cw, g_w3),
            *leaves(d_w_in, d_s, d_cw, d_w3),
            *leaves(nm_w_in, nm_s, nm_cw, nm_w3),
            *leaves(nv_w_in, nv_s, nv_cw, nv_w3))
```

```python
import functools

import jax
import jax.numpy as jnp
from jax import lax
from jax.experimental import pallas as pl
from jax.experimental.pallas import tpu as pltpu

D = 1024
N_HEADS = 16
HEAD_DIM = 64
N_SEG = 10
IN_COLS = N_SEG * D
N_DEV = 8
W_IN_SHARD = IN_COLS // N_DEV
ROW_SHARD = D // N_DEV
QB = 128
DILATIONS = (1, 4, 16)
EPS = 1e-6
NEG = -1e30
BF = jnp.bfloat16
F32 = jnp.float32
MESH = pl.DeviceIdType.MESH

ADAM_LR = 0.001
ADAM_B1 = 0.9
ADAM_B2 = 0.999
ADAM_EPS = 1e-08
ADAM_WD = 0.01
ADAM_STEP = 10

V7X_VMEM_BYTES = 64 * 1024 * 1024
VMEM_LIMIT = V7X_VMEM_BYTES - 8 * 1024 * 1024
ROW_TILE = 256

VMEM_SPEC = pl.BlockSpec(memory_space=pltpu.VMEM)
ANY_SPEC = pl.BlockSpec(memory_space=pl.ANY)
SMEM_SPEC = pl.BlockSpec(memory_space=pltpu.SMEM)


def _params(n_grid_axes, vmem=VMEM_LIMIT):
    return pltpu.CompilerParams(dimension_semantics=("arbitrary",) * n_grid_axes, vmem_limit_bytes=vmem)


def _dot(a, b):
    return jnp.dot(a, b, preferred_element_type=F32)


def _dot_nt(a, b):
    return lax.dot_general(a, b, (((1,), (1,)), ((), ())), preferred_element_type=F32)


def _dot_tn(a, b):
    return lax.dot_general(a, b, (((0,), (0,)), ((), ())), preferred_element_type=F32)


def _sigmoid(z):
    return 1.0 / (1.0 + jnp.exp(-z))


def _my_place():
    x, y, c = lax.axis_index("x"), lax.axis_index("y"), lax.axis_index("c")
    return x, y, c, 4 * x + 2 * y + c


def _peers(x, y, c):
    out = []
    for k in range(1, N_DEV):
        px = 1 - x if k & 4 else x
        py = 1 - y if k & 2 else y
        pc = 1 - c if k & 1 else c
        out.append(((px, py, pc), 4 * px + 2 * py + pc))
    return out


def _device(p):
    return (p >> 2, (p >> 1) & 1, p & 1)


def _shard_cols(*ranges):
    def cols(p):
        found = None
        for lo, hi in ranges:
            a, b = max(lo, p * W_IN_SHARD), min(hi, (p + 1) * W_IN_SHARD)
            if a < b:
                assert found is None
                found = (a - p * W_IN_SHARD, b - p * W_IN_SHARD)
        return found

    return cols


def _whole(p):
    return ()


def _block(ref, idx, cols):
    return ref.at[idx] if cols == () else ref.at[idx, :, cols[0]:cols[1]]


class _WeightGather:
    def __init__(self, src, dst, send_sems, forward_sems, recv_sems, cols):
        self.src, self.dst, self.cols = src, dst, cols
        self.send_sems, self.forward_sems, self.recv_sems = send_sems, forward_sems, recv_sems
        self.me = _my_place()[3]

    def _copy(self, p, target, passing_on=False):
        cols = self.cols(p)
        return pltpu.make_async_remote_copy(
            src_ref=_block(self.dst, p, cols) if passing_on else self.src(p, cols), dst_ref=_block(self.dst, p, cols),
            send_sem=self.forward_sems.at[p] if passing_on else self.send_sems.at[target],
            recv_sem=self.recv_sems.at[p], device_id=_device(target), device_id_type=MESH)

    def _as_each_device(self, own, relayed, other):
        for m in range(N_DEV):
            def branch(m=m):
                for p in range(N_DEV):
                    if self.cols(p) is None:
                        continue
                    if p == m:
                        for t in [m ^ 1] + [q for q in range(N_DEV) if q >> 1 != m >> 1 and q & 1 == m & 1]:
                            own(self._copy(m, t))
                    elif p >> 1 != m >> 1 and p & 1 == m & 1:
                        relayed(p, m ^ 1)
                    else:
                        other(p)

            pl.when(self.me == m)(branch)

    def start(self):
        self._as_each_device(lambda cp: cp.start(), lambda p, t: None, lambda p: None)

    def forward(self):
        def pass_on(p, t):
            self._copy(p, p).wait_recv()
            self._copy(p, t, passing_on=True).start()

        self._as_each_device(lambda cp: None, pass_on, lambda p: None)

    def finish(self):
        self._as_each_device(lambda cp: cp.wait_send(), lambda p, t: self._copy(p, t, passing_on=True).wait_send(),
                             lambda p: self._copy(p, p).wait_recv())


WEIGHT_GATHER_SEMS = [pltpu.SemaphoreType.DMA((N_DEV,))] * 3
FORWARD_STEP = 6
REST_COLS = _shard_cols((0, 4 * D), (7 * D, IN_COLS))
HEAD_PAIRS = D // 128


def _qkv_piece(h, seg):
    col = (4 + seg) * D + 128 * h
    return col // W_IN_SHARD, col % W_IN_SHARD


class _PieceGather:
    def __init__(self, src, dst, send_sems, recv_sems):
        self.src, self.dst, self.send_sems, self.recv_sems = src, dst, send_sems, recv_sems
        self.me = _my_place()[3]

    def _copy(self, i, target):
        p, lo = _qkv_piece(i // 3, i % 3)
        return pltpu.make_async_remote_copy(
            src_ref=self.src(p, lo, lo + 128), dst_ref=self.dst.at[p, :, lo:lo + 128], send_sem=self.send_sems.at[i, target],
            recv_sem=self.recv_sems.at[i], device_id=_device(target), device_id_type=MESH)

    def _owner(self, i, act):
        p = _qkv_piece(i // 3, i % 3)[0]

        def sender():
            for k in range(N_DEV - 1):
                act(self._copy(i, (p + 1 + (k + i) % (N_DEV - 1)) % N_DEV))

        pl.when(self.me == p)(sender)

    def start(self, pieces):
        for i in pieces:
            self._owner(i, lambda cp: cp.start())

    def wait_send(self, pieces):
        for i in pieces:
            self._owner(i, lambda cp: cp.wait_send())

    def wait_recv(self, pieces):
        for i in pieces:
            p = _qkv_piece(i // 3, i % 3)[0]
            pl.when(self.me != p)(lambda i=i, p=p: self._copy(i, p).wait_recv())


def _piece_sems(n):
    return [pltpu.SemaphoreType.DMA((n, N_DEV)), pltpu.SemaphoreType.DMA((n,))]


def _norm_gather_first_weights(x2, norm_g, w_in, w3, cw):
    S = x2.shape[0]
    tm = 2 * ROW_TILE
    nsteps = S // tm

    def body(x_ref, g_ref, w_in_ref, w3_ref, cw_ref, u_ref, ut_ref, o_in, o_3, o_cw, in_bf, w3_bf, local_sems, *sems):
        i = pl.program_id(0)
        me = _my_place()[3]
        gather = _PieceGather(lambda p, lo, hi: in_bf.at[:, lo:hi], o_in, *sems)
        local = [pltpu.make_async_copy(src, dst.at[me], local_sems.at[a])
                 for a, (src, dst) in enumerate(((in_bf, o_in), (w3_bf, o_3), (cw_ref, o_cw)))]

        @pl.when(i == 0)
        def _():
            def cast_rows(r, carry):
                rows = pl.ds(pl.multiple_of(r * 128, 128), 128)
                in_bf[rows, :] = w_in_ref[rows, :].astype(BF)
                return carry

            lax.fori_loop(0, D // 128, cast_rows, 0)
            for a in range(3):
                w3_bf[a] = w3_ref[a].astype(BF)
            gather.start(range(3))
            for cp in local:
                cp.start()

        xv = x_ref[...]
        r = lax.rsqrt(jnp.mean(xv * xv, axis=-1, keepdims=True) + EPS)
        u = xv * r * g_ref[...]
        u_ref[...] = u.astype(BF)
        ut_ref[...] = u.T.astype(BF)

        @pl.when(i == nsteps - 1)
        def _():
            gather.wait_recv(range(3))
            gather.wait_send(range(3))
            for cp in local:
                cp.wait()

    return pl.pallas_call(
        body, name="norm_gather_first_weights", grid=(nsteps,),
        out_shape=(jax.ShapeDtypeStruct((S, D), BF), jax.ShapeDtypeStruct((D, S), BF),
                   jax.ShapeDtypeStruct((N_DEV, D, W_IN_SHARD), BF),
                   jax.ShapeDtypeStruct((N_DEV, 3, ROW_SHARD, D), BF),
                   jax.ShapeDtypeStruct((N_DEV, 8, 128), F32)),
        in_specs=[pl.BlockSpec((tm, D), lambda i: (i, 0)), pl.BlockSpec((1, D), lambda i: (0, 0)),
                  VMEM_SPEC, VMEM_SPEC, VMEM_SPEC],
        out_specs=(pl.BlockSpec((tm, D), lambda i: (i, 0)), pl.BlockSpec((D, tm), lambda i: (0, i)),
                   ANY_SPEC, ANY_SPEC, ANY_SPEC),
        scratch_shapes=[pltpu.VMEM((D, W_IN_SHARD), BF), pltpu.VMEM((3, ROW_SHARD, D), BF),
                        pltpu.SemaphoreType.DMA((3,))] + _piece_sems(3),
        compiler_params=_params(1),
    )(x2, norm_g, w_in, w3, cw)


class _GradExchange:
    def __init__(self, src, dst, send_sems, recv_sems, local_sem, cols):
        self.src, self.dst, self.cols = src, dst, cols
        self.send_sems, self.recv_sems, self.local_sem = send_sems, recv_sems, local_sem
        self.me = _my_place()[3]

    def _remote(self, p, source):
        return pltpu.make_async_remote_copy(
            src_ref=_block(self.src, p, self.cols(p)), dst_ref=_block(self.dst, source, self.cols(p)),
            send_sem=self.send_sems.at[p], recv_sem=self.recv_sems.at[source],
            device_id=_device(p), device_id_type=MESH)

    def _local(self, p):
        return pltpu.make_async_copy(_block(self.src, p, self.cols(p)), _block(self.dst, p, self.cols(p)),
                                     self.local_sem)

    def _as_each_device(self, send, local, receive):
        for m in range(N_DEV):
            def branch(m=m):
                for k in range(1, N_DEV):
                    p = (m + k) % N_DEV
                    if self.cols(p) is not None:
                        send(self._remote(p, m))
                if self.cols(m) is not None:
                    if self.local_sem is not None:
                        local(self._local(m))
                    for k in range(1, N_DEV):
                        receive(self._remote(m, (m + k) % N_DEV))

            pl.when(self.me == m)(branch)

    def start(self):
        self._as_each_device(lambda cp: cp.start(), lambda cp: cp.start(), lambda cp: None)

    def finish(self):
        self._as_each_device(lambda cp: cp.wait_send(), lambda cp: cp.wait(), lambda cp: cp.wait_recv())


GRAD_EXCHANGE_SEMS = [pltpu.SemaphoreType.DMA((N_DEV,)), pltpu.SemaphoreType.DMA((N_DEV,)), pltpu.SemaphoreType.DMA]


def _allreduce_small(p_mid, p_conv, p_norm):
    def body(a_ref, b_ref, c_ref, out_ref, mine, gathered, send_sems, recv_sems):
        x, y, c, me = _my_place()
        mine[...] = a_ref[...] + b_ref[...] + c_ref[...]
        gathered[me] = mine[...]
        remote = []
        for k, (peer, _) in enumerate(_peers(x, y, c)):
            cp = pltpu.make_async_remote_copy(
                src_ref=mine, dst_ref=gathered.at[me], send_sem=send_sems.at[k], recv_sem=recv_sems.at[k],
                device_id=peer, device_id_type=MESH)
            cp.start()
            remote.append(cp)
        for cp in remote:
            cp.wait()
        total = gathered[0]
        for s in range(1, N_DEV):
            total = total + gathered[s]
        out_ref[...] = total

    return pl.pallas_call(
        body, name="allreduce_small",
        out_shape=jax.ShapeDtypeStruct((8, D), F32),
        in_specs=[VMEM_SPEC, VMEM_SPEC, VMEM_SPEC], out_specs=VMEM_SPEC,
        scratch_shapes=[pltpu.VMEM((8, D), F32), pltpu.VMEM((N_DEV, 8, D), F32),
                        pltpu.SemaphoreType.DMA((N_DEV - 1,)), pltpu.SemaphoreType.DMA((N_DEV - 1,))],
    )(p_mid, p_conv, p_norm)


def _proj_pieces():
    cuts = sorted(set(range(0, IN_COLS + 1, D)) | set(range(0, IN_COLS + 1, W_IN_SHARD)))
    return [(lo // D, lo % D, lo // W_IN_SHARD, lo % W_IN_SHARD, hi - lo) for lo, hi in zip(cuts[:-1], cuts[1:])]


PROJ_TN = 256


def _proj_cols(u, w_all, seg0, n_seg, dtype, name):
    S = u.shape[0]
    tn = PROJ_TN
    per_shard = W_IN_SHARD // tn
    tile0 = seg0 * D // tn

    def body(u_ref, w_ref, out_ref):
        out_ref[...] = _dot(u_ref[...], w_ref[0]).astype(dtype)

    return pl.pallas_call(
        body, name=name, grid=(n_seg * D // tn,),
        in_specs=[VMEM_SPEC, pl.BlockSpec((1, D, tn), lambda t: ((tile0 + t) // per_shard, 0, (tile0 + t) % per_shard))],
        out_specs=pl.BlockSpec((S, tn), lambda t: (0, t)),
        out_shape=jax.ShapeDtypeStruct((S, n_seg * D), dtype),
        compiler_params=_params(1),
    )(u, w_all)


CONV_TM, CONV_TC = 256, 512
HALO = 16


def _conv_fwd(pa, cw8):
    S = pa.shape[0]
    tm, tc = CONV_TM, CONV_TC
    nct = D // tc

    def seg(s):
        return pl.BlockSpec((tm, tc), lambda i, j, s=s: (i, s * nct + j))

    def halo_before(s):
        return pl.BlockSpec((HALO, tc), lambda i, j, s=s: (jnp.maximum(i * (tm // HALO) - 1, 0), s * nct + j))

    def body(xc, bg, cg, zc, xch, cgh, cw, out):
        i = pl.program_id(0)
        a = cg[...].astype(F32) * xc[...].astype(F32)
        ah = cgh[...].astype(F32) * xch[...].astype(F32)
        ah = jnp.where(i > 0, ah, 0.0)
        row = lax.broadcasted_iota(jnp.int32, (tm, tc), 0)
        a1 = jnp.where(row == 0, ah[HALO - 1:HALO, :], pltpu.roll(a, 1, 0))
        a2 = jnp.where(row == 0, ah[HALO - 2:HALO - 1, :],
                       jnp.where(row == 1, ah[HALO - 1:HALO, :], pltpu.roll(a, 2, 0)))
        w = cw[...]
        conv = w[0:1, :] * a2 + w[1:2, :] * a1 + w[2:3, :] * a
        z = zc[...].astype(F32)
        out[...] = (z * _sigmoid(z) * bg[...].astype(F32) * conv).astype(BF)

    return pl.pallas_call(
        body, name="conv_fwd", grid=(S // tm, nct),
        in_specs=[seg(0), seg(1), seg(2), seg(3), halo_before(0), halo_before(2),
                  pl.BlockSpec((8, tc), lambda i, j: (0, j))],
        out_specs=pl.BlockSpec((tm, tc), lambda i, j: (i, j)),
        out_shape=jax.ShapeDtypeStruct((S, D), BF),
        compiler_params=_params(2),
    )(pa, pa, pa, pa, pa, pa, cw8)


ATT_UNROLL = 32


LAYOUT_MOD = 4
RUN = QB // LAYOUT_MOD


def _fold_masks(d):
    row = lax.broadcasted_iota(jnp.int32, (QB, QB), 0)
    lane = lax.broadcasted_iota(jnp.int32, (QB, QB), 1)
    if d == 1:
        qpos, kpos = LAYOUT_MOD * (row % RUN) + row // RUN, LAYOUT_MOD * (lane % RUN) + lane // RUN
    else:
        qpos, kpos = row, lane
    tri_le = kpos <= qpos
    dist = jnp.where(tri_le, qpos - kpos, qpos - kpos + QB).astype(F32)
    return tri_le, dist, lane < HEAD_DIM


class _Rows:
    def __init__(self, slices):
        self.slices = slices

    def get(self, ref):
        parts = [ref[sl, :] for sl in self.slices]
        return parts[0] if len(parts) == 1 else jnp.concatenate(parts, axis=0)

    def put(self, ref, val):
        size = QB // len(self.slices)
        for g, sl in enumerate(self.slices):
            ref[sl, :] = val if len(self.slices) == 1 else val[g * size:(g + 1) * size]

    def add(self, ref, val):
        self.put(ref, self.get(ref) + val)


def _block_rows(b, d, S):
    quarter = S // LAYOUT_MOD
    nb = S // (QB * d)
    r, n = b // nb, b % nb
    n_prev = jnp.maximum(n - 1, 0)
    if d == 1:
        runs = lambda m: _Rows([pl.ds(pl.multiple_of(g * quarter + RUN * m, RUN), RUN) for g in range(LAYOUT_MOD)])
        return n, runs(n), runs(n_prev)
    if d == LAYOUT_MOD:
        block = lambda m: _Rows([pl.ds(pl.multiple_of(r * quarter + QB * m, QB), QB)])
        return n, block(n), block(n_prev)
    step = d // LAYOUT_MOD
    first = (r % LAYOUT_MOD) * quarter + r // LAYOUT_MOD
    strided = lambda m: _Rows([pl.ds(first + QB * step * m, QB, stride=step)])
    return n, strided(n), strided(n_prev)


def _natural_rows(i, S):
    per = S // LAYOUT_MOD // QB
    return pl.ds(i // per + LAYOUT_MOD * QB * (i % per), QB, stride=LAYOUT_MOD)


def _head_sum_matrix():
    r = lax.broadcasted_iota(jnp.int32, (2 * QB, 2 * QB), 0)
    c = lax.broadcasted_iota(jnp.int32, (2 * QB, 2 * QB), 1)
    return (((r % QB) // HEAD_DIM) == (c // QB)).astype(F32).astype(BF)


def _hi_lo(t):
    hi = t.astype(BF)
    return jnp.concatenate([hi, (t - hi.astype(F32)).astype(BF)], axis=1)


PROJ_ROWS = 512


def _attn_fwd(u, slopes, w_all, w3_all, cw_all):
    S = u.shape[0]
    hpr = HEAD_PAIRS
    n_blocks = S // QB
    later = range(3, 3 * hpr)

    def body(sl_ref, u_ref, w_in_ref, w3_in_ref, cw_in_ref, o_ref, lse_ref, q_ref, k_ref, v_ref, w_ref, w3_ref,
             cw_ref, acc, m_s, l_s, w_tile, staged, tile_sems, *sems):
        hp = pl.program_id(0)
        me = _my_place()[3]
        pieces = _PieceGather(lambda p, lo, hi: w_ref.at[p, :, lo:hi], w_ref, *sems[0:2])
        gathers = (_WeightGather(lambda p, cols: _block(w_ref, me, cols), w_ref, *sems[2:5], REST_COLS),
                   _WeightGather(lambda p, cols: w3_ref.at[me], w3_ref, *sems[5:8], _whole),
                   _WeightGather(lambda p, cols: cw_ref.at[me], cw_ref, *sems[8:11], _whole))

        @pl.when(hp == 0)
        def _():
            pieces.start(later)
            for g in gathers:
                g.start()

        @pl.when(hp == FORWARD_STEP)
        def _():
            for g in gathers:
                g.forward()

        for h in range(hpr):
            @pl.when(hp == h)
            def _(h=h):
                if h > 0:
                    pieces.wait_recv(range(3 * h, 3 * h + 3))
                fetch = []
                for seg in range(3):
                    p, lo = _qkv_piece(h, seg)
                    fetch.append(pltpu.make_async_copy(w_ref.at[p, :, lo:lo + 128], w_tile.at[:, seg * 128:(seg + 1) * 128],
                                                       tile_sems.at[seg]))
                    fetch[-1].start()
                for cp in fetch:
                    cp.wait()

        def project(i, carry):
            rows = pl.ds(pl.multiple_of(i * PROJ_ROWS, PROJ_ROWS), PROJ_ROWS)
            qkv = _dot(u_ref[rows, :], w_tile[...])
            per = PROJ_ROWS // LAYOUT_MOD
            for seg, ref in enumerate((q_ref, k_ref, v_ref)):
                staged[seg] = qkv[:, seg * 128:(seg + 1) * 128]
                for g in range(LAYOUT_MOD):
                    dst = pl.ds(pl.multiple_of(g * (S // LAYOUT_MOD) + i * per, per), per)
                    ref[dst, :] = staged.at[seg][pl.ds(g, per, stride=LAYOUT_MOD), :]
            return carry

        lax.fori_loop(0, S // PROJ_ROWS, project, 0)

        head_sum = _head_sum_matrix()
        ones_b = jnp.ones((2 * QB, QB), BF)
        m_s[...] = jnp.full(m_s.shape, NEG, F32)
        l_s[...] = jnp.zeros(l_s.shape, F32)
        acc[...] = jnp.zeros(acc.shape, F32)

        for d in DILATIONS:
            tri_le, dist, low = _fold_masks(d)
            low_b = low.astype(F32).astype(BF)
            high_b = 1.0 - low_b
            slope = [sl_ref[2 * hp + a] * float(d) for a in range(2)]
            bias = [slope[a] * dist for a in range(2)]

            def block(b, d=d, slope=slope, bias=bias, tri_le=tri_le, low=low, low_b=low_b, high_b=high_b):
                n, cur, prev = _block_rows(b, d, S)
                has_prev = n > 0
                valid = jnp.logical_or(tri_le, has_prev)
                q2 = (cur.get(q_ref) * 0.125).astype(BF)
                qs = jnp.concatenate([q2 * low_b, q2 * high_b], axis=0)
                vp = prev.get(v_ref)
                kp_b = prev.get(k_ref).astype(BF)
                kcat = jnp.concatenate([kp_b, cur.get(k_ref).astype(BF)], axis=0)
                vcat = jnp.concatenate([vp, cur.get(v_ref)], axis=0).astype(BF)
                s2 = _dot_nt(qs, kcat)
                e2 = _dot(_hi_lo(q2.astype(F32) * kp_b.astype(F32)), head_sum)
                p_rows, alpha_h, pe_h = [], [], []
                for a in range(2):
                    sp, sc = s2[a * QB:(a + 1) * QB, :QB], s2[a * QB:(a + 1) * QB, QB:]
                    comb = jnp.where(valid, jnp.where(tri_le, sc, sp) - bias[a], NEG)
                    e = jnp.where(has_prev, e2[:, a * QB:(a + 1) * QB] - slope[a] * float(QB), NEG)
                    m_old = cur.get(m_s.at[a])
                    m_new = jnp.maximum(jnp.maximum(m_old, jnp.max(comb, axis=-1, keepdims=True)), e)
                    cur.put(m_s.at[a], m_new)
                    p = jnp.exp(comb - m_new)
                    pe_h.append(jnp.exp(e - m_new))
                    alpha_h.append(jnp.exp(m_old - m_new))
                    p_rows.append(jnp.concatenate([jnp.where(tri_le, 0.0, p).astype(BF),
                                                   jnp.where(tri_le, p, 0.0).astype(BF)], axis=1))
                pv = _dot(jnp.concatenate(p_rows, axis=0), jnp.concatenate([vcat, ones_b], axis=1))
                for a in range(2):
                    cur.put(l_s.at[a], alpha_h[a] * cur.get(l_s.at[a]) + pv[a * QB:(a + 1) * QB, QB:] + pe_h[a])
                cur.put(acc, jnp.where(low, alpha_h[0], alpha_h[1]) * cur.get(acc)
                        + jnp.where(low, pv[:QB, :QB], pv[QB:, :QB]) + jnp.where(low, pe_h[0], pe_h[1]) * vp)

            def several(it, carry, block=block):
                for u in range(ATT_UNROLL):
                    block(it * ATT_UNROLL + u)
                return carry

            lax.fori_loop(0, n_blocks // ATT_UNROLL, several, 0)

        low = _fold_masks(LAYOUT_MOD)[2]

        def finish(i, carry):
            rows = pl.ds(pl.multiple_of(i * QB, QB), QB)
            l0, l1 = l_s[0, rows, :], l_s[1, rows, :]
            o_ref[_natural_rows(i, S), :] = acc[rows, :] / jnp.where(low, l0, l1)
            lse_ref[0, rows, :] = m_s[0, rows, :] + jnp.log(l0)
            lse_ref[1, rows, :] = m_s[1, rows, :] + jnp.log(l1)
            return carry

        lax.fori_loop(0, n_blocks, finish, 0)

        @pl.when(hp == hpr - 1)
        def _():
            pieces.wait_send(later)
            for g in gathers:
                g.finish()

    col = pl.BlockSpec((S, 128), lambda h: (0, h))
    act = jax.ShapeDtypeStruct((S, D), F32)
    gathered = (w_all, w3_all, cw_all)
    return pl.pallas_call(
        body, name="attn_fwd", grid=(hpr,),
        in_specs=[SMEM_SPEC, VMEM_SPEC, ANY_SPEC, ANY_SPEC, ANY_SPEC],
        out_specs=(col, pl.BlockSpec((2, S, 128), lambda h: (0, 0, h)), col, col, col, ANY_SPEC, ANY_SPEC, ANY_SPEC),
        out_shape=(act, jax.ShapeDtypeStruct((2, S, D), F32), act, act, act,
                   *[jax.ShapeDtypeStruct(t.shape, t.dtype) for t in gathered]),
        scratch_shapes=([pltpu.VMEM((S, 128), F32), pltpu.VMEM((2, S, 128), F32), pltpu.VMEM((2, S, 128), F32),
                         pltpu.VMEM((D, 3 * 128), BF), pltpu.VMEM((3, PROJ_ROWS, 128), F32),
                         pltpu.SemaphoreType.DMA((3,))]
                        + _piece_sems(3 * hpr) + WEIGHT_GATHER_SEMS * 3),
        input_output_aliases={2: 5, 3: 6, 4: 7},
        compiler_params=_params(1),
    )(slopes, u, *gathered)


def _set_rows(shape, rows):
    idx = lax.broadcasted_iota(jnp.int32, shape, 0)
    out = jnp.zeros(shape, F32)
    for r, val in rows.items():
        out = out + jnp.where(idx == r, val, 0.0)
    return out


def _mid(yc_in, pa_mid, o, x2, target, b_merge, final_g, w3):
    S = x2.shape[0]
    tm = ROW_TILE
    nsteps = S // tm
    tile = pl.BlockSpec((tm, D), lambda i: (i, 0))

    def body(yc_ref, za_ref, gcp_ref, gap_ref, o_ref, x_ref, t_ref, b_ref, fg_ref, w_ref,
             dh_ref, dmid_ref, do_ref, dyc_ref, gw_ref, small_ref, acc, stage):
        i = pl.program_id(0)

        @pl.when(i == 0)
        def _():
            acc[...] = jnp.zeros_like(acc)
            small_ref[...] = jnp.zeros_like(small_ref)

        wc, wa, wo = w_ref[0], w_ref[1], w_ref[2]
        z = za_ref[...].astype(F32)
        sg = _sigmoid(z)
        ov = o_ref[...]
        yc_in_b, ya_in_b = yc_ref[...], (z * sg * ov).astype(BF)
        yc = _dot(yc_in_b, wc)
        ya = _dot(ya_in_b, wa)
        b = b_ref[...]
        gc = _sigmoid(gcp_ref[...].astype(F32) + b[:, :D])
        ga = _sigmoid(gap_ref[...].astype(F32) + b[:, D:])
        merged = gc * yc + ga * ya
        merged_b = merged.astype(BF)
        h = x_ref[...] + _dot(merged_b, wo)
        r2 = lax.rsqrt(jnp.mean(h * h, axis=-1, keepdims=True) + EPS)
        n = h * r2
        fg = fg_ref[...]
        err = n * fg - t_ref[...]
        loss = 0.5 * jnp.sum(jnp.sum(err * err, axis=-1, keepdims=True) / D, axis=0, keepdims=True)
        dy = err / D
        g_fg = jnp.sum(dy * n, axis=0, keepdims=True)
        dn = dy * fg
        dh = r2 * (dn - n * jnp.mean(dn * n, axis=-1, keepdims=True))
        dh_ref[...] = dh
        dh_b = dh.astype(BF)
        dmerged = _dot_nt(dh_b, wo)
        acc[2] += _dot(merged.T.astype(BF), dh_b)
        dyc = (dmerged * gc).astype(BF)
        dya = (dmerged * ga).astype(BF)
        dgcp = dmerged * yc * gc * (1.0 - gc)
        dgap = dmerged * ya * ga * (1.0 - ga)
        dmid_ref[1] = dgcp.astype(BF)
        dmid_ref[2] = dgap.astype(BF)
        acc[0] += _dot(yc_in_b.astype(F32).T.astype(BF), dyc)
        acc[1] += _dot(ya_in_b.astype(F32).T.astype(BF), dya)
        dyc_ref[...] = _dot_nt(dyc, wc).astype(BF)
        dya_in = _dot_nt(dya, wa)
        do_ref[...] = dya_in * (z * sg)
        dmid_ref[0] = (dya_in * ov * (sg * (1.0 + z * (1.0 - sg)))).astype(BF)
        small_ref[...] += _set_rows((8, D), {
            1: jnp.sum(dgcp, axis=0, keepdims=True), 2: jnp.sum(dgap, axis=0, keepdims=True),
            3: g_fg, 7: jnp.broadcast_to(loss, (1, D))})

        @pl.when(i == nsteps - 1)
        def _():
            for p in range(N_DEV):
                for a in range(3):
                    stage[...] = acc[a, p * ROW_SHARD:(p + 1) * ROW_SHARD, :].astype(BF)
                    pltpu.sync_copy(stage, gw_ref.at[p, a])

    return pl.pallas_call(
        body, name="mid", grid=(nsteps,),
        in_specs=[tile, pl.BlockSpec((tm, D), lambda i: (i, 0)), pl.BlockSpec((tm, D), lambda i: (i, 1)),
                  pl.BlockSpec((tm, D), lambda i: (i, 2)), tile, tile, tile,
                  pl.BlockSpec((1, 2 * D), lambda i: (0, 0)), pl.BlockSpec((1, D), lambda i: (0, 0)), VMEM_SPEC],
        out_specs=(tile, pl.BlockSpec((3, tm, D), lambda i: (0, i, 0)), tile, tile,
                   ANY_SPEC, pl.BlockSpec((8, D), lambda i: (0, 0))),
        out_shape=(jax.ShapeDtypeStruct((S, D), F32), jax.ShapeDtypeStruct((3, S, D), BF),
                   jax.ShapeDtypeStruct((S, D), F32), jax.ShapeDtypeStruct((S, D), BF),
                   jax.ShapeDtypeStruct((N_DEV, 3, ROW_SHARD, D), BF), jax.ShapeDtypeStruct((8, D), F32)),
        scratch_shapes=[pltpu.VMEM((3, D, D), F32), pltpu.VMEM((ROW_SHARD, D), BF)],
        compiler_params=_params(1),
    )(yc_in, pa_mid, pa_mid, pa_mid, o, x2, target, b_merge, final_g, w3)


def _conv_bwd(dyc_in, pa, cw8):
    S = pa.shape[0]
    tm, tc = CONV_TM, CONV_TC
    nct = D // tc
    nrt = S // tm
    last_halo = S // HALO - 1

    def seg(s):
        return pl.BlockSpec((tm, tc), lambda j, i, s=s: (i, s * nct + j))

    def halo_before(s):
        return pl.BlockSpec((HALO, tc), lambda j, i, s=s: (jnp.maximum(i * (tm // HALO) - 1, 0), s * nct + j))

    def halo_after(s):
        return pl.BlockSpec((HALO, tc), lambda j, i, s=s: (jnp.minimum((i + 1) * (tm // HALO), last_halo), s * nct + j))

    def body(dy, xc, bg, cg, zc, xch, cgh, dyn, bgn, zcn, cw, dout, gcw):
        i = pl.program_id(1)

        @pl.when(i == 0)
        def _():
            gcw[...] = jnp.zeros_like(gcw)

        xcv, cgv = xc[...].astype(F32), cg[...].astype(F32)
        a = cgv * xcv
        ah = jnp.where(i > 0, cgh[...].astype(F32) * xch[...].astype(F32), 0.0)
        row = lax.broadcasted_iota(jnp.int32, (tm, tc), 0)
        a1 = jnp.where(row == 0, ah[HALO - 1:HALO, :], pltpu.roll(a, 1, 0))
        a2 = jnp.where(row == 0, ah[HALO - 2:HALO - 1, :],
                       jnp.where(row == 1, ah[HALO - 1:HALO, :], pltpu.roll(a, 2, 0)))
        w = cw[...]
        conv = w[0:1, :] * a2 + w[1:2, :] * a1 + w[2:3, :] * a
        z = zc[...].astype(F32)
        sg = _sigmoid(z)
        silu = z * sg
        bgv = bg[...].astype(F32)
        dyv = dy[...].astype(F32)
        dout[3] = (dyv * bgv * conv * (sg * (1.0 + z * (1.0 - sg)))).astype(BF)
        dout[1] = (dyv * silu * conv).astype(BF)
        dc = dyv * silu * bgv
        zn = zcn[...].astype(F32)
        dcn = dyn[...].astype(F32) * (zn * _sigmoid(zn)) * bgn[...].astype(F32)
        dcn = jnp.where(i < nrt - 1, dcn, 0.0)
        dc1 = jnp.where(row == tm - 1, dcn[0:1, :], pltpu.roll(dc, tm - 1, 0))
        dc2 = jnp.where(row == tm - 1, dcn[1:2, :],
                        jnp.where(row == tm - 2, dcn[0:1, :], pltpu.roll(dc, tm - 2, 0)))
        da = w[2:3, :] * dc + w[1:2, :] * dc1 + w[0:1, :] * dc2
        dout[2] = (da * xcv).astype(BF)
        dout[0] = (da * cgv).astype(BF)
        gcw[...] += _set_rows((8, tc), {
            4: jnp.sum(dc * a2, axis=0, keepdims=True), 5: jnp.sum(dc * a1, axis=0, keepdims=True),
            6: jnp.sum(dc * a, axis=0, keepdims=True)})

    return pl.pallas_call(
        body, name="conv_bwd", grid=(nct, nrt),
        in_specs=[pl.BlockSpec((tm, tc), lambda j, i: (i, j)), seg(0), seg(1), seg(2), seg(3),
                  halo_before(0), halo_before(2),
                  pl.BlockSpec((HALO, tc), lambda j, i: (jnp.minimum((i + 1) * (tm // HALO), last_halo), j)),
                  halo_after(1), halo_after(3), pl.BlockSpec((8, tc), lambda j, i: (0, j))],
        out_specs=(pl.BlockSpec((4, tm, tc), lambda j, i: (0, i, j)), pl.BlockSpec((8, tc), lambda j, i: (0, j))),
        out_shape=(jax.ShapeDtypeStruct((4, S, D), BF), jax.ShapeDtypeStruct((8, D), F32)),
        compiler_params=_params(2),
    )(dyc_in, pa, pa, pa, pa, pa, pa, dyc_in, pa, pa, cw8)


def _attn_bwd(q, k, v, slopes, do, o, lse, g_in, g_3):
    S = q.shape[0]
    hpr = HEAD_PAIRS
    n_blocks = S // QB

    def body(sl_ref, q_ref, k_ref, v_ref, do_ref, o_ref, lse_ref, gin_ref, g3_ref, out_ref, rin_ref, r3_ref,
             dq_s, dk_s, dv_s, do_s, dd_s, *sems):
        hp = pl.program_id(0)
        exchanges = (_GradExchange(gin_ref, rin_ref, *sems[:3], _shard_cols((0, SEG0_ATTN * D), (SEG0_MID * D, IN_COLS))),
                     _GradExchange(g3_ref, r3_ref, *sems[3:], _whole))

        @pl.when(hp == 0)
        def _():
            for ex in exchanges:
                ex.start()

        head_sum = _head_sum_matrix()
        dq_s[...] = jnp.zeros(dq_s.shape, F32)
        dk_s[...] = jnp.zeros(dk_s.shape, F32)
        dv_s[...] = jnp.zeros(dv_s.shape, F32)

        def row_dots(i, carry):
            rows = pl.ds(pl.multiple_of(i * QB, QB), QB)
            natural = _natural_rows(i, S)
            do_c = do_ref[natural, :]
            do_s[rows, :] = do_c
            dd = _dot(_hi_lo(do_c * o_ref[natural, :]), head_sum)
            dd_s[0, rows, :] = dd[:, :QB]
            dd_s[1, rows, :] = dd[:, QB:]
            return carry

        lax.fori_loop(0, n_blocks, row_dots, 0)

        for d in DILATIONS:
            tri_le, dist, low = _fold_masks(d)
            low_b = low.astype(F32).astype(BF)
            high_b = 1.0 - low_b
            slope = [sl_ref[2 * hp + a] * float(d) for a in range(2)]
            bias = [slope[a] * dist for a in range(2)]

            def block(b, d=d, slope=slope, bias=bias, tri_le=tri_le, low=low, low_b=low_b, high_b=high_b):
                n, cur, prev = _block_rows(b, d, S)
                has_prev = n > 0
                valid = jnp.logical_or(tri_le, has_prev)
                q2f = cur.get(q_ref) * 0.125
                q2 = q2f.astype(BF)
                qs = jnp.concatenate([q2 * low_b, q2 * high_b], axis=0)
                kp, vp = prev.get(k_ref), prev.get(v_ref)
                kp_b, vp_b = kp.astype(BF), vp.astype(BF)
                kcat = jnp.concatenate([kp_b, cur.get(k_ref).astype(BF)], axis=0)
                vcat = jnp.concatenate([vp_b, cur.get(v_ref).astype(BF)], axis=0)
                do2f = cur.get(do_s)
                do2 = do2f.astype(BF)
                dos = jnp.concatenate([do2 * low_b, do2 * high_b], axis=0)
                s2 = _dot_nt(qs, kcat)
                dp2 = _dot_nt(dos, vcat)
                diag2 = _dot(jnp.concatenate([_hi_lo(q2.astype(F32) * kp_b.astype(F32)),
                                              _hi_lo(do2.astype(F32) * vp_b.astype(F32))], axis=0), head_sum)
                p_rows, ds_rows, pe_h, dse_h = [], [], [], []
                for a in range(2):
                    hs = slice(a * QB, (a + 1) * QB)
                    sp, sc = s2[hs, :QB], s2[hs, QB:]
                    dpp, dpc = dp2[hs, :QB], dp2[hs, QB:]
                    lse_a, dd_a = cur.get(lse_ref.at[a]), cur.get(dd_s.at[a])
                    comb = jnp.where(tri_le, sc, sp) - bias[a]
                    e = diag2[:QB, hs] - slope[a] * float(QB)
                    p = jnp.where(valid, jnp.exp(comb - lse_a), 0.0)
                    pe = jnp.where(has_prev, jnp.exp(e - lse_a), 0.0)
                    ds = p * (jnp.where(tri_le, dpc, dpp) - dd_a)
                    dse_h.append(pe * (diag2[QB:, hs] - dd_a))
                    pe_h.append(pe)
                    p_rows.append(jnp.concatenate([jnp.where(tri_le, 0.0, p).astype(BF),
                                                   jnp.where(tri_le, p, 0.0).astype(BF)], axis=1))
                    ds_rows.append(jnp.concatenate([jnp.where(tri_le, 0.0, ds).astype(BF),
                                                    jnp.where(tri_le, ds, 0.0).astype(BF)], axis=1))
                pst = jnp.concatenate(p_rows, axis=0)
                dst = jnp.concatenate(ds_rows, axis=0)
                pe2 = jnp.where(low, pe_h[0], pe_h[1])
                dse2 = jnp.where(low, dse_h[0], dse_h[1])
                dq = _dot(dst, kcat)
                cur.add(dq_s, (jnp.where(low, dq[:QB], dq[QB:]) + dse2 * kp) * 0.125)
                dk = _dot_tn(dst, qs)
                dv = _dot_tn(pst, dos)
                prev.add(dk_s, dk[:QB] + dse2 * q2f)
                cur.add(dk_s, dk[QB:])
                prev.add(dv_s, dv[:QB] + pe2 * do2f)
                cur.add(dv_s, dv[QB:])

            def several(it, carry, block=block):
                for u in range(ATT_UNROLL):
                    block(it * ATT_UNROLL + u)
                return carry

            lax.fori_loop(0, n_blocks // ATT_UNROLL, several, 0)

        def finish(i, carry):
            rows = pl.ds(pl.multiple_of(i * QB, QB), QB)
            natural = _natural_rows(i, S)
            for t, ref in enumerate((dq_s, dk_s, dv_s)):
                out_ref.at[t][natural, :] = ref[rows, :]
            return carry

        lax.fori_loop(0, n_blocks, finish, 0)

        @pl.when(hp == hpr - 1)
        def _():
            for ex in exchanges:
                ex.finish()

    col = pl.BlockSpec((S, 128), lambda h: (0, h))
    return pl.pallas_call(
        body, name="attn_bwd", grid=(hpr,),
        in_specs=[SMEM_SPEC, col, col, col, col, col, pl.BlockSpec((2, S, 128), lambda h: (0, 0, h)),
                  ANY_SPEC, ANY_SPEC],
        out_specs=(pl.BlockSpec((3, S, 128), lambda h: (0, 0, h)), ANY_SPEC, ANY_SPEC),
        out_shape=(jax.ShapeDtypeStruct((3, S, D), F32), jax.ShapeDtypeStruct(g_in.shape, BF),
                   jax.ShapeDtypeStruct(g_3.shape, BF)),
        scratch_shapes=([pltpu.VMEM((S, 128), F32)] * 4 + [pltpu.VMEM((2, S, 128), F32)]
                        + GRAD_EXCHANGE_SEMS + GRAD_EXCHANGE_SEMS),
        compiler_params=_params(1),
    )(slopes, q, k, v, do, o, lse, g_in, g_3)


WG_TN = 256
SEG0_CONV, SEG0_ATTN, SEG0_MID = 0, 4, 7


def _wgrad_in(ut, d_group, seg0, g_in, name):
    S = ut.shape[1]
    tn = WG_TN
    per_seg = D // tn
    per_shard = W_IN_SHARD // tn
    n_tiles = d_group.shape[0] * per_seg
    tile0 = seg0 * per_seg

    def body(ut_ref, d_ref, *rest):
        rest[-1][0] = _dot(ut_ref[...], d_ref[0].astype(BF)).astype(BF)

    operands, in_specs, aliases = [ut, d_group], [VMEM_SPEC, pl.BlockSpec((1, S, tn), lambda t: (t // per_seg, 0, t % per_seg))], {}
    if g_in is not None:
        operands.append(g_in)
        in_specs.append(ANY_SPEC)
        aliases = {2: 0}
    return pl.pallas_call(
        body, name=name, grid=(n_tiles,), in_specs=in_specs,
        out_specs=pl.BlockSpec((1, D, tn), lambda t: ((tile0 + t) // per_shard, 0, (tile0 + t) % per_shard)),
        out_shape=jax.ShapeDtypeStruct((N_DEV, D, W_IN_SHARD), BF),
        input_output_aliases=aliases,
        compiler_params=_params(1),
    )(*operands)


def _dgrad_norm_bwd(d_conv, d_attn, d_mid, w_all, x2, dh, norm_g):
    S = x2.shape[0]
    tm = ROW_TILE
    nsteps = S // tm
    tile = pl.BlockSpec((tm, D), lambda i: (i, 0))
    pieces = _proj_pieces()

    def body(a_ref, b_ref, c_ref, w_ref, x_ref, dh_ref, g_ref, gx_ref, small_ref):
        i = pl.program_id(0)

        @pl.when(i == 0)
        def _():
            small_ref[...] = jnp.zeros_like(small_ref)

        groups = (a_ref, b_ref, c_ref)
        du = jnp.zeros((tm, D), F32)
        for s, sc, p, pc, width in pieces:
            g = 0 if s < 4 else (1 if s < 7 else 2)
            local = s - (0, 4, 7)[g]
            du = du + _dot_nt(groups[g][local, :, sc:sc + width].astype(BF), w_ref[p, :, pc:pc + width])
        xv = x_ref[...]
        r = lax.rsqrt(jnp.mean(xv * xv, axis=-1, keepdims=True) + EPS)
        n = xv * r
        dn = du * g_ref[...]
        gx_ref[...] = dh_ref[...] + r * (dn - n * jnp.mean(dn * n, axis=-1, keepdims=True))
        small_ref[...] += _set_rows((8, D), {0: jnp.sum(du * n, axis=0, keepdims=True)})

    return pl.pallas_call(
        body, name="dgrad_norm_bwd", grid=(nsteps,),
        in_specs=[pl.BlockSpec((4, tm, D), lambda i: (0, i, 0)), pl.BlockSpec((3, tm, D), lambda i: (0, i, 0)),
                  pl.BlockSpec((3, tm, D), lambda i: (0, i, 0)), VMEM_SPEC, tile, tile,
                  pl.BlockSpec((1, D), lambda i: (0, 0))],
        out_specs=(tile, pl.BlockSpec((8, D), lambda i: (0, 0))),
        out_shape=(jax.ShapeDtypeStruct((S, D), F32), jax.ShapeDtypeStruct((8, D), F32)),
        compiler_params=_params(1),
    )(d_conv, d_attn, d_mid, w_all, x2, dh, norm_g)


HBM_SPEC = pl.BlockSpec(memory_space=pltpu.HBM)
SEM_SPEC = pl.BlockSpec(memory_space=pltpu.SEMAPHORE)
ATTN_COLS = _shard_cols((SEG0_ATTN * D, SEG0_MID * D))


def _attn_cols_exchange_start(g_in, r_in):
    def body(g_ref, r_ref, send_sems, recv_sems, g_thru, r_thru, token):
        _GradExchange(g_ref, r_ref, send_sems, recv_sems, None, ATTN_COLS).start()
        token[...] = jnp.zeros_like(token)

    hbm = pltpu.with_memory_space_constraint
    return pl.pallas_call(
        body, name="attn_cols_exchange_start",
        out_shape=(pltpu.SemaphoreType.DMA((N_DEV,)), pltpu.SemaphoreType.DMA((N_DEV,)),
                   pltpu.HBM(g_in.shape, g_in.dtype), pltpu.HBM(r_in.shape, r_in.dtype),
                   jax.ShapeDtypeStruct((8, 128), F32)),
        in_specs=(HBM_SPEC, HBM_SPEC), out_specs=(SEM_SPEC, SEM_SPEC, HBM_SPEC, HBM_SPEC, VMEM_SPEC),
        input_output_aliases={0: 2, 1: 3},
        compiler_params=pltpu.CompilerParams(has_side_effects=pltpu.SideEffectType.DATAFLOW_SIDE_EFFECTING),
    )(hbm(g_in, pltpu.HBM), hbm(r_in, pltpu.HBM))


def _attn_cols_exchange_wait(send_sems, recv_sems, g_thru, r_thru, after):
    def body(g_ref, r_ref, send_sems, recv_sems, after_ref, g_dead, r_out):
        _GradExchange(g_ref, r_ref, send_sems, recv_sems, None, ATTN_COLS).finish()

    return pl.pallas_call(
        body, name="attn_cols_exchange_wait",
        out_shape=(pltpu.HBM(g_thru.shape, g_thru.dtype), pltpu.HBM(r_thru.shape, r_thru.dtype)),
        in_specs=(HBM_SPEC, HBM_SPEC, SEM_SPEC, SEM_SPEC, ANY_SPEC), out_specs=(HBM_SPEC, HBM_SPEC),
        input_output_aliases={0: 0, 1: 1},
        compiler_params=pltpu.CompilerParams(has_side_effects=pltpu.SideEffectType.DATAFLOW_SIDE_EFFECTING),
    )(g_thru, r_thru, send_sems, recv_sems, after)


def _adamw_math(w, g, m, v):
    m = ADAM_B1 * m + (1.0 - ADAM_B1) * g
    v = ADAM_B2 * v + (1.0 - ADAM_B2) * (g * g)
    m_hat = m / (1.0 - ADAM_B1 ** ADAM_STEP)
    v_hat = v / (1.0 - ADAM_B2 ** ADAM_STEP)
    delta = -ADAM_LR * (m_hat / (jnp.sqrt(v_hat) + ADAM_EPS) + ADAM_WD * w)
    return delta, m, v


def _sum_adamw(parts, w, m, v, tm, name):
    R, C = w.shape
    tile = pl.BlockSpec((tm, C), lambda i: (i, 0))

    def body(p_ref, w_ref, m_ref, v_ref, g_out, d_out, m_out, v_out):
        g = p_ref[0].astype(F32)
        for s in range(1, N_DEV):
            g = g + p_ref[s].astype(F32)
        g_out[...] = g
        d_out[...], m_out[...], v_out[...] = _adamw_math(w_ref[...], g, m_ref[...], v_ref[...])

    shape = jax.ShapeDtypeStruct((R, C), F32)
    return pl.pallas_call(
        body, name=name, grid=(R // tm,),
        in_specs=[pl.BlockSpec((N_DEV, tm, C), lambda i: (0, i, 0)), tile, tile, tile],
        out_specs=(tile, tile, tile, tile), out_shape=(shape, shape, shape, shape),
        compiler_params=_params(1),
    )(parts, w, m, v)


def _adamw(g, w, m, v, name):
    def body(g_ref, w_ref, m_ref, v_ref, d_out, m_out, v_out):
        d_out[...], m_out[...], v_out[...] = _adamw_math(w_ref[...], g_ref[...], m_ref[...], v_ref[...])

    shape = jax.ShapeDtypeStruct(w.shape, F32)
    return pl.pallas_call(
        body, name=name, in_specs=[VMEM_SPEC] * 4, out_specs=(VMEM_SPEC,) * 3, out_shape=(shape, shape, shape),
    )(g, w, m, v)


def _alibi_slopes():
    return jnp.exp2(-8.0 * jnp.arange(1, N_HEADS + 1, dtype=F32) / N_HEADS)


def _local_step(x2, target, norm_g, b_merge, final_g, w_in, w3_shard, cw_shard):
    slopes = _alibi_slopes()
    u, ut, w_all, w3_all, cw_all = _norm_gather_first_weights(x2, norm_g, w_in, w3_shard, cw_shard)
    o, lse, q, k, v, w_all, w3_all, cw_all = _attn_fwd(u, slopes, w_all, w3_all, cw_all)
    w3 = jnp.transpose(w3_all, (1, 0, 2, 3)).reshape(3, D, D)
    cw8 = jnp.transpose(cw_all, (1, 0, 2)).reshape(8, D)
    pa = _proj_cols(u, w_all, SEG0_CONV, 4, BF, "proj_conv")
    yc_in = _conv_fwd(pa, cw8)
    pa_mid = _proj_cols(u, w_all, SEG0_MID, 3, BF, "proj_mid")
    dh, d_mid, do, dyc_in, g_3, small_mid = _mid(yc_in, pa_mid, o, x2, target, b_merge, final_g, w3)
    g_in = _wgrad_in(ut, d_mid, SEG0_MID, None, "wgrad_in_mid")
    d_conv, small_conv = _conv_bwd(dyc_in, pa, cw8)
    g_in = _wgrad_in(ut, d_conv, SEG0_CONV, g_in, "wgrad_in_conv")
    d_attn, r_in, r_3 = _attn_bwd(q, k, v, slopes, do, o, lse, g_in, g_3)
    g_in = _wgrad_in(ut, d_attn, SEG0_ATTN, g_in, "wgrad_in_attn")
    *in_flight, token = _attn_cols_exchange_start(g_in, r_in)
    grad_x, small_norm = _dgrad_norm_bwd(d_conv, d_attn, d_mid, w_all, x2, dh, norm_g + token[0:1, 0:1])
    return grad_x, in_flight, r_3, small_mid, small_conv, small_norm


def kernel(x, norm_g, w_in, b_merge, conv_w, w_out_conv, w_out_attn, w_o, final_g, loss_target, m_norm_g, m_w_in, m_b_merge, m_conv_w, m_w_out_conv, m_w_out_attn, m_w_o, m_final_g, v_norm_g, v_w_in, v_b_merge, v_conv_w, v_w_out_conv, v_w_out_attn, v_w_o, v_final_g):
    me = 4 * lax.axis_index("x") + 2 * lax.axis_index("y") + lax.axis_index("c")
    stack3 = lambda a, b, c: jnp.concatenate([a, b, c], axis=0)
    pad8 = lambda a: jnp.pad(a, ((0, 8 - a.shape[0]), (0, 0)))

    w3_shard = stack3(w_out_conv, w_out_attn, w_o)
    final_g2 = final_g.reshape(1, D)
    grad_x, in_flight, r_3, small_mid, small_conv, small_norm = _local_step(
        x[0], loss_target[0], norm_g, b_merge, final_g2, w_in[0], w3_shard, pad8(conv_w[0]))

    small = _allreduce_small(small_mid, small_conv, small_norm)
    g_in, r_in = _attn_cols_exchange_wait(*in_flight, small)
    own = lax.dynamic_index_in_dim(g_in, me, 0, keepdims=True)
    r_in = lax.dynamic_update_slice(r_in, own, (me, 0, 0))

    g_w_in, d_w_in, nm_w_in, nv_w_in = _sum_adamw(r_in, w_in[0], m_w_in[0], v_w_in[0], 128, "adamw_w_in")
    g_w3, d_w3, nm_w3, nv_w3 = _sum_adamw(
        r_3.reshape(N_DEV, 3 * ROW_SHARD, D), w3_shard.reshape(3 * ROW_SHARD, D),
        stack3(m_w_out_conv, m_w_out_attn, m_w_o).reshape(3 * ROW_SHARD, D),
        stack3(v_w_out_conv, v_w_out_attn, v_w_o).reshape(3 * ROW_SHARD, D), ROW_SHARD, "adamw_w3")

    def pack(ng, bm, fg):
        return pad8(jnp.concatenate([ng, bm.reshape(2, D), fg.reshape(1, D)], axis=0))

    d_s, nm_s, nv_s = _adamw(small, pack(norm_g, b_merge, final_g), pack(m_norm_g, m_b_merge, m_final_g),
                             pack(v_norm_g, v_b_merge, v_final_g), "adamw_small")
    g_cw = lax.dynamic_slice(small, (4, me * ROW_SHARD), (3, ROW_SHARD))
    d_cw, nm_cw, nv_cw = _adamw(g_cw, conv_w[0], m_conv_w[0], v_conv_w[0], "adamw_conv_w")

    loss = small[7, 0]
    split3 = lambda t: tuple(t[a * ROW_SHARD:(a + 1) * ROW_SHARD][None] for a in range(3))
    unpack = lambda t: (t[0:1], t[1:3].reshape(1, 2 * D), t[3])

    def leaves(in_, small_, cw_, w3_):
        ng, bm, fg = unpack(small_)
        wc, wa, wo = split3(w3_)
        return (ng, in_[None], bm, cw_[None], wc, wa, wo, fg)

    return (loss, grad_x[None],
            *leaves(g_w_in, small, g_cw, g_w3),
            *leaves(d_w_in, d_s, d_cw, d_w3),
            *leaves(nm_w_in, nm_s, nm_cw, nm_w3),
            *leaves(nv_w_in, nv_s, nv_cw, nv_w3))
```

```python
import functools

import jax
import jax.numpy as jnp
from jax import lax
from jax.experimental import pallas as pl
from jax.experimental.pallas import tpu as pltpu

D = 1024
N_HEADS = 16
HEAD_DIM = 64
N_SEG = 10
IN_COLS = N_SEG * D
N_DEV = 8
W_IN_SHARD = IN_COLS // N_DEV
ROW_SHARD = D // N_DEV
QB = 128
DILATIONS = (1, 4, 16)
EPS = 1e-6
NEG = -1e30
BF = jnp.bfloat16
F32 = jnp.float32
MESH = pl.DeviceIdType.MESH

ADAM_LR = 0.001
ADAM_B1 = 0.9
ADAM_B2 = 0.999
ADAM_EPS = 1e-08
ADAM_WD = 0.01
ADAM_STEP = 10

V7X_VMEM_BYTES = 64 * 1024 * 1024
VMEM_LIMIT = V7X_VMEM_BYTES - 8 * 1024 * 1024
ROW_TILE = 256

VMEM_SPEC = pl.BlockSpec(memory_space=pltpu.VMEM)
ANY_SPEC = pl.BlockSpec(memory_space=pl.ANY)
SMEM_SPEC = pl.BlockSpec(memory_space=pltpu.SMEM)


def _params(n_grid_axes, vmem=VMEM_LIMIT):
    return pltpu.CompilerParams(dimension_semantics=("arbitrary",) * n_grid_axes, vmem_limit_bytes=vmem)


def _dot(a, b):
    return jnp.dot(a, b, preferred_element_type=F32)


def _dot_nt(a, b):
    return lax.dot_general(a, b, (((1,), (1,)), ((), ())), preferred_element_type=F32)


def _dot_tn(a, b):
    return lax.dot_general(a, b, (((0,), (0,)), ((), ())), preferred_element_type=F32)


def _sigmoid(z):
    return 1.0 / (1.0 + jnp.exp(-z))


def _my_place():
    x, y, c = lax.axis_index("x"), lax.axis_index("y"), lax.axis_index("c")
    return x, y, c, 4 * x + 2 * y + c


def _peers(x, y, c):
    out = []
    for k in range(1, N_DEV):
        px = 1 - x if k & 4 else x
        py = 1 - y if k & 2 else y
        pc = 1 - c if k & 1 else c
        out.append(((px, py, pc), 4 * px + 2 * py + pc))
    return out


def _device(p):
    return (p >> 2, (p >> 1) & 1, p & 1)


def _shard_cols(*ranges):
    def cols(p):
        found = None
        for lo, hi in ranges:
            a, b = max(lo, p * W_IN_SHARD), min(hi, (p + 1) * W_IN_SHARD)
            if a < b:
                assert found is None
                found = (a - p * W_IN_SHARD, b - p * W_IN_SHARD)
        return found

    return cols


def _whole(p):
    return ()


def _block(ref, idx, cols):
    return ref.at[idx] if cols == () else ref.at[idx, :, cols[0]:cols[1]]


class _WeightGather:
    def __init__(self, src, dst, send_sems, forward_sems, recv_sems, cols):
        self.src, self.dst, self.cols = src, dst, cols
        self.send_sems, self.forward_sems, self.recv_sems = send_sems, forward_sems, recv_sems
        self.me = _my_place()[3]

    def _copy(self, p, target, passing_on=False):
        cols = self.cols(p)
        return pltpu.make_async_remote_copy(
            src_ref=_block(self.dst, p, cols) if passing_on else self.src(p, cols), dst_ref=_block(self.dst, p, cols),
            send_sem=self.forward_sems.at[p] if passing_on else self.send_sems.at[target],
            recv_sem=self.recv_sems.at[p], device_id=_device(target), device_id_type=MESH)

    def _as_each_device(self, own, relayed, other):
        for m in range(N_DEV):
            def branch(m=m):
                for p in range(N_DEV):
                    if self.cols(p) is None:
                        continue
                    if p == m:
                        for t in [m ^ 1] + [q for q in range(N_DEV) if q >> 1 != m >> 1 and q & 1 == m & 1]:
                            own(self._copy(m, t))
                    elif p >> 1 != m >> 1 and p & 1 == m & 1:
                        relayed(p, m ^ 1)
                    else:
                        other(p)

            pl.when(self.me == m)(branch)

    def start(self):
        self._as_each_device(lambda cp: cp.start(), lambda p, t: None, lambda p: None)

    def forward(self):
        def pass_on(p, t):
            self._copy(p, p).wait_recv()
            self._copy(p, t, passing_on=True).start()

        self._as_each_device(lambda cp: None, pass_on, lambda p: None)

    def finish(self):
        self._as_each_device(lambda cp: cp.wait_send(), lambda p, t: self._copy(p, t, passing_on=True).wait_send(),
                             lambda p: self._copy(p, p).wait_recv())


WEIGHT_GATHER_SEMS = [pltpu.SemaphoreType.DMA((N_DEV,))] * 3
FORWARD_STEP = 7
REST_COLS = _shard_cols((0, 4 * D), (7 * D, IN_COLS))
HEAD_PAIRS = D // 128


def _qkv_piece(h, seg):
    col = (4 + seg) * D + 128 * h
    return col // W_IN_SHARD, col % W_IN_SHARD


class _PieceGather:
    def __init__(self, src, dst, send_sems, recv_sems):
        self.src, self.dst, self.send_sems, self.recv_sems = src, dst, send_sems, recv_sems
        self.me = _my_place()[3]

    def _copy(self, i, target):
        p, lo = _qkv_piece(i // 3, i % 3)
        return pltpu.make_async_remote_copy(
            src_ref=self.src(p, lo, lo + 128), dst_ref=self.dst.at[p, :, lo:lo + 128], send_sem=self.send_sems.at[i, target],
            recv_sem=self.recv_sems.at[i], device_id=_device(target), device_id_type=MESH)

    def _owner(self, i, act):
        p = _qkv_piece(i // 3, i % 3)[0]

        def sender():
            for k in range(N_DEV - 1):
                act(self._copy(i, (p + 1 + (k + i) % (N_DEV - 1)) % N_DEV))

        pl.when(self.me == p)(sender)

    def start(self, pieces):
        for i in pieces:
            self._owner(i, lambda cp: cp.start())

    def wait_send(self, pieces):
        for i in pieces:
            self._owner(i, lambda cp: cp.wait_send())

    def wait_recv(self, pieces):
        for i in pieces:
            p = _qkv_piece(i // 3, i % 3)[0]
            pl.when(self.me != p)(lambda i=i, p=p: self._copy(i, p).wait_recv())


def _piece_sems(n):
    return [pltpu.SemaphoreType.DMA((n, N_DEV)), pltpu.SemaphoreType.DMA((n,))]


def _norm_gather_first_weights(x2, norm_g, w_in, w3, cw):
    S = x2.shape[0]
    tm = ROW_TILE
    nsteps = S // tm

    def body(x_ref, g_ref, w_in_ref, w3_ref, cw_ref, u_ref, ut_ref, o_in, o_3, o_cw, in_bf, w3_bf, local_sems, *sems):
        i = pl.program_id(0)
        me = _my_place()[3]
        gather = _PieceGather(lambda p, lo, hi: in_bf.at[:, lo:hi], o_in, *sems)
        local = [pltpu.make_async_copy(src, dst.at[me], local_sems.at[a])
                 for a, (src, dst) in enumerate(((in_bf, o_in), (w3_bf, o_3), (cw_ref, o_cw)))]

        @pl.when(i == 0)
        def _():
            def cast_rows(r, carry):
                rows = pl.ds(pl.multiple_of(r * 128, 128), 128)
                in_bf[rows, :] = w_in_ref[rows, :].astype(BF)
                return carry

            lax.fori_loop(0, D // 128, cast_rows, 0)
            for a in range(3):
                w3_bf[a] = w3_ref[a].astype(BF)
            gather.start(range(3))
            for cp in local:
                cp.start()

        xv = x_ref[...]
        r = lax.rsqrt(jnp.mean(xv * xv, axis=-1, keepdims=True) + EPS)
        u = xv * r * g_ref[...]
        u_ref[...] = u.astype(BF)
        ut_ref[...] = u.T.astype(BF)

        @pl.when(i == nsteps - 1)
        def _():
            gather.wait_recv(range(3))
            gather.wait_send(range(3))
            for cp in local:
                cp.wait()

    return pl.pallas_call(
        body, name="norm_gather_first_weights", grid=(nsteps,),
        out_shape=(jax.ShapeDtypeStruct((S, D), BF), jax.ShapeDtypeStruct((D, S), BF),
                   jax.ShapeDtypeStruct((N_DEV, D, W_IN_SHARD), BF),
                   jax.ShapeDtypeStruct((N_DEV, 3, ROW_SHARD, D), BF),
                   jax.ShapeDtypeStruct((N_DEV, 8, 128), F32)),
        in_specs=[pl.BlockSpec((tm, D), lambda i: (i, 0)), pl.BlockSpec((1, D), lambda i: (0, 0)),
                  VMEM_SPEC, VMEM_SPEC, VMEM_SPEC],
        out_specs=(pl.BlockSpec((tm, D), lambda i: (i, 0)), pl.BlockSpec((D, tm), lambda i: (0, i)),
                   ANY_SPEC, ANY_SPEC, ANY_SPEC),
        scratch_shapes=[pltpu.VMEM((D, W_IN_SHARD), BF), pltpu.VMEM((3, ROW_SHARD, D), BF),
                        pltpu.SemaphoreType.DMA((3,))] + _piece_sems(3),
        compiler_params=_params(1),
    )(x2, norm_g, w_in, w3, cw)


class _GradExchange:
    def __init__(self, src, dst, send_sems, recv_sems, local_sem, cols):
        self.src, self.dst, self.cols = src, dst, cols
        self.send_sems, self.recv_sems, self.local_sem = send_sems, recv_sems, local_sem
        self.me = _my_place()[3]

    def _remote(self, p, source):
        return pltpu.make_async_remote_copy(
            src_ref=_block(self.src, p, self.cols(p)), dst_ref=_block(self.dst, source, self.cols(p)),
            send_sem=self.send_sems.at[p], recv_sem=self.recv_sems.at[source],
            device_id=_device(p), device_id_type=MESH)

    def _local(self, p):
        return pltpu.make_async_copy(_block(self.src, p, self.cols(p)), _block(self.dst, p, self.cols(p)),
                                     self.local_sem)

    def _as_each_device(self, send, local, receive):
        for m in range(N_DEV):
            def branch(m=m):
                for k in range(1, N_DEV):
                    p = (m + k) % N_DEV
                    if self.cols(p) is not None:
                        send(self._remote(p, m))
                if self.cols(m) is not None:
                    if self.local_sem is not None:
                        local(self._local(m))
                    for k in range(1, N_DEV):
                        receive(self._remote(m, (m + k) % N_DEV))

            pl.when(self.me == m)(branch)

    def start(self):
        self._as_each_device(lambda cp: cp.start(), lambda cp: cp.start(), lambda cp: None)

    def finish(self):
        self._as_each_device(lambda cp: cp.wait_send(), lambda cp: cp.wait(), lambda cp: cp.wait_recv())


GRAD_EXCHANGE_SEMS = [pltpu.SemaphoreType.DMA((N_DEV,)), pltpu.SemaphoreType.DMA((N_DEV,)), pltpu.SemaphoreType.DMA]


def _allreduce_small(p_mid, p_conv, p_norm):
    def body(a_ref, b_ref, c_ref, out_ref, mine, gathered, send_sems, recv_sems):
        x, y, c, me = _my_place()
        mine[...] = a_ref[...] + b_ref[...] + c_ref[...]
        gathered[me] = mine[...]
        remote = []
        for k, (peer, _) in enumerate(_peers(x, y, c)):
            cp = pltpu.make_async_remote_copy(
                src_ref=mine, dst_ref=gathered.at[me], send_sem=send_sems.at[k], recv_sem=recv_sems.at[k],
                device_id=peer, device_id_type=MESH)
            cp.start()
            remote.append(cp)
        for cp in remote:
            cp.wait()
        total = gathered[0]
        for s in range(1, N_DEV):
            total = total + gathered[s]
        out_ref[...] = total

    return pl.pallas_call(
        body, name="allreduce_small",
        out_shape=jax.ShapeDtypeStruct((8, D), F32),
        in_specs=[VMEM_SPEC, VMEM_SPEC, VMEM_SPEC], out_specs=VMEM_SPEC,
        scratch_shapes=[pltpu.VMEM((8, D), F32), pltpu.VMEM((N_DEV, 8, D), F32),
                        pltpu.SemaphoreType.DMA((N_DEV - 1,)), pltpu.SemaphoreType.DMA((N_DEV - 1,))],
    )(p_mid, p_conv, p_norm)


def _proj_pieces():
    cuts = sorted(set(range(0, IN_COLS + 1, D)) | set(range(0, IN_COLS + 1, W_IN_SHARD)))
    return [(lo // D, lo % D, lo // W_IN_SHARD, lo % W_IN_SHARD, hi - lo) for lo, hi in zip(cuts[:-1], cuts[1:])]


PROJ_TN = 256


def _proj_cols(u, w_all, seg0, n_seg, dtype, name):
    S = u.shape[0]
    tn = PROJ_TN
    per_shard = W_IN_SHARD // tn
    tile0 = seg0 * D // tn

    def body(u_ref, w_ref, out_ref):
        out_ref[...] = _dot(u_ref[...], w_ref[0]).astype(dtype)

    return pl.pallas_call(
        body, name=name, grid=(n_seg * D // tn,),
        in_specs=[VMEM_SPEC, pl.BlockSpec((1, D, tn), lambda t: ((tile0 + t) // per_shard, 0, (tile0 + t) % per_shard))],
        out_specs=pl.BlockSpec((S, tn), lambda t: (0, t)),
        out_shape=jax.ShapeDtypeStruct((S, n_seg * D), dtype),
        compiler_params=_params(1),
    )(u, w_all)


CONV_TM, CONV_TC = 256, 512
HALO = 16


def _conv_fwd(pa, cw8):
    S = pa.shape[0]
    tm, tc = CONV_TM, CONV_TC
    nct = D // tc

    def seg(s):
        return pl.BlockSpec((tm, tc), lambda i, j, s=s: (i, s * nct + j))

    def halo_before(s):
        return pl.BlockSpec((HALO, tc), lambda i, j, s=s: (jnp.maximum(i * (tm // HALO) - 1, 0), s * nct + j))

    def body(xc, bg, cg, zc, xch, cgh, cw, out):
        i = pl.program_id(0)
        a = cg[...].astype(F32) * xc[...].astype(F32)
        ah = cgh[...].astype(F32) * xch[...].astype(F32)
        ah = jnp.where(i > 0, ah, 0.0)
        row = lax.broadcasted_iota(jnp.int32, (tm, tc), 0)
        a1 = jnp.where(row == 0, ah[HALO - 1:HALO, :], pltpu.roll(a, 1, 0))
        a2 = jnp.where(row == 0, ah[HALO - 2:HALO - 1, :],
                       jnp.where(row == 1, ah[HALO - 1:HALO, :], pltpu.roll(a, 2, 0)))
        w = cw[...]
        conv = w[0:1, :] * a2 + w[1:2, :] * a1 + w[2:3, :] * a
        z = zc[...].astype(F32)
        out[...] = (z * _sigmoid(z) * bg[...].astype(F32) * conv).astype(BF)

    return pl.pallas_call(
        body, name="conv_fwd", grid=(S // tm, nct),
        in_specs=[seg(0), seg(1), seg(2), seg(3), halo_before(0), halo_before(2),
                  pl.BlockSpec((8, tc), lambda i, j: (0, j))],
        out_specs=pl.BlockSpec((tm, tc), lambda i, j: (i, j)),
        out_shape=jax.ShapeDtypeStruct((S, D), BF),
        compiler_params=_params(2),
    )(pa, pa, pa, pa, pa, pa, cw8)


ATT_UNROLL = 32


LAYOUT_MOD = 4
RUN = QB // LAYOUT_MOD


def _fold_masks(d):
    row = lax.broadcasted_iota(jnp.int32, (QB, QB), 0)
    lane = lax.broadcasted_iota(jnp.int32, (QB, QB), 1)
    if d == 1:
        qpos, kpos = LAYOUT_MOD * (row % RUN) + row // RUN, LAYOUT_MOD * (lane % RUN) + lane // RUN
    else:
        qpos, kpos = row, lane
    tri_le = kpos <= qpos
    dist = jnp.where(tri_le, qpos - kpos, qpos - kpos + QB).astype(F32)
    return tri_le, dist, lane < HEAD_DIM


class _Rows:
    def __init__(self, slices):
        self.slices = slices

    def get(self, ref):
        parts = [ref[sl, :] for sl in self.slices]
        return parts[0] if len(parts) == 1 else jnp.concatenate(parts, axis=0)

    def put(self, ref, val):
        size = QB // len(self.slices)
        for g, sl in enumerate(self.slices):
            ref[sl, :] = val if len(self.slices) == 1 else val[g * size:(g + 1) * size]

    def add(self, ref, val):
        self.put(ref, self.get(ref) + val)


def _block_rows(b, d, S):
    quarter = S // LAYOUT_MOD
    nb = S // (QB * d)
    r, n = b // nb, b % nb
    n_prev = jnp.maximum(n - 1, 0)
    if d == 1:
        runs = lambda m: _Rows([pl.ds(pl.multiple_of(g * quarter + RUN * m, RUN), RUN) for g in range(LAYOUT_MOD)])
        return n, runs(n), runs(n_prev)
    if d == LAYOUT_MOD:
        block = lambda m: _Rows([pl.ds(pl.multiple_of(r * quarter + QB * m, QB), QB)])
        return n, block(n), block(n_prev)
    step = d // LAYOUT_MOD
    first = (r % LAYOUT_MOD) * quarter + r // LAYOUT_MOD
    strided = lambda m: _Rows([pl.ds(first + QB * step * m, QB, stride=step)])
    return n, strided(n), strided(n_prev)


def _natural_rows(i, S):
    per = S // LAYOUT_MOD // QB
    return pl.ds(i // per + LAYOUT_MOD * QB * (i % per), QB, stride=LAYOUT_MOD)


def _head_sum_matrix():
    r = lax.broadcasted_iota(jnp.int32, (2 * QB, 2 * QB), 0)
    c = lax.broadcasted_iota(jnp.int32, (2 * QB, 2 * QB), 1)
    return (((r % QB) // HEAD_DIM) == (c // QB)).astype(F32).astype(BF)


def _hi_lo(t):
    hi = t.astype(BF)
    return jnp.concatenate([hi, (t - hi.astype(F32)).astype(BF)], axis=1)


PROJ_ROWS = 512


def _attn_fwd(u, slopes, w_all, w3_all, cw_all):
    S = u.shape[0]
    hpr = HEAD_PAIRS
    n_blocks = S // QB
    later = range(3, 3 * hpr)

    def body(sl_ref, u_ref, w_in_ref, w3_in_ref, cw_in_ref, o_ref, lse_ref, q_ref, k_ref, v_ref, w_ref, w3_ref,
             cw_ref, acc, m_s, l_s, w_tile, staged, tile_sems, *sems):
        hp = pl.program_id(0)
        me = _my_place()[3]
        pieces = _PieceGather(lambda p, lo, hi: w_ref.at[p, :, lo:hi], w_ref, *sems[0:2])
        gathers = (_WeightGather(lambda p, cols: _block(w_ref, me, cols), w_ref, *sems[2:5], REST_COLS),
                   _WeightGather(lambda p, cols: w3_ref.at[me], w3_ref, *sems[5:8], _whole),
                   _WeightGather(lambda p, cols: cw_ref.at[me], cw_ref, *sems[8:11], _whole))

        @pl.when(hp == 0)
        def _():
            pieces.start(later)
            for g in gathers:
                g.start()

        @pl.when(hp == FORWARD_STEP)
        def _():
            for g in gathers:
                g.forward()

        for h in range(hpr):
            @pl.when(hp == h)
            def _(h=h):
                if h > 0:
                    pieces.wait_recv(range(3 * h, 3 * h + 3))
                fetch = []
                for seg in range(3):
                    p, lo = _qkv_piece(h, seg)
                    fetch.append(pltpu.make_async_copy(w_ref.at[p, :, lo:lo + 128], w_tile.at[:, seg * 128:(seg + 1) * 128],
                                                       tile_sems.at[seg]))
                    fetch[-1].start()
                for cp in fetch:
                    cp.wait()

        def project(i, carry):
            rows = pl.ds(pl.multiple_of(i * PROJ_ROWS, PROJ_ROWS), PROJ_ROWS)
            qkv = _dot(u_ref[rows, :], w_tile[...])
            per = PROJ_ROWS // LAYOUT_MOD
            for seg, ref in enumerate((q_ref, k_ref, v_ref)):
                staged[seg] = qkv[:, seg * 128:(seg + 1) * 128]
                for g in range(LAYOUT_MOD):
                    dst = pl.ds(pl.multiple_of(g * (S // LAYOUT_MOD) + i * per, per), per)
                    ref[dst, :] = staged.at[seg][pl.ds(g, per, stride=LAYOUT_MOD), :]
            return carry

        lax.fori_loop(0, S // PROJ_ROWS, project, 0)

        head_sum = _head_sum_matrix()
        ones_b = jnp.ones((2 * QB, QB), BF)
        m_s[...] = jnp.full(m_s.shape, NEG, F32)
        l_s[...] = jnp.zeros(l_s.shape, F32)
        acc[...] = jnp.zeros(acc.shape, F32)

        for d in DILATIONS:
            tri_le, dist, low = _fold_masks(d)
            low_b = low.astype(F32).astype(BF)
            high_b = 1.0 - low_b
            slope = [sl_ref[2 * hp + a] * float(d) for a in range(2)]
            bias = [slope[a] * dist for a in range(2)]

            def block(b, d=d, slope=slope, bias=bias, tri_le=tri_le, low=low, low_b=low_b, high_b=high_b):
                n, cur, prev = _block_rows(b, d, S)
                has_prev = n > 0
                valid = jnp.logical_or(tri_le, has_prev)
                q2 = (cur.get(q_ref) * 0.125).astype(BF)
                qs = jnp.concatenate([q2 * low_b, q2 * high_b], axis=0)
                vp = prev.get(v_ref)
                kp_b = prev.get(k_ref).astype(BF)
                kcat = jnp.concatenate([kp_b, cur.get(k_ref).astype(BF)], axis=0)
                vcat = jnp.concatenate([vp, cur.get(v_ref)], axis=0).astype(BF)
                s2 = _dot_nt(qs, kcat)
                e2 = _dot(_hi_lo(q2.astype(F32) * kp_b.astype(F32)), head_sum)
                p_rows, alpha_h, pe_h = [], [], []
                for a in range(2):
                    sp, sc = s2[a * QB:(a + 1) * QB, :QB], s2[a * QB:(a + 1) * QB, QB:]
                    comb = jnp.where(valid, jnp.where(tri_le, sc, sp) - bias[a], NEG)
                    e = jnp.where(has_prev, e2[:, a * QB:(a + 1) * QB] - slope[a] * float(QB), NEG)
                    m_old = cur.get(m_s.at[a])
                    m_new = jnp.maximum(jnp.maximum(m_old, jnp.max(comb, axis=-1, keepdims=True)), e)
                    cur.put(m_s.at[a], m_new)
                    p = jnp.exp(comb - m_new)
                    pe_h.append(jnp.exp(e - m_new))
                    alpha_h.append(jnp.exp(m_old - m_new))
                    p_rows.append(jnp.concatenate([jnp.where(tri_le, 0.0, p).astype(BF),
                                                   jnp.where(tri_le, p, 0.0).astype(BF)], axis=1))
                pv = _dot(jnp.concatenate(p_rows, axis=0), jnp.concatenate([vcat, ones_b], axis=1))
                for a in range(2):
                    cur.put(l_s.at[a], alpha_h[a] * cur.get(l_s.at[a]) + pv[a * QB:(a + 1) * QB, QB:] + pe_h[a])
                cur.put(acc, jnp.where(low, alpha_h[0], alpha_h[1]) * cur.get(acc)
                        + jnp.where(low, pv[:QB, :QB], pv[QB:, :QB]) + jnp.where(low, pe_h[0], pe_h[1]) * vp)

            def several(it, carry, block=block):
                for u in range(ATT_UNROLL):
                    block(it * ATT_UNROLL + u)
                return carry

            lax.fori_loop(0, n_blocks // ATT_UNROLL, several, 0)

        low = _fold_masks(LAYOUT_MOD)[2]

        def finish(i, carry):
            rows = pl.ds(pl.multiple_of(i * QB, QB), QB)
            l0, l1 = l_s[0, rows, :], l_s[1, rows, :]
            o_ref[_natural_rows(i, S), :] = acc[rows, :] / jnp.where(low, l0, l1)
            lse_ref[0, rows, :] = m_s[0, rows, :] + jnp.log(l0)
            lse_ref[1, rows, :] = m_s[1, rows, :] + jnp.log(l1)
            return carry

        lax.fori_loop(0, n_blocks, finish, 0)

        @pl.when(hp == hpr - 1)
        def _():
            pieces.wait_send(later)
            for g in gathers:
                g.finish()

    col = pl.BlockSpec((S, 128), lambda h: (0, h))
    act = jax.ShapeDtypeStruct((S, D), F32)
    gathered = (w_all, w3_all, cw_all)
    return pl.pallas_call(
        body, name="attn_fwd", grid=(hpr,),
        in_specs=[SMEM_SPEC, VMEM_SPEC, ANY_SPEC, ANY_SPEC, ANY_SPEC],
        out_specs=(col, pl.BlockSpec((2, S, 128), lambda h: (0, 0, h)), col, col, col, ANY_SPEC, ANY_SPEC, ANY_SPEC),
        out_shape=(act, jax.ShapeDtypeStruct((2, S, D), F32), act, act, act,
                   *[jax.ShapeDtypeStruct(t.shape, t.dtype) for t in gathered]),
        scratch_shapes=([pltpu.VMEM((S, 128), F32), pltpu.VMEM((2, S, 128), F32), pltpu.VMEM((2, S, 128), F32),
                         pltpu.VMEM((D, 3 * 128), BF), pltpu.VMEM((3, PROJ_ROWS, 128), F32),
                         pltpu.SemaphoreType.DMA((3,))]
                        + _piece_sems(3 * hpr) + WEIGHT_GATHER_SEMS * 3),
        input_output_aliases={2: 5, 3: 6, 4: 7},
        compiler_params=_params(1),
    )(slopes, u, *gathered)


def _set_rows(shape, rows):
    idx = lax.broadcasted_iota(jnp.int32, shape, 0)
    out = jnp.zeros(shape, F32)
    for r, val in rows.items():
        out = out + jnp.where(idx == r, val, 0.0)
    return out


def _mid(yc_in, pa_mid, o, x2, target, b_merge, final_g, w3):
    S = x2.shape[0]
    tm = ROW_TILE
    nsteps = S // tm
    tile = pl.BlockSpec((tm, D), lambda i: (i, 0))

    def body(yc_ref, za_ref, gcp_ref, gap_ref, o_ref, x_ref, t_ref, b_ref, fg_ref, w_ref,
             dh_ref, dmid_ref, do_ref, dyc_ref, gw_ref, small_ref, acc, stage):
        i = pl.program_id(0)

        @pl.when(i == 0)
        def _():
            acc[...] = jnp.zeros_like(acc)
            small_ref[...] = jnp.zeros_like(small_ref)

        wc, wa, wo = w_ref[0], w_ref[1], w_ref[2]
        z = za_ref[...].astype(F32)
        sg = _sigmoid(z)
        ov = o_ref[...]
        yc_in_b, ya_in_b = yc_ref[...], (z * sg * ov).astype(BF)
        yc = _dot(yc_in_b, wc)
        ya = _dot(ya_in_b, wa)
        b = b_ref[...]
        gc = _sigmoid(gcp_ref[...].astype(F32) + b[:, :D])
        ga = _sigmoid(gap_ref[...].astype(F32) + b[:, D:])
        merged = gc * yc + ga * ya
        merged_b = merged.astype(BF)
        h = x_ref[...] + _dot(merged_b, wo)
        r2 = lax.rsqrt(jnp.mean(h * h, axis=-1, keepdims=True) + EPS)
        n = h * r2
        fg = fg_ref[...]
        err = n * fg - t_ref[...]
        loss = 0.5 * jnp.sum(jnp.sum(err * err, axis=-1, keepdims=True) / D, axis=0, keepdims=True)
        dy = err / D
        g_fg = jnp.sum(dy * n, axis=0, keepdims=True)
        dn = dy * fg
        dh = r2 * (dn - n * jnp.mean(dn * n, axis=-1, keepdims=True))
        dh_ref[...] = dh
        dh_b = dh.astype(BF)
        dmerged = _dot_nt(dh_b, wo)
        acc[2] += _dot(merged.T.astype(BF), dh_b)
        dyc = (dmerged * gc).astype(BF)
        dya = (dmerged * ga).astype(BF)
        dgcp = dmerged * yc * gc * (1.0 - gc)
        dgap = dmerged * ya * ga * (1.0 - ga)
        dmid_ref[1] = dgcp.astype(BF)
        dmid_ref[2] = dgap.astype(BF)
        acc[0] += _dot(yc_in_b.astype(F32).T.astype(BF), dyc)
        acc[1] += _dot(ya_in_b.astype(F32).T.astype(BF), dya)
        dyc_ref[...] = _dot_nt(dyc, wc).astype(BF)
        dya_in = _dot_nt(dya, wa)
        do_ref[...] = dya_in * (z * sg)
        dmid_ref[0] = (dya_in * ov * (sg * (1.0 + z * (1.0 - sg)))).astype(BF)
        small_ref[...] += _set_rows((8, D), {
            1: jnp.sum(dgcp, axis=0, keepdims=True), 2: jnp.sum(dgap, axis=0, keepdims=True),
            3: g_fg, 7: jnp.broadcast_to(loss, (1, D))})

        @pl.when(i == nsteps - 1)
        def _():
            for p in range(N_DEV):
                for a in range(3):
                    stage[...] = acc[a, p * ROW_SHARD:(p + 1) * ROW_SHARD, :].astype(BF)
                    pltpu.sync_copy(stage, gw_ref.at[p, a])

    return pl.pallas_call(
        body, name="mid", grid=(nsteps,),
        in_specs=[tile, pl.BlockSpec((tm, D), lambda i: (i, 0)), pl.BlockSpec((tm, D), lambda i: (i, 1)),
                  pl.BlockSpec((tm, D), lambda i: (i, 2)), tile, tile, tile,
                  pl.BlockSpec((1, 2 * D), lambda i: (0, 0)), pl.BlockSpec((1, D), lambda i: (0, 0)), VMEM_SPEC],
        out_specs=(tile, pl.BlockSpec((3, tm, D), lambda i: (0, i, 0)), tile, tile,
                   ANY_SPEC, pl.BlockSpec((8, D), lambda i: (0, 0))),
        out_shape=(jax.ShapeDtypeStruct((S, D), F32), jax.ShapeDtypeStruct((3, S, D), BF),
                   jax.ShapeDtypeStruct((S, D), F32), jax.ShapeDtypeStruct((S, D), BF),
                   jax.ShapeDtypeStruct((N_DEV, 3, ROW_SHARD, D), BF), jax.ShapeDtypeStruct((8, D), F32)),
        scratch_shapes=[pltpu.VMEM((3, D, D), F32), pltpu.VMEM((ROW_SHARD, D), BF)],
        compiler_params=_params(1),
    )(yc_in, pa_mid, pa_mid, pa_mid, o, x2, target, b_merge, final_g, w3)


def _conv_bwd(dyc_in, pa, cw8):
    S = pa.shape[0]
    tm, tc = CONV_TM, CONV_TC
    nct = D // tc
    nrt = S // tm
    last_halo = S // HALO - 1

    def seg(s):
        return pl.BlockSpec((tm, tc), lambda j, i, s=s: (i, s * nct + j))

    def halo_before(s):
        return pl.BlockSpec((HALO, tc), lambda j, i, s=s: (jnp.maximum(i * (tm // HALO) - 1, 0), s * nct + j))

    def halo_after(s):
        return pl.BlockSpec((HALO, tc), lambda j, i, s=s: (jnp.minimum((i + 1) * (tm // HALO), last_halo), s * nct + j))

    def body(dy, xc, bg, cg, zc, xch, cgh, dyn, bgn, zcn, cw, dout, gcw):
        i = pl.program_id(1)

        @pl.when(i == 0)
        def _():
            gcw[...] = jnp.zeros_like(gcw)

        xcv, cgv = xc[...].astype(F32), cg[...].astype(F32)
        a = cgv * xcv
        ah = jnp.where(i > 0, cgh[...].astype(F32) * xch[...].astype(F32), 0.0)
        row = lax.broadcasted_iota(jnp.int32, (tm, tc), 0)
        a1 = jnp.where(row == 0, ah[HALO - 1:HALO, :], pltpu.roll(a, 1, 0))
        a2 = jnp.where(row == 0, ah[HALO - 2:HALO - 1, :],
                       jnp.where(row == 1, ah[HALO - 1:HALO, :], pltpu.roll(a, 2, 0)))
        w = cw[...]
        conv = w[0:1, :] * a2 + w[1:2, :] * a1 + w[2:3, :] * a
        z = zc[...].astype(F32)
        sg = _sigmoid(z)
        silu = z * sg
        bgv = bg[...].astype(F32)
        dyv = dy[...].astype(F32)
        dout[3] = (dyv * bgv * conv * (sg * (1.0 + z * (1.0 - sg)))).astype(BF)
        dout[1] = (dyv * silu * conv).astype(BF)
        dc = dyv * silu * bgv
        zn = zcn[...].astype(F32)
        dcn = dyn[...].astype(F32) * (zn * _sigmoid(zn)) * bgn[...].astype(F32)
        dcn = jnp.where(i < nrt - 1, dcn, 0.0)
        dc1 = jnp.where(row == tm - 1, dcn[0:1, :], pltpu.roll(dc, tm - 1, 0))
        dc2 = jnp.where(row == tm - 1, dcn[1:2, :],
                        jnp.where(row == tm - 2, dcn[0:1, :], pltpu.roll(dc, tm - 2, 0)))
        da = w[2:3, :] * dc + w[1:2, :] * dc1 + w[0:1, :] * dc2
        dout[2] = (da * xcv).astype(BF)
        dout[0] = (da * cgv).astype(BF)
        gcw[...] += _set_rows((8, tc), {
            4: jnp.sum(dc * a2, axis=0, keepdims=True), 5: jnp.sum(dc * a1, axis=0, keepdims=True),
            6: jnp.sum(dc * a, axis=0, keepdims=True)})

    return pl.pallas_call(
        body, name="conv_bwd", grid=(nct, nrt),
        in_specs=[pl.BlockSpec((tm, tc), lambda j, i: (i, j)), seg(0), seg(1), seg(2), seg(3),
                  halo_before(0), halo_before(2),
                  pl.BlockSpec((HALO, tc), lambda j, i: (jnp.minimum((i + 1) * (tm // HALO), last_halo), j)),
                  halo_after(1), halo_after(3), pl.BlockSpec((8, tc), lambda j, i: (0, j))],
        out_specs=(pl.BlockSpec((4, tm, tc), lambda j, i: (0, i, j)), pl.BlockSpec((8, tc), lambda j, i: (0, j))),
        out_shape=(jax.ShapeDtypeStruct((4, S, D), BF), jax.ShapeDtypeStruct((8, D), F32)),
        compiler_params=_params(2),
    )(dyc_in, pa, pa, pa, pa, pa, pa, dyc_in, pa, pa, cw8)


def _attn_bwd(q, k, v, slopes, do, o, lse, g_in, g_3):
    S = q.shape[0]
    hpr = HEAD_PAIRS
    n_blocks = S // QB

    def body(sl_ref, q_ref, k_ref, v_ref, do_ref, o_ref, lse_ref, gin_ref, g3_ref, out_ref, rin_ref, r3_ref,
             dq_s, dk_s, dv_s, do_s, dd_s, *sems):
        hp = pl.program_id(0)
        exchanges = (_GradExchange(gin_ref, rin_ref, *sems[:3], _shard_cols((0, SEG0_ATTN * D), (SEG0_MID * D, IN_COLS))),
                     _GradExchange(g3_ref, r3_ref, *sems[3:], _whole))

        @pl.when(hp == 0)
        def _():
            for ex in exchanges:
                ex.start()

        head_sum = _head_sum_matrix()
        dq_s[...] = jnp.zeros(dq_s.shape, F32)
        dk_s[...] = jnp.zeros(dk_s.shape, F32)
        dv_s[...] = jnp.zeros(dv_s.shape, F32)

        def row_dots(i, carry):
            rows = pl.ds(pl.multiple_of(i * QB, QB), QB)
            natural = _natural_rows(i, S)
            do_c = do_ref[natural, :]
            do_s[rows, :] = do_c
            dd = _dot(_hi_lo(do_c * o_ref[natural, :]), head_sum)
            dd_s[0, rows, :] = dd[:, :QB]
            dd_s[1, rows, :] = dd[:, QB:]
            return carry

        lax.fori_loop(0, n_blocks, row_dots, 0)

        for d in DILATIONS:
            tri_le, dist, low = _fold_masks(d)
            low_b = low.astype(F32).astype(BF)
            high_b = 1.0 - low_b
            slope = [sl_ref[2 * hp + a] * float(d) for a in range(2)]
            bias = [slope[a] * dist for a in range(2)]

            def block(b, d=d, slope=slope, bias=bias, tri_le=tri_le, low=low, low_b=low_b, high_b=high_b):
                n, cur, prev = _block_rows(b, d, S)
                has_prev = n > 0
                valid = jnp.logical_or(tri_le, has_prev)
                q2f = cur.get(q_ref) * 0.125
                q2 = q2f.astype(BF)
                qs = jnp.concatenate([q2 * low_b, q2 * high_b], axis=0)
                kp, vp = prev.get(k_ref), prev.get(v_ref)
                kp_b, vp_b = kp.astype(BF), vp.astype(BF)
                kcat = jnp.concatenate([kp_b, cur.get(k_ref).astype(BF)], axis=0)
                vcat = jnp.concatenate([vp_b, cur.get(v_ref).astype(BF)], axis=0)
                do2f = cur.get(do_s)
                do2 = do2f.astype(BF)
                dos = jnp.concatenate([do2 * low_b, do2 * high_b], axis=0)
                s2 = _dot_nt(qs, kcat)
                dp2 = _dot_nt(dos, vcat)
                diag2 = _dot(jnp.concatenate([_hi_lo(q2.astype(F32) * kp_b.astype(F32)),
                                              _hi_lo(do2.astype(F32) * vp_b.astype(F32))], axis=0), head_sum)
                p_rows, ds_rows, pe_h, dse_h = [], [], [], []
                for a in range(2):
                    hs = slice(a * QB, (a + 1) * QB)
                    sp, sc = s2[hs, :QB], s2[hs, QB:]
                    dpp, dpc = dp2[hs, :QB], dp2[hs, QB:]
                    lse_a, dd_a = cur.get(lse_ref.at[a]), cur.get(dd_s.at[a])
                    comb = jnp.where(tri_le, sc, sp) - bias[a]
                    e = diag2[:QB, hs] - slope[a] * float(QB)
                    p = jnp.where(valid, jnp.exp(comb - lse_a), 0.0)
                    pe = jnp.where(has_prev, jnp.exp(e - lse_a), 0.0)
                    ds = p * (jnp.where(tri_le, dpc, dpp) - dd_a)
                    dse_h.append(pe * (diag2[QB:, hs] - dd_a))
                    pe_h.append(pe)
                    p_rows.append(jnp.concatenate([jnp.where(tri_le, 0.0, p).astype(BF),
                                                   jnp.where(tri_le, p, 0.0).astype(BF)], axis=1))
                    ds_rows.append(jnp.concatenate([jnp.where(tri_le, 0.0, ds).astype(BF),
                                                    jnp.where(tri_le, ds, 0.0).astype(BF)], axis=1))
                pst = jnp.concatenate(p_rows, axis=0)
                dst = jnp.concatenate(ds_rows, axis=0)
                pe2 = jnp.where(low, pe_h[0], pe_h[1])
                dse2 = jnp.where(low, dse_h[0], dse_h[1])
                dq = _dot(dst, kcat)
                cur.add(dq_s, (jnp.where(low, dq[:QB], dq[QB:]) + dse2 * kp) * 0.125)
                dk = _dot_tn(dst, qs)
                dv = _dot_tn(pst, dos)
                prev.add(dk_s, dk[:QB] + dse2 * q2f)
                cur.add(dk_s, dk[QB:])
                prev.add(dv_s, dv[:QB] + pe2 * do2f)
                cur.add(dv_s, dv[QB:])

            def several(it, carry, block=block):
                for u in range(ATT_UNROLL):
                    block(it * ATT_UNROLL + u)
                return carry

            lax.fori_loop(0, n_blocks // ATT_UNROLL, several, 0)

        def finish(i, carry):
            rows = pl.ds(pl.multiple_of(i * QB, QB), QB)
            natural = _natural_rows(i, S)
            for t, ref in enumerate((dq_s, dk_s, dv_s)):
                out_ref.at[t][natural, :] = ref[rows, :]
            return carry

        lax.fori_loop(0, n_blocks, finish, 0)

        @pl.when(hp == hpr - 1)
        def _():
            for ex in exchanges:
                ex.finish()

    col = pl.BlockSpec((S, 128), lambda h: (0, h))
    return pl.pallas_call(
        body, name="attn_bwd", grid=(hpr,),
        in_specs=[SMEM_SPEC, col, col, col, col, col, pl.BlockSpec((2, S, 128), lambda h: (0, 0, h)),
                  ANY_SPEC, ANY_SPEC],
        out_specs=(pl.BlockSpec((3, S, 128), lambda h: (0, 0, h)), ANY_SPEC, ANY_SPEC),
        out_shape=(jax.ShapeDtypeStruct((3, S, D), F32), jax.ShapeDtypeStruct(g_in.shape, BF),
                   jax.ShapeDtypeStruct(g_3.shape, BF)),
        scratch_shapes=([pltpu.VMEM((S, 128), F32)] * 4 + [pltpu.VMEM((2, S, 128), F32)]
                        + GRAD_EXCHANGE_SEMS + GRAD_EXCHANGE_SEMS),
        compiler_params=_params(1),
    )(slopes, q, k, v, do, o, lse, g_in, g_3)


WG_TN = 256
SEG0_CONV, SEG0_ATTN, SEG0_MID = 0, 4, 7


def _wgrad_in(ut, d_group, seg0, g_in, name):
    S = ut.shape[1]
    tn = WG_TN
    per_seg = D // tn
    per_shard = W_IN_SHARD // tn
    n_tiles = d_group.shape[0] * per_seg
    tile0 = seg0 * per_seg

    def body(ut_ref, d_ref, *rest):
        rest[-1][0] = _dot(ut_ref[...], d_ref[0].astype(BF)).astype(BF)

    operands, in_specs, aliases = [ut, d_group], [VMEM_SPEC, pl.BlockSpec((1, S, tn), lambda t: (t // per_seg, 0, t % per_seg))], {}
    if g_in is not None:
        operands.append(g_in)
        in_specs.append(ANY_SPEC)
        aliases = {2: 0}
    return pl.pallas_call(
        body, name=name, grid=(n_tiles,), in_specs=in_specs,
        out_specs=pl.BlockSpec((1, D, tn), lambda t: ((tile0 + t) // per_shard, 0, (tile0 + t) % per_shard)),
        out_shape=jax.ShapeDtypeStruct((N_DEV, D, W_IN_SHARD), BF),
        input_output_aliases=aliases,
        compiler_params=_params(1),
    )(*operands)


def _dgrad_norm_bwd(d_conv, d_attn, d_mid, w_all, x2, dh, norm_g):
    S = x2.shape[0]
    tm = ROW_TILE
    nsteps = S // tm
    tile = pl.BlockSpec((tm, D), lambda i: (i, 0))
    pieces = _proj_pieces()

    def body(a_ref, b_ref, c_ref, w_ref, x_ref, dh_ref, g_ref, gx_ref, small_ref):
        i = pl.program_id(0)

        @pl.when(i == 0)
        def _():
            small_ref[...] = jnp.zeros_like(small_ref)

        groups = (a_ref, b_ref, c_ref)
        du = jnp.zeros((tm, D), F32)
        for s, sc, p, pc, width in pieces:
            g = 0 if s < 4 else (1 if s < 7 else 2)
            local = s - (0, 4, 7)[g]
            du = du + _dot_nt(groups[g][local, :, sc:sc + width].astype(BF), w_ref[p, :, pc:pc + width])
        xv = x_ref[...]
        r = lax.rsqrt(jnp.mean(xv * xv, axis=-1, keepdims=True) + EPS)
        n = xv * r
        dn = du * g_ref[...]
        gx_ref[...] = dh_ref[...] + r * (dn - n * jnp.mean(dn * n, axis=-1, keepdims=True))
        small_ref[...] += _set_rows((8, D), {0: jnp.sum(du * n, axis=0, keepdims=True)})

    return pl.pallas_call(
        body, name="dgrad_norm_bwd", grid=(nsteps,),
        in_specs=[pl.BlockSpec((4, tm, D), lambda i: (0, i, 0)), pl.BlockSpec((3, tm, D), lambda i: (0, i, 0)),
                  pl.BlockSpec((3, tm, D), lambda i: (0, i, 0)), VMEM_SPEC, tile, tile,
                  pl.BlockSpec((1, D), lambda i: (0, 0))],
        out_specs=(tile, pl.BlockSpec((8, D), lambda i: (0, 0))),
        out_shape=(jax.ShapeDtypeStruct((S, D), F32), jax.ShapeDtypeStruct((8, D), F32)),
        compiler_params=_params(1),
    )(d_conv, d_attn, d_mid, w_all, x2, dh, norm_g)


HBM_SPEC = pl.BlockSpec(memory_space=pltpu.HBM)
SEM_SPEC = pl.BlockSpec(memory_space=pltpu.SEMAPHORE)
ATTN_COLS = _shard_cols((SEG0_ATTN * D, SEG0_MID * D))


def _attn_cols_exchange_start(g_in, r_in):
    def body(g_ref, r_ref, send_sems, recv_sems, g_thru, r_thru, token):
        _GradExchange(g_ref, r_ref, send_sems, recv_sems, None, ATTN_COLS).start()
        token[...] = jnp.zeros_like(token)

    hbm = pltpu.with_memory_space_constraint
    return pl.pallas_call(
        body, name="attn_cols_exchange_start",
        out_shape=(pltpu.SemaphoreType.DMA((N_DEV,)), pltpu.SemaphoreType.DMA((N_DEV,)),
                   pltpu.HBM(g_in.shape, g_in.dtype), pltpu.HBM(r_in.shape, r_in.dtype),
                   jax.ShapeDtypeStruct((8, 128), F32)),
        in_specs=(HBM_SPEC, HBM_SPEC), out_specs=(SEM_SPEC, SEM_SPEC, HBM_SPEC, HBM_SPEC, VMEM_SPEC),
        input_output_aliases={0: 2, 1: 3},
        compiler_params=pltpu.CompilerParams(has_side_effects=pltpu.SideEffectType.DATAFLOW_SIDE_EFFECTING),
    )(hbm(g_in, pltpu.HBM), hbm(r_in, pltpu.HBM))


def _attn_cols_exchange_wait(send_sems, recv_sems, g_thru, r_thru, after):
    def body(g_ref, r_ref, send_sems, recv_sems, after_ref, g_dead, r_out):
        _GradExchange(g_ref, r_ref, send_sems, recv_sems, None, ATTN_COLS).finish()

    return pl.pallas_call(
        body, name="attn_cols_exchange_wait",
        out_shape=(pltpu.HBM(g_thru.shape, g_thru.dtype), pltpu.HBM(r_thru.shape, r_thru.dtype)),
        in_specs=(HBM_SPEC, HBM_SPEC, SEM_SPEC, SEM_SPEC, ANY_SPEC), out_specs=(HBM_SPEC, HBM_SPEC),
        input_output_aliases={0: 0, 1: 1},
        compiler_params=pltpu.CompilerParams(has_side_effects=pltpu.SideEffectType.DATAFLOW_SIDE_EFFECTING),
    )(g_thru, r_thru, send_sems, recv_sems, after)


def _adamw_math(w, g, m, v):
    m = ADAM_B1 * m + (1.0 - ADAM_B1) * g
    v = ADAM_B2 * v + (1.0 - ADAM_B2) * (g * g)
    m_hat = m / (1.0 - ADAM_B1 ** ADAM_STEP)
    v_hat = v / (1.0 - ADAM_B2 ** ADAM_STEP)
    delta = -ADAM_LR * (m_hat / (jnp.sqrt(v_hat) + ADAM_EPS) + ADAM_WD * w)
    return delta, m, v


def _sum_adamw(parts, w, m, v, tm, name):
    R, C = w.shape
    tile = pl.BlockSpec((tm, C), lambda i: (i, 0))

    def body(p_ref, w_ref, m_ref, v_ref, g_out, d_out, m_out, v_out):
        g = p_ref[0].astype(F32)
        for s in range(1, N_DEV):
            g = g + p_ref[s].astype(F32)
        g_out[...] = g
        d_out[...], m_out[...], v_out[...] = _adamw_math(w_ref[...], g, m_ref[...], v_ref[...])

    shape = jax.ShapeDtypeStruct((R, C), F32)
    return pl.pallas_call(
        body, name=name, grid=(R // tm,),
        in_specs=[pl.BlockSpec((N_DEV, tm, C), lambda i: (0, i, 0)), tile, tile, tile],
        out_specs=(tile, tile, tile, tile), out_shape=(shape, shape, shape, shape),
        compiler_params=_params(1),
    )(parts, w, m, v)


def _adamw(g, w, m, v, name):
    def body(g_ref, w_ref, m_ref, v_ref, d_out, m_out, v_out):
        d_out[...], m_out[...], v_out[...] = _adamw_math(w_ref[...], g_ref[...], m_ref[...], v_ref[...])

    shape = jax.ShapeDtypeStruct(w.shape, F32)
    return pl.pallas_call(
        body, name=name, in_specs=[VMEM_SPEC] * 4, out_specs=(VMEM_SPEC,) * 3, out_shape=(shape, shape, shape),
    )(g, w, m, v)


def _alibi_slopes():
    return jnp.exp2(-8.0 * jnp.arange(1, N_HEADS + 1, dtype=F32) / N_HEADS)


def _local_step(x2, target, norm_g, b_merge, final_g, w_in, w3_shard, cw_shard):
    slopes = _alibi_slopes()
    u, ut, w_all, w3_all, cw_all = _norm_gather_first_weights(x2, norm_g, w_in, w3_shard, cw_shard)
    o, lse, q, k, v, w_all, w3_all, cw_all = _attn_fwd(u, slopes, w_all, w3_all, cw_all)
    w3 = jnp.transpose(w3_all, (1, 0, 2, 3)).reshape(3, D, D)
    cw8 = jnp.transpose(cw_all, (1, 0, 2)).reshape(8, D)
    pa = _proj_cols(u, w_all, SEG0_CONV, 4, BF, "proj_conv")
    yc_in = _conv_fwd(pa, cw8)
    pa_mid = _proj_cols(u, w_all, SEG0_MID, 3, BF, "proj_mid")
    dh, d_mid, do, dyc_in, g_3, small_mid = _mid(yc_in, pa_mid, o, x2, target, b_merge, final_g, w3)
    g_in = _wgrad_in(ut, d_mid, SEG0_MID, None, "wgrad_in_mid")
    d_conv, small_conv = _conv_bwd(dyc_in, pa, cw8)
    g_in = _wgrad_in(ut, d_conv, SEG0_CONV, g_in, "wgrad_in_conv")
    d_attn, r_in, r_3 = _attn_bwd(q, k, v, slopes, do, o, lse, g_in, g_3)
    g_in = _wgrad_in(ut, d_attn, SEG0_ATTN, g_in, "wgrad_in_attn")
    *in_flight, token = _attn_cols_exchange_start(g_in, r_in)
    grad_x, small_norm = _dgrad_norm_bwd(d_conv, d_attn, d_mid, w_all, x2, dh, norm_g + token[0:1, 0:1])
    return grad_x, in_flight, r_3, small_mid, small_conv, small_norm


def kernel(x, norm_g, w_in, b_merge, conv_w, w_out_conv, w_out_attn, w_o, final_g, loss_target, m_norm_g, m_w_in, m_b_merge, m_conv_w, m_w_out_conv, m_w_out_attn, m_w_o, m_final_g, v_norm_g, v_w_in, v_b_merge, v_conv_w, v_w_out_conv, v_w_out_attn, v_w_o, v_final_g):
    me = 4 * lax.axis_index("x") + 2 * lax.axis_index("y") + lax.axis_index("c")
    stack3 = lambda a, b, c: jnp.concatenate([a, b, c], axis=0)
    pad8 = lambda a: jnp.pad(a, ((0, 8 - a.shape[0]), (0, 0)))

    w3_shard = stack3(w_out_conv, w_out_attn, w_o)
    final_g2 = final_g.reshape(1, D)
    grad_x, in_flight, r_3, small_mid, small_conv, small_norm = _local_step(
        x[0], loss_target[0], norm_g, b_merge, final_g2, w_in[0], w3_shard, pad8(conv_w[0]))

    small = _allreduce_small(small_mid, small_conv, small_norm)
    g_in, r_in = _attn_cols_exchange_wait(*in_flight, small)
    own = lax.dynamic_index_in_dim(g_in, me, 0, keepdims=True)
    r_in = lax.dynamic_update_slice(r_in, own, (me, 0, 0))

    g_w_in, d_w_in, nm_w_in, nv_w_in = _sum_adamw(r_in, w_in[0], m_w_in[0], v_w_in[0], 128, "adamw_w_in")
    g_w3, d_w3, nm_w3, nv_w3 = _sum_adamw(
        r_3.reshape(N_DEV, 3 * ROW_SHARD, D), w3_shard.reshape(3 * ROW_SHARD, D),
        stack3(m_w_out_conv, m_w_out_attn, m_w_o).reshape(3 * ROW_SHARD, D),
        stack3(v_w_out_conv, v_w_out_attn, v_w_o).reshape(3 * ROW_SHARD, D), ROW_SHARD, "adamw_w3")

    def pack(ng, bm, fg):
        return pad8(jnp.concatenate([ng, bm.reshape(2, D), fg.reshape(1, D)], axis=0))

    d_s, nm_s, nv_s = _adamw(small, pack(norm_g, b_merge, final_g), pack(m_norm_g, m_b_merge, m_final_g),
                             pack(v_norm_g, v_b_merge, v_final_g), "adamw_small")
    g_cw = lax.dynamic_slice(small, (4, me * ROW_SHARD), (3, ROW_SHARD))
    d_cw, nm_cw, nv_cw = _adamw(g_cw, conv_w[0], m_conv_w[0], v_conv_w[0], "adamw_conv_w")

    loss = small[7, 0]
    split3 = lambda t: tuple(t[a * ROW_SHARD:(a + 1) * ROW_SHARD][None] for a in range(3))
    unpack = lambda t: (t[0:1], t[1:3].reshape(1, 2 * D), t[3])

    def leaves(in_, small_, cw_, w3_):
        ng, bm, fg = unpack(small_)
        wc, wa, wo = split3(w3_)
        return (ng, in_[None], bm, cw_[None], wc, wa, wo, fg)

    return (loss, grad_x[None],
            *leaves(g_w_in, small, g_cw, g_w3),
            *leaves(d_w_in, d_s, d_cw, d_w3),
            *leaves(nm_w_in, nm_s, nm_cw, nm_w3),
            *leaves(nv_w_in, nv_s, nv_cw, nv_w3))
```

```python
import functools

import jax
import jax.numpy as jnp
from jax import lax
from jax.experimental import pallas as pl
from jax.experimental.pallas import tpu as pltpu

D = 1024
N_HEADS = 16
HEAD_DIM = 64
N_SEG = 10
IN_COLS = N_SEG * D
N_DEV = 8
W_IN_SHARD = IN_COLS // N_DEV
ROW_SHARD = D // N_DEV
QB = 128
DILATIONS = (1, 4, 16)
EPS = 1e-6
NEG = -1e30
BF = jnp.bfloat16
F32 = jnp.float32
MESH = pl.DeviceIdType.MESH

ADAM_LR = 0.001
ADAM_B1 = 0.9
ADAM_B2 = 0.999
ADAM_EPS = 1e-08
ADAM_WD = 0.01
ADAM_STEP = 10

V7X_VMEM_BYTES = 64 * 1024 * 1024
VMEM_LIMIT = V7X_VMEM_BYTES - 8 * 1024 * 1024
ROW_TILE = 256

VMEM_SPEC = pl.BlockSpec(memory_space=pltpu.VMEM)
ANY_SPEC = pl.BlockSpec(memory_space=pl.ANY)
SMEM_SPEC = pl.BlockSpec(memory_space=pltpu.SMEM)


def _params(n_grid_axes, vmem=VMEM_LIMIT):
    return pltpu.CompilerParams(dimension_semantics=("arbitrary",) * n_grid_axes, vmem_limit_bytes=vmem)


def _dot(a, b):
    return jnp.dot(a, b, preferred_element_type=F32)


def _dot_nt(a, b):
    return lax.dot_general(a, b, (((1,), (1,)), ((), ())), preferred_element_type=F32)


def _dot_tn(a, b):
    return lax.dot_general(a, b, (((0,), (0,)), ((), ())), preferred_element_type=F32)


def _sigmoid(z):
    return 1.0 / (1.0 + jnp.exp(-z))


def _my_place():
    x, y, c = lax.axis_index("x"), lax.axis_index("y"), lax.axis_index("c")
    return x, y, c, 4 * x + 2 * y + c


def _peers(x, y, c):
    out = []
    for k in range(1, N_DEV):
        px = 1 - x if k & 4 else x
        py = 1 - y if k & 2 else y
        pc = 1 - c if k & 1 else c
        out.append(((px, py, pc), 4 * px + 2 * py + pc))
    return out


def _device(p):
    return (p >> 2, (p >> 1) & 1, p & 1)


def _shard_cols(*ranges):
    def cols(p):
        found = None
        for lo, hi in ranges:
            a, b = max(lo, p * W_IN_SHARD), min(hi, (p + 1) * W_IN_SHARD)
            if a < b:
                assert found is None
                found = (a - p * W_IN_SHARD, b - p * W_IN_SHARD)
        return found

    return cols


def _whole(p):
    return ()


def _block(ref, idx, cols):
    return ref.at[idx] if cols == () else ref.at[idx, :, cols[0]:cols[1]]


class _WeightGather:
    def __init__(self, src, dst, send_sems, forward_sems, recv_sems, cols):
        self.src, self.dst, self.cols = src, dst, cols
        self.send_sems, self.forward_sems, self.recv_sems = send_sems, forward_sems, recv_sems
        self.me = _my_place()[3]

    def _copy(self, p, target, passing_on=False):
        cols = self.cols(p)
        return pltpu.make_async_remote_copy(
            src_ref=_block(self.dst, p, cols) if passing_on else self.src(p, cols), dst_ref=_block(self.dst, p, cols),
            send_sem=self.forward_sems.at[p] if passing_on else self.send_sems.at[target],
            recv_sem=self.recv_sems.at[p], device_id=_device(target), device_id_type=MESH)

    def _as_each_device(self, own, relayed, other):
        for m in range(N_DEV):
            def branch(m=m):
                for p in range(N_DEV):
                    if self.cols(p) is None:
                        continue
                    if p == m:
                        for t in [m ^ 1] + [q for q in range(N_DEV) if q >> 1 != m >> 1 and q & 1 == m & 1]:
                            own(self._copy(m, t))
                    elif p >> 1 != m >> 1 and p & 1 == m & 1:
                        relayed(p, m ^ 1)
                    else:
                        other(p)

            pl.when(self.me == m)(branch)

    def start(self):
        self._as_each_device(lambda cp: cp.start(), lambda p, t: None, lambda p: None)

    def forward(self):
        def pass_on(p, t):
            self._copy(p, p).wait_recv()
            self._copy(p, t, passing_on=True).start()

        self._as_each_device(lambda cp: None, pass_on, lambda p: None)

    def finish(self):
        self._as_each_device(lambda cp: cp.wait_send(), lambda p, t: self._copy(p, t, passing_on=True).wait_send(),
                             lambda p: self._copy(p, p).wait_recv())


WEIGHT_GATHER_SEMS = [pltpu.SemaphoreType.DMA((N_DEV,))] * 3
FORWARD_STEP = 6
REST_COLS = _shard_cols((0, 4 * D), (7 * D, IN_COLS))
HEAD_PAIRS = D // 128


def _qkv_piece(h, seg):
    col = (4 + seg) * D + 128 * h
    return col // W_IN_SHARD, col % W_IN_SHARD


DIAGONAL_RELAY_FROM = 6


class _PieceGather:
    def __init__(self, src, dst, send_sems, recv_sems):
        self.src, self.dst, self.send_sems, self.recv_sems = src, dst, send_sems, recv_sems
        self.me = _my_place()[3]

    def _copy(self, i, target, passing_on=False):
        p, lo = _qkv_piece(i // 3, i % 3)
        block = self.dst.at[p, :, lo:lo + 128]
        return pltpu.make_async_remote_copy(
            src_ref=block if passing_on else self.src(p, lo, lo + 128), dst_ref=block,
            send_sem=self.send_sems.at[i, target], recv_sem=self.recv_sems.at[i],
            device_id=_device(target), device_id_type=MESH)

    def _as_each_device(self, pieces, own, relayed, other):
        for m in range(N_DEV):
            def branch(m=m):
                for i in pieces:
                    p = _qkv_piece(i // 3, i % 3)[0]
                    relay = i >= DIAGONAL_RELAY_FROM
                    if p == m:
                        for k in range(N_DEV - 1):
                            t = (p + 1 + (k + i) % (N_DEV - 1)) % N_DEV
                            if not (relay and t >> 1 == (p >> 1) ^ 3 and t & 1 != p & 1):
                                own(self._copy(i, t))
                    elif relay and m >> 1 == (p >> 1) ^ 3 and m & 1 == p & 1:
                        relayed(i, m ^ 1)
                    else:
                        other(i)

            pl.when(self.me == m)(branch)

    def start(self, pieces):
        self._as_each_device(pieces, lambda cp: cp.start(), lambda i, t: None, lambda i: None)

    def forward(self, pieces):
        def pass_on(i, t):
            self._copy(i, t).wait_recv()
            self._copy(i, t, passing_on=True).start()

        self._as_each_device(pieces, lambda cp: None, pass_on, lambda i: None)

    def wait_recv(self, pieces):
        self._as_each_device(pieces, lambda cp: None, lambda i, t: None, lambda i: self._copy(i, 0).wait_recv())

    def wait_send(self, pieces):
        self._as_each_device(pieces, lambda cp: cp.wait_send(),
                             lambda i, t: self._copy(i, t, passing_on=True).wait_send(), lambda i: None)


def _piece_sems(n):
    return [pltpu.SemaphoreType.DMA((n, N_DEV)), pltpu.SemaphoreType.DMA((n,))]


def _norm_gather_first_weights(x2, norm_g, w_in, w3, cw):
    S = x2.shape[0]
    tm = ROW_TILE
    nsteps = S // tm

    def body(x_ref, g_ref, w_in_ref, w3_ref, cw_ref, u_ref, ut_ref, o_in, o_3, o_cw, in_bf, w3_bf, local_sems, *sems):
        i = pl.program_id(0)
        me = _my_place()[3]
        gather = _PieceGather(lambda p, lo, hi: in_bf.at[:, lo:hi], o_in, *sems)
        local = [pltpu.make_async_copy(src, dst.at[me], local_sems.at[a])
                 for a, (src, dst) in enumerate(((in_bf, o_in), (w3_bf, o_3), (cw_ref, o_cw)))]

        @pl.when(i == 0)
        def _():
            def cast_rows(r, carry):
                rows = pl.ds(pl.multiple_of(r * 128, 128), 128)
                in_bf[rows, :] = w_in_ref[rows, :].astype(BF)
                return carry

            lax.fori_loop(0, D // 128, cast_rows, 0)
            for a in range(3):
                w3_bf[a] = w3_ref[a].astype(BF)
            gather.start(range(3))
            for cp in local:
                cp.start()

        xv = x_ref[...]
        r = lax.rsqrt(jnp.mean(xv * xv, axis=-1, keepdims=True) + EPS)
        u = xv * r * g_ref[...]
        u_ref[...] = u.astype(BF)
        ut_ref[...] = u.T.astype(BF)

        @pl.when(i == nsteps - 1)
        def _():
            gather.wait_recv(range(3))
            gather.wait_send(range(3))
            for cp in local:
                cp.wait()

    return pl.pallas_call(
        body, name="norm_gather_first_weights", grid=(nsteps,),
        out_shape=(jax.ShapeDtypeStruct((S, D), BF), jax.ShapeDtypeStruct((D, S), BF),
                   jax.ShapeDtypeStruct((N_DEV, D, W_IN_SHARD), BF),
                   jax.ShapeDtypeStruct((N_DEV, 3, ROW_SHARD, D), BF),
                   jax.ShapeDtypeStruct((N_DEV, 8, 128), F32)),
        in_specs=[pl.BlockSpec((tm, D), lambda i: (i, 0)), pl.BlockSpec((1, D), lambda i: (0, 0)),
                  VMEM_SPEC, VMEM_SPEC, VMEM_SPEC],
        out_specs=(pl.BlockSpec((tm, D), lambda i: (i, 0)), pl.BlockSpec((D, tm), lambda i: (0, i)),
                   ANY_SPEC, ANY_SPEC, ANY_SPEC),
        scratch_shapes=[pltpu.VMEM((D, W_IN_SHARD), BF), pltpu.VMEM((3, ROW_SHARD, D), BF),
                        pltpu.SemaphoreType.DMA((3,))] + _piece_sems(3),
        compiler_params=_params(1),
    )(x2, norm_g, w_in, w3, cw)


class _GradExchange:
    def __init__(self, src, dst, send_sems, recv_sems, local_sem, cols):
        self.src, self.dst, self.cols = src, dst, cols
        self.send_sems, self.recv_sems, self.local_sem = send_sems, recv_sems, local_sem
        self.me = _my_place()[3]

    def _remote(self, p, source):
        return pltpu.make_async_remote_copy(
            src_ref=_block(self.src, p, self.cols(p)), dst_ref=_block(self.dst, source, self.cols(p)),
            send_sem=self.send_sems.at[p], recv_sem=self.recv_sems.at[source],
            device_id=_device(p), device_id_type=MESH)

    def _local(self, p):
        return pltpu.make_async_copy(_block(self.src, p, self.cols(p)), _block(self.dst, p, self.cols(p)),
                                     self.local_sem)

    def _as_each_device(self, send, local, receive):
        for m in range(N_DEV):
            def branch(m=m):
                for k in range(1, N_DEV):
                    p = (m + k) % N_DEV
                    if self.cols(p) is not None:
                        send(self._remote(p, m))
                if self.cols(m) is not None:
                    if self.local_sem is not None:
                        local(self._local(m))
                    for k in range(1, N_DEV):
                        receive(self._remote(m, (m + k) % N_DEV))

            pl.when(self.me == m)(branch)

    def start(self):
        self._as_each_device(lambda cp: cp.start(), lambda cp: cp.start(), lambda cp: None)

    def finish(self):
        self._as_each_device(lambda cp: cp.wait_send(), lambda cp: cp.wait(), lambda cp: cp.wait_recv())


GRAD_EXCHANGE_SEMS = [pltpu.SemaphoreType.DMA((N_DEV,)), pltpu.SemaphoreType.DMA((N_DEV,)), pltpu.SemaphoreType.DMA]


def _allreduce_small(p_mid, p_conv, p_norm):
    def body(a_ref, b_ref, c_ref, out_ref, mine, gathered, send_sems, recv_sems):
        x, y, c, me = _my_place()
        mine[...] = a_ref[...] + b_ref[...] + c_ref[...]
        gathered[me] = mine[...]
        remote = []
        for k, (peer, _) in enumerate(_peers(x, y, c)):
            cp = pltpu.make_async_remote_copy(
                src_ref=mine, dst_ref=gathered.at[me], send_sem=send_sems.at[k], recv_sem=recv_sems.at[k],
                device_id=peer, device_id_type=MESH)
            cp.start()
            remote.append(cp)
        for cp in remote:
            cp.wait()
        total = gathered[0]
        for s in range(1, N_DEV):
            total = total + gathered[s]
        out_ref[...] = total

    return pl.pallas_call(
        body, name="allreduce_small",
        out_shape=jax.ShapeDtypeStruct((8, D), F32),
        in_specs=[VMEM_SPEC, VMEM_SPEC, VMEM_SPEC], out_specs=VMEM_SPEC,
        scratch_shapes=[pltpu.VMEM((8, D), F32), pltpu.VMEM((N_DEV, 8, D), F32),
                        pltpu.SemaphoreType.DMA((N_DEV - 1,)), pltpu.SemaphoreType.DMA((N_DEV - 1,))],
    )(p_mid, p_conv, p_norm)


def _proj_pieces():
    cuts = sorted(set(range(0, IN_COLS + 1, D)) | set(range(0, IN_COLS + 1, W_IN_SHARD)))
    return [(lo // D, lo % D, lo // W_IN_SHARD, lo % W_IN_SHARD, hi - lo) for lo, hi in zip(cuts[:-1], cuts[1:])]


PROJ_TN = 256


def _proj_cols(u, w_all, seg0, n_seg, dtype, name):
    S = u.shape[0]
    tn = PROJ_TN
    per_shard = W_IN_SHARD // tn
    tile0 = seg0 * D // tn

    def body(u_ref, w_ref, out_ref):
        out_ref[...] = _dot(u_ref[...], w_ref[0]).astype(dtype)

    return pl.pallas_call(
        body, name=name, grid=(n_seg * D // tn,),
        in_specs=[VMEM_SPEC, pl.BlockSpec((1, D, tn), lambda t: ((tile0 + t) // per_shard, 0, (tile0 + t) % per_shard))],
        out_specs=pl.BlockSpec((S, tn), lambda t: (0, t)),
        out_shape=jax.ShapeDtypeStruct((S, n_seg * D), dtype),
        compiler_params=_params(1),
    )(u, w_all)


CONV_TM, CONV_TC = 256, 512
HALO = 16


def _conv_fwd(pa, cw8):
    S = pa.shape[0]
    tm, tc = CONV_TM, CONV_TC
    nct = D // tc

    def seg(s):
        return pl.BlockSpec((tm, tc), lambda i, j, s=s: (i, s * nct + j))

    def halo_before(s):
        return pl.BlockSpec((HALO, tc), lambda i, j, s=s: (jnp.maximum(i * (tm // HALO) - 1, 0), s * nct + j))

    def body(xc, bg, cg, zc, xch, cgh, cw, out):
        i = pl.program_id(0)
        a = cg[...].astype(F32) * xc[...].astype(F32)
        ah = cgh[...].astype(F32) * xch[...].astype(F32)
        ah = jnp.where(i > 0, ah, 0.0)
        row = lax.broadcasted_iota(jnp.int32, (tm, tc), 0)
        a1 = jnp.where(row == 0, ah[HALO - 1:HALO, :], pltpu.roll(a, 1, 0))
        a2 = jnp.where(row == 0, ah[HALO - 2:HALO - 1, :],
                       jnp.where(row == 1, ah[HALO - 1:HALO, :], pltpu.roll(a, 2, 0)))
        w = cw[...]
        conv = w[0:1, :] * a2 + w[1:2, :] * a1 + w[2:3, :] * a
        z = zc[...].astype(F32)
        out[...] = (z * _sigmoid(z) * bg[...].astype(F32) * conv).astype(BF)

    return pl.pallas_call(
        body, name="conv_fwd", grid=(S // tm, nct),
        in_specs=[seg(0), seg(1), seg(2), seg(3), halo_before(0), halo_before(2),
                  pl.BlockSpec((8, tc), lambda i, j: (0, j))],
        out_specs=pl.BlockSpec((tm, tc), lambda i, j: (i, j)),
        out_shape=jax.ShapeDtypeStruct((S, D), BF),
        compiler_params=_params(2),
    )(pa, pa, pa, pa, pa, pa, cw8)


ATT_UNROLL = 32


LAYOUT_MOD = 4
RUN = QB // LAYOUT_MOD


def _fold_masks(d):
    row = lax.broadcasted_iota(jnp.int32, (QB, QB), 0)
    lane = lax.broadcasted_iota(jnp.int32, (QB, QB), 1)
    if d == 1:
        qpos, kpos = LAYOUT_MOD * (row % RUN) + row // RUN, LAYOUT_MOD * (lane % RUN) + lane // RUN
    else:
        qpos, kpos = row, lane
    tri_le = kpos <= qpos
    dist = jnp.where(tri_le, qpos - kpos, qpos - kpos + QB).astype(F32)
    return tri_le, dist, lane < HEAD_DIM


class _Rows:
    def __init__(self, slices):
        self.slices = slices

    def get(self, ref):
        parts = [ref[sl, :] for sl in self.slices]
        return parts[0] if len(parts) == 1 else jnp.concatenate(parts, axis=0)

    def put(self, ref, val):
        size = QB // len(self.slices)
        for g, sl in enumerate(self.slices):
            ref[sl, :] = val if len(self.slices) == 1 else val[g * size:(g + 1) * size]

    def add(self, ref, val):
        self.put(ref, self.get(ref) + val)


def _block_rows(b, d, S):
    quarter = S // LAYOUT_MOD
    nb = S // (QB * d)
    r, n = b // nb, b % nb
    n_prev = jnp.maximum(n - 1, 0)
    if d == 1:
        runs = lambda m: _Rows([pl.ds(pl.multiple_of(g * quarter + RUN * m, RUN), RUN) for g in range(LAYOUT_MOD)])
        return n, runs(n), runs(n_prev)
    if d == LAYOUT_MOD:
        block = lambda m: _Rows([pl.ds(pl.multiple_of(r * quarter + QB * m, QB), QB)])
        return n, block(n), block(n_prev)
    step = d // LAYOUT_MOD
    first = (r % LAYOUT_MOD) * quarter + r // LAYOUT_MOD
    strided = lambda m: _Rows([pl.ds(first + QB * step * m, QB, stride=step)])
    return n, strided(n), strided(n_prev)


def _natural_rows(i, S):
    per = S // LAYOUT_MOD // QB
    return pl.ds(i // per + LAYOUT_MOD * QB * (i % per), QB, stride=LAYOUT_MOD)


def _head_sum_matrix():
    r = lax.broadcasted_iota(jnp.int32, (2 * QB, 2 * QB), 0)
    c = lax.broadcasted_iota(jnp.int32, (2 * QB, 2 * QB), 1)
    return (((r % QB) // HEAD_DIM) == (c // QB)).astype(F32).astype(BF)


def _hi_lo(t):
    hi = t.astype(BF)
    return jnp.concatenate([hi, (t - hi.astype(F32)).astype(BF)], axis=1)


PROJ_ROWS = 512


def _attn_fwd(u, slopes, w_all, w3_all, cw_all):
    S = u.shape[0]
    hpr = HEAD_PAIRS
    n_blocks = S // QB
    later = range(3, 3 * hpr)

    def body(sl_ref, u_ref, w_in_ref, w3_in_ref, cw_in_ref, o_ref, lse_ref, q_ref, k_ref, v_ref, w_ref, w3_ref,
             cw_ref, acc, m_s, l_s, w_tile, staged, tile_sems, *sems):
        hp = pl.program_id(0)
        me = _my_place()[3]
        pieces = _PieceGather(lambda p, lo, hi: w_ref.at[p, :, lo:hi], w_ref, *sems[0:2])
        gathers = (_WeightGather(lambda p, cols: _block(w_ref, me, cols), w_ref, *sems[2:5], REST_COLS),
                   _WeightGather(lambda p, cols: w3_ref.at[me], w3_ref, *sems[5:8], _whole),
                   _WeightGather(lambda p, cols: cw_ref.at[me], cw_ref, *sems[8:11], _whole))

        @pl.when(hp == 0)
        def _():
            pieces.start(later)
            for g in gathers:
                g.start()

        @pl.when(hp == FORWARD_STEP)
        def _():
            for g in gathers:
                g.forward()

        for h in range(hpr):
            @pl.when(hp == h)
            def _(h=h):
                if h > 0:
                    pieces.wait_recv(range(3 * h, 3 * h + 3))
                if 3 * h + 3 >= DIAGONAL_RELAY_FROM and h + 1 < hpr:
                    pieces.forward(range(3 * h + 3, 3 * h + 6))
                fetch = []
                for seg in range(3):
                    p, lo = _qkv_piece(h, seg)
                    fetch.append(pltpu.make_async_copy(w_ref.at[p, :, lo:lo + 128], w_tile.at[:, seg * 128:(seg + 1) * 128],
                                                       tile_sems.at[seg]))
                    fetch[-1].start()
                for cp in fetch:
                    cp.wait()

        def project(i, carry):
            rows = pl.ds(pl.multiple_of(i * PROJ_ROWS, PROJ_ROWS), PROJ_ROWS)
            qkv = _dot(u_ref[rows, :], w_tile[...])
            per = PROJ_ROWS // LAYOUT_MOD
            for seg, ref in enumerate((q_ref, k_ref, v_ref)):
                staged[seg] = qkv[:, seg * 128:(seg + 1) * 128]
                for g in range(LAYOUT_MOD):
                    dst = pl.ds(pl.multiple_of(g * (S // LAYOUT_MOD) + i * per, per), per)
                    ref[dst, :] = staged.at[seg][pl.ds(g, per, stride=LAYOUT_MOD), :]
            return carry

        lax.fori_loop(0, S // PROJ_ROWS, project, 0)

        head_sum = _head_sum_matrix()
        ones_b = jnp.ones((2 * QB, QB), BF)
        m_s[...] = jnp.full(m_s.shape, NEG, F32)
        l_s[...] = jnp.zeros(l_s.shape, F32)
        acc[...] = jnp.zeros(acc.shape, F32)

        for d in DILATIONS:
            tri_le, dist, low = _fold_masks(d)
            low_b = low.astype(F32).astype(BF)
            high_b = 1.0 - low_b
            slope = [sl_ref[2 * hp + a] * float(d) for a in range(2)]
            bias = [slope[a] * dist for a in range(2)]

            def block(b, d=d, slope=slope, bias=bias, tri_le=tri_le, low=low, low_b=low_b, high_b=high_b):
                n, cur, prev = _block_rows(b, d, S)
                has_prev = n > 0
                valid = jnp.logical_or(tri_le, has_prev)
                q2 = (cur.get(q_ref) * 0.125).astype(BF)
                qs = jnp.concatenate([q2 * low_b, q2 * high_b], axis=0)
                vp = prev.get(v_ref)
                kp_b = prev.get(k_ref).astype(BF)
                kcat = jnp.concatenate([kp_b, cur.get(k_ref).astype(BF)], axis=0)
                vcat = jnp.concatenate([vp, cur.get(v_ref)], axis=0).astype(BF)
                s2 = _dot_nt(qs, kcat)
                e2 = _dot(_hi_lo(q2.astype(F32) * kp_b.astype(F32)), head_sum)
                p_rows, alpha_h, pe_h = [], [], []
                for a in range(2):
                    sp, sc = s2[a * QB:(a + 1) * QB, :QB], s2[a * QB:(a + 1) * QB, QB:]
                    comb = jnp.where(valid, jnp.where(tri_le, sc, sp) - bias[a], NEG)
                    e = jnp.where(has_prev, e2[:, a * QB:(a + 1) * QB] - slope[a] * float(QB), NEG)
                    m_old = cur.get(m_s.at[a])
                    m_new = jnp.maximum(jnp.maximum(m_old, jnp.max(comb, axis=-1, keepdims=True)), e)
                    cur.put(m_s.at[a], m_new)
                    p = jnp.exp(comb - m_new)
                    pe_h.append(jnp.exp(e - m_new))
                    alpha_h.append(jnp.exp(m_old - m_new))
                    p_rows.append(jnp.concatenate([jnp.where(tri_le, 0.0, p).astype(BF),
                                                   jnp.where(tri_le, p, 0.0).astype(BF)], axis=1))
                pv = _dot(jnp.concatenate(p_rows, axis=0), jnp.concatenate([vcat, ones_b], axis=1))
                for a in range(2):
                    cur.put(l_s.at[a], alpha_h[a] * cur.get(l_s.at[a]) + pv[a * QB:(a + 1) * QB, QB:] + pe_h[a])
                cur.put(acc, jnp.where(low, alpha_h[0], alpha_h[1]) * cur.get(acc)
                        + jnp.where(low, pv[:QB, :QB], pv[QB:, :QB]) + jnp.where(low, pe_h[0], pe_h[1]) * vp)

            def several(it, carry, block=block):
                for u in range(ATT_UNROLL):
                    block(it * ATT_UNROLL + u)
                return carry

            lax.fori_loop(0, n_blocks // ATT_UNROLL, several, 0)

        low = _fold_masks(LAYOUT_MOD)[2]

        def finish(i, carry):
            rows = pl.ds(pl.multiple_of(i * QB, QB), QB)
            l0, l1 = l_s[0, rows, :], l_s[1, rows, :]
            o_ref[_natural_rows(i, S), :] = acc[rows, :] / jnp.where(low, l0, l1)
            lse_ref[0, rows, :] = m_s[0, rows, :] + jnp.log(l0)
            lse_ref[1, rows, :] = m_s[1, rows, :] + jnp.log(l1)
            return carry

        lax.fori_loop(0, n_blocks, finish, 0)

        @pl.when(hp == hpr - 1)
        def _():
            pieces.wait_send(later)
            for g in gathers:
                g.finish()

    col = pl.BlockSpec((S, 128), lambda h: (0, h))
    act = jax.ShapeDtypeStruct((S, D), F32)
    gathered = (w_all, w3_all, cw_all)
    return pl.pallas_call(
        body, name="attn_fwd", grid=(hpr,),
        in_specs=[SMEM_SPEC, VMEM_SPEC, ANY_SPEC, ANY_SPEC, ANY_SPEC],
        out_specs=(col, pl.BlockSpec((2, S, 128), lambda h: (0, 0, h)), col, col, col, ANY_SPEC, ANY_SPEC, ANY_SPEC),
        out_shape=(act, jax.ShapeDtypeStruct((2, S, D), F32), act, act, act,
                   *[jax.ShapeDtypeStruct(t.shape, t.dtype) for t in gathered]),
        scratch_shapes=([pltpu.VMEM((S, 128), F32), pltpu.VMEM((2, S, 128), F32), pltpu.VMEM((2, S, 128), F32),
                         pltpu.VMEM((D, 3 * 128), BF), pltpu.VMEM((3, PROJ_ROWS, 128), F32),
                         pltpu.SemaphoreType.DMA((3,))]
                        + _piece_sems(3 * hpr) + WEIGHT_GATHER_SEMS * 3),
        input_output_aliases={2: 5, 3: 6, 4: 7},
        compiler_params=_params(1),
    )(slopes, u, *gathered)


def _set_rows(shape, rows):
    idx = lax.broadcasted_iota(jnp.int32, shape, 0)
    out = jnp.zeros(shape, F32)
    for r, val in rows.items():
        out = out + jnp.where(idx == r, val, 0.0)
    return out


def _mid(yc_in, pa_mid, o, x2, target, b_merge, final_g, w3):
    S = x2.shape[0]
    tm = ROW_TILE
    nsteps = S // tm
    tile = pl.BlockSpec((tm, D), lambda i: (i, 0))

    def body(yc_ref, za_ref, gcp_ref, gap_ref, o_ref, x_ref, t_ref, b_ref, fg_ref, w_ref,
             dh_ref, dmid_ref, do_ref, dyc_ref, gw_ref, small_ref, acc, stage):
        i = pl.program_id(0)

        @pl.when(i == 0)
        def _():
            acc[...] = jnp.zeros_like(acc)
            small_ref[...] = jnp.zeros_like(small_ref)

        wc, wa, wo = w_ref[0], w_ref[1], w_ref[2]
        z = za_ref[...].astype(F32)
        sg = _sigmoid(z)
        ov = o_ref[...]
        yc_in_b, ya_in_b = yc_ref[...], (z * sg * ov).astype(BF)
        yc = _dot(yc_in_b, wc)
        ya = _dot(ya_in_b, wa)
        b = b_ref[...]
        gc = _sigmoid(gcp_ref[...].astype(F32) + b[:, :D])
        ga = _sigmoid(gap_ref[...].astype(F32) + b[:, D:])
        merged = gc * yc + ga * ya
        merged_b = merged.astype(BF)
        h = x_ref[...] + _dot(merged_b, wo)
        r2 = lax.rsqrt(jnp.mean(h * h, axis=-1, keepdims=True) + EPS)
        n = h * r2
        fg = fg_ref[...]
        err = n * fg - t_ref[...]
        loss = 0.5 * jnp.sum(jnp.sum(err * err, axis=-1, keepdims=True) / D, axis=0, keepdims=True)
        dy = err / D
        g_fg = jnp.sum(dy * n, axis=0, keepdims=True)
        dn = dy * fg
        dh = r2 * (dn - n * jnp.mean(dn * n, axis=-1, keepdims=True))
        dh_ref[...] = dh
        dh_b = dh.astype(BF)
        dmerged = _dot_nt(dh_b, wo)
        acc[2] += _dot(merged.T.astype(BF), dh_b)
        dyc = (dmerged * gc).astype(BF)
        dya = (dmerged * ga).astype(BF)
        dgcp = dmerged * yc * gc * (1.0 - gc)
        dgap = dmerged * ya * ga * (1.0 - ga)
        dmid_ref[1] = dgcp.astype(BF)
        dmid_ref[2] = dgap.astype(BF)
        acc[0] += _dot(yc_in_b.astype(F32).T.astype(BF), dyc)
        acc[1] += _dot(ya_in_b.astype(F32).T.astype(BF), dya)
        dyc_ref[...] = _dot_nt(dyc, wc).astype(BF)
        dya_in = _dot_nt(dya, wa)
        do_ref[...] = dya_in * (z * sg)
        dmid_ref[0] = (dya_in * ov * (sg * (1.0 + z * (1.0 - sg)))).astype(BF)
        small_ref[...] += _set_rows((8, D), {
            1: jnp.sum(dgcp, axis=0, keepdims=True), 2: jnp.sum(dgap, axis=0, keepdims=True),
            3: g_fg, 7: jnp.broadcast_to(loss, (1, D))})

        @pl.when(i == nsteps - 1)
        def _():
            for p in range(N_DEV):
                for a in range(3):
                    stage[...] = acc[a, p * ROW_SHARD:(p + 1) * ROW_SHARD, :].astype(BF)
                    pltpu.sync_copy(stage, gw_ref.at[p, a])

    return pl.pallas_call(
        body, name="mid", grid=(nsteps,),
        in_specs=[tile, pl.BlockSpec((tm, D), lambda i: (i, 0)), pl.BlockSpec((tm, D), lambda i: (i, 1)),
                  pl.BlockSpec((tm, D), lambda i: (i, 2)), tile, tile, tile,
                  pl.BlockSpec((1, 2 * D), lambda i: (0, 0)), pl.BlockSpec((1, D), lambda i: (0, 0)), VMEM_SPEC],
        out_specs=(tile, pl.BlockSpec((3, tm, D), lambda i: (0, i, 0)), tile, tile,
                   ANY_SPEC, pl.BlockSpec((8, D), lambda i: (0, 0))),
        out_shape=(jax.ShapeDtypeStruct((S, D), F32), jax.ShapeDtypeStruct((3, S, D), BF),
                   jax.ShapeDtypeStruct((S, D), F32), jax.ShapeDtypeStruct((S, D), BF),
                   jax.ShapeDtypeStruct((N_DEV, 3, ROW_SHARD, D), BF), jax.ShapeDtypeStruct((8, D), F32)),
        scratch_shapes=[pltpu.VMEM((3, D, D), F32), pltpu.VMEM((ROW_SHARD, D), BF)],
        compiler_params=_params(1),
    )(yc_in, pa_mid, pa_mid, pa_mid, o, x2, target, b_merge, final_g, w3)


def _conv_bwd(dyc_in, pa, cw8):
    S = pa.shape[0]
    tm, tc = CONV_TM, CONV_TC
    nct = D // tc
    nrt = S // tm
    last_halo = S // HALO - 1

    def seg(s):
        return pl.BlockSpec((tm, tc), lambda j, i, s=s: (i, s * nct + j))

    def halo_before(s):
        return pl.BlockSpec((HALO, tc), lambda j, i, s=s: (jnp.maximum(i * (tm // HALO) - 1, 0), s * nct + j))

    def halo_after(s):
        return pl.BlockSpec((HALO, tc), lambda j, i, s=s: (jnp.minimum((i + 1) * (tm // HALO), last_halo), s * nct + j))

    def body(dy, xc, bg, cg, zc, xch, cgh, dyn, bgn, zcn, cw, dout, gcw):
        i = pl.program_id(1)

        @pl.when(i == 0)
        def _():
            gcw[...] = jnp.zeros_like(gcw)

        xcv, cgv = xc[...].astype(F32), cg[...].astype(F32)
        a = cgv * xcv
        ah = jnp.where(i > 0, cgh[...].astype(F32) * xch[...].astype(F32), 0.0)
        row = lax.broadcasted_iota(jnp.int32, (tm, tc), 0)
        a1 = jnp.where(row == 0, ah[HALO - 1:HALO, :], pltpu.roll(a, 1, 0))
        a2 = jnp.where(row == 0, ah[HALO - 2:HALO - 1, :],
                       jnp.where(row == 1, ah[HALO - 1:HALO, :], pltpu.roll(a, 2, 0)))
        w = cw[...]
        conv = w[0:1, :] * a2 + w[1:2, :] * a1 + w[2:3, :] * a
        z = zc[...].astype(F32)
        sg = _sigmoid(z)
        silu = z * sg
        bgv = bg[...].astype(F32)
        dyv = dy[...].astype(F32)
        dout[3] = (dyv * bgv * conv * (sg * (1.0 + z * (1.0 - sg)))).astype(BF)
        dout[1] = (dyv * silu * conv).astype(BF)
        dc = dyv * silu * bgv
        zn = zcn[...].astype(F32)
        dcn = dyn[...].astype(F32) * (zn * _sigmoid(zn)) * bgn[...].astype(F32)
        dcn = jnp.where(i < nrt - 1, dcn, 0.0)
        dc1 = jnp.where(row == tm - 1, dcn[0:1, :], pltpu.roll(dc, tm - 1, 0))
        dc2 = jnp.where(row == tm - 1, dcn[1:2, :],
                        jnp.where(row == tm - 2, dcn[0:1, :], pltpu.roll(dc, tm - 2, 0)))
        da = w[2:3, :] * dc + w[1:2, :] * dc1 + w[0:1, :] * dc2
        dout[2] = (da * xcv).astype(BF)
        dout[0] = (da * cgv).astype(BF)
        gcw[...] += _set_rows((8, tc), {
            4: jnp.sum(dc * a2, axis=0, keepdims=True), 5: jnp.sum(dc * a1, axis=0, keepdims=True),
            6: jnp.sum(dc * a, axis=0, keepdims=True)})

    return pl.pallas_call(
        body, name="conv_bwd", grid=(nct, nrt),
        in_specs=[pl.BlockSpec((tm, tc), lambda j, i: (i, j)), seg(0), seg(1), seg(2), seg(3),
                  halo_before(0), halo_before(2),
                  pl.BlockSpec((HALO, tc), lambda j, i: (jnp.minimum((i + 1) * (tm // HALO), last_halo), j)),
                  halo_after(1), halo_after(3), pl.BlockSpec((8, tc), lambda j, i: (0, j))],
        out_specs=(pl.BlockSpec((4, tm, tc), lambda j, i: (0, i, j)), pl.BlockSpec((8, tc), lambda j, i: (0, j))),
        out_shape=(jax.ShapeDtypeStruct((4, S, D), BF), jax.ShapeDtypeStruct((8, D), F32)),
        compiler_params=_params(2),
    )(dyc_in, pa, pa, pa, pa, pa, pa, dyc_in, pa, pa, cw8)


def _attn_bwd(q, k, v, slopes, do, o, lse, g_in, g_3):
    S = q.shape[0]
    hpr = HEAD_PAIRS
    n_blocks = S // QB

    def body(sl_ref, q_ref, k_ref, v_ref, do_ref, o_ref, lse_ref, gin_ref, g3_ref, out_ref, rin_ref, r3_ref,
             dq_s, dk_s, dv_s, do_s, dd_s, *sems):
        hp = pl.program_id(0)
        exchanges = (_GradExchange(gin_ref, rin_ref, *sems[:3], _shard_cols((0, SEG0_ATTN * D), (SEG0_MID * D, IN_COLS))),
                     _GradExchange(g3_ref, r3_ref, *sems[3:], _whole))

        @pl.when(hp == 0)
        def _():
            for ex in exchanges:
                ex.start()

        head_sum = _head_sum_matrix()
        dq_s[...] = jnp.zeros(dq_s.shape, F32)
        dk_s[...] = jnp.zeros(dk_s.shape, F32)
        dv_s[...] = jnp.zeros(dv_s.shape, F32)

        def row_dots(i, carry):
            rows = pl.ds(pl.multiple_of(i * QB, QB), QB)
            natural = _natural_rows(i, S)
            do_c = do_ref[natural, :]
            do_s[rows, :] = do_c
            dd = _dot(_hi_lo(do_c * o_ref[natural, :]), head_sum)
            dd_s[0, rows, :] = dd[:, :QB]
            dd_s[1, rows, :] = dd[:, QB:]
            return carry

        lax.fori_loop(0, n_blocks, row_dots, 0)

        for d in DILATIONS:
            tri_le, dist, low = _fold_masks(d)
            low_b = low.astype(F32).astype(BF)
            high_b = 1.0 - low_b
            slope = [sl_ref[2 * hp + a] * float(d) for a in range(2)]
            bias = [slope[a] * dist for a in range(2)]

            def block(b, d=d, slope=slope, bias=bias, tri_le=tri_le, low=low, low_b=low_b, high_b=high_b):
                n, cur, prev = _block_rows(b, d, S)
                has_prev = n > 0
                valid = jnp.logical_or(tri_le, has_prev)
                q2f = cur.get(q_ref) * 0.125
                q2 = q2f.astype(BF)
                qs = jnp.concatenate([q2 * low_b, q2 * high_b], axis=0)
                kp, vp = prev.get(k_ref), prev.get(v_ref)
                kp_b, vp_b = kp.astype(BF), vp.astype(BF)
                kcat = jnp.concatenate([kp_b, cur.get(k_ref).astype(BF)], axis=0)
                vcat = jnp.concatenate([vp_b, cur.get(v_ref).astype(BF)], axis=0)
                do2f = cur.get(do_s)
                do2 = do2f.astype(BF)
                dos = jnp.concatenate([do2 * low_b, do2 * high_b], axis=0)
                s2 = _dot_nt(qs, kcat)
                dp2 = _dot_nt(dos, vcat)
                diag2 = _dot(jnp.concatenate([_hi_lo(q2.astype(F32) * kp_b.astype(F32)),
                                              _hi_lo(do2.astype(F32) * vp_b.astype(F32))], axis=0), head_sum)
                p_rows, ds_rows, pe_h, dse_h = [], [], [], []
                for a in range(2):
                    hs = slice(a * QB, (a + 1) * QB)
                    sp, sc = s2[hs, :QB], s2[hs, QB:]
                    dpp, dpc = dp2[hs, :QB], dp2[hs, QB:]
                    lse_a, dd_a = cur.get(lse_ref.at[a]), cur.get(dd_s.at[a])
                    comb = jnp.where(tri_le, sc, sp) - bias[a]
                    e = diag2[:QB, hs] - slope[a] * float(QB)
                    p = jnp.where(valid, jnp.exp(comb - lse_a), 0.0)
                    pe = jnp.where(has_prev, jnp.exp(e - lse_a), 0.0)
                    ds = p * (jnp.where(tri_le, dpc, dpp) - dd_a)
                    dse_h.append(pe * (diag2[QB:, hs] - dd_a))
                    pe_h.append(pe)
                    p_rows.append(jnp.concatenate([jnp.where(tri_le, 0.0, p).astype(BF),
                                                   jnp.where(tri_le, p, 0.0).astype(BF)], axis=1))
                    ds_rows.append(jnp.concatenate([jnp.where(tri_le, 0.0, ds).astype(BF),
                                                    jnp.where(tri_le, ds, 0.0).astype(BF)], axis=1))
                pst = jnp.concatenate(p_rows, axis=0)
                dst = jnp.concatenate(ds_rows, axis=0)
                pe2 = jnp.where(low, pe_h[0], pe_h[1])
                dse2 = jnp.where(low, dse_h[0], dse_h[1])
                dq = _dot(dst, kcat)
                cur.add(dq_s, (jnp.where(low, dq[:QB], dq[QB:]) + dse2 * kp) * 0.125)
                dk = _dot_tn(dst, qs)
                dv = _dot_tn(pst, dos)
                prev.add(dk_s, dk[:QB] + dse2 * q2f)
                cur.add(dk_s, dk[QB:])
                prev.add(dv_s, dv[:QB] + pe2 * do2f)
                cur.add(dv_s, dv[QB:])

            def several(it, carry, block=block):
                for u in range(ATT_UNROLL):
                    block(it * ATT_UNROLL + u)
                return carry

            lax.fori_loop(0, n_blocks // ATT_UNROLL, several, 0)

        def finish(i, carry):
            rows = pl.ds(pl.multiple_of(i * QB, QB), QB)
            natural = _natural_rows(i, S)
            for t, ref in enumerate((dq_s, dk_s, dv_s)):
                out_ref.at[t][natural, :] = ref[rows, :]
            return carry

        lax.fori_loop(0, n_blocks, finish, 0)

        @pl.when(hp == hpr - 1)
        def _():
            for ex in exchanges:
                ex.finish()

    col = pl.BlockSpec((S, 128), lambda h: (0, h))
    return pl.pallas_call(
        body, name="attn_bwd", grid=(hpr,),
        in_specs=[SMEM_SPEC, col, col, col, col, col, pl.BlockSpec((2, S, 128), lambda h: (0, 0, h)),
                  ANY_SPEC, ANY_SPEC],
        out_specs=(pl.BlockSpec((3, S, 128), lambda h: (0, 0, h)), ANY_SPEC, ANY_SPEC),
        out_shape=(jax.ShapeDtypeStruct((3, S, D), F32), jax.ShapeDtypeStruct(g_in.shape, BF),
                   jax.ShapeDtypeStruct(g_3.shape, BF)),
        scratch_shapes=([pltpu.VMEM((S, 128), F32)] * 4 + [pltpu.VMEM((2, S, 128), F32)]
                        + GRAD_EXCHANGE_SEMS + GRAD_EXCHANGE_SEMS),
        compiler_params=_params(1),
    )(slopes, q, k, v, do, o, lse, g_in, g_3)


WG_TN = 256
SEG0_CONV, SEG0_ATTN, SEG0_MID = 0, 4, 7


def _wgrad_in(ut, d_group, seg0, g_in, name):
    S = ut.shape[1]
    tn = WG_TN
    per_seg = D // tn
    per_shard = W_IN_SHARD // tn
    n_tiles = d_group.shape[0] * per_seg
    tile0 = seg0 * per_seg

    def body(ut_ref, d_ref, *rest):
        rest[-1][0] = _dot(ut_ref[...], d_ref[0].astype(BF)).astype(BF)

    operands, in_specs, aliases = [ut, d_group], [VMEM_SPEC, pl.BlockSpec((1, S, tn), lambda t: (t // per_seg, 0, t % per_seg))], {}
    if g_in is not None:
        operands.append(g_in)
        in_specs.append(ANY_SPEC)
        aliases = {2: 0}
    return pl.pallas_call(
        body, name=name, grid=(n_tiles,), in_specs=in_specs,
        out_specs=pl.BlockSpec((1, D, tn), lambda t: ((tile0 + t) // per_shard, 0, (tile0 + t) % per_shard)),
        out_shape=jax.ShapeDtypeStruct((N_DEV, D, W_IN_SHARD), BF),
        input_output_aliases=aliases,
        compiler_params=_params(1),
    )(*operands)


def _dgrad_norm_bwd(d_conv, d_attn, d_mid, w_all, x2, dh, norm_g):
    S = x2.shape[0]
    tm = ROW_TILE
    nsteps = S // tm
    tile = pl.BlockSpec((tm, D), lambda i: (i, 0))
    pieces = _proj_pieces()

    def body(a_ref, b_ref, c_ref, w_ref, x_ref, dh_ref, g_ref, gx_ref, small_ref):
        i = pl.program_id(0)

        @pl.when(i == 0)
        def _():
            small_ref[...] = jnp.zeros_like(small_ref)

        groups = (a_ref, b_ref, c_ref)
        du = jnp.zeros((tm, D), F32)
        for s, sc, p, pc, width in pieces:
            g = 0 if s < 4 else (1 if s < 7 else 2)
            local = s - (0, 4, 7)[g]
            du = du + _dot_nt(groups[g][local, :, sc:sc + width].astype(BF), w_ref[p, :, pc:pc + width])
        xv = x_ref[...]
        r = lax.rsqrt(jnp.mean(xv * xv, axis=-1, keepdims=True) + EPS)
        n = xv * r
        dn = du * g_ref[...]
        gx_ref[...] = dh_ref[...] + r * (dn - n * jnp.mean(dn * n, axis=-1, keepdims=True))
        small_ref[...] += _set_rows((8, D), {0: jnp.sum(du * n, axis=0, keepdims=True)})

    return pl.pallas_call(
        body, name="dgrad_norm_bwd", grid=(nsteps,),
        in_specs=[pl.BlockSpec((4, tm, D), lambda i: (0, i, 0)), pl.BlockSpec((3, tm, D), lambda i: (0, i, 0)),
                  pl.BlockSpec((3, tm, D), lambda i: (0, i, 0)), VMEM_SPEC, tile, tile,
                  pl.BlockSpec((1, D), lambda i: (0, 0))],
        out_specs=(tile, pl.BlockSpec((8, D), lambda i: (0, 0))),
        out_shape=(jax.ShapeDtypeStruct((S, D), F32), jax.ShapeDtypeStruct((8, D), F32)),
        compiler_params=_params(1),
    )(d_conv, d_attn, d_mid, w_all, x2, dh, norm_g)


HBM_SPEC = pl.BlockSpec(memory_space=pltpu.HBM)
SEM_SPEC = pl.BlockSpec(memory_space=pltpu.SEMAPHORE)
ATTN_COLS = _shard_cols((SEG0_ATTN * D, SEG0_MID * D))


def _attn_cols_exchange_start(g_in, r_in):
    def body(g_ref, r_ref, send_sems, recv_sems, g_thru, r_thru, token):
        _GradExchange(g_ref, r_ref, send_sems, recv_sems, None, ATTN_COLS).start()
        token[...] = jnp.zeros_like(token)

    hbm = pltpu.with_memory_space_constraint
    return pl.pallas_call(
        body, name="attn_cols_exchange_start",
        out_shape=(pltpu.SemaphoreType.DMA((N_DEV,)), pltpu.SemaphoreType.DMA((N_DEV,)),
                   pltpu.HBM(g_in.shape, g_in.dtype), pltpu.HBM(r_in.shape, r_in.dtype),
                   jax.ShapeDtypeStruct((8, 128), F32)),
        in_specs=(HBM_SPEC, HBM_SPEC), out_specs=(SEM_SPEC, SEM_SPEC, HBM_SPEC, HBM_SPEC, VMEM_SPEC),
        input_output_aliases={0: 2, 1: 3},
        compiler_params=pltpu.CompilerParams(has_side_effects=pltpu.SideEffectType.DATAFLOW_SIDE_EFFECTING),
    )(hbm(g_in, pltpu.HBM), hbm(r_in, pltpu.HBM))


def _attn_cols_exchange_wait(send_sems, recv_sems, g_thru, r_thru, after):
    def body(g_ref, r_ref, send_sems, recv_sems, after_ref, g_dead, r_out):
        _GradExchange(g_ref, r_ref, send_sems, recv_sems, None, ATTN_COLS).finish()

    return pl.pallas_call(
        body, name="attn_cols_exchange_wait",
        out_shape=(pltpu.HBM(g_thru.shape, g_thru.dtype), pltpu.HBM(r_thru.shape, r_thru.dtype)),
        in_specs=(HBM_SPEC, HBM_SPEC, SEM_SPEC, SEM_SPEC, ANY_SPEC), out_specs=(HBM_SPEC, HBM_SPEC),
        input_output_aliases={0: 0, 1: 1},
        compiler_params=pltpu.CompilerParams(has_side_effects=pltpu.SideEffectType.DATAFLOW_SIDE_EFFECTING),
    )(g_thru, r_thru, send_sems, recv_sems, after)


def _adamw_math(w, g, m, v):
    m = ADAM_B1 * m + (1.0 - ADAM_B1) * g
    v = ADAM_B2 * v + (1.0 - ADAM_B2) * (g * g)
    m_hat = m / (1.0 - ADAM_B1 ** ADAM_STEP)
    v_hat = v / (1.0 - ADAM_B2 ** ADAM_STEP)
    delta = -ADAM_LR * (m_hat / (jnp.sqrt(v_hat) + ADAM_EPS) + ADAM_WD * w)
    return delta, m, v


def _sum_adamw(parts, w, m, v, tm, name):
    R, C = w.shape
    tile = pl.BlockSpec((tm, C), lambda i: (i, 0))

    def body(p_ref, w_ref, m_ref, v_ref, g_out, d_out, m_out, v_out):
        g = p_ref[0].astype(F32)
        for s in range(1, N_DEV):
            g = g + p_ref[s].astype(F32)
        g_out[...] = g
        d_out[...], m_out[...], v_out[...] = _adamw_math(w_ref[...], g, m_ref[...], v_ref[...])

    shape = jax.ShapeDtypeStruct((R, C), F32)
    return pl.pallas_call(
        body, name=name, grid=(R // tm,),
        in_specs=[pl.BlockSpec((N_DEV, tm, C), lambda i: (0, i, 0)), tile, tile, tile],
        out_specs=(tile, tile, tile, tile), out_shape=(shape, shape, shape, shape),
        compiler_params=_params(1),
    )(parts, w, m, v)


def _adamw(g, w, m, v, name):
    def body(g_ref, w_ref, m_ref, v_ref, d_out, m_out, v_out):
        d_out[...], m_out[...], v_out[...] = _adamw_math(w_ref[...], g_ref[...], m_ref[...], v_ref[...])

    shape = jax.ShapeDtypeStruct(w.shape, F32)
    return pl.pallas_call(
        body, name=name, in_specs=[VMEM_SPEC] * 4, out_specs=(VMEM_SPEC,) * 3, out_shape=(shape, shape, shape),
    )(g, w, m, v)


def _alibi_slopes():
    return jnp.exp2(-8.0 * jnp.arange(1, N_HEADS + 1, dtype=F32) / N_HEADS)


def _local_step(x2, target, norm_g, b_merge, final_g, w_in, w3_shard, cw_shard):
    slopes = _alibi_slopes()
    u, ut, w_all, w3_all, cw_all = _norm_gather_first_weights(x2, norm_g, w_in, w3_shard, cw_shard)
    o, lse, q, k, v, w_all, w3_all, cw_all = _attn_fwd(u, slopes, w_all, w3_all, cw_all)
    w3 = jnp.transpose(w3_all, (1, 0, 2, 3)).reshape(3, D, D)
    cw8 = jnp.transpose(cw_all, (1, 0, 2)).reshape(8, D)
    pa = _proj_cols(u, w_all, SEG0_CONV, 4, BF, "proj_conv")
    yc_in = _conv_fwd(pa, cw8)
    pa_mid = _proj_cols(u, w_all, SEG0_MID, 3, BF, "proj_mid")
    dh, d_mid, do, dyc_in, g_3, small_mid = _mid(yc_in, pa_mid, o, x2, target, b_merge, final_g, w3)
    g_in = _wgrad_in(ut, d_mid, SEG0_MID, None, "wgrad_in_mid")
    d_conv, small_conv = _conv_bwd(dyc_in, pa, cw8)
    g_in = _wgrad_in(ut, d_conv, SEG0_CONV, g_in, "wgrad_in_conv")
    d_attn, r_in, r_3 = _attn_bwd(q, k, v, slopes, do, o, lse, g_in, g_3)
    g_in = _wgrad_in(ut, d_attn, SEG0_ATTN, g_in, "wgrad_in_attn")
    *in_flight, token = _attn_cols_exchange_start(g_in, r_in)
    grad_x, small_norm = _dgrad_norm_bwd(d_conv, d_attn, d_mid, w_all, x2, dh, norm_g + token[0:1, 0:1])
    return grad_x, in_flight, r_3, small_mid, small_conv, small_norm


def kernel(x, norm_g, w_in, b_merge, conv_w, w_out_conv, w_out_attn, w_o, final_g, loss_target, m_norm_g, m_w_in, m_b_merge, m_conv_w, m_w_out_conv, m_w_out_attn, m_w_o, m_final_g, v_norm_g, v_w_in, v_b_merge, v_conv_w, v_w_out_conv, v_w_out_attn, v_w_o, v_final_g):
    me = 4 * lax.axis_index("x") + 2 * lax.axis_index("y") + lax.axis_index("c")
    stack3 = lambda a, b, c: jnp.concatenate([a, b, c], axis=0)
    pad8 = lambda a: jnp.pad(a, ((0, 8 - a.shape[0]), (0, 0)))

    w3_shard = stack3(w_out_conv, w_out_attn, w_o)
    final_g2 = final_g.reshape(1, D)
    grad_x, in_flight, r_3, small_mid, small_conv, small_norm = _local_step(
        x[0], loss_target[0], norm_g, b_merge, final_g2, w_in[0], w3_shard, pad8(conv_w[0]))

    small = _allreduce_small(small_mid, small_conv, small_norm)
    g_in, r_in = _attn_cols_exchange_wait(*in_flight, small)
    own = lax.dynamic_index_in_dim(g_in, me, 0, keepdims=True)
    r_in = lax.dynamic_update_slice(r_in, own, (me, 0, 0))

    g_w_in, d_w_in, nm_w_in, nv_w_in = _sum_adamw(r_in, w_in[0], m_w_in[0], v_w_in[0], 128, "adamw_w_in")
    g_w3, d_w3, nm_w3, nv_w3 = _sum_adamw(
        r_3.reshape(N_DEV, 3 * ROW_SHARD, D), w3_shard.reshape(3 * ROW_SHARD, D),
        stack3(m_w_out_conv, m_w_out_attn, m_w_o).reshape(3 * ROW_SHARD, D),
        stack3(v_w_out_conv, v_w_out_attn, v_w_o).reshape(3 * ROW_SHARD, D), ROW_SHARD, "adamw_w3")

    def pack(ng, bm, fg):
        return pad8(jnp.concatenate([ng, bm.reshape(2, D), fg.reshape(1, D)], axis=0))

    d_s, nm_s, nv_s = _adamw(small, pack(norm_g, b_merge, final_g), pack(m_norm_g, m_b_merge, m_final_g),
                             pack(v_norm_g, v_b_merge, v_final_g), "adamw_small")
    g_cw = lax.dynamic_slice(small, (4, me * ROW_SHARD), (3, ROW_SHARD))
    d_cw, nm_cw, nv_cw = _adamw(g_cw, conv_w[0], m_conv_w[0], v_conv_w[0], "adamw_conv_w")

    loss = small[7, 0]
    split3 = lambda t: tuple(t[a * ROW_SHARD:(a + 1) * ROW_SHARD][None] for a in range(3))
    unpack = lambda t: (t[0:1], t[1:3].reshape(1, 2 * D), t[3])

    def leaves(in_, small_, cw_, w3_):
        ng, bm, fg = unpack(small_)
        wc, wa, wo = split3(w3_)
        return (ng, in_[None], bm, cw_[None], wc, wa, wo, fg)

    return (loss, grad_x[None],
            *leaves(g_w_in, small, g_cw, g_w3),
            *leaves(d_w_in, d_s, d_cw, d_w3),
            *leaves(nm_w_in, nm_s, nm_cw, nm_w3),
            *leaves(nv_w_in, nv_s, nv_cw, nv_w3))
```

```python
import functools

import jax
import jax.numpy as jnp
from jax import lax
from jax.experimental import pallas as pl
from jax.experimental.pallas import tpu as pltpu

D = 1024
N_HEADS = 16
HEAD_DIM = 64
N_SEG = 10
IN_COLS = N_SEG * D
N_DEV = 8
W_IN_SHARD = IN_COLS // N_DEV
ROW_SHARD = D // N_DEV
QB = 128
DILATIONS = (1, 4, 16)
EPS = 1e-6
NEG = -1e30
BF = jnp.bfloat16
F32 = jnp.float32
MESH = pl.DeviceIdType.MESH

ADAM_LR = 0.001
ADAM_B1 = 0.9
ADAM_B2 = 0.999
ADAM_EPS = 1e-08
ADAM_WD = 0.01
ADAM_STEP = 10

V7X_VMEM_BYTES = 64 * 1024 * 1024
VMEM_LIMIT = V7X_VMEM_BYTES - 8 * 1024 * 1024
ROW_TILE = 256

VMEM_SPEC = pl.BlockSpec(memory_space=pltpu.VMEM)
ANY_SPEC = pl.BlockSpec(memory_space=pl.ANY)
SMEM_SPEC = pl.BlockSpec(memory_space=pltpu.SMEM)


def _params(n_grid_axes, vmem=VMEM_LIMIT):
    return pltpu.CompilerParams(dimension_semantics=("arbitrary",) * n_grid_axes, vmem_limit_bytes=vmem)


def _dot(a, b):
    return jnp.dot(a, b, preferred_element_type=F32)


def _dot_nt(a, b):
    return lax.dot_general(a, b, (((1,), (1,)), ((), ())), preferred_element_type=F32)


def _dot_tn(a, b):
    return lax.dot_general(a, b, (((0,), (0,)), ((), ())), preferred_element_type=F32)


def _sigmoid(z):
    return 1.0 / (1.0 + jnp.exp(-z))


def _my_place():
    x, y, c = lax.axis_index("x"), lax.axis_index("y"), lax.axis_index("c")
    return x, y, c, 4 * x + 2 * y + c


def _peers(x, y, c):
    out = []
    for k in range(1, N_DEV):
        px = 1 - x if k & 4 else x
        py = 1 - y if k & 2 else y
        pc = 1 - c if k & 1 else c
        out.append(((px, py, pc), 4 * px + 2 * py + pc))
    return out


def _device(p):
    return (p >> 2, (p >> 1) & 1, p & 1)


def _shard_cols(*ranges):
    def cols(p):
        found = None
        for lo, hi in ranges:
            a, b = max(lo, p * W_IN_SHARD), min(hi, (p + 1) * W_IN_SHARD)
            if a < b:
                assert found is None
                found = (a - p * W_IN_SHARD, b - p * W_IN_SHARD)
        return found

    return cols


def _whole(p):
    return ()


def _block(ref, idx, cols):
    return ref.at[idx] if cols == () else ref.at[idx, :, cols[0]:cols[1]]


class _WeightGather:
    def __init__(self, src, dst, send_sems, forward_sems, recv_sems, cols):
        self.src, self.dst, self.cols = src, dst, cols
        self.send_sems, self.forward_sems, self.recv_sems = send_sems, forward_sems, recv_sems
        self.me = _my_place()[3]

    def _copy(self, p, target, passing_on=False):
        cols = self.cols(p)
        return pltpu.make_async_remote_copy(
            src_ref=_block(self.dst, p, cols) if passing_on else self.src(p, cols), dst_ref=_block(self.dst, p, cols),
            send_sem=self.forward_sems.at[p] if passing_on else self.send_sems.at[target],
            recv_sem=self.recv_sems.at[p], device_id=_device(target), device_id_type=MESH)

    def _as_each_device(self, own, relayed, other):
        for m in range(N_DEV):
            def branch(m=m):
                for p in range(N_DEV):
                    if self.cols(p) is None:
                        continue
                    if p == m:
                        for t in [m ^ 1] + [q for q in range(N_DEV) if q >> 1 != m >> 1 and q & 1 == m & 1]:
                            own(self._copy(m, t))
                    elif p >> 1 != m >> 1 and p & 1 == m & 1:
                        relayed(p, m ^ 1)
                    else:
                        other(p)

            pl.when(self.me == m)(branch)

    def start(self):
        self._as_each_device(lambda cp: cp.start(), lambda p, t: None, lambda p: None)

    def forward(self):
        def pass_on(p, t):
            self._copy(p, p).wait_recv()
            self._copy(p, t, passing_on=True).start()

        self._as_each_device(lambda cp: None, pass_on, lambda p: None)

    def finish(self):
        self._as_each_device(lambda cp: cp.wait_send(), lambda p, t: self._copy(p, t, passing_on=True).wait_send(),
                             lambda p: self._copy(p, p).wait_recv())


WEIGHT_GATHER_SEMS = [pltpu.SemaphoreType.DMA((N_DEV,))] * 3
FORWARD_STEP = 6
REST_COLS = _shard_cols((0, 4 * D), (7 * D, IN_COLS))
HEAD_PAIRS = D // 128


def _qkv_piece(h, seg):
    col = (4 + seg) * D + 128 * h
    return col // W_IN_SHARD, col % W_IN_SHARD


class _PieceGather:
    def __init__(self, src, dst, send_sems, recv_sems):
        self.src, self.dst, self.send_sems, self.recv_sems = src, dst, send_sems, recv_sems
        self.me = _my_place()[3]

    def _copy(self, i, target):
        p, lo = _qkv_piece(i // 3, i % 3)
        return pltpu.make_async_remote_copy(
            src_ref=self.src(p, lo, lo + 128), dst_ref=self.dst.at[p, :, lo:lo + 128], send_sem=self.send_sems.at[i, target],
            recv_sem=self.recv_sems.at[i], device_id=_device(target), device_id_type=MESH)

    def _owner(self, i, act):
        p = _qkv_piece(i // 3, i % 3)[0]

        def sender():
            for k in range(N_DEV - 1):
                act(self._copy(i, (p + 1 + (k + i) % (N_DEV - 1)) % N_DEV))

        pl.when(self.me == p)(sender)

    def start(self, pieces):
        for i in pieces:
            self._owner(i, lambda cp: cp.start())

    def wait_send(self, pieces):
        for i in pieces:
            self._owner(i, lambda cp: cp.wait_send())

    def wait_recv(self, pieces):
        for i in pieces:
            p = _qkv_piece(i // 3, i % 3)[0]
            pl.when(self.me != p)(lambda i=i, p=p: self._copy(i, p).wait_recv())


def _piece_sems(n):
    return [pltpu.SemaphoreType.DMA((n, N_DEV)), pltpu.SemaphoreType.DMA((n,))]


def _norm_gather_first_weights(x2, norm_g, w_in, w3, cw):
    S = x2.shape[0]
    tm = ROW_TILE
    nsteps = S // tm

    def body(x_ref, g_ref, w_in_ref, w3_ref, cw_ref, u_ref, ut_ref, o_in, o_3, o_cw, in_bf, w3_bf, local_sems, *sems):
        i = pl.program_id(0)
        me = _my_place()[3]
        gather = _PieceGather(lambda p, lo, hi: in_bf.at[:, lo:hi], o_in, *sems)
        local = [pltpu.make_async_copy(src, dst.at[me], local_sems.at[a])
                 for a, (src, dst) in enumerate(((in_bf, o_in), (w3_bf, o_3), (cw_ref, o_cw)))]

        @pl.when(i == 0)
        def _():
            def cast_rows(r, carry):
                rows = pl.ds(pl.multiple_of(r * 128, 128), 128)
                in_bf[rows, :] = w_in_ref[rows, :].astype(BF)
                return carry

            lax.fori_loop(0, D // 128, cast_rows, 0)
            for a in range(3):
                w3_bf[a] = w3_ref[a].astype(BF)
            gather.start(range(3))
            for cp in local:
                cp.start()

        xv = x_ref[...]
        r = lax.rsqrt(jnp.mean(xv * xv, axis=-1, keepdims=True) + EPS)
        u = xv * r * g_ref[...]
        u_ref[...] = u.astype(BF)
        ut_ref[...] = u.T.astype(BF)

        @pl.when(i == nsteps - 1)
        def _():
            gather.wait_recv(range(3))
            gather.wait_send(range(3))
            for cp in local:
                cp.wait()

    return pl.pallas_call(
        body, name="norm_gather_first_weights", grid=(nsteps,),
        out_shape=(jax.ShapeDtypeStruct((S, D), BF), jax.ShapeDtypeStruct((D, S), BF),
                   jax.ShapeDtypeStruct((N_DEV, D, W_IN_SHARD), BF),
                   jax.ShapeDtypeStruct((N_DEV, 3, ROW_SHARD, D), BF),
                   jax.ShapeDtypeStruct((N_DEV, 8, 128), F32)),
        in_specs=[pl.BlockSpec((tm, D), lambda i: (i, 0)), pl.BlockSpec((1, D), lambda i: (0, 0)),
                  VMEM_SPEC, VMEM_SPEC, VMEM_SPEC],
        out_specs=(pl.BlockSpec((tm, D), lambda i: (i, 0)), pl.BlockSpec((D, tm), lambda i: (0, i)),
                   ANY_SPEC, ANY_SPEC, ANY_SPEC),
        scratch_shapes=[pltpu.VMEM((D, W_IN_SHARD), BF), pltpu.VMEM((3, ROW_SHARD, D), BF),
                        pltpu.SemaphoreType.DMA((3,))] + _piece_sems(3),
        compiler_params=_params(1),
    )(x2, norm_g, w_in, w3, cw)


class _GradExchange:
    def __init__(self, src, dst, send_sems, recv_sems, local_sem, cols):
        self.src, self.dst, self.cols = src, dst, cols
        self.send_sems, self.recv_sems, self.local_sem = send_sems, recv_sems, local_sem
        self.me = _my_place()[3]

    def _remote(self, p, source):
        return pltpu.make_async_remote_copy(
            src_ref=_block(self.src, p, self.cols(p)), dst_ref=_block(self.dst, source, self.cols(p)),
            send_sem=self.send_sems.at[p], recv_sem=self.recv_sems.at[source],
            device_id=_device(p), device_id_type=MESH)

    def _local(self, p):
        return pltpu.make_async_copy(_block(self.src, p, self.cols(p)), _block(self.dst, p, self.cols(p)),
                                     self.local_sem)

    def _as_each_device(self, send, local, receive):
        for m in range(N_DEV):
            def branch(m=m):
                for k in range(1, N_DEV):
                    p = (m + k) % N_DEV
                    if self.cols(p) is not None:
                        send(self._remote(p, m))
                if self.cols(m) is not None:
                    if self.local_sem is not None:
                        local(self._local(m))
                    for k in range(1, N_DEV):
                        receive(self._remote(m, (m + k) % N_DEV))

            pl.when(self.me == m)(branch)

    def start(self):
        self._as_each_device(lambda cp: cp.start(), lambda cp: cp.start(), lambda cp: None)

    def finish(self):
        self._as_each_device(lambda cp: cp.wait_send(), lambda cp: cp.wait(), lambda cp: cp.wait_recv())


GRAD_EXCHANGE_SEMS = [pltpu.SemaphoreType.DMA((N_DEV,)), pltpu.SemaphoreType.DMA((N_DEV,)), pltpu.SemaphoreType.DMA]


def _allreduce_small(p_mid, p_conv, p_norm):
    def body(a_ref, b_ref, c_ref, out_ref, mine, gathered, send_sems, recv_sems):
        x, y, c, me = _my_place()
        mine[...] = a_ref[...] + b_ref[...] + c_ref[...]
        gathered[me] = mine[...]
        remote = []
        for k, (peer, _) in enumerate(_peers(x, y, c)):
            cp = pltpu.make_async_remote_copy(
                src_ref=mine, dst_ref=gathered.at[me], send_sem=send_sems.at[k], recv_sem=recv_sems.at[k],
                device_id=peer, device_id_type=MESH)
            cp.start()
            remote.append(cp)
        for cp in remote:
            cp.wait()
        total = gathered[0]
        for s in range(1, N_DEV):
            total = total + gathered[s]
        out_ref[...] = total

    return pl.pallas_call(
        body, name="allreduce_small",
        out_shape=jax.ShapeDtypeStruct((8, D), F32),
        in_specs=[VMEM_SPEC, VMEM_SPEC, VMEM_SPEC], out_specs=VMEM_SPEC,
        scratch_shapes=[pltpu.VMEM((8, D), F32), pltpu.VMEM((N_DEV, 8, D), F32),
                        pltpu.SemaphoreType.DMA((N_DEV - 1,)), pltpu.SemaphoreType.DMA((N_DEV - 1,))],
    )(p_mid, p_conv, p_norm)


def _proj_pieces():
    cuts = sorted(set(range(0, IN_COLS + 1, D)) | set(range(0, IN_COLS + 1, W_IN_SHARD)))
    return [(lo // D, lo % D, lo // W_IN_SHARD, lo % W_IN_SHARD, hi - lo) for lo, hi in zip(cuts[:-1], cuts[1:])]


PROJ_TN = 256


def _proj_cols(u, w_all, seg0, n_seg, dtype, name):
    S = u.shape[0]
    tn = PROJ_TN
    per_shard = W_IN_SHARD // tn
    tile0 = seg0 * D // tn

    def body(u_ref, w_ref, out_ref):
        out_ref[...] = _dot(u_ref[...], w_ref[0]).astype(dtype)

    return pl.pallas_call(
        body, name=name, grid=(n_seg * D // tn,),
        in_specs=[VMEM_SPEC, pl.BlockSpec((1, D, tn), lambda t: ((tile0 + t) // per_shard, 0, (tile0 + t) % per_shard))],
        out_specs=pl.BlockSpec((S, tn), lambda t: (0, t)),
        out_shape=jax.ShapeDtypeStruct((S, n_seg * D), dtype),
        compiler_params=_params(1),
    )(u, w_all)


CONV_TM, CONV_TC = 512, 512
HALO = 16


def _conv_fwd(pa, cw8):
    S = pa.shape[0]
    tm, tc = CONV_TM, CONV_TC
    nct = D // tc

    def seg(s):
        return pl.BlockSpec((tm, tc), lambda i, j, s=s: (i, s * nct + j))

    def halo_before(s):
        return pl.BlockSpec((HALO, tc), lambda i, j, s=s: (jnp.maximum(i * (tm // HALO) - 1, 0), s * nct + j))

    def body(xc, bg, cg, zc, xch, cgh, cw, out):
        i = pl.program_id(0)
        a = cg[...].astype(F32) * xc[...].astype(F32)
        ah = cgh[...].astype(F32) * xch[...].astype(F32)
        ah = jnp.where(i > 0, ah, 0.0)
        row = lax.broadcasted_iota(jnp.int32, (tm, tc), 0)
        a1 = jnp.where(row == 0, ah[HALO - 1:HALO, :], pltpu.roll(a, 1, 0))
        a2 = jnp.where(row == 0, ah[HALO - 2:HALO - 1, :],
                       jnp.where(row == 1, ah[HALO - 1:HALO, :], pltpu.roll(a, 2, 0)))
        w = cw[...]
        conv = w[0:1, :] * a2 + w[1:2, :] * a1 + w[2:3, :] * a
        z = zc[...].astype(F32)
        out[...] = (z * _sigmoid(z) * bg[...].astype(F32) * conv).astype(BF)

    return pl.pallas_call(
        body, name="conv_fwd", grid=(S // tm, nct),
        in_specs=[seg(0), seg(1), seg(2), seg(3), halo_before(0), halo_before(2),
                  pl.BlockSpec((8, tc), lambda i, j: (0, j))],
        out_specs=pl.BlockSpec((tm, tc), lambda i, j: (i, j)),
        out_shape=jax.ShapeDtypeStruct((S, D), BF),
        compiler_params=_params(2),
    )(pa, pa, pa, pa, pa, pa, cw8)


ATT_UNROLL = 32


LAYOUT_MOD = 4
RUN = QB // LAYOUT_MOD


def _fold_masks(d):
    row = lax.broadcasted_iota(jnp.int32, (QB, QB), 0)
    lane = lax.broadcasted_iota(jnp.int32, (QB, QB), 1)
    if d == 1:
        qpos, kpos = LAYOUT_MOD * (row % RUN) + row // RUN, LAYOUT_MOD * (lane % RUN) + lane // RUN
    else:
        qpos, kpos = row, lane
    tri_le = kpos <= qpos
    dist = jnp.where(tri_le, qpos - kpos, qpos - kpos + QB).astype(F32)
    return tri_le, dist, lane < HEAD_DIM


class _Rows:
    def __init__(self, slices):
        self.slices = slices

    def get(self, ref):
        parts = [ref[sl, :] for sl in self.slices]
        return parts[0] if len(parts) == 1 else jnp.concatenate(parts, axis=0)

    def put(self, ref, val):
        size = QB // len(self.slices)
        for g, sl in enumerate(self.slices):
            ref[sl, :] = val if len(self.slices) == 1 else val[g * size:(g + 1) * size]

    def add(self, ref, val):
        self.put(ref, self.get(ref) + val)


def _block_rows(b, d, S):
    quarter = S // LAYOUT_MOD
    nb = S // (QB * d)
    r, n = b // nb, b % nb
    n_prev = jnp.maximum(n - 1, 0)
    if d == 1:
        runs = lambda m: _Rows([pl.ds(pl.multiple_of(g * quarter + RUN * m, RUN), RUN) for g in range(LAYOUT_MOD)])
        return n, runs(n), runs(n_prev)
    if d == LAYOUT_MOD:
        block = lambda m: _Rows([pl.ds(pl.multiple_of(r * quarter + QB * m, QB), QB)])
        return n, block(n), block(n_prev)
    step = d // LAYOUT_MOD
    first = (r % LAYOUT_MOD) * quarter + r // LAYOUT_MOD
    strided = lambda m: _Rows([pl.ds(first + QB * step * m, QB, stride=step)])
    return n, strided(n), strided(n_prev)


def _natural_rows(i, S):
    per = S // LAYOUT_MOD // QB
    return pl.ds(i // per + LAYOUT_MOD * QB * (i % per), QB, stride=LAYOUT_MOD)


def _head_sum_matrix():
    r = lax.broadcasted_iota(jnp.int32, (2 * QB, 2 * QB), 0)
    c = lax.broadcasted_iota(jnp.int32, (2 * QB, 2 * QB), 1)
    return (((r % QB) // HEAD_DIM) == (c // QB)).astype(F32).astype(BF)


def _hi_lo(t):
    hi = t.astype(BF)
    return jnp.concatenate([hi, (t - hi.astype(F32)).astype(BF)], axis=1)


PROJ_ROWS = 512


def _attn_fwd(u, slopes, w_all, w3_all, cw_all):
    S = u.shape[0]
    hpr = HEAD_PAIRS
    n_blocks = S // QB
    later = range(3, 3 * hpr)

    def body(sl_ref, u_ref, w_in_ref, w3_in_ref, cw_in_ref, o_ref, lse_ref, q_ref, k_ref, v_ref, w_ref, w3_ref,
             cw_ref, acc, m_s, l_s, w_tile, staged, tile_sems, *sems):
        hp = pl.program_id(0)
        me = _my_place()[3]
        pieces = _PieceGather(lambda p, lo, hi: w_ref.at[p, :, lo:hi], w_ref, *sems[0:2])
        gathers = (_WeightGather(lambda p, cols: _block(w_ref, me, cols), w_ref, *sems[2:5], REST_COLS),
                   _WeightGather(lambda p, cols: w3_ref.at[me], w3_ref, *sems[5:8], _whole),
                   _WeightGather(lambda p, cols: cw_ref.at[me], cw_ref, *sems[8:11], _whole))

        @pl.when(hp == 0)
        def _():
            pieces.start(later)
            for g in gathers:
                g.start()

        @pl.when(hp == FORWARD_STEP)
        def _():
            for g in gathers:
                g.forward()

        for h in range(hpr):
            @pl.when(hp == h)
            def _(h=h):
                if h > 0:
                    pieces.wait_recv(range(3 * h, 3 * h + 3))
                fetch = []
                for seg in range(3):
                    p, lo = _qkv_piece(h, seg)
                    fetch.append(pltpu.make_async_copy(w_ref.at[p, :, lo:lo + 128], w_tile.at[:, seg * 128:(seg + 1) * 128],
                                                       tile_sems.at[seg]))
                    fetch[-1].start()
                for cp in fetch:
                    cp.wait()

        def project(i, carry):
            rows = pl.ds(pl.multiple_of(i * PROJ_ROWS, PROJ_ROWS), PROJ_ROWS)
            qkv = _dot(u_ref[rows, :], w_tile[...])
            per = PROJ_ROWS // LAYOUT_MOD
            for seg, ref in enumerate((q_ref, k_ref, v_ref)):
                staged[seg] = qkv[:, seg * 128:(seg + 1) * 128]
                for g in range(LAYOUT_MOD):
                    dst = pl.ds(pl.multiple_of(g * (S // LAYOUT_MOD) + i * per, per), per)
                    ref[dst, :] = staged.at[seg][pl.ds(g, per, stride=LAYOUT_MOD), :]
            return carry

        lax.fori_loop(0, S // PROJ_ROWS, project, 0)

        head_sum = _head_sum_matrix()
        ones_b = jnp.ones((2 * QB, QB), BF)
        m_s[...] = jnp.full(m_s.shape, NEG, F32)
        l_s[...] = jnp.zeros(l_s.shape, F32)
        acc[...] = jnp.zeros(acc.shape, F32)

        for d in DILATIONS:
            tri_le, dist, low = _fold_masks(d)
            low_b = low.astype(F32).astype(BF)
            high_b = 1.0 - low_b
            slope = [sl_ref[2 * hp + a] * float(d) for a in range(2)]
            bias = [slope[a] * dist for a in range(2)]

            def block(b, d=d, slope=slope, bias=bias, tri_le=tri_le, low=low, low_b=low_b, high_b=high_b):
                n, cur, prev = _block_rows(b, d, S)
                has_prev = n > 0
                valid = jnp.logical_or(tri_le, has_prev)
                q2 = (cur.get(q_ref) * 0.125).astype(BF)
                qs = jnp.concatenate([q2 * low_b, q2 * high_b], axis=0)
                vp = prev.get(v_ref)
                kp_b = prev.get(k_ref).astype(BF)
                kcat = jnp.concatenate([kp_b, cur.get(k_ref).astype(BF)], axis=0)
                vcat = jnp.concatenate([vp, cur.get(v_ref)], axis=0).astype(BF)
                s2 = _dot_nt(qs, kcat)
                e2 = _dot(_hi_lo(q2.astype(F32) * kp_b.astype(F32)), head_sum)
                p_rows, alpha_h, pe_h = [], [], []
                for a in range(2):
                    sp, sc = s2[a * QB:(a + 1) * QB, :QB], s2[a * QB:(a + 1) * QB, QB:]
                    comb = jnp.where(valid, jnp.where(tri_le, sc, sp) - bias[a], NEG)
                    e = jnp.where(has_prev, e2[:, a * QB:(a + 1) * QB] - slope[a] * float(QB), NEG)
                    m_old = cur.get(m_s.at[a])
                    m_new = jnp.maximum(jnp.maximum(m_old, jnp.max(comb, axis=-1, keepdims=True)), e)
                    cur.put(m_s.at[a], m_new)
                    p = jnp.exp(comb - m_new)
                    pe_h.append(jnp.exp(e - m_new))
                    alpha_h.append(jnp.exp(m_old - m_new))
                    p_rows.append(jnp.concatenate([jnp.where(tri_le, 0.0, p).astype(BF),
                                                   jnp.where(tri_le, p, 0.0).astype(BF)], axis=1))
                pv = _dot(jnp.concatenate(p_rows, axis=0), jnp.concatenate([vcat, ones_b], axis=1))
                for a in range(2):
                    cur.put(l_s.at[a], alpha_h[a] * cur.get(l_s.at[a]) + pv[a * QB:(a + 1) * QB, QB:] + pe_h[a])
                cur.put(acc, jnp.where(low, alpha_h[0], alpha_h[1]) * cur.get(acc)
                        + jnp.where(low, pv[:QB, :QB], pv[QB:, :QB]) + jnp.where(low, pe_h[0], pe_h[1]) * vp)

            def several(it, carry, block=block):
                for u in range(ATT_UNROLL):
                    block(it * ATT_UNROLL + u)
                return carry

            lax.fori_loop(0, n_blocks // ATT_UNROLL, several, 0)

        low = _fold_masks(LAYOUT_MOD)[2]

        def finish(i, carry):
            rows = pl.ds(pl.multiple_of(i * QB, QB), QB)
            l0, l1 = l_s[0, rows, :], l_s[1, rows, :]
            o_ref[_natural_rows(i, S), :] = acc[rows, :] / jnp.where(low, l0, l1)
            lse_ref[0, rows, :] = m_s[0, rows, :] + jnp.log(l0)
            lse_ref[1, rows, :] = m_s[1, rows, :] + jnp.log(l1)
            return carry

        lax.fori_loop(0, n_blocks, finish, 0)

        @pl.when(hp == hpr - 1)
        def _():
            pieces.wait_send(later)
            for g in gathers:
                g.finish()

    col = pl.BlockSpec((S, 128), lambda h: (0, h))
    act = jax.ShapeDtypeStruct((S, D), F32)
    gathered = (w_all, w3_all, cw_all)
    return pl.pallas_call(
        body, name="attn_fwd", grid=(hpr,),
        in_specs=[SMEM_SPEC, VMEM_SPEC, ANY_SPEC, ANY_SPEC, ANY_SPEC],
        out_specs=(col, pl.BlockSpec((2, S, 128), lambda h: (0, 0, h)), col, col, col, ANY_SPEC, ANY_SPEC, ANY_SPEC),
        out_shape=(act, jax.ShapeDtypeStruct((2, S, D), F32), act, act, act,
                   *[jax.ShapeDtypeStruct(t.shape, t.dtype) for t in gathered]),
        scratch_shapes=([pltpu.VMEM((S, 128), F32), pltpu.VMEM((2, S, 128), F32), pltpu.VMEM((2, S, 128), F32),
                         pltpu.VMEM((D, 3 * 128), BF), pltpu.VMEM((3, PROJ_ROWS, 128), F32),
                         pltpu.SemaphoreType.DMA((3,))]
                        + _piece_sems(3 * hpr) + WEIGHT_GATHER_SEMS * 3),
        input_output_aliases={2: 5, 3: 6, 4: 7},
        compiler_params=_params(1),
    )(slopes, u, *gathered)


def _set_rows(shape, rows):
    idx = lax.broadcasted_iota(jnp.int32, shape, 0)
    out = jnp.zeros(shape, F32)
    for r, val in rows.items():
        out = out + jnp.where(idx == r, val, 0.0)
    return out


def _mid(yc_in, pa_mid, o, x2, target, b_merge, final_g, w3):
    S = x2.shape[0]
    tm = ROW_TILE
    nsteps = S // tm
    tile = pl.BlockSpec((tm, D), lambda i: (i, 0))

    def body(yc_ref, za_ref, gcp_ref, gap_ref, o_ref, x_ref, t_ref, b_ref, fg_ref, w_ref,
             dh_ref, dmid_ref, do_ref, dyc_ref, gw_ref, small_ref, acc, stage):
        i = pl.program_id(0)

        @pl.when(i == 0)
        def _():
            acc[...] = jnp.zeros_like(acc)
            small_ref[...] = jnp.zeros_like(small_ref)

        wc, wa, wo = w_ref[0], w_ref[1], w_ref[2]
        z = za_ref[...].astype(F32)
        sg = _sigmoid(z)
        ov = o_ref[...]
        yc_in_b, ya_in_b = yc_ref[...], (z * sg * ov).astype(BF)
        yc = _dot(yc_in_b, wc)
        ya = _dot(ya_in_b, wa)
        b = b_ref[...]
        gc = _sigmoid(gcp_ref[...].astype(F32) + b[:, :D])
        ga = _sigmoid(gap_ref[...].astype(F32) + b[:, D:])
        merged = gc * yc + ga * ya
        merged_b = merged.astype(BF)
        h = x_ref[...] + _dot(merged_b, wo)
        r2 = lax.rsqrt(jnp.mean(h * h, axis=-1, keepdims=True) + EPS)
        n = h * r2
        fg = fg_ref[...]
        err = n * fg - t_ref[...]
        loss = 0.5 * jnp.sum(jnp.sum(err * err, axis=-1, keepdims=True) / D, axis=0, keepdims=True)
        dy = err / D
        g_fg = jnp.sum(dy * n, axis=0, keepdims=True)
        dn = dy * fg
        dh = r2 * (dn - n * jnp.mean(dn * n, axis=-1, keepdims=True))
        dh_ref[...] = dh
        dh_b = dh.astype(BF)
        dmerged = _dot_nt(dh_b, wo)
        acc[2] += _dot(merged.T.astype(BF), dh_b)
        dyc = (dmerged * gc).astype(BF)
        dya = (dmerged * ga).astype(BF)
        dgcp = dmerged * yc * gc * (1.0 - gc)
        dgap = dmerged * ya * ga * (1.0 - ga)
        dmid_ref[1] = dgcp.astype(BF)
        dmid_ref[2] = dgap.astype(BF)
        acc[0] += _dot(yc_in_b.astype(F32).T.astype(BF), dyc)
        acc[1] += _dot(ya_in_b.astype(F32).T.astype(BF), dya)
        dyc_ref[...] = _dot_nt(dyc, wc).astype(BF)
        dya_in = _dot_nt(dya, wa)
        do_ref[...] = dya_in * (z * sg)
        dmid_ref[0] = (dya_in * ov * (sg * (1.0 + z * (1.0 - sg)))).astype(BF)
        small_ref[...] += _set_rows((8, D), {
            1: jnp.sum(dgcp, axis=0, keepdims=True), 2: jnp.sum(dgap, axis=0, keepdims=True),
            3: g_fg, 7: jnp.broadcast_to(loss, (1, D))})

        @pl.when(i == nsteps - 1)
        def _():
            for p in range(N_DEV):
                for a in range(3):
                    stage[...] = acc[a, p * ROW_SHARD:(p + 1) * ROW_SHARD, :].astype(BF)
                    pltpu.sync_copy(stage, gw_ref.at[p, a])

    return pl.pallas_call(
        body, name="mid", grid=(nsteps,),
        in_specs=[tile, pl.BlockSpec((tm, D), lambda i: (i, 0)), pl.BlockSpec((tm, D), lambda i: (i, 1)),
                  pl.BlockSpec((tm, D), lambda i: (i, 2)), tile, tile, tile,
                  pl.BlockSpec((1, 2 * D), lambda i: (0, 0)), pl.BlockSpec((1, D), lambda i: (0, 0)), VMEM_SPEC],
        out_specs=(tile, pl.BlockSpec((3, tm, D), lambda i: (0, i, 0)), tile, tile,
                   ANY_SPEC, pl.BlockSpec((8, D), lambda i: (0, 0))),
        out_shape=(jax.ShapeDtypeStruct((S, D), F32), jax.ShapeDtypeStruct((3, S, D), BF),
                   jax.ShapeDtypeStruct((S, D), F32), jax.ShapeDtypeStruct((S, D), BF),
                   jax.ShapeDtypeStruct((N_DEV, 3, ROW_SHARD, D), BF), jax.ShapeDtypeStruct((8, D), F32)),
        scratch_shapes=[pltpu.VMEM((3, D, D), F32), pltpu.VMEM((ROW_SHARD, D), BF)],
        compiler_params=_params(1),
    )(yc_in, pa_mid, pa_mid, pa_mid, o, x2, target, b_merge, final_g, w3)


def _conv_bwd(dyc_in, pa, cw8):
    S = pa.shape[0]
    tm, tc = CONV_TM, CONV_TC
    nct = D // tc
    nrt = S // tm
    last_halo = S // HALO - 1

    def seg(s):
        return pl.BlockSpec((tm, tc), lambda j, i, s=s: (i, s * nct + j))

    def halo_before(s):
        return pl.BlockSpec((HALO, tc), lambda j, i, s=s: (jnp.maximum(i * (tm // HALO) - 1, 0), s * nct + j))

    def halo_after(s):
        return pl.BlockSpec((HALO, tc), lambda j, i, s=s: (jnp.minimum((i + 1) * (tm // HALO), last_halo), s * nct + j))

    def body(dy, xc, bg, cg, zc, xch, cgh, dyn, bgn, zcn, cw, dout, gcw):
        i = pl.program_id(1)

        @pl.when(i == 0)
        def _():
            gcw[...] = jnp.zeros_like(gcw)

        xcv, cgv = xc[...].astype(F32), cg[...].astype(F32)
        a = cgv * xcv
        ah = jnp.where(i > 0, cgh[...].astype(F32) * xch[...].astype(F32), 0.0)
        row = lax.broadcasted_iota(jnp.int32, (tm, tc), 0)
        a1 = jnp.where(row == 0, ah[HALO - 1:HALO, :], pltpu.roll(a, 1, 0))
        a2 = jnp.where(row == 0, ah[HALO - 2:HALO - 1, :],
                       jnp.where(row == 1, ah[HALO - 1:HALO, :], pltpu.roll(a, 2, 0)))
        w = cw[...]
        conv = w[0:1, :] * a2 + w[1:2, :] * a1 + w[2:3, :] * a
        z = zc[...].astype(F32)
        sg = _sigmoid(z)
        silu = z * sg
        bgv = bg[...].astype(F32)
        dyv = dy[...].astype(F32)
        dout[3] = (dyv * bgv * conv * (sg * (1.0 + z * (1.0 - sg)))).astype(BF)
        dout[1] = (dyv * silu * conv).astype(BF)
        dc = dyv * silu * bgv
        zn = zcn[...].astype(F32)
        dcn = dyn[...].astype(F32) * (zn * _sigmoid(zn)) * bgn[...].astype(F32)
        dcn = jnp.where(i < nrt - 1, dcn, 0.0)
        dc1 = jnp.where(row == tm - 1, dcn[0:1, :], pltpu.roll(dc, tm - 1, 0))
        dc2 = jnp.where(row == tm - 1, dcn[1:2, :],
                        jnp.where(row == tm - 2, dcn[0:1, :], pltpu.roll(dc, tm - 2, 0)))
        da = w[2:3, :] * dc + w[1:2, :] * dc1 + w[0:1, :] * dc2
        dout[2] = (da * xcv).astype(BF)
        dout[0] = (da * cgv).astype(BF)
        gcw[...] += _set_rows((8, tc), {
            4: jnp.sum(dc * a2, axis=0, keepdims=True), 5: jnp.sum(dc * a1, axis=0, keepdims=True),
            6: jnp.sum(dc * a, axis=0, keepdims=True)})

    return pl.pallas_call(
        body, name="conv_bwd", grid=(nct, nrt),
        in_specs=[pl.BlockSpec((tm, tc), lambda j, i: (i, j)), seg(0), seg(1), seg(2), seg(3),
                  halo_before(0), halo_before(2),
                  pl.BlockSpec((HALO, tc), lambda j, i: (jnp.minimum((i + 1) * (tm // HALO), last_halo), j)),
                  halo_after(1), halo_after(3), pl.BlockSpec((8, tc), lambda j, i: (0, j))],
        out_specs=(pl.BlockSpec((4, tm, tc), lambda j, i: (0, i, j)), pl.BlockSpec((8, tc), lambda j, i: (0, j))),
        out_shape=(jax.ShapeDtypeStruct((4, S, D), BF), jax.ShapeDtypeStruct((8, D), F32)),
        compiler_params=_params(2),
    )(dyc_in, pa, pa, pa, pa, pa, pa, dyc_in, pa, pa, cw8)


def _attn_bwd(q, k, v, slopes, do, o, lse, g_in, g_3):
    S = q.shape[0]
    hpr = HEAD_PAIRS
    n_blocks = S // QB

    def body(sl_ref, q_ref, k_ref, v_ref, do_ref, o_ref, lse_ref, gin_ref, g3_ref, out_ref, rin_ref, r3_ref,
             dq_s, dk_s, dv_s, do_s, dd_s, *sems):
        hp = pl.program_id(0)
        exchanges = (_GradExchange(gin_ref, rin_ref, *sems[:3], _shard_cols((0, SEG0_ATTN * D), (SEG0_MID * D, IN_COLS))),
                     _GradExchange(g3_ref, r3_ref, *sems[3:], _whole))

        @pl.when(hp == 0)
        def _():
            for ex in exchanges:
                ex.start()

        head_sum = _head_sum_matrix()
        dq_s[...] = jnp.zeros(dq_s.shape, F32)
        dk_s[...] = jnp.zeros(dk_s.shape, F32)
        dv_s[...] = jnp.zeros(dv_s.shape, F32)

        def row_dots(i, carry):
            rows = pl.ds(pl.multiple_of(i * QB, QB), QB)
            natural = _natural_rows(i, S)
            do_c = do_ref[natural, :]
            do_s[rows, :] = do_c
            dd = _dot(_hi_lo(do_c * o_ref[natural, :]), head_sum)
            dd_s[0, rows, :] = dd[:, :QB]
            dd_s[1, rows, :] = dd[:, QB:]
            return carry

        lax.fori_loop(0, n_blocks, row_dots, 0)

        for d in DILATIONS:
            tri_le, dist, low = _fold_masks(d)
            low_b = low.astype(F32).astype(BF)
            high_b = 1.0 - low_b
            slope = [sl_ref[2 * hp + a] * float(d) for a in range(2)]
            bias = [slope[a] * dist for a in range(2)]

            def block(b, d=d, slope=slope, bias=bias, tri_le=tri_le, low=low, low_b=low_b, high_b=high_b):
                n, cur, prev = _block_rows(b, d, S)
                has_prev = n > 0
                valid = jnp.logical_or(tri_le, has_prev)
                q2f = cur.get(q_ref) * 0.125
                q2 = q2f.astype(BF)
                qs = jnp.concatenate([q2 * low_b, q2 * high_b], axis=0)
                kp, vp = prev.get(k_ref), prev.get(v_ref)
                kp_b, vp_b = kp.astype(BF), vp.astype(BF)
                kcat = jnp.concatenate([kp_b, cur.get(k_ref).astype(BF)], axis=0)
                vcat = jnp.concatenate([vp_b, cur.get(v_ref).astype(BF)], axis=0)
                do2f = cur.get(do_s)
                do2 = do2f.astype(BF)
                dos = jnp.concatenate([do2 * low_b, do2 * high_b], axis=0)
                s2 = _dot_nt(qs, kcat)
                dp2 = _dot_nt(dos, vcat)
                diag2 = _dot(jnp.concatenate([_hi_lo(q2.astype(F32) * kp_b.astype(F32)),
                                              _hi_lo(do2.astype(F32) * vp_b.astype(F32))], axis=0), head_sum)
                p_rows, ds_rows, pe_h, dse_h = [], [], [], []
                for a in range(2):
                    hs = slice(a * QB, (a + 1) * QB)
                    sp, sc = s2[hs, :QB], s2[hs, QB:]
                    dpp, dpc = dp2[hs, :QB], dp2[hs, QB:]
                    lse_a, dd_a = cur.get(lse_ref.at[a]), cur.get(dd_s.at[a])
                    comb = jnp.where(tri_le, sc, sp) - bias[a]
                    e = diag2[:QB, hs] - slope[a] * float(QB)
                    p = jnp.where(valid, jnp.exp(comb - lse_a), 0.0)
                    pe = jnp.where(has_prev, jnp.exp(e - lse_a), 0.0)
                    ds = p * (jnp.where(tri_le, dpc, dpp) - dd_a)
                    dse_h.append(pe * (diag2[QB:, hs] - dd_a))
                    pe_h.append(pe)
                    p_rows.append(jnp.concatenate([jnp.where(tri_le, 0.0, p).astype(BF),
                                                   jnp.where(tri_le, p, 0.0).astype(BF)], axis=1))
                    ds_rows.append(jnp.concatenate([jnp.where(tri_le, 0.0, ds).astype(BF),
                                                    jnp.where(tri_le, ds, 0.0).astype(BF)], axis=1))
                pst = jnp.concatenate(p_rows, axis=0)
                dst = jnp.concatenate(ds_rows, axis=0)
                pe2 = jnp.where(low, pe_h[0], pe_h[1])
                dse2 = jnp.where(low, dse_h[0], dse_h[1])
                dq = _dot(dst, kcat)
                cur.add(dq_s, (jnp.where(low, dq[:QB], dq[QB:]) + dse2 * kp) * 0.125)
                dk = _dot_tn(dst, qs)
                dv = _dot_tn(pst, dos)
                prev.add(dk_s, dk[:QB] + dse2 * q2f)
                cur.add(dk_s, dk[QB:])
                prev.add(dv_s, dv[:QB] + pe2 * do2f)
                cur.add(dv_s, dv[QB:])

            def several(it, carry, block=block):
                for u in range(ATT_UNROLL):
                    block(it * ATT_UNROLL + u)
                return carry

            lax.fori_loop(0, n_blocks // ATT_UNROLL, several, 0)

        def finish(i, carry):
            rows = pl.ds(pl.multiple_of(i * QB, QB), QB)
            natural = _natural_rows(i, S)
            for t, ref in enumerate((dq_s, dk_s, dv_s)):
                out_ref.at[t][natural, :] = ref[rows, :]
            return carry

        lax.fori_loop(0, n_blocks, finish, 0)

        @pl.when(hp == hpr - 1)
        def _():
            for ex in exchanges:
                ex.finish()

    col = pl.BlockSpec((S, 128), lambda h: (0, h))
    return pl.pallas_call(
        body, name="attn_bwd", grid=(hpr,),
        in_specs=[SMEM_SPEC, col, col, col, col, col, pl.BlockSpec((2, S, 128), lambda h: (0, 0, h)),
                  ANY_SPEC, ANY_SPEC],
        out_specs=(pl.BlockSpec((3, S, 128), lambda h: (0, 0, h)), ANY_SPEC, ANY_SPEC),
        out_shape=(jax.ShapeDtypeStruct((3, S, D), F32), jax.ShapeDtypeStruct(g_in.shape, BF),
                   jax.ShapeDtypeStruct(g_3.shape, BF)),
        scratch_shapes=([pltpu.VMEM((S, 128), F32)] * 4 + [pltpu.VMEM((2, S, 128), F32)]
                        + GRAD_EXCHANGE_SEMS + GRAD_EXCHANGE_SEMS),
        compiler_params=_params(1),
    )(slopes, q, k, v, do, o, lse, g_in, g_3)


WG_TN = 256
SEG0_CONV, SEG0_ATTN, SEG0_MID = 0, 4, 7


def _wgrad_in(ut, d_group, seg0, g_in, name):
    S = ut.shape[1]
    tn = WG_TN
    per_seg = D // tn
    per_shard = W_IN_SHARD // tn
    n_tiles = d_group.shape[0] * per_seg
    tile0 = seg0 * per_seg

    def body(ut_ref, d_ref, *rest):
        rest[-1][0] = _dot(ut_ref[...], d_ref[0].astype(BF)).astype(BF)

    operands, in_specs, aliases = [ut, d_group], [VMEM_SPEC, pl.BlockSpec((1, S, tn), lambda t: (t // per_seg, 0, t % per_seg))], {}
    if g_in is not None:
        operands.append(g_in)
        in_specs.append(ANY_SPEC)
        aliases = {2: 0}
    return pl.pallas_call(
        body, name=name, grid=(n_tiles,), in_specs=in_specs,
        out_specs=pl.BlockSpec((1, D, tn), lambda t: ((tile0 + t) // per_shard, 0, (tile0 + t) % per_shard)),
        out_shape=jax.ShapeDtypeStruct((N_DEV, D, W_IN_SHARD), BF),
        input_output_aliases=aliases,
        compiler_params=_params(1),
    )(*operands)


def _dgrad_norm_bwd(d_conv, d_attn, d_mid, w_all, x2, dh, norm_g):
    S = x2.shape[0]
    tm = ROW_TILE
    nsteps = S // tm
    tile = pl.BlockSpec((tm, D), lambda i: (i, 0))
    pieces = _proj_pieces()

    def body(a_ref, b_ref, c_ref, w_ref, x_ref, dh_ref, g_ref, gx_ref, small_ref):
        i = pl.program_id(0)

        @pl.when(i == 0)
        def _():
            small_ref[...] = jnp.zeros_like(small_ref)

        groups = (a_ref, b_ref, c_ref)
        du = jnp.zeros((tm, D), F32)
        for s, sc, p, pc, width in pieces:
            g = 0 if s < 4 else (1 if s < 7 else 2)
            local = s - (0, 4, 7)[g]
            du = du + _dot_nt(groups[g][local, :, sc:sc + width].astype(BF), w_ref[p, :, pc:pc + width])
        xv = x_ref[...]
        r = lax.rsqrt(jnp.mean(xv * xv, axis=-1, keepdims=True) + EPS)
        n = xv * r
        dn = du * g_ref[...]
        gx_ref[...] = dh_ref[...] + r * (dn - n * jnp.mean(dn * n, axis=-1, keepdims=True))
        small_ref[...] += _set_rows((8, D), {0: jnp.sum(du * n, axis=0, keepdims=True)})

    return pl.pallas_call(
        body, name="dgrad_norm_bwd", grid=(nsteps,),
        in_specs=[pl.BlockSpec((4, tm, D), lambda i: (0, i, 0)), pl.BlockSpec((3, tm, D), lambda i: (0, i, 0)),
                  pl.BlockSpec((3, tm, D), lambda i: (0, i, 0)), VMEM_SPEC, tile, tile,
                  pl.BlockSpec((1, D), lambda i: (0, 0))],
        out_specs=(tile, pl.BlockSpec((8, D), lambda i: (0, 0))),
        out_shape=(jax.ShapeDtypeStruct((S, D), F32), jax.ShapeDtypeStruct((8, D), F32)),
        compiler_params=_params(1),
    )(d_conv, d_attn, d_mid, w_all, x2, dh, norm_g)


HBM_SPEC = pl.BlockSpec(memory_space=pltpu.HBM)
SEM_SPEC = pl.BlockSpec(memory_space=pltpu.SEMAPHORE)
ATTN_COLS = _shard_cols((SEG0_ATTN * D, SEG0_MID * D))


def _attn_cols_exchange_start(g_in, r_in):
    def body(g_ref, r_ref, send_sems, recv_sems, g_thru, r_thru, token):
        _GradExchange(g_ref, r_ref, send_sems, recv_sems, None, ATTN_COLS).start()
        token[...] = jnp.zeros_like(token)

    hbm = pltpu.with_memory_space_constraint
    return pl.pallas_call(
        body, name="attn_cols_exchange_start",
        out_shape=(pltpu.SemaphoreType.DMA((N_DEV,)), pltpu.SemaphoreType.DMA((N_DEV,)),
                   pltpu.HBM(g_in.shape, g_in.dtype), pltpu.HBM(r_in.shape, r_in.dtype),
                   jax.ShapeDtypeStruct((8, 128), F32)),
        in_specs=(HBM_SPEC, HBM_SPEC), out_specs=(SEM_SPEC, SEM_SPEC, HBM_SPEC, HBM_SPEC, VMEM_SPEC),
        input_output_aliases={0: 2, 1: 3},
        compiler_params=pltpu.CompilerParams(has_side_effects=pltpu.SideEffectType.DATAFLOW_SIDE_EFFECTING),
    )(hbm(g_in, pltpu.HBM), hbm(r_in, pltpu.HBM))


def _attn_cols_exchange_wait(send_sems, recv_sems, g_thru, r_thru, after):
    def body(g_ref, r_ref, send_sems, recv_sems, after_ref, g_dead, r_out):
        _GradExchange(g_ref, r_ref, send_sems, recv_sems, None, ATTN_COLS).finish()

    return pl.pallas_call(
        body, name="attn_cols_exchange_wait",
        out_shape=(pltpu.HBM(g_thru.shape, g_thru.dtype), pltpu.HBM(r_thru.shape, r_thru.dtype)),
        in_specs=(HBM_SPEC, HBM_SPEC, SEM_SPEC, SEM_SPEC, ANY_SPEC), out_specs=(HBM_SPEC, HBM_SPEC),
        input_output_aliases={0: 0, 1: 1},
        compiler_params=pltpu.CompilerParams(has_side_effects=pltpu.SideEffectType.DATAFLOW_SIDE_EFFECTING),
    )(g_thru, r_thru, send_sems, recv_sems, after)


def _adamw_math(w, g, m, v):
    m = ADAM_B1 * m + (1.0 - ADAM_B1) * g
    v = ADAM_B2 * v + (1.0 - ADAM_B2) * (g * g)
    m_hat = m / (1.0 - ADAM_B1 ** ADAM_STEP)
    v_hat = v / (1.0 - ADAM_B2 ** ADAM_STEP)
    delta = -ADAM_LR * (m_hat / (jnp.sqrt(v_hat) + ADAM_EPS) + ADAM_WD * w)
    return delta, m, v


def _sum_adamw(parts, w, m, v, tm, name):
    R, C = w.shape
    tile = pl.BlockSpec((tm, C), lambda i: (i, 0))

    def body(p_ref, w_ref, m_ref, v_ref, g_out, d_out, m_out, v_out):
        g = p_ref[0].astype(F32)
        for s in range(1, N_DEV):
            g = g + p_ref[s].astype(F32)
        g_out[...] = g
        d_out[...], m_out[...], v_out[...] = _adamw_math(w_ref[...], g, m_ref[...], v_ref[...])

    shape = jax.ShapeDtypeStruct((R, C), F32)
    return pl.pallas_call(
        body, name=name, grid=(R // tm,),
        in_specs=[pl.BlockSpec((N_DEV, tm, C), lambda i: (0, i, 0)), tile, tile, tile],
        out_specs=(tile, tile, tile, tile), out_shape=(shape, shape, shape, shape),
        compiler_params=_params(1),
    )(parts, w, m, v)


def _adamw(g, w, m, v, name):
    def body(g_ref, w_ref, m_ref, v_ref, d_out, m_out, v_out):
        d_out[...], m_out[...], v_out[...] = _adamw_math(w_ref[...], g_ref[...], m_ref[...], v_ref[...])

    shape = jax.ShapeDtypeStruct(w.shape, F32)
    return pl.pallas_call(
        body, name=name, in_specs=[VMEM_SPEC] * 4, out_specs=(VMEM_SPEC,) * 3, out_shape=(shape, shape, shape),
    )(g, w, m, v)


def _alibi_slopes():
    return jnp.exp2(-8.0 * jnp.arange(1, N_HEADS + 1, dtype=F32) / N_HEADS)


def _local_step(x2, target, norm_g, b_merge, final_g, w_in, w3_shard, cw_shard):
    slopes = _alibi_slopes()
    u, ut, w_all, w3_all, cw_all = _norm_gather_first_weights(x2, norm_g, w_in, w3_shard, cw_shard)
    o, lse, q, k, v, w_all, w3_all, cw_all = _attn_fwd(u, slopes, w_all, w3_all, cw_all)
    w3 = jnp.transpose(w3_all, (1, 0, 2, 3)).reshape(3, D, D)
    cw8 = jnp.transpose(cw_all, (1, 0, 2)).reshape(8, D)
    pa = _proj_cols(u, w_all, SEG0_CONV, 4, BF, "proj_conv")
    yc_in = _conv_fwd(pa, cw8)
    pa_mid = _proj_cols(u, w_all, SEG0_MID, 3, BF, "proj_mid")
    dh, d_mid, do, dyc_in, g_3, small_mid = _mid(yc_in, pa_mid, o, x2, target, b_merge, final_g, w3)
    g_in = _wgrad_in(ut, d_mid, SEG0_MID, None, "wgrad_in_mid")
    d_conv, small_conv = _conv_bwd(dyc_in, pa, cw8)
    g_in = _wgrad_in(ut, d_conv, SEG0_CONV, g_in, "wgrad_in_conv")
    d_attn, r_in, r_3 = _attn_bwd(q, k, v, slopes, do, o, lse, g_in, g_3)
    g_in = _wgrad_in(ut, d_attn, SEG0_ATTN, g_in, "wgrad_in_attn")
    *in_flight, token = _attn_cols_exchange_start(g_in, r_in)
    grad_x, small_norm = _dgrad_norm_bwd(d_conv, d_attn, d_mid, w_all, x2, dh, norm_g + token[0:1, 0:1])
    return grad_x, in_flight, r_3, small_mid, small_conv, small_norm


def kernel(x, norm_g, w_in, b_merge, conv_w, w_out_conv, w_out_attn, w_o, final_g, loss_target, m_norm_g, m_w_in, m_b_merge, m_conv_w, m_w_out_conv, m_w_out_attn, m_w_o, m_final_g, v_norm_g, v_w_in, v_b_merge, v_conv_w, v_w_out_conv, v_w_out_attn, v_w_o, v_final_g):
    me = 4 * lax.axis_index("x") + 2 * lax.axis_index("y") + lax.axis_index("c")
    stack3 = lambda a, b, c: jnp.concatenate([a, b, c], axis=0)
    pad8 = lambda a: jnp.pad(a, ((0, 8 - a.shape[0]), (0, 0)))

    w3_shard = stack3(w_out_conv, w_out_attn, w_o)
    final_g2 = final_g.reshape(1, D)
    grad_x, in_flight, r_3, small_mid, small_conv, small_norm = _local_step(
        x[0], loss_target[0], norm_g, b_merge, final_g2, w_in[0], w3_shard, pad8(conv_w[0]))

    small = _allreduce_small(small_mid, small_conv, small_norm)
    g_in, r_in = _attn_cols_exchange_wait(*in_flight, small)
    own = lax.dynamic_index_in_dim(g_in, me, 0, keepdims=True)
    r_in = lax.dynamic_update_slice(r_in, own, (me, 0, 0))

    g_w_in, d_w_in, nm_w_in, nv_w_in = _sum_adamw(r_in, w_in[0], m_w_in[0], v_w_in[0], 128, "adamw_w_in")
    g_w3, d_w3, nm_w3, nv_w3 = _sum_adamw(
        r_3.reshape(N_DEV, 3 * ROW_SHARD, D), w3_shard.reshape(3 * ROW_SHARD, D),
        stack3(m_w_out_conv, m_w_out_attn, m_w_o).reshape(3 * ROW_SHARD, D),
        stack3(v_w_out_conv, v_w_out_attn, v_w_o).reshape(3 * ROW_SHARD, D), ROW_SHARD, "adamw_w3")

    def pack(ng, bm, fg):
        return pad8(jnp.concatenate([ng, bm.reshape(2, D), fg.reshape(1, D)], axis=0))

    d_s, nm_s, nv_s = _adamw(small, pack(norm_g, b_merge, final_g), pack(m_norm_g, m_b_merge, m_final_g),
                             pack(v_norm_g, v_b_merge, v_final_g), "adamw_small")
    g_cw = lax.dynamic_slice(small, (4, me * ROW_SHARD), (3, ROW_SHARD))
    d_cw, nm_cw, nv_cw = _adamw(g_cw, conv_w[0], m_conv_w[0], v_conv_w[0], "adamw_conv_w")

    loss = small[7, 0]
    split3 = lambda t: tuple(t[a * ROW_SHARD:(a + 1) * ROW_SHARD][None] for a in range(3))
    unpack = lambda t: (t[0:1], t[1:3].reshape(1, 2 * D), t[3])

    def leaves(in_, small_, cw_, w3_):
        ng, bm, fg = unpack(small_)
        wc, wa, wo = split3(w3_)
        return (ng, in_[None], bm, cw_[None], wc, wa, wo, fg)

    return (loss, grad_x[None],
            *leaves(g_w_in, small, g_cw, g_w3),
            *leaves(d_w_in, d_s, d_cw, d_w3),
            *leaves(nm_w_in, nm_s, nm_cw, nm_w3),
            *leaves(nv_w_in, nv_s, nv_cw, nv_w3))
```

```python
import functools

import jax
import jax.numpy as jnp
from jax import lax
from jax.experimental import pallas as pl
from jax.experimental.pallas import tpu as pltpu

D = 1024
N_HEADS = 16
HEAD_DIM = 64
N_SEG = 10
IN_COLS = N_SEG * D
N_DEV = 8
W_IN_SHARD = IN_COLS // N_DEV
ROW_SHARD = D // N_DEV
QB = 128
DILATIONS = (1, 4, 16)
EPS = 1e-6
NEG = -1e30
BF = jnp.bfloat16
F32 = jnp.float32
MESH = pl.DeviceIdType.MESH

ADAM_LR = 0.001
ADAM_B1 = 0.9
ADAM_B2 = 0.999
ADAM_EPS = 1e-08
ADAM_WD = 0.01
ADAM_STEP = 10

V7X_VMEM_BYTES = 64 * 1024 * 1024
VMEM_LIMIT = V7X_VMEM_BYTES - 8 * 1024 * 1024
ROW_TILE = 256

VMEM_SPEC = pl.BlockSpec(memory_space=pltpu.VMEM)
ANY_SPEC = pl.BlockSpec(memory_space=pl.ANY)
SMEM_SPEC = pl.BlockSpec(memory_space=pltpu.SMEM)


def _params(n_grid_axes, vmem=VMEM_LIMIT):
    return pltpu.CompilerParams(dimension_semantics=("arbitrary",) * n_grid_axes, vmem_limit_bytes=vmem)


def _dot(a, b):
    return jnp.dot(a, b, preferred_element_type=F32)


def _dot_nt(a, b):
    return lax.dot_general(a, b, (((1,), (1,)), ((), ())), preferred_element_type=F32)


def _dot_tn(a, b):
    return lax.dot_general(a, b, (((0,), (0,)), ((), ())), preferred_element_type=F32)


def _sigmoid(z):
    return 1.0 / (1.0 + jnp.exp(-z))


def _my_place():
    x, y, c = lax.axis_index("x"), lax.axis_index("y"), lax.axis_index("c")
    return x, y, c, 4 * x + 2 * y + c


def _peers(x, y, c):
    out = []
    for k in range(1, N_DEV):
        px = 1 - x if k & 4 else x
        py = 1 - y if k & 2 else y
        pc = 1 - c if k & 1 else c
        out.append(((px, py, pc), 4 * px + 2 * py + pc))
    return out


def _device(p):
    return (p >> 2, (p >> 1) & 1, p & 1)


def _shard_cols(*ranges):
    def cols(p):
        found = None
        for lo, hi in ranges:
            a, b = max(lo, p * W_IN_SHARD), min(hi, (p + 1) * W_IN_SHARD)
            if a < b:
                assert found is None
                found = (a - p * W_IN_SHARD, b - p * W_IN_SHARD)
        return found

    return cols


def _whole(p):
    return ()


def _block(ref, idx, cols):
    return ref.at[idx] if cols == () else ref.at[idx, :, cols[0]:cols[1]]


class _WeightGather:
    def __init__(self, src, dst, send_sems, forward_sems, recv_sems, cols):
        self.src, self.dst, self.cols = src, dst, cols
        self.send_sems, self.forward_sems, self.recv_sems = send_sems, forward_sems, recv_sems
        self.me = _my_place()[3]

    def _copy(self, p, target, passing_on=False):
        cols = self.cols(p)
        return pltpu.make_async_remote_copy(
            src_ref=_block(self.dst, p, cols) if passing_on else self.src(p, cols), dst_ref=_block(self.dst, p, cols),
            send_sem=self.forward_sems.at[p] if passing_on else self.send_sems.at[target],
            recv_sem=self.recv_sems.at[p], device_id=_device(target), device_id_type=MESH)

    def _as_each_device(self, own, relayed, other):
        for m in range(N_DEV):
            def branch(m=m):
                for p in range(N_DEV):
                    if self.cols(p) is None:
                        continue
                    if p == m:
                        for t in [m ^ 1] + [q for q in range(N_DEV) if q >> 1 != m >> 1 and q & 1 == m & 1]:
                            own(self._copy(m, t))
                    elif p >> 1 != m >> 1 and p & 1 == m & 1:
                        relayed(p, m ^ 1)
                    else:
                        other(p)

            pl.when(self.me == m)(branch)

    def start(self):
        self._as_each_device(lambda cp: cp.start(), lambda p, t: None, lambda p: None)

    def forward(self):
        def pass_on(p, t):
            self._copy(p, p).wait_recv()
            self._copy(p, t, passing_on=True).start()

        self._as_each_device(lambda cp: None, pass_on, lambda p: None)

    def finish(self):
        self._as_each_device(lambda cp: cp.wait_send(), lambda p, t: self._copy(p, t, passing_on=True).wait_send(),
                             lambda p: self._copy(p, p).wait_recv())


WEIGHT_GATHER_SEMS = [pltpu.SemaphoreType.DMA((N_DEV,))] * 3
FORWARD_STEP = 6
REST_COLS = _shard_cols((0, 4 * D), (7 * D, IN_COLS))
HEAD_PAIRS = D // 128


def _qkv_piece(h, seg):
    col = (4 + seg) * D + 128 * h
    return col // W_IN_SHARD, col % W_IN_SHARD


class _PieceGather:
    def __init__(self, src, dst, send_sems, recv_sems):
        self.src, self.dst, self.send_sems, self.recv_sems = src, dst, send_sems, recv_sems
        self.me = _my_place()[3]

    def _copy(self, i, target):
        p, lo = _qkv_piece(i // 3, i % 3)
        return pltpu.make_async_remote_copy(
            src_ref=self.src(p, lo, lo + 128), dst_ref=self.dst.at[p, :, lo:lo + 128], send_sem=self.send_sems.at[i, target],
            recv_sem=self.recv_sems.at[i], device_id=_device(target), device_id_type=MESH)

    def _owner(self, i, act):
        p = _qkv_piece(i // 3, i % 3)[0]

        def sender():
            for k in range(N_DEV - 1):
                act(self._copy(i, (p + 1 + (k + i) % (N_DEV - 1)) % N_DEV))

        pl.when(self.me == p)(sender)

    def start(self, pieces):
        for i in pieces:
            self._owner(i, lambda cp: cp.start())

    def wait_send(self, pieces):
        for i in pieces:
            self._owner(i, lambda cp: cp.wait_send())

    def wait_recv(self, pieces):
        for i in pieces:
            p = _qkv_piece(i // 3, i % 3)[0]
            pl.when(self.me != p)(lambda i=i, p=p: self._copy(i, p).wait_recv())


def _piece_sems(n):
    return [pltpu.SemaphoreType.DMA((n, N_DEV)), pltpu.SemaphoreType.DMA((n,))]


def _norm_gather_first_weights(x2, norm_g, w_in, w3, cw):
    S = x2.shape[0]
    tm = ROW_TILE
    nsteps = S // tm

    def body(x_ref, g_ref, w_in_ref, w3_ref, cw_ref, u_ref, ut_ref, o_in, o_3, o_cw, in_bf, w3_bf, local_sems, *sems):
        i = pl.program_id(0)
        me = _my_place()[3]
        gather = _PieceGather(lambda p, lo, hi: in_bf.at[:, lo:hi], o_in, *sems)
        local = [pltpu.make_async_copy(src, dst.at[me], local_sems.at[a])
                 for a, (src, dst) in enumerate(((in_bf, o_in), (w3_bf, o_3), (cw_ref, o_cw)))]

        @pl.when(i == 0)
        def _():
            def cast_rows(r, carry):
                rows = pl.ds(pl.multiple_of(r * 128, 128), 128)
                in_bf[rows, :] = w_in_ref[rows, :].astype(BF)
                return carry

            lax.fori_loop(0, D // 128, cast_rows, 0)
            for a in range(3):
                w3_bf[a] = w3_ref[a].astype(BF)
            gather.start(range(3))
            for cp in local:
                cp.start()

        xv = x_ref[...]
        r = lax.rsqrt(jnp.mean(xv * xv, axis=-1, keepdims=True) + EPS)
        u = xv * r * g_ref[...]
        u_ref[...] = u.astype(BF)
        ut_ref[...] = u.T.astype(BF)

        @pl.when(i == nsteps - 1)
        def _():
            gather.wait_recv(range(3))
            gather.wait_send(range(3))
            for cp in local:
                cp.wait()

    return pl.pallas_call(
        body, name="norm_gather_first_weights", grid=(nsteps,),
        out_shape=(jax.ShapeDtypeStruct((S, D), BF), jax.ShapeDtypeStruct((D, S), BF),
                   jax.ShapeDtypeStruct((N_DEV, D, W_IN_SHARD), BF),
                   jax.ShapeDtypeStruct((N_DEV, 3, ROW_SHARD, D), BF),
                   jax.ShapeDtypeStruct((N_DEV, 8, 128), F32)),
        in_specs=[pl.BlockSpec((tm, D), lambda i: (i, 0)), pl.BlockSpec((1, D), lambda i: (0, 0)),
                  VMEM_SPEC, VMEM_SPEC, VMEM_SPEC],
        out_specs=(pl.BlockSpec((tm, D), lambda i: (i, 0)), pl.BlockSpec((D, tm), lambda i: (0, i)),
                   ANY_SPEC, ANY_SPEC, ANY_SPEC),
        scratch_shapes=[pltpu.VMEM((D, W_IN_SHARD), BF), pltpu.VMEM((3, ROW_SHARD, D), BF),
                        pltpu.SemaphoreType.DMA((3,))] + _piece_sems(3),
        compiler_params=_params(1),
    )(x2, norm_g, w_in, w3, cw)


class _GradExchange:
    def __init__(self, src, dst, send_sems, recv_sems, local_sem, cols):
        self.src, self.dst, self.cols = src, dst, cols
        self.send_sems, self.recv_sems, self.local_sem = send_sems, recv_sems, local_sem
        self.me = _my_place()[3]

    def _remote(self, p, source):
        return pltpu.make_async_remote_copy(
            src_ref=_block(self.src, p, self.cols(p)), dst_ref=_block(self.dst, source, self.cols(p)),
            send_sem=self.send_sems.at[p], recv_sem=self.recv_sems.at[source],
            device_id=_device(p), device_id_type=MESH)

    def _local(self, p):
        return pltpu.make_async_copy(_block(self.src, p, self.cols(p)), _block(self.dst, p, self.cols(p)),
                                     self.local_sem)

    def _as_each_device(self, send, local, receive):
        for m in range(N_DEV):
            def branch(m=m):
                for k in range(1, N_DEV):
                    p = (m + k) % N_DEV
                    if self.cols(p) is not None:
                        send(self._remote(p, m))
                if self.cols(m) is not None:
                    if self.local_sem is not None:
                        local(self._local(m))
                    for k in range(1, N_DEV):
                        receive(self._remote(m, (m + k) % N_DEV))

            pl.when(self.me == m)(branch)

    def start(self):
        self._as_each_device(lambda cp: cp.start(), lambda cp: cp.start(), lambda cp: None)

    def finish(self):
        self._as_each_device(lambda cp: cp.wait_send(), lambda cp: cp.wait(), lambda cp: cp.wait_recv())


GRAD_EXCHANGE_SEMS = [pltpu.SemaphoreType.DMA((N_DEV,)), pltpu.SemaphoreType.DMA((N_DEV,)), pltpu.SemaphoreType.DMA]


def _allreduce_small(p_mid, p_conv, p_norm):
    def body(a_ref, b_ref, c_ref, out_ref, mine, gathered, send_sems, recv_sems):
        x, y, c, me = _my_place()
        mine[...] = a_ref[...] + b_ref[...] + c_ref[...]
        gathered[me] = mine[...]
        remote = []
        for k, (peer, _) in enumerate(_peers(x, y, c)):
            cp = pltpu.make_async_remote_copy(
                src_ref=mine, dst_ref=gathered.at[me], send_sem=send_sems.at[k], recv_sem=recv_sems.at[k],
                device_id=peer, device_id_type=MESH)
            cp.start()
            remote.append(cp)
        for cp in remote:
            cp.wait()
        total = gathered[0]
        for s in range(1, N_DEV):
            total = total + gathered[s]
        out_ref[...] = total

    return pl.pallas_call(
        body, name="allreduce_small",
        out_shape=jax.ShapeDtypeStruct((8, D), F32),
        in_specs=[VMEM_SPEC, VMEM_SPEC, VMEM_SPEC], out_specs=VMEM_SPEC,
        scratch_shapes=[pltpu.VMEM((8, D), F32), pltpu.VMEM((N_DEV, 8, D), F32),
                        pltpu.SemaphoreType.DMA((N_DEV - 1,)), pltpu.SemaphoreType.DMA((N_DEV - 1,))],
    )(p_mid, p_conv, p_norm)


def _proj_pieces():
    cuts = sorted(set(range(0, IN_COLS + 1, D)) | set(range(0, IN_COLS + 1, W_IN_SHARD)))
    return [(lo // D, lo % D, lo // W_IN_SHARD, lo % W_IN_SHARD, hi - lo) for lo, hi in zip(cuts[:-1], cuts[1:])]


PROJ_TN = 256


def _proj_cols(u, w_all, seg0, n_seg, dtype, name):
    S = u.shape[0]
    tn = PROJ_TN
    per_shard = W_IN_SHARD // tn
    tile0 = seg0 * D // tn

    def body(u_ref, w_ref, out_ref):
        out_ref[...] = _dot(u_ref[...], w_ref[0]).astype(dtype)

    return pl.pallas_call(
        body, name=name, grid=(n_seg * D // tn,),
        in_specs=[VMEM_SPEC, pl.BlockSpec((1, D, tn), lambda t: ((tile0 + t) // per_shard, 0, (tile0 + t) % per_shard))],
        out_specs=pl.BlockSpec((S, tn), lambda t: (0, t)),
        out_shape=jax.ShapeDtypeStruct((S, n_seg * D), dtype),
        compiler_params=_params(1),
    )(u, w_all)


CONV_TM, CONV_TC = 1024, 512
HALO = 16


def _conv_fwd(pa, cw8):
    S = pa.shape[0]
    tm, tc = CONV_TM, CONV_TC
    nct = D // tc

    def seg(s):
        return pl.BlockSpec((tm, tc), lambda i, j, s=s: (i, s * nct + j))

    def halo_before(s):
        return pl.BlockSpec((HALO, tc), lambda i, j, s=s: (jnp.maximum(i * (tm // HALO) - 1, 0), s * nct + j))

    def body(xc, bg, cg, zc, xch, cgh, cw, out):
        i = pl.program_id(0)
        a = cg[...].astype(F32) * xc[...].astype(F32)
        ah = cgh[...].astype(F32) * xch[...].astype(F32)
        ah = jnp.where(i > 0, ah, 0.0)
        row = lax.broadcasted_iota(jnp.int32, (tm, tc), 0)
        a1 = jnp.where(row == 0, ah[HALO - 1:HALO, :], pltpu.roll(a, 1, 0))
        a2 = jnp.where(row == 0, ah[HALO - 2:HALO - 1, :],
                       jnp.where(row == 1, ah[HALO - 1:HALO, :], pltpu.roll(a, 2, 0)))
        w = cw[...]
        conv = w[0:1, :] * a2 + w[1:2, :] * a1 + w[2:3, :] * a
        z = zc[...].astype(F32)
        out[...] = (z * _sigmoid(z) * bg[...].astype(F32) * conv).astype(BF)

    return pl.pallas_call(
        body, name="conv_fwd", grid=(S // tm, nct),
        in_specs=[seg(0), seg(1), seg(2), seg(3), halo_before(0), halo_before(2),
                  pl.BlockSpec((8, tc), lambda i, j: (0, j))],
        out_specs=pl.BlockSpec((tm, tc), lambda i, j: (i, j)),
        out_shape=jax.ShapeDtypeStruct((S, D), BF),
        compiler_params=_params(2),
    )(pa, pa, pa, pa, pa, pa, cw8)


ATT_UNROLL = 32


LAYOUT_MOD = 4
RUN = QB // LAYOUT_MOD


def _fold_masks(d):
    row = lax.broadcasted_iota(jnp.int32, (QB, QB), 0)
    lane = lax.broadcasted_iota(jnp.int32, (QB, QB), 1)
    if d == 1:
        qpos, kpos = LAYOUT_MOD * (row % RUN) + row // RUN, LAYOUT_MOD * (lane % RUN) + lane // RUN
    else:
        qpos, kpos = row, lane
    tri_le = kpos <= qpos
    dist = jnp.where(tri_le, qpos - kpos, qpos - kpos + QB).astype(F32)
    return tri_le, dist, lane < HEAD_DIM


class _Rows:
    def __init__(self, slices):
        self.slices = slices

    def get(self, ref):
        parts = [ref[sl, :] for sl in self.slices]
        return parts[0] if len(parts) == 1 else jnp.concatenate(parts, axis=0)

    def put(self, ref, val):
        size = QB // len(self.slices)
        for g, sl in enumerate(self.slices):
            ref[sl, :] = val if len(self.slices) == 1 else val[g * size:(g + 1) * size]

    def add(self, ref, val):
        self.put(ref, self.get(ref) + val)


def _block_rows(b, d, S):
    quarter = S // LAYOUT_MOD
    nb = S // (QB * d)
    r, n = b // nb, b % nb
    n_prev = jnp.maximum(n - 1, 0)
    if d == 1:
        runs = lambda m: _Rows([pl.ds(pl.multiple_of(g * quarter + RUN * m, RUN), RUN) for g in range(LAYOUT_MOD)])
        return n, runs(n), runs(n_prev)
    if d == LAYOUT_MOD:
        block = lambda m: _Rows([pl.ds(pl.multiple_of(r * quarter + QB * m, QB), QB)])
        return n, block(n), block(n_prev)
    step = d // LAYOUT_MOD
    first = (r % LAYOUT_MOD) * quarter + r // LAYOUT_MOD
    strided = lambda m: _Rows([pl.ds(first + QB * step * m, QB, stride=step)])
    return n, strided(n), strided(n_prev)


def _natural_rows(i, S):
    per = S // LAYOUT_MOD // QB
    return pl.ds(i // per + LAYOUT_MOD * QB * (i % per), QB, stride=LAYOUT_MOD)


def _head_sum_matrix():
    r = lax.broadcasted_iota(jnp.int32, (2 * QB, 2 * QB), 0)
    c = lax.broadcasted_iota(jnp.int32, (2 * QB, 2 * QB), 1)
    return (((r % QB) // HEAD_DIM) == (c // QB)).astype(F32).astype(BF)


def _hi_lo(t):
    hi = t.astype(BF)
    return jnp.concatenate([hi, (t - hi.astype(F32)).astype(BF)], axis=1)


PROJ_ROWS = 512


def _attn_fwd(u, slopes, w_all, w3_all, cw_all):
    S = u.shape[0]
    hpr = HEAD_PAIRS
    n_blocks = S // QB
    later = range(3, 3 * hpr)

    def body(sl_ref, u_ref, w_in_ref, w3_in_ref, cw_in_ref, o_ref, lse_ref, q_ref, k_ref, v_ref, w_ref, w3_ref,
             cw_ref, acc, m_s, l_s, w_tile, staged, tile_sems, *sems):
        hp = pl.program_id(0)
        me = _my_place()[3]
        pieces = _PieceGather(lambda p, lo, hi: w_ref.at[p, :, lo:hi], w_ref, *sems[0:2])
        gathers = (_WeightGather(lambda p, cols: _block(w_ref, me, cols), w_ref, *sems[2:5], REST_COLS),
                   _WeightGather(lambda p, cols: w3_ref.at[me], w3_ref, *sems[5:8], _whole),
                   _WeightGather(lambda p, cols: cw_ref.at[me], cw_ref, *sems[8:11], _whole))

        @pl.when(hp == 0)
        def _():
            pieces.start(later)
            for g in gathers:
                g.start()

        @pl.when(hp == FORWARD_STEP)
        def _():
            for g in gathers:
                g.forward()

        for h in range(hpr):
            @pl.when(hp == h)
            def _(h=h):
                if h > 0:
                    pieces.wait_recv(range(3 * h, 3 * h + 3))
                fetch = []
                for seg in range(3):
                    p, lo = _qkv_piece(h, seg)
                    fetch.append(pltpu.make_async_copy(w_ref.at[p, :, lo:lo + 128], w_tile.at[:, seg * 128:(seg + 1) * 128],
                                                       tile_sems.at[seg]))
                    fetch[-1].start()
                for cp in fetch:
                    cp.wait()

        def project(i, carry):
            rows = pl.ds(pl.multiple_of(i * PROJ_ROWS, PROJ_ROWS), PROJ_ROWS)
            qkv = _dot(u_ref[rows, :], w_tile[...])
            per = PROJ_ROWS // LAYOUT_MOD
            for seg, ref in enumerate((q_ref, k_ref, v_ref)):
                staged[seg] = qkv[:, seg * 128:(seg + 1) * 128]
                for g in range(LAYOUT_MOD):
                    dst = pl.ds(pl.multiple_of(g * (S // LAYOUT_MOD) + i * per, per), per)
                    ref[dst, :] = staged.at[seg][pl.ds(g, per, stride=LAYOUT_MOD), :]
            return carry

        lax.fori_loop(0, S // PROJ_ROWS, project, 0)

        head_sum = _head_sum_matrix()
        ones_b = jnp.ones((2 * QB, QB), BF)
        m_s[...] = jnp.full(m_s.shape, NEG, F32)
        l_s[...] = jnp.zeros(l_s.shape, F32)
        acc[...] = jnp.zeros(acc.shape, F32)

        for d in DILATIONS:
            tri_le, dist, low = _fold_masks(d)
            low_b = low.astype(F32).astype(BF)
            high_b = 1.0 - low_b
            slope = [sl_ref[2 * hp + a] * float(d) for a in range(2)]
            bias = [slope[a] * dist for a in range(2)]

            def block(b, d=d, slope=slope, bias=bias, tri_le=tri_le, low=low, low_b=low_b, high_b=high_b):
                n, cur, prev = _block_rows(b, d, S)
                has_prev = n > 0
                valid = jnp.logical_or(tri_le, has_prev)
                q2 = (cur.get(q_ref) * 0.125).astype(BF)
                qs = jnp.concatenate([q2 * low_b, q2 * high_b], axis=0)
                vp = prev.get(v_ref)
                kp_b = prev.get(k_ref).astype(BF)
                kcat = jnp.concatenate([kp_b, cur.get(k_ref).astype(BF)], axis=0)
                vcat = jnp.concatenate([vp, cur.get(v_ref)], axis=0).astype(BF)
                s2 = _dot_nt(qs, kcat)
                e2 = _dot(_hi_lo(q2.astype(F32) * kp_b.astype(F32)), head_sum)
                p_rows, alpha_h, pe_h = [], [], []
                for a in range(2):
                    sp, sc = s2[a * QB:(a + 1) * QB, :QB], s2[a * QB:(a + 1) * QB, QB:]
                    comb = jnp.where(valid, jnp.where(tri_le, sc, sp) - bias[a], NEG)
                    e = jnp.where(has_prev, e2[:, a * QB:(a + 1) * QB] - slope[a] * float(QB), NEG)
                    m_old = cur.get(m_s.at[a])
                    m_new = jnp.maximum(jnp.maximum(m_old, jnp.max(comb, axis=-1, keepdims=True)), e)
                    cur.put(m_s.at[a], m_new)
                    p = jnp.exp(comb - m_new)
                    pe_h.append(jnp.exp(e - m_new))
                    alpha_h.append(jnp.exp(m_old - m_new))
                    p_rows.append(jnp.concatenate([jnp.where(tri_le, 0.0, p).astype(BF),
                                                   jnp.where(tri_le, p, 0.0).astype(BF)], axis=1))
                pv = _dot(jnp.concatenate(p_rows, axis=0), jnp.concatenate([vcat, ones_b], axis=1))
                for a in range(2):
                    cur.put(l_s.at[a], alpha_h[a] * cur.get(l_s.at[a]) + pv[a * QB:(a + 1) * QB, QB:] + pe_h[a])
                cur.put(acc, jnp.where(low, alpha_h[0], alpha_h[1]) * cur.get(acc)
                        + jnp.where(low, pv[:QB, :QB], pv[QB:, :QB]) + jnp.where(low, pe_h[0], pe_h[1]) * vp)

            def several(it, carry, block=block):
                for u in range(ATT_UNROLL):
                    block(it * ATT_UNROLL + u)
                return carry

            lax.fori_loop(0, n_blocks // ATT_UNROLL, several, 0)

        low = _fold_masks(LAYOUT_MOD)[2]

        def finish(i, carry):
            rows = pl.ds(pl.multiple_of(i * QB, QB), QB)
            l0, l1 = l_s[0, rows, :], l_s[1, rows, :]
            o_ref[_natural_rows(i, S), :] = acc[rows, :] / jnp.where(low, l0, l1)
            lse_ref[0, rows, :] = m_s[0, rows, :] + jnp.log(l0)
            lse_ref[1, rows, :] = m_s[1, rows, :] + jnp.log(l1)
            return carry

        lax.fori_loop(0, n_blocks, finish, 0)

        @pl.when(hp == hpr - 1)
        def _():
            pieces.wait_send(later)
            for g in gathers:
                g.finish()

    col = pl.BlockSpec((S, 128), lambda h: (0, h))
    act = jax.ShapeDtypeStruct((S, D), F32)
    gathered = (w_all, w3_all, cw_all)
    return pl.pallas_call(
        body, name="attn_fwd", grid=(hpr,),
        in_specs=[SMEM_SPEC, VMEM_SPEC, ANY_SPEC, ANY_SPEC, ANY_SPEC],
        out_specs=(col, pl.BlockSpec((2, S, 128), lambda h: (0, 0, h)), col, col, col, ANY_SPEC, ANY_SPEC, ANY_SPEC),
        out_shape=(act, jax.ShapeDtypeStruct((2, S, D), F32), act, act, act,
                   *[jax.ShapeDtypeStruct(t.shape, t.dtype) for t in gathered]),
        scratch_shapes=([pltpu.VMEM((S, 128), F32), pltpu.VMEM((2, S, 128), F32), pltpu.VMEM((2, S, 128), F32),
                         pltpu.VMEM((D, 3 * 128), BF), pltpu.VMEM((3, PROJ_ROWS, 128), F32),
                         pltpu.SemaphoreType.DMA((3,))]
                        + _piece_sems(3 * hpr) + WEIGHT_GATHER_SEMS * 3),
        input_output_aliases={2: 5, 3: 6, 4: 7},
        compiler_params=_params(1),
    )(slopes, u, *gathered)


def _set_rows(shape, rows):
    idx = lax.broadcasted_iota(jnp.int32, shape, 0)
    out = jnp.zeros(shape, F32)
    for r, val in rows.items():
        out = out + jnp.where(idx == r, val, 0.0)
    return out


def _mid(yc_in, pa_mid, o, x2, target, b_merge, final_g, w3):
    S = x2.shape[0]
    tm = ROW_TILE
    nsteps = S // tm
    tile = pl.BlockSpec((tm, D), lambda i: (i, 0))

    def body(yc_ref, za_ref, gcp_ref, gap_ref, o_ref, x_ref, t_ref, b_ref, fg_ref, w_ref,
             dh_ref, dmid_ref, do_ref, dyc_ref, gw_ref, small_ref, acc, stage):
        i = pl.program_id(0)

        @pl.when(i == 0)
        def _():
            acc[...] = jnp.zeros_like(acc)
            small_ref[...] = jnp.zeros_like(small_ref)

        wc, wa, wo = w_ref[0], w_ref[1], w_ref[2]
        z = za_ref[...].astype(F32)
        sg = _sigmoid(z)
        ov = o_ref[...]
        yc_in_b, ya_in_b = yc_ref[...], (z * sg * ov).astype(BF)
        yc = _dot(yc_in_b, wc)
        ya = _dot(ya_in_b, wa)
        b = b_ref[...]
        gc = _sigmoid(gcp_ref[...].astype(F32) + b[:, :D])
        ga = _sigmoid(gap_ref[...].astype(F32) + b[:, D:])
        merged = gc * yc + ga * ya
        merged_b = merged.astype(BF)
        h = x_ref[...] + _dot(merged_b, wo)
        r2 = lax.rsqrt(jnp.mean(h * h, axis=-1, keepdims=True) + EPS)
        n = h * r2
        fg = fg_ref[...]
        err = n * fg - t_ref[...]
        loss = 0.5 * jnp.sum(jnp.sum(err * err, axis=-1, keepdims=True) / D, axis=0, keepdims=True)
        dy = err / D
        g_fg = jnp.sum(dy * n, axis=0, keepdims=True)
        dn = dy * fg
        dh = r2 * (dn - n * jnp.mean(dn * n, axis=-1, keepdims=True))
        dh_ref[...] = dh
        dh_b = dh.astype(BF)
        dmerged = _dot_nt(dh_b, wo)
        acc[2] += _dot(merged.T.astype(BF), dh_b)
        dyc = (dmerged * gc).astype(BF)
        dya = (dmerged * ga).astype(BF)
        dgcp = dmerged * yc * gc * (1.0 - gc)
        dgap = dmerged * ya * ga * (1.0 - ga)
        dmid_ref[1] = dgcp.astype(BF)
        dmid_ref[2] = dgap.astype(BF)
        acc[0] += _dot(yc_in_b.astype(F32).T.astype(BF), dyc)
        acc[1] += _dot(ya_in_b.astype(F32).T.astype(BF), dya)
        dyc_ref[...] = _dot_nt(dyc, wc).astype(BF)
        dya_in = _dot_nt(dya, wa)
        do_ref[...] = dya_in * (z * sg)
        dmid_ref[0] = (dya_in * ov * (sg * (1.0 + z * (1.0 - sg)))).astype(BF)
        small_ref[...] += _set_rows((8, D), {
            1: jnp.sum(dgcp, axis=0, keepdims=True), 2: jnp.sum(dgap, axis=0, keepdims=True),
            3: g_fg, 7: jnp.broadcast_to(loss, (1, D))})

        @pl.when(i == nsteps - 1)
        def _():
            for p in range(N_DEV):
                for a in range(3):
                    stage[...] = acc[a, p * ROW_SHARD:(p + 1) * ROW_SHARD, :].astype(BF)
                    pltpu.sync_copy(stage, gw_ref.at[p, a])

    return pl.pallas_call(
        body, name="mid", grid=(nsteps,),
        in_specs=[tile, pl.BlockSpec((tm, D), lambda i: (i, 0)), pl.BlockSpec((tm, D), lambda i: (i, 1)),
                  pl.BlockSpec((tm, D), lambda i: (i, 2)), tile, tile, tile,
                  pl.BlockSpec((1, 2 * D), lambda i: (0, 0)), pl.BlockSpec((1, D), lambda i: (0, 0)), VMEM_SPEC],
        out_specs=(tile, pl.BlockSpec((3, tm, D), lambda i: (0, i, 0)), tile, tile,
                   ANY_SPEC, pl.BlockSpec((8, D), lambda i: (0, 0))),
        out_shape=(jax.ShapeDtypeStruct((S, D), F32), jax.ShapeDtypeStruct((3, S, D), BF),
                   jax.ShapeDtypeStruct((S, D), F32), jax.ShapeDtypeStruct((S, D), BF),
                   jax.ShapeDtypeStruct((N_DEV, 3, ROW_SHARD, D), BF), jax.ShapeDtypeStruct((8, D), F32)),
        scratch_shapes=[pltpu.VMEM((3, D, D), F32), pltpu.VMEM((ROW_SHARD, D), BF)],
        compiler_params=_params(1),
    )(yc_in, pa_mid, pa_mid, pa_mid, o, x2, target, b_merge, final_g, w3)


def _conv_bwd(dyc_in, pa, cw8):
    S = pa.shape[0]
    tm, tc = CONV_TM, CONV_TC
    nct = D // tc
    nrt = S // tm
    last_halo = S // HALO - 1

    def seg(s):
        return pl.BlockSpec((tm, tc), lambda j, i, s=s: (i, s * nct + j))

    def halo_before(s):
        return pl.BlockSpec((HALO, tc), lambda j, i, s=s: (jnp.maximum(i * (tm // HALO) - 1, 0), s * nct + j))

    def halo_after(s):
        return pl.BlockSpec((HALO, tc), lambda j, i, s=s: (jnp.minimum((i + 1) * (tm // HALO), last_halo), s * nct + j))

    def body(dy, xc, bg, cg, zc, xch, cgh, dyn, bgn, zcn, cw, dout, gcw):
        i = pl.program_id(1)

        @pl.when(i == 0)
        def _():
            gcw[...] = jnp.zeros_like(gcw)

        xcv, cgv = xc[...].astype(F32), cg[...].astype(F32)
        a = cgv * xcv
        ah = jnp.where(i > 0, cgh[...].astype(F32) * xch[...].astype(F32), 0.0)
        row = lax.broadcasted_iota(jnp.int32, (tm, tc), 0)
        a1 = jnp.where(row == 0, ah[HALO - 1:HALO, :], pltpu.roll(a, 1, 0))
        a2 = jnp.where(row == 0, ah[HALO - 2:HALO - 1, :],
                       jnp.where(row == 1, ah[HALO - 1:HALO, :], pltpu.roll(a, 2, 0)))
        w = cw[...]
        conv = w[0:1, :] * a2 + w[1:2, :] * a1 + w[2:3, :] * a
        z = zc[...].astype(F32)
        sg = _sigmoid(z)
        silu = z * sg
        bgv = bg[...].astype(F32)
        dyv = dy[...].astype(F32)
        dout[3] = (dyv * bgv * conv * (sg * (1.0 + z * (1.0 - sg)))).astype(BF)
        dout[1] = (dyv * silu * conv).astype(BF)
        dc = dyv * silu * bgv
        zn = zcn[...].astype(F32)
        dcn = dyn[...].astype(F32) * (zn * _sigmoid(zn)) * bgn[...].astype(F32)
        dcn = jnp.where(i < nrt - 1, dcn, 0.0)
        dc1 = jnp.where(row == tm - 1, dcn[0:1, :], pltpu.roll(dc, tm - 1, 0))
        dc2 = jnp.where(row == tm - 1, dcn[1:2, :],
                        jnp.where(row == tm - 2, dcn[0:1, :], pltpu.roll(dc, tm - 2, 0)))
        da = w[2:3, :] * dc + w[1:2, :] * dc1 + w[0:1, :] * dc2
        dout[2] = (da * xcv).astype(BF)
        dout[0] = (da * cgv).astype(BF)
        gcw[...] += _set_rows((8, tc), {
            4: jnp.sum(dc * a2, axis=0, keepdims=True), 5: jnp.sum(dc * a1, axis=0, keepdims=True),
            6: jnp.sum(dc * a, axis=0, keepdims=True)})

    return pl.pallas_call(
        body, name="conv_bwd", grid=(nct, nrt),
        in_specs=[pl.BlockSpec((tm, tc), lambda j, i: (i, j)), seg(0), seg(1), seg(2), seg(3),
                  halo_before(0), halo_before(2),
                  pl.BlockSpec((HALO, tc), lambda j, i: (jnp.minimum((i + 1) * (tm // HALO), last_halo), j)),
                  halo_after(1), halo_after(3), pl.BlockSpec((8, tc), lambda j, i: (0, j))],
        out_specs=(pl.BlockSpec((4, tm, tc), lambda j, i: (0, i, j)), pl.BlockSpec((8, tc), lambda j, i: (0, j))),
        out_shape=(jax.ShapeDtypeStruct((4, S, D), BF), jax.ShapeDtypeStruct((8, D), F32)),
        compiler_params=_params(2),
    )(dyc_in, pa, pa, pa, pa, pa, pa, dyc_in, pa, pa, cw8)


def _attn_bwd(q, k, v, slopes, do, o, lse, g_in, g_3):
    S = q.shape[0]
    hpr = HEAD_PAIRS
    n_blocks = S // QB

    def body(sl_ref, q_ref, k_ref, v_ref, do_ref, o_ref, lse_ref, gin_ref, g3_ref, out_ref, rin_ref, r3_ref,
             dq_s, dk_s, dv_s, do_s, dd_s, *sems):
        hp = pl.program_id(0)
        exchanges = (_GradExchange(gin_ref, rin_ref, *sems[:3], _shard_cols((0, SEG0_ATTN * D), (SEG0_MID * D, IN_COLS))),
                     _GradExchange(g3_ref, r3_ref, *sems[3:], _whole))

        @pl.when(hp == 0)
        def _():
            for ex in exchanges:
                ex.start()

        head_sum = _head_sum_matrix()
        dq_s[...] = jnp.zeros(dq_s.shape, F32)
        dk_s[...] = jnp.zeros(dk_s.shape, F32)
        dv_s[...] = jnp.zeros(dv_s.shape, F32)

        def row_dots(i, carry):
            rows = pl.ds(pl.multiple_of(i * QB, QB), QB)
            natural = _natural_rows(i, S)
            do_c = do_ref[natural, :]
            do_s[rows, :] = do_c
            dd = _dot(_hi_lo(do_c * o_ref[natural, :]), head_sum)
            dd_s[0, rows, :] = dd[:, :QB]
            dd_s[1, rows, :] = dd[:, QB:]
            return carry

        lax.fori_loop(0, n_blocks, row_dots, 0)

        for d in DILATIONS:
            tri_le, dist, low = _fold_masks(d)
            low_b = low.astype(F32).astype(BF)
            high_b = 1.0 - low_b
            slope = [sl_ref[2 * hp + a] * float(d) for a in range(2)]
            bias = [slope[a] * dist for a in range(2)]

            def block(b, d=d, slope=slope, bias=bias, tri_le=tri_le, low=low, low_b=low_b, high_b=high_b):
                n, cur, prev = _block_rows(b, d, S)
                has_prev = n > 0
                valid = jnp.logical_or(tri_le, has_prev)
                q2f = cur.get(q_ref) * 0.125
                q2 = q2f.astype(BF)
                qs = jnp.concatenate([q2 * low_b, q2 * high_b], axis=0)
                kp, vp = prev.get(k_ref), prev.get(v_ref)
                kp_b, vp_b = kp.astype(BF), vp.astype(BF)
                kcat = jnp.concatenate([kp_b, cur.get(k_ref).astype(BF)], axis=0)
                vcat = jnp.concatenate([vp_b, cur.get(v_ref).astype(BF)], axis=0)
                do2f = cur.get(do_s)
                do2 = do2f.astype(BF)
                dos = jnp.concatenate([do2 * low_b, do2 * high_b], axis=0)
                s2 = _dot_nt(qs, kcat)
                dp2 = _dot_nt(dos, vcat)
                diag2 = _dot(jnp.concatenate([_hi_lo(q2.astype(F32) * kp_b.astype(F32)),
                                              _hi_lo(do2.astype(F32) * vp_b.astype(F32))], axis=0), head_sum)
                p_rows, ds_rows, pe_h, dse_h = [], [], [], []
                for a in range(2):
                    hs = slice(a * QB, (a + 1) * QB)
                    sp, sc = s2[hs, :QB], s2[hs, QB:]
                    dpp, dpc = dp2[hs, :QB], dp2[hs, QB:]
                    lse_a, dd_a = cur.get(lse_ref.at[a]), cur.get(dd_s.at[a])
                    comb = jnp.where(tri_le, sc, sp) - bias[a]
                    e = diag2[:QB, hs] - slope[a] * float(QB)
                    p = jnp.where(valid, jnp.exp(comb - lse_a), 0.0)
                    pe = jnp.where(has_prev, jnp.exp(e - lse_a), 0.0)
                    ds = p * (jnp.where(tri_le, dpc, dpp) - dd_a)
                    dse_h.append(pe * (diag2[QB:, hs] - dd_a))
                    pe_h.append(pe)
                    p_rows.append(jnp.concatenate([jnp.where(tri_le, 0.0, p).astype(BF),
                                                   jnp.where(tri_le, p, 0.0).astype(BF)], axis=1))
                    ds_rows.append(jnp.concatenate([jnp.where(tri_le, 0.0, ds).astype(BF),
                                                    jnp.where(tri_le, ds, 0.0).astype(BF)], axis=1))
                pst = jnp.concatenate(p_rows, axis=0)
                dst = jnp.concatenate(ds_rows, axis=0)
                pe2 = jnp.where(low, pe_h[0], pe_h[1])
                dse2 = jnp.where(low, dse_h[0], dse_h[1])
                dq = _dot(dst, kcat)
                cur.add(dq_s, (jnp.where(low, dq[:QB], dq[QB:]) + dse2 * kp) * 0.125)
                dk = _dot_tn(dst, qs)
                dv = _dot_tn(pst, dos)
                prev.add(dk_s, dk[:QB] + dse2 * q2f)
                cur.add(dk_s, dk[QB:])
                prev.add(dv_s, dv[:QB] + pe2 * do2f)
                cur.add(dv_s, dv[QB:])

            def several(it, carry, block=block):
                for u in range(ATT_UNROLL):
                    block(it * ATT_UNROLL + u)
                return carry

            lax.fori_loop(0, n_blocks // ATT_UNROLL, several, 0)

        def finish(i, carry):
            rows = pl.ds(pl.multiple_of(i * QB, QB), QB)
            natural = _natural_rows(i, S)
            for t, ref in enumerate((dq_s, dk_s, dv_s)):
                out_ref.at[t][natural, :] = ref[rows, :]
            return carry

        lax.fori_loop(0, n_blocks, finish, 0)

        @pl.when(hp == hpr - 1)
        def _():
            for ex in exchanges:
                ex.finish()

    col = pl.BlockSpec((S, 128), lambda h: (0, h))
    return pl.pallas_call(
        body, name="attn_bwd", grid=(hpr,),
        in_specs=[SMEM_SPEC, col, col, col, col, col, pl.BlockSpec((2, S, 128), lambda h: (0, 0, h)),
                  ANY_SPEC, ANY_SPEC],
        out_specs=(pl.BlockSpec((3, S, 128), lambda h: (0, 0, h)), ANY_SPEC, ANY_SPEC),
        out_shape=(jax.ShapeDtypeStruct((3, S, D), F32), jax.ShapeDtypeStruct(g_in.shape, BF),
                   jax.ShapeDtypeStruct(g_3.shape, BF)),
        scratch_shapes=([pltpu.VMEM((S, 128), F32)] * 4 + [pltpu.VMEM((2, S, 128), F32)]
                        + GRAD_EXCHANGE_SEMS + GRAD_EXCHANGE_SEMS),
        compiler_params=_params(1),
    )(slopes, q, k, v, do, o, lse, g_in, g_3)


WG_TN = 256
SEG0_CONV, SEG0_ATTN, SEG0_MID = 0, 4, 7


def _wgrad_in(ut, d_group, seg0, g_in, name):
    S = ut.shape[1]
    tn = WG_TN
    per_seg = D // tn
    per_shard = W_IN_SHARD // tn
    n_tiles = d_group.shape[0] * per_seg
    tile0 = seg0 * per_seg

    def body(ut_ref, d_ref, *rest):
        rest[-1][0] = _dot(ut_ref[...], d_ref[0].astype(BF)).astype(BF)

    operands, in_specs, aliases = [ut, d_group], [VMEM_SPEC, pl.BlockSpec((1, S, tn), lambda t: (t // per_seg, 0, t % per_seg))], {}
    if g_in is not None:
        operands.append(g_in)
        in_specs.append(ANY_SPEC)
        aliases = {2: 0}
    return pl.pallas_call(
        body, name=name, grid=(n_tiles,), in_specs=in_specs,
        out_specs=pl.BlockSpec((1, D, tn), lambda t: ((tile0 + t) // per_shard, 0, (tile0 + t) % per_shard)),
        out_shape=jax.ShapeDtypeStruct((N_DEV, D, W_IN_SHARD), BF),
        input_output_aliases=aliases,
        compiler_params=_params(1),
    )(*operands)


def _dgrad_norm_bwd(d_conv, d_attn, d_mid, w_all, x2, dh, norm_g):
    S = x2.shape[0]
    tm = ROW_TILE
    nsteps = S // tm
    tile = pl.BlockSpec((tm, D), lambda i: (i, 0))
    pieces = _proj_pieces()

    def body(a_ref, b_ref, c_ref, w_ref, x_ref, dh_ref, g_ref, gx_ref, small_ref):
        i = pl.program_id(0)

        @pl.when(i == 0)
        def _():
            small_ref[...] = jnp.zeros_like(small_ref)

        groups = (a_ref, b_ref, c_ref)
        du = jnp.zeros((tm, D), F32)
        for s, sc, p, pc, width in pieces:
            g = 0 if s < 4 else (1 if s < 7 else 2)
            local = s - (0, 4, 7)[g]
            du = du + _dot_nt(groups[g][local, :, sc:sc + width].astype(BF), w_ref[p, :, pc:pc + width])
        xv = x_ref[...]
        r = lax.rsqrt(jnp.mean(xv * xv, axis=-1, keepdims=True) + EPS)
        n = xv * r
        dn = du * g_ref[...]
        gx_ref[...] = dh_ref[...] + r * (dn - n * jnp.mean(dn * n, axis=-1, keepdims=True))
        small_ref[...] += _set_rows((8, D), {0: jnp.sum(du * n, axis=0, keepdims=True)})

    return pl.pallas_call(
        body, name="dgrad_norm_bwd", grid=(nsteps,),
        in_specs=[pl.BlockSpec((4, tm, D), lambda i: (0, i, 0)), pl.BlockSpec((3, tm, D), lambda i: (0, i, 0)),
                  pl.BlockSpec((3, tm, D), lambda i: (0, i, 0)), VMEM_SPEC, tile, tile,
                  pl.BlockSpec((1, D), lambda i: (0, 0))],
        out_specs=(tile, pl.BlockSpec((8, D), lambda i: (0, 0))),
        out_shape=(jax.ShapeDtypeStruct((S, D), F32), jax.ShapeDtypeStruct((8, D), F32)),
        compiler_params=_params(1),
    )(d_conv, d_attn, d_mid, w_all, x2, dh, norm_g)


HBM_SPEC = pl.BlockSpec(memory_space=pltpu.HBM)
SEM_SPEC = pl.BlockSpec(memory_space=pltpu.SEMAPHORE)
ATTN_COLS = _shard_cols((SEG0_ATTN * D, SEG0_MID * D))


def _attn_cols_exchange_start(g_in, r_in):
    def body(g_ref, r_ref, send_sems, recv_sems, g_thru, r_thru, token):
        _GradExchange(g_ref, r_ref, send_sems, recv_sems, None, ATTN_COLS).start()
        token[...] = jnp.zeros_like(token)

    hbm = pltpu.with_memory_space_constraint
    return pl.pallas_call(
        body, name="attn_cols_exchange_start",
        out_shape=(pltpu.SemaphoreType.DMA((N_DEV,)), pltpu.SemaphoreType.DMA((N_DEV,)),
                   pltpu.HBM(g_in.shape, g_in.dtype), pltpu.HBM(r_in.shape, r_in.dtype),
                   jax.ShapeDtypeStruct((8, 128), F32)),
        in_specs=(HBM_SPEC, HBM_SPEC), out_specs=(SEM_SPEC, SEM_SPEC, HBM_SPEC, HBM_SPEC, VMEM_SPEC),
        input_output_aliases={0: 2, 1: 3},
        compiler_params=pltpu.CompilerParams(has_side_effects=pltpu.SideEffectType.DATAFLOW_SIDE_EFFECTING),
    )(hbm(g_in, pltpu.HBM), hbm(r_in, pltpu.HBM))


def _attn_cols_exchange_wait(send_sems, recv_sems, g_thru, r_thru, after):
    def body(g_ref, r_ref, send_sems, recv_sems, after_ref, g_dead, r_out):
        _GradExchange(g_ref, r_ref, send_sems, recv_sems, None, ATTN_COLS).finish()

    return pl.pallas_call(
        body, name="attn_cols_exchange_wait",
        out_shape=(pltpu.HBM(g_thru.shape, g_thru.dtype), pltpu.HBM(r_thru.shape, r_thru.dtype)),
        in_specs=(HBM_SPEC, HBM_SPEC, SEM_SPEC, SEM_SPEC, ANY_SPEC), out_specs=(HBM_SPEC, HBM_SPEC),
        input_output_aliases={0: 0, 1: 1},
        compiler_params=pltpu.CompilerParams(has_side_effects=pltpu.SideEffectType.DATAFLOW_SIDE_EFFECTING),
    )(g_thru, r_thru, send_sems, recv_sems, after)


def _adamw_math(w, g, m, v):
    m = ADAM_B1 * m + (1.0 - ADAM_B1) * g
    v = ADAM_B2 * v + (1.0 - ADAM_B2) * (g * g)
    m_hat = m / (1.0 - ADAM_B1 ** ADAM_STEP)
    v_hat = v / (1.0 - ADAM_B2 ** ADAM_STEP)
    delta = -ADAM_LR * (m_hat / (jnp.sqrt(v_hat) + ADAM_EPS) + ADAM_WD * w)
    return delta, m, v


def _sum_adamw(parts, w, m, v, tm, name):
    R, C = w.shape
    tile = pl.BlockSpec((tm, C), lambda i: (i, 0))

    def body(p_ref, w_ref, m_ref, v_ref, g_out, d_out, m_out, v_out):
        g = p_ref[0].astype(F32)
        for s in range(1, N_DEV):
            g = g + p_ref[s].astype(F32)
        g_out[...] = g
        d_out[...], m_out[...], v_out[...] = _adamw_math(w_ref[...], g, m_ref[...], v_ref[...])

    shape = jax.ShapeDtypeStruct((R, C), F32)
    return pl.pallas_call(
        body, name=name, grid=(R // tm,),
        in_specs=[pl.BlockSpec((N_DEV, tm, C), lambda i: (0, i, 0)), tile, tile, tile],
        out_specs=(tile, tile, tile, tile), out_shape=(shape, shape, shape, shape),
        compiler_params=_params(1),
    )(parts, w, m, v)


def _adamw(g, w, m, v, name):
    def body(g_ref, w_ref, m_ref, v_ref, d_out, m_out, v_out):
        d_out[...], m_out[...], v_out[...] = _adamw_math(w_ref[...], g_ref[...], m_ref[...], v_ref[...])

    shape = jax.ShapeDtypeStruct(w.shape, F32)
    return pl.pallas_call(
        body, name=name, in_specs=[VMEM_SPEC] * 4, out_specs=(VMEM_SPEC,) * 3, out_shape=(shape, shape, shape),
    )(g, w, m, v)


def _alibi_slopes():
    return jnp.exp2(-8.0 * jnp.arange(1, N_HEADS + 1, dtype=F32) / N_HEADS)


def _local_step(x2, target, norm_g, b_merge, final_g, w_in, w3_shard, cw_shard):
    slopes = _alibi_slopes()
    u, ut, w_all, w3_all, cw_all = _norm_gather_first_weights(x2, norm_g, w_in, w3_shard, cw_shard)
    o, lse, q, k, v, w_all, w3_all, cw_all = _attn_fwd(u, slopes, w_all, w3_all, cw_all)
    w3 = jnp.transpose(w3_all, (1, 0, 2, 3)).reshape(3, D, D)
    cw8 = jnp.transpose(cw_all, (1, 0, 2)).reshape(8, D)
    pa = _proj_cols(u, w_all, SEG0_CONV, 4, BF, "proj_conv")
    yc_in = _conv_fwd(pa, cw8)
    pa_mid = _proj_cols(u, w_all, SEG0_MID, 3, BF, "proj_mid")
    dh, d_mid, do, dyc_in, g_3, small_mid = _mid(yc_in, pa_mid, o, x2, target, b_merge, final_g, w3)
    g_in = _wgrad_in(ut, d_mid, SEG0_MID, None, "wgrad_in_mid")
    d_conv, small_conv = _conv_bwd(dyc_in, pa, cw8)
    g_in = _wgrad_in(ut, d_conv, SEG0_CONV, g_in, "wgrad_in_conv")
    d_attn, r_in, r_3 = _attn_bwd(q, k, v, slopes, do, o, lse, g_in, g_3)
    g_in = _wgrad_in(ut, d_attn, SEG0_ATTN, g_in, "wgrad_in_attn")
    *in_flight, token = _attn_cols_exchange_start(g_in, r_in)
    grad_x, small_norm = _dgrad_norm_bwd(d_conv, d_attn, d_mid, w_all, x2, dh, norm_g + token[0:1, 0:1])
    return grad_x, in_flight, r_3, small_mid, small_conv, small_norm


def kernel(x, norm_g, w_in, b_merge, conv_w, w_out_conv, w_out_attn, w_o, final_g, loss_target, m_norm_g, m_w_in, m_b_merge, m_conv_w, m_w_out_conv, m_w_out_attn, m_w_o, m_final_g, v_norm_g, v_w_in, v_b_merge, v_conv_w, v_w_out_conv, v_w_out_attn, v_w_o, v_final_g):
    me = 4 * lax.axis_index("x") + 2 * lax.axis_index("y") + lax.axis_index("c")
    stack3 = lambda a, b, c: jnp.concatenate([a, b, c], axis=0)
    pad8 = lambda a: jnp.pad(a, ((0, 8 - a.shape[0]), (0, 0)))

    w3_shard = stack3(w_out_conv, w_out_attn, w_o)
    final_g2 = final_g.reshape(1, D)
    grad_x, in_flight, r_3, small_mid, small_conv, small_norm = _local_step(
        x[0], loss_target[0], norm_g, b_merge, final_g2, w_in[0], w3_shard, pad8(conv_w[0]))

    small = _allreduce_small(small_mid, small_conv, small_norm)
    g_in, r_in = _attn_cols_exchange_wait(*in_flight, small)
    own = lax.dynamic_index_in_dim(g_in, me, 0, keepdims=True)
    r_in = lax.dynamic_update_slice(r_in, own, (me, 0, 0))

    g_w_in, d_w_in, nm_w_in, nv_w_in = _sum_adamw(r_in, w_in[0], m_w_in[0], v_w_in[0], 256, "adamw_w_in")
    g_w3, d_w3, nm_w3, nv_w3 = _sum_adamw(
        r_3.reshape(N_DEV, 3 * ROW_SHARD, D), w3_shard.reshape(3 * ROW_SHARD, D),
        stack3(m_w_out_conv, m_w_out_attn, m_w_o).reshape(3 * ROW_SHARD, D),
        stack3(v_w_out_conv, v_w_out_attn, v_w_o).reshape(3 * ROW_SHARD, D), ROW_SHARD, "adamw_w3")

    def pack(ng, bm, fg):
        return pad8(jnp.concatenate([ng, bm.reshape(2, D), fg.reshape(1, D)], axis=0))

    d_s, nm_s, nv_s = _adamw(small, pack(norm_g, b_merge, final_g), pack(m_norm_g, m_b_merge, m_final_g),
                             pack(v_norm_g, v_b_merge, v_final_g), "adamw_small")
    g_cw = lax.dynamic_slice(small, (4, me * ROW_SHARD), (3, ROW_SHARD))
    d_cw, nm_cw, nv_cw = _adamw(g_cw, conv_w[0], m_conv_w[0], v_conv_w[0], "adamw_conv_w")

    loss = small[7, 0]
    split3 = lambda t: tuple(t[a * ROW_SHARD:(a + 1) * ROW_SHARD][None] for a in range(3))
    unpack = lambda t: (t[0:1], t[1:3].reshape(1, 2 * D), t[3])

    def leaves(in_, small_, cw_, w3_):
        ng, bm, fg = unpack(small_)
        wc, wa, wo = split3(w3_)
        return (ng, in_[None], bm, cw_[None], wc, wa, wo, fg)

    return (loss, grad_x[None],
            *leaves(g_w_in, small, g_cw, g_w3),
            *leaves(d_w_in, d_s, d_cw, d_w3),
            *leaves(nm_w_in, nm_s, nm_cw, nm_w3),
            *leaves(nv_w_in, nv_s, nv_cw, nv_w3))
```

```python
import functools

import jax
import jax.numpy as jnp
from jax import lax
from jax.experimental import pallas as pl
from jax.experimental.pallas import tpu as pltpu

D = 1024
N_HEADS = 16
HEAD_DIM = 64
N_SEG = 10
IN_COLS = N_SEG * D
N_DEV = 8
W_IN_SHARD = IN_COLS // N_DEV
ROW_SHARD = D // N_DEV
QB = 128
DILATIONS = (1, 4, 16)
EPS = 1e-6
NEG = -1e30
BF = jnp.bfloat16
F32 = jnp.float32
MESH = pl.DeviceIdType.MESH

ADAM_LR = 0.001
ADAM_B1 = 0.9
ADAM_B2 = 0.999
ADAM_EPS = 1e-08
ADAM_WD = 0.01
ADAM_STEP = 10

V7X_VMEM_BYTES = 64 * 1024 * 1024
VMEM_LIMIT = V7X_VMEM_BYTES - 8 * 1024 * 1024
ROW_TILE = 256

VMEM_SPEC = pl.BlockSpec(memory_space=pltpu.VMEM)
ANY_SPEC = pl.BlockSpec(memory_space=pl.ANY)
SMEM_SPEC = pl.BlockSpec(memory_space=pltpu.SMEM)


def _params(n_grid_axes, vmem=VMEM_LIMIT):
    return pltpu.CompilerParams(dimension_semantics=("arbitrary",) * n_grid_axes, vmem_limit_bytes=vmem)


def _dot(a, b):
    return jnp.dot(a, b, preferred_element_type=F32)


def _dot_nt(a, b):
    return lax.dot_general(a, b, (((1,), (1,)), ((), ())), preferred_element_type=F32)


def _dot_tn(a, b):
    return lax.dot_general(a, b, (((0,), (0,)), ((), ())), preferred_element_type=F32)


def _sigmoid(z):
    return 1.0 / (1.0 + jnp.exp(-z))


def _my_place():
    x, y, c = lax.axis_index("x"), lax.axis_index("y"), lax.axis_index("c")
    return x, y, c, 4 * x + 2 * y + c


def _peers(x, y, c):
    out = []
    for k in range(1, N_DEV):
        px = 1 - x if k & 4 else x
        py = 1 - y if k & 2 else y
        pc = 1 - c if k & 1 else c
        out.append(((px, py, pc), 4 * px + 2 * py + pc))
    return out


def _device(p):
    return (p >> 2, (p >> 1) & 1, p & 1)


def _shard_cols(*ranges):
    def cols(p):
        found = None
        for lo, hi in ranges:
            a, b = max(lo, p * W_IN_SHARD), min(hi, (p + 1) * W_IN_SHARD)
            if a < b:
                assert found is None
                found = (a - p * W_IN_SHARD, b - p * W_IN_SHARD)
        return found

    return cols


def _whole(p):
    return ()


def _block(ref, idx, cols):
    return ref.at[idx] if cols == () else ref.at[idx, :, cols[0]:cols[1]]


class _WeightGather:
    def __init__(self, src, dst, send_sems, forward_sems, recv_sems, cols):
        self.src, self.dst, self.cols = src, dst, cols
        self.send_sems, self.forward_sems, self.recv_sems = send_sems, forward_sems, recv_sems
        self.me = _my_place()[3]

    def _copy(self, p, target, passing_on=False):
        cols = self.cols(p)
        return pltpu.make_async_remote_copy(
            src_ref=_block(self.dst, p, cols) if passing_on else self.src(p, cols), dst_ref=_block(self.dst, p, cols),
            send_sem=self.forward_sems.at[p] if passing_on else self.send_sems.at[target],
            recv_sem=self.recv_sems.at[p], device_id=_device(target), device_id_type=MESH)

    def _as_each_device(self, own, relayed, other):
        for m in range(N_DEV):
            def branch(m=m):
                for p in range(N_DEV):
                    if self.cols(p) is None:
                        continue
                    if p == m:
                        for t in [m ^ 1] + [q for q in range(N_DEV) if q >> 1 != m >> 1 and q & 1 == m & 1]:
                            own(self._copy(m, t))
                    elif p >> 1 != m >> 1 and p & 1 == m & 1:
                        relayed(p, m ^ 1)
                    else:
                        other(p)

            pl.when(self.me == m)(branch)

    def start(self):
        self._as_each_device(lambda cp: cp.start(), lambda p, t: None, lambda p: None)

    def forward(self):
        def pass_on(p, t):
            self._copy(p, p).wait_recv()
            self._copy(p, t, passing_on=True).start()

        self._as_each_device(lambda cp: None, pass_on, lambda p: None)

    def finish(self):
        self._as_each_device(lambda cp: cp.wait_send(), lambda p, t: self._copy(p, t, passing_on=True).wait_send(),
                             lambda p: self._copy(p, p).wait_recv())


WEIGHT_GATHER_SEMS = [pltpu.SemaphoreType.DMA((N_DEV,))] * 3
FORWARD_STEP = 6
REST_COLS = _shard_cols((0, 4 * D), (7 * D, IN_COLS))
HEAD_PAIRS = D // 128


def _qkv_piece(h, seg):
    col = (4 + seg) * D + 128 * h
    return col // W_IN_SHARD, col % W_IN_SHARD


class _PieceGather:
    def __init__(self, src, dst, send_sems, recv_sems):
        self.src, self.dst, self.send_sems, self.recv_sems = src, dst, send_sems, recv_sems
        self.me = _my_place()[3]

    def _copy(self, i, target):
        p, lo = _qkv_piece(i // 3, i % 3)
        return pltpu.make_async_remote_copy(
            src_ref=self.src(p, lo, lo + 128), dst_ref=self.dst.at[p, :, lo:lo + 128], send_sem=self.send_sems.at[i, target],
            recv_sem=self.recv_sems.at[i], device_id=_device(target), device_id_type=MESH)

    def _owner(self, i, act):
        p = _qkv_piece(i // 3, i % 3)[0]

        def sender():
            for k in range(N_DEV - 1):
                act(self._copy(i, (p + 1 + (k + i) % (N_DEV - 1)) % N_DEV))

        pl.when(self.me == p)(sender)

    def start(self, pieces):
        for i in pieces:
            self._owner(i, lambda cp: cp.start())

    def wait_send(self, pieces):
        for i in pieces:
            self._owner(i, lambda cp: cp.wait_send())

    def wait_recv(self, pieces):
        for i in pieces:
            p = _qkv_piece(i // 3, i % 3)[0]
            pl.when(self.me != p)(lambda i=i, p=p: self._copy(i, p).wait_recv())


def _piece_sems(n):
    return [pltpu.SemaphoreType.DMA((n, N_DEV)), pltpu.SemaphoreType.DMA((n,))]


def _norm_gather_first_weights(x2, norm_g, w_in, w3, cw):
    S = x2.shape[0]
    tm = ROW_TILE
    nsteps = S // tm

    def body(x_ref, g_ref, w_in_ref, w3_ref, cw_ref, u_ref, ut_ref, o_in, o_3, o_cw, in_bf, w3_bf, local_sems, *sems):
        i = pl.program_id(0)
        me = _my_place()[3]
        gather = _PieceGather(lambda p, lo, hi: in_bf.at[:, lo:hi], o_in, *sems)
        local = [pltpu.make_async_copy(src, dst.at[me], local_sems.at[a])
                 for a, (src, dst) in enumerate(((in_bf, o_in), (w3_bf, o_3), (cw_ref, o_cw)))]

        @pl.when(i == 0)
        def _():
            def cast_rows(r, carry):
                rows = pl.ds(pl.multiple_of(r * 128, 128), 128)
                in_bf[rows, :] = w_in_ref[rows, :].astype(BF)
                return carry

            lax.fori_loop(0, D // 128, cast_rows, 0)
            for a in range(3):
                w3_bf[a] = w3_ref[a].astype(BF)
            gather.start(range(3))
            for cp in local:
                cp.start()

        xv = x_ref[...]
        r = lax.rsqrt(jnp.mean(xv * xv, axis=-1, keepdims=True) + EPS)
        u = xv * r * g_ref[...]
        u_ref[...] = u.astype(BF)
        ut_ref[...] = u.T.astype(BF)

        @pl.when(i == nsteps - 1)
        def _():
            gather.wait_recv(range(3))
            gather.wait_send(range(3))
            for cp in local:
                cp.wait()

    return pl.pallas_call(
        body, name="norm_gather_first_weights", grid=(nsteps,),
        out_shape=(jax.ShapeDtypeStruct((S, D), BF), jax.ShapeDtypeStruct((D, S), BF),
                   jax.ShapeDtypeStruct((N_DEV, D, W_IN_SHARD), BF),
                   jax.ShapeDtypeStruct((N_DEV, 3, ROW_SHARD, D), BF),
                   jax.ShapeDtypeStruct((N_DEV, 8, 128), F32)),
        in_specs=[pl.BlockSpec((tm, D), lambda i: (i, 0)), pl.BlockSpec((1, D), lambda i: (0, 0)),
                  VMEM_SPEC, VMEM_SPEC, VMEM_SPEC],
        out_specs=(pl.BlockSpec((tm, D), lambda i: (i, 0)), pl.BlockSpec((D, tm), lambda i: (0, i)),
                   ANY_SPEC, ANY_SPEC, ANY_SPEC),
        scratch_shapes=[pltpu.VMEM((D, W_IN_SHARD), BF), pltpu.VMEM((3, ROW_SHARD, D), BF),
                        pltpu.SemaphoreType.DMA((3,))] + _piece_sems(3),
        compiler_params=_params(1),
    )(x2, norm_g, w_in, w3, cw)


class _GradExchange:
    def __init__(self, src, dst, send_sems, recv_sems, local_sem, cols):
        self.src, self.dst, self.cols = src, dst, cols
        self.send_sems, self.recv_sems, self.local_sem = send_sems, recv_sems, local_sem
        self.me = _my_place()[3]

    def _remote(self, p, source):
        return pltpu.make_async_remote_copy(
            src_ref=_block(self.src, p, self.cols(p)), dst_ref=_block(self.dst, source, self.cols(p)),
            send_sem=self.send_sems.at[p], recv_sem=self.recv_sems.at[source],
            device_id=_device(p), device_id_type=MESH)

    def _local(self, p):
        return pltpu.make_async_copy(_block(self.src, p, self.cols(p)), _block(self.dst, p, self.cols(p)),
                                     self.local_sem)

    def _as_each_device(self, send, local, receive):
        for m in range(N_DEV):
            def branch(m=m):
                for k in range(1, N_DEV):
                    p = (m + k) % N_DEV
                    if self.cols(p) is not None:
                        send(self._remote(p, m))
                if self.cols(m) is not None:
                    if self.local_sem is not None:
                        local(self._local(m))
                    for k in range(1, N_DEV):
                        receive(self._remote(m, (m + k) % N_DEV))

            pl.when(self.me == m)(branch)

    def start(self):
        self._as_each_device(lambda cp: cp.start(), lambda cp: cp.start(), lambda cp: None)

    def finish(self):
        self._as_each_device(lambda cp: cp.wait_send(), lambda cp: cp.wait(), lambda cp: cp.wait_recv())


GRAD_EXCHANGE_SEMS = [pltpu.SemaphoreType.DMA((N_DEV,)), pltpu.SemaphoreType.DMA((N_DEV,)), pltpu.SemaphoreType.DMA]


def _allreduce_small(p_mid, p_conv, p_norm):
    def body(a_ref, b_ref, c_ref, out_ref, mine, gathered, send_sems, recv_sems):
        x, y, c, me = _my_place()
        mine[...] = a_ref[...] + b_ref[...] + c_ref[...]
        gathered[me] = mine[...]
        remote = []
        for k, (peer, _) in enumerate(_peers(x, y, c)):
            cp = pltpu.make_async_remote_copy(
                src_ref=mine, dst_ref=gathered.at[me], send_sem=send_sems.at[k], recv_sem=recv_sems.at[k],
                device_id=peer, device_id_type=MESH)
            cp.start()
            remote.append(cp)
        for cp in remote:
            cp.wait()
        total = gathered[0]
        for s in range(1, N_DEV):
            total = total + gathered[s]
        out_ref[...] = total

    return pl.pallas_call(
        body, name="allreduce_small",
        out_shape=jax.ShapeDtypeStruct((8, D), F32),
        in_specs=[VMEM_SPEC, VMEM_SPEC, VMEM_SPEC], out_specs=VMEM_SPEC,
        scratch_shapes=[pltpu.VMEM((8, D), F32), pltpu.VMEM((N_DEV, 8, D), F32),
                        pltpu.SemaphoreType.DMA((N_DEV - 1,)), pltpu.SemaphoreType.DMA((N_DEV - 1,))],
    )(p_mid, p_conv, p_norm)


def _proj_pieces():
    cuts = sorted(set(range(0, IN_COLS + 1, D)) | set(range(0, IN_COLS + 1, W_IN_SHARD)))
    return [(lo // D, lo % D, lo // W_IN_SHARD, lo % W_IN_SHARD, hi - lo) for lo, hi in zip(cuts[:-1], cuts[1:])]


PROJ_TN = 256


def _proj_cols(u, w_all, seg0, n_seg, dtype, name):
    S = u.shape[0]
    tn = PROJ_TN
    per_shard = W_IN_SHARD // tn
    tile0 = seg0 * D // tn

    def body(u_ref, w_ref, out_ref):
        out_ref[...] = _dot(u_ref[...], w_ref[0]).astype(dtype)

    return pl.pallas_call(
        body, name=name, grid=(n_seg * D // tn,),
        in_specs=[VMEM_SPEC, pl.BlockSpec((1, D, tn), lambda t: ((tile0 + t) // per_shard, 0, (tile0 + t) % per_shard))],
        out_specs=pl.BlockSpec((S, tn), lambda t: (0, t)),
        out_shape=jax.ShapeDtypeStruct((S, n_seg * D), dtype),
        compiler_params=_params(1),
    )(u, w_all)


CONV_TM, CONV_TC = 1024, 512
HALO = 16


def _conv_fwd(pa, cw8):
    S = pa.shape[0]
    tm, tc = CONV_TM, CONV_TC
    nct = D // tc

    def seg(s):
        return pl.BlockSpec((tm, tc), lambda i, j, s=s: (i, s * nct + j))

    def halo_before(s):
        return pl.BlockSpec((HALO, tc), lambda i, j, s=s: (jnp.maximum(i * (tm // HALO) - 1, 0), s * nct + j))

    def body(xc, bg, cg, zc, xch, cgh, cw, out):
        i = pl.program_id(0)
        a = cg[...].astype(F32) * xc[...].astype(F32)
        ah = cgh[...].astype(F32) * xch[...].astype(F32)
        ah = jnp.where(i > 0, ah, 0.0)
        row = lax.broadcasted_iota(jnp.int32, (tm, tc), 0)
        a1 = jnp.where(row == 0, ah[HALO - 1:HALO, :], pltpu.roll(a, 1, 0))
        a2 = jnp.where(row == 0, ah[HALO - 2:HALO - 1, :],
                       jnp.where(row == 1, ah[HALO - 1:HALO, :], pltpu.roll(a, 2, 0)))
        w = cw[...]
        conv = w[0:1, :] * a2 + w[1:2, :] * a1 + w[2:3, :] * a
        z = zc[...].astype(F32)
        out[...] = (z * _sigmoid(z) * bg[...].astype(F32) * conv).astype(BF)

    return pl.pallas_call(
        body, name="conv_fwd", grid=(S // tm, nct),
        in_specs=[seg(0), seg(1), seg(2), seg(3), halo_before(0), halo_before(2),
                  pl.BlockSpec((8, tc), lambda i, j: (0, j))],
        out_specs=pl.BlockSpec((tm, tc), lambda i, j: (i, j)),
        out_shape=jax.ShapeDtypeStruct((S, D), BF),
        compiler_params=_params(2),
    )(pa, pa, pa, pa, pa, pa, cw8)


ATT_UNROLL = 32


LAYOUT_MOD = 4
RUN = QB // LAYOUT_MOD


def _fold_masks(d):
    row = lax.broadcasted_iota(jnp.int32, (QB, QB), 0)
    lane = lax.broadcasted_iota(jnp.int32, (QB, QB), 1)
    if d == 1:
        qpos, kpos = LAYOUT_MOD * (row % RUN) + row // RUN, LAYOUT_MOD * (lane % RUN) + lane // RUN
    else:
        qpos, kpos = row, lane
    tri_le = kpos <= qpos
    dist = jnp.where(tri_le, qpos - kpos, qpos - kpos + QB).astype(F32)
    return tri_le, dist, lane < HEAD_DIM


class _Rows:
    def __init__(self, slices):
        self.slices = slices

    def get(self, ref):
        parts = [ref[sl, :] for sl in self.slices]
        return parts[0] if len(parts) == 1 else jnp.concatenate(parts, axis=0)

    def put(self, ref, val):
        size = QB // len(self.slices)
        for g, sl in enumerate(self.slices):
            ref[sl, :] = val if len(self.slices) == 1 else val[g * size:(g + 1) * size]

    def add(self, ref, val):
        self.put(ref, self.get(ref) + val)


def _block_rows(b, d, S):
    quarter = S // LAYOUT_MOD
    nb = S // (QB * d)
    r, n = b // nb, b % nb
    n_prev = jnp.maximum(n - 1, 0)
    if d == 1:
        runs = lambda m: _Rows([pl.ds(pl.multiple_of(g * quarter + RUN * m, RUN), RUN) for g in range(LAYOUT_MOD)])
        return n, runs(n), runs(n_prev)
    if d == LAYOUT_MOD:
        block = lambda m: _Rows([pl.ds(pl.multiple_of(r * quarter + QB * m, QB), QB)])
        return n, block(n), block(n_prev)
    step = d // LAYOUT_MOD
    first = (r % LAYOUT_MOD) * quarter + r // LAYOUT_MOD
    strided = lambda m: _Rows([pl.ds(first + QB * step * m, QB, stride=step)])
    return n, strided(n), strided(n_prev)


def _natural_rows(i, S):
    per = S // LAYOUT_MOD // QB
    return pl.ds(i // per + LAYOUT_MOD * QB * (i % per), QB, stride=LAYOUT_MOD)


def _head_sum_matrix():
    r = lax.broadcasted_iota(jnp.int32, (2 * QB, 2 * QB), 0)
    c = lax.broadcasted_iota(jnp.int32, (2 * QB, 2 * QB), 1)
    return (((r % QB) // HEAD_DIM) == (c // QB)).astype(F32).astype(BF)


def _hi_lo(t):
    hi = t.astype(BF)
    return jnp.concatenate([hi, (t - hi.astype(F32)).astype(BF)], axis=1)


PROJ_ROWS = 1024


def _attn_fwd(u, slopes, w_all, w3_all, cw_all):
    S = u.shape[0]
    hpr = HEAD_PAIRS
    n_blocks = S // QB
    later = range(3, 3 * hpr)

    def body(sl_ref, u_ref, w_in_ref, w3_in_ref, cw_in_ref, o_ref, lse_ref, q_ref, k_ref, v_ref, w_ref, w3_ref,
             cw_ref, acc, m_s, l_s, w_tile, staged, tile_sems, *sems):
        hp = pl.program_id(0)
        me = _my_place()[3]
        pieces = _PieceGather(lambda p, lo, hi: w_ref.at[p, :, lo:hi], w_ref, *sems[0:2])
        gathers = (_WeightGather(lambda p, cols: _block(w_ref, me, cols), w_ref, *sems[2:5], REST_COLS),
                   _WeightGather(lambda p, cols: w3_ref.at[me], w3_ref, *sems[5:8], _whole),
                   _WeightGather(lambda p, cols: cw_ref.at[me], cw_ref, *sems[8:11], _whole))

        @pl.when(hp == 0)
        def _():
            pieces.start(later)
            for g in gathers:
                g.start()

        @pl.when(hp == FORWARD_STEP)
        def _():
            for g in gathers:
                g.forward()

        for h in range(hpr):
            @pl.when(hp == h)
            def _(h=h):
                if h > 0:
                    pieces.wait_recv(range(3 * h, 3 * h + 3))
                fetch = []
                for seg in range(3):
                    p, lo = _qkv_piece(h, seg)
                    fetch.append(pltpu.make_async_copy(w_ref.at[p, :, lo:lo + 128], w_tile.at[:, seg * 128:(seg + 1) * 128],
                                                       tile_sems.at[seg]))
                    fetch[-1].start()
                for cp in fetch:
                    cp.wait()

        def project(i, carry):
            rows = pl.ds(pl.multiple_of(i * PROJ_ROWS, PROJ_ROWS), PROJ_ROWS)
            qkv = _dot(u_ref[rows, :], w_tile[...])
            per = PROJ_ROWS // LAYOUT_MOD
            for seg, ref in enumerate((q_ref, k_ref, v_ref)):
                staged[seg] = qkv[:, seg * 128:(seg + 1) * 128]
                for g in range(LAYOUT_MOD):
                    dst = pl.ds(pl.multiple_of(g * (S // LAYOUT_MOD) + i * per, per), per)
                    ref[dst, :] = staged.at[seg][pl.ds(g, per, stride=LAYOUT_MOD), :]
            return carry

        lax.fori_loop(0, S // PROJ_ROWS, project, 0)

        head_sum = _head_sum_matrix()
        ones_b = jnp.ones((2 * QB, QB), BF)
        m_s[...] = jnp.full(m_s.shape, NEG, F32)
        l_s[...] = jnp.zeros(l_s.shape, F32)
        acc[...] = jnp.zeros(acc.shape, F32)

        for d in DILATIONS:
            tri_le, dist, low = _fold_masks(d)
            low_b = low.astype(F32).astype(BF)
            high_b = 1.0 - low_b
            slope = [sl_ref[2 * hp + a] * float(d) for a in range(2)]
            bias = [slope[a] * dist for a in range(2)]

            def block(b, d=d, slope=slope, bias=bias, tri_le=tri_le, low=low, low_b=low_b, high_b=high_b):
                n, cur, prev = _block_rows(b, d, S)
                has_prev = n > 0
                valid = jnp.logical_or(tri_le, has_prev)
                q2 = (cur.get(q_ref) * 0.125).astype(BF)
                qs = jnp.concatenate([q2 * low_b, q2 * high_b], axis=0)
                vp = prev.get(v_ref)
                kp_b = prev.get(k_ref).astype(BF)
                kcat = jnp.concatenate([kp_b, cur.get(k_ref).astype(BF)], axis=0)
                vcat = jnp.concatenate([vp, cur.get(v_ref)], axis=0).astype(BF)
                s2 = _dot_nt(qs, kcat)
                e2 = _dot(_hi_lo(q2.astype(F32) * kp_b.astype(F32)), head_sum)
                p_rows, alpha_h, pe_h = [], [], []
                for a in range(2):
                    sp, sc = s2[a * QB:(a + 1) * QB, :QB], s2[a * QB:(a + 1) * QB, QB:]
                    comb = jnp.where(valid, jnp.where(tri_le, sc, sp) - bias[a], NEG)
                    e = jnp.where(has_prev, e2[:, a * QB:(a + 1) * QB] - slope[a] * float(QB), NEG)
                    m_old = cur.get(m_s.at[a])
                    m_new = jnp.maximum(jnp.maximum(m_old, jnp.max(comb, axis=-1, keepdims=True)), e)
                    cur.put(m_s.at[a], m_new)
                    p = jnp.exp(comb - m_new)
                    pe_h.append(jnp.exp(e - m_new))
                    alpha_h.append(jnp.exp(m_old - m_new))
                    p_rows.append(jnp.concatenate([jnp.where(tri_le, 0.0, p).astype(BF),
                                                   jnp.where(tri_le, p, 0.0).astype(BF)], axis=1))
                pv = _dot(jnp.concatenate(p_rows, axis=0), jnp.concatenate([vcat, ones_b], axis=1))
                for a in range(2):
                    cur.put(l_s.at[a], alpha_h[a] * cur.get(l_s.at[a]) + pv[a * QB:(a + 1) * QB, QB:] + pe_h[a])
                cur.put(acc, jnp.where(low, alpha_h[0], alpha_h[1]) * cur.get(acc)
                        + jnp.where(low, pv[:QB, :QB], pv[QB:, :QB]) + jnp.where(low, pe_h[0], pe_h[1]) * vp)

            def several(it, carry, block=block):
                for u in range(ATT_UNROLL):
                    block(it * ATT_UNROLL + u)
                return carry

            lax.fori_loop(0, n_blocks // ATT_UNROLL, several, 0)

        low = _fold_masks(LAYOUT_MOD)[2]

        def finish(i, carry):
            rows = pl.ds(pl.multiple_of(i * QB, QB), QB)
            l0, l1 = l_s[0, rows, :], l_s[1, rows, :]
            o_ref[_natural_rows(i, S), :] = acc[rows, :] / jnp.where(low, l0, l1)
            lse_ref[0, rows, :] = m_s[0, rows, :] + jnp.log(l0)
            lse_ref[1, rows, :] = m_s[1, rows, :] + jnp.log(l1)
            return carry

        lax.fori_loop(0, n_blocks, finish, 0)

        @pl.when(hp == hpr - 1)
        def _():
            pieces.wait_send(later)
            for g in gathers:
                g.finish()

    col = pl.BlockSpec((S, 128), lambda h: (0, h))
    act = jax.ShapeDtypeStruct((S, D), F32)
    gathered = (w_all, w3_all, cw_all)
    return pl.pallas_call(
        body, name="attn_fwd", grid=(hpr,),
        in_specs=[SMEM_SPEC, VMEM_SPEC, ANY_SPEC, ANY_SPEC, ANY_SPEC],
        out_specs=(col, pl.BlockSpec((2, S, 128), lambda h: (0, 0, h)), col, col, col, ANY_SPEC, ANY_SPEC, ANY_SPEC),
        out_shape=(act, jax.ShapeDtypeStruct((2, S, D), F32), act, act, act,
                   *[jax.ShapeDtypeStruct(t.shape, t.dtype) for t in gathered]),
        scratch_shapes=([pltpu.VMEM((S, 128), F32), pltpu.VMEM((2, S, 128), F32), pltpu.VMEM((2, S, 128), F32),
                         pltpu.VMEM((D, 3 * 128), BF), pltpu.VMEM((3, PROJ_ROWS, 128), F32),
                         pltpu.SemaphoreType.DMA((3,))]
                        + _piece_sems(3 * hpr) + WEIGHT_GATHER_SEMS * 3),
        input_output_aliases={2: 5, 3: 6, 4: 7},
        compiler_params=_params(1),
    )(slopes, u, *gathered)


def _set_rows(shape, rows):
    idx = lax.broadcasted_iota(jnp.int32, shape, 0)
    out = jnp.zeros(shape, F32)
    for r, val in rows.items():
        out = out + jnp.where(idx == r, val, 0.0)
    return out


def _mid(yc_in, pa_mid, o, x2, target, b_merge, final_g, w3):
    S = x2.shape[0]
    tm = ROW_TILE
    nsteps = S // tm
    tile = pl.BlockSpec((tm, D), lambda i: (i, 0))

    def body(yc_ref, za_ref, gcp_ref, gap_ref, o_ref, x_ref, t_ref, b_ref, fg_ref, w_ref,
             dh_ref, dmid_ref, do_ref, dyc_ref, gw_ref, small_ref, acc, stage):
        i = pl.program_id(0)

        @pl.when(i == 0)
        def _():
            acc[...] = jnp.zeros_like(acc)
            small_ref[...] = jnp.zeros_like(small_ref)

        wc, wa, wo = w_ref[0], w_ref[1], w_ref[2]
        z = za_ref[...].astype(F32)
        sg = _sigmoid(z)
        ov = o_ref[...]
        yc_in_b, ya_in_b = yc_ref[...], (z * sg * ov).astype(BF)
        yc = _dot(yc_in_b, wc)
        ya = _dot(ya_in_b, wa)
        b = b_ref[...]
        gc = _sigmoid(gcp_ref[...].astype(F32) + b[:, :D])
        ga = _sigmoid(gap_ref[...].astype(F32) + b[:, D:])
        merged = gc * yc + ga * ya
        merged_b = merged.astype(BF)
        h = x_ref[...] + _dot(merged_b, wo)
        r2 = lax.rsqrt(jnp.mean(h * h, axis=-1, keepdims=True) + EPS)
        n = h * r2
        fg = fg_ref[...]
        err = n * fg - t_ref[...]
        loss = 0.5 * jnp.sum(jnp.sum(err * err, axis=-1, keepdims=True) / D, axis=0, keepdims=True)
        dy = err / D
        g_fg = jnp.sum(dy * n, axis=0, keepdims=True)
        dn = dy * fg
        dh = r2 * (dn - n * jnp.mean(dn * n, axis=-1, keepdims=True))
        dh_ref[...] = dh
        dh_b = dh.astype(BF)
        dmerged = _dot_nt(dh_b, wo)
        acc[2] += _dot(merged.T.astype(BF), dh_b)
        dyc = (dmerged * gc).astype(BF)
        dya = (dmerged * ga).astype(BF)
        dgcp = dmerged * yc * gc * (1.0 - gc)
        dgap = dmerged * ya * ga * (1.0 - ga)
        dmid_ref[1] = dgcp.astype(BF)
        dmid_ref[2] = dgap.astype(BF)
        acc[0] += _dot(yc_in_b.astype(F32).T.astype(BF), dyc)
        acc[1] += _dot(ya_in_b.astype(F32).T.astype(BF), dya)
        dyc_ref[...] = _dot_nt(dyc, wc).astype(BF)
        dya_in = _dot_nt(dya, wa)
        do_ref[...] = dya_in * (z * sg)
        dmid_ref[0] = (dya_in * ov * (sg * (1.0 + z * (1.0 - sg)))).astype(BF)
        small_ref[...] += _set_rows((8, D), {
            1: jnp.sum(dgcp, axis=0, keepdims=True), 2: jnp.sum(dgap, axis=0, keepdims=True),
            3: g_fg, 7: jnp.broadcast_to(loss, (1, D))})

        @pl.when(i == nsteps - 1)
        def _():
            for p in range(N_DEV):
                for a in range(3):
                    stage[...] = acc[a, p * ROW_SHARD:(p + 1) * ROW_SHARD, :].astype(BF)
                    pltpu.sync_copy(stage, gw_ref.at[p, a])

    return pl.pallas_call(
        body, name="mid", grid=(nsteps,),
        in_specs=[tile, pl.BlockSpec((tm, D), lambda i: (i, 0)), pl.BlockSpec((tm, D), lambda i: (i, 1)),
                  pl.BlockSpec((tm, D), lambda i: (i, 2)), tile, tile, tile,
                  pl.BlockSpec((1, 2 * D), lambda i: (0, 0)), pl.BlockSpec((1, D), lambda i: (0, 0)), VMEM_SPEC],
        out_specs=(tile, pl.BlockSpec((3, tm, D), lambda i: (0, i, 0)), tile, tile,
                   ANY_SPEC, pl.BlockSpec((8, D), lambda i: (0, 0))),
        out_shape=(jax.ShapeDtypeStruct((S, D), F32), jax.ShapeDtypeStruct((3, S, D), BF),
                   jax.ShapeDtypeStruct((S, D), F32), jax.ShapeDtypeStruct((S, D), BF),
                   jax.ShapeDtypeStruct((N_DEV, 3, ROW_SHARD, D), BF), jax.ShapeDtypeStruct((8, D), F32)),
        scratch_shapes=[pltpu.VMEM((3, D, D), F32), pltpu.VMEM((ROW_SHARD, D), BF)],
        compiler_params=_params(1),
    )(yc_in, pa_mid, pa_mid, pa_mid, o, x2, target, b_merge, final_g, w3)


def _conv_bwd(dyc_in, pa, cw8):
    S = pa.shape[0]
    tm, tc = CONV_TM, CONV_TC
    nct = D // tc
    nrt = S // tm
    last_halo = S // HALO - 1

    def seg(s):
        return pl.BlockSpec((tm, tc), lambda j, i, s=s: (i, s * nct + j))

    def halo_before(s):
        return pl.BlockSpec((HALO, tc), lambda j, i, s=s: (jnp.maximum(i * (tm // HALO) - 1, 0), s * nct + j))

    def halo_after(s):
        return pl.BlockSpec((HALO, tc), lambda j, i, s=s: (jnp.minimum((i + 1) * (tm // HALO), last_halo), s * nct + j))

    def body(dy, xc, bg, cg, zc, xch, cgh, dyn, bgn, zcn, cw, dout, gcw):
        i = pl.program_id(1)

        @pl.when(i == 0)
        def _():
            gcw[...] = jnp.zeros_like(gcw)

        xcv, cgv = xc[...].astype(F32), cg[...].astype(F32)
        a = cgv * xcv
        ah = jnp.where(i > 0, cgh[...].astype(F32) * xch[...].astype(F32), 0.0)
        row = lax.broadcasted_iota(jnp.int32, (tm, tc), 0)
        a1 = jnp.where(row == 0, ah[HALO - 1:HALO, :], pltpu.roll(a, 1, 0))
        a2 = jnp.where(row == 0, ah[HALO - 2:HALO - 1, :],
                       jnp.where(row == 1, ah[HALO - 1:HALO, :], pltpu.roll(a, 2, 0)))
        w = cw[...]
        conv = w[0:1, :] * a2 + w[1:2, :] * a1 + w[2:3, :] * a
        z = zc[...].astype(F32)
        sg = _sigmoid(z)
        silu = z * sg
        bgv = bg[...].astype(F32)
        dyv = dy[...].astype(F32)
        dout[3] = (dyv * bgv * conv * (sg * (1.0 + z * (1.0 - sg)))).astype(BF)
        dout[1] = (dyv * silu * conv).astype(BF)
        dc = dyv * silu * bgv
        zn = zcn[...].astype(F32)
        dcn = dyn[...].astype(F32) * (zn * _sigmoid(zn)) * bgn[...].astype(F32)
        dcn = jnp.where(i < nrt - 1, dcn, 0.0)
        dc1 = jnp.where(row == tm - 1, dcn[0:1, :], pltpu.roll(dc, tm - 1, 0))
        dc2 = jnp.where(row == tm - 1, dcn[1:2, :],
                        jnp.where(row == tm - 2, dcn[0:1, :], pltpu.roll(dc, tm - 2, 0)))
        da = w[2:3, :] * dc + w[1:2, :] * dc1 + w[0:1, :] * dc2
        dout[2] = (da * xcv).astype(BF)
        dout[0] = (da * cgv).astype(BF)
        gcw[...] += _set_rows((8, tc), {
            4: jnp.sum(dc * a2, axis=0, keepdims=True), 5: jnp.sum(dc * a1, axis=0, keepdims=True),
            6: jnp.sum(dc * a, axis=0, keepdims=True)})

    return pl.pallas_call(
        body, name="conv_bwd", grid=(nct, nrt),
        in_specs=[pl.BlockSpec((tm, tc), lambda j, i: (i, j)), seg(0), seg(1), seg(2), seg(3),
                  halo_before(0), halo_before(2),
                  pl.BlockSpec((HALO, tc), lambda j, i: (jnp.minimum((i + 1) * (tm // HALO), last_halo), j)),
                  halo_after(1), halo_after(3), pl.BlockSpec((8, tc), lambda j, i: (0, j))],
        out_specs=(pl.BlockSpec((4, tm, tc), lambda j, i: (0, i, j)), pl.BlockSpec((8, tc), lambda j, i: (0, j))),
        out_shape=(jax.ShapeDtypeStruct((4, S, D), BF), jax.ShapeDtypeStruct((8, D), F32)),
        compiler_params=_params(2),
    )(dyc_in, pa, pa, pa, pa, pa, pa, dyc_in, pa, pa, cw8)


def _attn_bwd(q, k, v, slopes, do, o, lse, g_in, g_3):
    S = q.shape[0]
    hpr = HEAD_PAIRS
    n_blocks = S // QB

    def body(sl_ref, q_ref, k_ref, v_ref, do_ref, o_ref, lse_ref, gin_ref, g3_ref, out_ref, rin_ref, r3_ref,
             dq_s, dk_s, dv_s, do_s, dd_s, *sems):
        hp = pl.program_id(0)
        exchanges = (_GradExchange(gin_ref, rin_ref, *sems[:3], _shard_cols((0, SEG0_ATTN * D), (SEG0_MID * D, IN_COLS))),
                     _GradExchange(g3_ref, r3_ref, *sems[3:], _whole))

        @pl.when(hp == 0)
        def _():
            for ex in exchanges:
                ex.start()

        head_sum = _head_sum_matrix()
        dq_s[...] = jnp.zeros(dq_s.shape, F32)
        dk_s[...] = jnp.zeros(dk_s.shape, F32)
        dv_s[...] = jnp.zeros(dv_s.shape, F32)

        def row_dots(i, carry):
            rows = pl.ds(pl.multiple_of(i * QB, QB), QB)
            natural = _natural_rows(i, S)
            do_c = do_ref[natural, :]
            do_s[rows, :] = do_c
            dd = _dot(_hi_lo(do_c * o_ref[natural, :]), head_sum)
            dd_s[0, rows, :] = dd[:, :QB]
            dd_s[1, rows, :] = dd[:, QB:]
            return carry

        lax.fori_loop(0, n_blocks, row_dots, 0)

        for d in DILATIONS:
            tri_le, dist, low = _fold_masks(d)
            low_b = low.astype(F32).astype(BF)
            high_b = 1.0 - low_b
            slope = [sl_ref[2 * hp + a] * float(d) for a in range(2)]
            bias = [slope[a] * dist for a in range(2)]

            def block(b, d=d, slope=slope, bias=bias, tri_le=tri_le, low=low, low_b=low_b, high_b=high_b):
                n, cur, prev = _block_rows(b, d, S)
                has_prev = n > 0
                valid = jnp.logical_or(tri_le, has_prev)
                q2f = cur.get(q_ref) * 0.125
                q2 = q2f.astype(BF)
                qs = jnp.concatenate([q2 * low_b, q2 * high_b], axis=0)
                kp, vp = prev.get(k_ref), prev.get(v_ref)
                kp_b, vp_b = kp.astype(BF), vp.astype(BF)
                kcat = jnp.concatenate([kp_b, cur.get(k_ref).astype(BF)], axis=0)
                vcat = jnp.concatenate([vp_b, cur.get(v_ref).astype(BF)], axis=0)
                do2f = cur.get(do_s)
                do2 = do2f.astype(BF)
                dos = jnp.concatenate([do2 * low_b, do2 * high_b], axis=0)
                s2 = _dot_nt(qs, kcat)
                dp2 = _dot_nt(dos, vcat)
                diag2 = _dot(jnp.concatenate([_hi_lo(q2.astype(F32) * kp_b.astype(F32)),
                                              _hi_lo(do2.astype(F32) * vp_b.astype(F32))], axis=0), head_sum)
                p_rows, ds_rows, pe_h, dse_h = [], [], [], []
                for a in range(2):
                    hs = slice(a * QB, (a + 1) * QB)
                    sp, sc = s2[hs, :QB], s2[hs, QB:]
                    dpp, dpc = dp2[hs, :QB], dp2[hs, QB:]
                    lse_a, dd_a = cur.get(lse_ref.at[a]), cur.get(dd_s.at[a])
                    comb = jnp.where(tri_le, sc, sp) - bias[a]
                    e = diag2[:QB, hs] - slope[a] * float(QB)
                    p = jnp.where(valid, jnp.exp(comb - lse_a), 0.0)
                    pe = jnp.where(has_prev, jnp.exp(e - lse_a), 0.0)
                    ds = p * (jnp.where(tri_le, dpc, dpp) - dd_a)
                    dse_h.append(pe * (diag2[QB:, hs] - dd_a))
                    pe_h.append(pe)
                    p_rows.append(jnp.concatenate([jnp.where(tri_le, 0.0, p).astype(BF),
                                                   jnp.where(tri_le, p, 0.0).astype(BF)], axis=1))
                    ds_rows.append(jnp.concatenate([jnp.where(tri_le, 0.0, ds).astype(BF),
                                                    jnp.where(tri_le, ds, 0.0).astype(BF)], axis=1))
                pst = jnp.concatenate(p_rows, axis=0)
                dst = jnp.concatenate(ds_rows, axis=0)
                pe2 = jnp.where(low, pe_h[0], pe_h[1])
                dse2 = jnp.where(low, dse_h[0], dse_h[1])
                dq = _dot(dst, kcat)
                cur.add(dq_s, (jnp.where(low, dq[:QB], dq[QB:]) + dse2 * kp) * 0.125)
                dk = _dot_tn(dst, qs)
                dv = _dot_tn(pst, dos)
                prev.add(dk_s, dk[:QB] + dse2 * q2f)
                cur.add(dk_s, dk[QB:])
                prev.add(dv_s, dv[:QB] + pe2 * do2f)
                cur.add(dv_s, dv[QB:])

            def several(it, carry, block=block):
                for u in range(ATT_UNROLL):
                    block(it * ATT_UNROLL + u)
                return carry

            lax.fori_loop(0, n_blocks // ATT_UNROLL, several, 0)

        def finish(i, carry):
            rows = pl.ds(pl.multiple_of(i * QB, QB), QB)
            natural = _natural_rows(i, S)
            for t, ref in enumerate((dq_s, dk_s, dv_s)):
                out_ref.at[t][natural, :] = ref[rows, :]
            return carry

        lax.fori_loop(0, n_blocks, finish, 0)

        @pl.when(hp == hpr - 1)
        def _():
            for ex in exchanges:
                ex.finish()

    col = pl.BlockSpec((S, 128), lambda h: (0, h))
    return pl.pallas_call(
        body, name="attn_bwd", grid=(hpr,),
        in_specs=[SMEM_SPEC, col, col, col, col, col, pl.BlockSpec((2, S, 128), lambda h: (0, 0, h)),
                  ANY_SPEC, ANY_SPEC],
        out_specs=(pl.BlockSpec((3, S, 128), lambda h: (0, 0, h)), ANY_SPEC, ANY_SPEC),
        out_shape=(jax.ShapeDtypeStruct((3, S, D), F32), jax.ShapeDtypeStruct(g_in.shape, BF),
                   jax.ShapeDtypeStruct(g_3.shape, BF)),
        scratch_shapes=([pltpu.VMEM((S, 128), F32)] * 4 + [pltpu.VMEM((2, S, 128), F32)]
                        + GRAD_EXCHANGE_SEMS + GRAD_EXCHANGE_SEMS),
        compiler_params=_params(1),
    )(slopes, q, k, v, do, o, lse, g_in, g_3)


WG_TN = 256
SEG0_CONV, SEG0_ATTN, SEG0_MID = 0, 4, 7


def _wgrad_in(ut, d_group, seg0, g_in, name):
    S = ut.shape[1]
    tn = WG_TN
    per_seg = D // tn
    per_shard = W_IN_SHARD // tn
    n_tiles = d_group.shape[0] * per_seg
    tile0 = seg0 * per_seg

    def body(ut_ref, d_ref, *rest):
        rest[-1][0] = _dot(ut_ref[...], d_ref[0].astype(BF)).astype(BF)

    operands, in_specs, aliases = [ut, d_group], [VMEM_SPEC, pl.BlockSpec((1, S, tn), lambda t: (t // per_seg, 0, t % per_seg))], {}
    if g_in is not None:
        operands.append(g_in)
        in_specs.append(ANY_SPEC)
        aliases = {2: 0}
    return pl.pallas_call(
        body, name=name, grid=(n_tiles,), in_specs=in_specs,
        out_specs=pl.BlockSpec((1, D, tn), lambda t: ((tile0 + t) // per_shard, 0, (tile0 + t) % per_shard)),
        out_shape=jax.ShapeDtypeStruct((N_DEV, D, W_IN_SHARD), BF),
        input_output_aliases=aliases,
        compiler_params=_params(1),
    )(*operands)


def _dgrad_norm_bwd(d_conv, d_attn, d_mid, w_all, x2, dh, norm_g):
    S = x2.shape[0]
    tm = ROW_TILE
    nsteps = S // tm
    tile = pl.BlockSpec((tm, D), lambda i: (i, 0))
    pieces = _proj_pieces()

    def body(a_ref, b_ref, c_ref, w_ref, x_ref, dh_ref, g_ref, gx_ref, small_ref):
        i = pl.program_id(0)

        @pl.when(i == 0)
        def _():
            small_ref[...] = jnp.zeros_like(small_ref)

        groups = (a_ref, b_ref, c_ref)
        du = jnp.zeros((tm, D), F32)
        for s, sc, p, pc, width in pieces:
            g = 0 if s < 4 else (1 if s < 7 else 2)
            local = s - (0, 4, 7)[g]
            du = du + _dot_nt(groups[g][local, :, sc:sc + width].astype(BF), w_ref[p, :, pc:pc + width])
        xv = x_ref[...]
        r = lax.rsqrt(jnp.mean(xv * xv, axis=-1, keepdims=True) + EPS)
        n = xv * r
        dn = du * g_ref[...]
        gx_ref[...] = dh_ref[...] + r * (dn - n * jnp.mean(dn * n, axis=-1, keepdims=True))
        small_ref[...] += _set_rows((8, D), {0: jnp.sum(du * n, axis=0, keepdims=True)})

    return pl.pallas_call(
        body, name="dgrad_norm_bwd", grid=(nsteps,),
        in_specs=[pl.BlockSpec((4, tm, D), lambda i: (0, i, 0)), pl.BlockSpec((3, tm, D), lambda i: (0, i, 0)),
                  pl.BlockSpec((3, tm, D), lambda i: (0, i, 0)), VMEM_SPEC, tile, tile,
                  pl.BlockSpec((1, D), lambda i: (0, 0))],
        out_specs=(tile, pl.BlockSpec((8, D), lambda i: (0, 0))),
        out_shape=(jax.ShapeDtypeStruct((S, D), F32), jax.ShapeDtypeStruct((8, D), F32)),
        compiler_params=_params(1),
    )(d_conv, d_attn, d_mid, w_all, x2, dh, norm_g)


HBM_SPEC = pl.BlockSpec(memory_space=pltpu.HBM)
SEM_SPEC = pl.BlockSpec(memory_space=pltpu.SEMAPHORE)
ATTN_COLS = _shard_cols((SEG0_ATTN * D, SEG0_MID * D))


def _attn_cols_exchange_start(g_in, r_in):
    def body(g_ref, r_ref, send_sems, recv_sems, g_thru, r_thru, token):
        _GradExchange(g_ref, r_ref, send_sems, recv_sems, None, ATTN_COLS).start()
        token[...] = jnp.zeros_like(token)

    hbm = pltpu.with_memory_space_constraint
    return pl.pallas_call(
        body, name="attn_cols_exchange_start",
        out_shape=(pltpu.SemaphoreType.DMA((N_DEV,)), pltpu.SemaphoreType.DMA((N_DEV,)),
                   pltpu.HBM(g_in.shape, g_in.dtype), pltpu.HBM(r_in.shape, r_in.dtype),
                   jax.ShapeDtypeStruct((8, 128), F32)),
        in_specs=(HBM_SPEC, HBM_SPEC), out_specs=(SEM_SPEC, SEM_SPEC, HBM_SPEC, HBM_SPEC, VMEM_SPEC),
        input_output_aliases={0: 2, 1: 3},
        compiler_params=pltpu.CompilerParams(has_side_effects=pltpu.SideEffectType.DATAFLOW_SIDE_EFFECTING),
    )(hbm(g_in, pltpu.HBM), hbm(r_in, pltpu.HBM))


def _attn_cols_exchange_wait(send_sems, recv_sems, g_thru, r_thru, after):
    def body(g_ref, r_ref, send_sems, recv_sems, after_ref, g_dead, r_out):
        _GradExchange(g_ref, r_ref, send_sems, recv_sems, None, ATTN_COLS).finish()

    return pl.pallas_call(
        body, name="attn_cols_exchange_wait",
        out_shape=(pltpu.HBM(g_thru.shape, g_thru.dtype), pltpu.HBM(r_thru.shape, r_thru.dtype)),
        in_specs=(HBM_SPEC, HBM_SPEC, SEM_SPEC, SEM_SPEC, ANY_SPEC), out_specs=(HBM_SPEC, HBM_SPEC),
        input_output_aliases={0: 0, 1: 1},
        compiler_params=pltpu.CompilerParams(has_side_effects=pltpu.SideEffectType.DATAFLOW_SIDE_EFFECTING),
    )(g_thru, r_thru, send_sems, recv_sems, after)


def _adamw_math(w, g, m, v):
    m = ADAM_B1 * m + (1.0 - ADAM_B1) * g
    v = ADAM_B2 * v + (1.0 - ADAM_B2) * (g * g)
    m_hat = m / (1.0 - ADAM_B1 ** ADAM_STEP)
    v_hat = v / (1.0 - ADAM_B2 ** ADAM_STEP)
    delta = -ADAM_LR * (m_hat / (jnp.sqrt(v_hat) + ADAM_EPS) + ADAM_WD * w)
    return delta, m, v


def _sum_adamw(parts, w, m, v, tm, name):
    R, C = w.shape
    tile = pl.BlockSpec((tm, C), lambda i: (i, 0))

    def body(p_ref, w_ref, m_ref, v_ref, g_out, d_out, m_out, v_out):
        g = p_ref[0].astype(F32)
        for s in range(1, N_DEV):
            g = g + p_ref[s].astype(F32)
        g_out[...] = g
        d_out[...], m_out[...], v_out[...] = _adamw_math(w_ref[...], g, m_ref[...], v_ref[...])

    shape = jax.ShapeDtypeStruct((R, C), F32)
    return pl.pallas_call(
        body, name=name, grid=(R // tm,),
        in_specs=[pl.BlockSpec((N_DEV, tm, C), lambda i: (0, i, 0)), tile, tile, tile],
        out_specs=(tile, tile, tile, tile), out_shape=(shape, shape, shape, shape),
        compiler_params=_params(1),
    )(parts, w, m, v)


def _adamw(g, w, m, v, name):
    def body(g_ref, w_ref, m_ref, v_ref, d_out, m_out, v_out):
        d_out[...], m_out[...], v_out[...] = _adamw_math(w_ref[...], g_ref[...], m_ref[...], v_ref[...])

    shape = jax.ShapeDtypeStruct(w.shape, F32)
    return pl.pallas_call(
        body, name=name, in_specs=[VMEM_SPEC] * 4, out_specs=(VMEM_SPEC,) * 3, out_shape=(shape, shape, shape),
    )(g, w, m, v)


def _alibi_slopes():
    return jnp.exp2(-8.0 * jnp.arange(1, N_HEADS + 1, dtype=F32) / N_HEADS)


def _local_step(x2, target, norm_g, b_merge, final_g, w_in, w3_shard, cw_shard):
    slopes = _alibi_slopes()
    u, ut, w_all, w3_all, cw_all = _norm_gather_first_weights(x2, norm_g, w_in, w3_shard, cw_shard)
    o, lse, q, k, v, w_all, w3_all, cw_all = _attn_fwd(u, slopes, w_all, w3_all, cw_all)
    w3 = jnp.transpose(w3_all, (1, 0, 2, 3)).reshape(3, D, D)
    cw8 = jnp.transpose(cw_all, (1, 0, 2)).reshape(8, D)
    pa = _proj_cols(u, w_all, SEG0_CONV, 4, BF, "proj_conv")
    yc_in = _conv_fwd(pa, cw8)
    pa_mid = _proj_cols(u, w_all, SEG0_MID, 3, BF, "proj_mid")
    dh, d_mid, do, dyc_in, g_3, small_mid = _mid(yc_in, pa_mid, o, x2, target, b_merge, final_g, w3)
    g_in = _wgrad_in(ut, d_mid, SEG0_MID, None, "wgrad_in_mid")
    d_conv, small_conv = _conv_bwd(dyc_in, pa, cw8)
    g_in = _wgrad_in(ut, d_conv, SEG0_CONV, g_in, "wgrad_in_conv")
    d_attn, r_in, r_3 = _attn_bwd(q, k, v, slopes, do, o, lse, g_in, g_3)
    g_in = _wgrad_in(ut, d_attn, SEG0_ATTN, g_in, "wgrad_in_attn")
    *in_flight, token = _attn_cols_exchange_start(g_in, r_in)
    grad_x, small_norm = _dgrad_norm_bwd(d_conv, d_attn, d_mid, w_all, x2, dh, norm_g + token[0:1, 0:1])
    return grad_x, in_flight, r_3, small_mid, small_conv, small_norm


def kernel(x, norm_g, w_in, b_merge, conv_w, w_out_conv, w_out_attn, w_o, final_g, loss_target, m_norm_g, m_w_in, m_b_merge, m_conv_w, m_w_out_conv, m_w_out_attn, m_w_o, m_final_g, v_norm_g, v_w_in, v_b_merge, v_conv_w, v_w_out_conv, v_w_out_attn, v_w_o, v_final_g):
    me = 4 * lax.axis_index("x") + 2 * lax.axis_index("y") + lax.axis_index("c")
    stack3 = lambda a, b, c: jnp.concatenate([a, b, c], axis=0)
    pad8 = lambda a: jnp.pad(a, ((0, 8 - a.shape[0]), (0, 0)))

    w3_shard = stack3(w_out_conv, w_out_attn, w_o)
    final_g2 = final_g.reshape(1, D)
    grad_x, in_flight, r_3, small_mid, small_conv, small_norm = _local_step(
        x[0], loss_target[0], norm_g, b_merge, final_g2, w_in[0], w3_shard, pad8(conv_w[0]))

    small = _allreduce_small(small_mid, small_conv, small_norm)
    g_in, r_in = _attn_cols_exchange_wait(*in_flight, small)
    own = lax.dynamic_index_in_dim(g_in, me, 0, keepdims=True)
    r_in = lax.dynamic_update_slice(r_in, own, (me, 0, 0))

    g_w_in, d_w_in, nm_w_in, nv_w_in = _sum_adamw(r_in, w_in[0], m_w_in[0], v_w_in[0], 256, "adamw_w_in")
    g_w3, d_w3, nm_w3, nv_w3 = _sum_adamw(
        r_3.reshape(N_DEV, 3 * ROW_SHARD, D), w3_shard.reshape(3 * ROW_SHARD, D),
        stack3(m_w_out_conv, m_w_out_attn, m_w_o).reshape(3 * ROW_SHARD, D),
        stack3(v_w_out_conv, v_w_out_attn, v_w_o).reshape(3 * ROW_SHARD, D), ROW_SHARD, "adamw_w3")

    def pack(ng, bm, fg):
        return pad8(jnp.concatenate([ng, bm.reshape(2, D), fg.reshape(1, D)], axis=0))

    d_s, nm_s, nv_s = _adamw(small, pack(norm_g, b_merge, final_g), pack(m_norm_g, m_b_merge, m_final_g),
                             pack(v_norm_g, v_b_merge, v_final_g), "adamw_small")
    g_cw = lax.dynamic_slice(small, (4, me * ROW_SHARD), (3, ROW_SHARD))
    d_cw, nm_cw, nv_cw = _adamw(g_cw, conv_w[0], m_conv_w[0], v_conv_w[0], "adamw_conv_w")

    loss = small[7, 0]
    split3 = lambda t: tuple(t[a * ROW_SHARD:(a + 1) * ROW_SHARD][None] for a in range(3))
    unpack = lambda t: (t[0:1], t[1:3].reshape(1, 2 * D), t[3])

    def leaves(in_, small_, cw_, w3_):
        ng, bm, fg = unpack(small_)
        wc, wa, wo = split3(w3_)
        return (ng, in_[None], bm, cw_[None], wc, wa, wo, fg)

    return (loss, grad_x[None],
            *leaves(g_w_in, small, g_cw, g_w3),
            *leaves(d_w_in, d_s, d_cw, d_w3),
            *leaves(nm_w_in, nm_s, nm_cw, nm_w3),
            *leaves(nv_w_in, nv_s, nv_cw, nv_w3))
```

```python
import functools

import jax
import jax.numpy as jnp
from jax import lax
from jax.experimental import pallas as pl
from jax.experimental.pallas import tpu as pltpu

D = 1024
N_HEADS = 16
HEAD_DIM = 64
N_SEG = 10
IN_COLS = N_SEG * D
N_DEV = 8
W_IN_SHARD = IN_COLS // N_DEV
ROW_SHARD = D // N_DEV
QB = 128
DILATIONS = (1, 4, 16)
EPS = 1e-6
NEG = -1e30
BF = jnp.bfloat16
F32 = jnp.float32
MESH = pl.DeviceIdType.MESH

ADAM_LR = 0.001
ADAM_B1 = 0.9
ADAM_B2 = 0.999
ADAM_EPS = 1e-08
ADAM_WD = 0.01
ADAM_STEP = 10

V7X_VMEM_BYTES = 64 * 1024 * 1024
VMEM_LIMIT = V7X_VMEM_BYTES - 8 * 1024 * 1024
ROW_TILE = 256

VMEM_SPEC = pl.BlockSpec(memory_space=pltpu.VMEM)
ANY_SPEC = pl.BlockSpec(memory_space=pl.ANY)
SMEM_SPEC = pl.BlockSpec(memory_space=pltpu.SMEM)


def _params(n_grid_axes, vmem=VMEM_LIMIT):
    return pltpu.CompilerParams(dimension_semantics=("arbitrary",) * n_grid_axes, vmem_limit_bytes=vmem)


def _dot(a, b):
    return jnp.dot(a, b, preferred_element_type=F32)


def _dot_nt(a, b):
    return lax.dot_general(a, b, (((1,), (1,)), ((), ())), preferred_element_type=F32)


def _dot_tn(a, b):
    return lax.dot_general(a, b, (((0,), (0,)), ((), ())), preferred_element_type=F32)


def _sigmoid(z):
    return 1.0 / (1.0 + jnp.exp(-z))


def _my_place():
    x, y, c = lax.axis_index("x"), lax.axis_index("y"), lax.axis_index("c")
    return x, y, c, 4 * x + 2 * y + c


def _peers(x, y, c):
    out = []
    for k in range(1, N_DEV):
        px = 1 - x if k & 4 else x
        py = 1 - y if k & 2 else y
        pc = 1 - c if k & 1 else c
        out.append(((px, py, pc), 4 * px + 2 * py + pc))
    return out


def _device(p):
    return (p >> 2, (p >> 1) & 1, p & 1)


def _shard_cols(*ranges):
    def cols(p):
        found = None
        for lo, hi in ranges:
            a, b = max(lo, p * W_IN_SHARD), min(hi, (p + 1) * W_IN_SHARD)
            if a < b:
                assert found is None
                found = (a - p * W_IN_SHARD, b - p * W_IN_SHARD)
        return found

    return cols


def _whole(p):
    return ()


def _block(ref, idx, cols):
    return ref.at[idx] if cols == () else ref.at[idx, :, cols[0]:cols[1]]


class _WeightGather:
    def __init__(self, src, dst, send_sems, forward_sems, recv_sems, cols):
        self.src, self.dst, self.cols = src, dst, cols
        self.send_sems, self.forward_sems, self.recv_sems = send_sems, forward_sems, recv_sems
        self.me = _my_place()[3]

    def _copy(self, p, target, passing_on=False):
        cols = self.cols(p)
        return pltpu.make_async_remote_copy(
            src_ref=_block(self.dst, p, cols) if passing_on else self.src(p, cols), dst_ref=_block(self.dst, p, cols),
            send_sem=self.forward_sems.at[p] if passing_on else self.send_sems.at[target],
            recv_sem=self.recv_sems.at[p], device_id=_device(target), device_id_type=MESH)

    def _as_each_device(self, own, relayed, other):
        for m in range(N_DEV):
            def branch(m=m):
                for p in range(N_DEV):
                    if self.cols(p) is None:
                        continue
                    if p == m:
                        for t in [m ^ 1] + [q for q in range(N_DEV) if q >> 1 != m >> 1 and q & 1 == m & 1]:
                            own(self._copy(m, t))
                    elif p >> 1 != m >> 1 and p & 1 == m & 1:
                        relayed(p, m ^ 1)
                    else:
                        other(p)

            pl.when(self.me == m)(branch)

    def start(self):
        self._as_each_device(lambda cp: cp.start(), lambda p, t: None, lambda p: None)

    def forward(self):
        def pass_on(p, t):
            self._copy(p, p).wait_recv()
            self._copy(p, t, passing_on=True).start()

        self._as_each_device(lambda cp: None, pass_on, lambda p: None)

    def finish(self):
        self._as_each_device(lambda cp: cp.wait_send(), lambda p, t: self._copy(p, t, passing_on=True).wait_send(),
                             lambda p: self._copy(p, p).wait_recv())


WEIGHT_GATHER_SEMS = [pltpu.SemaphoreType.DMA((N_DEV,))] * 3
FORWARD_STEP = 6
REST_COLS = _shard_cols((0, 4 * D), (7 * D, IN_COLS))
HEAD_PAIRS = D // 128


def _qkv_piece(h, seg):
    col = (4 + seg) * D + 128 * h
    return col // W_IN_SHARD, col % W_IN_SHARD


class _PieceGather:
    def __init__(self, src, dst, send_sems, recv_sems):
        self.src, self.dst, self.send_sems, self.recv_sems = src, dst, send_sems, recv_sems
        self.me = _my_place()[3]

    def _copy(self, i, target):
        p, lo = _qkv_piece(i // 3, i % 3)
        return pltpu.make_async_remote_copy(
            src_ref=self.src(p, lo, lo + 128), dst_ref=self.dst.at[p, :, lo:lo + 128], send_sem=self.send_sems.at[i, target],
            recv_sem=self.recv_sems.at[i], device_id=_device(target), device_id_type=MESH)

    def _owner(self, i, act):
        p = _qkv_piece(i // 3, i % 3)[0]

        def sender():
            for k in range(N_DEV - 1):
                act(self._copy(i, (p + 1 + (k + i) % (N_DEV - 1)) % N_DEV))

        pl.when(self.me == p)(sender)

    def start(self, pieces):
        for i in pieces:
            self._owner(i, lambda cp: cp.start())

    def wait_send(self, pieces):
        for i in pieces:
            self._owner(i, lambda cp: cp.wait_send())

    def wait_recv(self, pieces):
        for i in pieces:
            p = _qkv_piece(i // 3, i % 3)[0]
            pl.when(self.me != p)(lambda i=i, p=p: self._copy(i, p).wait_recv())


def _piece_sems(n):
    return [pltpu.SemaphoreType.DMA((n, N_DEV)), pltpu.SemaphoreType.DMA((n,))]


def _norm_gather_first_weights(x2, norm_g, w_in, w3, cw):
    S = x2.shape[0]
    tm = ROW_TILE
    nsteps = S // tm

    def body(x_ref, g_ref, w_in_ref, w3_ref, cw_ref, u_ref, ut_ref, o_in, o_3, o_cw, in_bf, w3_bf, local_sems, *sems):
        i = pl.program_id(0)
        me = _my_place()[3]
        gather = _PieceGather(lambda p, lo, hi: in_bf.at[:, lo:hi], o_in, *sems)
        local = [pltpu.make_async_copy(src, dst.at[me], local_sems.at[a])
                 for a, (src, dst) in enumerate(((in_bf, o_in), (w3_bf, o_3), (cw_ref, o_cw)))]

        @pl.when(i == 0)
        def _():
            def cast_rows(r, carry):
                rows = pl.ds(pl.multiple_of(r * 128, 128), 128)
                in_bf[rows, :] = w_in_ref[rows, :].astype(BF)
                return carry

            lax.fori_loop(0, D // 128, cast_rows, 0)
            for a in range(3):
                w3_bf[a] = w3_ref[a].astype(BF)
            gather.start(range(3))
            for cp in local:
                cp.start()

        xv = x_ref[...]
        r = lax.rsqrt(jnp.mean(xv * xv, axis=-1, keepdims=True) + EPS)
        u = xv * r * g_ref[...]
        u_ref[...] = u.astype(BF)
        ut_ref[...] = u.T.astype(BF)

        @pl.when(i == nsteps - 1)
        def _():
            gather.wait_recv(range(3))
            gather.wait_send(range(3))
            for cp in local:
                cp.wait()

    return pl.pallas_call(
        body, name="norm_gather_first_weights", grid=(nsteps,),
        out_shape=(jax.ShapeDtypeStruct((S, D), BF), jax.ShapeDtypeStruct((D, S), BF),
                   jax.ShapeDtypeStruct((N_DEV, D, W_IN_SHARD), BF),
                   jax.ShapeDtypeStruct((N_DEV, 3, ROW_SHARD, D), BF),
                   jax.ShapeDtypeStruct((N_DEV, 8, 128), F32)),
        in_specs=[pl.BlockSpec((tm, D), lambda i: (i, 0)), pl.BlockSpec((1, D), lambda i: (0, 0)),
                  VMEM_SPEC, VMEM_SPEC, VMEM_SPEC],
        out_specs=(pl.BlockSpec((tm, D), lambda i: (i, 0)), pl.BlockSpec((D, tm), lambda i: (0, i)),
                   ANY_SPEC, ANY_SPEC, ANY_SPEC),
        scratch_shapes=[pltpu.VMEM((D, W_IN_SHARD), BF), pltpu.VMEM((3, ROW_SHARD, D), BF),
                        pltpu.SemaphoreType.DMA((3,))] + _piece_sems(3),
        compiler_params=_params(1),
    )(x2, norm_g, w_in, w3, cw)


class _GradExchange:
    def __init__(self, src, dst, send_sems, recv_sems, local_sem, cols):
        self.src, self.dst, self.cols = src, dst, cols
        self.send_sems, self.recv_sems, self.local_sem = send_sems, recv_sems, local_sem
        self.me = _my_place()[3]

    def _remote(self, p, source):
        return pltpu.make_async_remote_copy(
            src_ref=_block(self.src, p, self.cols(p)), dst_ref=_block(self.dst, source, self.cols(p)),
            send_sem=self.send_sems.at[p], recv_sem=self.recv_sems.at[source],
            device_id=_device(p), device_id_type=MESH)

    def _local(self, p):
        return pltpu.make_async_copy(_block(self.src, p, self.cols(p)), _block(self.dst, p, self.cols(p)),
                                     self.local_sem)

    def _as_each_device(self, send, local, receive):
        for m in range(N_DEV):
            def branch(m=m):
                for k in range(1, N_DEV):
                    p = (m + k) % N_DEV
                    if self.cols(p) is not None:
                        send(self._remote(p, m))
                if self.cols(m) is not None:
                    if self.local_sem is not None:
                        local(self._local(m))
                    for k in range(1, N_DEV):
                        receive(self._remote(m, (m + k) % N_DEV))

            pl.when(self.me == m)(branch)

    def start(self):
        self._as_each_device(lambda cp: cp.start(), lambda cp: cp.start(), lambda cp: None)

    def finish(self):
        self._as_each_device(lambda cp: cp.wait_send(), lambda cp: cp.wait(), lambda cp: cp.wait_recv())


GRAD_EXCHANGE_SEMS = [pltpu.SemaphoreType.DMA((N_DEV,)), pltpu.SemaphoreType.DMA((N_DEV,)), pltpu.SemaphoreType.DMA]


def _allreduce_small(p_mid, p_conv, p_norm):
    def body(a_ref, b_ref, c_ref, out_ref, mine, gathered, send_sems, recv_sems):
        x, y, c, me = _my_place()
        mine[...] = a_ref[...] + b_ref[...] + c_ref[...]
        gathered[me] = mine[...]
        remote = []
        for k, (peer, _) in enumerate(_peers(x, y, c)):
            cp = pltpu.make_async_remote_copy(
                src_ref=mine, dst_ref=gathered.at[me], send_sem=send_sems.at[k], recv_sem=recv_sems.at[k],
                device_id=peer, device_id_type=MESH)
            cp.start()
            remote.append(cp)
        for cp in remote:
            cp.wait()
        total = gathered[0]
        for s in range(1, N_DEV):
            total = total + gathered[s]
        out_ref[...] = total

    return pl.pallas_call(
        body, name="allreduce_small",
        out_shape=jax.ShapeDtypeStruct((8, D), F32),
        in_specs=[VMEM_SPEC, VMEM_SPEC, VMEM_SPEC], out_specs=VMEM_SPEC,
        scratch_shapes=[pltpu.VMEM((8, D), F32), pltpu.VMEM((N_DEV, 8, D), F32),
                        pltpu.SemaphoreType.DMA((N_DEV - 1,)), pltpu.SemaphoreType.DMA((N_DEV - 1,))],
    )(p_mid, p_conv, p_norm)


def _proj_pieces():
    cuts = sorted(set(range(0, IN_COLS + 1, D)) | set(range(0, IN_COLS + 1, W_IN_SHARD)))
    return [(lo // D, lo % D, lo // W_IN_SHARD, lo % W_IN_SHARD, hi - lo) for lo, hi in zip(cuts[:-1], cuts[1:])]


PROJ_TN = 256


def _proj_cols(u, w_all, seg0, n_seg, dtype, name):
    S = u.shape[0]
    tn = PROJ_TN
    per_shard = W_IN_SHARD // tn
    tile0 = seg0 * D // tn

    def body(u_ref, w_ref, out_ref):
        out_ref[...] = _dot(u_ref[...], w_ref[0]).astype(dtype)

    return pl.pallas_call(
        body, name=name, grid=(n_seg * D // tn,),
        in_specs=[VMEM_SPEC, pl.BlockSpec((1, D, tn), lambda t: ((tile0 + t) // per_shard, 0, (tile0 + t) % per_shard))],
        out_specs=pl.BlockSpec((S, tn), lambda t: (0, t)),
        out_shape=jax.ShapeDtypeStruct((S, n_seg * D), dtype),
        compiler_params=_params(1),
    )(u, w_all)


CONV_TM, CONV_TC = 1024, 512
HALO = 16


def _conv_fwd(pa, cw8):
    S = pa.shape[0]
    tm, tc = CONV_TM, CONV_TC
    nct = D // tc

    def seg(s):
        return pl.BlockSpec((tm, tc), lambda i, j, s=s: (i, s * nct + j))

    def halo_before(s):
        return pl.BlockSpec((HALO, tc), lambda i, j, s=s: (jnp.maximum(i * (tm // HALO) - 1, 0), s * nct + j))

    def body(xc, bg, cg, zc, xch, cgh, cw, out):
        i = pl.program_id(0)
        a = cg[...].astype(F32) * xc[...].astype(F32)
        ah = cgh[...].astype(F32) * xch[...].astype(F32)
        ah = jnp.where(i > 0, ah, 0.0)
        row = lax.broadcasted_iota(jnp.int32, (tm, tc), 0)
        a1 = jnp.where(row == 0, ah[HALO - 1:HALO, :], pltpu.roll(a, 1, 0))
        a2 = jnp.where(row == 0, ah[HALO - 2:HALO - 1, :],
                       jnp.where(row == 1, ah[HALO - 1:HALO, :], pltpu.roll(a, 2, 0)))
        w = cw[...]
        conv = w[0:1, :] * a2 + w[1:2, :] * a1 + w[2:3, :] * a
        z = zc[...].astype(F32)
        out[...] = (z * _sigmoid(z) * bg[...].astype(F32) * conv).astype(BF)

    return pl.pallas_call(
        body, name="conv_fwd", grid=(S // tm, nct),
        in_specs=[seg(0), seg(1), seg(2), seg(3), halo_before(0), halo_before(2),
                  pl.BlockSpec((8, tc), lambda i, j: (0, j))],
        out_specs=pl.BlockSpec((tm, tc), lambda i, j: (i, j)),
        out_shape=jax.ShapeDtypeStruct((S, D), BF),
        compiler_params=_params(2),
    )(pa, pa, pa, pa, pa, pa, cw8)


ATT_UNROLL = 32


LAYOUT_MOD = 4
RUN = QB // LAYOUT_MOD


def _fold_masks(d):
    row = lax.broadcasted_iota(jnp.int32, (QB, QB), 0)
    lane = lax.broadcasted_iota(jnp.int32, (QB, QB), 1)
    if d == 1:
        qpos, kpos = LAYOUT_MOD * (row % RUN) + row // RUN, LAYOUT_MOD * (lane % RUN) + lane // RUN
    else:
        qpos, kpos = row, lane
    tri_le = kpos <= qpos
    dist = jnp.where(tri_le, qpos - kpos, qpos - kpos + QB).astype(F32)
    return tri_le, dist, lane < HEAD_DIM


class _Rows:
    def __init__(self, slices):
        self.slices = slices

    def get(self, ref):
        parts = [ref[sl, :] for sl in self.slices]
        return parts[0] if len(parts) == 1 else jnp.concatenate(parts, axis=0)

    def put(self, ref, val):
        size = QB // len(self.slices)
        for g, sl in enumerate(self.slices):
            ref[sl, :] = val if len(self.slices) == 1 else val[g * size:(g + 1) * size]

    def add(self, ref, val):
        self.put(ref, self.get(ref) + val)


def _block_rows(b, d, S):
    quarter = S // LAYOUT_MOD
    nb = S // (QB * d)
    r, n = b // nb, b % nb
    n_prev = jnp.maximum(n - 1, 0)
    if d == 1:
        runs = lambda m: _Rows([pl.ds(pl.multiple_of(g * quarter + RUN * m, RUN), RUN) for g in range(LAYOUT_MOD)])
        return n, runs(n), runs(n_prev)
    if d == LAYOUT_MOD:
        block = lambda m: _Rows([pl.ds(pl.multiple_of(r * quarter + QB * m, QB), QB)])
        return n, block(n), block(n_prev)
    step = d // LAYOUT_MOD
    first = (r % LAYOUT_MOD) * quarter + r // LAYOUT_MOD
    strided = lambda m: _Rows([pl.ds(first + QB * step * m, QB, stride=step)])
    return n, strided(n), strided(n_prev)


def _natural_rows(i, S):
    per = S // LAYOUT_MOD // QB
    return pl.ds(i // per + LAYOUT_MOD * QB * (i % per), QB, stride=LAYOUT_MOD)


def _head_sum_matrix():
    r = lax.broadcasted_iota(jnp.int32, (2 * QB, 2 * QB), 0)
    c = lax.broadcasted_iota(jnp.int32, (2 * QB, 2 * QB), 1)
    return (((r % QB) // HEAD_DIM) == (c // QB)).astype(F32).astype(BF)


def _hi_lo(t):
    hi = t.astype(BF)
    return jnp.concatenate([hi, (t - hi.astype(F32)).astype(BF)], axis=1)


PROJ_ROWS = 1024


def _attn_fwd(u, slopes, w_all, w3_all, cw_all):
    S = u.shape[0]
    hpr = HEAD_PAIRS
    n_blocks = S // QB
    later = range(3, 3 * hpr)

    def body(sl_ref, u_ref, w_in_ref, w3_in_ref, cw_in_ref, o_ref, lse_ref, q_ref, k_ref, v_ref, w_ref, w3_ref,
             cw_ref, acc, m_s, l_s, w_tile, staged, tile_sems, *sems):
        hp = pl.program_id(0)
        me = _my_place()[3]
        pieces = _PieceGather(lambda p, lo, hi: w_ref.at[p, :, lo:hi], w_ref, *sems[0:2])
        gathers = (_WeightGather(lambda p, cols: _block(w_ref, me, cols), w_ref, *sems[2:5], REST_COLS),
                   _WeightGather(lambda p, cols: w3_ref.at[me], w3_ref, *sems[5:8], _whole),
                   _WeightGather(lambda p, cols: cw_ref.at[me], cw_ref, *sems[8:11], _whole))

        @pl.when(hp == 0)
        def _():
            pieces.start(later)
            for g in gathers:
                g.start()

        @pl.when(hp == FORWARD_STEP)
        def _():
            for g in gathers:
                g.forward()

        for h in range(hpr):
            @pl.when(hp == h)
            def _(h=h):
                if h > 0:
                    pieces.wait_recv(range(3 * h, 3 * h + 3))
                fetch = []
                for seg in range(3):
                    p, lo = _qkv_piece(h, seg)
                    fetch.append(pltpu.make_async_copy(w_ref.at[p, :, lo:lo + 128], w_tile.at[:, seg * 128:(seg + 1) * 128],
                                                       tile_sems.at[seg]))
                    fetch[-1].start(priority=1)
                for cp in fetch:
                    cp.wait()

        def project(i, carry):
            rows = pl.ds(pl.multiple_of(i * PROJ_ROWS, PROJ_ROWS), PROJ_ROWS)
            qkv = _dot(u_ref[rows, :], w_tile[...])
            per = PROJ_ROWS // LAYOUT_MOD
            for seg, ref in enumerate((q_ref, k_ref, v_ref)):
                staged[seg] = qkv[:, seg * 128:(seg + 1) * 128]
                for g in range(LAYOUT_MOD):
                    dst = pl.ds(pl.multiple_of(g * (S // LAYOUT_MOD) + i * per, per), per)
                    ref[dst, :] = staged.at[seg][pl.ds(g, per, stride=LAYOUT_MOD), :]
            return carry

        lax.fori_loop(0, S // PROJ_ROWS, project, 0)

        head_sum = _head_sum_matrix()
        ones_b = jnp.ones((2 * QB, QB), BF)
        m_s[...] = jnp.full(m_s.shape, NEG, F32)
        l_s[...] = jnp.zeros(l_s.shape, F32)
        acc[...] = jnp.zeros(acc.shape, F32)

        for d in DILATIONS:
            tri_le, dist, low = _fold_masks(d)
            low_b = low.astype(F32).astype(BF)
            high_b = 1.0 - low_b
            slope = [sl_ref[2 * hp + a] * float(d) for a in range(2)]
            bias = [slope[a] * dist for a in range(2)]

            def block(b, d=d, slope=slope, bias=bias, tri_le=tri_le, low=low, low_b=low_b, high_b=high_b):
                n, cur, prev = _block_rows(b, d, S)
                has_prev = n > 0
                valid = jnp.logical_or(tri_le, has_prev)
                q2 = (cur.get(q_ref) * 0.125).astype(BF)
                qs = jnp.concatenate([q2 * low_b, q2 * high_b], axis=0)
                vp = prev.get(v_ref)
                kp_b = prev.get(k_ref).astype(BF)
                kcat = jnp.concatenate([kp_b, cur.get(k_ref).astype(BF)], axis=0)
                vcat = jnp.concatenate([vp, cur.get(v_ref)], axis=0).astype(BF)
                s2 = _dot_nt(qs, kcat)
                e2 = _dot(_hi_lo(q2.astype(F32) * kp_b.astype(F32)), head_sum)
                p_rows, alpha_h, pe_h = [], [], []
                for a in range(2):
                    sp, sc = s2[a * QB:(a + 1) * QB, :QB], s2[a * QB:(a + 1) * QB, QB:]
                    comb = jnp.where(valid, jnp.where(tri_le, sc, sp) - bias[a], NEG)
                    e = jnp.where(has_prev, e2[:, a * QB:(a + 1) * QB] - slope[a] * float(QB), NEG)
                    m_old = cur.get(m_s.at[a])
                    m_new = jnp.maximum(jnp.maximum(m_old, jnp.max(comb, axis=-1, keepdims=True)), e)
                    cur.put(m_s.at[a], m_new)
                    p = jnp.exp(comb - m_new)
                    pe_h.append(jnp.exp(e - m_new))
                    alpha_h.append(jnp.exp(m_old - m_new))
                    p_rows.append(jnp.concatenate([jnp.where(tri_le, 0.0, p).astype(BF),
                                                   jnp.where(tri_le, p, 0.0).astype(BF)], axis=1))
                pv = _dot(jnp.concatenate(p_rows, axis=0), jnp.concatenate([vcat, ones_b], axis=1))
                for a in range(2):
                    cur.put(l_s.at[a], alpha_h[a] * cur.get(l_s.at[a]) + pv[a * QB:(a + 1) * QB, QB:] + pe_h[a])
                cur.put(acc, jnp.where(low, alpha_h[0], alpha_h[1]) * cur.get(acc)
                        + jnp.where(low, pv[:QB, :QB], pv[QB:, :QB]) + jnp.where(low, pe_h[0], pe_h[1]) * vp)

            def several(it, carry, block=block):
                for u in range(ATT_UNROLL):
                    block(it * ATT_UNROLL + u)
                return carry

            lax.fori_loop(0, n_blocks // ATT_UNROLL, several, 0)

        low = _fold_masks(LAYOUT_MOD)[2]

        def finish(i, carry):
            rows = pl.ds(pl.multiple_of(i * QB, QB), QB)
            l0, l1 = l_s[0, rows, :], l_s[1, rows, :]
            o_ref[_natural_rows(i, S), :] = acc[rows, :] / jnp.where(low, l0, l1)
            lse_ref[0, rows, :] = m_s[0, rows, :] + jnp.log(l0)
            lse_ref[1, rows, :] = m_s[1, rows, :] + jnp.log(l1)
            return carry

        lax.fori_loop(0, n_blocks, finish, 0)

        @pl.when(hp == hpr - 1)
        def _():
            pieces.wait_send(later)
            for g in gathers:
                g.finish()

    col = pl.BlockSpec((S, 128), lambda h: (0, h))
    act = jax.ShapeDtypeStruct((S, D), F32)
    gathered = (w_all, w3_all, cw_all)
    return pl.pallas_call(
        body, name="attn_fwd", grid=(hpr,),
        in_specs=[SMEM_SPEC, VMEM_SPEC, ANY_SPEC, ANY_SPEC, ANY_SPEC],
        out_specs=(col, pl.BlockSpec((2, S, 128), lambda h: (0, 0, h)), col, col, col, ANY_SPEC, ANY_SPEC, ANY_SPEC),
        out_shape=(act, jax.ShapeDtypeStruct((2, S, D), F32), act, act, act,
                   *[jax.ShapeDtypeStruct(t.shape, t.dtype) for t in gathered]),
        scratch_shapes=([pltpu.VMEM((S, 128), F32), pltpu.VMEM((2, S, 128), F32), pltpu.VMEM((2, S, 128), F32),
                         pltpu.VMEM((D, 3 * 128), BF), pltpu.VMEM((3, PROJ_ROWS, 128), F32),
                         pltpu.SemaphoreType.DMA((3,))]
                        + _piece_sems(3 * hpr) + WEIGHT_GATHER_SEMS * 3),
        input_output_aliases={2: 5, 3: 6, 4: 7},
        compiler_params=_params(1),
    )(slopes, u, *gathered)


def _set_rows(shape, rows):
    idx = lax.broadcasted_iota(jnp.int32, shape, 0)
    out = jnp.zeros(shape, F32)
    for r, val in rows.items():
        out = out + jnp.where(idx == r, val, 0.0)
    return out


def _mid(yc_in, pa_mid, o, x2, target, b_merge, final_g, w3):
    S = x2.shape[0]
    tm = ROW_TILE
    nsteps = S // tm
    tile = pl.BlockSpec((tm, D), lambda i: (i, 0))

    def body(yc_ref, za_ref, gcp_ref, gap_ref, o_ref, x_ref, t_ref, b_ref, fg_ref, w_ref,
             dh_ref, dmid_ref, do_ref, dyc_ref, gw_ref, small_ref, acc, stage):
        i = pl.program_id(0)

        @pl.when(i == 0)
        def _():
            acc[...] = jnp.zeros_like(acc)
            small_ref[...] = jnp.zeros_like(small_ref)

        wc, wa, wo = w_ref[0], w_ref[1], w_ref[2]
        z = za_ref[...].astype(F32)
        sg = _sigmoid(z)
        ov = o_ref[...]
        yc_in_b, ya_in_b = yc_ref[...], (z * sg * ov).astype(BF)
        yc = _dot(yc_in_b, wc)
        ya = _dot(ya_in_b, wa)
        b = b_ref[...]
        gc = _sigmoid(gcp_ref[...].astype(F32) + b[:, :D])
        ga = _sigmoid(gap_ref[...].astype(F32) + b[:, D:])
        merged = gc * yc + ga * ya
        merged_b = merged.astype(BF)
        h = x_ref[...] + _dot(merged_b, wo)
        r2 = lax.rsqrt(jnp.mean(h * h, axis=-1, keepdims=True) + EPS)
        n = h * r2
        fg = fg_ref[...]
        err = n * fg - t_ref[...]
        loss = 0.5 * jnp.sum(jnp.sum(err * err, axis=-1, keepdims=True) / D, axis=0, keepdims=True)
        dy = err / D
        g_fg = jnp.sum(dy * n, axis=0, keepdims=True)
        dn = dy * fg
        dh = r2 * (dn - n * jnp.mean(dn * n, axis=-1, keepdims=True))
        dh_ref[...] = dh
        dh_b = dh.astype(BF)
        dmerged = _dot_nt(dh_b, wo)
        acc[2] += _dot(merged.T.astype(BF), dh_b)
        dyc = (dmerged * gc).astype(BF)
        dya = (dmerged * ga).astype(BF)
        dgcp = dmerged * yc * gc * (1.0 - gc)
        dgap = dmerged * ya * ga * (1.0 - ga)
        dmid_ref[1] = dgcp.astype(BF)
        dmid_ref[2] = dgap.astype(BF)
        acc[0] += _dot(yc_in_b.astype(F32).T.astype(BF), dyc)
        acc[1] += _dot(ya_in_b.astype(F32).T.astype(BF), dya)
        dyc_ref[...] = _dot_nt(dyc, wc).astype(BF)
        dya_in = _dot_nt(dya, wa)
        do_ref[...] = dya_in * (z * sg)
        dmid_ref[0] = (dya_in * ov * (sg * (1.0 + z * (1.0 - sg)))).astype(BF)
        small_ref[...] += _set_rows((8, D), {
            1: jnp.sum(dgcp, axis=0, keepdims=True), 2: jnp.sum(dgap, axis=0, keepdims=True),
            3: g_fg, 7: jnp.broadcast_to(loss, (1, D))})

        @pl.when(i == nsteps - 1)
        def _():
            for p in range(N_DEV):
                for a in range(3):
                    stage[...] = acc[a, p * ROW_SHARD:(p + 1) * ROW_SHARD, :].astype(BF)
                    pltpu.sync_copy(stage, gw_ref.at[p, a])

    return pl.pallas_call(
        body, name="mid", grid=(nsteps,),
        in_specs=[tile, pl.BlockSpec((tm, D), lambda i: (i, 0)), pl.BlockSpec((tm, D), lambda i: (i, 1)),
                  pl.BlockSpec((tm, D), lambda i: (i, 2)), tile, tile, tile,
                  pl.BlockSpec((1, 2 * D), lambda i: (0, 0)), pl.BlockSpec((1, D), lambda i: (0, 0)), VMEM_SPEC],
        out_specs=(tile, pl.BlockSpec((3, tm, D), lambda i: (0, i, 0)), tile, tile,
                   ANY_SPEC, pl.BlockSpec((8, D), lambda i: (0, 0))),
        out_shape=(jax.ShapeDtypeStruct((S, D), F32), jax.ShapeDtypeStruct((3, S, D), BF),
                   jax.ShapeDtypeStruct((S, D), F32), jax.ShapeDtypeStruct((S, D), BF),
                   jax.ShapeDtypeStruct((N_DEV, 3, ROW_SHARD, D), BF), jax.ShapeDtypeStruct((8, D), F32)),
        scratch_shapes=[pltpu.VMEM((3, D, D), F32), pltpu.VMEM((ROW_SHARD, D), BF)],
        compiler_params=_params(1),
    )(yc_in, pa_mid, pa_mid, pa_mid, o, x2, target, b_merge, final_g, w3)


def _conv_bwd(dyc_in, pa, cw8):
    S = pa.shape[0]
    tm, tc = CONV_TM, CONV_TC
    nct = D // tc
    nrt = S // tm
    last_halo = S // HALO - 1

    def seg(s):
        return pl.BlockSpec((tm, tc), lambda j, i, s=s: (i, s * nct + j))

    def halo_before(s):
        return pl.BlockSpec((HALO, tc), lambda j, i, s=s: (jnp.maximum(i * (tm // HALO) - 1, 0), s * nct + j))

    def halo_after(s):
        return pl.BlockSpec((HALO, tc), lambda j, i, s=s: (jnp.minimum((i + 1) * (tm // HALO), last_halo), s * nct + j))

    def body(dy, xc, bg, cg, zc, xch, cgh, dyn, bgn, zcn, cw, dout, gcw):
        i = pl.program_id(1)

        @pl.when(i == 0)
        def _():
            gcw[...] = jnp.zeros_like(gcw)

        xcv, cgv = xc[...].astype(F32), cg[...].astype(F32)
        a = cgv * xcv
        ah = jnp.where(i > 0, cgh[...].astype(F32) * xch[...].astype(F32), 0.0)
        row = lax.broadcasted_iota(jnp.int32, (tm, tc), 0)
        a1 = jnp.where(row == 0, ah[HALO - 1:HALO, :], pltpu.roll(a, 1, 0))
        a2 = jnp.where(row == 0, ah[HALO - 2:HALO - 1, :],
                       jnp.where(row == 1, ah[HALO - 1:HALO, :], pltpu.roll(a, 2, 0)))
        w = cw[...]
        conv = w[0:1, :] * a2 + w[1:2, :] * a1 + w[2:3, :] * a
        z = zc[...].astype(F32)
        sg = _sigmoid(z)
        silu = z * sg
        bgv = bg[...].astype(F32)
        dyv = dy[...].astype(F32)
        dout[3] = (dyv * bgv * conv * (sg * (1.0 + z * (1.0 - sg)))).astype(BF)
        dout[1] = (dyv * silu * conv).astype(BF)
        dc = dyv * silu * bgv
        zn = zcn[...].astype(F32)
        dcn = dyn[...].astype(F32) * (zn * _sigmoid(zn)) * bgn[...].astype(F32)
        dcn = jnp.where(i < nrt - 1, dcn, 0.0)
        dc1 = jnp.where(row == tm - 1, dcn[0:1, :], pltpu.roll(dc, tm - 1, 0))
        dc2 = jnp.where(row == tm - 1, dcn[1:2, :],
                        jnp.where(row == tm - 2, dcn[0:1, :], pltpu.roll(dc, tm - 2, 0)))
        da = w[2:3, :] * dc + w[1:2, :] * dc1 + w[0:1, :] * dc2
        dout[2] = (da * xcv).astype(BF)
        dout[0] = (da * cgv).astype(BF)
        gcw[...] += _set_rows((8, tc), {
            4: jnp.sum(dc * a2, axis=0, keepdims=True), 5: jnp.sum(dc * a1, axis=0, keepdims=True),
            6: jnp.sum(dc * a, axis=0, keepdims=True)})

    return pl.pallas_call(
        body, name="conv_bwd", grid=(nct, nrt),
        in_specs=[pl.BlockSpec((tm, tc), lambda j, i: (i, j)), seg(0), seg(1), seg(2), seg(3),
                  halo_before(0), halo_before(2),
                  pl.BlockSpec((HALO, tc), lambda j, i: (jnp.minimum((i + 1) * (tm // HALO), last_halo), j)),
                  halo_after(1), halo_after(3), pl.BlockSpec((8, tc), lambda j, i: (0, j))],
        out_specs=(pl.BlockSpec((4, tm, tc), lambda j, i: (0, i, j)), pl.BlockSpec((8, tc), lambda j, i: (0, j))),
        out_shape=(jax.ShapeDtypeStruct((4, S, D), BF), jax.ShapeDtypeStruct((8, D), F32)),
        compiler_params=_params(2),
    )(dyc_in, pa, pa, pa, pa, pa, pa, dyc_in, pa, pa, cw8)


def _attn_bwd(q, k, v, slopes, do, o, lse, g_in, g_3):
    S = q.shape[0]
    hpr = HEAD_PAIRS
    n_blocks = S // QB

    def body(sl_ref, q_ref, k_ref, v_ref, do_ref, o_ref, lse_ref, gin_ref, g3_ref, out_ref, rin_ref, r3_ref,
             dq_s, dk_s, dv_s, do_s, dd_s, *sems):
        hp = pl.program_id(0)
        exchanges = (_GradExchange(gin_ref, rin_ref, *sems[:3], _shard_cols((0, SEG0_ATTN * D), (SEG0_MID * D, IN_COLS))),
                     _GradExchange(g3_ref, r3_ref, *sems[3:], _whole))

        @pl.when(hp == 0)
        def _():
            for ex in exchanges:
                ex.start()

        head_sum = _head_sum_matrix()
        dq_s[...] = jnp.zeros(dq_s.shape, F32)
        dk_s[...] = jnp.zeros(dk_s.shape, F32)
        dv_s[...] = jnp.zeros(dv_s.shape, F32)

        def row_dots(i, carry):
            rows = pl.ds(pl.multiple_of(i * QB, QB), QB)
            natural = _natural_rows(i, S)
            do_c = do_ref[natural, :]
            do_s[rows, :] = do_c
            dd = _dot(_hi_lo(do_c * o_ref[natural, :]), head_sum)
            dd_s[0, rows, :] = dd[:, :QB]
            dd_s[1, rows, :] = dd[:, QB:]
            return carry

        lax.fori_loop(0, n_blocks, row_dots, 0)

        for d in DILATIONS:
            tri_le, dist, low = _fold_masks(d)
            low_b = low.astype(F32).astype(BF)
            high_b = 1.0 - low_b
            slope = [sl_ref[2 * hp + a] * float(d) for a in range(2)]
            bias = [slope[a] * dist for a in range(2)]

            def block(b, d=d, slope=slope, bias=bias, tri_le=tri_le, low=low, low_b=low_b, high_b=high_b):
                n, cur, prev = _block_rows(b, d, S)
                has_prev = n > 0
                valid = jnp.logical_or(tri_le, has_prev)
                q2f = cur.get(q_ref) * 0.125
                q2 = q2f.astype(BF)
                qs = jnp.concatenate([q2 * low_b, q2 * high_b], axis=0)
                kp, vp = prev.get(k_ref), prev.get(v_ref)
                kp_b, vp_b = kp.astype(BF), vp.astype(BF)
                kcat = jnp.concatenate([kp_b, cur.get(k_ref).astype(BF)], axis=0)
                vcat = jnp.concatenate([vp_b, cur.get(v_ref).astype(BF)], axis=0)
                do2f = cur.get(do_s)
                do2 = do2f.astype(BF)
                dos = jnp.concatenate([do2 * low_b, do2 * high_b], axis=0)
                s2 = _dot_nt(qs, kcat)
                dp2 = _dot_nt(dos, vcat)
                diag2 = _dot(jnp.concatenate([_hi_lo(q2.astype(F32) * kp_b.astype(F32)),
                                              _hi_lo(do2.astype(F32) * vp_b.astype(F32))], axis=0), head_sum)
                p_rows, ds_rows, pe_h, dse_h = [], [], [], []
                for a in range(2):
                    hs = slice(a * QB, (a + 1) * QB)
                    sp, sc = s2[hs, :QB], s2[hs, QB:]
                    dpp, dpc = dp2[hs, :QB], dp2[hs, QB:]
                    lse_a, dd_a = cur.get(lse_ref.at[a]), cur.get(dd_s.at[a])
                    comb = jnp.where(tri_le, sc, sp) - bias[a]
                    e = diag2[:QB, hs] - slope[a] * float(QB)
                    p = jnp.where(valid, jnp.exp(comb - lse_a), 0.0)
                    pe = jnp.where(has_prev, jnp.exp(e - lse_a), 0.0)
                    ds = p * (jnp.where(tri_le, dpc, dpp) - dd_a)
                    dse_h.append(pe * (diag2[QB:, hs] - dd_a))
                    pe_h.append(pe)
                    p_rows.append(jnp.concatenate([jnp.where(tri_le, 0.0, p).astype(BF),
                                                   jnp.where(tri_le, p, 0.0).astype(BF)], axis=1))
                    ds_rows.append(jnp.concatenate([jnp.where(tri_le, 0.0, ds).astype(BF),
                                                    jnp.where(tri_le, ds, 0.0).astype(BF)], axis=1))
                pst = jnp.concatenate(p_rows, axis=0)
                dst = jnp.concatenate(ds_rows, axis=0)
                pe2 = jnp.where(low, pe_h[0], pe_h[1])
                dse2 = jnp.where(low, dse_h[0], dse_h[1])
                dq = _dot(dst, kcat)
                cur.add(dq_s, (jnp.where(low, dq[:QB], dq[QB:]) + dse2 * kp) * 0.125)
                dk = _dot_tn(dst, qs)
                dv = _dot_tn(pst, dos)
                prev.add(dk_s, dk[:QB] + dse2 * q2f)
                cur.add(dk_s, dk[QB:])
                prev.add(dv_s, dv[:QB] + pe2 * do2f)
                cur.add(dv_s, dv[QB:])

            def several(it, carry, block=block):
                for u in range(ATT_UNROLL):
                    block(it * ATT_UNROLL + u)
                return carry

            lax.fori_loop(0, n_blocks // ATT_UNROLL, several, 0)

        def finish(i, carry):
            rows = pl.ds(pl.multiple_of(i * QB, QB), QB)
            natural = _natural_rows(i, S)
            for t, ref in enumerate((dq_s, dk_s, dv_s)):
                out_ref.at[t][natural, :] = ref[rows, :]
            return carry

        lax.fori_loop(0, n_blocks, finish, 0)

        @pl.when(hp == hpr - 1)
        def _():
            for ex in exchanges:
                ex.finish()

    col = pl.BlockSpec((S, 128), lambda h: (0, h))
    return pl.pallas_call(
        body, name="attn_bwd", grid=(hpr,),
        in_specs=[SMEM_SPEC, col, col, col, col, col, pl.BlockSpec((2, S, 128), lambda h: (0, 0, h)),
                  ANY_SPEC, ANY_SPEC],
        out_specs=(pl.BlockSpec((3, S, 128), lambda h: (0, 0, h)), ANY_SPEC, ANY_SPEC),
        out_shape=(jax.ShapeDtypeStruct((3, S, D), F32), jax.ShapeDtypeStruct(g_in.shape, BF),
                   jax.ShapeDtypeStruct(g_3.shape, BF)),
        scratch_shapes=([pltpu.VMEM((S, 128), F32)] * 4 + [pltpu.VMEM((2, S, 128), F32)]
                        + GRAD_EXCHANGE_SEMS + GRAD_EXCHANGE_SEMS),
        compiler_params=_params(1),
    )(slopes, q, k, v, do, o, lse, g_in, g_3)


WG_TN = 256
SEG0_CONV, SEG0_ATTN, SEG0_MID = 0, 4, 7


def _wgrad_in(ut, d_group, seg0, g_in, name):
    S = ut.shape[1]
    tn = WG_TN
    per_seg = D // tn
    per_shard = W_IN_SHARD // tn
    n_tiles = d_group.shape[0] * per_seg
    tile0 = seg0 * per_seg

    def body(ut_ref, d_ref, *rest):
        rest[-1][0] = _dot(ut_ref[...], d_ref[0].astype(BF)).astype(BF)

    operands, in_specs, aliases = [ut, d_group], [VMEM_SPEC, pl.BlockSpec((1, S, tn), lambda t: (t // per_seg, 0, t % per_seg))], {}
    if g_in is not None:
        operands.append(g_in)
        in_specs.append(ANY_SPEC)
        aliases = {2: 0}
    return pl.pallas_call(
        body, name=name, grid=(n_tiles,), in_specs=in_specs,
        out_specs=pl.BlockSpec((1, D, tn), lambda t: ((tile0 + t) // per_shard, 0, (tile0 + t) % per_shard)),
        out_shape=jax.ShapeDtypeStruct((N_DEV, D, W_IN_SHARD), BF),
        input_output_aliases=aliases,
        compiler_params=_params(1),
    )(*operands)


def _dgrad_norm_bwd(d_conv, d_attn, d_mid, w_all, x2, dh, norm_g):
    S = x2.shape[0]
    tm = ROW_TILE
    nsteps = S // tm
    tile = pl.BlockSpec((tm, D), lambda i: (i, 0))
    pieces = _proj_pieces()

    def body(a_ref, b_ref, c_ref, w_ref, x_ref, dh_ref, g_ref, gx_ref, small_ref):
        i = pl.program_id(0)

        @pl.when(i == 0)
        def _():
            small_ref[...] = jnp.zeros_like(small_ref)

        groups = (a_ref, b_ref, c_ref)
        du = jnp.zeros((tm, D), F32)
        for s, sc, p, pc, width in pieces:
            g = 0 if s < 4 else (1 if s < 7 else 2)
            local = s - (0, 4, 7)[g]
            du = du + _dot_nt(groups[g][local, :, sc:sc + width].astype(BF), w_ref[p, :, pc:pc + width])
        xv = x_ref[...]
        r = lax.rsqrt(jnp.mean(xv * xv, axis=-1, keepdims=True) + EPS)
        n = xv * r
        dn = du * g_ref[...]
        gx_ref[...] = dh_ref[...] + r * (dn - n * jnp.mean(dn * n, axis=-1, keepdims=True))
        small_ref[...] += _set_rows((8, D), {0: jnp.sum(du * n, axis=0, keepdims=True)})

    return pl.pallas_call(
        body, name="dgrad_norm_bwd", grid=(nsteps,),
        in_specs=[pl.BlockSpec((4, tm, D), lambda i: (0, i, 0)), pl.BlockSpec((3, tm, D), lambda i: (0, i, 0)),
                  pl.BlockSpec((3, tm, D), lambda i: (0, i, 0)), VMEM_SPEC, tile, tile,
                  pl.BlockSpec((1, D), lambda i: (0, 0))],
        out_specs=(tile, pl.BlockSpec((8, D), lambda i: (0, 0))),
        out_shape=(jax.ShapeDtypeStruct((S, D), F32), jax.ShapeDtypeStruct((8, D), F32)),
        compiler_params=_params(1),
    )(d_conv, d_attn, d_mid, w_all, x2, dh, norm_g)


HBM_SPEC = pl.BlockSpec(memory_space=pltpu.HBM)
SEM_SPEC = pl.BlockSpec(memory_space=pltpu.SEMAPHORE)
ATTN_COLS = _shard_cols((SEG0_ATTN * D, SEG0_MID * D))


def _attn_cols_exchange_start(g_in, r_in):
    def body(g_ref, r_ref, send_sems, recv_sems, g_thru, r_thru, token):
        _GradExchange(g_ref, r_ref, send_sems, recv_sems, None, ATTN_COLS).start()
        token[...] = jnp.zeros_like(token)

    hbm = pltpu.with_memory_space_constraint
    return pl.pallas_call(
        body, name="attn_cols_exchange_start",
        out_shape=(pltpu.SemaphoreType.DMA((N_DEV,)), pltpu.SemaphoreType.DMA((N_DEV,)),
                   pltpu.HBM(g_in.shape, g_in.dtype), pltpu.HBM(r_in.shape, r_in.dtype),
                   jax.ShapeDtypeStruct((8, 128), F32)),
        in_specs=(HBM_SPEC, HBM_SPEC), out_specs=(SEM_SPEC, SEM_SPEC, HBM_SPEC, HBM_SPEC, VMEM_SPEC),
        input_output_aliases={0: 2, 1: 3},
        compiler_params=pltpu.CompilerParams(has_side_effects=pltpu.SideEffectType.DATAFLOW_SIDE_EFFECTING),
    )(hbm(g_in, pltpu.HBM), hbm(r_in, pltpu.HBM))


def _attn_cols_exchange_wait(send_sems, recv_sems, g_thru, r_thru, after):
    def body(g_ref, r_ref, send_sems, recv_sems, after_ref, g_dead, r_out):
        _GradExchange(g_ref, r_ref, send_sems, recv_sems, None, ATTN_COLS).finish()

    return pl.pallas_call(
        body, name="attn_cols_exchange_wait",
        out_shape=(pltpu.HBM(g_thru.shape, g_thru.dtype), pltpu.HBM(r_thru.shape, r_thru.dtype)),
        in_specs=(HBM_SPEC, HBM_SPEC, SEM_SPEC, SEM_SPEC, ANY_SPEC), out_specs=(HBM_SPEC, HBM_SPEC),
        input_output_aliases={0: 0, 1: 1},
        compiler_params=pltpu.CompilerParams(has_side_effects=pltpu.SideEffectType.DATAFLOW_SIDE_EFFECTING),
    )(g_thru, r_thru, send_sems, recv_sems, after)


def _adamw_math(w, g, m, v):
    m = ADAM_B1 * m + (1.0 - ADAM_B1) * g
    v = ADAM_B2 * v + (1.0 - ADAM_B2) * (g * g)
    m_hat = m / (1.0 - ADAM_B1 ** ADAM_STEP)
    v_hat = v / (1.0 - ADAM_B2 ** ADAM_STEP)
    delta = -ADAM_LR * (m_hat / (jnp.sqrt(v_hat) + ADAM_EPS) + ADAM_WD * w)
    return delta, m, v


def _sum_adamw(parts, w, m, v, tm, name):
    R, C = w.shape
    tile = pl.BlockSpec((tm, C), lambda i: (i, 0))

    def body(p_ref, w_ref, m_ref, v_ref, g_out, d_out, m_out, v_out):
        g = p_ref[0].astype(F32)
        for s in range(1, N_DEV):
            g = g + p_ref[s].astype(F32)
        g_out[...] = g
        d_out[...], m_out[...], v_out[...] = _adamw_math(w_ref[...], g, m_ref[...], v_ref[...])

    shape = jax.ShapeDtypeStruct((R, C), F32)
    return pl.pallas_call(
        body, name=name, grid=(R // tm,),
        in_specs=[pl.BlockSpec((N_DEV, tm, C), lambda i: (0, i, 0)), tile, tile, tile],
        out_specs=(tile, tile, tile, tile), out_shape=(shape, shape, shape, shape),
        compiler_params=_params(1),
    )(parts, w, m, v)


def _adamw(g, w, m, v, name):
    def body(g_ref, w_ref, m_ref, v_ref, d_out, m_out, v_out):
        d_out[...], m_out[...], v_out[...] = _adamw_math(w_ref[...], g_ref[...], m_ref[...], v_ref[...])

    shape = jax.ShapeDtypeStruct(w.shape, F32)
    return pl.pallas_call(
        body, name=name, in_specs=[VMEM_SPEC] * 4, out_specs=(VMEM_SPEC,) * 3, out_shape=(shape, shape, shape),
    )(g, w, m, v)


def _alibi_slopes():
    return jnp.exp2(-8.0 * jnp.arange(1, N_HEADS + 1, dtype=F32) / N_HEADS)


def _local_step(x2, target, norm_g, b_merge, final_g, w_in, w3_shard, cw_shard):
    slopes = _alibi_slopes()
    u, ut, w_all, w3_all, cw_all = _norm_gather_first_weights(x2, norm_g, w_in, w3_shard, cw_shard)
    o, lse, q, k, v, w_all, w3_all, cw_all = _attn_fwd(u, slopes, w_all, w3_all, cw_all)
    w3 = jnp.transpose(w3_all, (1, 0, 2, 3)).reshape(3, D, D)
    cw8 = jnp.transpose(cw_all, (1, 0, 2)).reshape(8, D)
    pa = _proj_cols(u, w_all, SEG0_CONV, 4, BF, "proj_conv")
    yc_in = _conv_fwd(pa, cw8)
    pa_mid = _proj_cols(u, w_all, SEG0_MID, 3, BF, "proj_mid")
    dh, d_mid, do, dyc_in, g_3, small_mid = _mid(yc_in, pa_mid, o, x2, target, b_merge, final_g, w3)
    g_in = _wgrad_in(ut, d_mid, SEG0_MID, None, "wgrad_in_mid")
    d_conv, small_conv = _conv_bwd(dyc_in, pa, cw8)
    g_in = _wgrad_in(ut, d_conv, SEG0_CONV, g_in, "wgrad_in_conv")
    d_attn, r_in, r_3 = _attn_bwd(q, k, v, slopes, do, o, lse, g_in, g_3)
    g_in = _wgrad_in(ut, d_attn, SEG0_ATTN, g_in, "wgrad_in_attn")
    *in_flight, token = _attn_cols_exchange_start(g_in, r_in)
    grad_x, small_norm = _dgrad_norm_bwd(d_conv, d_attn, d_mid, w_all, x2, dh, norm_g + token[0:1, 0:1])
    return grad_x, in_flight, r_3, small_mid, small_conv, small_norm


def kernel(x, norm_g, w_in, b_merge, conv_w, w_out_conv, w_out_attn, w_o, final_g, loss_target, m_norm_g, m_w_in, m_b_merge, m_conv_w, m_w_out_conv, m_w_out_attn, m_w_o, m_final_g, v_norm_g, v_w_in, v_b_merge, v_conv_w, v_w_out_conv, v_w_out_attn, v_w_o, v_final_g):
    me = 4 * lax.axis_index("x") + 2 * lax.axis_index("y") + lax.axis_index("c")
    stack3 = lambda a, b, c: jnp.concatenate([a, b, c], axis=0)
    pad8 = lambda a: jnp.pad(a, ((0, 8 - a.shape[0]), (0, 0)))

    w3_shard = stack3(w_out_conv, w_out_attn, w_o)
    final_g2 = final_g.reshape(1, D)
    grad_x, in_flight, r_3, small_mid, small_conv, small_norm = _local_step(
        x[0], loss_target[0], norm_g, b_merge, final_g2, w_in[0], w3_shard, pad8(conv_w[0]))

    small = _allreduce_small(small_mid, small_conv, small_norm)
    g_in, r_in = _attn_cols_exchange_wait(*in_flight, small)
    own = lax.dynamic_index_in_dim(g_in, me, 0, keepdims=True)
    r_in = lax.dynamic_update_slice(r_in, own, (me, 0, 0))

    g_w_in, d_w_in, nm_w_in, nv_w_in = _sum_adamw(r_in, w_in[0], m_w_in[0], v_w_in[0], 256, "adamw_w_in")
    g_w3, d_w3, nm_w3, nv_w3 = _sum_adamw(
        r_3.reshape(N_DEV, 3 * ROW_SHARD, D), w3_shard.reshape(3 * ROW_SHARD, D),
        stack3(m_w_out_conv, m_w_out_attn, m_w_o).reshape(3 * ROW_SHARD, D),
        stack3(v_w_out_conv, v_w_out_attn, v_w_o).reshape(3 * ROW_SHARD, D), ROW_SHARD, "adamw_w3")

    def pack(ng, bm, fg):
        return pad8(jnp.concatenate([ng, bm.reshape(2, D), fg.reshape(1, D)], axis=0))

    d_s, nm_s, nv_s = _adamw(small, pack(norm_g, b_merge, final_g), pack(m_norm_g, m_b_merge, m_final_g),
                             pack(v_norm_g, v_b_merge, v_final_g), "adamw_small")
    g_cw = lax.dynamic_slice(small, (4, me * ROW_SHARD), (3, ROW_SHARD))
    d_cw, nm_cw, nv_cw = _adamw(g_cw, conv_w[0], m_conv_w[0], v_conv_w[0], "adamw_conv_w")

    loss = small[7, 0]
    split3 = lambda t: tuple(t[a * ROW_SHARD:(a + 1) * ROW_SHARD][None] for a in range(3))
    unpack = lambda t: (t[0:1], t[1:3].reshape(1, 2 * D), t[3])

    def leaves(in_, small_, cw_, w3_):
        ng, bm, fg = unpack(small_)
        wc, wa, wo = split3(w3_)
        return (ng, in_[None], bm, cw_[None], wc, wa, wo, fg)

    return (loss, grad_x[None],
            *leaves(g_w_in, small, g_cw, g_w3),
            *leaves(d_w_in, d_s, d_cw, d_w3),
            *leaves(nm_w_in, nm_s, nm_cw, nm_w3),
            *leaves(nv_w_in, nv_s, nv_cw, nv_w3))
```

```python
import functools

import jax
import jax.numpy as jnp
from jax import lax
from jax.experimental import pallas as pl
from jax.experimental.pallas import tpu as pltpu

D = 1024
N_HEADS = 16
HEAD_DIM = 64
N_SEG = 10
IN_COLS = N_SEG * D
N_DEV = 8
W_IN_SHARD = IN_COLS // N_DEV
ROW_SHARD = D // N_DEV
QB = 128
DILATIONS = (1, 4, 16)
EPS = 1e-6
NEG = -1e30
BF = jnp.bfloat16
F32 = jnp.float32
MESH = pl.DeviceIdType.MESH

ADAM_LR = 0.001
ADAM_B1 = 0.9
ADAM_B2 = 0.999
ADAM_EPS = 1e-08
ADAM_WD = 0.01
ADAM_STEP = 10

V7X_VMEM_BYTES = 64 * 1024 * 1024
VMEM_LIMIT = V7X_VMEM_BYTES - 8 * 1024 * 1024
ROW_TILE = 256

VMEM_SPEC = pl.BlockSpec(memory_space=pltpu.VMEM)
ANY_SPEC = pl.BlockSpec(memory_space=pl.ANY)
SMEM_SPEC = pl.BlockSpec(memory_space=pltpu.SMEM)


def _params(n_grid_axes, vmem=VMEM_LIMIT):
    return pltpu.CompilerParams(dimension_semantics=("arbitrary",) * n_grid_axes, vmem_limit_bytes=vmem)


def _dot(a, b):
    return jnp.dot(a, b, preferred_element_type=F32)


def _dot_nt(a, b):
    return lax.dot_general(a, b, (((1,), (1,)), ((), ())), preferred_element_type=F32)


def _dot_tn(a, b):
    return lax.dot_general(a, b, (((0,), (0,)), ((), ())), preferred_element_type=F32)


def _sigmoid(z):
    return 1.0 / (1.0 + jnp.exp(-z))


def _my_place():
    x, y, c = lax.axis_index("x"), lax.axis_index("y"), lax.axis_index("c")
    return x, y, c, 4 * x + 2 * y + c


def _peers(x, y, c):
    out = []
    for k in range(1, N_DEV):
        px = 1 - x if k & 4 else x
        py = 1 - y if k & 2 else y
        pc = 1 - c if k & 1 else c
        out.append(((px, py, pc), 4 * px + 2 * py + pc))
    return out


def _device(p):
    return (p >> 2, (p >> 1) & 1, p & 1)


def _shard_cols(*ranges):
    def cols(p):
        found = None
        for lo, hi in ranges:
            a, b = max(lo, p * W_IN_SHARD), min(hi, (p + 1) * W_IN_SHARD)
            if a < b:
                assert found is None
                found = (a - p * W_IN_SHARD, b - p * W_IN_SHARD)
        return found

    return cols


def _whole(p):
    return ()


def _block(ref, idx, cols):
    return ref.at[idx] if cols == () else ref.at[idx, :, cols[0]:cols[1]]


class _WeightGather:
    def __init__(self, src, dst, send_sems, forward_sems, recv_sems, cols):
        self.src, self.dst, self.cols = src, dst, cols
        self.send_sems, self.forward_sems, self.recv_sems = send_sems, forward_sems, recv_sems
        self.me = _my_place()[3]

    def _copy(self, p, target, passing_on=False):
        cols = self.cols(p)
        return pltpu.make_async_remote_copy(
            src_ref=_block(self.dst, p, cols) if passing_on else self.src(p, cols), dst_ref=_block(self.dst, p, cols),
            send_sem=self.forward_sems.at[p] if passing_on else self.send_sems.at[target],
            recv_sem=self.recv_sems.at[p], device_id=_device(target), device_id_type=MESH)

    def _as_each_device(self, own, relayed, other):
        for m in range(N_DEV):
            def branch(m=m):
                for p in range(N_DEV):
                    if self.cols(p) is None:
                        continue
                    if p == m:
                        for t in [m ^ 1] + [q for q in range(N_DEV) if q >> 1 != m >> 1 and q & 1 == m & 1]:
                            own(self._copy(m, t))
                    elif p >> 1 != m >> 1 and p & 1 == m & 1:
                        relayed(p, m ^ 1)
                    else:
                        other(p)

            pl.when(self.me == m)(branch)

    def start(self):
        self._as_each_device(lambda cp: cp.start(), lambda p, t: None, lambda p: None)

    def forward(self):
        def pass_on(p, t):
            self._copy(p, p).wait_recv()
            self._copy(p, t, passing_on=True).start()

        self._as_each_device(lambda cp: None, pass_on, lambda p: None)

    def finish(self):
        self._as_each_device(lambda cp: cp.wait_send(), lambda p, t: self._copy(p, t, passing_on=True).wait_send(),
                             lambda p: self._copy(p, p).wait_recv())


WEIGHT_GATHER_SEMS = [pltpu.SemaphoreType.DMA((N_DEV,))] * 3
FORWARD_STEP = 6
REST_COLS = _shard_cols((0, 4 * D), (7 * D, IN_COLS))
HEAD_PAIRS = D // 128


def _qkv_piece(h, seg):
    col = (4 + seg) * D + 128 * h
    return col // W_IN_SHARD, col % W_IN_SHARD


class _PieceGather:
    def __init__(self, src, dst, send_sems, recv_sems):
        self.src, self.dst, self.send_sems, self.recv_sems = src, dst, send_sems, recv_sems
        self.me = _my_place()[3]

    def _copy(self, i, target):
        p, lo = _qkv_piece(i // 3, i % 3)
        return pltpu.make_async_remote_copy(
            src_ref=self.src(p, lo, lo + 128), dst_ref=self.dst.at[p, :, lo:lo + 128], send_sem=self.send_sems.at[i, target],
            recv_sem=self.recv_sems.at[i], device_id=_device(target), device_id_type=MESH)

    def _owner(self, i, act):
        p = _qkv_piece(i // 3, i % 3)[0]

        def sender():
            for k in range(N_DEV - 1):
                act(self._copy(i, (p + 1 + (k + i) % (N_DEV - 1)) % N_DEV))

        pl.when(self.me == p)(sender)

    def start(self, pieces):
        for i in pieces:
            self._owner(i, lambda cp: cp.start())

    def wait_send(self, pieces):
        for i in pieces:
            self._owner(i, lambda cp: cp.wait_send())

    def wait_recv(self, pieces):
        for i in pieces:
            p = _qkv_piece(i // 3, i % 3)[0]
            pl.when(self.me != p)(lambda i=i, p=p: self._copy(i, p).wait_recv())


def _piece_sems(n):
    return [pltpu.SemaphoreType.DMA((n, N_DEV)), pltpu.SemaphoreType.DMA((n,))]


def _norm_gather_first_weights(x2, norm_g, w_in, w3, cw):
    S = x2.shape[0]
    tm = ROW_TILE
    nsteps = S // tm

    def body(x_ref, g_ref, w_in_ref, w3_ref, cw_ref, u_ref, ut_ref, o_in, o_3, o_cw, in_bf, w3_bf, local_sems, *sems):
        i = pl.program_id(0)
        me = _my_place()[3]
        gather = _PieceGather(lambda p, lo, hi: in_bf.at[:, lo:hi], o_in, *sems)
        local = [pltpu.make_async_copy(src, dst.at[me], local_sems.at[a])
                 for a, (src, dst) in enumerate(((in_bf, o_in), (w3_bf, o_3), (cw_ref, o_cw)))]

        @pl.when(i == 0)
        def _():
            def cast_rows(r, carry):
                rows = pl.ds(pl.multiple_of(r * 128, 128), 128)
                in_bf[rows, :] = w_in_ref[rows, :].astype(BF)
                return carry

            lax.fori_loop(0, D // 128, cast_rows, 0)
            for a in range(3):
                w3_bf[a] = w3_ref[a].astype(BF)
            gather.start(range(3))
            for cp in local:
                cp.start()

        xv = x_ref[...]
        r = lax.rsqrt(jnp.mean(xv * xv, axis=-1, keepdims=True) + EPS)
        u = xv * r * g_ref[...]
        u_ref[...] = u.astype(BF)
        ut_ref[...] = u.T.astype(BF)

        @pl.when(i == nsteps - 1)
        def _():
            gather.wait_recv(range(3))
            gather.wait_send(range(3))
            for cp in local:
                cp.wait()

    return pl.pallas_call(
        body, name="norm_gather_first_weights", grid=(nsteps,),
        out_shape=(jax.ShapeDtypeStruct((S, D), BF), jax.ShapeDtypeStruct((D, S), BF),
                   jax.ShapeDtypeStruct((N_DEV, D, W_IN_SHARD), BF),
                   jax.ShapeDtypeStruct((N_DEV, 3, ROW_SHARD, D), BF),
                   jax.ShapeDtypeStruct((N_DEV, 8, 128), F32)),
        in_specs=[pl.BlockSpec((tm, D), lambda i: (i, 0)), pl.BlockSpec((1, D), lambda i: (0, 0)),
                  VMEM_SPEC, VMEM_SPEC, VMEM_SPEC],
        out_specs=(pl.BlockSpec((tm, D), lambda i: (i, 0)), pl.BlockSpec((D, tm), lambda i: (0, i)),
                   ANY_SPEC, ANY_SPEC, ANY_SPEC),
        scratch_shapes=[pltpu.VMEM((D, W_IN_SHARD), BF), pltpu.VMEM((3, ROW_SHARD, D), BF),
                        pltpu.SemaphoreType.DMA((3,))] + _piece_sems(3),
        compiler_params=_params(1),
    )(x2, norm_g, w_in, w3, cw)


class _GradExchange:
    def __init__(self, src, dst, send_sems, recv_sems, local_sem, cols):
        self.src, self.dst, self.cols = src, dst, cols
        self.send_sems, self.recv_sems, self.local_sem = send_sems, recv_sems, local_sem
        self.me = _my_place()[3]

    def _remote(self, p, source):
        return pltpu.make_async_remote_copy(
            src_ref=_block(self.src, p, self.cols(p)), dst_ref=_block(self.dst, source, self.cols(p)),
            send_sem=self.send_sems.at[p], recv_sem=self.recv_sems.at[source],
            device_id=_device(p), device_id_type=MESH)

    def _local(self, p):
        return pltpu.make_async_copy(_block(self.src, p, self.cols(p)), _block(self.dst, p, self.cols(p)),
                                     self.local_sem)

    def _as_each_device(self, send, local, receive):
        for m in range(N_DEV):
            def branch(m=m):
                for k in range(1, N_DEV):
                    p = (m + k) % N_DEV
                    if self.cols(p) is not None:
                        send(self._remote(p, m))
                if self.cols(m) is not None:
                    if self.local_sem is not None:
                        local(self._local(m))
                    for k in range(1, N_DEV):
                        receive(self._remote(m, (m + k) % N_DEV))

            pl.when(self.me == m)(branch)

    def start(self):
        self._as_each_device(lambda cp: cp.start(), lambda cp: cp.start(), lambda cp: None)

    def finish(self):
        self._as_each_device(lambda cp: cp.wait_send(), lambda cp: cp.wait(), lambda cp: cp.wait_recv())


GRAD_EXCHANGE_SEMS = [pltpu.SemaphoreType.DMA((N_DEV,)), pltpu.SemaphoreType.DMA((N_DEV,)), pltpu.SemaphoreType.DMA]


def _allreduce_small(p_mid, p_conv, p_norm):
    def body(a_ref, b_ref, c_ref, out_ref, mine, gathered, send_sems, recv_sems):
        x, y, c, me = _my_place()
        mine[...] = a_ref[...] + b_ref[...] + c_ref[...]
        gathered[me] = mine[...]
        remote = []
        for k, (peer, _) in enumerate(_peers(x, y, c)):
            cp = pltpu.make_async_remote_copy(
                src_ref=mine, dst_ref=gathered.at[me], send_sem=send_sems.at[k], recv_sem=recv_sems.at[k],
                device_id=peer, device_id_type=MESH)
            cp.start()
            remote.append(cp)
        for cp in remote:
            cp.wait()
        total = gathered[0]
        for s in range(1, N_DEV):
            total = total + gathered[s]
        out_ref[...] = total

    return pl.pallas_call(
        body, name="allreduce_small",
        out_shape=jax.ShapeDtypeStruct((8, D), F32),
        in_specs=[VMEM_SPEC, VMEM_SPEC, VMEM_SPEC], out_specs=VMEM_SPEC,
        scratch_shapes=[pltpu.VMEM((8, D), F32), pltpu.VMEM((N_DEV, 8, D), F32),
                        pltpu.SemaphoreType.DMA((N_DEV - 1,)), pltpu.SemaphoreType.DMA((N_DEV - 1,))],
    )(p_mid, p_conv, p_norm)


def _proj_pieces():
    cuts = sorted(set(range(0, IN_COLS + 1, D)) | set(range(0, IN_COLS + 1, W_IN_SHARD)))
    return [(lo // D, lo % D, lo // W_IN_SHARD, lo % W_IN_SHARD, hi - lo) for lo, hi in zip(cuts[:-1], cuts[1:])]


PROJ_TN = 256


def _proj_cols(u, w_all, seg0, n_seg, dtype, name):
    S = u.shape[0]
    tn = PROJ_TN
    per_shard = W_IN_SHARD // tn
    tile0 = seg0 * D // tn

    def body(u_ref, w_ref, out_ref):
        out_ref[...] = _dot(u_ref[...], w_ref[0]).astype(dtype)

    return pl.pallas_call(
        body, name=name, grid=(n_seg * D // tn,),
        in_specs=[VMEM_SPEC, pl.BlockSpec((1, D, tn), lambda t: ((tile0 + t) // per_shard, 0, (tile0 + t) % per_shard))],
        out_specs=pl.BlockSpec((S, tn), lambda t: (0, t)),
        out_shape=jax.ShapeDtypeStruct((S, n_seg * D), dtype),
        compiler_params=_params(1),
    )(u, w_all)


CONV_TM, CONV_TC = 1024, 512
HALO = 16


def _conv_fwd(pa, cw8):
    S = pa.shape[0]
    tm, tc = CONV_TM, CONV_TC
    nct = D // tc

    def seg(s):
        return pl.BlockSpec((tm, tc), lambda i, j, s=s: (i, s * nct + j))

    def halo_before(s):
        return pl.BlockSpec((HALO, tc), lambda i, j, s=s: (jnp.maximum(i * (tm // HALO) - 1, 0), s * nct + j))

    def body(xc, bg, cg, zc, xch, cgh, cw, out):
        i = pl.program_id(0)
        a = cg[...].astype(F32) * xc[...].astype(F32)
        ah = cgh[...].astype(F32) * xch[...].astype(F32)
        ah = jnp.where(i > 0, ah, 0.0)
        row = lax.broadcasted_iota(jnp.int32, (tm, tc), 0)
        a1 = jnp.where(row == 0, ah[HALO - 1:HALO, :], pltpu.roll(a, 1, 0))
        a2 = jnp.where(row == 0, ah[HALO - 2:HALO - 1, :],
                       jnp.where(row == 1, ah[HALO - 1:HALO, :], pltpu.roll(a, 2, 0)))
        w = cw[...]
        conv = w[0:1, :] * a2 + w[1:2, :] * a1 + w[2:3, :] * a
        z = zc[...].astype(F32)
        out[...] = (z * _sigmoid(z) * bg[...].astype(F32) * conv).astype(BF)

    return pl.pallas_call(
        body, name="conv_fwd", grid=(S // tm, nct),
        in_specs=[seg(0), seg(1), seg(2), seg(3), halo_before(0), halo_before(2),
                  pl.BlockSpec((8, tc), lambda i, j: (0, j))],
        out_specs=pl.BlockSpec((tm, tc), lambda i, j: (i, j)),
        out_shape=jax.ShapeDtypeStruct((S, D), BF),
        compiler_params=_params(2),
    )(pa, pa, pa, pa, pa, pa, cw8)


ATT_UNROLL = 32


LAYOUT_MOD = 4
RUN = QB // LAYOUT_MOD


def _fold_masks(d):
    row = lax.broadcasted_iota(jnp.int32, (QB, QB), 0)
    lane = lax.broadcasted_iota(jnp.int32, (QB, QB), 1)
    if d == 1:
        qpos, kpos = LAYOUT_MOD * (row % RUN) + row // RUN, LAYOUT_MOD * (lane % RUN) + lane // RUN
    else:
        qpos, kpos = row, lane
    tri_le = kpos <= qpos
    dist = jnp.where(tri_le, qpos - kpos, qpos - kpos + QB).astype(F32)
    return tri_le, dist, lane < HEAD_DIM


class _Rows:
    def __init__(self, slices):
        self.slices = slices

    def get(self, ref):
        parts = [ref[sl, :] for sl in self.slices]
        return parts[0] if len(parts) == 1 else jnp.concatenate(parts, axis=0)

    def put(self, ref, val):
        size = QB // len(self.slices)
        for g, sl in enumerate(self.slices):
            ref[sl, :] = val if len(self.slices) == 1 else val[g * size:(g + 1) * size]

    def add(self, ref, val):
        self.put(ref, self.get(ref) + val)


def _block_rows(b, d, S):
    quarter = S // LAYOUT_MOD
    nb = S // (QB * d)
    r, n = b // nb, b % nb
    n_prev = jnp.maximum(n - 1, 0)
    if d == 1:
        runs = lambda m: _Rows([pl.ds(pl.multiple_of(g * quarter + RUN * m, RUN), RUN) for g in range(LAYOUT_MOD)])
        return n, runs(n), runs(n_prev)
    if d == LAYOUT_MOD:
        block = lambda m: _Rows([pl.ds(pl.multiple_of(r * quarter + QB * m, QB), QB)])
        return n, block(n), block(n_prev)
    step = d // LAYOUT_MOD
    first = (r % LAYOUT_MOD) * quarter + r // LAYOUT_MOD
    strided = lambda m: _Rows([pl.ds(first + QB * step * m, QB, stride=step)])
    return n, strided(n), strided(n_prev)


def _natural_rows(i, S):
    per = S // LAYOUT_MOD // QB
    return pl.ds(i // per + LAYOUT_MOD * QB * (i % per), QB, stride=LAYOUT_MOD)


def _head_sum_matrix():
    r = lax.broadcasted_iota(jnp.int32, (2 * QB, 2 * QB), 0)
    c = lax.broadcasted_iota(jnp.int32, (2 * QB, 2 * QB), 1)
    return (((r % QB) // HEAD_DIM) == (c // QB)).astype(F32).astype(BF)


def _hi_lo(t):
    hi = t.astype(BF)
    return jnp.concatenate([hi, (t - hi.astype(F32)).astype(BF)], axis=1)


PROJ_ROWS = 1024


def _attn_fwd(u, slopes, w_all, w3_all, cw_all):
    S = u.shape[0]
    hpr = HEAD_PAIRS
    n_blocks = S // QB
    later = range(3, 3 * hpr)

    def body(sl_ref, u_ref, w_in_ref, w3_in_ref, cw_in_ref, o_ref, lse_ref, q_ref, k_ref, v_ref, w_ref, w3_ref,
             cw_ref, acc, m_s, l_s, w_tile, staged, tile_sems, *sems):
        hp = pl.program_id(0)
        me = _my_place()[3]
        pieces = _PieceGather(lambda p, lo, hi: w_ref.at[p, :, lo:hi], w_ref, *sems[0:2])
        gathers = (_WeightGather(lambda p, cols: _block(w_ref, me, cols), w_ref, *sems[2:5], REST_COLS),
                   _WeightGather(lambda p, cols: w3_ref.at[me], w3_ref, *sems[5:8], _whole),
                   _WeightGather(lambda p, cols: cw_ref.at[me], cw_ref, *sems[8:11], _whole))

        @pl.when(hp == 0)
        def _():
            pieces.start(later)
            for g in gathers:
                g.start()

        @pl.when(hp == FORWARD_STEP)
        def _():
            for g in gathers:
                g.forward()

        for h in range(hpr):
            @pl.when(hp == h)
            def _(h=h):
                if h > 0:
                    pieces.wait_recv(range(3 * h, 3 * h + 3))
                fetch = []
                for seg in range(3):
                    p, lo = _qkv_piece(h, seg)
                    fetch.append(pltpu.make_async_copy(w_ref.at[p, :, lo:lo + 128], w_tile.at[:, seg * 128:(seg + 1) * 128],
                                                       tile_sems.at[seg]))
                    fetch[-1].start()
                for cp in fetch:
                    cp.wait()

        def project(i, carry):
            rows = pl.ds(pl.multiple_of(i * PROJ_ROWS, PROJ_ROWS), PROJ_ROWS)
            qkv = _dot(u_ref[rows, :], w_tile[...])
            per = PROJ_ROWS // LAYOUT_MOD
            for seg, ref in enumerate((q_ref, k_ref, v_ref)):
                staged[seg] = qkv[:, seg * 128:(seg + 1) * 128]
                for g in range(LAYOUT_MOD):
                    dst = pl.ds(pl.multiple_of(g * (S // LAYOUT_MOD) + i * per, per), per)
                    ref[dst, :] = staged.at[seg][pl.ds(g, per, stride=LAYOUT_MOD), :]
            return carry

        lax.fori_loop(0, S // PROJ_ROWS, project, 0)

        head_sum = _head_sum_matrix()
        ones_b = jnp.ones((2 * QB, QB), BF)
        m_s[...] = jnp.full(m_s.shape, NEG, F32)
        l_s[...] = jnp.zeros(l_s.shape, F32)
        acc[...] = jnp.zeros(acc.shape, F32)

        for d in DILATIONS:
            tri_le, dist, low = _fold_masks(d)
            low_b = low.astype(F32).astype(BF)
            high_b = 1.0 - low_b
            slope = [sl_ref[2 * hp + a] * float(d) for a in range(2)]
            bias = [slope[a] * dist for a in range(2)]

            def block(b, d=d, slope=slope, bias=bias, tri_le=tri_le, low=low, low_b=low_b, high_b=high_b):
                n, cur, prev = _block_rows(b, d, S)
                has_prev = n > 0
                valid = jnp.logical_or(tri_le, has_prev)
                q2 = (cur.get(q_ref) * 0.125).astype(BF)
                qs = jnp.concatenate([q2 * low_b, q2 * high_b], axis=0)
                vp = prev.get(v_ref)
                kp_b = prev.get(k_ref).astype(BF)
                kcat = jnp.concatenate([kp_b, cur.get(k_ref).astype(BF)], axis=0)
                vcat = jnp.concatenate([vp, cur.get(v_ref)], axis=0).astype(BF)
                s2 = _dot_nt(qs, kcat)
                e2 = _dot(_hi_lo(q2.astype(F32) * kp_b.astype(F32)), head_sum)
                p_rows, alpha_h, pe_h = [], [], []
                for a in range(2):
                    sp, sc = s2[a * QB:(a + 1) * QB, :QB], s2[a * QB:(a + 1) * QB, QB:]
                    comb = jnp.where(valid, jnp.where(tri_le, sc, sp) - bias[a], NEG)
                    e = jnp.where(has_prev, e2[:, a * QB:(a + 1) * QB] - slope[a] * float(QB), NEG)
                    m_old = cur.get(m_s.at[a])
                    m_new = jnp.maximum(jnp.maximum(m_old, jnp.max(comb, axis=-1, keepdims=True)), e)
                    cur.put(m_s.at[a], m_new)
                    p = jnp.exp(comb - m_new)
                    pe_h.append(jnp.exp(e - m_new))
                    alpha_h.append(jnp.exp(m_old - m_new))
                    p_rows.append(jnp.concatenate([jnp.where(tri_le, 0.0, p).astype(BF),
                                                   jnp.where(tri_le, p, 0.0).astype(BF)], axis=1))
                pv = _dot(jnp.concatenate(p_rows, axis=0), jnp.concatenate([vcat, ones_b], axis=1))
                for a in range(2):
                    cur.put(l_s.at[a], alpha_h[a] * cur.get(l_s.at[a]) + pv[a * QB:(a + 1) * QB, QB:] + pe_h[a])
                cur.put(acc, jnp.where(low, alpha_h[0], alpha_h[1]) * cur.get(acc)
                        + jnp.where(low, pv[:QB, :QB], pv[QB:, :QB]) + jnp.where(low, pe_h[0], pe_h[1]) * vp)

            def several(it, carry, block=block):
                for u in range(ATT_UNROLL):
                    block(it * ATT_UNROLL + u)
                return carry

            lax.fori_loop(0, n_blocks // ATT_UNROLL, several, 0)

        low = _fold_masks(LAYOUT_MOD)[2]

        def finish(i, carry):
            rows = pl.ds(pl.multiple_of(i * QB, QB), QB)
            l0, l1 = l_s[0, rows, :], l_s[1, rows, :]
            o_ref[_natural_rows(i, S), :] = acc[rows, :] / jnp.where(low, l0, l1)
            lse_ref[0, rows, :] = m_s[0, rows, :] + jnp.log(l0)
            lse_ref[1, rows, :] = m_s[1, rows, :] + jnp.log(l1)
            return carry

        lax.fori_loop(0, n_blocks, finish, 0)

        @pl.when(hp == hpr - 1)
        def _():
            pieces.wait_send(later)
            for g in gathers:
                g.finish()

    col = pl.BlockSpec((S, 128), lambda h: (0, h))
    act = jax.ShapeDtypeStruct((S, D), F32)
    gathered = (w_all, w3_all, cw_all)
    return pl.pallas_call(
        body, name="attn_fwd", grid=(hpr,),
        in_specs=[SMEM_SPEC, VMEM_SPEC, ANY_SPEC, ANY_SPEC, ANY_SPEC],
        out_specs=(col, pl.BlockSpec((2, S, 128), lambda h: (0, 0, h)), col, col, col, ANY_SPEC, ANY_SPEC, ANY_SPEC),
        out_shape=(act, jax.ShapeDtypeStruct((2, S, D), F32), act, act, act,
                   *[jax.ShapeDtypeStruct(t.shape, t.dtype) for t in gathered]),
        scratch_shapes=([pltpu.VMEM((S, 128), F32), pltpu.VMEM((2, S, 128), F32), pltpu.VMEM((2, S, 128), F32),
                         pltpu.VMEM((D, 3 * 128), BF), pltpu.VMEM((3, PROJ_ROWS, 128), F32),
                         pltpu.SemaphoreType.DMA((3,))]
                        + _piece_sems(3 * hpr) + WEIGHT_GATHER_SEMS * 3),
        input_output_aliases={2: 5, 3: 6, 4: 7},
        compiler_params=_params(1),
    )(slopes, u, *gathered)


def _set_rows(shape, rows):
    idx = lax.broadcasted_iota(jnp.int32, shape, 0)
    out = jnp.zeros(shape, F32)
    for r, val in rows.items():
        out = out + jnp.where(idx == r, val, 0.0)
    return out


def _mid(yc_in, pa_mid, o, x2, target, b_merge, final_g, w3):
    S = x2.shape[0]
    tm = ROW_TILE
    nsteps = S // tm
    tile = pl.BlockSpec((tm, D), lambda i: (i, 0))

    def body(yc_ref, za_ref, gcp_ref, gap_ref, o_ref, x_ref, t_ref, b_ref, fg_ref, w_ref,
             dh_ref, dmid_ref, do_ref, dyc_ref, gw_ref, small_ref, acc, stage, stage_sems):
        i = pl.program_id(0)

        @pl.when(i == 0)
        def _():
            acc[...] = jnp.zeros_like(acc)
            small_ref[...] = jnp.zeros_like(small_ref)

        wc, wa, wo = w_ref[0], w_ref[1], w_ref[2]
        z = za_ref[...].astype(F32)
        sg = _sigmoid(z)
        ov = o_ref[...]
        yc_in_b, ya_in_b = yc_ref[...], (z * sg * ov).astype(BF)
        yc = _dot(yc_in_b, wc)
        ya = _dot(ya_in_b, wa)
        b = b_ref[...]
        gc = _sigmoid(gcp_ref[...].astype(F32) + b[:, :D])
        ga = _sigmoid(gap_ref[...].astype(F32) + b[:, D:])
        merged = gc * yc + ga * ya
        merged_b = merged.astype(BF)
        h = x_ref[...] + _dot(merged_b, wo)
        r2 = lax.rsqrt(jnp.mean(h * h, axis=-1, keepdims=True) + EPS)
        n = h * r2
        fg = fg_ref[...]
        err = n * fg - t_ref[...]
        loss = 0.5 * jnp.sum(jnp.sum(err * err, axis=-1, keepdims=True) / D, axis=0, keepdims=True)
        dy = err / D
        g_fg = jnp.sum(dy * n, axis=0, keepdims=True)
        dn = dy * fg
        dh = r2 * (dn - n * jnp.mean(dn * n, axis=-1, keepdims=True))
        dh_ref[...] = dh
        dh_b = dh.astype(BF)
        dmerged = _dot_nt(dh_b, wo)
        acc[2] += _dot(merged.T.astype(BF), dh_b)
        dyc = (dmerged * gc).astype(BF)
        dya = (dmerged * ga).astype(BF)
        dgcp = dmerged * yc * gc * (1.0 - gc)
        dgap = dmerged * ya * ga * (1.0 - ga)
        dmid_ref[1] = dgcp.astype(BF)
        dmid_ref[2] = dgap.astype(BF)
        acc[0] += _dot(yc_in_b.astype(F32).T.astype(BF), dyc)
        acc[1] += _dot(ya_in_b.astype(F32).T.astype(BF), dya)
        dyc_ref[...] = _dot_nt(dyc, wc).astype(BF)
        dya_in = _dot_nt(dya, wa)
        do_ref[...] = dya_in * (z * sg)
        dmid_ref[0] = (dya_in * ov * (sg * (1.0 + z * (1.0 - sg)))).astype(BF)
        small_ref[...] += _set_rows((8, D), {
            1: jnp.sum(dgcp, axis=0, keepdims=True), 2: jnp.sum(dgap, axis=0, keepdims=True),
            3: g_fg, 7: jnp.broadcast_to(loss, (1, D))})

        @pl.when(i == nsteps - 1)
        def _():
            copies = []
            for n, (p, a) in enumerate((p, a) for p in range(N_DEV) for a in range(3)):
                if n >= 2:
                    copies[n - 2].wait()
                stage[n % 2] = acc[a, p * ROW_SHARD:(p + 1) * ROW_SHARD, :].astype(BF)
                copies.append(pltpu.make_async_copy(stage.at[n % 2], gw_ref.at[p, a], stage_sems.at[n % 2]))
                copies[n].start()
            for cp in copies[-2:]:
                cp.wait()

    return pl.pallas_call(
        body, name="mid", grid=(nsteps,),
        in_specs=[tile, pl.BlockSpec((tm, D), lambda i: (i, 0)), pl.BlockSpec((tm, D), lambda i: (i, 1)),
                  pl.BlockSpec((tm, D), lambda i: (i, 2)), tile, tile, tile,
                  pl.BlockSpec((1, 2 * D), lambda i: (0, 0)), pl.BlockSpec((1, D), lambda i: (0, 0)), VMEM_SPEC],
        out_specs=(tile, pl.BlockSpec((3, tm, D), lambda i: (0, i, 0)), tile, tile,
                   ANY_SPEC, pl.BlockSpec((8, D), lambda i: (0, 0))),
        out_shape=(jax.ShapeDtypeStruct((S, D), F32), jax.ShapeDtypeStruct((3, S, D), BF),
                   jax.ShapeDtypeStruct((S, D), F32), jax.ShapeDtypeStruct((S, D), BF),
                   jax.ShapeDtypeStruct((N_DEV, 3, ROW_SHARD, D), BF), jax.ShapeDtypeStruct((8, D), F32)),
        scratch_shapes=[pltpu.VMEM((3, D, D), F32), pltpu.VMEM((2, ROW_SHARD, D), BF), pltpu.SemaphoreType.DMA((2,))],
        compiler_params=_params(1),
    )(yc_in, pa_mid, pa_mid, pa_mid, o, x2, target, b_merge, final_g, w3)


def _conv_bwd(dyc_in, pa, cw8):
    S = pa.shape[0]
    tm, tc = CONV_TM, CONV_TC
    nct = D // tc
    nrt = S // tm
    last_halo = S // HALO - 1

    def seg(s):
        return pl.BlockSpec((tm, tc), lambda j, i, s=s: (i, s * nct + j))

    def halo_before(s):
        return pl.BlockSpec((HALO, tc), lambda j, i, s=s: (jnp.maximum(i * (tm // HALO) - 1, 0), s * nct + j))

    def halo_after(s):
        return pl.BlockSpec((HALO, tc), lambda j, i, s=s: (jnp.minimum((i + 1) * (tm // HALO), last_halo), s * nct + j))

    def body(dy, xc, bg, cg, zc, xch, cgh, dyn, bgn, zcn, cw, dout, gcw):
        i = pl.program_id(1)

        @pl.when(i == 0)
        def _():
            gcw[...] = jnp.zeros_like(gcw)

        xcv, cgv = xc[...].astype(F32), cg[...].astype(F32)
        a = cgv * xcv
        ah = jnp.where(i > 0, cgh[...].astype(F32) * xch[...].astype(F32), 0.0)
        row = lax.broadcasted_iota(jnp.int32, (tm, tc), 0)
        a1 = jnp.where(row == 0, ah[HALO - 1:HALO, :], pltpu.roll(a, 1, 0))
        a2 = jnp.where(row == 0, ah[HALO - 2:HALO - 1, :],
                       jnp.where(row == 1, ah[HALO - 1:HALO, :], pltpu.roll(a, 2, 0)))
        w = cw[...]
        conv = w[0:1, :] * a2 + w[1:2, :] * a1 + w[2:3, :] * a
        z = zc[...].astype(F32)
        sg = _sigmoid(z)
        silu = z * sg
        bgv = bg[...].astype(F32)
        dyv = dy[...].astype(F32)
        dout[3] = (dyv * bgv * conv * (sg * (1.0 + z * (1.0 - sg)))).astype(BF)
        dout[1] = (dyv * silu * conv).astype(BF)
        dc = dyv * silu * bgv
        zn = zcn[...].astype(F32)
        dcn = dyn[...].astype(F32) * (zn * _sigmoid(zn)) * bgn[...].astype(F32)
        dcn = jnp.where(i < nrt - 1, dcn, 0.0)
        dc1 = jnp.where(row == tm - 1, dcn[0:1, :], pltpu.roll(dc, tm - 1, 0))
        dc2 = jnp.where(row == tm - 1, dcn[1:2, :],
                        jnp.where(row == tm - 2, dcn[0:1, :], pltpu.roll(dc, tm - 2, 0)))
        da = w[2:3, :] * dc + w[1:2, :] * dc1 + w[0:1, :] * dc2
        dout[2] = (da * xcv).astype(BF)
        dout[0] = (da * cgv).astype(BF)
        gcw[...] += _set_rows((8, tc), {
            4: jnp.sum(dc * a2, axis=0, keepdims=True), 5: jnp.sum(dc * a1, axis=0, keepdims=True),
            6: jnp.sum(dc * a, axis=0, keepdims=True)})

    return pl.pallas_call(
        body, name="conv_bwd", grid=(nct, nrt),
        in_specs=[pl.BlockSpec((tm, tc), lambda j, i: (i, j)), seg(0), seg(1), seg(2), seg(3),
                  halo_before(0), halo_before(2),
                  pl.BlockSpec((HALO, tc), lambda j, i: (jnp.minimum((i + 1) * (tm // HALO), last_halo), j)),
                  halo_after(1), halo_after(3), pl.BlockSpec((8, tc), lambda j, i: (0, j))],
        out_specs=(pl.BlockSpec((4, tm, tc), lambda j, i: (0, i, j)), pl.BlockSpec((8, tc), lambda j, i: (0, j))),
        out_shape=(jax.ShapeDtypeStruct((4, S, D), BF), jax.ShapeDtypeStruct((8, D), F32)),
        compiler_params=_params(2),
    )(dyc_in, pa, pa, pa, pa, pa, pa, dyc_in, pa, pa, cw8)


def _attn_bwd(q, k, v, slopes, do, o, lse, g_in, g_3):
    S = q.shape[0]
    hpr = HEAD_PAIRS
    n_blocks = S // QB

    def body(sl_ref, q_ref, k_ref, v_ref, do_ref, o_ref, lse_ref, gin_ref, g3_ref, out_ref, rin_ref, r3_ref,
             dq_s, dk_s, dv_s, do_s, dd_s, *sems):
        hp = pl.program_id(0)
        exchanges = (_GradExchange(gin_ref, rin_ref, *sems[:3], _shard_cols((0, SEG0_ATTN * D), (SEG0_MID * D, IN_COLS))),
                     _GradExchange(g3_ref, r3_ref, *sems[3:], _whole))

        @pl.when(hp == 0)
        def _():
            for ex in exchanges:
                ex.start()

        head_sum = _head_sum_matrix()
        dq_s[...] = jnp.zeros(dq_s.shape, F32)
        dk_s[...] = jnp.zeros(dk_s.shape, F32)
        dv_s[...] = jnp.zeros(dv_s.shape, F32)

        def row_dots(i, carry):
            rows = pl.ds(pl.multiple_of(i * QB, QB), QB)
            natural = _natural_rows(i, S)
            do_c = do_ref[natural, :]
            do_s[rows, :] = do_c
            dd = _dot(_hi_lo(do_c * o_ref[natural, :]), head_sum)
            dd_s[0, rows, :] = dd[:, :QB]
            dd_s[1, rows, :] = dd[:, QB:]
            return carry

        lax.fori_loop(0, n_blocks, row_dots, 0)

        for d in DILATIONS:
            tri_le, dist, low = _fold_masks(d)
            low_b = low.astype(F32).astype(BF)
            high_b = 1.0 - low_b
            slope = [sl_ref[2 * hp + a] * float(d) for a in range(2)]
            bias = [slope[a] * dist for a in range(2)]

            def block(b, d=d, slope=slope, bias=bias, tri_le=tri_le, low=low, low_b=low_b, high_b=high_b):
                n, cur, prev = _block_rows(b, d, S)
                has_prev = n > 0
                valid = jnp.logical_or(tri_le, has_prev)
                q2f = cur.get(q_ref) * 0.125
                q2 = q2f.astype(BF)
                qs = jnp.concatenate([q2 * low_b, q2 * high_b], axis=0)
                kp, vp = prev.get(k_ref), prev.get(v_ref)
                kp_b, vp_b = kp.astype(BF), vp.astype(BF)
                kcat = jnp.concatenate([kp_b, cur.get(k_ref).astype(BF)], axis=0)
                vcat = jnp.concatenate([vp_b, cur.get(v_ref).astype(BF)], axis=0)
                do2f = cur.get(do_s)
                do2 = do2f.astype(BF)
                dos = jnp.concatenate([do2 * low_b, do2 * high_b], axis=0)
                s2 = _dot_nt(qs, kcat)
                dp2 = _dot_nt(dos, vcat)
                diag2 = _dot(jnp.concatenate([_hi_lo(q2.astype(F32) * kp_b.astype(F32)),
                                              _hi_lo(do2.astype(F32) * vp_b.astype(F32))], axis=0), head_sum)
                p_rows, ds_rows, pe_h, dse_h = [], [], [], []
                for a in range(2):
                    hs = slice(a * QB, (a + 1) * QB)
                    sp, sc = s2[hs, :QB], s2[hs, QB:]
                    dpp, dpc = dp2[hs, :QB], dp2[hs, QB:]
                    lse_a, dd_a = cur.get(lse_ref.at[a]), cur.get(dd_s.at[a])
                    comb = jnp.where(tri_le, sc, sp) - bias[a]
                    e = diag2[:QB, hs] - slope[a] * float(QB)
                    p = jnp.where(valid, jnp.exp(comb - lse_a), 0.0)
                    pe = jnp.where(has_prev, jnp.exp(e - lse_a), 0.0)
                    ds = p * (jnp.where(tri_le, dpc, dpp) - dd_a)
                    dse_h.append(pe * (diag2[QB:, hs] - dd_a))
                    pe_h.append(pe)
                    p_rows.append(jnp.concatenate([jnp.where(tri_le, 0.0, p).astype(BF),
                                                   jnp.where(tri_le, p, 0.0).astype(BF)], axis=1))
                    ds_rows.append(jnp.concatenate([jnp.where(tri_le, 0.0, ds).astype(BF),
                                                    jnp.where(tri_le, ds, 0.0).astype(BF)], axis=1))
                pst = jnp.concatenate(p_rows, axis=0)
                dst = jnp.concatenate(ds_rows, axis=0)
                pe2 = jnp.where(low, pe_h[0], pe_h[1])
                dse2 = jnp.where(low, dse_h[0], dse_h[1])
                dq = _dot(dst, kcat)
                cur.add(dq_s, (jnp.where(low, dq[:QB], dq[QB:]) + dse2 * kp) * 0.125)
                dk = _dot_tn(dst, qs)
                dv = _dot_tn(pst, dos)
                prev.add(dk_s, dk[:QB] + dse2 * q2f)
                cur.add(dk_s, dk[QB:])
                prev.add(dv_s, dv[:QB] + pe2 * do2f)
                cur.add(dv_s, dv[QB:])

            def several(it, carry, block=block):
                for u in range(ATT_UNROLL):
                    block(it * ATT_UNROLL + u)
                return carry

            lax.fori_loop(0, n_blocks // ATT_UNROLL, several, 0)

        def finish(i, carry):
            rows = pl.ds(pl.multiple_of(i * QB, QB), QB)
            natural = _natural_rows(i, S)
            for t, ref in enumerate((dq_s, dk_s, dv_s)):
                out_ref.at[t][natural, :] = ref[rows, :]
            return carry

        lax.fori_loop(0, n_blocks, finish, 0)

        @pl.when(hp == hpr - 1)
        def _():
            for ex in exchanges:
                ex.finish()

    col = pl.BlockSpec((S, 128), lambda h: (0, h))
    return pl.pallas_call(
        body, name="attn_bwd", grid=(hpr,),
        in_specs=[SMEM_SPEC, col, col, col, col, col, pl.BlockSpec((2, S, 128), lambda h: (0, 0, h)),
                  ANY_SPEC, ANY_SPEC],
        out_specs=(pl.BlockSpec((3, S, 128), lambda h: (0, 0, h)), ANY_SPEC, ANY_SPEC),
        out_shape=(jax.ShapeDtypeStruct((3, S, D), F32), jax.ShapeDtypeStruct(g_in.shape, BF),
                   jax.ShapeDtypeStruct(g_3.shape, BF)),
        scratch_shapes=([pltpu.VMEM((S, 128), F32)] * 4 + [pltpu.VMEM((2, S, 128), F32)]
                        + GRAD_EXCHANGE_SEMS + GRAD_EXCHANGE_SEMS),
        compiler_params=_params(1),
    )(slopes, q, k, v, do, o, lse, g_in, g_3)


WG_TN = 256
SEG0_CONV, SEG0_ATTN, SEG0_MID = 0, 4, 7


def _wgrad_in(ut, d_group, seg0, g_in, name):
    S = ut.shape[1]
    tn = WG_TN
    per_seg = D // tn
    per_shard = W_IN_SHARD // tn
    n_tiles = d_group.shape[0] * per_seg
    tile0 = seg0 * per_seg

    def body(ut_ref, d_ref, *rest):
        rest[-1][0] = _dot(ut_ref[...], d_ref[0].astype(BF)).astype(BF)

    operands, in_specs, aliases = [ut, d_group], [VMEM_SPEC, pl.BlockSpec((1, S, tn), lambda t: (t // per_seg, 0, t % per_seg))], {}
    if g_in is not None:
        operands.append(g_in)
        in_specs.append(ANY_SPEC)
        aliases = {2: 0}
    return pl.pallas_call(
        body, name=name, grid=(n_tiles,), in_specs=in_specs,
        out_specs=pl.BlockSpec((1, D, tn), lambda t: ((tile0 + t) // per_shard, 0, (tile0 + t) % per_shard)),
        out_shape=jax.ShapeDtypeStruct((N_DEV, D, W_IN_SHARD), BF),
        input_output_aliases=aliases,
        compiler_params=_params(1),
    )(*operands)


def _dgrad_norm_bwd(d_conv, d_attn, d_mid, w_all, x2, dh, norm_g):
    S = x2.shape[0]
    tm = ROW_TILE
    nsteps = S // tm
    tile = pl.BlockSpec((tm, D), lambda i: (i, 0))
    pieces = _proj_pieces()

    def body(a_ref, b_ref, c_ref, w_ref, x_ref, dh_ref, g_ref, gx_ref, small_ref):
        i = pl.program_id(0)

        @pl.when(i == 0)
        def _():
            small_ref[...] = jnp.zeros_like(small_ref)

        groups = (a_ref, b_ref, c_ref)
        du = jnp.zeros((tm, D), F32)
        for s, sc, p, pc, width in pieces:
            g = 0 if s < 4 else (1 if s < 7 else 2)
            local = s - (0, 4, 7)[g]
            du = du + _dot_nt(groups[g][local, :, sc:sc + width].astype(BF), w_ref[p, :, pc:pc + width])
        xv = x_ref[...]
        r = lax.rsqrt(jnp.mean(xv * xv, axis=-1, keepdims=True) + EPS)
        n = xv * r
        dn = du * g_ref[...]
        gx_ref[...] = dh_ref[...] + r * (dn - n * jnp.mean(dn * n, axis=-1, keepdims=True))
        small_ref[...] += _set_rows((8, D), {0: jnp.sum(du * n, axis=0, keepdims=True)})

    return pl.pallas_call(
        body, name="dgrad_norm_bwd", grid=(nsteps,),
        in_specs=[pl.BlockSpec((4, tm, D), lambda i: (0, i, 0)), pl.BlockSpec((3, tm, D), lambda i: (0, i, 0)),
                  pl.BlockSpec((3, tm, D), lambda i: (0, i, 0)), VMEM_SPEC, tile, tile,
                  pl.BlockSpec((1, D), lambda i: (0, 0))],
        out_specs=(tile, pl.BlockSpec((8, D), lambda i: (0, 0))),
        out_shape=(jax.ShapeDtypeStruct((S, D), F32), jax.ShapeDtypeStruct((8, D), F32)),
        compiler_params=_params(1),
    )(d_conv, d_attn, d_mid, w_all, x2, dh, norm_g)


HBM_SPEC = pl.BlockSpec(memory_space=pltpu.HBM)
SEM_SPEC = pl.BlockSpec(memory_space=pltpu.SEMAPHORE)
ATTN_COLS = _shard_cols((SEG0_ATTN * D, SEG0_MID * D))


def _attn_cols_exchange_start(g_in, r_in):
    def body(g_ref, r_ref, send_sems, recv_sems, g_thru, r_thru, token):
        _GradExchange(g_ref, r_ref, send_sems, recv_sems, None, ATTN_COLS).start()
        token[...] = jnp.zeros_like(token)

    hbm = pltpu.with_memory_space_constraint
    return pl.pallas_call(
        body, name="attn_cols_exchange_start",
        out_shape=(pltpu.SemaphoreType.DMA((N_DEV,)), pltpu.SemaphoreType.DMA((N_DEV,)),
                   pltpu.HBM(g_in.shape, g_in.dtype), pltpu.HBM(r_in.shape, r_in.dtype),
                   jax.ShapeDtypeStruct((8, 128), F32)),
        in_specs=(HBM_SPEC, HBM_SPEC), out_specs=(SEM_SPEC, SEM_SPEC, HBM_SPEC, HBM_SPEC, VMEM_SPEC),
        input_output_aliases={0: 2, 1: 3},
        compiler_params=pltpu.CompilerParams(has_side_effects=pltpu.SideEffectType.DATAFLOW_SIDE_EFFECTING),
    )(hbm(g_in, pltpu.HBM), hbm(r_in, pltpu.HBM))


def _attn_cols_exchange_wait(send_sems, recv_sems, g_thru, r_thru, after):
    def body(g_ref, r_ref, send_sems, recv_sems, after_ref, g_dead, r_out):
        _GradExchange(g_ref, r_ref, send_sems, recv_sems, None, ATTN_COLS).finish()

    return pl.pallas_call(
        body, name="attn_cols_exchange_wait",
        out_shape=(pltpu.HBM(g_thru.shape, g_thru.dtype), pltpu.HBM(r_thru.shape, r_thru.dtype)),
        in_specs=(HBM_SPEC, HBM_SPEC, SEM_SPEC, SEM_SPEC, ANY_SPEC), out_specs=(HBM_SPEC, HBM_SPEC),
        input_output_aliases={0: 0, 1: 1},
        compiler_params=pltpu.CompilerParams(has_side_effects=pltpu.SideEffectType.DATAFLOW_SIDE_EFFECTING),
    )(g_thru, r_thru, send_sems, recv_sems, after)


def _adamw_math(w, g, m, v):
    m = ADAM_B1 * m + (1.0 - ADAM_B1) * g
    v = ADAM_B2 * v + (1.0 - ADAM_B2) * (g * g)
    m_hat = m / (1.0 - ADAM_B1 ** ADAM_STEP)
    v_hat = v / (1.0 - ADAM_B2 ** ADAM_STEP)
    delta = -ADAM_LR * (m_hat / (jnp.sqrt(v_hat) + ADAM_EPS) + ADAM_WD * w)
    return delta, m, v


def _sum_adamw(parts, w, m, v, tm, name):
    R, C = w.shape
    tile = pl.BlockSpec((tm, C), lambda i: (i, 0))

    def body(p_ref, w_ref, m_ref, v_ref, g_out, d_out, m_out, v_out):
        g = p_ref[0].astype(F32)
        for s in range(1, N_DEV):
            g = g + p_ref[s].astype(F32)
        g_out[...] = g
        d_out[...], m_out[...], v_out[...] = _adamw_math(w_ref[...], g, m_ref[...], v_ref[...])

    shape = jax.ShapeDtypeStruct((R, C), F32)
    return pl.pallas_call(
        body, name=name, grid=(R // tm,),
        in_specs=[pl.BlockSpec((N_DEV, tm, C), lambda i: (0, i, 0)), tile, tile, tile],
        out_specs=(tile, tile, tile, tile), out_shape=(shape, shape, shape, shape),
        compiler_params=_params(1),
    )(parts, w, m, v)


def _adamw(g, w, m, v, name):
    def body(g_ref, w_ref, m_ref, v_ref, d_out, m_out, v_out):
        d_out[...], m_out[...], v_out[...] = _adamw_math(w_ref[...], g_ref[...], m_ref[...], v_ref[...])

    shape = jax.ShapeDtypeStruct(w.shape, F32)
    return pl.pallas_call(
        body, name=name, in_specs=[VMEM_SPEC] * 4, out_specs=(VMEM_SPEC,) * 3, out_shape=(shape, shape, shape),
    )(g, w, m, v)


def _alibi_slopes():
    return jnp.exp2(-8.0 * jnp.arange(1, N_HEADS + 1, dtype=F32) / N_HEADS)


def _local_step(x2, target, norm_g, b_merge, final_g, w_in, w3_shard, cw_shard):
    slopes = _alibi_slopes()
    u, ut, w_all, w3_all, cw_all = _norm_gather_first_weights(x2, norm_g, w_in, w3_shard, cw_shard)
    o, lse, q, k, v, w_all, w3_all, cw_all = _attn_fwd(u, slopes, w_all, w3_all, cw_all)
    w3 = jnp.transpose(w3_all, (1, 0, 2, 3)).reshape(3, D, D)
    cw8 = jnp.transpose(cw_all, (1, 0, 2)).reshape(8, D)
    pa = _proj_cols(u, w_all, SEG0_CONV, 4, BF, "proj_conv")
    yc_in = _conv_fwd(pa, cw8)
    pa_mid = _proj_cols(u, w_all, SEG0_MID, 3, BF, "proj_mid")
    dh, d_mid, do, dyc_in, g_3, small_mid = _mid(yc_in, pa_mid, o, x2, target, b_merge, final_g, w3)
    g_in = _wgrad_in(ut, d_mid, SEG0_MID, None, "wgrad_in_mid")
    d_conv, small_conv = _conv_bwd(dyc_in, pa, cw8)
    g_in = _wgrad_in(ut, d_conv, SEG0_CONV, g_in, "wgrad_in_conv")
    d_attn, r_in, r_3 = _attn_bwd(q, k, v, slopes, do, o, lse, g_in, g_3)
    g_in = _wgrad_in(ut, d_attn, SEG0_ATTN, g_in, "wgrad_in_attn")
    *in_flight, token = _attn_cols_exchange_start(g_in, r_in)
    grad_x, small_norm = _dgrad_norm_bwd(d_conv, d_attn, d_mid, w_all, x2, dh, norm_g + token[0:1, 0:1])
    return grad_x, in_flight, r_3, small_mid, small_conv, small_norm


def kernel(x, norm_g, w_in, b_merge, conv_w, w_out_conv, w_out_attn, w_o, final_g, loss_target, m_norm_g, m_w_in, m_b_merge, m_conv_w, m_w_out_conv, m_w_out_attn, m_w_o, m_final_g, v_norm_g, v_w_in, v_b_merge, v_conv_w, v_w_out_conv, v_w_out_attn, v_w_o, v_final_g):
    me = 4 * lax.axis_index("x") + 2 * lax.axis_index("y") + lax.axis_index("c")
    stack3 = lambda a, b, c: jnp.concatenate([a, b, c], axis=0)
    pad8 = lambda a: jnp.pad(a, ((0, 8 - a.shape[0]), (0, 0)))

    w3_shard = stack3(w_out_conv, w_out_attn, w_o)
    final_g2 = final_g.reshape(1, D)
    grad_x, in_flight, r_3, small_mid, small_conv, small_norm = _local_step(
        x[0], loss_target[0], norm_g, b_merge, final_g2, w_in[0], w3_shard, pad8(conv_w[0]))

    small = _allreduce_small(small_mid, small_conv, small_norm)
    g_in, r_in = _attn_cols_exchange_wait(*in_flight, small)
    own = lax.dynamic_index_in_dim(g_in, me, 0, keepdims=True)
    r_in = lax.dynamic_update_slice(r_in, own, (me, 0, 0))

    g_w_in, d_w_in, nm_w_in, nv_w_in = _sum_adamw(r_in, w_in[0], m_w_in[0], v_w_in[0], 256, "adamw_w_in")
    g_w3, d_w3, nm_w3, nv_w3 = _sum_adamw(
        r_3.reshape(N_DEV, 3 * ROW_SHARD, D), w3_shard.reshape(3 * ROW_SHARD, D),
        stack3(m_w_out_conv, m_w_out_attn, m_w_o).reshape(3 * ROW_SHARD, D),
        stack3(v_w_out_conv, v_w_out_attn, v_w_o).reshape(3 * ROW_SHARD, D), ROW_SHARD, "adamw_w3")

    def pack(ng, bm, fg):
        return pad8(jnp.concatenate([ng, bm.reshape(2, D), fg.reshape(1, D)], axis=0))

    d_s, nm_s, nv_s = _adamw(small, pack(norm_g, b_merge, final_g), pack(m_norm_g, m_b_merge, m_final_g),
                             pack(v_norm_g, v_b_merge, v_final_g), "adamw_small")
    g_cw = lax.dynamic_slice(small, (4, me * ROW_SHARD), (3, ROW_SHARD))
    d_cw, nm_cw, nv_cw = _adamw(g_cw, conv_w[0], m_conv_w[0], v_conv_w[0], "adamw_conv_w")

    loss = small[7, 0]
    split3 = lambda t: tuple(t[a * ROW_SHARD:(a + 1) * ROW_SHARD][None] for a in range(3))
    unpack = lambda t: (t[0:1], t[1:3].reshape(1, 2 * D), t[3])

    def leaves(in_, small_, cw_, w3_):
        ng, bm, fg = unpack(small_)
        wc, wa, wo = split3(w3_)
        return (ng, in_[None], bm, cw_[None], wc, wa, wo, fg)

    return (loss, grad_x[None],
            *leaves(g_w_in, small, g_cw, g_w3),
            *leaves(d_w_in, d_s, d_cw, d_w3),
            *leaves(nm_w_in, nm_s, nm_cw, nm_w3),
            *leaves(nv_w_in, nv_s, nv_cw, nv_w3))
```
